```python
import math
import jax, jax.numpy as jnp
from jax import lax
import numpy as np


D_MODEL = 1024
BATCH = 32
SEQ = 2048
DEPTH = 4

D_MIX = D_MODEL
DN_HEADS = 4
DN_HEAD_DIM = 128
DN_WIDTH = DN_HEADS * DN_HEAD_DIM
LRU_WIDTH = D_MIX - DN_WIDTH
LRU_BLOCKS = 8
LRU_BLOCK = LRU_WIDTH // LRU_BLOCKS
LRU_C = 8.0
SHORT_CONV = 4
SHORT_CONV_LEFT = 2
FFN_CONV = 3
FFN_CONV_LEFT = 1
D_FF = 2816
PLE_DIM = 256
CHUNK = 64
EPS = 1e-6

Q_OFF = 0
K_OFF = DN_WIDTH
V_OFF = 2 * DN_WIDTH
Z_OFF = 3 * DN_WIDTH
BETA_OFF = 4 * DN_WIDTH
ALPHA_OFF = BETA_OFF + 2 * DN_HEADS
LX_OFF = ALPHA_OFF + 2 * DN_HEADS
LG_OFF = LX_OFF + LRU_WIDTH
IN_COLS = LG_OFF + LRU_WIDTH

kernel_name = 'hymba_gdn_rglru_convglu_ple_encoder'


def rmsnorm(x, g):
    x32 = x.astype(jnp.float32)
    y = x32 * lax.rsqrt(jnp.mean(x32 * x32, axis=-1, keepdims=True) + EPS)
    return (y * g.astype(jnp.float32)).astype(x.dtype)


def l2norm(x):
    x32 = x.astype(jnp.float32)
    return (x32 * lax.rsqrt(jnp.sum(x32 * x32, axis=-1, keepdims=True) + EPS)).astype(x.dtype)


def dwconv(x, w, left):
    k = w.shape[0]
    s = x.shape[1]
    xp = jnp.pad(x, ((0, 0), (left, k - 1 - left), (0, 0)))
    out = xp[:, 0:s] * w[0]
    for j in range(1, k):
        out = out + xp[:, j:j + s] * w[j]
    return out


def flip(t):
    return jnp.flip(t, axis=1)


def gated_delta_chunked(q, k, v, g, beta):
    B, S, H, Dk = q.shape
    Dv = v.shape[-1]
    N = S // CHUNK
    f32 = jnp.float32

    def chunks(t):
        t = t.astype(f32).reshape((B, N, CHUNK) + t.shape[2:])
        return jnp.moveaxis(t, 3, 1)

    qc = chunks(q) * (Dk ** -0.5)
    kc, vc, gc, bc = chunks(k), chunks(v), chunks(g), chunks(beta)
    gcum = jnp.cumsum(gc, axis=-1)
    idx = jnp.arange(CHUNK)
    incl = idx[:, None] >= idx[None, :]
    strict = idx[:, None] > idx[None, :]
    decay = jnp.exp(jnp.where(incl, gcum[..., :, None] - gcum[..., None, :], -jnp.inf))
    kb = kc * bc[..., None]
    a_mat = jnp.where(strict, jnp.einsum('bhnid,bhnjd->bhnij', kb, kc) * decay, 0.0)
    rhs = jnp.concatenate([vc * bc[..., None], kb * jnp.exp(gcum)[..., None]], axis=-1)
    sol = lax.linalg.triangular_solve(a_mat, rhs, left_side=True, lower=True, unit_diagonal=True)
    u, w = sol[..., :Dv], sol[..., Dv:]
    attn = jnp.einsum('bhnid,bhnjd->bhnij', qc, kc) * decay
    glast = gcum[..., -1:]
    q_dec = qc * jnp.exp(gcum)[..., None]
    k_dec = kc * jnp.exp(glast - gcum)[..., None]
    cdec = jnp.exp(glast[..., 0])
    xs = tuple(jnp.moveaxis(t, 2, 0) for t in (q_dec, k_dec, w, u, attn, cdec))

    def step(state, inp):
        qd, kd, wi, ui, ai, cd = inp
        v_new = ui - jnp.einsum('bhcd,bhde->bhce', wi, state)
        o = jnp.einsum('bhcd,bhde->bhce', qd, state) + jnp.einsum('bhij,bhje->bhie', ai, v_new)
        state = state * cd[..., None, None] + jnp.einsum('bhcd,bhce->bhde', kd, v_new)
        return state, o

    state0 = jnp.zeros((B, H, Dk, Dv), f32)
    _, o = lax.scan(step, state0, xs)
    o = jnp.transpose(o, (1, 0, 3, 2, 4)).reshape(B, S, H, Dv)
    return o.astype(v.dtype)


def rglru(x, wa, ba, wx, bx, lam):
    B, S, W = x.shape
    xr = x.reshape(B, S, LRU_BLOCKS, LRU_BLOCK)
    r = jax.nn.sigmoid(jnp.einsum('bsnc,ncd->bsnd', xr, wa).reshape(B, S, W) + ba)
    ig = jax.nn.sigmoid(jnp.einsum('bsnc,ncd->bsnd', xr, wx).reshape(B, S, W) + bx)
    log_a = -LRU_C * r.astype(jnp.float32) * jax.nn.softplus(-lam.astype(jnp.float32))
    a = jnp.exp(log_a)
    b = jnp.sqrt(-jnp.expm1(2.0 * log_a)) * (ig * x).astype(jnp.float32)

    def combine(e1, e2):
        a1, b1 = e1
        a2, b2 = e2
        return a1 * a2, a2 * b1 + b2

    _, h = lax.associative_scan(combine, (a, b), axis=1)
    return h.astype(x.dtype)


def _fwd_setup_inputs(seed: int = 0) -> dict:
    key = jax.random.key(seed)
    ks = iter(jax.random.split(key, 32))
    f32 = jnp.float32

    def nrm(shape, scale):
        return scale * jax.random.normal(next(ks), shape, f32)

    def gain(shape):
        return 1.0 + nrm(shape, 0.02)

    L, H = DEPTH, DN_HEADS
    x = nrm((BATCH, SEQ, D_MODEL), 1.0)
    p = nrm((DEPTH, BATCH, SEQ, PLE_DIM), 1.0)
    norm1_g = gain((L, D_MODEL))
    w_in = nrm((L, D_MODEL, IN_COLS), D_MODEL ** -0.5)
    dn_conv_w = nrm((L, SHORT_CONV, 3 * DN_WIDTH), SHORT_CONV ** -0.5)
    dn_a_log = jnp.log(jax.random.uniform(next(ks), (L, 2, H), f32, 1.0, 16.0))
    dt = jnp.exp(jax.random.uniform(next(ks), (L, 2, H), f32, math.log(1e-3), math.log(1e-1)))
    dn_dt_bias = dt + jnp.log(-jnp.expm1(-dt))
    dn_norm_g = gain((L, DN_HEAD_DIM))
    lru_conv_w = nrm((L, SHORT_CONV, LRU_WIDTH), SHORT_CONV ** -0.5)
    lru_conv_b = nrm((L, LRU_WIDTH), 0.02)
    lru_wa = nrm((L, 2, LRU_BLOCKS, LRU_BLOCK, LRU_BLOCK), LRU_BLOCK ** -0.5)
    lru_ba = nrm((L, 2, LRU_WIDTH), 0.02)
    lru_wx = nrm((L, 2, LRU_BLOCKS, LRU_BLOCK, LRU_BLOCK), LRU_BLOCK ** -0.5)
    lru_bx = nrm((L, 2, LRU_WIDTH), 0.02)
    a0 = jax.random.uniform(next(ks), (L, 2, LRU_WIDTH), f32, 0.9, 0.999) ** (1.0 / LRU_C)
    lru_lambda = jnp.log(a0) - jnp.log1p(-a0)
    lru_norm_g = gain((L, LRU_WIDTH))
    w_out = nrm((L, D_MIX, D_MODEL), D_MIX ** -0.5)
    norm2_g = gain((L, D_MODEL))
    ffn_wg = nrm((L, D_MODEL, D_FF), D_MODEL ** -0.5)
    ffn_wu = nrm((L, D_MODEL, D_FF), D_MODEL ** -0.5)
    ffn_conv_w = nrm((L, FFN_CONV, D_FF), FFN_CONV ** -0.5)
    ffn_conv_b = nrm((L, D_FF), 0.02)
    ffn_wd = nrm((L, D_FF, D_MODEL), D_FF ** -0.5)
    ple_norm_g = gain((L, D_MODEL))
    ple_wg = nrm((L, D_MODEL, D_MODEL), D_MODEL ** -0.5)
    ple_bg = nrm((L, D_MODEL), 0.02)
    ple_wp = nrm((L, PLE_DIM, D_MODEL), PLE_DIM ** -0.5)
    final_g = gain((D_MODEL,))
    return {'x': x, 'p': p, 'norm1_g': norm1_g, 'w_in': w_in, 'dn_conv_w': dn_conv_w,
            'dn_a_log': dn_a_log, 'dn_dt_bias': dn_dt_bias, 'dn_norm_g': dn_norm_g,
            'lru_conv_w': lru_conv_w, 'lru_conv_b': lru_conv_b, 'lru_wa': lru_wa, 'lru_ba': lru_ba,
            'lru_wx': lru_wx, 'lru_bx': lru_bx, 'lru_lambda': lru_lambda, 'lru_norm_g': lru_norm_g,
            'w_out': w_out, 'norm2_g': norm2_g, 'ffn_wg': ffn_wg, 'ffn_wu': ffn_wu,
            'ffn_conv_w': ffn_conv_w, 'ffn_conv_b': ffn_conv_b, 'ffn_wd': ffn_wd,
            'ple_norm_g': ple_norm_g, 'ple_wg': ple_wg, 'ple_bg': ple_bg, 'ple_wp': ple_wp,
            'final_g': final_g}


def _fwd_reference(x, p, norm1_g, w_in, dn_conv_w, dn_a_log, dn_dt_bias, dn_norm_g,
              lru_conv_w, lru_conv_b, lru_wa, lru_ba, lru_wx, lru_bx, lru_lambda, lru_norm_g,
              w_out, norm2_g, ffn_wg, ffn_wu, ffn_conv_w, ffn_conv_b, ffn_wd,
              ple_norm_g, ple_wg, ple_bg, ple_wp, final_g):
    B, S, _ = x.shape
    H, Dh = DN_HEADS, DN_HEAD_DIM
    r = x
    for i in range(DEPTH):
        h = rmsnorm(r, norm1_g[i])
        proj = h @ w_in[i]
        qkv = jax.nn.silu(dwconv(proj[..., Q_OFF:Z_OFF], dn_conv_w[i], SHORT_CONV_LEFT))
        q = l2norm(qkv[..., Q_OFF:K_OFF].reshape(B, S, H, Dh))
        k = l2norm(qkv[..., K_OFF:V_OFF].reshape(B, S, H, Dh))
        v = qkv[..., V_OFF:Z_OFF].reshape(B, S, H, Dh)
        z = proj[..., Z_OFF:BETA_OFF].reshape(B, S, H, Dh)
        beta = jax.nn.sigmoid(proj[..., BETA_OFF:ALPHA_OFF].reshape(B, S, 2, H))
        alpha = proj[..., ALPHA_OFF:LX_OFF].reshape(B, S, 2, H).astype(jnp.float32)
        g = -jnp.exp(dn_a_log[i].astype(jnp.float32)) * jax.nn.softplus(alpha + dn_dt_bias[i].astype(jnp.float32))
        o_f = gated_delta_chunked(q, k, v, g[:, :, 0], beta[:, :, 0])
        o_b = flip(gated_delta_chunked(flip(q), flip(k), flip(v), flip(g[:, :, 1]), flip(beta[:, :, 1])))
        dn_out = (rmsnorm(o_f + o_b, dn_norm_g[i]) * jax.nn.silu(z)).reshape(B, S, DN_WIDTH)
        xc = dwconv(proj[..., LX_OFF:LG_OFF], lru_conv_w[i], SHORT_CONV_LEFT) + lru_conv_b[i]
        h_f = rglru(xc, lru_wa[i, 0], lru_ba[i, 0], lru_wx[i, 0], lru_bx[i, 0], lru_lambda[i, 0])
        h_b = flip(rglru(flip(xc), lru_wa[i, 1], lru_ba[i, 1], lru_wx[i, 1], lru_bx[i, 1], lru_lambda[i, 1]))
        lru_out = rmsnorm(jax.nn.gelu(proj[..., LG_OFF:IN_COLS]) * (h_f + h_b), lru_norm_g[i])
        r = r + jnp.concatenate([dn_out, lru_out], axis=-1) @ w_out[i]
        h2 = rmsnorm(r, norm2_g[i])
        gate = dwconv(h2 @ ffn_wg[i], ffn_conv_w[i], FFN_CONV_LEFT) + ffn_conv_b[i]
        r = r + (jax.nn.gelu(gate) * (h2 @ ffn_wu[i])) @ ffn_wd[i]
        pg = jax.nn.sigmoid(rmsnorm(r, ple_norm_g[i]) @ ple_wg[i] + ple_bg[i])
        r = r + pg * (p[i] @ ple_wp[i])
    return rmsnorm(r, final_g)


import jax as _jax
import jax.numpy as _jnp

TWIN_FORMAT = 'train_step'
FWD_PARAMS = ['x', 'p', 'norm1_g', 'w_in', 'dn_conv_w', 'dn_a_log', 'dn_dt_bias', 'dn_norm_g', 'lru_conv_w', 'lru_conv_b', 'lru_wa', 'lru_ba', 'lru_wx', 'lru_bx', 'lru_lambda', 'lru_norm_g', 'w_out', 'norm2_g', 'ffn_wg', 'ffn_wu', 'ffn_conv_w', 'ffn_conv_b', 'ffn_wd', 'ple_norm_g', 'ple_wg', 'ple_bg', 'ple_wp', 'final_g']
TWIN_WEIGHTS = ['norm1_g', 'w_in', 'dn_conv_w', 'dn_a_log', 'dn_dt_bias', 'dn_norm_g', 'lru_conv_w', 'lru_conv_b', 'lru_wa', 'lru_ba', 'lru_wx', 'lru_bx', 'lru_lambda', 'lru_norm_g', 'w_out', 'norm2_g', 'ffn_wg', 'ffn_wu', 'ffn_conv_w', 'ffn_conv_b', 'ffn_wd', 'ple_norm_g', 'ple_wg', 'ple_bg', 'ple_wp', 'final_g']
TWIN_DIFF_INPUT = 'x'
TWIN_INPUTS = ['x', 'p', 'norm1_g', 'w_in', 'dn_conv_w', 'dn_a_log', 'dn_dt_bias', 'dn_norm_g', 'lru_conv_w', 'lru_conv_b', 'lru_wa', 'lru_ba', 'lru_wx', 'lru_bx', 'lru_lambda', 'lru_norm_g', 'w_out', 'norm2_g', 'ffn_wg', 'ffn_wu', 'ffn_conv_w', 'ffn_conv_b', 'ffn_wd', 'ple_norm_g', 'ple_wg', 'ple_bg', 'ple_wp', 'final_g', 'loss_target', 'm_norm1_g', 'm_w_in', 'm_dn_conv_w', 'm_dn_a_log', 'm_dn_dt_bias', 'm_dn_norm_g', 'm_lru_conv_w', 'm_lru_conv_b', 'm_lru_wa', 'm_lru_ba', 'm_lru_wx', 'm_lru_bx', 'm_lru_lambda', 'm_lru_norm_g', 'm_w_out', 'm_norm2_g', 'm_ffn_wg', 'm_ffn_wu', 'm_ffn_conv_w', 'm_ffn_conv_b', 'm_ffn_wd', 'm_ple_norm_g', 'm_ple_wg', 'm_ple_bg', 'm_ple_wp', 'm_final_g', 'v_norm1_g', 'v_w_in', 'v_dn_conv_w', 'v_dn_a_log', 'v_dn_dt_bias', 'v_dn_norm_g', 'v_lru_conv_w', 'v_lru_conv_b', 'v_lru_wa', 'v_lru_ba', 'v_lru_wx', 'v_lru_bx', 'v_lru_lambda', 'v_lru_norm_g', 'v_w_out', 'v_norm2_g', 'v_ffn_wg', 'v_ffn_wu', 'v_ffn_conv_w', 'v_ffn_conv_b', 'v_ffn_wd', 'v_ple_norm_g', 'v_ple_wg', 'v_ple_bg', 'v_ple_wp', 'v_final_g']
TWIN_OUTPUTS = ['loss', 'grad_x', 'grad_norm1_g', 'grad_w_in', 'grad_dn_conv_w', 'grad_dn_a_log', 'grad_dn_dt_bias', 'grad_dn_norm_g', 'grad_lru_conv_w', 'grad_lru_conv_b', 'grad_lru_wa', 'grad_lru_ba', 'grad_lru_wx', 'grad_lru_bx', 'grad_lru_lambda', 'grad_lru_norm_g', 'grad_w_out', 'grad_norm2_g', 'grad_ffn_wg', 'grad_ffn_wu', 'grad_ffn_conv_w', 'grad_ffn_conv_b', 'grad_ffn_wd', 'grad_ple_norm_g', 'grad_ple_wg', 'grad_ple_bg', 'grad_ple_wp', 'grad_final_g', 'delta_norm1_g', 'delta_w_in', 'delta_dn_conv_w', 'delta_dn_a_log', 'delta_dn_dt_bias', 'delta_dn_norm_g', 'delta_lru_conv_w', 'delta_lru_conv_b', 'delta_lru_wa', 'delta_lru_ba', 'delta_lru_wx', 'delta_lru_bx', 'delta_lru_lambda', 'delta_lru_norm_g', 'delta_w_out', 'delta_norm2_g', 'delta_ffn_wg', 'delta_ffn_wu', 'delta_ffn_conv_w', 'delta_ffn_conv_b', 'delta_ffn_wd', 'delta_ple_norm_g', 'delta_ple_wg', 'delta_ple_bg', 'delta_ple_wp', 'delta_final_g', 'new_m_norm1_g', 'new_m_w_in', 'new_m_dn_conv_w', 'new_m_dn_a_log', 'new_m_dn_dt_bias', 'new_m_dn_norm_g', 'new_m_lru_conv_w', 'new_m_lru_conv_b', 'new_m_lru_wa', 'new_m_lru_ba', 'new_m_lru_wx', 'new_m_lru_bx', 'new_m_lru_lambda', 'new_m_lru_norm_g', 'new_m_w_out', 'new_m_norm2_g', 'new_m_ffn_wg', 'new_m_ffn_wu', 'new_m_ffn_conv_w', 'new_m_ffn_conv_b', 'new_m_ffn_wd', 'new_m_ple_norm_g', 'new_m_ple_wg', 'new_m_ple_bg', 'new_m_ple_wp', 'new_m_final_g', 'new_v_norm1_g', 'new_v_w_in', 'new_v_dn_conv_w', 'new_v_dn_a_log', 'new_v_dn_dt_bias', 'new_v_dn_norm_g', 'new_v_lru_conv_w', 'new_v_lru_conv_b', 'new_v_lru_wa', 'new_v_lru_ba', 'new_v_lru_wx', 'new_v_lru_bx', 'new_v_lru_lambda', 'new_v_lru_norm_g', 'new_v_w_out', 'new_v_norm2_g', 'new_v_ffn_wg', 'new_v_ffn_wu', 'new_v_ffn_conv_w', 'new_v_ffn_conv_b', 'new_v_ffn_wd', 'new_v_ple_norm_g', 'new_v_ple_wg', 'new_v_ple_bg', 'new_v_ple_wp', 'new_v_final_g']
TWIN_LEAF_KINDS = {'loss': 'loss', 'grad_x': 'grad_x', 'grad_norm1_g': 'grad_w', 'grad_w_in': 'grad_w', 'grad_dn_conv_w': 'grad_w', 'grad_dn_a_log': 'grad_w', 'grad_dn_dt_bias': 'grad_w', 'grad_dn_norm_g': 'grad_w', 'grad_lru_conv_w': 'grad_w', 'grad_lru_conv_b': 'grad_w', 'grad_lru_wa': 'grad_w', 'grad_lru_ba': 'grad_w', 'grad_lru_wx': 'grad_w', 'grad_lru_bx': 'grad_w', 'grad_lru_lambda': 'grad_w', 'grad_lru_norm_g': 'grad_w', 'grad_w_out': 'grad_w', 'grad_norm2_g': 'grad_w', 'grad_ffn_wg': 'grad_w', 'grad_ffn_wu': 'grad_w', 'grad_ffn_conv_w': 'grad_w', 'grad_ffn_conv_b': 'grad_w', 'grad_ffn_wd': 'grad_w', 'grad_ple_norm_g': 'grad_w', 'grad_ple_wg': 'grad_w', 'grad_ple_bg': 'grad_w', 'grad_ple_wp': 'grad_w', 'grad_final_g': 'grad_w', 'delta_norm1_g': 'delta_w', 'delta_w_in': 'delta_w', 'delta_dn_conv_w': 'delta_w', 'delta_dn_a_log': 'delta_w', 'delta_dn_dt_bias': 'delta_w', 'delta_dn_norm_g': 'delta_w', 'delta_lru_conv_w': 'delta_w', 'delta_lru_conv_b': 'delta_w', 'delta_lru_wa': 'delta_w', 'delta_lru_ba': 'delta_w', 'delta_lru_wx': 'delta_w', 'delta_lru_bx': 'delta_w', 'delta_lru_lambda': 'delta_w', 'delta_lru_norm_g': 'delta_w', 'delta_w_out': 'delta_w', 'delta_norm2_g': 'delta_w', 'delta_ffn_wg': 'delta_w', 'delta_ffn_wu': 'delta_w', 'delta_ffn_conv_w': 'delta_w', 'delta_ffn_conv_b': 'delta_w', 'delta_ffn_wd': 'delta_w', 'delta_ple_norm_g': 'delta_w', 'delta_ple_wg': 'delta_w', 'delta_ple_bg': 'delta_w', 'delta_ple_wp': 'delta_w', 'delta_final_g': 'delta_w', 'new_m_norm1_g': 'new_m', 'new_m_w_in': 'new_m', 'new_m_dn_conv_w': 'new_m', 'new_m_dn_a_log': 'new_m', 'new_m_dn_dt_bias': 'new_m', 'new_m_dn_norm_g': 'new_m', 'new_m_lru_conv_w': 'new_m', 'new_m_lru_conv_b': 'new_m', 'new_m_lru_wa': 'new_m', 'new_m_lru_ba': 'new_m', 'new_m_lru_wx': 'new_m', 'new_m_lru_bx': 'new_m', 'new_m_lru_lambda': 'new_m', 'new_m_lru_norm_g': 'new_m', 'new_m_w_out': 'new_m', 'new_m_norm2_g': 'new_m', 'new_m_ffn_wg': 'new_m', 'new_m_ffn_wu': 'new_m', 'new_m_ffn_conv_w': 'new_m', 'new_m_ffn_conv_b': 'new_m', 'new_m_ffn_wd': 'new_m', 'new_m_ple_norm_g': 'new_m', 'new_m_ple_wg': 'new_m', 'new_m_ple_bg': 'new_m', 'new_m_ple_wp': 'new_m', 'new_m_final_g': 'new_m', 'new_v_norm1_g': 'new_v', 'new_v_w_in': 'new_v', 'new_v_dn_conv_w': 'new_v', 'new_v_dn_a_log': 'new_v', 'new_v_dn_dt_bias': 'new_v', 'new_v_dn_norm_g': 'new_v', 'new_v_lru_conv_w': 'new_v', 'new_v_lru_conv_b': 'new_v', 'new_v_lru_wa': 'new_v', 'new_v_lru_ba': 'new_v', 'new_v_lru_wx': 'new_v', 'new_v_lru_bx': 'new_v', 'new_v_lru_lambda': 'new_v', 'new_v_lru_norm_g': 'new_v', 'new_v_w_out': 'new_v', 'new_v_norm2_g': 'new_v', 'new_v_ffn_wg': 'new_v', 'new_v_ffn_wu': 'new_v', 'new_v_ffn_conv_w': 'new_v', 'new_v_ffn_conv_b': 'new_v', 'new_v_ffn_wd': 'new_v', 'new_v_ple_norm_g': 'new_v', 'new_v_ple_wg': 'new_v', 'new_v_ple_bg': 'new_v', 'new_v_ple_wp': 'new_v', 'new_v_final_g': 'new_v'}


def _forward(args):
    return _fwd_reference(*[args[k] for k in FWD_PARAMS])


def _output_shape():
    out = _jax.eval_shape(lambda: _forward(_fwd_setup_inputs(0)))
    return out.shape, out.dtype

N_MICROBATCH = 1
ADAM_LR = 0.001
ADAM_B1 = 0.9
ADAM_B2 = 0.999
ADAM_EPS = 1e-08
ADAM_WD = 0.01
ADAM_STEP = 10
PER_EXAMPLE_BATCH_AXIS = {'x': 0, 'p': 1, 'loss_target': 0}
SHARED_INPUTS = []
_WEIGHT_DTYPES = {'norm1_g': _jnp.float32, 'w_in': _jnp.float32, 'dn_conv_w': _jnp.float32, 'dn_a_log': _jnp.float32, 'dn_dt_bias': _jnp.float32, 'dn_norm_g': _jnp.float32, 'lru_conv_w': _jnp.float32, 'lru_conv_b': _jnp.float32, 'lru_wa': _jnp.float32, 'lru_ba': _jnp.float32, 'lru_wx': _jnp.float32, 'lru_bx': _jnp.float32, 'lru_lambda': _jnp.float32, 'lru_norm_g': _jnp.float32, 'w_out': _jnp.float32, 'norm2_g': _jnp.float32, 'ffn_wg': _jnp.float32, 'ffn_wu': _jnp.float32, 'ffn_conv_w': _jnp.float32, 'ffn_conv_b': _jnp.float32, 'ffn_wd': _jnp.float32, 'ple_norm_g': _jnp.float32, 'ple_wg': _jnp.float32, 'ple_bg': _jnp.float32, 'ple_wp': _jnp.float32, 'final_g': _jnp.float32}
MOMENT_SCALE = {'norm1_g': 2.247895e-01, 'w_in': 1.290837e-01, 'dn_conv_w': 8.082963e-02, 'dn_a_log': 2.595625e-01, 'dn_dt_bias': 2.508858e-01, 'dn_norm_g': 2.499137e-01, 'lru_conv_w': 2.187053e-01, 'lru_conv_b': 2.803062e+00, 'lru_wa': 5.081341e-02, 'lru_ba': 4.483228e-02, 'lru_wx': 9.530933e-02, 'lru_bx': 4.643409e-02, 'lru_lambda': 8.143782e-02, 'lru_norm_g': 2.074750e-01, 'w_out': 1.772650e-01, 'norm2_g': 1.462071e-01, 'ffn_wg': 6.023103e-02, 'ffn_wu': 5.971876e-02, 'ffn_conv_w': 6.261709e-02, 'ffn_conv_b': 6.179536e-02, 'ffn_wd': 9.878093e-02, 'ple_norm_g': 3.109590e-02, 'ple_wg': 3.075874e-02, 'ple_bg': 3.427176e-02, 'ple_wp': 7.855910e-02, 'final_g': 6.412754e+01}


def _to_microbatches(a, axis):
    t = _jnp.moveaxis(a, axis, 0)
    t = t.reshape((N_MICROBATCH, t.shape[0] // N_MICROBATCH) + t.shape[1:])
    return _jnp.moveaxis(t, 1, axis + 1)


def setup_inputs(seed: int = 0) -> dict:
    inp = _fwd_setup_inputs(seed)
    key = _jax.random.fold_in(_jax.random.key(seed), 7919)
    shape, _ = _output_shape()
    out = dict(inp)
    out["loss_target"] = _jax.random.normal(_jax.random.fold_in(key, 0), shape, _jnp.float32)
    for i, name in enumerate(TWIN_WEIGHTS):
        w = inp[name].astype(_jnp.float32)
        if MOMENT_SCALE is None:
            s = _jnp.sqrt(_jnp.mean(_jnp.square(w)) + 1e-30)
        else:
            s = MOMENT_SCALE[name]
        km, kv = _jax.random.split(_jax.random.fold_in(key, i + 1))
        out[name] = w
        out["m_" + name] = s * _jax.random.normal(km, w.shape, _jnp.float32)
        out["v_" + name] = (s * s) * _jax.random.uniform(kv, w.shape, _jnp.float32, 0.5, 1.5)
    if N_MICROBATCH > 1:
        for name, axis in PER_EXAMPLE_BATCH_AXIS.items():
            out[name] = _to_microbatches(out[name], axis)
    return {'x': out['x'], 'p': out['p'], 'norm1_g': out['norm1_g'], 'w_in': out['w_in'], 'dn_conv_w': out['dn_conv_w'], 'dn_a_log': out['dn_a_log'], 'dn_dt_bias': out['dn_dt_bias'], 'dn_norm_g': out['dn_norm_g'], 'lru_conv_w': out['lru_conv_w'], 'lru_conv_b': out['lru_conv_b'], 'lru_wa': out['lru_wa'], 'lru_ba': out['lru_ba'], 'lru_wx': out['lru_wx'], 'lru_bx': out['lru_bx'], 'lru_lambda': out['lru_lambda'], 'lru_norm_g': out['lru_norm_g'], 'w_out': out['w_out'], 'norm2_g': out['norm2_g'], 'ffn_wg': out['ffn_wg'], 'ffn_wu': out['ffn_wu'], 'ffn_conv_w': out['ffn_conv_w'], 'ffn_conv_b': out['ffn_conv_b'], 'ffn_wd': out['ffn_wd'], 'ple_norm_g': out['ple_norm_g'], 'ple_wg': out['ple_wg'], 'ple_bg': out['ple_bg'], 'ple_wp': out['ple_wp'], 'final_g': out['final_g'], 'loss_target': out['loss_target'], 'm_norm1_g': out['m_norm1_g'], 'm_w_in': out['m_w_in'], 'm_dn_conv_w': out['m_dn_conv_w'], 'm_dn_a_log': out['m_dn_a_log'], 'm_dn_dt_bias': out['m_dn_dt_bias'], 'm_dn_norm_g': out['m_dn_norm_g'], 'm_lru_conv_w': out['m_lru_conv_w'], 'm_lru_conv_b': out['m_lru_conv_b'], 'm_lru_wa': out['m_lru_wa'], 'm_lru_ba': out['m_lru_ba'], 'm_lru_wx': out['m_lru_wx'], 'm_lru_bx': out['m_lru_bx'], 'm_lru_lambda': out['m_lru_lambda'], 'm_lru_norm_g': out['m_lru_norm_g'], 'm_w_out': out['m_w_out'], 'm_norm2_g': out['m_norm2_g'], 'm_ffn_wg': out['m_ffn_wg'], 'm_ffn_wu': out['m_ffn_wu'], 'm_ffn_conv_w': out['m_ffn_conv_w'], 'm_ffn_conv_b': out['m_ffn_conv_b'], 'm_ffn_wd': out['m_ffn_wd'], 'm_ple_norm_g': out['m_ple_norm_g'], 'm_ple_wg': out['m_ple_wg'], 'm_ple_bg': out['m_ple_bg'], 'm_ple_wp': out['m_ple_wp'], 'm_final_g': out['m_final_g'], 'v_norm1_g': out['v_norm1_g'], 'v_w_in': out['v_w_in'], 'v_dn_conv_w': out['v_dn_conv_w'], 'v_dn_a_log': out['v_dn_a_log'], 'v_dn_dt_bias': out['v_dn_dt_bias'], 'v_dn_norm_g': out['v_dn_norm_g'], 'v_lru_conv_w': out['v_lru_conv_w'], 'v_lru_conv_b': out['v_lru_conv_b'], 'v_lru_wa': out['v_lru_wa'], 'v_lru_ba': out['v_lru_ba'], 'v_lru_wx': out['v_lru_wx'], 'v_lru_bx': out['v_lru_bx'], 'v_lru_lambda': out['v_lru_lambda'], 'v_lru_norm_g': out['v_lru_norm_g'], 'v_w_out': out['v_w_out'], 'v_norm2_g': out['v_norm2_g'], 'v_ffn_wg': out['v_ffn_wg'], 'v_ffn_wu': out['v_ffn_wu'], 'v_ffn_conv_w': out['v_ffn_conv_w'], 'v_ffn_conv_b': out['v_ffn_conv_b'], 'v_ffn_wd': out['v_ffn_wd'], 'v_ple_norm_g': out['v_ple_norm_g'], 'v_ple_wg': out['v_ple_wg'], 'v_ple_bg': out['v_ple_bg'], 'v_ple_wp': out['v_ple_wp'], 'v_final_g': out['v_final_g']}


def _loss(weights, diff, rest, loss_target):
    with _jax.named_scope("forward"):
        args = {**rest, TWIN_DIFF_INPUT: diff, **{k: w.astype(_WEIGHT_DTYPES[k]) for k, w in weights.items()}}
        y = _forward(args)
    with _jax.named_scope("loss_head"):
        err = _jnp.square(y.astype(_jnp.float32) - loss_target)
        return 0.5 * _jnp.sum(_jnp.mean(err, axis=-1)) if err.ndim else 0.5 * err


def _adamw(w, g, m, v):
    m = ADAM_B1 * m + (1.0 - ADAM_B1) * g
    v = ADAM_B2 * v + (1.0 - ADAM_B2) * _jnp.square(g)
    m_hat = m / (1.0 - ADAM_B1 ** ADAM_STEP)
    v_hat = v / (1.0 - ADAM_B2 ** ADAM_STEP)
    delta = -ADAM_LR * (m_hat / (_jnp.sqrt(v_hat) + ADAM_EPS) + ADAM_WD * w)
    return delta, m, v


def reference(x, p, norm1_g, w_in, dn_conv_w, dn_a_log, dn_dt_bias, dn_norm_g, lru_conv_w, lru_conv_b, lru_wa, lru_ba, lru_wx, lru_bx, lru_lambda, lru_norm_g, w_out, norm2_g, ffn_wg, ffn_wu, ffn_conv_w, ffn_conv_b, ffn_wd, ple_norm_g, ple_wg, ple_bg, ple_wp, final_g, loss_target, m_norm1_g, m_w_in, m_dn_conv_w, m_dn_a_log, m_dn_dt_bias, m_dn_norm_g, m_lru_conv_w, m_lru_conv_b, m_lru_wa, m_lru_ba, m_lru_wx, m_lru_bx, m_lru_lambda, m_lru_norm_g, m_w_out, m_norm2_g, m_ffn_wg, m_ffn_wu, m_ffn_conv_w, m_ffn_conv_b, m_ffn_wd, m_ple_norm_g, m_ple_wg, m_ple_bg, m_ple_wp, m_final_g, v_norm1_g, v_w_in, v_dn_conv_w, v_dn_a_log, v_dn_dt_bias, v_dn_norm_g, v_lru_conv_w, v_lru_conv_b, v_lru_wa, v_lru_ba, v_lru_wx, v_lru_bx, v_lru_lambda, v_lru_norm_g, v_w_out, v_norm2_g, v_ffn_wg, v_ffn_wu, v_ffn_conv_w, v_ffn_conv_b, v_ffn_wd, v_ple_norm_g, v_ple_wg, v_ple_bg, v_ple_wp, v_final_g):
    given = dict(x=x, p=p, norm1_g=norm1_g, w_in=w_in, dn_conv_w=dn_conv_w, dn_a_log=dn_a_log, dn_dt_bias=dn_dt_bias, dn_norm_g=dn_norm_g, lru_conv_w=lru_conv_w, lru_conv_b=lru_conv_b, lru_wa=lru_wa, lru_ba=lru_ba, lru_wx=lru_wx, lru_bx=lru_bx, lru_lambda=lru_lambda, lru_norm_g=lru_norm_g, w_out=w_out, norm2_g=norm2_g, ffn_wg=ffn_wg, ffn_wu=ffn_wu, ffn_conv_w=ffn_conv_w, ffn_conv_b=ffn_conv_b, ffn_wd=ffn_wd, ple_norm_g=ple_norm_g, ple_wg=ple_wg, ple_bg=ple_bg, ple_wp=ple_wp, final_g=final_g, loss_target=loss_target, m_norm1_g=m_norm1_g, m_w_in=m_w_in, m_dn_conv_w=m_dn_conv_w, m_dn_a_log=m_dn_a_log, m_dn_dt_bias=m_dn_dt_bias, m_dn_norm_g=m_dn_norm_g, m_lru_conv_w=m_lru_conv_w, m_lru_conv_b=m_lru_conv_b, m_lru_wa=m_lru_wa, m_lru_ba=m_lru_ba, m_lru_wx=m_lru_wx, m_lru_bx=m_lru_bx, m_lru_lambda=m_lru_lambda, m_lru_norm_g=m_lru_norm_g, m_w_out=m_w_out, m_norm2_g=m_norm2_g, m_ffn_wg=m_ffn_wg, m_ffn_wu=m_ffn_wu, m_ffn_conv_w=m_ffn_conv_w, m_ffn_conv_b=m_ffn_conv_b, m_ffn_wd=m_ffn_wd, m_ple_norm_g=m_ple_norm_g, m_ple_wg=m_ple_wg, m_ple_bg=m_ple_bg, m_ple_wp=m_ple_wp, m_final_g=m_final_g, v_norm1_g=v_norm1_g, v_w_in=v_w_in, v_dn_conv_w=v_dn_conv_w, v_dn_a_log=v_dn_a_log, v_dn_dt_bias=v_dn_dt_bias, v_dn_norm_g=v_dn_norm_g, v_lru_conv_w=v_lru_conv_w, v_lru_conv_b=v_lru_conv_b, v_lru_wa=v_lru_wa, v_lru_ba=v_lru_ba, v_lru_wx=v_lru_wx, v_lru_bx=v_lru_bx, v_lru_lambda=v_lru_lambda, v_lru_norm_g=v_lru_norm_g, v_w_out=v_w_out, v_norm2_g=v_norm2_g, v_ffn_wg=v_ffn_wg, v_ffn_wu=v_ffn_wu, v_ffn_conv_w=v_ffn_conv_w, v_ffn_conv_b=v_ffn_conv_b, v_ffn_wd=v_ffn_wd, v_ple_norm_g=v_ple_norm_g, v_ple_wg=v_ple_wg, v_ple_bg=v_ple_bg, v_ple_wp=v_ple_wp, v_final_g=v_final_g)
    weights = {n: given[n] for n in TWIN_WEIGHTS}
    shared = {n: given[n] for n in SHARED_INPUTS}
    per_example = {n: given[n] for n in ['x', 'p']}
    grad_fn = _jax.value_and_grad(_loss, argnums=(0, 1))

    def one_microbatch(ex, loss_target):
        ex = dict(ex)
        diff = ex.pop(TWIN_DIFF_INPUT)
        return grad_fn(weights, diff, {**shared, **ex}, loss_target)

    if N_MICROBATCH == 1:
        loss, (grad_w, grad_x) = one_microbatch(per_example, given["loss_target"])
    else:
        def body(carry, xs):
            loss_sum, grad_sum = carry
            l_k, (gw_k, gx_k) = one_microbatch(xs[0], xs[1])
            with _jax.named_scope("update"):
                return (loss_sum + l_k, _jax.tree.map(_jnp.add, grad_sum, gw_k)), gx_k

        init = (_jnp.zeros((), _jnp.float32), _jax.tree.map(_jnp.zeros_like, weights))
        (loss, grad_w), grad_x = _jax.lax.scan(body, init, (per_example, given["loss_target"]))
    with _jax.named_scope("update"):
        delta_w, new_m, new_v = {}, {}, {}
        for n in TWIN_WEIGHTS:
            delta_w[n], new_m[n], new_v[n] = _adamw(weights[n], grad_w[n], given["m_" + n], given["v_" + n])
    return (loss, grad_x, *[grad_w[n] for n in TWIN_WEIGHTS], *[delta_w[n] for n in TWIN_WEIGHTS],
            *[new_m[n] for n in TWIN_WEIGHTS], *[new_v[n] for n in TWIN_WEIGHTS])
```

```python
import functools
import math

import jax
import jax.numpy as jnp
from jax import lax
from jax.experimental import pallas as pl
from jax.experimental.pallas import tpu as pltpu

F32 = jnp.float32
BF16 = jnp.bfloat16

D_MODEL = 1024
DN_HEADS = 4
DN_HEAD_DIM = 128
DN_WIDTH = 512
LRU_WIDTH = 512
LRU_C = 8.0
CHUNK = 64
EPS = 1e-6
P_LX, P_LG, P_Q, P_K, P_V, P_Z, P_GATE, P_COLS = 0, 512, 1024, 1536, 2048, 2560, 3072, 3200
N_GATE = 16
LANES = 128
VMEM_LIMIT = 56 * 1024 * 1024
MM_A_BLOCK_BYTES = 6 * 1024 * 1024
MM_B_BLOCK_BYTES = 4 * 1024 * 1024

ADAM_LR, ADAM_B1, ADAM_B2, ADAM_EPS, ADAM_WD, ADAM_STEP = 0.001, 0.9, 0.999, 1e-08, 0.01, 10


def _cparams(sem):
    return pltpu.CompilerParams(dimension_semantics=sem, vmem_limit_bytes=VMEM_LIMIT)


def _row_tile(m, want=512):
    t = min(want, m)
    while m % t:
        t //= 2
    return t


def _row_tile16(r, cap=4096):
    best = None
    for d in range(16, min(r, cap) + 1, 16):
        if r % d == 0:
            best = d
    return best if best is not None else r


def _col_tile(n, cap=1536):
    best = None
    for d in range(LANES, min(n, cap) + 1, LANES):
        if n % d == 0:
            best = d
    return best if best is not None else n


def _bdot(a, b):
    return jnp.dot(a.astype(BF16), b.astype(BF16), preferred_element_type=F32)


def _bdot_nt(a, b):
    return lax.dot_general(a.astype(BF16), b.astype(BF16), (((1,), (1,)), ((), ())), preferred_element_type=F32)


def _bdot_tn(a, b):
    return lax.dot_general(a.astype(BF16), b.astype(BF16), (((0,), (0,)), ((), ())), preferred_element_type=F32)


def _hdot(a, b):
    return jnp.dot(a, b, preferred_element_type=F32, precision=lax.Precision.HIGHEST)


def _hdot_tn(a, b):
    return lax.dot_general(a, b, (((0,), (0,)), ((), ())), preferred_element_type=F32, precision=lax.Precision.HIGHEST)


def _rms(x, g):
    return x * lax.rsqrt(jnp.mean(x * x, axis=-1, keepdims=True) + EPS) * g


def _gelu(x):
    return 0.5 * x * (1.0 + jnp.tanh(0.7978845608028654 * (x + 0.044715 * x * x * x)))


def _sigmoid(x):
    return 1.0 / (1.0 + jnp.exp(-x))


def _silu(x):
    return x * _sigmoid(x)


def _softplus(x):
    return jnp.maximum(x, 0.0) + jnp.log(1.0 + jnp.exp(-jnp.abs(x)))


def _rmsnorm_fwd(x, g, name):
    t, d = x.shape
    tm = _row_tile(t)

    def body(x_ref, g_ref, o_ref):
        o_ref[...] = _rms(x_ref[...], g_ref[...]).astype(o_ref.dtype)

    return pl.pallas_call(
        body, name=name, grid=(t // tm,),
        in_specs=[pl.BlockSpec((tm, d), lambda i: (i, 0)), pl.BlockSpec((1, d), lambda i: (0, 0))],
        out_specs=pl.BlockSpec((tm, d), lambda i: (i, 0)),
        out_shape=jax.ShapeDtypeStruct((t, d), BF16),
        compiler_params=_cparams(("parallel",)),
    )(x, g)


def _rmsnorm_bwd(x, g, dh, dres, name):
    t, d = x.shape
    tm = _row_tile(t)

    def body(x_ref, g_ref, dh_ref, dres_ref, dx_ref, dg_ref):
        _, vjp = jax.vjp(_rms, x_ref[...], g_ref[...])
        dx, dg = vjp(dh_ref[...])
        dx_ref[...] = dx + dres_ref[...]

        @pl.when(pl.program_id(0) == 0)
        def _():
            dg_ref[...] = jnp.zeros_like(dg_ref)

        dg_ref[...] += dg

    return pl.pallas_call(
        body, name=name, grid=(t // tm,),
        in_specs=[pl.BlockSpec((tm, d), lambda i: (i, 0)), pl.BlockSpec((1, d), lambda i: (0, 0)),
                  pl.BlockSpec((tm, d), lambda i: (i, 0)), pl.BlockSpec((tm, d), lambda i: (i, 0))],
        out_specs=[pl.BlockSpec((tm, d), lambda i: (i, 0)), pl.BlockSpec((1, d), lambda i: (0, 0))],
        out_shape=[jax.ShapeDtypeStruct((t, d), F32), jax.ShapeDtypeStruct((1, d), F32)],
        compiler_params=_cparams(("arbitrary",)),
    )(x, g, dh, dres)


def _matmul(a_list, b_list, name, res=None, bias=None):
    m = a_list[0].shape[0]
    n = b_list[0].shape[1]
    a_row_bytes = sum(a.shape[1] * a.dtype.itemsize for a in a_list)
    tm = _row_tile(m, max(256, min(1024, MM_A_BLOCK_BYTES // a_row_bytes // 256 * 256)))
    k_total = sum(b.shape[0] for b in b_list)
    tn = _col_tile(n, max(LANES, min(1024, MM_B_BLOCK_BYTES // (2 * k_total) // LANES * LANES)))
    na = len(a_list)

    def body(*refs):
        a_refs, b_refs = refs[:na], refs[na:2 * na]
        rest = list(refs[2 * na:])
        o_ref = rest.pop()
        acc = _bdot(a_refs[0][...], b_refs[0][...])
        for a_ref, b_ref in zip(a_refs[1:], b_refs[1:]):
            acc = acc + _bdot(a_ref[...], b_ref[...])
        if res is not None:
            acc = acc + rest.pop(0)[...]
        if bias is not None:
            acc = acc + rest.pop(0)[...]
        o_ref[...] = acc

    in_specs = [pl.BlockSpec((tm, a.shape[1]), lambda i, j: (i, 0)) for a in a_list]
    in_specs += [pl.BlockSpec((b.shape[0], tn), lambda i, j: (0, j)) for b in b_list]
    args = list(a_list) + list(b_list)
    if res is not None:
        in_specs.append(pl.BlockSpec((tm, tn), lambda i, j: (i, j)))
        args.append(res)
    if bias is not None:
        in_specs.append(pl.BlockSpec((1, tn), lambda i, j: (0, j)))
        args.append(bias)
    return pl.pallas_call(
        body, name=name, grid=(m // tm, n // tn), in_specs=in_specs,
        out_specs=pl.BlockSpec((tm, tn), lambda i, j: (i, j)),
        out_shape=jax.ShapeDtypeStruct((m, n), F32),
        compiler_params=_cparams(("parallel", "parallel")),
    )(*args)


def _matmul_tn(a, b, name):
    m, k = a.shape
    n = b.shape[1]
    tm = _row_tile(m, 512)
    tn = _col_tile(n, 512)

    def body(a_ref, b_ref, o_ref):
        @pl.when(pl.program_id(1) == 0)
        def _():
            o_ref[...] = jnp.zeros_like(o_ref)

        o_ref[...] += _bdot_tn(a_ref[...], b_ref[...])

    return pl.pallas_call(
        body, name=name, grid=(n // tn, m // tm),
        in_specs=[pl.BlockSpec((tm, k), lambda j, i: (i, 0)), pl.BlockSpec((tm, tn), lambda j, i: (i, j))],
        out_specs=pl.BlockSpec((k, tn), lambda j, i: (0, j)),
        out_shape=jax.ShapeDtypeStruct((k, n), F32),
        compiler_params=_cparams(("parallel", "arbitrary")),
    )(a, b)


def _shift_down(x, k):
    row = lax.broadcasted_iota(jnp.int32, x.shape, 0)
    return jnp.where(row >= k, pltpu.roll(x, k, 0), 0.0)


def _shift_up(x, k):
    s = x.shape[0]
    row = lax.broadcasted_iota(jnp.int32, x.shape, 0)
    return jnp.where(row < s - k, pltpu.roll(x, s - k, 0), 0.0)


def _conv_taps(x, ntaps, left):
    out = []
    for j in range(ntaps):
        off = j - left
        out.append(_shift_down(x, -off) if off < 0 else (_shift_up(x, off) if off > 0 else x))
    return out


def _conv_fwd(x, w, left):
    taps = _conv_taps(x, w.shape[0], left)
    acc = taps[0] * w[0:1, :]
    for j in range(1, w.shape[0]):
        acc = acc + taps[j] * w[j:j + 1, :]
    return acc


def _conv_bwd(x, w, left, dout):
    ntaps = w.shape[0]
    dx = None
    for j in range(ntaps):
        off = j - left
        sh = _shift_up(dout, -off) if off < 0 else (_shift_down(dout, off) if off > 0 else dout)
        term = sh * w[j:j + 1, :]
        dx = term if dx is None else dx + term
    taps = _conv_taps(x, ntaps, left)
    dw = jnp.concatenate([jnp.sum(dout * tp, axis=0, keepdims=True) for tp in taps], axis=0)
    return dx, dw


def _scan(a, b, reverse):
    s = a.shape[0]
    d = 1
    sh = _shift_up if reverse else _shift_down
    while d < s:
        b = a * sh(b, d) + b
        a = a * sh(a, d)
        d *= 2
    return b


def _dn_gates_fn(pre, alog, dtb):
    lane = lax.broadcasted_iota(jnp.int32, pre.shape, 1)
    beta = _sigmoid(pre)
    g = -jnp.exp(alog) * _softplus(pre + dtb)
    return jnp.where(lane < N_GATE // 2, beta, jnp.where(lane < N_GATE, g, 0.0))


def _dn_gates_fwd(proj, alog, dtb, name):
    t = proj.shape[0]
    tm = _row_tile(t)
    cb = P_GATE // LANES

    def body(p_ref, a_ref, d_ref, o_ref):
        o_ref[...] = _dn_gates_fn(p_ref[...], a_ref[...], d_ref[...])

    return pl.pallas_call(
        body, name=name, grid=(t // tm,),
        in_specs=[pl.BlockSpec((tm, LANES), lambda i: (i, cb)), pl.BlockSpec((1, LANES), lambda i: (0, 0)),
                  pl.BlockSpec((1, LANES), lambda i: (0, 0))],
        out_specs=pl.BlockSpec((tm, LANES), lambda i: (i, 0)),
        out_shape=jax.ShapeDtypeStruct((t, LANES), F32),
        compiler_params=_cparams(("parallel",)),
    )(proj, alog, dtb)


def _dn_gates_bwd(proj, alog, dtb, dgb_f, dgb_b, name):
    t = proj.shape[0]
    tm = _row_tile(t)
    cb = P_GATE // LANES

    def body(p_ref, a_ref, d_ref, g1_ref, g2_ref, dp_ref, da_ref, dd_ref):
        _, vjp = jax.vjp(_dn_gates_fn, p_ref[...], a_ref[...], d_ref[...])
        dp, da, dd = vjp(g1_ref[...] + g2_ref[...])
        dp_ref[...] = dp

        @pl.when(pl.program_id(0) == 0)
        def _():
            da_ref[...] = jnp.zeros_like(da_ref)
            dd_ref[...] = jnp.zeros_like(dd_ref)

        da_ref[...] += da
        dd_ref[...] += dd

    row = pl.BlockSpec((tm, LANES), lambda i: (i, 0))
    vec = pl.BlockSpec((1, LANES), lambda i: (0, 0))
    return pl.pallas_call(
        body, name=name, grid=(t // tm,),
        in_specs=[pl.BlockSpec((tm, LANES), lambda i: (i, cb)), vec, vec, row, row],
        out_specs=[row, vec, vec],
        out_shape=[jax.ShapeDtypeStruct((t, LANES), F32), jax.ShapeDtypeStruct((1, LANES), F32),
                   jax.ShapeDtypeStruct((1, LANES), F32)],
        compiler_params=_cparams(("arbitrary",)),
    )(proj, alog, dtb, dgb_f, dgb_b)


def _dn_prep_fn(x, w, is_qk):
    act = _silu(_conv_fwd(x, w, 2))
    nrm = act * lax.rsqrt(jnp.sum(act * act, axis=-1, keepdims=True) + EPS)
    return jnp.where(is_qk, nrm, act)


def _dn_prep_fwd(proj3, conv_w, name):
    bsz, s, _ = proj3.shape
    nblk = 3 * DN_WIDTH // LANES
    cb = P_Q // LANES

    def body(x_ref, w_ref, o_ref):
        o_ref[0] = _dn_prep_fn(x_ref[0], w_ref[...], pl.program_id(1) < 2 * DN_HEADS)

    return pl.pallas_call(
        body, name=name, grid=(bsz, nblk),
        in_specs=[pl.BlockSpec((1, s, LANES), lambda b, j: (b, 0, cb + j)), pl.BlockSpec((4, LANES), lambda b, j: (0, j))],
        out_specs=pl.BlockSpec((1, s, LANES), lambda b, j: (b, 0, j)),
        out_shape=jax.ShapeDtypeStruct((bsz, s, 3 * DN_WIDTH), F32),
        compiler_params=_cparams(("parallel", "parallel")),
    )(proj3, conv_w)


def _dn_prep_bwd(proj3, conv_w, dqkv, name):
    bsz, s, _ = proj3.shape
    nblk = 3 * DN_WIDTH // LANES
    cb = P_Q // LANES

    def body(x_ref, w_ref, d_ref, dx_ref, dw_ref):
        x, w, d = x_ref[0], w_ref[...], d_ref[0]
        is_qk = pl.program_id(0) < 2 * DN_HEADS
        pre = _conv_fwd(x, w, 2)

        def post(pre):
            act = _silu(pre)
            nrm = act * lax.rsqrt(jnp.sum(act * act, axis=-1, keepdims=True) + EPS)
            return jnp.where(is_qk, nrm, act)

        _, vjp = jax.vjp(post, pre)
        (dpre,) = vjp(d)
        dx, dw = _conv_bwd(x, w, 2, dpre)
        dx_ref[0] = dx

        @pl.when(pl.program_id(1) == 0)
        def _():
            dw_ref[...] = jnp.zeros_like(dw_ref)

        dw_ref[...] += dw

    return pl.pallas_call(
        body, name=name, grid=(nblk, bsz),
        in_specs=[pl.BlockSpec((1, s, LANES), lambda j, b: (b, 0, cb + j)), pl.BlockSpec((4, LANES), lambda j, b: (0, j)),
                  pl.BlockSpec((1, s, LANES), lambda j, b: (b, 0, j))],
        out_specs=[pl.BlockSpec((1, s, LANES), lambda j, b: (b, 0, j)), pl.BlockSpec((4, LANES), lambda j, b: (0, j))],
        out_shape=[jax.ShapeDtypeStruct((bsz, s, 3 * DN_WIDTH), F32), jax.ShapeDtypeStruct((4, 3 * DN_WIDTH), F32)],
        compiler_params=_cparams(("parallel", "arbitrary")),
    )(proj3, conv_w, dqkv)


def _dn_chunk(q, k, v, g, beta, state, rev):
    c = q.shape[0]
    row = lax.broadcasted_iota(jnp.int32, (c, c), 0)
    col = lax.broadcasted_iota(jnp.int32, (c, c), 1)
    incl = (row <= col) if rev else (row >= col)
    strict = (row < col) if rev else (row > col)
    ones_incl = jnp.where(incl, 1.0, 0.0)
    gb = jnp.broadcast_to(g, (c, c))
    gc = _hdot(ones_incl, gb)
    gr = _hdot_tn(gb, jnp.where(incl, 0.0, 1.0) + jnp.where(row == col, 1.0, 0.0))
    gcum = gc[:, 0:1]
    decay = jnp.where(incl, jnp.exp(jnp.where(incl, gc - gr, 0.0)), 0.0)
    qs = q * (DN_HEAD_DIM ** -0.5)
    kb = k * beta
    a = jnp.where(strict, _bdot_nt(kb, k) * decay, 0.0)
    eye = jnp.where(row == col, 1.0, 0.0)
    tinv = eye - a
    pw = _hdot(a, a)
    n = 2
    while n < c:
        tinv = tinv + _hdot(tinv, pw)
        n *= 2
        if n < c:
            pw = _hdot(pw, pw)
    egc = jnp.exp(gcum)
    u = _bdot(tinv, v * beta)
    w = _bdot(tinv, kb * egc)
    attn = _bdot_nt(qs, k) * decay
    glast = gcum[0:1, :] if rev else gcum[c - 1:c, :]
    q_dec = qs * egc
    k_dec = k * jnp.exp(glast - gcum)
    v_new = u - _bdot(w, state)
    o = _bdot(q_dec, state) + _bdot(attn, v_new)
    new_state = state * jnp.exp(glast) + _bdot_tn(k_dec, v_new)
    return o, new_state


def _dn_block_tokens(s):
    return min(512, s)


def _dn_fwd(qkv, gb, rev, name):
    bsz, s, _ = qkv.shape
    tb = _dn_block_tokens(s)
    nt = s // tb
    nc = tb // CHUNK
    lane0 = DN_HEADS if rev else 0

    def body(qkv_ref, gb_ref, o_ref, st_ref, state):
        @pl.when(pl.program_id(1) == 0)
        def _():
            state[...] = jnp.zeros_like(state)

        def step(ci, carry):
            cidx = (nc - 1 - ci) if rev else ci
            rows = pl.ds(pl.multiple_of(cidx * CHUNK, CHUNK), CHUNK)
            gates = gb_ref[0, rows, :]
            outs = []
            for h in range(DN_HEADS):
                q = qkv_ref[0, rows, h * LANES:(h + 1) * LANES]
                k = qkv_ref[0, rows, DN_WIDTH + h * LANES:DN_WIDTH + (h + 1) * LANES]
                v = qkv_ref[0, rows, 2 * DN_WIDTH + h * LANES:2 * DN_WIDTH + (h + 1) * LANES]
                beta = gates[:, lane0 + h:lane0 + h + 1]
                g = gates[:, N_GATE // 2 + lane0 + h:N_GATE // 2 + lane0 + h + 1]
                st_ref[0, cidx, h] = state[h]
                o, new_state = _dn_chunk(q, k, v, g, beta, state[h], rev)
                state[h] = new_state
                outs.append(o)
            o_ref[0, rows, :] = jnp.concatenate(outs, axis=-1)
            return carry

        lax.fori_loop(0, nc, step, 0)

    tix = (lambda b, j: (b, nt - 1 - j, 0)) if rev else (lambda b, j: (b, j, 0))
    six = (lambda b, j: (b, nt - 1 - j, 0, 0, 0)) if rev else (lambda b, j: (b, j, 0, 0, 0))
    return pl.pallas_call(
        body, name=name, grid=(bsz, nt),
        in_specs=[pl.BlockSpec((1, tb, 3 * DN_WIDTH), tix), pl.BlockSpec((1, tb, LANES), tix)],
        out_specs=[pl.BlockSpec((1, tb, DN_WIDTH), tix), pl.BlockSpec((1, nc, DN_HEADS, DN_HEAD_DIM, DN_HEAD_DIM), six)],
        out_shape=[jax.ShapeDtypeStruct((bsz, s, DN_WIDTH), F32),
                   jax.ShapeDtypeStruct((bsz, s // CHUNK, DN_HEADS, DN_HEAD_DIM, DN_HEAD_DIM), F32)],
        scratch_shapes=[pltpu.VMEM((DN_HEADS, DN_HEAD_DIM, DN_HEAD_DIM), F32)],
        compiler_params=_cparams(("parallel", "arbitrary")),
    )(qkv, gb)


def _dn_bwd(qkv, gb, states, do, rev, name):
    bsz, s, _ = qkv.shape
    tb = _dn_block_tokens(s)
    nt = s // tb
    nc = tb // CHUNK
    lane0 = DN_HEADS if rev else 0

    def body(qkv_ref, gb_ref, st_ref, do_ref, dqkv_ref, dgb_ref, dstate):
        @pl.when(pl.program_id(1) == 0)
        def _():
            dstate[...] = jnp.zeros_like(dstate)

        lane = lax.broadcasted_iota(jnp.int32, (CHUNK, LANES), 1)

        def step(ci, carry):
            cidx = ci if rev else (nc - 1 - ci)
            rows = pl.ds(pl.multiple_of(cidx * CHUNK, CHUNK), CHUNK)
            gates = gb_ref[0, rows, :]
            dgates = jnp.zeros((CHUNK, LANES), F32)
            dqs, dks, dvs = [], [], []
            for h in range(DN_HEADS):
                q = qkv_ref[0, rows, h * LANES:(h + 1) * LANES]
                k = qkv_ref[0, rows, DN_WIDTH + h * LANES:DN_WIDTH + (h + 1) * LANES]
                v = qkv_ref[0, rows, 2 * DN_WIDTH + h * LANES:2 * DN_WIDTH + (h + 1) * LANES]
                lb = lane0 + h
                lg = N_GATE // 2 + lane0 + h
                beta = gates[:, lb:lb + 1]
                g = gates[:, lg:lg + 1]
                _, vjp = jax.vjp(functools.partial(_dn_chunk, rev=rev), q, k, v, g, beta, st_ref[0, cidx, h])
                dq, dk, dv, dg, dbeta, dst = vjp((do_ref[0, rows, h * LANES:(h + 1) * LANES], dstate[h]))
                dstate[h] = dst
                dqs.append(dq)
                dks.append(dk)
                dvs.append(dv)
                dgates = dgates + jnp.where(lane == lb, dbeta, 0.0) + jnp.where(lane == lg, dg, 0.0)
            dqkv_ref[0, rows, :] = jnp.concatenate(dqs + dks + dvs, axis=-1)
            dgb_ref[0, rows, :] = dgates
            return carry

        lax.fori_loop(0, nc, step, 0)

    tix = (lambda b, j: (b, j, 0)) if rev else (lambda b, j: (b, nt - 1 - j, 0))
    six = (lambda b, j: (b, j, 0, 0, 0)) if rev else (lambda b, j: (b, nt - 1 - j, 0, 0, 0))
    return pl.pallas_call(
        body, name=name, grid=(bsz, nt),
        in_specs=[pl.BlockSpec((1, tb, 3 * DN_WIDTH), tix), pl.BlockSpec((1, tb, LANES), tix),
                  pl.BlockSpec((1, nc, DN_HEADS, DN_HEAD_DIM, DN_HEAD_DIM), six), pl.BlockSpec((1, tb, DN_WIDTH), tix)],
        out_specs=[pl.BlockSpec((1, tb, 3 * DN_WIDTH), tix), pl.BlockSpec((1, tb, LANES), tix)],
        out_shape=[jax.ShapeDtypeStruct((bsz, s, 3 * DN_WIDTH), F32), jax.ShapeDtypeStruct((bsz, s, LANES), F32)],
        scratch_shapes=[pltpu.VMEM((DN_HEADS, DN_HEAD_DIM, DN_HEAD_DIM), F32)],
        compiler_params=_cparams(("parallel", "arbitrary")),
    )(qkv, gb, states, do)


def _lru_gate_fn(xc, wa, ba, wx, bx, lam):
    r = _sigmoid(_bdot(xc, wa) + ba)
    ig = _sigmoid(_bdot(xc, wx) + bx)
    log_a = -LRU_C * r * _softplus(-lam)
    a = jnp.exp(log_a)
    b = jnp.sqrt(-jnp.tanh(log_a) * (a * a + 1.0)) * (ig * xc)
    return a, b


def _lru_fwd(proj3, conv_w, conv_b, wa, ba, wx, bx, lam, name):
    bsz, s, _ = proj3.shape
    nblk = LRU_WIDTH // LANES

    def body(x_ref, cw_ref, cb_ref, wa_ref, ba_ref, wx_ref, bx_ref, lam_ref, hf_ref, hb_ref):
        xc = _conv_fwd(x_ref[0], cw_ref[...], 2) + cb_ref[...]
        for d, h_ref in ((0, hf_ref), (1, hb_ref)):
            a, b = _lru_gate_fn(xc, wa_ref[d, 0], ba_ref[d:d + 1, :], wx_ref[d, 0], bx_ref[d:d + 1, :], lam_ref[d:d + 1, :])
            h_ref[0] = _scan(a, b, reverse=(d == 1))

    col = pl.BlockSpec((1, s, LANES), lambda b, j: (b, 0, j))
    vec2 = pl.BlockSpec((2, LANES), lambda b, j: (0, j))
    wspec = pl.BlockSpec((2, 1, LANES, LANES), lambda b, j: (0, j, 0, 0))
    return pl.pallas_call(
        body, name=name, grid=(bsz, nblk),
        in_specs=[col, pl.BlockSpec((4, LANES), lambda b, j: (0, j)), pl.BlockSpec((1, LANES), lambda b, j: (0, j)),
                  wspec, vec2, wspec, vec2, vec2],
        out_specs=[col, col],
        out_shape=[jax.ShapeDtypeStruct((bsz, s, LRU_WIDTH), F32)] * 2,
        compiler_params=_cparams(("parallel", "parallel")),
    )(proj3, conv_w, conv_b, wa, ba, wx, bx, lam)


def _lru_bwd(proj3, conv_w, conv_b, wa, ba, wx, bx, lam, hf, hb, dh, name):
    bsz, s, _ = proj3.shape
    nblk = LRU_WIDTH // LANES

    def body(x_ref, cw_ref, cb_ref, wa_ref, ba_ref, wx_ref, bx_ref, lam_ref, hf_ref, hb_ref, dh_ref,
             dx_ref, dcw_ref, dcb_ref, dwa_ref, dba_ref, dwx_ref, dbx_ref, dlam_ref):
        @pl.when(pl.program_id(1) == 0)
        def _():
            for r in (dcw_ref, dcb_ref, dwa_ref, dba_ref, dwx_ref, dbx_ref, dlam_ref):
                r[...] = jnp.zeros_like(r)

        x, cw = x_ref[0], cw_ref[...]
        xc = _conv_fwd(x, cw, 2) + cb_ref[...]
        dhv = dh_ref[0]
        dxc = jnp.zeros_like(xc)
        for d, h_ref in ((0, hf_ref), (1, hb_ref)):
            rev = d == 1
            args = (xc, wa_ref[d, 0].astype(F32), ba_ref[d:d + 1, :], wx_ref[d, 0].astype(F32), bx_ref[d:d + 1, :],
                    lam_ref[d:d + 1, :])
            (a, _), vjp = jax.vjp(_lru_gate_fn, *args)
            h = h_ref[0]
            a_next = _shift_down(a, 1) if rev else _shift_up(a, 1)
            lam_adj = _scan(a_next, dhv, reverse=not rev)
            h_prev = _shift_up(h, 1) if rev else _shift_down(h, 1)
            dxc_d, dwa, dba, dwx, dbx, dlam = vjp((lam_adj * h_prev, lam_adj))
            dxc = dxc + dxc_d
            dwa_ref[d, 0] += dwa
            dwx_ref[d, 0] += dwx
            dba_ref[d:d + 1, :] += dba
            dbx_ref[d:d + 1, :] += dbx
            dlam_ref[d:d + 1, :] += dlam
        dx, dcw = _conv_bwd(x, cw, 2, dxc)
        dx_ref[0] = dx
        dcw_ref[...] += dcw
        dcb_ref[...] += jnp.sum(dxc, axis=0, keepdims=True)

    col = pl.BlockSpec((1, s, LANES), lambda j, b: (b, 0, j))
    vec1 = pl.BlockSpec((1, LANES), lambda j, b: (0, j))
    vec2 = pl.BlockSpec((2, LANES), lambda j, b: (0, j))
    vec4 = pl.BlockSpec((4, LANES), lambda j, b: (0, j))
    wspec = pl.BlockSpec((2, 1, LANES, LANES), lambda j, b: (0, j, 0, 0))
    wshape = jax.ShapeDtypeStruct((2, nblk, LANES, LANES), F32)
    v2shape = jax.ShapeDtypeStruct((2, LRU_WIDTH), F32)
    return pl.pallas_call(
        body, name=name, grid=(nblk, bsz),
        in_specs=[col, vec4, vec1, wspec, vec2, wspec, vec2, vec2, col, col, col],
        out_specs=[col, vec4, vec1, wspec, vec2, wspec, vec2, vec2],
        out_shape=[jax.ShapeDtypeStruct((bsz, s, LRU_WIDTH), F32), jax.ShapeDtypeStruct((4, LRU_WIDTH), F32),
                   jax.ShapeDtypeStruct((1, LRU_WIDTH), F32), wshape, v2shape, wshape, v2shape, v2shape],
        compiler_params=_cparams(("parallel", "arbitrary")),
    )(proj3, conv_w, conv_b, wa, ba, wx, bx, lam, hf, hb, dh)


def _mix_fn(o_f, o_b, z, lg, hf, hb, dn_g, lru_g):
    osum = o_f + o_b
    heads = []
    for h in range(DN_HEADS):
        sl = slice(h * LANES, (h + 1) * LANES)
        heads.append(_rms(osum[:, sl], dn_g) * _silu(z[:, sl]))
    lru = _rms(_gelu(lg) * (hf + hb), lru_g)
    return jnp.concatenate(heads + [lru], axis=-1)


def _mix_specs(tm):
    w = DN_WIDTH
    row = pl.BlockSpec((tm, w), lambda i: (i, 0))
    z = pl.BlockSpec((tm, w), lambda i: (i, P_Z // w))
    lg = pl.BlockSpec((tm, w), lambda i: (i, P_LG // w))
    return [row, row, z, lg, row, row, pl.BlockSpec((1, LANES), lambda i: (0, 0)), pl.BlockSpec((1, w), lambda i: (0, 0))]


def _mix_fwd(o_f, o_b, proj, hf, hb, dn_g, lru_g, name):
    t = proj.shape[0]
    tm = _row_tile(t)

    def body(of_ref, ob_ref, z_ref, lg_ref, hf_ref, hb_ref, dg_ref, lgn_ref, o_ref):
        o_ref[...] = _mix_fn(of_ref[...], ob_ref[...], z_ref[...], lg_ref[...], hf_ref[...], hb_ref[...],
                             dg_ref[...], lgn_ref[...]).astype(o_ref.dtype)

    return pl.pallas_call(
        body, name=name, grid=(t // tm,), in_specs=_mix_specs(tm),
        out_specs=pl.BlockSpec((tm, D_MODEL), lambda i: (i, 0)),
        out_shape=jax.ShapeDtypeStruct((t, D_MODEL), BF16),
        compiler_params=_cparams(("parallel",)),
    )(o_f, o_b, proj, proj, hf, hb, dn_g, lru_g)


def _mix_bwd(o_f, o_b, proj, hf, hb, dn_g, lru_g, dmix, name):
    t = proj.shape[0]
    tm = _row_tile(t)

    def body(of_ref, ob_ref, z_ref, lg_ref, hf_ref, hb_ref, dg_ref, lgn_ref, dm_ref,
             do_ref, dz_ref, dlg_ref, dh_ref, ddg_ref, dlgn_ref):
        _, vjp = jax.vjp(_mix_fn, of_ref[...], ob_ref[...], z_ref[...], lg_ref[...], hf_ref[...], hb_ref[...],
                         dg_ref[...], lgn_ref[...])
        do, _, dz, dlg, dh, _, ddg, dlgn = vjp(dm_ref[...])
        do_ref[...] = do
        dz_ref[...] = dz
        dlg_ref[...] = dlg
        dh_ref[...] = dh

        @pl.when(pl.program_id(0) == 0)
        def _():
            ddg_ref[...] = jnp.zeros_like(ddg_ref)
            dlgn_ref[...] = jnp.zeros_like(dlgn_ref)

        ddg_ref[...] += ddg
        dlgn_ref[...] += dlgn

    row = pl.BlockSpec((tm, DN_WIDTH), lambda i: (i, 0))
    return pl.pallas_call(
        body, name=name, grid=(t // tm,),
        in_specs=_mix_specs(tm) + [pl.BlockSpec((tm, D_MODEL), lambda i: (i, 0))],
        out_specs=[row, row, row, row, pl.BlockSpec((1, LANES), lambda i: (0, 0)), pl.BlockSpec((1, DN_WIDTH), lambda i: (0, 0))],
        out_shape=[jax.ShapeDtypeStruct((t, DN_WIDTH), F32)] * 4
        + [jax.ShapeDtypeStruct((1, LANES), F32), jax.ShapeDtypeStruct((1, DN_WIDTH), F32)],
        compiler_params=_cparams(("arbitrary",)),
    )(o_f, o_b, proj, proj, hf, hb, dn_g, lru_g, dmix)


def _ffn_act_fwd(g3, u3, conv_w, conv_b, name):
    bsz, s, f = g3.shape
    nblk = f // LANES

    def body(g_ref, u_ref, w_ref, b_ref, o_ref):
        gate = _conv_fwd(g_ref[0], w_ref[...], 1) + b_ref[...]
        o_ref[0] = (_gelu(gate) * u_ref[0]).astype(o_ref.dtype)

    col = pl.BlockSpec((1, s, LANES), lambda b, j: (b, 0, j))
    return pl.pallas_call(
        body, name=name, grid=(bsz, nblk),
        in_specs=[col, col, pl.BlockSpec((3, LANES), lambda b, j: (0, j)), pl.BlockSpec((1, LANES), lambda b, j: (0, j))],
        out_specs=col, out_shape=jax.ShapeDtypeStruct((bsz, s, f), BF16),
        compiler_params=_cparams(("parallel", "parallel")),
    )(g3, u3, conv_w, conv_b)


def _ffn_act_bwd(g3, u3, conv_w, conv_b, dact3, name):
    bsz, s, f = g3.shape
    nblk = f // LANES

    def body(g_ref, u_ref, w_ref, b_ref, d_ref, dg_ref, du_ref, dw_ref, db_ref):
        g, w, u = g_ref[0], w_ref[...], u_ref[0]
        gate = _conv_fwd(g, w, 1) + b_ref[...]
        _, vjp = jax.vjp(lambda gt, uu: _gelu(gt) * uu, gate, u)
        dgate, du = vjp(d_ref[0])
        dg, dw = _conv_bwd(g, w, 1, dgate)
        dg_ref[0] = dg
        du_ref[0] = du

        @pl.when(pl.program_id(1) == 0)
        def _():
            dw_ref[...] = jnp.zeros_like(dw_ref)
            db_ref[...] = jnp.zeros_like(db_ref)

        dw_ref[...] += dw
        db_ref[...] += jnp.sum(dgate, axis=0, keepdims=True)

    col = pl.BlockSpec((1, s, LANES), lambda j, b: (b, 0, j))
    w3 = pl.BlockSpec((3, LANES), lambda j, b: (0, j))
    w1 = pl.BlockSpec((1, LANES), lambda j, b: (0, j))
    return pl.pallas_call(
        body, name=name, grid=(nblk, bsz),
        in_specs=[col, col, w3, w1, col], out_specs=[col, col, w3, w1],
        out_shape=[jax.ShapeDtypeStruct((bsz, s, f), F32), jax.ShapeDtypeStruct((bsz, s, f), F32),
                   jax.ShapeDtypeStruct((3, f), F32), jax.ShapeDtypeStruct((1, f), F32)],
        compiler_params=_cparams(("parallel", "arbitrary")),
    )(g3, u3, conv_w, conv_b, dact3)


def _ple_fn(r, pg, pp, bg):
    return r + _sigmoid(pg + bg) * pp


def _ple_fwd(r, pg, pp, bg, name):
    t, d = r.shape
    tm = _row_tile(t)

    def body(r_ref, pg_ref, pp_ref, bg_ref, o_ref):
        o_ref[...] = _ple_fn(r_ref[...], pg_ref[...], pp_ref[...], bg_ref[...])

    row = pl.BlockSpec((tm, d), lambda i: (i, 0))
    return pl.pallas_call(
        body, name=name, grid=(t // tm,), in_specs=[row, row, row, pl.BlockSpec((1, d), lambda i: (0, 0))],
        out_specs=row, out_shape=jax.ShapeDtypeStruct((t, d), F32),
        compiler_params=_cparams(("parallel",)),
    )(r, pg, pp, bg)


def _ple_bwd(pg, pp, bg, dr, name):
    t, d = pg.shape
    tm = _row_tile(t)

    def body(pg_ref, pp_ref, bg_ref, dr_ref, dpg_ref, dpp_ref, dbg_ref):
        _, vjp = jax.vjp(lambda a, b, c: _sigmoid(a + c) * b, pg_ref[...], pp_ref[...], bg_ref[...])
        dpg, dpp, dbg = vjp(dr_ref[...])
        dpg_ref[...] = dpg
        dpp_ref[...] = dpp

        @pl.when(pl.program_id(0) == 0)
        def _():
            dbg_ref[...] = jnp.zeros_like(dbg_ref)

        dbg_ref[...] += dbg

    row = pl.BlockSpec((tm, d), lambda i: (i, 0))
    vec = pl.BlockSpec((1, d), lambda i: (0, 0))
    return pl.pallas_call(
        body, name=name, grid=(t // tm,), in_specs=[row, row, vec, row], out_specs=[row, row, vec],
        out_shape=[jax.ShapeDtypeStruct((t, d), F32), jax.ShapeDtypeStruct((t, d), F32), jax.ShapeDtypeStruct((1, d), F32)],
        compiler_params=_cparams(("arbitrary",)),
    )(pg, pp, bg, dr)


def _loss_head(r, g, target, name):
    t, d = r.shape
    tm = _row_tile(t)

    def loss_fn(x, gg, tgt):
        err = _rms(x, gg) - tgt
        return 0.5 * jnp.sum(jnp.sum(err * err, axis=-1, keepdims=True) * (1.0 / d), axis=0, keepdims=True)

    def body(r_ref, g_ref, t_ref, l_ref, dr_ref, dg_ref):
        val, vjp = jax.vjp(lambda x, gg: loss_fn(x, gg, t_ref[...]), r_ref[...], g_ref[...])
        dx, dg = vjp(jnp.ones((1, 1), F32))
        dr_ref[...] = dx

        @pl.when(pl.program_id(0) == 0)
        def _():
            l_ref[...] = jnp.zeros_like(l_ref)
            dg_ref[...] = jnp.zeros_like(dg_ref)

        l_ref[...] += val
        dg_ref[...] += dg

    row = pl.BlockSpec((tm, d), lambda i: (i, 0))
    vec = pl.BlockSpec((1, d), lambda i: (0, 0))
    one = pl.BlockSpec((1, 1), lambda i: (0, 0))
    return pl.pallas_call(
        body, name=name, grid=(t // tm,), in_specs=[row, vec, row], out_specs=[one, row, vec],
        out_shape=[jax.ShapeDtypeStruct((1, 1), F32), jax.ShapeDtypeStruct((t, d), F32), jax.ShapeDtypeStruct((1, d), F32)],
        compiler_params=_cparams(("arbitrary",)),
    )(r, g, target)


def _adamw(w, g, m, v, name):
    r, c = w.shape
    tr = r if r <= 512 else _row_tile(r, 512)

    def body(w_ref, g_ref, m_ref, v_ref, d_ref, nm_ref, nv_ref):
        gg = g_ref[...]
        nm = ADAM_B1 * m_ref[...] + (1.0 - ADAM_B1) * gg
        nv = ADAM_B2 * v_ref[...] + (1.0 - ADAM_B2) * (gg * gg)
        m_hat = nm / (1.0 - ADAM_B1 ** ADAM_STEP)
        v_hat = nv / (1.0 - ADAM_B2 ** ADAM_STEP)
        d_ref[...] = -ADAM_LR * (m_hat / (jnp.sqrt(v_hat) + ADAM_EPS) + ADAM_WD * w_ref[...])
        nm_ref[...] = nm
        nv_ref[...] = nv

    blk = pl.BlockSpec((tr, c), lambda i: (i, 0))
    return pl.pallas_call(
        body, name=name, grid=(r // tr,), in_specs=[blk] * 4, out_specs=[blk] * 3,
        out_shape=[jax.ShapeDtypeStruct((r, c), F32)] * 3,
        compiler_params=_cparams(("parallel",)),
    )(w, g, m, v)


def _prepare_weights(w):
    nl = w["w_in"].shape[0]
    w_in = w["w_in"].astype(BF16)
    pad = jnp.zeros(w_in.shape[:2] + (P_COLS - w_in.shape[2],), BF16)
    split = 4 * DN_WIDTH + N_GATE
    w_in_p = jnp.concatenate([w_in[:, :, split:], w_in[:, :, :split], pad], axis=-1)
    gate_vec = lambda a: jnp.pad(a.reshape(nl, 1, N_GATE // 2), ((0, 0), (0, 0), (N_GATE // 2, LANES - N_GATE)))

    def pair_blocks(a):
        a = a.reshape(nl, 2, 4, 2, 64, 64)
        z = jnp.zeros_like(a[:, :, :, 0])
        top = jnp.concatenate([a[:, :, :, 0], z], axis=-1)
        bot = jnp.concatenate([z, a[:, :, :, 1]], axis=-1)
        return jnp.concatenate([top, bot], axis=-2).astype(BF16)

    bf = lambda a: a.astype(BF16)
    tr = lambda a: jnp.swapaxes(a.astype(BF16), 1, 2)
    return dict(
        norm1_g=w["norm1_g"][:, None, :], w_in=w_in_p, w_in_t=jnp.swapaxes(w_in_p, 1, 2),
        dn_conv_w=w["dn_conv_w"], alog=gate_vec(w["dn_a_log"]), dtb=gate_vec(w["dn_dt_bias"]),
        dn_norm_g=w["dn_norm_g"][:, None, :], lru_conv_w=w["lru_conv_w"], lru_conv_b=w["lru_conv_b"][:, None, :],
        lru_wa=pair_blocks(w["lru_wa"]), lru_ba=w["lru_ba"], lru_wx=pair_blocks(w["lru_wx"]), lru_bx=w["lru_bx"],
        lru_lambda=w["lru_lambda"], lru_norm_g=w["lru_norm_g"][:, None, :],
        w_out=bf(w["w_out"]), w_out_t=tr(w["w_out"]), norm2_g=w["norm2_g"][:, None, :],
        ffn_wg=bf(w["ffn_wg"]), ffn_wg_t=tr(w["ffn_wg"]), ffn_wu=bf(w["ffn_wu"]), ffn_wu_t=tr(w["ffn_wu"]),
        ffn_conv_w=w["ffn_conv_w"], ffn_conv_b=w["ffn_conv_b"][:, None, :],
        ffn_wd=bf(w["ffn_wd"]), ffn_wd_t=tr(w["ffn_wd"]), ple_norm_g=w["ple_norm_g"][:, None, :],
        ple_wg=bf(w["ple_wg"]), ple_wg_t=tr(w["ple_wg"]), ple_bg=w["ple_bg"][:, None, :], ple_wp=bf(w["ple_wp"]),
        final_g=w["final_g"][None, :],
    )


def _unpair_blocks(a):
    top = a[:, :, :64, :64]
    bot = a[:, :, 64:, 64:]
    return jnp.stack([top, bot], axis=2).reshape(2, 8, 64, 64)


def _local_step(x, p, target, w):
    bsz, s, d = x.shape
    t = bsz * s
    nl = w["w_in"].shape[0]
    kw = _prepare_weights(w)
    flat = lambda a: a.reshape(t, a.shape[-1])
    seq = lambda a: a.reshape(bsz, s, a.shape[-1])

    saved = []
    r = flat(x)
    for i in range(nl):
        n = f"l{i}_"
        sv = {"r0": r}
        h = _rmsnorm_fwd(r, kw["norm1_g"][i], n + "norm1")
        proj = _matmul([h], [kw["w_in"][i]], n + "in_proj")
        gb = _dn_gates_fwd(proj, kw["alog"][i], kw["dtb"][i], n + "dn_gates")
        qkv = _dn_prep_fwd(seq(proj), kw["dn_conv_w"][i], n + "dn_prep")
        o_f, st_f = _dn_fwd(qkv, seq(gb), False, n + "dn_fwd")
        o_b, st_b = _dn_fwd(qkv, seq(gb), True, n + "dn_rev")
        lru_args = (seq(proj), kw["lru_conv_w"][i], kw["lru_conv_b"][i], kw["lru_wa"][i], kw["lru_ba"][i],
                    kw["lru_wx"][i], kw["lru_bx"][i], kw["lru_lambda"][i])
        hf, hb = _lru_fwd(*lru_args, n + "lru")
        mix_args = (flat(o_f), flat(o_b), proj, flat(hf), flat(hb), kw["dn_norm_g"][i], kw["lru_norm_g"][i])
        mix = _mix_fwd(*mix_args, n + "mix")
        r1 = _matmul([mix], [kw["w_out"][i]], n + "out_proj", res=r)
        h2 = _rmsnorm_fwd(r1, kw["norm2_g"][i], n + "norm2")
        fg = _matmul([h2], [kw["ffn_wg"][i]], n + "ffn_g")
        fu = _matmul([h2], [kw["ffn_wu"][i]], n + "ffn_u")
        act = _ffn_act_fwd(seq(fg), seq(fu), kw["ffn_conv_w"][i], kw["ffn_conv_b"][i], n + "ffn_act")
        r2 = _matmul([flat(act)], [kw["ffn_wd"][i]], n + "ffn_d", res=r1)
        hp = _rmsnorm_fwd(r2, kw["ple_norm_g"][i], n + "ple_norm")
        pg = _matmul([hp], [kw["ple_wg"][i]], n + "ple_g")
        pi = flat(p[i])
        pp = _matmul([pi], [kw["ple_wp"][i]], n + "ple_p")
        r3 = _ple_fwd(r2, pg, pp, kw["ple_bg"][i], n + "ple")
        sv.update(h=h, proj=proj, gb=gb, qkv=qkv, st_f=st_f, st_b=st_b, lru_args=lru_args, hf=hf, hb=hb,
                  mix_args=mix_args, mix=mix, r1=r1, h2=h2, fg=fg, fu=fu, act=act, r2=r2, hp=hp, pg=pg, pp=pp, pi=pi)
        saved.append(sv)
        r = r3

    loss, dr, dfinal = _loss_head(r, kw["final_g"], flat(target), "loss_head")
    grads = {k: [None] * nl for k in (
        "norm1_g", "w_in", "dn_conv_w", "dn_a_log", "dn_dt_bias", "dn_norm_g", "lru_conv_w", "lru_conv_b", "lru_wa", "lru_ba",
        "lru_wx", "lru_bx", "lru_lambda", "lru_norm_g", "w_out", "norm2_g", "ffn_wg", "ffn_wu", "ffn_conv_w", "ffn_conv_b",
        "ffn_wd", "ple_norm_g", "ple_wg", "ple_bg", "ple_wp")}

    for i in reversed(range(nl)):
        n = f"b{i}_"
        sv = saved[i]
        dpg, dpp, dbg = _ple_bwd(sv["pg"], sv["pp"], kw["ple_bg"][i], dr, n + "ple")
        grads["ple_bg"][i] = dbg[0]
        grads["ple_wp"][i] = _matmul_tn(sv["pi"], dpp, n + "ple_wp")
        grads["ple_wg"][i] = _matmul_tn(sv["hp"], dpg, n + "ple_wg")
        dhp = _matmul([dpg], [kw["ple_wg_t"][i]], n + "ple_dh")
        dr2, dg = _rmsnorm_bwd(sv["r2"], kw["ple_norm_g"][i], dhp, dr, n + "ple_norm")
        grads["ple_norm_g"][i] = dg[0]
        grads["ffn_wd"][i] = _matmul_tn(flat(sv["act"]), dr2, n + "ffn_wd")
        dact = _matmul([dr2], [kw["ffn_wd_t"][i]], n + "ffn_dact")
        dfg, dfu, dcw, dcb = _ffn_act_bwd(seq(sv["fg"]), seq(sv["fu"]), kw["ffn_conv_w"][i], kw["ffn_conv_b"][i], seq(dact),
                                          n + "ffn_act")
        grads["ffn_conv_w"][i] = dcw
        grads["ffn_conv_b"][i] = dcb[0]
        grads["ffn_wg"][i] = _matmul_tn(sv["h2"], flat(dfg), n + "ffn_wg")
        grads["ffn_wu"][i] = _matmul_tn(sv["h2"], flat(dfu), n + "ffn_wu")
        dh2 = _matmul([flat(dfg), flat(dfu)], [kw["ffn_wg_t"][i], kw["ffn_wu_t"][i]], n + "ffn_dh")
        dr1, dg = _rmsnorm_bwd(sv["r1"], kw["norm2_g"][i], dh2, dr2, n + "norm2")
        grads["norm2_g"][i] = dg[0]
        grads["w_out"][i] = _matmul_tn(sv["mix"], dr1, n + "w_out")
        dmix = _matmul([dr1], [kw["w_out_t"][i]], n + "dmix")
        do, dz, dlg, dh, ddn_g, dlru_g = _mix_bwd(*sv["mix_args"], dmix, n + "mix")
        grads["dn_norm_g"][i] = ddn_g[0]
        grads["lru_norm_g"][i] = dlru_g[0]
        dlx, dcw, dcb, dwa, dba, dwx, dbx, dlam = _lru_bwd(*sv["lru_args"], sv["hf"], sv["hb"], seq(dh), n + "lru")
        grads["lru_conv_w"][i] = dcw
        grads["lru_conv_b"][i] = dcb[0]
        grads["lru_wa"][i] = _unpair_blocks(dwa)
        grads["lru_wx"][i] = _unpair_blocks(dwx)
        grads["lru_ba"][i], grads["lru_bx"][i], grads["lru_lambda"][i] = dba, dbx, dlam
        dqkv_f, dgb_f = _dn_bwd(sv["qkv"], seq(sv["gb"]), sv["st_f"], seq(do), False, n + "dn_fwd")
        dqkv_b, dgb_b = _dn_bwd(sv["qkv"], seq(sv["gb"]), sv["st_b"], seq(do), True, n + "dn_rev")
        dgate, dalog, ddtb = _dn_gates_bwd(sv["proj"], kw["alog"][i], kw["dtb"][i], flat(dgb_f), flat(dgb_b), n + "dn_gates")
        grads["dn_a_log"][i] = dalog[0, N_GATE // 2:N_GATE].reshape(2, DN_HEADS)
        grads["dn_dt_bias"][i] = ddtb[0, N_GATE // 2:N_GATE].reshape(2, DN_HEADS)
        dpqkv, dcw = _dn_prep_bwd(seq(sv["proj"]), kw["dn_conv_w"][i], dqkv_f + dqkv_b, n + "dn_prep")
        grads["dn_conv_w"][i] = dcw
        segs = [flat(dlx), dlg, flat(dpqkv), dz, dgate]
        offs = [P_LX, P_LG, P_Q, P_Z, P_GATE]
        wt = kw["w_in_t"][i]
        dh1 = _matmul(segs, [wt[o:o + sg.shape[1]] for o, sg in zip(offs, segs)], n + "in_dh")
        dwp = jnp.concatenate([_matmul_tn(sv["h"], sg, n + f"w_in{j}") for j, sg in enumerate(segs)], axis=-1)
        split = 4 * DN_WIDTH + N_GATE
        grads["w_in"][i] = jnp.concatenate([dwp[:, P_Q:P_Q + split], dwp[:, :P_Q]], axis=-1)
        dr, dg = _rmsnorm_bwd(sv["r0"], kw["norm1_g"][i], dh1, dr1, n + "norm1")
        grads["norm1_g"][i] = dg[0]

    out = {k: jnp.stack(v) for k, v in grads.items()}
    out["final_g"] = dfinal[0]
    return loss[0, 0], dr.reshape(bsz, s, d), out


MESH = pl.DeviceIdType.MESH
ANY = pl.BlockSpec(memory_space=pl.ANY)
N_CHIPS = 4
HALF = 2


def _place():
    x, y, c = lax.axis_index("x"), lax.axis_index("y"), lax.axis_index("c")
    chips = [(1 - x, y), (x, 1 - y), (1 - x, 1 - y)]
    return x, y, c, chips


def _gather_weights(wb, ws):
    nl = wb.shape[0]

    def body(wb_ref, ws_ref, gb_ref, gs_ref, send_sems, recv_sems, local_sems):
        x, y, c, chips = _place()
        me = 2 * x + y
        mine = pl.ds(c * HALF, HALF)
        theirs = pl.ds((1 - c) * HALF, HALF)
        loc_b = pltpu.make_async_copy(wb_ref, gb_ref.at[me], local_sems.at[0])
        loc_s = pltpu.make_async_copy(ws_ref, gs_ref.at[me], local_sems.at[1])
        loc_b.start()
        loc_s.start()

        def rc(src, dst, k, to):
            return pltpu.make_async_remote_copy(src_ref=src, dst_ref=dst, send_sem=send_sems.at[k], recv_sem=recv_sems.at[k],
                                                device_id=to, device_id_type=MESH)

        first = []
        for r, (px, py) in enumerate(chips):
            first.append(rc(wb_ref.at[mine], gb_ref.at[me, mine], r, (px, py, c)))
            first.append(rc(ws_ref, gs_ref.at[me], 3 + r, (px, py, c)))
        for cp in first:
            cp.start()
        passed = []
        for r, (px, py) in enumerate(chips):
            peer = 2 * px + py
            rc(wb_ref.at[mine], gb_ref.at[peer, mine], r, (px, py, c)).wait_recv()
            fw = rc(gb_ref.at[peer, mine], gb_ref.at[peer, mine], 6 + r, (x, y, 1 - c))
            fw.start()
            passed.append(fw)
        for r, (px, py) in enumerate(chips):
            peer = 2 * px + py
            rc(ws_ref, gs_ref.at[peer], 3 + r, (px, py, c)).wait_recv()
            rc(gb_ref.at[peer, theirs], gb_ref.at[peer, theirs], 6 + r, (x, y, 1 - c)).wait_recv()
        for cp in first + passed:
            cp.wait_send()
        loc_b.wait()
        loc_s.wait()

    return pl.pallas_call(
        body, name="gather_weights", in_specs=[ANY, ANY], out_specs=[ANY, ANY],
        out_shape=[jax.ShapeDtypeStruct((N_CHIPS,) + wb.shape, wb.dtype), jax.ShapeDtypeStruct((N_CHIPS,) + ws.shape, ws.dtype)],
        scratch_shapes=[pltpu.SemaphoreType.DMA((9,)), pltpu.SemaphoreType.DMA((9,)), pltpu.SemaphoreType.DMA((2,))],
    )(wb, ws)


def _swap_halves(pg):
    def body(pg_ref, l1_ref, send_sem, recv_sem):
        x, y, c, _ = _place()
        theirs = pl.ds((1 - c) * HALF, HALF)
        cp = pltpu.make_async_remote_copy(src_ref=pg_ref.at[:, theirs], dst_ref=l1_ref, send_sem=send_sem, recv_sem=recv_sem,
                                          device_id=(x, y, 1 - c), device_id_type=MESH)
        cp.start()
        cp.wait_send()
        cp.wait_recv()

    n, _, r, lanes = pg.shape
    return pl.pallas_call(
        body, name="swap_halves", in_specs=[ANY], out_specs=ANY,
        out_shape=jax.ShapeDtypeStruct((n, HALF, r, lanes), pg.dtype),
        scratch_shapes=[pltpu.SemaphoreType.DMA, pltpu.SemaphoreType.DMA],
    )(pg)


def _add_halves(pg, l1, c_arr):
    n, _, r, lanes = pg.shape
    tr = _row_tile16(r)

    def body(c_ref, a_ref, b_ref, o_ref):
        o_ref[...] = (a_ref[...] + b_ref[...]).astype(o_ref.dtype)

    grid_spec = pltpu.PrefetchScalarGridSpec(
        num_scalar_prefetch=1, grid=(n, HALF, r // tr),
        in_specs=[pl.BlockSpec((1, 1, tr, lanes), lambda k, l, i, c_ref: (k, c_ref[0] * HALF + l, i, 0)),
                  pl.BlockSpec((1, 1, tr, lanes), lambda k, l, i, c_ref: (k, l, i, 0))],
        out_specs=pl.BlockSpec((1, 1, tr, lanes), lambda k, l, i, c_ref: (k, l, i, 0)))
    return pl.pallas_call(
        body, name="add_halves", grid_spec=grid_spec, out_shape=jax.ShapeDtypeStruct((n, HALF, r, lanes), BF16),
        compiler_params=_cparams(("parallel", "parallel", "parallel")),
    )(c_arr, pg, l1)


def _scatter_chips(qb):
    def body(q_ref, l2_ref, send_sems, recv_sems, local_sem):
        x, y, c, chips = _place()
        me = 2 * x + y
        loc = pltpu.make_async_copy(q_ref.at[me], l2_ref.at[me], local_sem)
        loc.start()
        sends = []
        for r, (px, py) in enumerate(chips):
            peer = 2 * px + py
            sends.append(pltpu.make_async_remote_copy(src_ref=q_ref.at[peer], dst_ref=l2_ref.at[me], send_sem=send_sems.at[r],
                                                      recv_sem=recv_sems.at[r], device_id=(px, py, c), device_id_type=MESH))
        for cp in sends:
            cp.start()
        for r, (px, py) in enumerate(chips):
            peer = 2 * px + py
            pltpu.make_async_remote_copy(src_ref=q_ref.at[peer], dst_ref=l2_ref.at[peer], send_sem=send_sems.at[r],
                                         recv_sem=recv_sems.at[r], device_id=(px, py, c), device_id_type=MESH).wait_recv()
        for cp in sends:
            cp.wait_send()
        loc.wait()

    return pl.pallas_call(
        body, name="scatter_chips", in_specs=[ANY], out_specs=ANY, out_shape=jax.ShapeDtypeStruct(qb.shape, qb.dtype),
        scratch_shapes=[pltpu.SemaphoreType.DMA((3,)), pltpu.SemaphoreType.DMA((3,)), pltpu.SemaphoreType.DMA],
    )(qb)


def _sum_chips(l2):
    n, _, r, lanes = l2.shape
    tr = _row_tile16(r)

    def body(a_ref, o_ref):
        acc = a_ref[0].astype(F32)
        for k in range(1, n):
            acc = acc + a_ref[k].astype(F32)
        o_ref[...] = acc

    return pl.pallas_call(
        body, name="sum_chips", grid=(HALF, r // tr),
        in_specs=[pl.BlockSpec((n, 1, tr, lanes), lambda l, i: (0, l, i, 0))],
        out_specs=pl.BlockSpec((1, tr, lanes), lambda l, i: (l, i, 0)),
        out_shape=jax.ShapeDtypeStruct((HALF, r, lanes), F32),
        compiler_params=_cparams(("parallel", "parallel")),
    )(l2)


def _join_halves(rs):
    def body(rs_ref, g_ref, send_sem, recv_sem, local_sem):
        x, y, c, _ = _place()
        mine = pl.ds(c * HALF, HALF)
        theirs = pl.ds((1 - c) * HALF, HALF)
        loc = pltpu.make_async_copy(rs_ref, g_ref.at[mine], local_sem)
        loc.start()
        cp = pltpu.make_async_remote_copy(src_ref=rs_ref, dst_ref=g_ref.at[mine], send_sem=send_sem, recv_sem=recv_sem,
                                          device_id=(x, y, 1 - c), device_id_type=MESH)
        cp.start()
        pltpu.make_async_remote_copy(src_ref=rs_ref, dst_ref=g_ref.at[theirs], send_sem=send_sem, recv_sem=recv_sem,
                                     device_id=(x, y, 1 - c), device_id_type=MESH).wait_recv()
        cp.wait_send()
        loc.wait()

    h, r, lanes = rs.shape
    return pl.pallas_call(
        body, name="join_halves", in_specs=[ANY], out_specs=ANY, out_shape=jax.ShapeDtypeStruct((2 * h, r, lanes), rs.dtype),
        scratch_shapes=[pltpu.SemaphoreType.DMA, pltpu.SemaphoreType.DMA, pltpu.SemaphoreType.DMA],
    )(rs)


def _gather_chips(v):
    def body(v_ref, g_ref, send_sems, recv_sems, local_sem):
        x, y, c, chips = _place()
        me = 2 * x + y
        loc = pltpu.make_async_copy(v_ref, g_ref.at[me], local_sem)
        loc.start()
        sends = [pltpu.make_async_remote_copy(src_ref=v_ref, dst_ref=g_ref.at[me], send_sem=send_sems.at[r], recv_sem=recv_sems.at[r],
                                              device_id=(px, py, c), device_id_type=MESH) for r, (px, py) in enumerate(chips)]
        for cp in sends:
            cp.start()
        for r, (px, py) in enumerate(chips):
            pltpu.make_async_remote_copy(src_ref=v_ref, dst_ref=g_ref.at[2 * px + py], send_sem=send_sems.at[r],
                                         recv_sem=recv_sems.at[r], device_id=(px, py, c), device_id_type=MESH).wait_recv()
        for cp in sends:
            cp.wait_send()
        loc.wait()

    return pl.pallas_call(
        body, name="gather_chips", in_specs=[ANY], out_specs=ANY, out_shape=jax.ShapeDtypeStruct((N_CHIPS,) + v.shape, v.dtype),
        scratch_shapes=[pltpu.SemaphoreType.DMA((3,)), pltpu.SemaphoreType.DMA((3,)), pltpu.SemaphoreType.DMA],
    )(v)


BIG = (("w_in", 2), ("w_out", 1), ("ffn_wg", 2), ("ffn_wu", 2), ("ffn_wd", 1), ("ple_wg", 1), ("ple_wp", 2))
SMALL = (("dn_conv_w", 2), ("lru_conv_w", 2), ("lru_ba", 2), ("lru_bx", 2), ("lru_lambda", 2), ("ffn_conv_w", 2))
REPL = ("norm1_g", "dn_a_log", "dn_dt_bias", "dn_norm_g", "lru_conv_b", "lru_wa", "lru_wx", "lru_norm_g", "norm2_g",
        "ffn_conv_b", "ple_norm_g", "ple_bg", "final_g")
WEIGHTS = ("norm1_g", "w_in", "dn_conv_w", "dn_a_log", "dn_dt_bias", "dn_norm_g", "lru_conv_w", "lru_conv_b", "lru_wa", "lru_ba",
           "lru_wx", "lru_bx", "lru_lambda", "lru_norm_g", "w_out", "norm2_g", "ffn_wg", "ffn_wu", "ffn_conv_w", "ffn_conv_b",
           "ffn_wd", "ple_norm_g", "ple_wg", "ple_bg", "ple_wp", "final_g")
ROW_ALIGN = 8


def _rows_for(n_elems):
    rows = -(-n_elems // LANES)
    return -(-rows // ROW_ALIGN) * ROW_ALIGN


def _pack_shards(shards, names, dtype):
    nl = shards[names[0][0]].shape[0]
    flat = jnp.concatenate([shards[n].reshape(nl, -1).astype(dtype) for n, _ in names], axis=1)
    rows = _rows_for(flat.shape[1])
    flat = jnp.pad(flat, ((0, 0), (0, rows * LANES - flat.shape[1])))
    return flat.reshape(nl, rows, LANES)


def _unpack_gathered(g, shards, names):
    nl = g.shape[1]
    flat = g.reshape(N_CHIPS, nl, -1)
    out, off = {}, 0
    for n, axis in names:
        _, a, b = shards[n].shape
        piece = flat[:, :, off:off + a * b].reshape(N_CHIPS, nl, a, b)
        off += a * b
        if axis == 2:
            out[n] = jnp.transpose(piece, (1, 2, 0, 3)).reshape(nl, a, N_CHIPS * b)
        else:
            out[n] = jnp.transpose(piece, (1, 0, 2, 3)).reshape(nl, N_CHIPS * a, b)
    return out


def _pack_grads(grads, shard_shapes):
    nl = grads["w_in"].shape[0]

    def per_chip(names):
        cols = []
        for n, axis in names:
            _, a, b = shard_shapes[n]
            gfull = grads[n]
            if axis == 2:
                piece = jnp.transpose(gfull.reshape(nl, a, N_CHIPS, b), (2, 0, 1, 3))
            else:
                piece = jnp.transpose(gfull.reshape(nl, N_CHIPS, a, b), (1, 0, 2, 3))
            cols.append(piece.reshape(N_CHIPS, nl, a * b))
        flat = jnp.concatenate(cols, axis=2)
        rows = _rows_for(flat.shape[2])
        return jnp.pad(flat, ((0, 0), (0, 0), (0, rows * LANES - flat.shape[2]))), rows

    big, rb = per_chip(BIG)
    small, rs = per_chip(SMALL)
    rep = jnp.concatenate([grads[n].reshape(-1) for n in REPL])
    rr = _rows_for(-(-rep.shape[0] // (N_CHIPS * nl)))
    rep = jnp.pad(rep, (0, N_CHIPS * nl * rr * LANES - rep.shape[0])).reshape(N_CHIPS, nl, rr * LANES)
    pack = jnp.concatenate([big, small, rep], axis=2)
    return pack.reshape(N_CHIPS, nl, rb + rs + rr, LANES), (rb, rs, rr)


def _unpack_reduced(g, rep_all, rows, shard_shapes, repl_shapes):
    rb, rs, rr = rows
    nl = g.shape[0]
    out = {}
    for names, lo, hi in ((BIG, 0, rb), (SMALL, rb, rb + rs)):
        flat = g[:, lo:hi].reshape(nl, -1)
        off = 0
        for n, _ in names:
            _, a, b = shard_shapes[n]
            out[n] = flat[:, off:off + a * b].reshape(nl, a, b)
            off += a * b
    flat = rep_all.reshape(-1)
    off = 0
    for n in REPL:
        size = math.prod(repl_shapes[n])
        out[n] = flat[off:off + size].reshape(repl_shapes[n])
        off += size
    return out


def _as2d(a):
    if a.ndim == 1:
        return a.reshape(1, -1)
    return a.reshape(-1, a.shape[-1])


def kernel(x, p, norm1_g, w_in, dn_conv_w, dn_a_log, dn_dt_bias, dn_norm_g, lru_conv_w, lru_conv_b, lru_wa, lru_ba, lru_wx, lru_bx, lru_lambda, lru_norm_g, w_out, norm2_g, ffn_wg, ffn_wu, ffn_conv_w, ffn_conv_b, ffn_wd, ple_norm_g, ple_wg, ple_bg, ple_wp, final_g, loss_target, m_norm1_g, m_w_in, m_dn_conv_w, m_dn_a_log, m_dn_dt_bias, m_dn_norm_g, m_lru_conv_w, m_lru_conv_b, m_lru_wa, m_lru_ba, m_lru_wx, m_lru_bx, m_lru_lambda, m_lru_norm_g, m_w_out, m_norm2_g, m_ffn_wg, m_ffn_wu, m_ffn_conv_w, m_ffn_conv_b, m_ffn_wd, m_ple_norm_g, m_ple_wg, m_ple_bg, m_ple_wp, m_final_g, v_norm1_g, v_w_in, v_dn_conv_w, v_dn_a_log, v_dn_dt_bias, v_dn_norm_g, v_lru_conv_w, v_lru_conv_b, v_lru_wa, v_lru_ba, v_lru_wx, v_lru_bx, v_lru_lambda, v_lru_norm_g, v_w_out, v_norm2_g, v_ffn_wg, v_ffn_wu, v_ffn_conv_w, v_ffn_conv_b, v_ffn_wd, v_ple_norm_g, v_ple_wg, v_ple_bg, v_ple_wp, v_final_g):
    w = dict(norm1_g=norm1_g, w_in=w_in, dn_conv_w=dn_conv_w, dn_a_log=dn_a_log, dn_dt_bias=dn_dt_bias, dn_norm_g=dn_norm_g,
             lru_conv_w=lru_conv_w, lru_conv_b=lru_conv_b, lru_wa=lru_wa, lru_ba=lru_ba, lru_wx=lru_wx, lru_bx=lru_bx,
             lru_lambda=lru_lambda, lru_norm_g=lru_norm_g, w_out=w_out, norm2_g=norm2_g, ffn_wg=ffn_wg, ffn_wu=ffn_wu,
             ffn_conv_w=ffn_conv_w, ffn_conv_b=ffn_conv_b, ffn_wd=ffn_wd, ple_norm_g=ple_norm_g, ple_wg=ple_wg, ple_bg=ple_bg,
             ple_wp=ple_wp, final_g=final_g)
    m = dict(norm1_g=m_norm1_g, w_in=m_w_in, dn_conv_w=m_dn_conv_w, dn_a_log=m_dn_a_log, dn_dt_bias=m_dn_dt_bias,
             dn_norm_g=m_dn_norm_g, lru_conv_w=m_lru_conv_w, lru_conv_b=m_lru_conv_b, lru_wa=m_lru_wa, lru_ba=m_lru_ba,
             lru_wx=m_lru_wx, lru_bx=m_lru_bx, lru_lambda=m_lru_lambda, lru_norm_g=m_lru_norm_g, w_out=m_w_out, norm2_g=m_norm2_g,
             ffn_wg=m_ffn_wg, ffn_wu=m_ffn_wu, ffn_conv_w=m_ffn_conv_w, ffn_conv_b=m_ffn_conv_b, ffn_wd=m_ffn_wd,
             ple_norm_g=m_ple_norm_g, ple_wg=m_ple_wg, ple_bg=m_ple_bg, ple_wp=m_ple_wp, final_g=m_final_g)
    v = dict(norm1_g=v_norm1_g, w_in=v_w_in, dn_conv_w=v_dn_conv_w, dn_a_log=v_dn_a_log, dn_dt_bias=v_dn_dt_bias,
             dn_norm_g=v_dn_norm_g, lru_conv_w=v_lru_conv_w, lru_conv_b=v_lru_conv_b, lru_wa=v_lru_wa, lru_ba=v_lru_ba,
             lru_wx=v_lru_wx, lru_bx=v_lru_bx, lru_lambda=v_lru_lambda, lru_norm_g=v_lru_norm_g, w_out=v_w_out, norm2_g=v_norm2_g,
             ffn_wg=v_ffn_wg, ffn_wu=v_ffn_wu, ffn_conv_w=v_ffn_conv_w, ffn_conv_b=v_ffn_conv_b, ffn_wd=v_ffn_wd,
             ple_norm_g=v_ple_norm_g, ple_wg=v_ple_wg, ple_bg=v_ple_bg, ple_wp=v_ple_wp, final_g=v_final_g)

    gb, gs = _gather_weights(_pack_shards(w, BIG, BF16), _pack_shards(w, SMALL, F32))
    full = {n: w[n] for n in REPL}
    full.update(_unpack_gathered(gb, w, BIG))
    full.update(_unpack_gathered(gs, w, SMALL))

    loss_local, grad_x, grads = _local_step(x, p, loss_target, full)
    loss = lax.psum(loss_local, ("x", "y", "c"))

    shard_shapes = {n: w[n].shape for n, _ in BIG + SMALL}
    pack, rows = _pack_grads(grads, shard_shapes)
    c_arr = lax.axis_index("c").astype(jnp.int32).reshape(1)
    qb = _add_halves(pack, _swap_halves(pack), c_arr)
    reduced = _join_halves(_sum_chips(_scatter_chips(qb)))
    rep_all = _gather_chips(reduced[:, rows[0] + rows[1]:])
    g = _unpack_reduced(reduced, rep_all, rows, shard_shapes, {n: w[n].shape for n in REPL})

    deltas, new_m, new_v = {}, {}, {}
    for n in WEIGHTS:
        d2, m2, v2 = _adamw(_as2d(w[n]), _as2d(g[n]), _as2d(m[n]), _as2d(v[n]), "adamw_" + n)
        deltas[n], new_m[n], new_v[n] = d2.reshape(w[n].shape), m2.reshape(w[n].shape), v2.reshape(w[n].shape)
    return (loss, grad_x, *[g[n] for n in WEIGHTS], *[deltas[n] for n in WEIGHTS], *[new_m[n] for n in WEIGHTS],
            *[new_v[n] for n in WEIGHTS])
```

```python
import functools
import math

import jax
import jax.numpy as jnp
from jax import lax
from jax.experimental import pallas as pl
from jax.experimental.pallas import tpu as pltpu

F32 = jnp.float32
BF16 = jnp.bfloat16

D_MODEL = 1024
DN_HEADS = 4
DN_HEAD_DIM = 128
DN_WIDTH = 512
LRU_WIDTH = 512
LRU_C = 8.0
CHUNK = 64
EPS = 1e-6
P_LX, P_LG, P_Q, P_K, P_V, P_Z, P_GATE, P_COLS = 0, 512, 1024, 1536, 2048, 2560, 3072, 3200
N_GATE = 16
LANES = 128
VMEM_LIMIT = 56 * 1024 * 1024
MM_A_BLOCK_BYTES = 6 * 1024 * 1024
MM_B_BLOCK_BYTES = 4 * 1024 * 1024

ADAM_LR, ADAM_B1, ADAM_B2, ADAM_EPS, ADAM_WD, ADAM_STEP = 0.001, 0.9, 0.999, 1e-08, 0.01, 10


def _cparams(sem):
    return pltpu.CompilerParams(dimension_semantics=sem, vmem_limit_bytes=VMEM_LIMIT)


def _row_tile(m, want=512):
    t = min(want, m)
    while m % t:
        t //= 2
    return t


def _col_tile(n, cap=1536):
    best = None
    for d in range(LANES, min(n, cap) + 1, LANES):
        if n % d == 0:
            best = d
    return best if best is not None else n


def _bdot(a, b):
    return jnp.dot(a.astype(BF16), b.astype(BF16), preferred_element_type=F32)


def _bdot_nt(a, b):
    return lax.dot_general(a.astype(BF16), b.astype(BF16), (((1,), (1,)), ((), ())), preferred_element_type=F32)


def _bdot_tn(a, b):
    return lax.dot_general(a.astype(BF16), b.astype(BF16), (((0,), (0,)), ((), ())), preferred_element_type=F32)


def _rms(x, g):
    return x * lax.rsqrt(jnp.mean(x * x, axis=-1, keepdims=True) + EPS) * g


def _gelu(x):
    return 0.5 * x * (1.0 + jnp.tanh(0.7978845608028654 * (x + 0.044715 * x * x * x)))


def _sigmoid(x):
    return 1.0 / (1.0 + jnp.exp(-x))


def _silu(x):
    return x * _sigmoid(x)


def _softplus(x):
    return jnp.maximum(x, 0.0) + jnp.log(1.0 + jnp.exp(-jnp.abs(x)))


def _rmsnorm_fwd(x, g, name):
    t, d = x.shape
    tm = _row_tile(t)

    def body(x_ref, g_ref, o_ref):
        o_ref[...] = _rms(x_ref[...], g_ref[...]).astype(o_ref.dtype)

    return pl.pallas_call(
        body, name=name, grid=(t // tm,),
        in_specs=[pl.BlockSpec((tm, d), lambda i: (i, 0)), pl.BlockSpec((1, d), lambda i: (0, 0))],
        out_specs=pl.BlockSpec((tm, d), lambda i: (i, 0)),
        out_shape=jax.ShapeDtypeStruct((t, d), BF16),
        compiler_params=_cparams(("parallel",)),
    )(x, g)


def _rmsnorm_bwd(x, g, dh, dres, name):
    t, d = x.shape
    tm = _row_tile(t)

    def body(x_ref, g_ref, dh_ref, dres_ref, dx_ref, dg_ref):
        _, vjp = jax.vjp(_rms, x_ref[...], g_ref[...])
        dx, dg = vjp(dh_ref[...])
        dx_ref[...] = dx + dres_ref[...]

        @pl.when(pl.program_id(0) == 0)
        def _():
            dg_ref[...] = jnp.zeros_like(dg_ref)

        dg_ref[...] += dg

    return pl.pallas_call(
        body, name=name, grid=(t // tm,),
        in_specs=[pl.BlockSpec((tm, d), lambda i: (i, 0)), pl.BlockSpec((1, d), lambda i: (0, 0)),
                  pl.BlockSpec((tm, d), lambda i: (i, 0)), pl.BlockSpec((tm, d), lambda i: (i, 0))],
        out_specs=[pl.BlockSpec((tm, d), lambda i: (i, 0)), pl.BlockSpec((1, d), lambda i: (0, 0))],
        out_shape=[jax.ShapeDtypeStruct((t, d), F32), jax.ShapeDtypeStruct((1, d), F32)],
        compiler_params=_cparams(("arbitrary",)),
    )(x, g, dh, dres)


def _matmul(a_list, b_list, name, res=None, bias=None):
    m = a_list[0].shape[0]
    n = b_list[0].shape[1]
    a_row_bytes = sum(a.shape[1] * a.dtype.itemsize for a in a_list)
    tm = _row_tile(m, max(256, min(1024, MM_A_BLOCK_BYTES // a_row_bytes // 256 * 256)))
    k_total = sum(b.shape[0] for b in b_list)
    tn = _col_tile(n, max(LANES, min(1024, MM_B_BLOCK_BYTES // (2 * k_total) // LANES * LANES)))
    na = len(a_list)

    def body(*refs):
        a_refs, b_refs = refs[:na], refs[na:2 * na]
        rest = list(refs[2 * na:])
        o_ref = rest.pop()
        acc = _bdot(a_refs[0][...], b_refs[0][...])
        for a_ref, b_ref in zip(a_refs[1:], b_refs[1:]):
            acc = acc + _bdot(a_ref[...], b_ref[...])
        if res is not None:
            acc = acc + rest.pop(0)[...]
        if bias is not None:
            acc = acc + rest.pop(0)[...]
        o_ref[...] = acc

    in_specs = [pl.BlockSpec((tm, a.shape[1]), lambda i, j: (i, 0)) for a in a_list]
    in_specs += [pl.BlockSpec((b.shape[0], tn), lambda i, j: (0, j)) for b in b_list]
    args = list(a_list) + list(b_list)
    if res is not None:
        in_specs.append(pl.BlockSpec((tm, tn), lambda i, j: (i, j)))
        args.append(res)
    if bias is not None:
        in_specs.append(pl.BlockSpec((1, tn), lambda i, j: (0, j)))
        args.append(bias)
    return pl.pallas_call(
        body, name=name, grid=(m // tm, n // tn), in_specs=in_specs,
        out_specs=pl.BlockSpec((tm, tn), lambda i, j: (i, j)),
        out_shape=jax.ShapeDtypeStruct((m, n), F32),
        compiler_params=_cparams(("parallel", "parallel")),
    )(*args)


def _matmul_tn(a, b, name):
    m, k = a.shape
    n = b.shape[1]
    tm = _row_tile(m, 512)
    tn = _col_tile(n, 512)

    def body(a_ref, b_ref, o_ref):
        @pl.when(pl.program_id(1) == 0)
        def _():
            o_ref[...] = jnp.zeros_like(o_ref)

        o_ref[...] += _bdot_tn(a_ref[...], b_ref[...])

    return pl.pallas_call(
        body, name=name, grid=(n // tn, m // tm),
        in_specs=[pl.BlockSpec((tm, k), lambda j, i: (i, 0)), pl.BlockSpec((tm, tn), lambda j, i: (i, j))],
        out_specs=pl.BlockSpec((k, tn), lambda j, i: (0, j)),
        out_shape=jax.ShapeDtypeStruct((k, n), F32),
        compiler_params=_cparams(("parallel", "arbitrary")),
    )(a, b)


def _shift_down(x, k):
    row = lax.broadcasted_iota(jnp.int32, x.shape, 0)
    return jnp.where(row >= k, pltpu.roll(x, k, 0), 0.0)


def _shift_up(x, k):
    s = x.shape[0]
    row = lax.broadcasted_iota(jnp.int32, x.shape, 0)
    return jnp.where(row < s - k, pltpu.roll(x, s - k, 0), 0.0)


def _conv_taps(x, ntaps, left):
    out = []
    for j in range(ntaps):
        off = j - left
        out.append(_shift_down(x, -off) if off < 0 else (_shift_up(x, off) if off > 0 else x))
    return out


def _conv_fwd(x, w, left):
    taps = _conv_taps(x, w.shape[0], left)
    acc = taps[0] * w[0:1, :]
    for j in range(1, w.shape[0]):
        acc = acc + taps[j] * w[j:j + 1, :]
    return acc


def _conv_bwd(x, w, left, dout):
    ntaps = w.shape[0]
    dx = None
    for j in range(ntaps):
        off = j - left
        sh = _shift_up(dout, -off) if off < 0 else (_shift_down(dout, off) if off > 0 else dout)
        term = sh * w[j:j + 1, :]
        dx = term if dx is None else dx + term
    taps = _conv_taps(x, ntaps, left)
    dw = jnp.concatenate([jnp.sum(dout * tp, axis=0, keepdims=True) for tp in taps], axis=0)
    return dx, dw


def _scan(a, b, reverse):
    s = a.shape[0]
    d = 1
    sh = _shift_up if reverse else _shift_down
    while d < s:
        b = a * sh(b, d) + b
        a = a * sh(a, d)
        d *= 2
    return b


def _dn_gates_fn(pre, alog, dtb):
    lane = lax.broadcasted_iota(jnp.int32, pre.shape, 1)
    beta = _sigmoid(pre)
    g = -jnp.exp(alog) * _softplus(pre + dtb)
    return jnp.where(lane < N_GATE // 2, beta, jnp.where(lane < N_GATE, g, 0.0))


def _dn_gates_fwd(proj, alog, dtb, name):
    t = proj.shape[0]
    tm = _row_tile(t)
    cb = P_GATE // LANES

    def body(p_ref, a_ref, d_ref, o_ref):
        o_ref[...] = _dn_gates_fn(p_ref[...], a_ref[...], d_ref[...])

    return pl.pallas_call(
        body, name=name, grid=(t // tm,),
        in_specs=[pl.BlockSpec((tm, LANES), lambda i: (i, cb)), pl.BlockSpec((1, LANES), lambda i: (0, 0)),
                  pl.BlockSpec((1, LANES), lambda i: (0, 0))],
        out_specs=pl.BlockSpec((tm, LANES), lambda i: (i, 0)),
        out_shape=jax.ShapeDtypeStruct((t, LANES), F32),
        compiler_params=_cparams(("parallel",)),
    )(proj, alog, dtb)


def _dn_gates_bwd(proj, alog, dtb, dgb_f, dgb_b, name):
    t = proj.shape[0]
    tm = _row_tile(t)
    cb = P_GATE // LANES

    def body(p_ref, a_ref, d_ref, g1_ref, g2_ref, dp_ref, da_ref, dd_ref):
        _, vjp = jax.vjp(_dn_gates_fn, p_ref[...], a_ref[...], d_ref[...])
        dp, da, dd = vjp(g1_ref[...] + g2_ref[...])
        dp_ref[...] = dp

        @pl.when(pl.program_id(0) == 0)
        def _():
            da_ref[...] = jnp.zeros_like(da_ref)
            dd_ref[...] = jnp.zeros_like(dd_ref)

        da_ref[...] += da
        dd_ref[...] += dd

    row = pl.BlockSpec((tm, LANES), lambda i: (i, 0))
    vec = pl.BlockSpec((1, LANES), lambda i: (0, 0))
    return pl.pallas_call(
        body, name=name, grid=(t // tm,),
        in_specs=[pl.BlockSpec((tm, LANES), lambda i: (i, cb)), vec, vec, row, row],
        out_specs=[row, vec, vec],
        out_shape=[jax.ShapeDtypeStruct((t, LANES), F32), jax.ShapeDtypeStruct((1, LANES), F32),
                   jax.ShapeDtypeStruct((1, LANES), F32)],
        compiler_params=_cparams(("arbitrary",)),
    )(proj, alog, dtb, dgb_f, dgb_b)


def _dn_prep_fn(x, w, is_qk):
    act = _silu(_conv_fwd(x, w, 2))
    nrm = act * lax.rsqrt(jnp.sum(act * act, axis=-1, keepdims=True) + EPS)
    return jnp.where(is_qk, nrm, act)


def _dn_prep_fwd(proj3, conv_w, name):
    bsz, s, _ = proj3.shape
    nblk = 3 * DN_WIDTH // LANES
    cb = P_Q // LANES

    def body(x_ref, w_ref, o_ref):
        o_ref[0] = _dn_prep_fn(x_ref[0], w_ref[...], pl.program_id(1) < 2 * DN_HEADS)

    return pl.pallas_call(
        body, name=name, grid=(bsz, nblk),
        in_specs=[pl.BlockSpec((1, s, LANES), lambda b, j: (b, 0, cb + j)), pl.BlockSpec((4, LANES), lambda b, j: (0, j))],
        out_specs=pl.BlockSpec((1, s, LANES), lambda b, j: (b, 0, j)),
        out_shape=jax.ShapeDtypeStruct((bsz, s, 3 * DN_WIDTH), F32),
        compiler_params=_cparams(("parallel", "parallel")),
    )(proj3, conv_w)


def _dn_prep_bwd(proj3, conv_w, dqkv, name):
    bsz, s, _ = proj3.shape
    nblk = 3 * DN_WIDTH // LANES
    cb = P_Q // LANES

    def body(x_ref, w_ref, d_ref, dx_ref, dw_ref):
        x, w, d = x_ref[0], w_ref[...], d_ref[0]
        is_qk = pl.program_id(0) < 2 * DN_HEADS
        pre = _conv_fwd(x, w, 2)

        def post(pre):
            act = _silu(pre)
            nrm = act * lax.rsqrt(jnp.sum(act * act, axis=-1, keepdims=True) + EPS)
            return jnp.where(is_qk, nrm, act)

        _, vjp = jax.vjp(post, pre)
        (dpre,) = vjp(d)
        dx, dw = _conv_bwd(x, w, 2, dpre)
        dx_ref[0] = dx

        @pl.when(pl.program_id(1) == 0)
        def _():
            dw_ref[...] = jnp.zeros_like(dw_ref)

        dw_ref[...] += dw

    return pl.pallas_call(
        body, name=name, grid=(nblk, bsz),
        in_specs=[pl.BlockSpec((1, s, LANES), lambda j, b: (b, 0, cb + j)), pl.BlockSpec((4, LANES), lambda j, b: (0, j)),
                  pl.BlockSpec((1, s, LANES), lambda j, b: (b, 0, j))],
        out_specs=[pl.BlockSpec((1, s, LANES), lambda j, b: (b, 0, j)), pl.BlockSpec((4, LANES), lambda j, b: (0, j))],
        out_shape=[jax.ShapeDtypeStruct((bsz, s, 3 * DN_WIDTH), F32), jax.ShapeDtypeStruct((4, 3 * DN_WIDTH), F32)],
        compiler_params=_cparams(("parallel", "arbitrary")),
    )(proj3, conv_w, dqkv)


def _parts(x, n):
    out = []
    for _ in range(n):
        bits = lax.bitcast_convert_type(x, jnp.uint32) & jnp.uint32(0xFFFF0000)
        t = lax.bitcast_convert_type(bits, F32)
        out.append(t.astype(BF16))
        x = x - t
    return out


def _dg(x, y, cx, cy):
    return lax.dot_general(x, y, (((cx + 1,), (cy + 1,)), ((0,), (0,))), preferred_element_type=F32)


def _bmm(a, b):
    return _dg(a.astype(BF16), b.astype(BF16), 1, 0)


def _bmm_nt(a, b):
    return _dg(a.astype(BF16), b.astype(BF16), 1, 1)


def _bmm_tn(a, b):
    return _dg(a.astype(BF16), b.astype(BF16), 0, 0)


def _dot3_raw(a, b, ca, cb):
    a_hi, a_lo = _parts(a, 2)
    b_hi, b_lo = _parts(b, 2)
    return _dg(a_hi, b_hi, ca, cb) + (_dg(a_hi, b_lo, ca, cb) + _dg(a_lo, b_hi, ca, cb))


@jax.custom_vjp
def _dot3(a, b):
    return _dot3_raw(a, b, 1, 0)


def _dot3_fwd(a, b):
    return _dot3_raw(a, b, 1, 0), (a, b)


def _dot3_bwd(res, ct):
    a, b = res
    return _dot3_raw(ct, b, 1, 1), _dot3_raw(a, ct, 0, 0)


_dot3.defvjp(_dot3_fwd, _dot3_bwd)


@jax.custom_vjp
def _sum_left(m, x):
    return sum(_dg(m, pt, 1, 0) for pt in _parts(x, 3))


def _sum_left_fwd(m, x):
    return _sum_left(m, x), m


def _sum_left_bwd(m, ct):
    return jnp.zeros_like(m), sum(_dg(m, pt, 0, 0) for pt in _parts(ct, 2))


_sum_left.defvjp(_sum_left_fwd, _sum_left_bwd)


@jax.custom_vjp
def _sum_right(x, m):
    return sum(_dg(pt, m, 0, 0) for pt in _parts(x, 3))


def _sum_right_fwd(x, m):
    return _sum_right(x, m), m


def _sum_right_bwd(m, ct):
    return sum(_dg(m, pt, 1, 1) for pt in _parts(ct, 2)), jnp.zeros_like(m)


_sum_right.defvjp(_sum_right_fwd, _sum_right_bwd)


def _dn_intra(q, k, v, g, beta, rev):
    nu, c, _ = q.shape
    row = lax.broadcasted_iota(jnp.int32, (nu, c, c), 1)
    col = lax.broadcasted_iota(jnp.int32, (nu, c, c), 2)
    incl = (row <= col) if rev else (row >= col)
    strict = (row < col) if rev else (row > col)
    ones_incl = jnp.where(incl, 1.0, 0.0).astype(BF16)
    ones_tr = jnp.where((row >= col) if rev else (row <= col), 1.0, 0.0).astype(BF16)
    gbc = jnp.broadcast_to(g, (nu, c, c))
    gc = _sum_left(ones_incl, gbc)
    gr = _sum_right(gbc, ones_tr)
    gcum = gc[:, :, 0:1]
    decay = jnp.where(incl, jnp.exp(jnp.where(incl, gc - gr, 0.0)), 0.0)
    qs = q * (DN_HEAD_DIM ** -0.5)
    kb = k * beta
    a = jnp.where(strict, _bmm_nt(kb, k) * decay, 0.0)
    same = lambda n: (row // n) == (col // n)
    d = jnp.where(same(16), a, 0.0)
    tinv = jnp.where(row == col, 1.0, 0.0) - d
    pw = _dot3(d, d)
    tinv = tinv + _dot3(tinv, pw)
    pw = _dot3(pw, pw)
    tinv = tinv + _dot3(tinv, pw)
    pw = _dot3(pw, pw)
    tinv = tinv + _dot3(tinv, pw)
    n = 16
    while n < c:
        e = jnp.where(same(2 * n) & jnp.logical_not(same(n)), a, 0.0)
        tinv = tinv - _bmm(tinv, _bmm(e, tinv))
        n *= 2
    egc = jnp.exp(gcum)
    u = _bmm(tinv, v * beta)
    w = _bmm(tinv, kb * egc)
    attn = _bmm_nt(qs, k) * decay
    glast = gcum[:, 0:1, :] if rev else gcum[:, c - 1:c, :]
    q_dec = qs * egc
    k_dec = k * jnp.exp(glast - gcum)
    cdec = jnp.broadcast_to(jnp.exp(glast), (nu, 1, LANES))
    return u, w, q_dec, k_dec, attn, cdec


def _dn_rec(u, w, q_dec, k_dec, attn, cdec, state):
    v_new = u - _bmm(w, state)
    o = _bmm(q_dec, state) + _bmm(attn, v_new)
    return o, state * cdec + _bmm_tn(k_dec, v_new)


DN_INTRA_TOKENS = 256
DN_REC_TOKENS = 512


def _dn_gate_lanes(rev, h):
    lb = (DN_HEADS if rev else 0) + h
    return lb, N_GATE // 2 + lb


def _dn_intra_shapes(bsz, s):
    n = s // CHUNK
    return [jax.ShapeDtypeStruct((bsz, s, DN_WIDTH), F32), jax.ShapeDtypeStruct((bsz, s, DN_WIDTH), BF16),
            jax.ShapeDtypeStruct((bsz, s, DN_WIDTH), BF16), jax.ShapeDtypeStruct((bsz, s, DN_WIDTH), BF16),
            jax.ShapeDtypeStruct((bsz, n, DN_HEADS, CHUNK, CHUNK), BF16), jax.ShapeDtypeStruct((bsz, n, DN_HEADS, 1, LANES), F32)]


def _dn_intra_specs(tb, ix):
    nc = tb // CHUNK
    ix5 = lambda b, j: ix(b, j) + (0, 0)
    row = pl.BlockSpec((1, tb, DN_WIDTH), ix)
    return [row, row, row, row, pl.BlockSpec((1, nc, DN_HEADS, CHUNK, CHUNK), ix5), pl.BlockSpec((1, nc, DN_HEADS, 1, LANES), ix5)]


def _dn_units(nc):
    return [(ci, h) for ci in range(nc) for h in range(DN_HEADS)]


def _dn_load_units(qkv_ref, gb_ref, nc, rev):
    qs, ks, vs, gs, bs = [], [], [], [], []
    for ci, h in _dn_units(nc):
        rows = slice(ci * CHUNK, (ci + 1) * CHUNK)
        lb, lg = _dn_gate_lanes(rev, h)
        qs.append(qkv_ref[0, rows, h * LANES:(h + 1) * LANES])
        ks.append(qkv_ref[0, rows, DN_WIDTH + h * LANES:DN_WIDTH + (h + 1) * LANES])
        vs.append(qkv_ref[0, rows, 2 * DN_WIDTH + h * LANES:2 * DN_WIDTH + (h + 1) * LANES])
        gs.append(gb_ref[0, rows, lg:lg + 1])
        bs.append(gb_ref[0, rows, lb:lb + 1])
    return jnp.stack(qs), jnp.stack(ks), jnp.stack(vs), jnp.stack(gs), jnp.stack(bs)


def _dn_intra_fwd(qkv, gb, rev, name):
    bsz, s, _ = qkv.shape
    tb = min(DN_INTRA_TOKENS, s)
    nc = tb // CHUNK

    def body(qkv_ref, gb_ref, u_ref, w_ref, qd_ref, kd_ref, at_ref, cd_ref):
        q, k, v, g, beta = _dn_load_units(qkv_ref, gb_ref, nc, rev)
        u, w, qd, kd, at, cd = _dn_intra(q, k, v, g, beta, rev)
        for i, (ci, h) in enumerate(_dn_units(nc)):
            rows = slice(ci * CHUNK, (ci + 1) * CHUNK)
            cols = slice(h * LANES, (h + 1) * LANES)
            u_ref[0, rows, cols] = u[i]
            w_ref[0, rows, cols] = w[i].astype(BF16)
            qd_ref[0, rows, cols] = qd[i].astype(BF16)
            kd_ref[0, rows, cols] = kd[i].astype(BF16)
            at_ref[0, ci, h] = at[i].astype(BF16)
            cd_ref[0, ci, h] = cd[i]

    ix = lambda b, j: (b, j, 0)
    return pl.pallas_call(
        body, name=name, grid=(bsz, s // tb),
        in_specs=[pl.BlockSpec((1, tb, 3 * DN_WIDTH), ix), pl.BlockSpec((1, tb, LANES), ix)],
        out_specs=_dn_intra_specs(tb, ix), out_shape=_dn_intra_shapes(bsz, s),
        compiler_params=_cparams(("parallel", "parallel")),
    )(qkv, gb)


def _dn_intra_bwd(qkv, gb, cts, rev, name):
    bsz, s, _ = qkv.shape
    tb = min(DN_INTRA_TOKENS, s)
    nc = tb // CHUNK

    def body(qkv_ref, gb_ref, du_ref, dw_ref, dqd_ref, dkd_ref, dat_ref, dcd_ref, dqkv_ref, dgb_ref):
        units = _dn_units(nc)
        q, k, v, g, beta = _dn_load_units(qkv_ref, gb_ref, nc, rev)
        _, vjp = jax.vjp(functools.partial(_dn_intra, rev=rev), q, k, v, g, beta)
        tok = lambda ref: jnp.stack([ref[0, ci * CHUNK:(ci + 1) * CHUNK, h * LANES:(h + 1) * LANES] for ci, h in units])
        per = lambda ref: jnp.stack([ref[0, ci, h] for ci, h in units])
        dq, dk, dv, dg, dbeta = vjp((tok(du_ref), tok(dw_ref), tok(dqd_ref), tok(dkd_ref), per(dat_ref), per(dcd_ref)))
        lane = lax.broadcasted_iota(jnp.int32, (CHUNK, LANES), 1)
        for ci in range(nc):
            rows = slice(ci * CHUNK, (ci + 1) * CHUNK)
            dgates = jnp.zeros((CHUNK, LANES), F32)
            for h in range(DN_HEADS):
                i = units.index((ci, h))
                lb, lg = _dn_gate_lanes(rev, h)
                dqkv_ref[0, rows, h * LANES:(h + 1) * LANES] = dq[i]
                dqkv_ref[0, rows, DN_WIDTH + h * LANES:DN_WIDTH + (h + 1) * LANES] = dk[i]
                dqkv_ref[0, rows, 2 * DN_WIDTH + h * LANES:2 * DN_WIDTH + (h + 1) * LANES] = dv[i]
                dgates = dgates + jnp.where(lane == lb, dbeta[i], 0.0) + jnp.where(lane == lg, dg[i], 0.0)
            dgb_ref[0, rows, :] = dgates

    ix = lambda b, j: (b, j, 0)
    ix5 = lambda b, j: (b, j, 0, 0, 0)
    row = pl.BlockSpec((1, tb, DN_WIDTH), ix)
    return pl.pallas_call(
        body, name=name, grid=(bsz, s // tb),
        in_specs=[pl.BlockSpec((1, tb, 3 * DN_WIDTH), ix), pl.BlockSpec((1, tb, LANES), ix), row, row, row, row,
                  pl.BlockSpec((1, nc, DN_HEADS, CHUNK, CHUNK), ix5), pl.BlockSpec((1, nc, DN_HEADS, 1, LANES), ix5)],
        out_specs=[pl.BlockSpec((1, tb, 3 * DN_WIDTH), ix), pl.BlockSpec((1, tb, LANES), ix)],
        out_shape=[jax.ShapeDtypeStruct((bsz, s, 3 * DN_WIDTH), F32), jax.ShapeDtypeStruct((bsz, s, LANES), F32)],
        compiler_params=_cparams(("parallel", "parallel")),
    )(qkv, gb, *cts)


def _dn_rec_fwd(intra, rev, name):
    u = intra[0]
    bsz, s, _ = u.shape
    tb = min(DN_REC_TOKENS, s)
    nt = s // tb
    nc = tb // CHUNK

    def body(u_ref, w_ref, qd_ref, kd_ref, at_ref, cd_ref, o_ref, st_ref, state):
        @pl.when(pl.program_id(1) == 0)
        def _():
            state[...] = jnp.zeros_like(state)

        def step(ci, carry):
            cidx = (nc - 1 - ci) if rev else ci
            rows = pl.ds(pl.multiple_of(cidx * CHUNK, CHUNK), CHUNK)
            heads = lambda ref: jnp.stack([ref[0, rows, h * LANES:(h + 1) * LANES] for h in range(DN_HEADS)])
            st = state[...]
            o, new_state = _dn_rec(heads(u_ref), heads(w_ref), heads(qd_ref), heads(kd_ref), at_ref[0, cidx], cd_ref[0, cidx], st)
            st_ref[0, cidx] = st
            state[...] = new_state
            o_ref[0, rows, :] = jnp.concatenate([o[h] for h in range(DN_HEADS)], axis=-1)
            return carry

        lax.fori_loop(0, nc, step, 0)

    ix = (lambda b, j: (b, nt - 1 - j, 0)) if rev else (lambda b, j: (b, j, 0))
    ix5 = lambda b, j: ix(b, j) + (0, 0)
    return pl.pallas_call(
        body, name=name, grid=(bsz, nt), in_specs=_dn_intra_specs(tb, ix),
        out_specs=[pl.BlockSpec((1, tb, DN_WIDTH), ix), pl.BlockSpec((1, nc, DN_HEADS, DN_HEAD_DIM, DN_HEAD_DIM), ix5)],
        out_shape=[jax.ShapeDtypeStruct((bsz, s, DN_WIDTH), F32),
                   jax.ShapeDtypeStruct((bsz, s // CHUNK, DN_HEADS, DN_HEAD_DIM, DN_HEAD_DIM), F32)],
        scratch_shapes=[pltpu.VMEM((DN_HEADS, DN_HEAD_DIM, DN_HEAD_DIM), F32)],
        compiler_params=_cparams(("parallel", "arbitrary")),
    )(*intra)


def _dn_rec_bwd(intra, states, do, rev, name):
    u = intra[0]
    bsz, s, _ = u.shape
    tb = min(DN_REC_TOKENS, s)
    nt = s // tb
    nc = tb // CHUNK

    def body(u_ref, w_ref, qd_ref, kd_ref, at_ref, cd_ref, st_ref, do_ref,
             du_ref, dw_ref, dqd_ref, dkd_ref, dat_ref, dcd_ref, dstate):
        @pl.when(pl.program_id(1) == 0)
        def _():
            dstate[...] = jnp.zeros_like(dstate)

        def step(ci, carry):
            cidx = ci if rev else (nc - 1 - ci)
            rows = pl.ds(pl.multiple_of(cidx * CHUNK, CHUNK), CHUNK)
            heads = lambda ref: jnp.stack([ref[0, rows, h * LANES:(h + 1) * LANES] for h in range(DN_HEADS)])
            args = (heads(u_ref), heads(w_ref).astype(F32), heads(qd_ref).astype(F32), heads(kd_ref).astype(F32),
                    at_ref[0, cidx].astype(F32), cd_ref[0, cidx], st_ref[0, cidx])
            _, vjp = jax.vjp(_dn_rec, *args)
            du, dw, dqd, dkd, dat, dcd, dst = vjp((heads(do_ref), dstate[...]))
            dat_ref[0, cidx] = dat
            dcd_ref[0, cidx] = dcd
            dstate[...] = dst
            for ref, val in ((du_ref, du), (dw_ref, dw), (dqd_ref, dqd), (dkd_ref, dkd)):
                ref[0, rows, :] = jnp.concatenate([val[h] for h in range(DN_HEADS)], axis=-1)
            return carry

        lax.fori_loop(0, nc, step, 0)

    ix = (lambda b, j: (b, j, 0)) if rev else (lambda b, j: (b, nt - 1 - j, 0))
    ix5 = lambda b, j: ix(b, j) + (0, 0)
    row = pl.BlockSpec((1, tb, DN_WIDTH), ix)
    f32 = lambda sd: jax.ShapeDtypeStruct(sd.shape, F32)
    return pl.pallas_call(
        body, name=name, grid=(bsz, nt),
        in_specs=_dn_intra_specs(tb, ix) + [pl.BlockSpec((1, nc, DN_HEADS, DN_HEAD_DIM, DN_HEAD_DIM), ix5), row],
        out_specs=_dn_intra_specs(tb, ix), out_shape=[f32(sd) for sd in _dn_intra_shapes(bsz, s)],
        scratch_shapes=[pltpu.VMEM((DN_HEADS, DN_HEAD_DIM, DN_HEAD_DIM), F32)],
        compiler_params=_cparams(("parallel", "arbitrary")),
    )(*intra, states, do)


def _lru_gate_fn(xc, wa, ba, wx, bx, lam):
    r = _sigmoid(_bdot(xc, wa) + ba)
    ig = _sigmoid(_bdot(xc, wx) + bx)
    log_a = -LRU_C * r * _softplus(-lam)
    a = jnp.exp(log_a)
    b = jnp.sqrt(-jnp.tanh(log_a) * (a * a + 1.0)) * (ig * xc)
    return a, b


def _lru_fwd(proj3, conv_w, conv_b, wa, ba, wx, bx, lam, name):
    bsz, s, _ = proj3.shape
    nblk = LRU_WIDTH // LANES

    def body(x_ref, cw_ref, cb_ref, wa_ref, ba_ref, wx_ref, bx_ref, lam_ref, hf_ref, hb_ref):
        xc = _conv_fwd(x_ref[0], cw_ref[...], 2) + cb_ref[...]
        for d, h_ref in ((0, hf_ref), (1, hb_ref)):
            a, b = _lru_gate_fn(xc, wa_ref[d, 0], ba_ref[d:d + 1, :], wx_ref[d, 0], bx_ref[d:d + 1, :], lam_ref[d:d + 1, :])
            h_ref[0] = _scan(a, b, reverse=(d == 1))

    col = pl.BlockSpec((1, s, LANES), lambda b, j: (b, 0, j))
    vec2 = pl.BlockSpec((2, LANES), lambda b, j: (0, j))
    wspec = pl.BlockSpec((2, 1, LANES, LANES), lambda b, j: (0, j, 0, 0))
    return pl.pallas_call(
        body, name=name, grid=(bsz, nblk),
        in_specs=[col, pl.BlockSpec((4, LANES), lambda b, j: (0, j)), pl.BlockSpec((1, LANES), lambda b, j: (0, j)),
                  wspec, vec2, wspec, vec2, vec2],
        out_specs=[col, col],
        out_shape=[jax.ShapeDtypeStruct((bsz, s, LRU_WIDTH), F32)] * 2,
        compiler_params=_cparams(("parallel", "parallel")),
    )(proj3, conv_w, conv_b, wa, ba, wx, bx, lam)


def _lru_bwd(proj3, conv_w, conv_b, wa, ba, wx, bx, lam, hf, hb, dh, name):
    bsz, s, _ = proj3.shape
    nblk = LRU_WIDTH // LANES

    def body(x_ref, cw_ref, cb_ref, wa_ref, ba_ref, wx_ref, bx_ref, lam_ref, hf_ref, hb_ref, dh_ref,
             dx_ref, dcw_ref, dcb_ref, dwa_ref, dba_ref, dwx_ref, dbx_ref, dlam_ref):
        @pl.when(pl.program_id(1) == 0)
        def _():
            for r in (dcw_ref, dcb_ref, dwa_ref, dba_ref, dwx_ref, dbx_ref, dlam_ref):
                r[...] = jnp.zeros_like(r)

        x, cw = x_ref[0], cw_ref[...]
        xc = _conv_fwd(x, cw, 2) + cb_ref[...]
        dhv = dh_ref[0]
        dxc = jnp.zeros_like(xc)
        for d, h_ref in ((0, hf_ref), (1, hb_ref)):
            rev = d == 1
            args = (xc, wa_ref[d, 0].astype(F32), ba_ref[d:d + 1, :], wx_ref[d, 0].astype(F32), bx_ref[d:d + 1, :],
                    lam_ref[d:d + 1, :])
            (a, _), vjp = jax.vjp(_lru_gate_fn, *args)
            h = h_ref[0]
            a_next = _shift_down(a, 1) if rev else _shift_up(a, 1)
            lam_adj = _scan(a_next, dhv, reverse=not rev)
            h_prev = _shift_up(h, 1) if rev else _shift_down(h, 1)
            dxc_d, dwa, dba, dwx, dbx, dlam = vjp((lam_adj * h_prev, lam_adj))
            dxc = dxc + dxc_d
            dwa_ref[d, 0] += dwa
            dwx_ref[d, 0] += dwx
            dba_ref[d:d + 1, :] += dba
            dbx_ref[d:d + 1, :] += dbx
            dlam_ref[d:d + 1, :] += dlam
        dx, dcw = _conv_bwd(x, cw, 2, dxc)
        dx_ref[0] = dx
        dcw_ref[...] += dcw
        dcb_ref[...] += jnp.sum(dxc, axis=0, keepdims=True)

    col = pl.BlockSpec((1, s, LANES), lambda j, b: (b, 0, j))
    vec1 = pl.BlockSpec((1, LANES), lambda j, b: (0, j))
    vec2 = pl.BlockSpec((2, LANES), lambda j, b: (0, j))
    vec4 = pl.BlockSpec((4, LANES), lambda j, b: (0, j))
    wspec = pl.BlockSpec((2, 1, LANES, LANES), lambda j, b: (0, j, 0, 0))
    wshape = jax.ShapeDtypeStruct((2, nblk, LANES, LANES), F32)
    v2shape = jax.ShapeDtypeStruct((2, LRU_WIDTH), F32)
    return pl.pallas_call(
        body, name=name, grid=(nblk, bsz),
        in_specs=[col, vec4, vec1, wspec, vec2, wspec, vec2, vec2, col, col, col],
        out_specs=[col, vec4, vec1, wspec, vec2, wspec, vec2, vec2],
        out_shape=[jax.ShapeDtypeStruct((bsz, s, LRU_WIDTH), F32), jax.ShapeDtypeStruct((4, LRU_WIDTH), F32),
                   jax.ShapeDtypeStruct((1, LRU_WIDTH), F32), wshape, v2shape, wshape, v2shape, v2shape],
        compiler_params=_cparams(("parallel", "arbitrary")),
    )(proj3, conv_w, conv_b, wa, ba, wx, bx, lam, hf, hb, dh)


def _mix_fn(o_f, o_b, z, lg, hf, hb, dn_g, lru_g):
    osum = o_f + o_b
    heads = []
    for h in range(DN_HEADS):
        sl = slice(h * LANES, (h + 1) * LANES)
        heads.append(_rms(osum[:, sl], dn_g) * _silu(z[:, sl]))
    lru = _rms(_gelu(lg) * (hf + hb), lru_g)
    return jnp.concatenate(heads + [lru], axis=-1)


def _mix_specs(tm):
    w = DN_WIDTH
    row = pl.BlockSpec((tm, w), lambda i: (i, 0))
    z = pl.BlockSpec((tm, w), lambda i: (i, P_Z // w))
    lg = pl.BlockSpec((tm, w), lambda i: (i, P_LG // w))
    return [row, row, z, lg, row, row, pl.BlockSpec((1, LANES), lambda i: (0, 0)), pl.BlockSpec((1, w), lambda i: (0, 0))]


def _mix_fwd(o_f, o_b, proj, hf, hb, dn_g, lru_g, name):
    t = proj.shape[0]
    tm = _row_tile(t)

    def body(of_ref, ob_ref, z_ref, lg_ref, hf_ref, hb_ref, dg_ref, lgn_ref, o_ref):
        o_ref[...] = _mix_fn(of_ref[...], ob_ref[...], z_ref[...], lg_ref[...], hf_ref[...], hb_ref[...],
                             dg_ref[...], lgn_ref[...]).astype(o_ref.dtype)

    return pl.pallas_call(
        body, name=name, grid=(t // tm,), in_specs=_mix_specs(tm),
        out_specs=pl.BlockSpec((tm, D_MODEL), lambda i: (i, 0)),
        out_shape=jax.ShapeDtypeStruct((t, D_MODEL), BF16),
        compiler_params=_cparams(("parallel",)),
    )(o_f, o_b, proj, proj, hf, hb, dn_g, lru_g)


def _mix_bwd(o_f, o_b, proj, hf, hb, dn_g, lru_g, dmix, name):
    t = proj.shape[0]
    tm = _row_tile(t)

    def body(of_ref, ob_ref, z_ref, lg_ref, hf_ref, hb_ref, dg_ref, lgn_ref, dm_ref,
             do_ref, dz_ref, dlg_ref, dh_ref, ddg_ref, dlgn_ref):
        _, vjp = jax.vjp(_mix_fn, of_ref[...], ob_ref[...], z_ref[...], lg_ref[...], hf_ref[...], hb_ref[...],
                         dg_ref[...], lgn_ref[...])
        do, _, dz, dlg, dh, _, ddg, dlgn = vjp(dm_ref[...])
        do_ref[...] = do
        dz_ref[...] = dz
        dlg_ref[...] = dlg
        dh_ref[...] = dh

        @pl.when(pl.program_id(0) == 0)
        def _():
            ddg_ref[...] = jnp.zeros_like(ddg_ref)
            dlgn_ref[...] = jnp.zeros_like(dlgn_ref)

        ddg_ref[...] += ddg
        dlgn_ref[...] += dlgn

    row = pl.BlockSpec((tm, DN_WIDTH), lambda i: (i, 0))
    return pl.pallas_call(
        body, name=name, grid=(t // tm,),
        in_specs=_mix_specs(tm) + [pl.BlockSpec((tm, D_MODEL), lambda i: (i, 0))],
        out_specs=[row, row, row, row, pl.BlockSpec((1, LANES), lambda i: (0, 0)), pl.BlockSpec((1, DN_WIDTH), lambda i: (0, 0))],
        out_shape=[jax.ShapeDtypeStruct((t, DN_WIDTH), F32)] * 4
        + [jax.ShapeDtypeStruct((1, LANES), F32), jax.ShapeDtypeStruct((1, DN_WIDTH), F32)],
        compiler_params=_cparams(("arbitrary",)),
    )(o_f, o_b, proj, proj, hf, hb, dn_g, lru_g, dmix)


def _ffn_act_fwd(g3, u3, conv_w, conv_b, name):
    bsz, s, f = g3.shape
    nblk = f // LANES

    def body(g_ref, u_ref, w_ref, b_ref, o_ref):
        gate = _conv_fwd(g_ref[0], w_ref[...], 1) + b_ref[...]
        o_ref[0] = (_gelu(gate) * u_ref[0]).astype(o_ref.dtype)

    col = pl.BlockSpec((1, s, LANES), lambda b, j: (b, 0, j))
    return pl.pallas_call(
        body, name=name, grid=(bsz, nblk),
        in_specs=[col, col, pl.BlockSpec((3, LANES), lambda b, j: (0, j)), pl.BlockSpec((1, LANES), lambda b, j: (0, j))],
        out_specs=col, out_shape=jax.ShapeDtypeStruct((bsz, s, f), BF16),
        compiler_params=_cparams(("parallel", "parallel")),
    )(g3, u3, conv_w, conv_b)


def _ffn_act_bwd(g3, u3, conv_w, conv_b, dact3, name):
    bsz, s, f = g3.shape
    nblk = f // LANES

    def body(g_ref, u_ref, w_ref, b_ref, d_ref, dg_ref, du_ref, dw_ref, db_ref):
        g, w, u = g_ref[0], w_ref[...], u_ref[0]
        gate = _conv_fwd(g, w, 1) + b_ref[...]
        _, vjp = jax.vjp(lambda gt, uu: _gelu(gt) * uu, gate, u)
        dgate, du = vjp(d_ref[0])
        dg, dw = _conv_bwd(g, w, 1, dgate)
        dg_ref[0] = dg
        du_ref[0] = du

        @pl.when(pl.program_id(1) == 0)
        def _():
            dw_ref[...] = jnp.zeros_like(dw_ref)
            db_ref[...] = jnp.zeros_like(db_ref)

        dw_ref[...] += dw
        db_ref[...] += jnp.sum(dgate, axis=0, keepdims=True)

    col = pl.BlockSpec((1, s, LANES), lambda j, b: (b, 0, j))
    w3 = pl.BlockSpec((3, LANES), lambda j, b: (0, j))
    w1 = pl.BlockSpec((1, LANES), lambda j, b: (0, j))
    return pl.pallas_call(
        body, name=name, grid=(nblk, bsz),
        in_specs=[col, col, w3, w1, col], out_specs=[col, col, w3, w1],
        out_shape=[jax.ShapeDtypeStruct((bsz, s, f), F32), jax.ShapeDtypeStruct((bsz, s, f), F32),
                   jax.ShapeDtypeStruct((3, f), F32), jax.ShapeDtypeStruct((1, f), F32)],
        compiler_params=_cparams(("parallel", "arbitrary")),
    )(g3, u3, conv_w, conv_b, dact3)


def _ple_fn(r, pg, pp, bg):
    return r + _sigmoid(pg + bg) * pp


def _ple_fwd(r, pg, pp, bg, name):
    t, d = r.shape
    tm = _row_tile(t)

    def body(r_ref, pg_ref, pp_ref, bg_ref, o_ref):
        o_ref[...] = _ple_fn(r_ref[...], pg_ref[...], pp_ref[...], bg_ref[...])

    row = pl.BlockSpec((tm, d), lambda i: (i, 0))
    return pl.pallas_call(
        body, name=name, grid=(t // tm,), in_specs=[row, row, row, pl.BlockSpec((1, d), lambda i: (0, 0))],
        out_specs=row, out_shape=jax.ShapeDtypeStruct((t, d), F32),
        compiler_params=_cparams(("parallel",)),
    )(r, pg, pp, bg)


def _ple_bwd(pg, pp, bg, dr, name):
    t, d = pg.shape
    tm = _row_tile(t)

    def body(pg_ref, pp_ref, bg_ref, dr_ref, dpg_ref, dpp_ref, dbg_ref):
        _, vjp = jax.vjp(lambda a, b, c: _sigmoid(a + c) * b, pg_ref[...], pp_ref[...], bg_ref[...])
        dpg, dpp, dbg = vjp(dr_ref[...])
        dpg_ref[...] = dpg
        dpp_ref[...] = dpp

        @pl.when(pl.program_id(0) == 0)
        def _():
            dbg_ref[...] = jnp.zeros_like(dbg_ref)

        dbg_ref[...] += dbg

    row = pl.BlockSpec((tm, d), lambda i: (i, 0))
    vec = pl.BlockSpec((1, d), lambda i: (0, 0))
    return pl.pallas_call(
        body, name=name, grid=(t // tm,), in_specs=[row, row, vec, row], out_specs=[row, row, vec],
        out_shape=[jax.ShapeDtypeStruct((t, d), F32), jax.ShapeDtypeStruct((t, d), F32), jax.ShapeDtypeStruct((1, d), F32)],
        compiler_params=_cparams(("arbitrary",)),
    )(pg, pp, bg, dr)


def _loss_head(r, g, target, name):
    t, d = r.shape
    tm = _row_tile(t)

    def loss_fn(x, gg, tgt):
        err = _rms(x, gg) - tgt
        return 0.5 * jnp.sum(jnp.sum(err * err, axis=-1, keepdims=True) * (1.0 / d), axis=0, keepdims=True)

    def body(r_ref, g_ref, t_ref, l_ref, dr_ref, dg_ref):
        val, vjp = jax.vjp(lambda x, gg: loss_fn(x, gg, t_ref[...]), r_ref[...], g_ref[...])
        dx, dg = vjp(jnp.ones((1, 1), F32))
        dr_ref[...] = dx

        @pl.when(pl.program_id(0) == 0)
        def _():
            l_ref[...] = jnp.zeros_like(l_ref)
            dg_ref[...] = jnp.zeros_like(dg_ref)

        l_ref[...] += val
        dg_ref[...] += dg

    row = pl.BlockSpec((tm, d), lambda i: (i, 0))
    vec = pl.BlockSpec((1, d), lambda i: (0, 0))
    one = pl.BlockSpec((1, 1), lambda i: (0, 0))
    return pl.pallas_call(
        body, name=name, grid=(t // tm,), in_specs=[row, vec, row], out_specs=[one, row, vec],
        out_shape=[jax.ShapeDtypeStruct((1, 1), F32), jax.ShapeDtypeStruct((t, d), F32), jax.ShapeDtypeStruct((1, d), F32)],
        compiler_params=_cparams(("arbitrary",)),
    )(r, g, target)


def _adamw(w, g, m, v, name):
    r, c = w.shape
    tr = r if r <= 512 else _row_tile(r, 512)

    def body(w_ref, g_ref, m_ref, v_ref, d_ref, nm_ref, nv_ref):
        gg = g_ref[...]
        nm = ADAM_B1 * m_ref[...] + (1.0 - ADAM_B1) * gg
        nv = ADAM_B2 * v_ref[...] + (1.0 - ADAM_B2) * (gg * gg)
        m_hat = nm / (1.0 - ADAM_B1 ** ADAM_STEP)
        v_hat = nv / (1.0 - ADAM_B2 ** ADAM_STEP)
        d_ref[...] = -ADAM_LR * (m_hat / (jnp.sqrt(v_hat) + ADAM_EPS) + ADAM_WD * w_ref[...])
        nm_ref[...] = nm
        nv_ref[...] = nv

    blk = pl.BlockSpec((tr, c), lambda i: (i, 0))
    return pl.pallas_call(
        body, name=name, grid=(r // tr,), in_specs=[blk] * 4, out_specs=[blk] * 3,
        out_shape=[jax.ShapeDtypeStruct((r, c), F32)] * 3,
        compiler_params=_cparams(("parallel",)),
    )(w, g, m, v)


def _prepare_weights(w):
    nl = w["w_in"].shape[0]
    w_in = w["w_in"].astype(BF16)
    pad = jnp.zeros(w_in.shape[:2] + (P_COLS - w_in.shape[2],), BF16)
    split = 4 * DN_WIDTH + N_GATE
    w_in_p = jnp.concatenate([w_in[:, :, split:], w_in[:, :, :split], pad], axis=-1)
    gate_vec = lambda a: jnp.pad(a.reshape(nl, 1, N_GATE // 2), ((0, 0), (0, 0), (N_GATE // 2, LANES - N_GATE)))

    def pair_blocks(a):
        a = a.reshape(nl, 2, 4, 2, 64, 64)
        z = jnp.zeros_like(a[:, :, :, 0])
        top = jnp.concatenate([a[:, :, :, 0], z], axis=-1)
        bot = jnp.concatenate([z, a[:, :, :, 1]], axis=-1)
        return jnp.concatenate([top, bot], axis=-2).astype(BF16)

    bf = lambda a: a.astype(BF16)
    tr = lambda a: jnp.swapaxes(a.astype(BF16), 1, 2)
    return dict(
        norm1_g=w["norm1_g"][:, None, :], w_in=w_in_p, w_in_t=jnp.swapaxes(w_in_p, 1, 2),
        dn_conv_w=w["dn_conv_w"], alog=gate_vec(w["dn_a_log"]), dtb=gate_vec(w["dn_dt_bias"]),
        dn_norm_g=w["dn_norm_g"][:, None, :], lru_conv_w=w["lru_conv_w"], lru_conv_b=w["lru_conv_b"][:, None, :],
        lru_wa=pair_blocks(w["lru_wa"]), lru_ba=w["lru_ba"], lru_wx=pair_blocks(w["lru_wx"]), lru_bx=w["lru_bx"],
        lru_lambda=w["lru_lambda"], lru_norm_g=w["lru_norm_g"][:, None, :],
        w_out=bf(w["w_out"]), w_out_t=tr(w["w_out"]), norm2_g=w["norm2_g"][:, None, :],
        ffn_wg=bf(w["ffn_wg"]), ffn_wg_t=tr(w["ffn_wg"]), ffn_wu=bf(w["ffn_wu"]), ffn_wu_t=tr(w["ffn_wu"]),
        ffn_conv_w=w["ffn_conv_w"], ffn_conv_b=w["ffn_conv_b"][:, None, :],
        ffn_wd=bf(w["ffn_wd"]), ffn_wd_t=tr(w["ffn_wd"]), ple_norm_g=w["ple_norm_g"][:, None, :],
        ple_wg=bf(w["ple_wg"]), ple_wg_t=tr(w["ple_wg"]), ple_bg=w["ple_bg"][:, None, :], ple_wp=bf(w["ple_wp"]),
        final_g=w["final_g"][None, :],
    )


def _unpair_blocks(a):
    top = a[:, :, :64, :64]
    bot = a[:, :, 64:, 64:]
    return jnp.stack([top, bot], axis=2).reshape(2, 8, 64, 64)


def _local_step(x, p, target, w):
    bsz, s, d = x.shape
    t = bsz * s
    nl = w["w_in"].shape[0]
    kw = _prepare_weights(w)
    flat = lambda a: a.reshape(t, a.shape[-1])
    seq = lambda a: a.reshape(bsz, s, a.shape[-1])

    saved = []
    r = flat(x)
    for i in range(nl):
        n = f"l{i}_"
        sv = {"r0": r}
        h = _rmsnorm_fwd(r, kw["norm1_g"][i], n + "norm1")
        proj = _matmul([h], [kw["w_in"][i]], n + "in_proj")
        gb = _dn_gates_fwd(proj, kw["alog"][i], kw["dtb"][i], n + "dn_gates")
        qkv = _dn_prep_fwd(seq(proj), kw["dn_conv_w"][i], n + "dn_prep")
        in_f = _dn_intra_fwd(qkv, seq(gb), False, n + "dn_intra_fwd")
        in_b = _dn_intra_fwd(qkv, seq(gb), True, n + "dn_intra_rev")
        o_f, st_f = _dn_rec_fwd(in_f, False, n + "dn_rec_fwd")
        o_b, st_b = _dn_rec_fwd(in_b, True, n + "dn_rec_rev")
        lru_args = (seq(proj), kw["lru_conv_w"][i], kw["lru_conv_b"][i], kw["lru_wa"][i], kw["lru_ba"][i],
                    kw["lru_wx"][i], kw["lru_bx"][i], kw["lru_lambda"][i])
        hf, hb = _lru_fwd(*lru_args, n + "lru")
        mix_args = (flat(o_f), flat(o_b), proj, flat(hf), flat(hb), kw["dn_norm_g"][i], kw["lru_norm_g"][i])
        mix = _mix_fwd(*mix_args, n + "mix")
        r1 = _matmul([mix], [kw["w_out"][i]], n + "out_proj", res=r)
        h2 = _rmsnorm_fwd(r1, kw["norm2_g"][i], n + "norm2")
        fg = _matmul([h2], [kw["ffn_wg"][i]], n + "ffn_g")
        fu = _matmul([h2], [kw["ffn_wu"][i]], n + "ffn_u")
        act = _ffn_act_fwd(seq(fg), seq(fu), kw["ffn_conv_w"][i], kw["ffn_conv_b"][i], n + "ffn_act")
        r2 = _matmul([flat(act)], [kw["ffn_wd"][i]], n + "ffn_d", res=r1)
        hp = _rmsnorm_fwd(r2, kw["ple_norm_g"][i], n + "ple_norm")
        pg = _matmul([hp], [kw["ple_wg"][i]], n + "ple_g")
        pi = flat(p[i])
        pp = _matmul([pi], [kw["ple_wp"][i]], n + "ple_p")
        r3 = _ple_fwd(r2, pg, pp, kw["ple_bg"][i], n + "ple")
        sv.update(h=h, proj=proj, gb=gb, qkv=qkv, st_f=st_f, st_b=st_b, in_f=in_f, in_b=in_b, lru_args=lru_args, hf=hf, hb=hb,
                  mix_args=mix_args, mix=mix, r1=r1, h2=h2, fg=fg, fu=fu, act=act, r2=r2, hp=hp, pg=pg, pp=pp, pi=pi)
        saved.append(sv)
        r = r3

    loss, dr, dfinal = _loss_head(r, kw["final_g"], flat(target), "loss_head")
    grads = {k: [None] * nl for k in (
        "norm1_g", "w_in", "dn_conv_w", "dn_a_log", "dn_dt_bias", "dn_norm_g", "lru_conv_w", "lru_conv_b", "lru_wa", "lru_ba",
        "lru_wx", "lru_bx", "lru_lambda", "lru_norm_g", "w_out", "norm2_g", "ffn_wg", "ffn_wu", "ffn_conv_w", "ffn_conv_b",
        "ffn_wd", "ple_norm_g", "ple_wg", "ple_bg", "ple_wp")}

    for i in reversed(range(nl)):
        n = f"b{i}_"
        sv = saved[i]
        dpg, dpp, dbg = _ple_bwd(sv["pg"], sv["pp"], kw["ple_bg"][i], dr, n + "ple")
        grads["ple_bg"][i] = dbg[0]
        grads["ple_wp"][i] = _matmul_tn(sv["pi"], dpp, n + "ple_wp")
        grads["ple_wg"][i] = _matmul_tn(sv["hp"], dpg, n + "ple_wg")
        dhp = _matmul([dpg], [kw["ple_wg_t"][i]], n + "ple_dh")
        dr2, dg = _rmsnorm_bwd(sv["r2"], kw["ple_norm_g"][i], dhp, dr, n + "ple_norm")
        grads["ple_norm_g"][i] = dg[0]
        grads["ffn_wd"][i] = _matmul_tn(flat(sv["act"]), dr2, n + "ffn_wd")
        dact = _matmul([dr2], [kw["ffn_wd_t"][i]], n + "ffn_dact")
        dfg, dfu, dcw, dcb = _ffn_act_bwd(seq(sv["fg"]), seq(sv["fu"]), kw["ffn_conv_w"][i], kw["ffn_conv_b"][i], seq(dact),
                                          n + "ffn_act")
        grads["ffn_conv_w"][i] = dcw
        grads["ffn_conv_b"][i] = dcb[0]
        grads["ffn_wg"][i] = _matmul_tn(sv["h2"], flat(dfg), n + "ffn_wg")
        grads["ffn_wu"][i] = _matmul_tn(sv["h2"], flat(dfu), n + "ffn_wu")
        dh2 = _matmul([flat(dfg), flat(dfu)], [kw["ffn_wg_t"][i], kw["ffn_wu_t"][i]], n + "ffn_dh")
        dr1, dg = _rmsnorm_bwd(sv["r1"], kw["norm2_g"][i], dh2, dr2, n + "norm2")
        grads["norm2_g"][i] = dg[0]
        grads["w_out"][i] = _matmul_tn(sv["mix"], dr1, n + "w_out")
        dmix = _matmul([dr1], [kw["w_out_t"][i]], n + "dmix")
        do, dz, dlg, dh, ddn_g, dlru_g = _mix_bwd(*sv["mix_args"], dmix, n + "mix")
        grads["dn_norm_g"][i] = ddn_g[0]
        grads["lru_norm_g"][i] = dlru_g[0]
        dlx, dcw, dcb, dwa, dba, dwx, dbx, dlam = _lru_bwd(*sv["lru_args"], sv["hf"], sv["hb"], seq(dh), n + "lru")
        grads["lru_conv_w"][i] = dcw
        grads["lru_conv_b"][i] = dcb[0]
        grads["lru_wa"][i] = _unpair_blocks(dwa)
        grads["lru_wx"][i] = _unpair_blocks(dwx)
        grads["lru_ba"][i], grads["lru_bx"][i], grads["lru_lambda"][i] = dba, dbx, dlam
        ct_f = _dn_rec_bwd(sv["in_f"], sv["st_f"], seq(do), False, n + "dn_rec_fwd")
        ct_b = _dn_rec_bwd(sv["in_b"], sv["st_b"], seq(do), True, n + "dn_rec_rev")
        dqkv_f, dgb_f = _dn_intra_bwd(sv["qkv"], seq(sv["gb"]), ct_f, False, n + "dn_intra_fwd")
        dqkv_b, dgb_b = _dn_intra_bwd(sv["qkv"], seq(sv["gb"]), ct_b, True, n + "dn_intra_rev")
        dgate, dalog, ddtb = _dn_gates_bwd(sv["proj"], kw["alog"][i], kw["dtb"][i], flat(dgb_f), flat(dgb_b), n + "dn_gates")
        grads["dn_a_log"][i] = dalog[0, N_GATE // 2:N_GATE].reshape(2, DN_HEADS)
        grads["dn_dt_bias"][i] = ddtb[0, N_GATE // 2:N_GATE].reshape(2, DN_HEADS)
        dpqkv, dcw = _dn_prep_bwd(seq(sv["proj"]), kw["dn_conv_w"][i], dqkv_f + dqkv_b, n + "dn_prep")
        grads["dn_conv_w"][i] = dcw
        segs = [flat(dlx), dlg, flat(dpqkv), dz, dgate]
        offs = [P_LX, P_LG, P_Q, P_Z, P_GATE]
        wt = kw["w_in_t"][i]
        dh1 = _matmul(segs, [wt[o:o + sg.shape[1]] for o, sg in zip(offs, segs)], n + "in_dh")
        dwp = jnp.concatenate([_matmul_tn(sv["h"], sg, n + f"w_in{j}") for j, sg in enumerate(segs)], axis=-1)
        split = 4 * DN_WIDTH + N_GATE
        grads["w_in"][i] = jnp.concatenate([dwp[:, P_Q:P_Q + split], dwp[:, :P_Q]], axis=-1)
        dr, dg = _rmsnorm_bwd(sv["r0"], kw["norm1_g"][i], dh1, dr1, n + "norm1")
        grads["norm1_g"][i] = dg[0]

    grads["final_g"] = dfinal[0]
    return loss[0, 0], dr.reshape(bsz, s, d), grads


MESH = pl.DeviceIdType.MESH
ANY = pl.BlockSpec(memory_space=pl.ANY)
N_CHIPS = 4
HALF = 2


def _place():
    x, y, c = lax.axis_index("x"), lax.axis_index("y"), lax.axis_index("c")
    chips = [(1 - x, y), (x, 1 - y), (1 - x, 1 - y)]
    return x, y, c, chips


def _dma_sems(n):
    return pltpu.SemaphoreType.DMA((n,))


def _gather_weights(shards):
    nt = len(shards)

    def body(*refs):
        w_refs, g_refs = refs[:nt], refs[nt:2 * nt]
        send_sems, recv_sems, local_sems = refs[2 * nt:]
        x, y, c, chips = _place()
        me = 2 * x + y
        mine = pl.ds(c * HALF, HALF)
        theirs = pl.ds((1 - c) * HALF, HALF)

        def rc(src, dst, k, to):
            return pltpu.make_async_remote_copy(src_ref=src, dst_ref=dst, send_sem=send_sems.at[k], recv_sem=recv_sems.at[k],
                                                device_id=to, device_id_type=MESH)

        local = [pltpu.make_async_copy(w_refs[t], g_refs[t].at[me], local_sems.at[t]) for t in range(nt)]
        for cp in local:
            cp.start()
        first = []
        for t in range(nt):
            for r, (px, py) in enumerate(chips):
                first.append(rc(w_refs[t].at[mine], g_refs[t].at[me, mine], 6 * t + r, (px, py, c)))
        for cp in first:
            cp.start()
        passed = []
        for t in range(nt):
            for r, (px, py) in enumerate(chips):
                peer = 2 * px + py
                rc(w_refs[t].at[mine], g_refs[t].at[peer, mine], 6 * t + r, (px, py, c)).wait_recv()
                fw = rc(g_refs[t].at[peer, mine], g_refs[t].at[peer, mine], 6 * t + 3 + r, (x, y, 1 - c))
                fw.start()
                passed.append(fw)
        for t in range(nt):
            for r, (px, py) in enumerate(chips):
                peer = 2 * px + py
                rc(g_refs[t].at[peer, theirs], g_refs[t].at[peer, theirs], 6 * t + 3 + r, (x, y, 1 - c)).wait_recv()
        for cp in first + passed:
            cp.wait_send()
        for cp in local:
            cp.wait()

    return pl.pallas_call(
        body, name="gather_weights", in_specs=[ANY] * nt, out_specs=[ANY] * nt,
        out_shape=[jax.ShapeDtypeStruct((N_CHIPS,) + a.shape, a.dtype) for a in shards],
        scratch_shapes=[_dma_sems(6 * nt), _dma_sems(6 * nt), _dma_sems(nt)],
    )(*shards)


def _swap_halves(blocks):
    nt = len(blocks)

    def body(*refs):
        p_refs, l_refs = refs[:nt], refs[nt:2 * nt]
        send_sems, recv_sems = refs[2 * nt:]
        x, y, c, _ = _place()
        theirs = pl.ds((1 - c) * HALF, HALF)
        cps = [pltpu.make_async_remote_copy(src_ref=p_refs[t].at[:, theirs], dst_ref=l_refs[t], send_sem=send_sems.at[t],
                                            recv_sem=recv_sems.at[t], device_id=(x, y, 1 - c), device_id_type=MESH)
               for t in range(nt)]
        for cp in cps:
            cp.start()
        for cp in cps:
            cp.wait_send()
            cp.wait_recv()

    return pl.pallas_call(
        body, name="swap_halves", in_specs=[ANY] * nt, out_specs=[ANY] * nt,
        out_shape=[jax.ShapeDtypeStruct((a.shape[0], HALF) + a.shape[2:], a.dtype) for a in blocks],
        scratch_shapes=[_dma_sems(nt), _dma_sems(nt)],
    )(*blocks)


def _tile_rows(a, row_bytes):
    best = None
    for d in range(16, a + 1, 16):
        if a % d == 0 and d * row_bytes <= 2 * 1024 * 1024:
            best = d
    return best if best is not None else a


def _add_halves(pg, l1, c_arr, name):
    n, _, a, b = pg.shape
    ta = _tile_rows(a, 4 * b)

    def body(c_ref, a_ref, b_ref, o_ref):
        o_ref[...] = (a_ref[...] + b_ref[...]).astype(o_ref.dtype)

    grid_spec = pltpu.PrefetchScalarGridSpec(
        num_scalar_prefetch=1, grid=(n, HALF, a // ta),
        in_specs=[pl.BlockSpec((1, 1, ta, b), lambda k, l, i, c_ref: (k, c_ref[0] * HALF + l, i, 0)),
                  pl.BlockSpec((1, 1, ta, b), lambda k, l, i, c_ref: (k, l, i, 0))],
        out_specs=pl.BlockSpec((1, 1, ta, b), lambda k, l, i, c_ref: (k, l, i, 0)))
    return pl.pallas_call(
        body, name=name, grid_spec=grid_spec, out_shape=jax.ShapeDtypeStruct((n, HALF, a, b), BF16),
        compiler_params=_cparams(("parallel", "parallel", "parallel")),
    )(c_arr, pg, l1)


def _scatter_chips(blocks):
    nt = len(blocks)

    def body(*refs):
        q_refs, l_refs = refs[:nt], refs[nt:2 * nt]
        send_sems, recv_sems, local_sems = refs[2 * nt:]
        x, y, c, chips = _place()
        me = 2 * x + y
        local = [pltpu.make_async_copy(q_refs[t].at[me], l_refs[t].at[me], local_sems.at[t]) for t in range(nt)]
        for cp in local:
            cp.start()

        def rc(t, r, src_chip, dst_chip, to):
            return pltpu.make_async_remote_copy(src_ref=q_refs[t].at[src_chip], dst_ref=l_refs[t].at[dst_chip],
                                                send_sem=send_sems.at[3 * t + r], recv_sem=recv_sems.at[3 * t + r],
                                                device_id=to, device_id_type=MESH)

        sends = [rc(t, r, 2 * px + py, me, (px, py, c)) for t in range(nt) for r, (px, py) in enumerate(chips)]
        for cp in sends:
            cp.start()
        for t in range(nt):
            for r, (px, py) in enumerate(chips):
                rc(t, r, 2 * px + py, 2 * px + py, (px, py, c)).wait_recv()
        for cp in sends:
            cp.wait_send()
        for cp in local:
            cp.wait()

    return pl.pallas_call(
        body, name="scatter_chips", in_specs=[ANY] * nt, out_specs=[ANY] * nt,
        out_shape=[jax.ShapeDtypeStruct(a.shape, a.dtype) for a in blocks],
        scratch_shapes=[_dma_sems(3 * nt), _dma_sems(3 * nt), _dma_sems(nt)],
    )(*blocks)


def _sum_chips(l2, name):
    n, _, a, b = l2.shape
    ta = _tile_rows(a, 4 * b)

    def body(a_ref, o_ref):
        acc = a_ref[0].astype(F32)
        for k in range(1, n):
            acc = acc + a_ref[k].astype(F32)
        o_ref[...] = acc

    return pl.pallas_call(
        body, name=name, grid=(HALF, a // ta),
        in_specs=[pl.BlockSpec((n, 1, ta, b), lambda l, i: (0, l, i, 0))],
        out_specs=pl.BlockSpec((1, ta, b), lambda l, i: (l, i, 0)),
        out_shape=jax.ShapeDtypeStruct((HALF, a, b), F32),
        compiler_params=_cparams(("parallel", "parallel")),
    )(l2)


def _join_halves(parts):
    nt = len(parts)

    def body(*refs):
        r_refs, g_refs = refs[:nt], refs[nt:2 * nt]
        send_sems, recv_sems, local_sems = refs[2 * nt:]
        x, y, c, _ = _place()
        mine = pl.ds(c * HALF, HALF)
        theirs = pl.ds((1 - c) * HALF, HALF)
        local = [pltpu.make_async_copy(r_refs[t], g_refs[t].at[mine], local_sems.at[t]) for t in range(nt)]
        for cp in local:
            cp.start()

        def rc(t, dst):
            return pltpu.make_async_remote_copy(src_ref=r_refs[t], dst_ref=g_refs[t].at[dst], send_sem=send_sems.at[t],
                                                recv_sem=recv_sems.at[t], device_id=(x, y, 1 - c), device_id_type=MESH)

        sends = [rc(t, mine) for t in range(nt)]
        for cp in sends:
            cp.start()
        for t in range(nt):
            rc(t, theirs).wait_recv()
        for cp in sends:
            cp.wait_send()
        for cp in local:
            cp.wait()

    return pl.pallas_call(
        body, name="join_halves", in_specs=[ANY] * nt, out_specs=[ANY] * nt,
        out_shape=[jax.ShapeDtypeStruct((2 * HALF,) + a.shape[1:], a.dtype) for a in parts],
        scratch_shapes=[_dma_sems(nt), _dma_sems(nt), _dma_sems(nt)],
    )(*parts)


def _gather_chips(v):
    def body(v_ref, g_ref, send_sems, recv_sems, local_sem):
        x, y, c, chips = _place()
        me = 2 * x + y
        loc = pltpu.make_async_copy(v_ref, g_ref.at[me], local_sem)
        loc.start()
        sends = [pltpu.make_async_remote_copy(src_ref=v_ref, dst_ref=g_ref.at[me], send_sem=send_sems.at[r], recv_sem=recv_sems.at[r],
                                              device_id=(px, py, c), device_id_type=MESH) for r, (px, py) in enumerate(chips)]
        for cp in sends:
            cp.start()
        for r, (px, py) in enumerate(chips):
            pltpu.make_async_remote_copy(src_ref=v_ref, dst_ref=g_ref.at[2 * px + py], send_sem=send_sems.at[r],
                                         recv_sem=recv_sems.at[r], device_id=(px, py, c), device_id_type=MESH).wait_recv()
        for cp in sends:
            cp.wait_send()
        loc.wait()

    return pl.pallas_call(
        body, name="gather_chips", in_specs=[ANY], out_specs=ANY, out_shape=jax.ShapeDtypeStruct((N_CHIPS,) + v.shape, v.dtype),
        scratch_shapes=[_dma_sems(3), _dma_sems(3), pltpu.SemaphoreType.DMA],
    )(v)


BIG = (("w_in", 2), ("w_out", 1), ("ffn_wg", 2), ("ffn_wu", 2), ("ffn_wd", 1), ("ple_wg", 1), ("ple_wp", 2))
SMALL = (("dn_conv_w", 2), ("lru_conv_w", 2), ("lru_ba", 2), ("lru_bx", 2), ("lru_lambda", 2), ("ffn_conv_w", 2))
REPL = ("norm1_g", "dn_a_log", "dn_dt_bias", "dn_norm_g", "lru_conv_b", "lru_wa", "lru_wx", "lru_norm_g", "norm2_g",
        "ffn_conv_b", "ple_norm_g", "ple_bg", "final_g")
WEIGHTS = ("norm1_g", "w_in", "dn_conv_w", "dn_a_log", "dn_dt_bias", "dn_norm_g", "lru_conv_w", "lru_conv_b", "lru_wa", "lru_ba",
           "lru_wx", "lru_bx", "lru_lambda", "lru_norm_g", "w_out", "norm2_g", "ffn_wg", "ffn_wu", "ffn_conv_w", "ffn_conv_b",
           "ffn_wd", "ple_norm_g", "ple_wg", "ple_bg", "ple_wp", "final_g")
ROW_ALIGN = 16


def _rows_for(n_elems):
    rows = -(-n_elems // LANES)
    return -(-rows // ROW_ALIGN) * ROW_ALIGN


def _join_chips(g, axis):
    if axis == 2:
        return jnp.concatenate([g[k] for k in range(N_CHIPS)], axis=2)
    return jnp.concatenate([g[k] for k in range(N_CHIPS)], axis=1)


def _chip_blocks(layers, axis):
    nl = len(layers)
    size = layers[0].shape[axis - 1] // N_CHIPS
    cut = (lambda g, k: g[:, k * size:(k + 1) * size]) if axis == 2 else (lambda g, k: g[k * size:(k + 1) * size])
    return jnp.stack([jnp.stack([cut(layers[l], k) for l in range(nl)]) for k in range(N_CHIPS)])


def _pack_small(w):
    nl = w[SMALL[0][0]].shape[0]
    flat = jnp.concatenate([w[n].reshape(nl, -1) for n, _ in SMALL], axis=1)
    rows = _rows_for(flat.shape[1])
    return jnp.pad(flat, ((0, 0), (0, rows * LANES - flat.shape[1]))).reshape(nl, rows, LANES)


def _unpack_small_gathered(g, w):
    nl = g.shape[1]
    flat = g.reshape(N_CHIPS, nl, -1)
    out, off = {}, 0
    for n, _ in SMALL:
        _, a, b = w[n].shape
        piece = flat[:, :, off:off + a * b].reshape(N_CHIPS, nl, a, b)
        off += a * b
        out[n] = jnp.transpose(piece, (1, 2, 0, 3)).reshape(nl, a, N_CHIPS * b)
    return out


def _pack_small_grads(grads, w):
    nl = w["w_in"].shape[0]
    cols = []
    for n, _ in SMALL:
        _, a, b = w[n].shape
        gfull = jnp.stack(grads[n])
        cols.append(jnp.transpose(gfull.reshape(nl, a, N_CHIPS, b), (2, 0, 1, 3)).reshape(N_CHIPS, nl, a * b))
    small = jnp.concatenate(cols, axis=2)
    rs = _rows_for(small.shape[2])
    small = jnp.pad(small, ((0, 0), (0, 0), (0, rs * LANES - small.shape[2])))
    rep = jnp.concatenate([(grads[n] if n == "final_g" else jnp.stack(grads[n])).reshape(-1) for n in REPL])
    rr = _rows_for(-(-rep.shape[0] // (N_CHIPS * nl)))
    rep = jnp.pad(rep, (0, N_CHIPS * nl * rr * LANES - rep.shape[0])).reshape(N_CHIPS, nl, rr * LANES)
    return jnp.concatenate([small, rep], axis=2).reshape(N_CHIPS, nl, rs + rr, LANES), (rs, rr)


def _unpack_small_reduced(g, rep_all, rows, w):
    rs, _ = rows
    nl = g.shape[0]
    out = {}
    flat = g[:, :rs].reshape(nl, -1)
    off = 0
    for n, _ in SMALL:
        _, a, b = w[n].shape
        out[n] = flat[:, off:off + a * b].reshape(nl, a, b)
        off += a * b
    flat = rep_all.reshape(-1)
    off = 0
    for n in REPL:
        size = math.prod(w[n].shape)
        out[n] = flat[off:off + size].reshape(w[n].shape)
        off += size
    return out


def _as2d(a):
    if a.ndim == 1:
        return a.reshape(1, -1)
    return a.reshape(-1, a.shape[-1])


def kernel(x, p, norm1_g, w_in, dn_conv_w, dn_a_log, dn_dt_bias, dn_norm_g, lru_conv_w, lru_conv_b, lru_wa, lru_ba, lru_wx, lru_bx, lru_lambda, lru_norm_g, w_out, norm2_g, ffn_wg, ffn_wu, ffn_conv_w, ffn_conv_b, ffn_wd, ple_norm_g, ple_wg, ple_bg, ple_wp, final_g, loss_target, m_norm1_g, m_w_in, m_dn_conv_w, m_dn_a_log, m_dn_dt_bias, m_dn_norm_g, m_lru_conv_w, m_lru_conv_b, m_lru_wa, m_lru_ba, m_lru_wx, m_lru_bx, m_lru_lambda, m_lru_norm_g, m_w_out, m_norm2_g, m_ffn_wg, m_ffn_wu, m_ffn_conv_w, m_ffn_conv_b, m_ffn_wd, m_ple_norm_g, m_ple_wg, m_ple_bg, m_ple_wp, m_final_g, v_norm1_g, v_w_in, v_dn_conv_w, v_dn_a_log, v_dn_dt_bias, v_dn_norm_g, v_lru_conv_w, v_lru_conv_b, v_lru_wa, v_lru_ba, v_lru_wx, v_lru_bx, v_lru_lambda, v_lru_norm_g, v_w_out, v_norm2_g, v_ffn_wg, v_ffn_wu, v_ffn_conv_w, v_ffn_conv_b, v_ffn_wd, v_ple_norm_g, v_ple_wg, v_ple_bg, v_ple_wp, v_final_g):
    w = dict(norm1_g=norm1_g, w_in=w_in, dn_conv_w=dn_conv_w, dn_a_log=dn_a_log, dn_dt_bias=dn_dt_bias, dn_norm_g=dn_norm_g,
             lru_conv_w=lru_conv_w, lru_conv_b=lru_conv_b, lru_wa=lru_wa, lru_ba=lru_ba, lru_wx=lru_wx, lru_bx=lru_bx,
             lru_lambda=lru_lambda, lru_norm_g=lru_norm_g, w_out=w_out, norm2_g=norm2_g, ffn_wg=ffn_wg, ffn_wu=ffn_wu,
             ffn_conv_w=ffn_conv_w, ffn_conv_b=ffn_conv_b, ffn_wd=ffn_wd, ple_norm_g=ple_norm_g, ple_wg=ple_wg, ple_bg=ple_bg,
             ple_wp=ple_wp, final_g=final_g)
    m = dict(norm1_g=m_norm1_g, w_in=m_w_in, dn_conv_w=m_dn_conv_w, dn_a_log=m_dn_a_log, dn_dt_bias=m_dn_dt_bias,
             dn_norm_g=m_dn_norm_g, lru_conv_w=m_lru_conv_w, lru_conv_b=m_lru_conv_b, lru_wa=m_lru_wa, lru_ba=m_lru_ba,
             lru_wx=m_lru_wx, lru_bx=m_lru_bx, lru_lambda=m_lru_lambda, lru_norm_g=m_lru_norm_g, w_out=m_w_out, norm2_g=m_norm2_g,
             ffn_wg=m_ffn_wg, ffn_wu=m_ffn_wu, ffn_conv_w=m_ffn_conv_w, ffn_conv_b=m_ffn_conv_b, ffn_wd=m_ffn_wd,
             ple_norm_g=m_ple_norm_g, ple_wg=m_ple_wg, ple_bg=m_ple_bg, ple_wp=m_ple_wp, final_g=m_final_g)
    v = dict(norm1_g=v_norm1_g, w_in=v_w_in, dn_conv_w=v_dn_conv_w, dn_a_log=v_dn_a_log, dn_dt_bias=v_dn_dt_bias,
             dn_norm_g=v_dn_norm_g, lru_conv_w=v_lru_conv_w, lru_conv_b=v_lru_conv_b, lru_wa=v_lru_wa, lru_ba=v_lru_ba,
             lru_wx=v_lru_wx, lru_bx=v_lru_bx, lru_lambda=v_lru_lambda, lru_norm_g=v_lru_norm_g, w_out=v_w_out, norm2_g=v_norm2_g,
             ffn_wg=v_ffn_wg, ffn_wu=v_ffn_wu, ffn_conv_w=v_ffn_conv_w, ffn_conv_b=v_ffn_conv_b, ffn_wd=v_ffn_wd,
             ple_norm_g=v_ple_norm_g, ple_wg=v_ple_wg, ple_bg=v_ple_bg, ple_wp=v_ple_wp, final_g=v_final_g)

    gathered = _gather_weights([w[n].astype(BF16) for n, _ in BIG] + [_pack_small(w)])
    full = {n: w[n] for n in REPL}
    for (n, axis), g in zip(BIG, gathered):
        full[n] = _join_chips(g, axis)
    full.update(_unpack_small_gathered(gathered[-1], w))

    loss_local, grad_x, grads = _local_step(x, p, loss_target, full)
    loss = lax.psum(loss_local, ("x", "y", "c"))

    small_pack, rows = _pack_small_grads(grads, w)
    blocks = [_chip_blocks(grads[n], axis) for n, axis in BIG]
    blocks.append(small_pack)
    names = [n for n, _ in BIG] + ["small"]
    c_arr = lax.axis_index("c").astype(jnp.int32).reshape(1)
    from_sibling = _swap_halves(blocks)
    halves = [_add_halves(b4.reshape(b4.shape[:2] + (-1, b4.shape[-1])), l1.reshape(l1.shape[:2] + (-1, l1.shape[-1])), c_arr,
                          "add_halves_" + n) for n, b4, l1 in zip(names, blocks, from_sibling)]
    arrived = _scatter_chips(halves)
    sums = [_sum_chips(l2, "sum_chips_" + n) for n, l2 in zip(names, arrived)]
    reduced = _join_halves(sums)
    g = {n: r for n, r in zip(names[:-1], reduced[:-1])}
    rep_all = _gather_chips(reduced[-1][:, rows[0]:])
    g.update(_unpack_small_reduced(reduced[-1], rep_all, rows, w))

    deltas, new_m, new_v = {}, {}, {}
    for n in WEIGHTS:
        gn = g[n].reshape(w[n].shape)
        g[n] = gn
        d2, m2, v2 = _adamw(_as2d(w[n]), _as2d(gn), _as2d(m[n]), _as2d(v[n]), "adamw_" + n)
        deltas[n], new_m[n], new_v[n] = d2.reshape(w[n].shape), m2.reshape(w[n].shape), v2.reshape(w[n].shape)
    return (loss, grad_x, *[g[n] for n in WEIGHTS], *[deltas[n] for n in WEIGHTS], *[new_m[n] for n in WEIGHTS],
            *[new_v[n] for n in WEIGHTS])
```

```python
import functools
import math

import jax
import jax.numpy as jnp
from jax import lax
from jax.experimental import pallas as pl
from jax.experimental.pallas import tpu as pltpu

F32 = jnp.float32
BF16 = jnp.bfloat16

D_MODEL = 1024
DN_HEADS = 4
DN_HEAD_DIM = 128
DN_WIDTH = 512
LRU_WIDTH = 512
LRU_C = 8.0
CHUNK = 64
EPS = 1e-6
P_LX, P_LG, P_Q, P_K, P_V, P_Z, P_GATE, P_COLS = 0, 512, 1024, 1536, 2048, 2560, 3072, 3200
N_GATE = 16
LANES = 128
VMEM_LIMIT = 56 * 1024 * 1024
MM_A_BLOCK_BYTES = 6 * 1024 * 1024
MM_B_BLOCK_BYTES = 4 * 1024 * 1024

ADAM_LR, ADAM_B1, ADAM_B2, ADAM_EPS, ADAM_WD, ADAM_STEP = 0.001, 0.9, 0.999, 1e-08, 0.01, 10


def _cparams(sem):
    return pltpu.CompilerParams(dimension_semantics=sem, vmem_limit_bytes=VMEM_LIMIT)


def _row_tile(m, want=512):
    for t in range(min(want, m) // 16 * 16, 0, -16):
        if m % t == 0:
            return t
    return m


def _col_tile(n, cap=1536):
    best = None
    for d in range(LANES, min(n, cap) + 1, LANES):
        if n % d == 0:
            best = d
    return best if best is not None else n


def _bdot(a, b):
    return jnp.dot(a.astype(BF16), b.astype(BF16), preferred_element_type=F32)


def _bdot_nt(a, b):
    return lax.dot_general(a.astype(BF16), b.astype(BF16), (((1,), (1,)), ((), ())), preferred_element_type=F32)


def _bdot_tn(a, b):
    return lax.dot_general(a.astype(BF16), b.astype(BF16), (((0,), (0,)), ((), ())), preferred_element_type=F32)


def _rms(x, g):
    return x * lax.rsqrt(jnp.mean(x * x, axis=-1, keepdims=True) + EPS) * g


def _gelu(x):
    return 0.5 * x * (1.0 + jnp.tanh(0.7978845608028654 * (x + 0.044715 * x * x * x)))


def _sigmoid(x):
    return 1.0 / (1.0 + jnp.exp(-x))


def _silu(x):
    return x * _sigmoid(x)


def _softplus(x):
    return jnp.maximum(x, 0.0) + jnp.log(1.0 + jnp.exp(-jnp.abs(x)))


def _rmsnorm_fwd(x, g, name):
    t, d = x.shape
    tm = _row_tile(t)

    def body(x_ref, g_ref, o_ref):
        o_ref[...] = _rms(x_ref[...], g_ref[...]).astype(o_ref.dtype)

    return pl.pallas_call(
        body, name=name, grid=(t // tm,),
        in_specs=[pl.BlockSpec((tm, d), lambda i: (i, 0)), pl.BlockSpec((1, d), lambda i: (0, 0))],
        out_specs=pl.BlockSpec((tm, d), lambda i: (i, 0)),
        out_shape=jax.ShapeDtypeStruct((t, d), BF16),
        compiler_params=_cparams(("parallel",)),
    )(x, g)


def _rmsnorm_bwd(x, g, dh, dres, name):
    t, d = x.shape
    tm = _row_tile(t)

    def body(x_ref, g_ref, dh_ref, dres_ref, dx_ref, dxb_ref, dg_ref):
        _, vjp = jax.vjp(_rms, x_ref[...], g_ref[...])
        dx, dg = vjp(dh_ref[...])
        dx = dx + dres_ref[...]
        dx_ref[...] = dx
        dxb_ref[...] = dx.astype(BF16)

        @pl.when(pl.program_id(0) == 0)
        def _():
            dg_ref[...] = jnp.zeros_like(dg_ref)

        dg_ref[...] += dg

    return pl.pallas_call(
        body, name=name, grid=(t // tm,),
        in_specs=[pl.BlockSpec((tm, d), lambda i: (i, 0)), pl.BlockSpec((1, d), lambda i: (0, 0)),
                  pl.BlockSpec((tm, d), lambda i: (i, 0)), pl.BlockSpec((tm, d), lambda i: (i, 0))],
        out_specs=[pl.BlockSpec((tm, d), lambda i: (i, 0)), pl.BlockSpec((tm, d), lambda i: (i, 0)),
                   pl.BlockSpec((1, d), lambda i: (0, 0))],
        out_shape=[jax.ShapeDtypeStruct((t, d), F32), jax.ShapeDtypeStruct((t, d), BF16), jax.ShapeDtypeStruct((1, d), F32)],
        compiler_params=_cparams(("arbitrary",)),
    )(x, g, dh, dres)


def _matmul(terms, name, res=None, nt=False):
    norm = lambda op, axis: op if isinstance(op, tuple) else (op, op.shape[axis], 0)
    a_ops = [norm(a, 1) for a, _ in terms]
    b_ops = [norm(b, 1 if nt else 0) for _, b in terms]
    m = a_ops[0][0].shape[0]
    n = b_ops[0][0].shape[0 if nt else 1]
    a_row_bytes = sum(kw * a.dtype.itemsize for a, kw, _ in a_ops)
    tm = _row_tile(m, max(256, min(1024, MM_A_BLOCK_BYTES // a_row_bytes // 256 * 256)))
    k_total = sum(kw for _, kw, _ in b_ops)
    tn = _col_tile(n, max(LANES, min(1024, MM_B_BLOCK_BYTES // (2 * k_total) // LANES * LANES)))
    na = len(terms)
    dot = _bdot_nt if nt else _bdot

    def body(*refs):
        a_refs, b_refs = refs[:na], refs[na:2 * na]
        acc = dot(a_refs[0][...], b_refs[0][...])
        for a_ref, b_ref in zip(a_refs[1:], b_refs[1:]):
            acc = acc + dot(a_ref[...], b_ref[...])
        if res is not None:
            acc = acc + refs[2 * na][...]
        refs[-1][...] = acc

    in_specs = [pl.BlockSpec((tm, kw), functools.partial(lambda i, j, kb: (i, kb), kb=kb)) for _, kw, kb in a_ops]
    if nt:
        in_specs += [pl.BlockSpec((tn, kw), functools.partial(lambda i, j, kb: (j, kb), kb=kb)) for _, kw, kb in b_ops]
    else:
        in_specs += [pl.BlockSpec((kw, tn), functools.partial(lambda i, j, kb: (kb, j), kb=kb)) for _, kw, kb in b_ops]
    args = [a for a, _, _ in a_ops] + [b for b, _, _ in b_ops]
    if res is not None:
        in_specs.append(pl.BlockSpec((tm, tn), lambda i, j: (i, j)))
        args.append(res)
    return pl.pallas_call(
        body, name=name, grid=(m // tm, n // tn), in_specs=in_specs,
        out_specs=pl.BlockSpec((tm, tn), lambda i, j: (i, j)),
        out_shape=jax.ShapeDtypeStruct((m, n), F32),
        compiler_params=_cparams(("parallel", "parallel")),
    )(*args)


def _matmul_tn(a, b, name):
    m, k = a.shape
    n = b.shape[1]
    tm = _row_tile(m, 512)
    tn = _col_tile(n, 512)

    def body(a_ref, b_ref, o_ref):
        @pl.when(pl.program_id(1) == 0)
        def _():
            o_ref[...] = jnp.zeros_like(o_ref)

        o_ref[...] += _bdot_tn(a_ref[...], b_ref[...])

    return pl.pallas_call(
        body, name=name, grid=(n // tn, m // tm),
        in_specs=[pl.BlockSpec((tm, k), lambda j, i: (i, 0)), pl.BlockSpec((tm, tn), lambda j, i: (i, j))],
        out_specs=pl.BlockSpec((k, tn), lambda j, i: (0, j)),
        out_shape=jax.ShapeDtypeStruct((k, n), F32),
        compiler_params=_cparams(("parallel", "arbitrary")),
    )(a, b)


def _shift_down(x, k):
    row = lax.broadcasted_iota(jnp.int32, x.shape, 0)
    return jnp.where(row >= k, pltpu.roll(x, k, 0), 0.0)


def _shift_up(x, k):
    s = x.shape[0]
    row = lax.broadcasted_iota(jnp.int32, x.shape, 0)
    return jnp.where(row < s - k, pltpu.roll(x, s - k, 0), 0.0)


def _conv_taps(x, ntaps, left):
    out = []
    for j in range(ntaps):
        off = j - left
        out.append(_shift_down(x, -off) if off < 0 else (_shift_up(x, off) if off > 0 else x))
    return out


def _conv_fwd(x, w, left):
    taps = _conv_taps(x, w.shape[0], left)
    acc = taps[0] * w[0:1, :]
    for j in range(1, w.shape[0]):
        acc = acc + taps[j] * w[j:j + 1, :]
    return acc


def _conv_bwd(x, w, left, dout):
    ntaps = w.shape[0]
    dx = None
    for j in range(ntaps):
        off = j - left
        sh = _shift_up(dout, -off) if off < 0 else (_shift_down(dout, off) if off > 0 else dout)
        term = sh * w[j:j + 1, :]
        dx = term if dx is None else dx + term
    taps = _conv_taps(x, ntaps, left)
    dw = jnp.concatenate([jnp.sum(dout * tp, axis=0, keepdims=True) for tp in taps], axis=0)
    return dx, dw


def _scan(a, b, reverse):
    s = a.shape[0]
    d = 1
    sh = _shift_up if reverse else _shift_down
    while d < s:
        b = a * sh(b, d) + b
        a = a * sh(a, d)
        d *= 2
    return b


def _dn_gates_fn(pre, alog, dtb):
    lane = lax.broadcasted_iota(jnp.int32, pre.shape, 1)
    beta = _sigmoid(pre)
    g = -jnp.exp(alog) * _softplus(pre + dtb)
    return jnp.where(lane < N_GATE // 2, beta, jnp.where(lane < N_GATE, g, 0.0))


def _dn_gates_fwd(proj, alog, dtb, name):
    t = proj.shape[0]
    tm = _row_tile(t)
    cb = P_GATE // LANES

    def body(p_ref, a_ref, d_ref, o_ref):
        o_ref[...] = _dn_gates_fn(p_ref[...], a_ref[...], d_ref[...])

    return pl.pallas_call(
        body, name=name, grid=(t // tm,),
        in_specs=[pl.BlockSpec((tm, LANES), lambda i: (i, cb)), pl.BlockSpec((1, LANES), lambda i: (0, 0)),
                  pl.BlockSpec((1, LANES), lambda i: (0, 0))],
        out_specs=pl.BlockSpec((tm, LANES), lambda i: (i, 0)),
        out_shape=jax.ShapeDtypeStruct((t, LANES), F32),
        compiler_params=_cparams(("parallel",)),
    )(proj, alog, dtb)


def _dn_gates_bwd(proj, alog, dtb, dgb_f, dgb_b, name):
    t = proj.shape[0]
    tm = _row_tile(t)
    cb = P_GATE // LANES

    def body(p_ref, a_ref, d_ref, g1_ref, g2_ref, dp_ref, da_ref, dd_ref):
        _, vjp = jax.vjp(_dn_gates_fn, p_ref[...], a_ref[...], d_ref[...])
        dp, da, dd = vjp(g1_ref[...] + g2_ref[...])
        dp_ref[...] = dp.astype(dp_ref.dtype)

        @pl.when(pl.program_id(0) == 0)
        def _():
            da_ref[...] = jnp.zeros_like(da_ref)
            dd_ref[...] = jnp.zeros_like(dd_ref)

        da_ref[...] += da
        dd_ref[...] += dd

    row = pl.BlockSpec((tm, LANES), lambda i: (i, 0))
    vec = pl.BlockSpec((1, LANES), lambda i: (0, 0))
    return pl.pallas_call(
        body, name=name, grid=(t // tm,),
        in_specs=[pl.BlockSpec((tm, LANES), lambda i: (i, cb)), vec, vec, row, row],
        out_specs=[row, vec, vec],
        out_shape=[jax.ShapeDtypeStruct((t, LANES), BF16), jax.ShapeDtypeStruct((1, LANES), F32),
                   jax.ShapeDtypeStruct((1, LANES), F32)],
        compiler_params=_cparams(("arbitrary",)),
    )(proj, alog, dtb, dgb_f, dgb_b)


def _dn_prep_fn(x, w, is_qk):
    act = _silu(_conv_fwd(x, w, 2))
    nrm = act * lax.rsqrt(jnp.sum(act * act, axis=-1, keepdims=True) + EPS)
    return jnp.where(is_qk, nrm, act)


def _dn_prep_fwd(proj3, conv_w, name):
    bsz, s, _ = proj3.shape
    nblk = 3 * DN_WIDTH // LANES
    cb = P_Q // LANES

    def body(x_ref, w_ref, o_ref):
        o_ref[0] = _dn_prep_fn(x_ref[0], w_ref[...], pl.program_id(1) < 2 * DN_HEADS)

    return pl.pallas_call(
        body, name=name, grid=(bsz, nblk),
        in_specs=[pl.BlockSpec((1, s, LANES), lambda b, j: (b, 0, cb + j)), pl.BlockSpec((4, LANES), lambda b, j: (0, j))],
        out_specs=pl.BlockSpec((1, s, LANES), lambda b, j: (b, 0, j)),
        out_shape=jax.ShapeDtypeStruct((bsz, s, 3 * DN_WIDTH), F32),
        compiler_params=_cparams(("parallel", "parallel")),
    )(proj3, conv_w)


def _dn_prep_bwd(proj3, conv_w, dqkv_f, dqkv_b, name):
    bsz, s, _ = proj3.shape
    nblk = 3 * DN_WIDTH // LANES
    cb = P_Q // LANES

    def body(x_ref, w_ref, d1_ref, d2_ref, dx_ref, dw_ref):
        x, w, d = x_ref[0], w_ref[...], d1_ref[0] + d2_ref[0]
        is_qk = pl.program_id(0) < 2 * DN_HEADS
        pre = _conv_fwd(x, w, 2)

        def post(pre):
            act = _silu(pre)
            nrm = act * lax.rsqrt(jnp.sum(act * act, axis=-1, keepdims=True) + EPS)
            return jnp.where(is_qk, nrm, act)

        _, vjp = jax.vjp(post, pre)
        (dpre,) = vjp(d)
        dx, dw = _conv_bwd(x, w, 2, dpre)
        dx_ref[0] = dx.astype(dx_ref.dtype)

        @pl.when(pl.program_id(1) == 0)
        def _():
            dw_ref[...] = jnp.zeros_like(dw_ref)

        dw_ref[...] += dw

    col = pl.BlockSpec((1, s, LANES), lambda j, b: (b, 0, j))
    return pl.pallas_call(
        body, name=name, grid=(nblk, bsz),
        in_specs=[pl.BlockSpec((1, s, LANES), lambda j, b: (b, 0, cb + j)), pl.BlockSpec((4, LANES), lambda j, b: (0, j)), col, col],
        out_specs=[col, pl.BlockSpec((4, LANES), lambda j, b: (0, j))],
        out_shape=[jax.ShapeDtypeStruct((bsz, s, 3 * DN_WIDTH), BF16), jax.ShapeDtypeStruct((4, 3 * DN_WIDTH), F32)],
        compiler_params=_cparams(("parallel", "arbitrary")),
    )(proj3, conv_w, dqkv_f, dqkv_b)


def _parts(x, n):
    out = []
    for _ in range(n):
        bits = lax.bitcast_convert_type(x, jnp.uint32) & jnp.uint32(0xFFFF0000)
        t = lax.bitcast_convert_type(bits, F32)
        out.append(t.astype(BF16))
        x = x - t
    return out


def _dg(x, y, cx, cy):
    return lax.dot_general(x, y, (((cx + 1,), (cy + 1,)), ((0,), (0,))), preferred_element_type=F32)


def _bmm(a, b):
    return _dg(a.astype(BF16), b.astype(BF16), 1, 0)


def _bmm_nt(a, b):
    return _dg(a.astype(BF16), b.astype(BF16), 1, 1)


def _bmm_tn(a, b):
    return _dg(a.astype(BF16), b.astype(BF16), 0, 0)


def _dot3_raw(a, b, ca, cb):
    a_hi, a_lo = _parts(a, 2)
    b_hi, b_lo = _parts(b, 2)
    return _dg(a_hi, b_hi, ca, cb) + (_dg(a_hi, b_lo, ca, cb) + _dg(a_lo, b_hi, ca, cb))


@jax.custom_vjp
def _dot3(a, b):
    return _dot3_raw(a, b, 1, 0)


def _dot3_fwd(a, b):
    return _dot3_raw(a, b, 1, 0), (a, b)


def _dot3_bwd(res, ct):
    a, b = res
    return _dot3_raw(ct, b, 1, 1), _dot3_raw(a, ct, 0, 0)


_dot3.defvjp(_dot3_fwd, _dot3_bwd)


@jax.custom_vjp
def _sum_left(m, x):
    return sum(_dg(m, pt, 1, 0) for pt in _parts(x, 3))


def _sum_left_fwd(m, x):
    return _sum_left(m, x), m


def _sum_left_bwd(m, ct):
    return jnp.zeros_like(m), sum(_dg(m, pt, 0, 0) for pt in _parts(ct, 2))


_sum_left.defvjp(_sum_left_fwd, _sum_left_bwd)


@jax.custom_vjp
def _sum_right(x, m):
    return sum(_dg(pt, m, 0, 0) for pt in _parts(x, 3))


def _sum_right_fwd(x, m):
    return _sum_right(x, m), m


def _sum_right_bwd(m, ct):
    return sum(_dg(m, pt, 1, 1) for pt in _parts(ct, 2)), jnp.zeros_like(m)


_sum_right.defvjp(_sum_right_fwd, _sum_right_bwd)


def _dn_intra(q, k, v, g, beta, rev):
    nu, c, _ = q.shape
    row = lax.broadcasted_iota(jnp.int32, (nu, c, c), 1)
    col = lax.broadcasted_iota(jnp.int32, (nu, c, c), 2)
    incl = (row <= col) if rev else (row >= col)
    strict = (row < col) if rev else (row > col)
    ones_incl = jnp.where(incl, 1.0, 0.0).astype(BF16)
    ones_tr = jnp.where((row >= col) if rev else (row <= col), 1.0, 0.0).astype(BF16)
    gbc = jnp.broadcast_to(g, (nu, c, c))
    gc = _sum_left(ones_incl, gbc)
    gr = _sum_right(gbc, ones_tr)
    gcum = gc[:, :, 0:1]
    decay = jnp.where(incl, jnp.exp(jnp.where(incl, gc - gr, 0.0)), 0.0)
    qs = q * (DN_HEAD_DIM ** -0.5)
    kb = k * beta
    a = jnp.where(strict, _bmm_nt(kb, k) * decay, 0.0)
    same = lambda n: (row // n) == (col // n)
    d = jnp.where(same(16), a, 0.0)
    tinv = jnp.where(row == col, 1.0, 0.0) - d
    pw = _dot3(d, d)
    tinv = tinv + _dot3(tinv, pw)
    pw = _dot3(pw, pw)
    tinv = tinv + _dot3(tinv, pw)
    pw = _dot3(pw, pw)
    tinv = tinv + _dot3(tinv, pw)
    n = 16
    while n < c:
        e = jnp.where(same(2 * n) & jnp.logical_not(same(n)), a, 0.0)
        tinv = tinv - _bmm(tinv, _bmm(e, tinv))
        n *= 2
    egc = jnp.exp(gcum)
    u = _bmm(tinv, v * beta)
    w = _bmm(tinv, kb * egc)
    attn = _bmm_nt(qs, k) * decay
    glast = gcum[:, 0:1, :] if rev else gcum[:, c - 1:c, :]
    q_dec = qs * egc
    k_dec = k * jnp.exp(glast - gcum)
    cdec = jnp.broadcast_to(jnp.exp(glast), (nu, 1, LANES))
    return u, w, q_dec, k_dec, attn, cdec


def _dn_rec(u, w, q_dec, k_dec, attn, cdec, state):
    v_new = u - _bmm(w, state)
    o = _bmm(q_dec, state) + _bmm(attn, v_new)
    return o, state * cdec + _bmm_tn(k_dec, v_new)


DN_INTRA_TOKENS = 256
DN_REC_TOKENS = 512


def _dn_gate_lanes(rev, h):
    lb = (DN_HEADS if rev else 0) + h
    return lb, N_GATE // 2 + lb


def _dn_intra_shapes(bsz, s):
    n = s // CHUNK
    return [jax.ShapeDtypeStruct((bsz, s, DN_WIDTH), F32), jax.ShapeDtypeStruct((bsz, s, DN_WIDTH), BF16),
            jax.ShapeDtypeStruct((bsz, s, DN_WIDTH), BF16), jax.ShapeDtypeStruct((bsz, s, DN_WIDTH), BF16),
            jax.ShapeDtypeStruct((bsz, n, DN_HEADS, CHUNK, CHUNK), BF16), jax.ShapeDtypeStruct((bsz, n, DN_HEADS, 1, LANES), F32)]


def _dn_intra_specs(tb, ix):
    nc = tb // CHUNK
    ix5 = lambda b, j: ix(b, j) + (0, 0)
    row = pl.BlockSpec((1, tb, DN_WIDTH), ix)
    return [row, row, row, row, pl.BlockSpec((1, nc, DN_HEADS, CHUNK, CHUNK), ix5), pl.BlockSpec((1, nc, DN_HEADS, 1, LANES), ix5)]


def _dn_units(nc):
    return [(ci, h) for ci in range(nc) for h in range(DN_HEADS)]


def _dn_load_units(qkv_ref, gb_ref, nc, rev):
    qs, ks, vs, gs, bs = [], [], [], [], []
    for ci, h in _dn_units(nc):
        rows = slice(ci * CHUNK, (ci + 1) * CHUNK)
        lb, lg = _dn_gate_lanes(rev, h)
        qs.append(qkv_ref[0, rows, h * LANES:(h + 1) * LANES])
        ks.append(qkv_ref[0, rows, DN_WIDTH + h * LANES:DN_WIDTH + (h + 1) * LANES])
        vs.append(qkv_ref[0, rows, 2 * DN_WIDTH + h * LANES:2 * DN_WIDTH + (h + 1) * LANES])
        gs.append(gb_ref[0, rows, lg:lg + 1])
        bs.append(gb_ref[0, rows, lb:lb + 1])
    return jnp.stack(qs), jnp.stack(ks), jnp.stack(vs), jnp.stack(gs), jnp.stack(bs)


def _dn_intra_fwd(qkv, gb, rev, name):
    bsz, s, _ = qkv.shape
    tb = min(DN_INTRA_TOKENS, s)
    nc = tb // CHUNK

    def body(qkv_ref, gb_ref, u_ref, w_ref, qd_ref, kd_ref, at_ref, cd_ref):
        q, k, v, g, beta = _dn_load_units(qkv_ref, gb_ref, nc, rev)
        u, w, qd, kd, at, cd = _dn_intra(q, k, v, g, beta, rev)
        for i, (ci, h) in enumerate(_dn_units(nc)):
            rows = slice(ci * CHUNK, (ci + 1) * CHUNK)
            cols = slice(h * LANES, (h + 1) * LANES)
            u_ref[0, rows, cols] = u[i]
            w_ref[0, rows, cols] = w[i].astype(BF16)
            qd_ref[0, rows, cols] = qd[i].astype(BF16)
            kd_ref[0, rows, cols] = kd[i].astype(BF16)
            at_ref[0, ci, h] = at[i].astype(BF16)
            cd_ref[0, ci, h] = cd[i]

    ix = lambda b, j: (b, j, 0)
    return pl.pallas_call(
        body, name=name, grid=(bsz, s // tb),
        in_specs=[pl.BlockSpec((1, tb, 3 * DN_WIDTH), ix), pl.BlockSpec((1, tb, LANES), ix)],
        out_specs=_dn_intra_specs(tb, ix), out_shape=_dn_intra_shapes(bsz, s),
        compiler_params=_cparams(("parallel", "parallel")),
    )(qkv, gb)


def _dn_intra_bwd(qkv, gb, cts, rev, name):
    bsz, s, _ = qkv.shape
    tb = min(DN_INTRA_TOKENS, s)
    nc = tb // CHUNK

    def body(qkv_ref, gb_ref, du_ref, dw_ref, dqd_ref, dkd_ref, dat_ref, dcd_ref, dqkv_ref, dgb_ref):
        units = _dn_units(nc)
        q, k, v, g, beta = _dn_load_units(qkv_ref, gb_ref, nc, rev)
        _, vjp = jax.vjp(functools.partial(_dn_intra, rev=rev), q, k, v, g, beta)
        tok = lambda ref: jnp.stack([ref[0, ci * CHUNK:(ci + 1) * CHUNK, h * LANES:(h + 1) * LANES] for ci, h in units])
        per = lambda ref: jnp.stack([ref[0, ci, h] for ci, h in units])
        dq, dk, dv, dg, dbeta = vjp((tok(du_ref), tok(dw_ref), tok(dqd_ref), tok(dkd_ref), per(dat_ref), per(dcd_ref)))
        lane = lax.broadcasted_iota(jnp.int32, (CHUNK, LANES), 1)
        for ci in range(nc):
            rows = slice(ci * CHUNK, (ci + 1) * CHUNK)
            dgates = jnp.zeros((CHUNK, LANES), F32)
            for h in range(DN_HEADS):
                i = units.index((ci, h))
                lb, lg = _dn_gate_lanes(rev, h)
                dqkv_ref[0, rows, h * LANES:(h + 1) * LANES] = dq[i]
                dqkv_ref[0, rows, DN_WIDTH + h * LANES:DN_WIDTH + (h + 1) * LANES] = dk[i]
                dqkv_ref[0, rows, 2 * DN_WIDTH + h * LANES:2 * DN_WIDTH + (h + 1) * LANES] = dv[i]
                dgates = dgates + jnp.where(lane == lb, dbeta[i], 0.0) + jnp.where(lane == lg, dg[i], 0.0)
            dgb_ref[0, rows, :] = dgates

    ix = lambda b, j: (b, j, 0)
    ix5 = lambda b, j: (b, j, 0, 0, 0)
    row = pl.BlockSpec((1, tb, DN_WIDTH), ix)
    return pl.pallas_call(
        body, name=name, grid=(bsz, s // tb),
        in_specs=[pl.BlockSpec((1, tb, 3 * DN_WIDTH), ix), pl.BlockSpec((1, tb, LANES), ix), row, row, row, row,
                  pl.BlockSpec((1, nc, DN_HEADS, CHUNK, CHUNK), ix5), pl.BlockSpec((1, nc, DN_HEADS, 1, LANES), ix5)],
        out_specs=[pl.BlockSpec((1, tb, 3 * DN_WIDTH), ix), pl.BlockSpec((1, tb, LANES), ix)],
        out_shape=[jax.ShapeDtypeStruct((bsz, s, 3 * DN_WIDTH), F32), jax.ShapeDtypeStruct((bsz, s, LANES), F32)],
        compiler_params=_cparams(("parallel", "parallel")),
    )(qkv, gb, *cts)


def _dn_rec_fwd(intra, rev, name):
    u = intra[0]
    bsz, s, _ = u.shape
    tb = min(DN_REC_TOKENS, s)
    nt = s // tb
    nc = tb // CHUNK

    def body(u_ref, w_ref, qd_ref, kd_ref, at_ref, cd_ref, o_ref, st_ref, state):
        @pl.when(pl.program_id(1) == 0)
        def _():
            state[...] = jnp.zeros_like(state)

        def step(ci, carry):
            cidx = (nc - 1 - ci) if rev else ci
            rows = pl.ds(pl.multiple_of(cidx * CHUNK, CHUNK), CHUNK)
            heads = lambda ref: jnp.stack([ref[0, rows, h * LANES:(h + 1) * LANES] for h in range(DN_HEADS)])
            st = state[...]
            o, new_state = _dn_rec(heads(u_ref), heads(w_ref), heads(qd_ref), heads(kd_ref), at_ref[0, cidx], cd_ref[0, cidx], st)
            st_ref[0, cidx] = st
            state[...] = new_state
            o_ref[0, rows, :] = jnp.concatenate([o[h] for h in range(DN_HEADS)], axis=-1)
            return carry

        lax.fori_loop(0, nc, step, 0)

    ix = (lambda b, j: (b, nt - 1 - j, 0)) if rev else (lambda b, j: (b, j, 0))
    ix5 = lambda b, j: ix(b, j) + (0, 0)
    return pl.pallas_call(
        body, name=name, grid=(bsz, nt), in_specs=_dn_intra_specs(tb, ix),
        out_specs=[pl.BlockSpec((1, tb, DN_WIDTH), ix), pl.BlockSpec((1, nc, DN_HEADS, DN_HEAD_DIM, DN_HEAD_DIM), ix5)],
        out_shape=[jax.ShapeDtypeStruct((bsz, s, DN_WIDTH), F32),
                   jax.ShapeDtypeStruct((bsz, s // CHUNK, DN_HEADS, DN_HEAD_DIM, DN_HEAD_DIM), F32)],
        scratch_shapes=[pltpu.VMEM((DN_HEADS, DN_HEAD_DIM, DN_HEAD_DIM), F32)],
        compiler_params=_cparams(("parallel", "arbitrary")),
    )(*intra)


def _dn_rec_bwd(intra, states, do, rev, name):
    u = intra[0]
    bsz, s, _ = u.shape
    tb = min(DN_REC_TOKENS, s)
    nt = s // tb
    nc = tb // CHUNK

    def body(u_ref, w_ref, qd_ref, kd_ref, at_ref, cd_ref, st_ref, do_ref,
             du_ref, dw_ref, dqd_ref, dkd_ref, dat_ref, dcd_ref, dstate):
        @pl.when(pl.program_id(1) == 0)
        def _():
            dstate[...] = jnp.zeros_like(dstate)

        def step(ci, carry):
            cidx = ci if rev else (nc - 1 - ci)
            rows = pl.ds(pl.multiple_of(cidx * CHUNK, CHUNK), CHUNK)
            heads = lambda ref: jnp.stack([ref[0, rows, h * LANES:(h + 1) * LANES] for h in range(DN_HEADS)])
            args = (heads(u_ref), heads(w_ref).astype(F32), heads(qd_ref).astype(F32), heads(kd_ref).astype(F32),
                    at_ref[0, cidx].astype(F32), cd_ref[0, cidx], st_ref[0, cidx])
            _, vjp = jax.vjp(_dn_rec, *args)
            du, dw, dqd, dkd, dat, dcd, dst = vjp((heads(do_ref), dstate[...]))
            dat_ref[0, cidx] = dat
            dcd_ref[0, cidx] = dcd
            dstate[...] = dst
            for ref, val in ((du_ref, du), (dw_ref, dw), (dqd_ref, dqd), (dkd_ref, dkd)):
                ref[0, rows, :] = jnp.concatenate([val[h] for h in range(DN_HEADS)], axis=-1)
            return carry

        lax.fori_loop(0, nc, step, 0)

    ix = (lambda b, j: (b, j, 0)) if rev else (lambda b, j: (b, nt - 1 - j, 0))
    ix5 = lambda b, j: ix(b, j) + (0, 0)
    row = pl.BlockSpec((1, tb, DN_WIDTH), ix)
    f32 = lambda sd: jax.ShapeDtypeStruct(sd.shape, F32)
    return pl.pallas_call(
        body, name=name, grid=(bsz, nt),
        in_specs=_dn_intra_specs(tb, ix) + [pl.BlockSpec((1, nc, DN_HEADS, DN_HEAD_DIM, DN_HEAD_DIM), ix5), row],
        out_specs=_dn_intra_specs(tb, ix), out_shape=[f32(sd) for sd in _dn_intra_shapes(bsz, s)],
        scratch_shapes=[pltpu.VMEM((DN_HEADS, DN_HEAD_DIM, DN_HEAD_DIM), F32)],
        compiler_params=_cparams(("parallel", "arbitrary")),
    )(*intra, states, do)


def _lru_gate_fn(xc, wa, ba, wx, bx, lam):
    r = _sigmoid(_bdot(xc, wa) + ba)
    ig = _sigmoid(_bdot(xc, wx) + bx)
    log_a = -LRU_C * r * _softplus(-lam)
    a = jnp.exp(log_a)
    b = jnp.sqrt(-jnp.tanh(log_a) * (a * a + 1.0)) * (ig * xc)
    return a, b


def _lru_fwd(proj3, conv_w, conv_b, wa, ba, wx, bx, lam, name):
    bsz, s, _ = proj3.shape
    nblk = LRU_WIDTH // LANES

    def body(x_ref, cw_ref, cb_ref, wa_ref, ba_ref, wx_ref, bx_ref, lam_ref, hf_ref, hb_ref):
        xc = _conv_fwd(x_ref[0], cw_ref[...], 2) + cb_ref[...]
        for d, h_ref in ((0, hf_ref), (1, hb_ref)):
            a, b = _lru_gate_fn(xc, wa_ref[d, 0], ba_ref[d:d + 1, :], wx_ref[d, 0], bx_ref[d:d + 1, :], lam_ref[d:d + 1, :])
            h_ref[0] = _scan(a, b, reverse=(d == 1))

    col = pl.BlockSpec((1, s, LANES), lambda b, j: (b, 0, j))
    vec2 = pl.BlockSpec((2, LANES), lambda b, j: (0, j))
    wspec = pl.BlockSpec((2, 1, LANES, LANES), lambda b, j: (0, j, 0, 0))
    return pl.pallas_call(
        body, name=name, grid=(bsz, nblk),
        in_specs=[col, pl.BlockSpec((4, LANES), lambda b, j: (0, j)), pl.BlockSpec((1, LANES), lambda b, j: (0, j)),
                  wspec, vec2, wspec, vec2, vec2],
        out_specs=[col, col],
        out_shape=[jax.ShapeDtypeStruct((bsz, s, LRU_WIDTH), F32)] * 2,
        compiler_params=_cparams(("parallel", "parallel")),
    )(proj3, conv_w, conv_b, wa, ba, wx, bx, lam)


def _lru_bwd(proj3, conv_w, conv_b, wa, ba, wx, bx, lam, hf, hb, dh, name):
    bsz, s, _ = proj3.shape
    nblk = LRU_WIDTH // LANES

    def body(x_ref, cw_ref, cb_ref, wa_ref, ba_ref, wx_ref, bx_ref, lam_ref, hf_ref, hb_ref, dh_ref,
             dx_ref, dcw_ref, dcb_ref, dwa_ref, dba_ref, dwx_ref, dbx_ref, dlam_ref):
        @pl.when(pl.program_id(1) == 0)
        def _():
            for r in (dcw_ref, dcb_ref, dwa_ref, dba_ref, dwx_ref, dbx_ref, dlam_ref):
                r[...] = jnp.zeros_like(r)

        x, cw = x_ref[0], cw_ref[...]
        xc = _conv_fwd(x, cw, 2) + cb_ref[...]
        dhv = dh_ref[0]
        dxc = jnp.zeros_like(xc)
        for d, h_ref in ((0, hf_ref), (1, hb_ref)):
            rev = d == 1
            args = (xc, wa_ref[d, 0].astype(F32), ba_ref[d:d + 1, :], wx_ref[d, 0].astype(F32), bx_ref[d:d + 1, :],
                    lam_ref[d:d + 1, :])
            (a, _), vjp = jax.vjp(_lru_gate_fn, *args)
            h = h_ref[0]
            a_next = _shift_down(a, 1) if rev else _shift_up(a, 1)
            lam_adj = _scan(a_next, dhv, reverse=not rev)
            h_prev = _shift_up(h, 1) if rev else _shift_down(h, 1)
            dxc_d, dwa, dba, dwx, dbx, dlam = vjp((lam_adj * h_prev, lam_adj))
            dxc = dxc + dxc_d
            dwa_ref[d, 0] += dwa
            dwx_ref[d, 0] += dwx
            dba_ref[d:d + 1, :] += dba
            dbx_ref[d:d + 1, :] += dbx
            dlam_ref[d:d + 1, :] += dlam
        dx, dcw = _conv_bwd(x, cw, 2, dxc)
        dx_ref[0] = dx.astype(dx_ref.dtype)
        dcw_ref[...] += dcw
        dcb_ref[...] += jnp.sum(dxc, axis=0, keepdims=True)

    col = pl.BlockSpec((1, s, LANES), lambda j, b: (b, 0, j))
    vec1 = pl.BlockSpec((1, LANES), lambda j, b: (0, j))
    vec2 = pl.BlockSpec((2, LANES), lambda j, b: (0, j))
    vec4 = pl.BlockSpec((4, LANES), lambda j, b: (0, j))
    wspec = pl.BlockSpec((2, 1, LANES, LANES), lambda j, b: (0, j, 0, 0))
    wshape = jax.ShapeDtypeStruct((2, nblk, LANES, LANES), F32)
    v2shape = jax.ShapeDtypeStruct((2, LRU_WIDTH), F32)
    return pl.pallas_call(
        body, name=name, grid=(nblk, bsz),
        in_specs=[col, vec4, vec1, wspec, vec2, wspec, vec2, vec2, col, col, col],
        out_specs=[col, vec4, vec1, wspec, vec2, wspec, vec2, vec2],
        out_shape=[jax.ShapeDtypeStruct((bsz, s, LRU_WIDTH), BF16), jax.ShapeDtypeStruct((4, LRU_WIDTH), F32),
                   jax.ShapeDtypeStruct((1, LRU_WIDTH), F32), wshape, v2shape, wshape, v2shape, v2shape],
        compiler_params=_cparams(("parallel", "arbitrary")),
    )(proj3, conv_w, conv_b, wa, ba, wx, bx, lam, hf, hb, dh)


def _mix_fn(o_f, o_b, z, lg, hf, hb, dn_g, lru_g):
    osum = o_f + o_b
    heads = []
    for h in range(DN_HEADS):
        sl = slice(h * LANES, (h + 1) * LANES)
        heads.append(_rms(osum[:, sl], dn_g) * _silu(z[:, sl]))
    lru = _rms(_gelu(lg) * (hf + hb), lru_g)
    return jnp.concatenate(heads + [lru], axis=-1)


def _mix_specs(tm):
    w = DN_WIDTH
    row = pl.BlockSpec((tm, w), lambda i: (i, 0))
    z = pl.BlockSpec((tm, w), lambda i: (i, P_Z // w))
    lg = pl.BlockSpec((tm, w), lambda i: (i, P_LG // w))
    return [row, row, z, lg, row, row, pl.BlockSpec((1, LANES), lambda i: (0, 0)), pl.BlockSpec((1, w), lambda i: (0, 0))]


def _mix_fwd(o_f, o_b, proj, hf, hb, dn_g, lru_g, name):
    t = proj.shape[0]
    tm = _row_tile(t)

    def body(of_ref, ob_ref, z_ref, lg_ref, hf_ref, hb_ref, dg_ref, lgn_ref, o_ref):
        o_ref[...] = _mix_fn(of_ref[...], ob_ref[...], z_ref[...], lg_ref[...], hf_ref[...], hb_ref[...],
                             dg_ref[...], lgn_ref[...]).astype(o_ref.dtype)

    return pl.pallas_call(
        body, name=name, grid=(t // tm,), in_specs=_mix_specs(tm),
        out_specs=pl.BlockSpec((tm, D_MODEL), lambda i: (i, 0)),
        out_shape=jax.ShapeDtypeStruct((t, D_MODEL), BF16),
        compiler_params=_cparams(("parallel",)),
    )(o_f, o_b, proj, proj, hf, hb, dn_g, lru_g)


def _mix_bwd(o_f, o_b, proj, hf, hb, dn_g, lru_g, dmix, name):
    t = proj.shape[0]
    tm = _row_tile(t)

    def body(of_ref, ob_ref, z_ref, lg_ref, hf_ref, hb_ref, dg_ref, lgn_ref, dm_ref,
             do_ref, dz_ref, dlg_ref, dh_ref, ddg_ref, dlgn_ref):
        _, vjp = jax.vjp(_mix_fn, of_ref[...], ob_ref[...], z_ref[...], lg_ref[...], hf_ref[...], hb_ref[...],
                         dg_ref[...], lgn_ref[...])
        do, _, dz, dlg, dh, _, ddg, dlgn = vjp(dm_ref[...])
        do_ref[...] = do
        dz_ref[...] = dz.astype(dz_ref.dtype)
        dlg_ref[...] = dlg.astype(dlg_ref.dtype)
        dh_ref[...] = dh

        @pl.when(pl.program_id(0) == 0)
        def _():
            ddg_ref[...] = jnp.zeros_like(ddg_ref)
            dlgn_ref[...] = jnp.zeros_like(dlgn_ref)

        ddg_ref[...] += ddg
        dlgn_ref[...] += dlgn

    row = pl.BlockSpec((tm, DN_WIDTH), lambda i: (i, 0))
    return pl.pallas_call(
        body, name=name, grid=(t // tm,),
        in_specs=_mix_specs(tm) + [pl.BlockSpec((tm, D_MODEL), lambda i: (i, 0))],
        out_specs=[row, row, row, row, pl.BlockSpec((1, LANES), lambda i: (0, 0)), pl.BlockSpec((1, DN_WIDTH), lambda i: (0, 0))],
        out_shape=[jax.ShapeDtypeStruct((t, DN_WIDTH), dt) for dt in (F32, BF16, BF16, F32)]
        + [jax.ShapeDtypeStruct((1, LANES), F32), jax.ShapeDtypeStruct((1, DN_WIDTH), F32)],
        compiler_params=_cparams(("arbitrary",)),
    )(o_f, o_b, proj, proj, hf, hb, dn_g, lru_g, dmix)


def _ffn_act_fwd(g3, u3, conv_w, conv_b, name):
    bsz, s, f = g3.shape
    nblk = f // LANES

    def body(g_ref, u_ref, w_ref, b_ref, o_ref):
        gate = _conv_fwd(g_ref[0], w_ref[...], 1) + b_ref[...]
        o_ref[0] = (_gelu(gate) * u_ref[0]).astype(o_ref.dtype)

    col = pl.BlockSpec((1, s, LANES), lambda b, j: (b, 0, j))
    return pl.pallas_call(
        body, name=name, grid=(bsz, nblk),
        in_specs=[col, col, pl.BlockSpec((3, LANES), lambda b, j: (0, j)), pl.BlockSpec((1, LANES), lambda b, j: (0, j))],
        out_specs=col, out_shape=jax.ShapeDtypeStruct((bsz, s, f), BF16),
        compiler_params=_cparams(("parallel", "parallel")),
    )(g3, u3, conv_w, conv_b)


def _ffn_act_bwd(g3, u3, conv_w, conv_b, dact3, name):
    bsz, s, f = g3.shape
    nblk = f // LANES

    def body(g_ref, u_ref, w_ref, b_ref, d_ref, dg_ref, du_ref, dw_ref, db_ref):
        g, w, u = g_ref[0], w_ref[...], u_ref[0]
        gate = _conv_fwd(g, w, 1) + b_ref[...]
        _, vjp = jax.vjp(lambda gt, uu: _gelu(gt) * uu, gate, u)
        dgate, du = vjp(d_ref[0])
        dg, dw = _conv_bwd(g, w, 1, dgate)
        dg_ref[0] = dg.astype(dg_ref.dtype)
        du_ref[0] = du.astype(du_ref.dtype)

        @pl.when(pl.program_id(1) == 0)
        def _():
            dw_ref[...] = jnp.zeros_like(dw_ref)
            db_ref[...] = jnp.zeros_like(db_ref)

        dw_ref[...] += dw
        db_ref[...] += jnp.sum(dgate, axis=0, keepdims=True)

    col = pl.BlockSpec((1, s, LANES), lambda j, b: (b, 0, j))
    w3 = pl.BlockSpec((3, LANES), lambda j, b: (0, j))
    w1 = pl.BlockSpec((1, LANES), lambda j, b: (0, j))
    return pl.pallas_call(
        body, name=name, grid=(nblk, bsz),
        in_specs=[col, col, w3, w1, col], out_specs=[col, col, w3, w1],
        out_shape=[jax.ShapeDtypeStruct((bsz, s, f), BF16), jax.ShapeDtypeStruct((bsz, s, f), BF16),
                   jax.ShapeDtypeStruct((3, f), F32), jax.ShapeDtypeStruct((1, f), F32)],
        compiler_params=_cparams(("parallel", "arbitrary")),
    )(g3, u3, conv_w, conv_b, dact3)


def _ple_fn(r, pg, pp, bg):
    return r + _sigmoid(pg + bg) * pp


def _ple_fwd(r, pg, pp, bg, name):
    t, d = r.shape
    tm = _row_tile(t)

    def body(r_ref, pg_ref, pp_ref, bg_ref, o_ref):
        o_ref[...] = _ple_fn(r_ref[...], pg_ref[...], pp_ref[...], bg_ref[...])

    row = pl.BlockSpec((tm, d), lambda i: (i, 0))
    return pl.pallas_call(
        body, name=name, grid=(t // tm,), in_specs=[row, row, row, pl.BlockSpec((1, d), lambda i: (0, 0))],
        out_specs=row, out_shape=jax.ShapeDtypeStruct((t, d), F32),
        compiler_params=_cparams(("parallel",)),
    )(r, pg, pp, bg)


def _ple_bwd(pg, pp, bg, dr, name):
    t, d = pg.shape
    tm = _row_tile(t)

    def body(pg_ref, pp_ref, bg_ref, dr_ref, dpg_ref, dpp_ref, dbg_ref):
        _, vjp = jax.vjp(lambda a, b, c: _sigmoid(a + c) * b, pg_ref[...], pp_ref[...], bg_ref[...])
        dpg, dpp, dbg = vjp(dr_ref[...])
        dpg_ref[...] = dpg.astype(dpg_ref.dtype)
        dpp_ref[...] = dpp.astype(dpp_ref.dtype)

        @pl.when(pl.program_id(0) == 0)
        def _():
            dbg_ref[...] = jnp.zeros_like(dbg_ref)

        dbg_ref[...] += dbg

    row = pl.BlockSpec((tm, d), lambda i: (i, 0))
    vec = pl.BlockSpec((1, d), lambda i: (0, 0))
    return pl.pallas_call(
        body, name=name, grid=(t // tm,), in_specs=[row, row, vec, row], out_specs=[row, row, vec],
        out_shape=[jax.ShapeDtypeStruct((t, d), BF16), jax.ShapeDtypeStruct((t, d), BF16), jax.ShapeDtypeStruct((1, d), F32)],
        compiler_params=_cparams(("arbitrary",)),
    )(pg, pp, bg, dr)


def _loss_head(r, g, target, name):
    t, d = r.shape
    tm = _row_tile(t)

    def loss_fn(x, gg, tgt):
        err = _rms(x, gg) - tgt
        return 0.5 * jnp.sum(jnp.sum(err * err, axis=-1, keepdims=True) * (1.0 / d), axis=0, keepdims=True)

    def body(r_ref, g_ref, t_ref, l_ref, dr_ref, dg_ref):
        val, vjp = jax.vjp(lambda x, gg: loss_fn(x, gg, t_ref[...]), r_ref[...], g_ref[...])
        dx, dg = vjp(jnp.ones((1, 1), F32))
        dr_ref[...] = dx

        @pl.when(pl.program_id(0) == 0)
        def _():
            l_ref[...] = jnp.zeros_like(l_ref)
            dg_ref[...] = jnp.zeros_like(dg_ref)

        l_ref[...] += val
        dg_ref[...] += dg

    row = pl.BlockSpec((tm, d), lambda i: (i, 0))
    vec = pl.BlockSpec((1, d), lambda i: (0, 0))
    one = pl.BlockSpec((1, 1), lambda i: (0, 0))
    return pl.pallas_call(
        body, name=name, grid=(t // tm,), in_specs=[row, vec, row], out_specs=[one, row, vec],
        out_shape=[jax.ShapeDtypeStruct((1, 1), F32), jax.ShapeDtypeStruct((t, d), F32), jax.ShapeDtypeStruct((1, d), F32)],
        compiler_params=_cparams(("arbitrary",)),
    )(r, g, target)


def _adamw_math(w, gg, m, v, d_ref, nm_ref, nv_ref):
    nm = ADAM_B1 * m + (1.0 - ADAM_B1) * gg
    nv = ADAM_B2 * v + (1.0 - ADAM_B2) * (gg * gg)
    m_hat = nm / (1.0 - ADAM_B1 ** ADAM_STEP)
    v_hat = nv / (1.0 - ADAM_B2 ** ADAM_STEP)
    d_ref[...] = -ADAM_LR * (m_hat / (jnp.sqrt(v_hat) + ADAM_EPS) + ADAM_WD * w)
    nm_ref[...] = nm
    nv_ref[...] = nv


def _adamw(w, g, m, v, name):
    r, c = w.shape
    tr = r if r <= 512 else _row_tile(r, 512)

    def body(w_ref, g_ref, m_ref, v_ref, d_ref, nm_ref, nv_ref):
        _adamw_math(w_ref[...], g_ref[...], m_ref[...], v_ref[...], d_ref, nm_ref, nv_ref)

    blk = pl.BlockSpec((tr, c), lambda i: (i, 0))
    return pl.pallas_call(
        body, name=name, grid=(r // tr,), in_specs=[blk] * 4, out_specs=[blk] * 3,
        out_shape=[jax.ShapeDtypeStruct((r, c), F32)] * 3,
        compiler_params=_cparams(("parallel",)),
    )(w, g, m, v)


def _prepare_weights(w):
    nl = w["w_in"].shape[0]
    w_in = w["w_in"].astype(BF16)
    pad = jnp.zeros(w_in.shape[:2] + (P_COLS - w_in.shape[2],), BF16)
    split = 4 * DN_WIDTH + N_GATE
    w_in_p = jnp.concatenate([w_in[:, :, split:], w_in[:, :, :split], pad], axis=-1)
    gate_vec = lambda a: jnp.pad(a.reshape(nl, 1, N_GATE // 2), ((0, 0), (0, 0), (N_GATE // 2, LANES - N_GATE)))

    def pair_blocks(a):
        a = a.reshape(nl, 2, 4, 2, 64, 64)
        z = jnp.zeros_like(a[:, :, :, 0])
        top = jnp.concatenate([a[:, :, :, 0], z], axis=-1)
        bot = jnp.concatenate([z, a[:, :, :, 1]], axis=-1)
        return jnp.concatenate([top, bot], axis=-2).astype(BF16)

    bf = lambda a: a.astype(BF16)
    return dict(
        norm1_g=w["norm1_g"][:, None, :], w_in=w_in_p,
        dn_conv_w=w["dn_conv_w"], alog=gate_vec(w["dn_a_log"]), dtb=gate_vec(w["dn_dt_bias"]),
        dn_norm_g=w["dn_norm_g"][:, None, :], lru_conv_w=w["lru_conv_w"], lru_conv_b=w["lru_conv_b"][:, None, :],
        lru_wa=pair_blocks(w["lru_wa"]), lru_ba=w["lru_ba"], lru_wx=pair_blocks(w["lru_wx"]), lru_bx=w["lru_bx"],
        lru_lambda=w["lru_lambda"], lru_norm_g=w["lru_norm_g"][:, None, :],
        w_out=bf(w["w_out"]), norm2_g=w["norm2_g"][:, None, :], ffn_wg=bf(w["ffn_wg"]), ffn_wu=bf(w["ffn_wu"]),
        ffn_conv_w=w["ffn_conv_w"], ffn_conv_b=w["ffn_conv_b"][:, None, :], ffn_wd=bf(w["ffn_wd"]),
        ple_norm_g=w["ple_norm_g"][:, None, :], ple_wg=bf(w["ple_wg"]), ple_bg=w["ple_bg"][:, None, :], ple_wp=bf(w["ple_wp"]),
        final_g=w["final_g"][None, :],
    )


def _unpair_blocks(a):
    top = a[:, :, :64, :64]
    bot = a[:, :, 64:, 64:]
    return jnp.stack([top, bot], axis=2).reshape(2, 8, 64, 64)


def _local_step(x, p, target, w):
    bsz, s, d = x.shape
    t = bsz * s
    nl = w["w_in"].shape[0]
    kw = _prepare_weights(w)
    flat = lambda a: a.reshape(t, a.shape[-1])
    seq = lambda a: a.reshape(bsz, s, a.shape[-1])

    saved = []
    r = flat(x)
    for i in range(nl):
        n = f"l{i}_"
        sv = {"r0": r}
        h = _rmsnorm_fwd(r, kw["norm1_g"][i], n + "norm1")
        proj = _matmul([(h, kw["w_in"][i])], n + "in_proj")
        gb = _dn_gates_fwd(proj, kw["alog"][i], kw["dtb"][i], n + "dn_gates")
        qkv = _dn_prep_fwd(seq(proj), kw["dn_conv_w"][i], n + "dn_prep")
        in_f = _dn_intra_fwd(qkv, seq(gb), False, n + "dn_intra_fwd")
        in_b = _dn_intra_fwd(qkv, seq(gb), True, n + "dn_intra_rev")
        o_f, st_f = _dn_rec_fwd(in_f, False, n + "dn_rec_fwd")
        o_b, st_b = _dn_rec_fwd(in_b, True, n + "dn_rec_rev")
        lru_args = (seq(proj), kw["lru_conv_w"][i], kw["lru_conv_b"][i], kw["lru_wa"][i], kw["lru_ba"][i],
                    kw["lru_wx"][i], kw["lru_bx"][i], kw["lru_lambda"][i])
        hf, hb = _lru_fwd(*lru_args, n + "lru")
        mix_args = (flat(o_f), flat(o_b), proj, flat(hf), flat(hb), kw["dn_norm_g"][i], kw["lru_norm_g"][i])
        mix = _mix_fwd(*mix_args, n + "mix")
        r1 = _matmul([(mix, kw["w_out"][i])], n + "out_proj", res=r)
        h2 = _rmsnorm_fwd(r1, kw["norm2_g"][i], n + "norm2")
        fg = _matmul([(h2, kw["ffn_wg"][i])], n + "ffn_g")
        fu = _matmul([(h2, kw["ffn_wu"][i])], n + "ffn_u")
        act = _ffn_act_fwd(seq(fg), seq(fu), kw["ffn_conv_w"][i], kw["ffn_conv_b"][i], n + "ffn_act")
        r2 = _matmul([(flat(act), kw["ffn_wd"][i])], n + "ffn_d", res=r1)
        hp = _rmsnorm_fwd(r2, kw["ple_norm_g"][i], n + "ple_norm")
        pg = _matmul([(hp, kw["ple_wg"][i])], n + "ple_g")
        pi = flat(p[i])
        pp = _matmul([(pi, kw["ple_wp"][i])], n + "ple_p")
        r3 = _ple_fwd(r2, pg, pp, kw["ple_bg"][i], n + "ple")
        sv.update(h=h, proj=proj, gb=gb, qkv=qkv, st_f=st_f, st_b=st_b, in_f=in_f, in_b=in_b, lru_args=lru_args, hf=hf, hb=hb,
                  mix_args=mix_args, mix=mix, r1=r1, h2=h2, fg=fg, fu=fu, act=act, r2=r2, hp=hp, pg=pg, pp=pp, pi=pi)
        saved.append(sv)
        r = r3

    loss, dr, dfinal = _loss_head(r, kw["final_g"], flat(target), "loss_head")
    grads = {k: [None] * nl for k in (
        "norm1_g", "w_in", "dn_conv_w", "dn_a_log", "dn_dt_bias", "dn_norm_g", "lru_conv_w", "lru_conv_b", "lru_wa", "lru_ba",
        "lru_wx", "lru_bx", "lru_lambda", "lru_norm_g", "w_out", "norm2_g", "ffn_wg", "ffn_wu", "ffn_conv_w", "ffn_conv_b",
        "ffn_wd", "ple_norm_g", "ple_wg", "ple_bg", "ple_wp")}

    for i in reversed(range(nl)):
        n = f"b{i}_"
        sv = saved[i]
        dpg, dpp, dbg = _ple_bwd(sv["pg"], sv["pp"], kw["ple_bg"][i], dr, n + "ple")
        grads["ple_bg"][i] = dbg[0]
        grads["ple_wp"][i] = _matmul_tn(sv["pi"], dpp, n + "ple_wp")
        grads["ple_wg"][i] = _matmul_tn(sv["hp"], dpg, n + "ple_wg")
        dhp = _matmul([(dpg, kw["ple_wg"][i])], n + "ple_dh", nt=True)
        dr2, dr2b, dg = _rmsnorm_bwd(sv["r2"], kw["ple_norm_g"][i], dhp, dr, n + "ple_norm")
        grads["ple_norm_g"][i] = dg[0]
        grads["ffn_wd"][i] = _matmul_tn(flat(sv["act"]), dr2b, n + "ffn_wd")
        dact = _matmul([(dr2b, kw["ffn_wd"][i])], n + "ffn_dact", nt=True)
        dfg, dfu, dcw, dcb = _ffn_act_bwd(seq(sv["fg"]), seq(sv["fu"]), kw["ffn_conv_w"][i], kw["ffn_conv_b"][i], seq(dact),
                                          n + "ffn_act")
        grads["ffn_conv_w"][i] = dcw
        grads["ffn_conv_b"][i] = dcb[0]
        grads["ffn_wg"][i] = _matmul_tn(sv["h2"], flat(dfg), n + "ffn_wg")
        grads["ffn_wu"][i] = _matmul_tn(sv["h2"], flat(dfu), n + "ffn_wu")
        dh2 = _matmul([(flat(dfg), kw["ffn_wg"][i]), (flat(dfu), kw["ffn_wu"][i])], n + "ffn_dh", nt=True)
        dr1, dr1b, dg = _rmsnorm_bwd(sv["r1"], kw["norm2_g"][i], dh2, dr2, n + "norm2")
        grads["norm2_g"][i] = dg[0]
        grads["w_out"][i] = _matmul_tn(sv["mix"], dr1b, n + "w_out")
        dmix = _matmul([(dr1b, kw["w_out"][i])], n + "dmix", nt=True)
        do, dz, dlg, dh, ddn_g, dlru_g = _mix_bwd(*sv["mix_args"], dmix, n + "mix")
        grads["dn_norm_g"][i] = ddn_g[0]
        grads["lru_norm_g"][i] = dlru_g[0]
        dlx, dcw, dcb, dwa, dba, dwx, dbx, dlam = _lru_bwd(*sv["lru_args"], sv["hf"], sv["hb"], seq(dh), n + "lru")
        grads["lru_conv_w"][i] = dcw
        grads["lru_conv_b"][i] = dcb[0]
        grads["lru_wa"][i] = _unpair_blocks(dwa)
        grads["lru_wx"][i] = _unpair_blocks(dwx)
        grads["lru_ba"][i], grads["lru_bx"][i], grads["lru_lambda"][i] = dba, dbx, dlam
        ct_f = _dn_rec_bwd(sv["in_f"], sv["st_f"], seq(do), False, n + "dn_rec_fwd")
        ct_b = _dn_rec_bwd(sv["in_b"], sv["st_b"], seq(do), True, n + "dn_rec_rev")
        dqkv_f, dgb_f = _dn_intra_bwd(sv["qkv"], seq(sv["gb"]), ct_f, False, n + "dn_intra_fwd")
        dqkv_b, dgb_b = _dn_intra_bwd(sv["qkv"], seq(sv["gb"]), ct_b, True, n + "dn_intra_rev")
        dgate, dalog, ddtb = _dn_gates_bwd(sv["proj"], kw["alog"][i], kw["dtb"][i], flat(dgb_f), flat(dgb_b), n + "dn_gates")
        grads["dn_a_log"][i] = dalog[0, N_GATE // 2:N_GATE].reshape(2, DN_HEADS)
        grads["dn_dt_bias"][i] = ddtb[0, N_GATE // 2:N_GATE].reshape(2, DN_HEADS)
        dpqkv, dcw = _dn_prep_bwd(seq(sv["proj"]), kw["dn_conv_w"][i], dqkv_f, dqkv_b, n + "dn_prep")
        grads["dn_conv_w"][i] = dcw
        segs = [flat(dlx), dlg, flat(dpqkv), dz, dgate]
        w_in_i = kw["w_in"][i]
        dq3 = flat(dpqkv)
        wseg = DN_WIDTH
        terms = [(flat(dlx), (w_in_i, wseg, P_LX // wseg)), (dlg, (w_in_i, wseg, P_LG // wseg))]
        terms += [((dq3, wseg, j), (w_in_i, wseg, P_Q // wseg + j)) for j in range(3)]
        terms += [(dz, (w_in_i, wseg, P_Z // wseg)), (dgate, (w_in_i, LANES, P_GATE // LANES))]
        dh1 = _matmul(terms, n + "in_dh", nt=True)
        dwp = jnp.concatenate([_matmul_tn(sv["h"], sg, n + f"w_in{j}") for j, sg in enumerate(segs)], axis=-1)
        split = 4 * DN_WIDTH + N_GATE
        grads["w_in"][i] = jnp.concatenate([dwp[:, P_Q:P_Q + split], dwp[:, :P_Q]], axis=-1)
        dr, _, dg = _rmsnorm_bwd(sv["r0"], kw["norm1_g"][i], dh1, dr1, n + "norm1")
        grads["norm1_g"][i] = dg[0]

    grads["final_g"] = dfinal[0]
    return loss[0, 0], dr.reshape(bsz, s, d), grads


MESH = pl.DeviceIdType.MESH
ANY = pl.BlockSpec(memory_space=pl.ANY)
N_CHIPS = 4
HALF = 2


def _place():
    x, y, c = lax.axis_index("x"), lax.axis_index("y"), lax.axis_index("c")
    chips = [(1 - x, y), (x, 1 - y), (1 - x, 1 - y)]
    return x, y, c, chips


def _dma_sems(n):
    return pltpu.SemaphoreType.DMA((n,))


def _gather_weights(shards):
    nt = len(shards)

    def body(*refs):
        w_refs, g_refs = refs[:nt], refs[nt:2 * nt]
        send_sems, recv_sems = refs[2 * nt:]
        x, y, c, chips = _place()
        me = 2 * x + y
        mine = pl.ds(c * HALF, HALF)
        theirs = pl.ds((1 - c) * HALF, HALF)

        def rc(src, dst, k, to):
            return pltpu.make_async_remote_copy(src_ref=src, dst_ref=dst, send_sem=send_sems.at[k], recv_sem=recv_sems.at[k],
                                                device_id=to, device_id_type=MESH)

        first = []
        for t in range(nt):
            for r, (px, py) in enumerate(chips):
                first.append(rc(w_refs[t].at[mine], g_refs[t].at[me, mine], 6 * t + r, (px, py, c)))
        for cp in first:
            cp.start()
        passed = []
        for t in range(nt):
            for r, (px, py) in enumerate(chips):
                peer = 2 * px + py
                rc(w_refs[t].at[mine], g_refs[t].at[peer, mine], 6 * t + r, (px, py, c)).wait_recv()
                fw = rc(g_refs[t].at[peer, mine], g_refs[t].at[peer, mine], 6 * t + 3 + r, (x, y, 1 - c))
                fw.start()
                passed.append(fw)
        for t in range(nt):
            for r, (px, py) in enumerate(chips):
                peer = 2 * px + py
                rc(g_refs[t].at[peer, theirs], g_refs[t].at[peer, theirs], 6 * t + 3 + r, (x, y, 1 - c)).wait_recv()
        for cp in first + passed:
            cp.wait_send()

    return pl.pallas_call(
        body, name="gather_weights", in_specs=[ANY] * nt, out_specs=[ANY] * nt,
        out_shape=[jax.ShapeDtypeStruct((N_CHIPS,) + a.shape, a.dtype) for a in shards],
        scratch_shapes=[_dma_sems(6 * nt), _dma_sems(6 * nt)],
    )(*shards)


def _swap_halves(blocks):
    nt = len(blocks)

    def body(*refs):
        p_refs, l_refs = refs[:nt], refs[nt:2 * nt]
        send_sems, recv_sems = refs[2 * nt:]
        x, y, c, _ = _place()
        theirs = pl.ds((1 - c) * HALF, HALF)
        cps = [pltpu.make_async_remote_copy(src_ref=p_refs[t].at[:, theirs], dst_ref=l_refs[t], send_sem=send_sems.at[t],
                                            recv_sem=recv_sems.at[t], device_id=(x, y, 1 - c), device_id_type=MESH)
               for t in range(nt)]
        for cp in cps:
            cp.start()
        for cp in cps:
            cp.wait_send()
            cp.wait_recv()

    return pl.pallas_call(
        body, name="swap_halves", in_specs=[ANY] * nt, out_specs=[ANY] * nt,
        out_shape=[jax.ShapeDtypeStruct((a.shape[0], HALF) + a.shape[2:], a.dtype) for a in blocks],
        scratch_shapes=[_dma_sems(nt), _dma_sems(nt)],
    )(*blocks)


def _tile_rows(a, row_bytes):
    best = None
    for d in range(16, a + 1, 16):
        if a % d == 0 and d * row_bytes <= 2 * 1024 * 1024:
            best = d
    return best if best is not None else a


def _add_halves(pg, l1, c_arr, name):
    n, _, a, b = pg.shape
    ta = _tile_rows(a, 4 * b)

    def body(c_ref, a_ref, b_ref, o_ref):
        o_ref[...] = (a_ref[...] + b_ref[...]).astype(o_ref.dtype)

    grid_spec = pltpu.PrefetchScalarGridSpec(
        num_scalar_prefetch=1, grid=(n, HALF, a // ta),
        in_specs=[pl.BlockSpec((1, 1, ta, b), lambda k, l, i, c_ref: (k, c_ref[0] * HALF + l, i, 0)),
                  pl.BlockSpec((1, 1, ta, b), lambda k, l, i, c_ref: (k, l, i, 0))],
        out_specs=pl.BlockSpec((1, 1, ta, b), lambda k, l, i, c_ref: (k, l, i, 0)))
    return pl.pallas_call(
        body, name=name, grid_spec=grid_spec, out_shape=jax.ShapeDtypeStruct((n, HALF, a, b), BF16),
        compiler_params=_cparams(("parallel", "parallel", "parallel")),
    )(c_arr, pg, l1)


def _scatter_chips(blocks):
    nt = len(blocks)

    def body(*refs):
        q_refs, l_refs = refs[:nt], refs[nt:2 * nt]
        send_sems, recv_sems = refs[2 * nt:]
        x, y, c, chips = _place()
        me = 2 * x + y

        def rc(t, r, src_chip, dst_chip, to):
            return pltpu.make_async_remote_copy(src_ref=q_refs[t].at[src_chip], dst_ref=l_refs[t].at[dst_chip],
                                                send_sem=send_sems.at[3 * t + r], recv_sem=recv_sems.at[3 * t + r],
                                                device_id=to, device_id_type=MESH)

        sends = [rc(t, r, 2 * px + py, me, (px, py, c)) for t in range(nt) for r, (px, py) in enumerate(chips)]
        for cp in sends:
            cp.start()
        for t in range(nt):
            for r, (px, py) in enumerate(chips):
                rc(t, r, 2 * px + py, 2 * px + py, (px, py, c)).wait_recv()
        for cp in sends:
            cp.wait_send()

    return pl.pallas_call(
        body, name="scatter_chips", in_specs=[ANY] * nt, out_specs=[ANY] * nt,
        out_shape=[jax.ShapeDtypeStruct(a.shape, a.dtype) for a in blocks],
        scratch_shapes=[_dma_sems(3 * nt), _dma_sems(3 * nt)],
    )(*blocks)


def _sum_chips(l2, own, me_arr, name):
    n, _, a, b = l2.shape
    ta = _tile_rows(a, 4 * b)

    def body(me_ref, a_ref, own_ref, o_ref):
        me = me_ref[0]
        acc = jnp.where(me == 0, own_ref[0, 0], a_ref[0, 0]).astype(F32)
        for k in range(1, n):
            acc = acc + jnp.where(me == k, own_ref[0, 0], a_ref[k, 0]).astype(F32)
        o_ref[0] = acc

    grid_spec = pltpu.PrefetchScalarGridSpec(
        num_scalar_prefetch=1, grid=(HALF, a // ta),
        in_specs=[pl.BlockSpec((n, 1, ta, b), lambda l, i, me_ref: (0, l, i, 0)),
                  pl.BlockSpec((1, 1, ta, b), lambda l, i, me_ref: (me_ref[0], l, i, 0))],
        out_specs=pl.BlockSpec((1, ta, b), lambda l, i, me_ref: (l, i, 0)))
    return pl.pallas_call(
        body, name=name, grid_spec=grid_spec, out_shape=jax.ShapeDtypeStruct((HALF, a, b), F32),
        compiler_params=_cparams(("parallel", "parallel")),
    )(me_arr, l2, own)


def _swap_reduced(parts):
    nt = len(parts)

    def body(*refs):
        r_refs, g_refs = refs[:nt], refs[nt:2 * nt]
        send_sems, recv_sems = refs[2 * nt:]
        x, y, c, _ = _place()
        cps = [pltpu.make_async_remote_copy(src_ref=r_refs[t], dst_ref=g_refs[t], send_sem=send_sems.at[t],
                                            recv_sem=recv_sems.at[t], device_id=(x, y, 1 - c), device_id_type=MESH)
               for t in range(nt)]
        for cp in cps:
            cp.start()
        for cp in cps:
            cp.wait_send()
            cp.wait_recv()

    return pl.pallas_call(
        body, name="swap_reduced", in_specs=[ANY] * nt, out_specs=[ANY] * nt,
        out_shape=[jax.ShapeDtypeStruct(a.shape, a.dtype) for a in parts],
        scratch_shapes=[_dma_sems(nt), _dma_sems(nt)],
    )(*parts)


def _adamw_halves(w, mine, theirs, m, v, c_arr, name):
    nl, a, b = w.shape
    ta = _tile_rows(a, 4 * b)

    def body(c_ref, w_ref, g1_ref, g2_ref, m_ref, v_ref, g_ref, d_ref, nm_ref, nv_ref):
        gg = jnp.where(pl.program_id(0) // HALF == c_ref[0], g1_ref[...], g2_ref[...])
        g_ref[...] = gg
        _adamw_math(w_ref[...], gg, m_ref[...], v_ref[...], d_ref, nm_ref, nv_ref)

    full = pl.BlockSpec((1, ta, b), lambda l, i, c_ref: (l, i, 0))
    half = pl.BlockSpec((1, ta, b), lambda l, i, c_ref: (l % HALF, i, 0))
    grid_spec = pltpu.PrefetchScalarGridSpec(num_scalar_prefetch=1, grid=(nl, a // ta),
                                             in_specs=[full, half, half, full, full], out_specs=[full] * 4)
    return pl.pallas_call(
        body, name=name, grid_spec=grid_spec, out_shape=[jax.ShapeDtypeStruct(w.shape, F32)] * 4,
        compiler_params=_cparams(("parallel", "parallel")),
    )(c_arr, w, mine, theirs, m, v)


def _gather_chips(v):
    def body(v_ref, g_ref, send_sems, recv_sems, local_sem):
        x, y, c, chips = _place()
        me = 2 * x + y
        loc = pltpu.make_async_copy(v_ref, g_ref.at[me], local_sem)
        loc.start()
        sends = [pltpu.make_async_remote_copy(src_ref=v_ref, dst_ref=g_ref.at[me], send_sem=send_sems.at[r], recv_sem=recv_sems.at[r],
                                              device_id=(px, py, c), device_id_type=MESH) for r, (px, py) in enumerate(chips)]
        for cp in sends:
            cp.start()
        for r, (px, py) in enumerate(chips):
            pltpu.make_async_remote_copy(src_ref=v_ref, dst_ref=g_ref.at[2 * px + py], send_sem=send_sems.at[r],
                                         recv_sem=recv_sems.at[r], device_id=(px, py, c), device_id_type=MESH).wait_recv()
        for cp in sends:
            cp.wait_send()
        loc.wait()

    return pl.pallas_call(
        body, name="gather_chips", in_specs=[ANY], out_specs=ANY, out_shape=jax.ShapeDtypeStruct((N_CHIPS,) + v.shape, v.dtype),
        scratch_shapes=[_dma_sems(3), _dma_sems(3), pltpu.SemaphoreType.DMA],
    )(v)


BIG = (("w_in", 2), ("w_out", 1), ("ffn_wg", 2), ("ffn_wu", 2), ("ffn_wd", 1), ("ple_wg", 1), ("ple_wp", 2))
SMALL = (("dn_conv_w", 2), ("lru_conv_w", 2), ("lru_ba", 2), ("lru_bx", 2), ("lru_lambda", 2), ("ffn_conv_w", 2))
REPL = ("norm1_g", "dn_a_log", "dn_dt_bias", "dn_norm_g", "lru_conv_b", "lru_wa", "lru_wx", "lru_norm_g", "norm2_g",
        "ffn_conv_b", "ple_norm_g", "ple_bg", "final_g")
WEIGHTS = ("norm1_g", "w_in", "dn_conv_w", "dn_a_log", "dn_dt_bias", "dn_norm_g", "lru_conv_w", "lru_conv_b", "lru_wa", "lru_ba",
           "lru_wx", "lru_bx", "lru_lambda", "lru_norm_g", "w_out", "norm2_g", "ffn_wg", "ffn_wu", "ffn_conv_w", "ffn_conv_b",
           "ffn_wd", "ple_norm_g", "ple_wg", "ple_bg", "ple_wp", "final_g")
ROW_ALIGN = 16


def _rows_for(n_elems):
    rows = -(-n_elems // LANES)
    return -(-rows // ROW_ALIGN) * ROW_ALIGN


def _join_chips(g, own, me, axis):
    return jnp.concatenate([jnp.where(me == k, own, g[k]) for k in range(N_CHIPS)], axis=axis)


def _chip_blocks(layers, axis):
    nl = len(layers)
    size = layers[0].shape[axis - 1] // N_CHIPS
    cut = (lambda g, k: g[:, k * size:(k + 1) * size]) if axis == 2 else (lambda g, k: g[k * size:(k + 1) * size])
    return jnp.stack([jnp.stack([cut(layers[l], k) for l in range(nl)]) for k in range(N_CHIPS)])


def _pack_small(w):
    nl = w[SMALL[0][0]].shape[0]
    flat = jnp.concatenate([w[n].reshape(nl, -1) for n, _ in SMALL], axis=1)
    rows = _rows_for(flat.shape[1])
    return jnp.pad(flat, ((0, 0), (0, rows * LANES - flat.shape[1]))).reshape(nl, rows, LANES)


def _unpack_small_gathered(g, own, me, w):
    nl = g.shape[1]
    flat = jnp.stack([jnp.where(me == k, own, g[k]) for k in range(N_CHIPS)]).reshape(N_CHIPS, nl, -1)
    out, off = {}, 0
    for n, _ in SMALL:
        _, a, b = w[n].shape
        piece = flat[:, :, off:off + a * b].reshape(N_CHIPS, nl, a, b)
        off += a * b
        out[n] = jnp.transpose(piece, (1, 2, 0, 3)).reshape(nl, a, N_CHIPS * b)
    return out


def _pack_small_grads(grads, w):
    nl = w["w_in"].shape[0]
    cols = []
    for n, _ in SMALL:
        _, a, b = w[n].shape
        gfull = jnp.stack(grads[n])
        cols.append(jnp.transpose(gfull.reshape(nl, a, N_CHIPS, b), (2, 0, 1, 3)).reshape(N_CHIPS, nl, a * b))
    small = jnp.concatenate(cols, axis=2)
    rs = _rows_for(small.shape[2])
    small = jnp.pad(small, ((0, 0), (0, 0), (0, rs * LANES - small.shape[2])))
    rep = jnp.concatenate([(grads[n] if n == "final_g" else jnp.stack(grads[n])).reshape(-1) for n in REPL])
    rr = _rows_for(-(-rep.shape[0] // (N_CHIPS * nl)))
    rep = jnp.pad(rep, (0, N_CHIPS * nl * rr * LANES - rep.shape[0])).reshape(N_CHIPS, nl, rr * LANES)
    return jnp.concatenate([small, rep], axis=2).reshape(N_CHIPS, nl, rs + rr, LANES), (rs, rr)


def _unpack_small_reduced(g, rep_all, rows, w):
    rs, _ = rows
    nl = g.shape[0]
    out = {}
    flat = g[:, :rs].reshape(nl, -1)
    off = 0
    for n, _ in SMALL:
        _, a, b = w[n].shape
        out[n] = flat[:, off:off + a * b].reshape(nl, a, b)
        off += a * b
    flat = rep_all.reshape(-1)
    off = 0
    for n in REPL:
        size = math.prod(w[n].shape)
        out[n] = flat[off:off + size].reshape(w[n].shape)
        off += size
    return out


def _as2d(a):
    if a.ndim == 1:
        return a.reshape(1, -1)
    return a.reshape(-1, a.shape[-1])


def kernel(x, p, norm1_g, w_in, dn_conv_w, dn_a_log, dn_dt_bias, dn_norm_g, lru_conv_w, lru_conv_b, lru_wa, lru_ba, lru_wx, lru_bx, lru_lambda, lru_norm_g, w_out, norm2_g, ffn_wg, ffn_wu, ffn_conv_w, ffn_conv_b, ffn_wd, ple_norm_g, ple_wg, ple_bg, ple_wp, final_g, loss_target, m_norm1_g, m_w_in, m_dn_conv_w, m_dn_a_log, m_dn_dt_bias, m_dn_norm_g, m_lru_conv_w, m_lru_conv_b, m_lru_wa, m_lru_ba, m_lru_wx, m_lru_bx, m_lru_lambda, m_lru_norm_g, m_w_out, m_norm2_g, m_ffn_wg, m_ffn_wu, m_ffn_conv_w, m_ffn_conv_b, m_ffn_wd, m_ple_norm_g, m_ple_wg, m_ple_bg, m_ple_wp, m_final_g, v_norm1_g, v_w_in, v_dn_conv_w, v_dn_a_log, v_dn_dt_bias, v_dn_norm_g, v_lru_conv_w, v_lru_conv_b, v_lru_wa, v_lru_ba, v_lru_wx, v_lru_bx, v_lru_lambda, v_lru_norm_g, v_w_out, v_norm2_g, v_ffn_wg, v_ffn_wu, v_ffn_conv_w, v_ffn_conv_b, v_ffn_wd, v_ple_norm_g, v_ple_wg, v_ple_bg, v_ple_wp, v_final_g):
    w = dict(norm1_g=norm1_g, w_in=w_in, dn_conv_w=dn_conv_w, dn_a_log=dn_a_log, dn_dt_bias=dn_dt_bias, dn_norm_g=dn_norm_g,
             lru_conv_w=lru_conv_w, lru_conv_b=lru_conv_b, lru_wa=lru_wa, lru_ba=lru_ba, lru_wx=lru_wx, lru_bx=lru_bx,
             lru_lambda=lru_lambda, lru_norm_g=lru_norm_g, w_out=w_out, norm2_g=norm2_g, ffn_wg=ffn_wg, ffn_wu=ffn_wu,
             ffn_conv_w=ffn_conv_w, ffn_conv_b=ffn_conv_b, ffn_wd=ffn_wd, ple_norm_g=ple_norm_g, ple_wg=ple_wg, ple_bg=ple_bg,
             ple_wp=ple_wp, final_g=final_g)
    m = dict(norm1_g=m_norm1_g, w_in=m_w_in, dn_conv_w=m_dn_conv_w, dn_a_log=m_dn_a_log, dn_dt_bias=m_dn_dt_bias,
             dn_norm_g=m_dn_norm_g, lru_conv_w=m_lru_conv_w, lru_conv_b=m_lru_conv_b, lru_wa=m_lru_wa, lru_ba=m_lru_ba,
             lru_wx=m_lru_wx, lru_bx=m_lru_bx, lru_lambda=m_lru_lambda, lru_norm_g=m_lru_norm_g, w_out=m_w_out, norm2_g=m_norm2_g,
             ffn_wg=m_ffn_wg, ffn_wu=m_ffn_wu, ffn_conv_w=m_ffn_conv_w, ffn_conv_b=m_ffn_conv_b, ffn_wd=m_ffn_wd,
             ple_norm_g=m_ple_norm_g, ple_wg=m_ple_wg, ple_bg=m_ple_bg, ple_wp=m_ple_wp, final_g=m_final_g)
    v = dict(norm1_g=v_norm1_g, w_in=v_w_in, dn_conv_w=v_dn_conv_w, dn_a_log=v_dn_a_log, dn_dt_bias=v_dn_dt_bias,
             dn_norm_g=v_dn_norm_g, lru_conv_w=v_lru_conv_w, lru_conv_b=v_lru_conv_b, lru_wa=v_lru_wa, lru_ba=v_lru_ba,
             lru_wx=v_lru_wx, lru_bx=v_lru_bx, lru_lambda=v_lru_lambda, lru_norm_g=v_lru_norm_g, w_out=v_w_out, norm2_g=v_norm2_g,
             ffn_wg=v_ffn_wg, ffn_wu=v_ffn_wu, ffn_conv_w=v_ffn_conv_w, ffn_conv_b=v_ffn_conv_b, ffn_wd=v_ffn_wd,
             ple_norm_g=v_ple_norm_g, ple_wg=v_ple_wg, ple_bg=v_ple_bg, ple_wp=v_ple_wp, final_g=v_final_g)

    me = 2 * lax.axis_index("x") + lax.axis_index("y")
    own = [w[n].astype(BF16) for n, _ in BIG] + [_pack_small(w)]
    gathered = _gather_weights(own)
    full = {n: w[n] for n in REPL}
    for (n, axis), g, o in zip(BIG, gathered, own):
        full[n] = _join_chips(g, o, me, axis)
    full.update(_unpack_small_gathered(gathered[-1], own[-1], me, w))

    loss_local, grad_x, grads = _local_step(x, p, loss_target, full)
    loss = lax.psum(loss_local, ("x", "y", "c"))

    small_pack, rows = _pack_small_grads(grads, w)
    blocks = [_chip_blocks(grads[n], axis) for n, axis in BIG]
    blocks.append(small_pack)
    names = [n for n, _ in BIG] + ["small"]
    c_idx = lax.axis_index("c")
    c_arr = c_idx.astype(jnp.int32).reshape(1)
    me_arr = me.astype(jnp.int32).reshape(1)
    from_sibling = _swap_halves(blocks)
    halves = [_add_halves(b4, l1, c_arr, "add_halves_" + n) for n, b4, l1 in zip(names, blocks, from_sibling)]
    arrived = _scatter_chips(halves)
    mine = [_sum_chips(l2, q, me_arr, "sum_chips_" + n) for n, l2, q in zip(names, arrived, halves)]
    theirs = _swap_reduced(mine)

    g, deltas, new_m, new_v = {}, {}, {}, {}
    for i, (n, _) in enumerate(BIG):
        g[n], deltas[n], new_m[n], new_v[n] = _adamw_halves(w[n], mine[i], theirs[i], m[n], v[n], c_arr, "adamw_" + n)
    small = jnp.where(c_idx == 0, jnp.concatenate([mine[-1], theirs[-1]]), jnp.concatenate([theirs[-1], mine[-1]]))
    rep_all = _gather_chips(small[:, rows[0]:])
    gs = _unpack_small_reduced(small, rep_all, rows, w)
    for n in WEIGHTS:
        if n in g:
            continue
        g[n] = gs[n].reshape(w[n].shape)
        d2, m2, v2 = _adamw(_as2d(w[n]), _as2d(g[n]), _as2d(m[n]), _as2d(v[n]), "adamw_" + n)
        deltas[n], new_m[n], new_v[n] = d2.reshape(w[n].shape), m2.reshape(w[n].shape), v2.reshape(w[n].shape)
    return (loss, grad_x, *[g[n] for n in WEIGHTS], *[deltas[n] for n in WEIGHTS], *[new_m[n] for n in WEIGHTS],
            *[new_v[n] for n in WEIGHTS])
```

```python
import functools
import math

import jax
import jax.numpy as jnp
from jax import lax
from jax.experimental import pallas as pl
from jax.experimental.pallas import tpu as pltpu

F32 = jnp.float32
BF16 = jnp.bfloat16

D_MODEL = 1024
DN_HEADS = 4
DN_HEAD_DIM = 128
DN_WIDTH = 512
LRU_WIDTH = 512
LRU_C = 8.0
CHUNK = 64
EPS = 1e-6
P_LX, P_LG, P_Q, P_K, P_V, P_Z, P_GATE, P_COLS = 0, 512, 1024, 1536, 2048, 2560, 3072, 3200
N_GATE = 16
LANES = 128
VMEM_LIMIT = 56 * 1024 * 1024
MM_VMEM_BYTES = 40 * 1024 * 1024

ADAM_LR, ADAM_B1, ADAM_B2, ADAM_EPS, ADAM_WD, ADAM_STEP = 0.001, 0.9, 0.999, 1e-08, 0.01, 10


def _cparams(sem):
    return pltpu.CompilerParams(dimension_semantics=sem, vmem_limit_bytes=VMEM_LIMIT)


def _row_tile(m, want=512):
    for t in range(min(want, m) // 16 * 16, 0, -16):
        if m % t == 0:
            return t
    return m


def _lane_divisors(n):
    out = [d for d in range(n, 0, -LANES) if d % LANES == 0 and n % d == 0] if n % LANES == 0 else []
    return out or [n]


def _mm_tiles(m, n, a_row_bytes, k_total, with_res):
    best = None
    for tm in (1024, 512, 256, 128):
        if m % tm:
            continue
        for tn in _lane_divisors(n):
            need = 2 * (tm * a_row_bytes + k_total * tn * 2 + tm * tn * 4 * (2 if with_res else 1))
            if need <= MM_VMEM_BYTES and (best is None or tm * tn > best[0] * best[1]):
                best = (tm, tn)
    return best if best is not None else (_row_tile(m, 128), _lane_divisors(n)[-1])


def _mm_tn_tiles(m, k, n):
    for tn in _lane_divisors(n):
        for tm in (1024, 512, 256, 128):
            if m % tm == 0 and 2 * (k * tn * 4 + tm * k * 2 + tm * tn * 2) <= MM_VMEM_BYTES:
                return tm, tn
    return _row_tile(m, 128), _lane_divisors(n)[-1]


def _bdot(a, b):
    return jnp.dot(a.astype(BF16), b.astype(BF16), preferred_element_type=F32)


def _bdot_nt(a, b):
    return lax.dot_general(a.astype(BF16), b.astype(BF16), (((1,), (1,)), ((), ())), preferred_element_type=F32)


def _bdot_tn(a, b):
    return lax.dot_general(a.astype(BF16), b.astype(BF16), (((0,), (0,)), ((), ())), preferred_element_type=F32)


def _rms(x, g):
    return x * lax.rsqrt(jnp.mean(x * x, axis=-1, keepdims=True) + EPS) * g


def _gelu(x):
    return 0.5 * x * (1.0 + jnp.tanh(0.7978845608028654 * (x + 0.044715 * x * x * x)))


def _sigmoid(x):
    return 1.0 / (1.0 + jnp.exp(-x))


def _silu(x):
    return x * _sigmoid(x)


def _softplus(x):
    return jnp.maximum(x, 0.0) + jnp.log(1.0 + jnp.exp(-jnp.abs(x)))


def _rmsnorm_fwd(x, g, name):
    t, d = x.shape
    tm = _row_tile(t)

    def body(x_ref, g_ref, o_ref):
        o_ref[...] = _rms(x_ref[...], g_ref[...]).astype(o_ref.dtype)

    return pl.pallas_call(
        body, name=name, grid=(t // tm,),
        in_specs=[pl.BlockSpec((tm, d), lambda i: (i, 0)), pl.BlockSpec((1, d), lambda i: (0, 0))],
        out_specs=pl.BlockSpec((tm, d), lambda i: (i, 0)),
        out_shape=jax.ShapeDtypeStruct((t, d), BF16),
        compiler_params=_cparams(("parallel",)),
    )(x, g)


def _rmsnorm_bwd(x, g, dh, dres, name):
    t, d = x.shape
    tm = _row_tile(t)

    def body(x_ref, g_ref, dh_ref, dres_ref, dx_ref, dxb_ref, dg_ref):
        _, vjp = jax.vjp(_rms, x_ref[...], g_ref[...])
        dx, dg = vjp(dh_ref[...])
        dx = dx + dres_ref[...]
        dx_ref[...] = dx
        dxb_ref[...] = dx.astype(BF16)

        @pl.when(pl.program_id(0) == 0)
        def _():
            dg_ref[...] = jnp.zeros_like(dg_ref)

        dg_ref[...] += dg

    return pl.pallas_call(
        body, name=name, grid=(t // tm,),
        in_specs=[pl.BlockSpec((tm, d), lambda i: (i, 0)), pl.BlockSpec((1, d), lambda i: (0, 0)),
                  pl.BlockSpec((tm, d), lambda i: (i, 0)), pl.BlockSpec((tm, d), lambda i: (i, 0))],
        out_specs=[pl.BlockSpec((tm, d), lambda i: (i, 0)), pl.BlockSpec((tm, d), lambda i: (i, 0)),
                   pl.BlockSpec((1, d), lambda i: (0, 0))],
        out_shape=[jax.ShapeDtypeStruct((t, d), F32), jax.ShapeDtypeStruct((t, d), BF16), jax.ShapeDtypeStruct((1, d), F32)],
        compiler_params=_cparams(("arbitrary",)),
    )(x, g, dh, dres)


def _matmul(terms, name, res=None, nt=False):
    norm = lambda op, axis: op if isinstance(op, tuple) else (op, op.shape[axis], 0)
    a_ops = [norm(a, 1) for a, _ in terms]
    b_ops = [norm(b, 1 if nt else 0) for _, b in terms]
    m = a_ops[0][0].shape[0]
    n = b_ops[0][0].shape[0 if nt else 1]
    a_row_bytes = sum(kw * a.dtype.itemsize for a, kw, _ in a_ops)
    tm, tn = _mm_tiles(m, n, a_row_bytes, sum(kw for _, kw, _ in b_ops), res is not None)
    na = len(terms)
    dot = _bdot_nt if nt else _bdot

    def body(*refs):
        a_refs, b_refs = refs[:na], refs[na:2 * na]
        acc = dot(a_refs[0][...], b_refs[0][...])
        for a_ref, b_ref in zip(a_refs[1:], b_refs[1:]):
            acc = acc + dot(a_ref[...], b_ref[...])
        if res is not None:
            acc = acc + refs[2 * na][...]
        refs[-1][...] = acc

    in_specs = [pl.BlockSpec((tm, kw), functools.partial(lambda i, j, kb: (i, kb), kb=kb)) for _, kw, kb in a_ops]
    if nt:
        in_specs += [pl.BlockSpec((tn, kw), functools.partial(lambda i, j, kb: (j, kb), kb=kb)) for _, kw, kb in b_ops]
    else:
        in_specs += [pl.BlockSpec((kw, tn), functools.partial(lambda i, j, kb: (kb, j), kb=kb)) for _, kw, kb in b_ops]
    args = [a for a, _, _ in a_ops] + [b for b, _, _ in b_ops]
    if res is not None:
        in_specs.append(pl.BlockSpec((tm, tn), lambda i, j: (i, j)))
        args.append(res)
    return pl.pallas_call(
        body, name=name, grid=(m // tm, n // tn), in_specs=in_specs,
        out_specs=pl.BlockSpec((tm, tn), lambda i, j: (i, j)),
        out_shape=jax.ShapeDtypeStruct((m, n), F32),
        compiler_params=_cparams(("parallel", "parallel")),
    )(*args)


def _matmul_tn(a, b, name):
    m, k = a.shape
    n = b.shape[1]
    tm, tn = _mm_tn_tiles(m, k, n)

    def body(a_ref, b_ref, o_ref):
        @pl.when(pl.program_id(1) == 0)
        def _():
            o_ref[...] = jnp.zeros_like(o_ref)

        o_ref[...] += _bdot_tn(a_ref[...], b_ref[...])

    return pl.pallas_call(
        body, name=name, grid=(n // tn, m // tm),
        in_specs=[pl.BlockSpec((tm, k), lambda j, i: (i, 0)), pl.BlockSpec((tm, tn), lambda j, i: (i, j))],
        out_specs=pl.BlockSpec((k, tn), lambda j, i: (0, j)),
        out_shape=jax.ShapeDtypeStruct((k, n), F32),
        compiler_params=_cparams(("parallel", "arbitrary")),
    )(a, b)


def _shift_down(x, k):
    row = lax.broadcasted_iota(jnp.int32, x.shape, 0)
    return jnp.where(row >= k, pltpu.roll(x, k, 0), 0.0)


def _shift_up(x, k):
    s = x.shape[0]
    row = lax.broadcasted_iota(jnp.int32, x.shape, 0)
    return jnp.where(row < s - k, pltpu.roll(x, s - k, 0), 0.0)


def _conv_taps(x, ntaps, left):
    out = []
    for j in range(ntaps):
        off = j - left
        out.append(_shift_down(x, -off) if off < 0 else (_shift_up(x, off) if off > 0 else x))
    return out


def _conv_fwd(x, w, left):
    taps = _conv_taps(x, w.shape[0], left)
    acc = taps[0] * w[0:1, :]
    for j in range(1, w.shape[0]):
        acc = acc + taps[j] * w[j:j + 1, :]
    return acc


def _conv_bwd(x, w, left, dout):
    ntaps = w.shape[0]
    dx = None
    for j in range(ntaps):
        off = j - left
        sh = _shift_up(dout, -off) if off < 0 else (_shift_down(dout, off) if off > 0 else dout)
        term = sh * w[j:j + 1, :]
        dx = term if dx is None else dx + term
    taps = _conv_taps(x, ntaps, left)
    dw = jnp.concatenate([jnp.sum(dout * tp, axis=0, keepdims=True) for tp in taps], axis=0)
    return dx, dw


def _scan(a, b, reverse):
    s = a.shape[0]
    d = 1
    sh = _shift_up if reverse else _shift_down
    while d < s:
        b = a * sh(b, d) + b
        a = a * sh(a, d)
        d *= 2
    return b


def _dn_gates_fn(pre, alog, dtb):
    lane = lax.broadcasted_iota(jnp.int32, pre.shape, 1)
    beta = _sigmoid(pre)
    g = -jnp.exp(alog) * _softplus(pre + dtb)
    return jnp.where(lane < N_GATE // 2, beta, jnp.where(lane < N_GATE, g, 0.0))


def _dn_gates_fwd(proj, alog, dtb, name):
    t = proj.shape[0]
    tm = _row_tile(t)
    cb = P_GATE // LANES

    def body(p_ref, a_ref, d_ref, o_ref):
        o_ref[...] = _dn_gates_fn(p_ref[...], a_ref[...], d_ref[...])

    return pl.pallas_call(
        body, name=name, grid=(t // tm,),
        in_specs=[pl.BlockSpec((tm, LANES), lambda i: (i, cb)), pl.BlockSpec((1, LANES), lambda i: (0, 0)),
                  pl.BlockSpec((1, LANES), lambda i: (0, 0))],
        out_specs=pl.BlockSpec((tm, LANES), lambda i: (i, 0)),
        out_shape=jax.ShapeDtypeStruct((t, LANES), F32),
        compiler_params=_cparams(("parallel",)),
    )(proj, alog, dtb)


def _dn_gates_bwd(proj, alog, dtb, dgb_f, dgb_b, name):
    t = proj.shape[0]
    tm = _row_tile(t)
    cb = P_GATE // LANES

    def body(p_ref, a_ref, d_ref, g1_ref, g2_ref, dp_ref, da_ref, dd_ref):
        _, vjp = jax.vjp(_dn_gates_fn, p_ref[...], a_ref[...], d_ref[...])
        dp, da, dd = vjp(g1_ref[...] + g2_ref[...])
        dp_ref[...] = dp.astype(dp_ref.dtype)

        @pl.when(pl.program_id(0) == 0)
        def _():
            da_ref[...] = jnp.zeros_like(da_ref)
            dd_ref[...] = jnp.zeros_like(dd_ref)

        da_ref[...] += da
        dd_ref[...] += dd

    row = pl.BlockSpec((tm, LANES), lambda i: (i, 0))
    vec = pl.BlockSpec((1, LANES), lambda i: (0, 0))
    return pl.pallas_call(
        body, name=name, grid=(t // tm,),
        in_specs=[pl.BlockSpec((tm, LANES), lambda i: (i, cb)), vec, vec, row, row],
        out_specs=[row, vec, vec],
        out_shape=[jax.ShapeDtypeStruct((t, LANES), BF16), jax.ShapeDtypeStruct((1, LANES), F32),
                   jax.ShapeDtypeStruct((1, LANES), F32)],
        compiler_params=_cparams(("arbitrary",)),
    )(proj, alog, dtb, dgb_f, dgb_b)


def _dn_prep_fn(x, w, is_qk):
    act = _silu(_conv_fwd(x, w, 2))
    nrm = act * lax.rsqrt(jnp.sum(act * act, axis=-1, keepdims=True) + EPS)
    return jnp.where(is_qk, nrm, act)


def _dn_prep_fwd(proj3, conv_w, name):
    bsz, s, _ = proj3.shape
    nblk = 3 * DN_WIDTH // LANES
    cb = P_Q // LANES

    def body(x_ref, w_ref, o_ref):
        o_ref[0] = _dn_prep_fn(x_ref[0], w_ref[...], pl.program_id(1) < 2 * DN_HEADS)

    return pl.pallas_call(
        body, name=name, grid=(bsz, nblk),
        in_specs=[pl.BlockSpec((1, s, LANES), lambda b, j: (b, 0, cb + j)), pl.BlockSpec((4, LANES), lambda b, j: (0, j))],
        out_specs=pl.BlockSpec((1, s, LANES), lambda b, j: (b, 0, j)),
        out_shape=jax.ShapeDtypeStruct((bsz, s, 3 * DN_WIDTH), F32),
        compiler_params=_cparams(("parallel", "parallel")),
    )(proj3, conv_w)


def _dn_prep_bwd(proj3, conv_w, dqkv_f, dqkv_b, name):
    bsz, s, _ = proj3.shape
    nblk = 3 * DN_WIDTH // LANES
    cb = P_Q // LANES

    def body(x_ref, w_ref, d1_ref, d2_ref, dx_ref, dw_ref):
        x, w, d = x_ref[0], w_ref[...], d1_ref[0] + d2_ref[0]
        is_qk = pl.program_id(0) < 2 * DN_HEADS
        pre = _conv_fwd(x, w, 2)

        def post(pre):
            act = _silu(pre)
            nrm = act * lax.rsqrt(jnp.sum(act * act, axis=-1, keepdims=True) + EPS)
            return jnp.where(is_qk, nrm, act)

        _, vjp = jax.vjp(post, pre)
        (dpre,) = vjp(d)
        dx, dw = _conv_bwd(x, w, 2, dpre)
        dx_ref[0] = dx.astype(dx_ref.dtype)

        @pl.when(pl.program_id(1) == 0)
        def _():
            dw_ref[...] = jnp.zeros_like(dw_ref)

        dw_ref[...] += dw

    col = pl.BlockSpec((1, s, LANES), lambda j, b: (b, 0, j))
    return pl.pallas_call(
        body, name=name, grid=(nblk, bsz),
        in_specs=[pl.BlockSpec((1, s, LANES), lambda j, b: (b, 0, cb + j)), pl.BlockSpec((4, LANES), lambda j, b: (0, j)), col, col],
        out_specs=[col, pl.BlockSpec((4, LANES), lambda j, b: (0, j))],
        out_shape=[jax.ShapeDtypeStruct((bsz, s, 3 * DN_WIDTH), BF16), jax.ShapeDtypeStruct((4, 3 * DN_WIDTH), F32)],
        compiler_params=_cparams(("parallel", "arbitrary")),
    )(proj3, conv_w, dqkv_f, dqkv_b)


def _parts(x, n):
    out = []
    for _ in range(n):
        bits = lax.bitcast_convert_type(x, jnp.uint32) & jnp.uint32(0xFFFF0000)
        t = lax.bitcast_convert_type(bits, F32)
        out.append(t.astype(BF16))
        x = x - t
    return out


def _dg(x, y, cx, cy):
    return lax.dot_general(x, y, (((cx + 1,), (cy + 1,)), ((0,), (0,))), preferred_element_type=F32)


def _bmm(a, b):
    return _dg(a.astype(BF16), b.astype(BF16), 1, 0)


def _bmm_nt(a, b):
    return _dg(a.astype(BF16), b.astype(BF16), 1, 1)


def _bmm_tn(a, b):
    return _dg(a.astype(BF16), b.astype(BF16), 0, 0)


def _dot3_raw(a, b, ca, cb):
    a_hi, a_lo = _parts(a, 2)
    b_hi, b_lo = _parts(b, 2)
    return _dg(a_hi, b_hi, ca, cb) + (_dg(a_hi, b_lo, ca, cb) + _dg(a_lo, b_hi, ca, cb))


@jax.custom_vjp
def _dot3(a, b):
    return _dot3_raw(a, b, 1, 0)


def _dot3_fwd(a, b):
    return _dot3_raw(a, b, 1, 0), (a, b)


def _dot3_bwd(res, ct):
    a, b = res
    return _dot3_raw(ct, b, 1, 1), _dot3_raw(a, ct, 0, 0)


_dot3.defvjp(_dot3_fwd, _dot3_bwd)


@jax.custom_vjp
def _sum_left(m, x):
    return sum(_dg(m, pt, 1, 0) for pt in _parts(x, 3))


def _sum_left_fwd(m, x):
    return _sum_left(m, x), m


def _sum_left_bwd(m, ct):
    return jnp.zeros_like(m), sum(_dg(m, pt, 0, 0) for pt in _parts(ct, 2))


_sum_left.defvjp(_sum_left_fwd, _sum_left_bwd)


@jax.custom_vjp
def _sum_right(x, m):
    return sum(_dg(pt, m, 0, 0) for pt in _parts(x, 3))


def _sum_right_fwd(x, m):
    return _sum_right(x, m), m


def _sum_right_bwd(m, ct):
    return sum(_dg(m, pt, 1, 1) for pt in _parts(ct, 2)), jnp.zeros_like(m)


_sum_right.defvjp(_sum_right_fwd, _sum_right_bwd)


def _dn_intra(q, k, v, g, beta, rev):
    nu, c, _ = q.shape
    row = lax.broadcasted_iota(jnp.int32, (nu, c, c), 1)
    col = lax.broadcasted_iota(jnp.int32, (nu, c, c), 2)
    incl = (row <= col) if rev else (row >= col)
    strict = (row < col) if rev else (row > col)
    ones_incl = jnp.where(incl, 1.0, 0.0).astype(BF16)
    ones_tr = jnp.where((row >= col) if rev else (row <= col), 1.0, 0.0).astype(BF16)
    gbc = jnp.broadcast_to(g, (nu, c, c))
    gc = _sum_left(ones_incl, gbc)
    gr = _sum_right(gbc, ones_tr)
    gcum = gc[:, :, 0:1]
    decay = jnp.where(incl, jnp.exp(jnp.where(incl, gc - gr, 0.0)), 0.0)
    qs = q * (DN_HEAD_DIM ** -0.5)
    kb = k * beta
    a = jnp.where(strict, _bmm_nt(kb, k) * decay, 0.0)
    same = lambda n: (row // n) == (col // n)
    d = jnp.where(same(16), a, 0.0)
    tinv = jnp.where(row == col, 1.0, 0.0) - d
    pw = _dot3(d, d)
    tinv = tinv + _dot3(tinv, pw)
    pw = _dot3(pw, pw)
    tinv = tinv + _dot3(tinv, pw)
    pw = _dot3(pw, pw)
    tinv = tinv + _dot3(tinv, pw)
    n = 16
    while n < c:
        e = jnp.where(same(2 * n) & jnp.logical_not(same(n)), a, 0.0)
        tinv = tinv - _bmm(tinv, _bmm(e, tinv))
        n *= 2
    egc = jnp.exp(gcum)
    u = _bmm(tinv, v * beta)
    w = _bmm(tinv, kb * egc)
    attn = _bmm_nt(qs, k) * decay
    glast = gcum[:, 0:1, :] if rev else gcum[:, c - 1:c, :]
    q_dec = qs * egc
    k_dec = k * jnp.exp(glast - gcum)
    cdec = jnp.broadcast_to(jnp.exp(glast), (nu, 1, LANES))
    return u, w, q_dec, k_dec, attn, cdec


def _dn_rec(u, w, q_dec, k_dec, attn, cdec, state):
    v_new = u - _bmm(w, state)
    o = _bmm(q_dec, state) + _bmm(attn, v_new)
    return o, state * cdec + _bmm_tn(k_dec, v_new)


DN_INTRA_TOKENS = 512
DN_REC_TOKENS = 512


def _dn_gate_lanes(rev, h):
    lb = (DN_HEADS if rev else 0) + h
    return lb, N_GATE // 2 + lb


def _dn_intra_shapes(bsz, s):
    n = s // CHUNK
    return [jax.ShapeDtypeStruct((bsz, s, DN_WIDTH), F32), jax.ShapeDtypeStruct((bsz, s, DN_WIDTH), BF16),
            jax.ShapeDtypeStruct((bsz, s, DN_WIDTH), BF16), jax.ShapeDtypeStruct((bsz, s, DN_WIDTH), BF16),
            jax.ShapeDtypeStruct((bsz, n, DN_HEADS, CHUNK, CHUNK), BF16), jax.ShapeDtypeStruct((bsz, n, DN_HEADS, 1, LANES), F32)]


def _dn_intra_specs(tb, ix):
    nc = tb // CHUNK
    ix5 = lambda b, j: ix(b, j) + (0, 0)
    row = pl.BlockSpec((1, tb, DN_WIDTH), ix)
    return [row, row, row, row, pl.BlockSpec((1, nc, DN_HEADS, CHUNK, CHUNK), ix5), pl.BlockSpec((1, nc, DN_HEADS, 1, LANES), ix5)]


def _dn_units(nc):
    return [(ci, h) for ci in range(nc) for h in range(DN_HEADS)]


def _dn_load_units(qkv_ref, gb_ref, nc, rev):
    qs, ks, vs, gs, bs = [], [], [], [], []
    for ci, h in _dn_units(nc):
        rows = slice(ci * CHUNK, (ci + 1) * CHUNK)
        lb, lg = _dn_gate_lanes(rev, h)
        qs.append(qkv_ref[0, rows, h * LANES:(h + 1) * LANES])
        ks.append(qkv_ref[0, rows, DN_WIDTH + h * LANES:DN_WIDTH + (h + 1) * LANES])
        vs.append(qkv_ref[0, rows, 2 * DN_WIDTH + h * LANES:2 * DN_WIDTH + (h + 1) * LANES])
        gs.append(gb_ref[0, rows, lg:lg + 1])
        bs.append(gb_ref[0, rows, lb:lb + 1])
    return jnp.stack(qs), jnp.stack(ks), jnp.stack(vs), jnp.stack(gs), jnp.stack(bs)


def _dn_intra_fwd(qkv, gb, rev, name):
    bsz, s, _ = qkv.shape
    tb = min(DN_INTRA_TOKENS, s)
    nc = tb // CHUNK

    def body(qkv_ref, gb_ref, u_ref, w_ref, qd_ref, kd_ref, at_ref, cd_ref):
        q, k, v, g, beta = _dn_load_units(qkv_ref, gb_ref, nc, rev)
        u, w, qd, kd, at, cd = _dn_intra(q, k, v, g, beta, rev)
        for i, (ci, h) in enumerate(_dn_units(nc)):
            rows = slice(ci * CHUNK, (ci + 1) * CHUNK)
            cols = slice(h * LANES, (h + 1) * LANES)
            u_ref[0, rows, cols] = u[i]
            w_ref[0, rows, cols] = w[i].astype(BF16)
            qd_ref[0, rows, cols] = qd[i].astype(BF16)
            kd_ref[0, rows, cols] = kd[i].astype(BF16)
            at_ref[0, ci, h] = at[i].astype(BF16)
            cd_ref[0, ci, h] = cd[i]

    ix = lambda b, j: (b, j, 0)
    return pl.pallas_call(
        body, name=name, grid=(bsz, s // tb),
        in_specs=[pl.BlockSpec((1, tb, 3 * DN_WIDTH), ix), pl.BlockSpec((1, tb, LANES), ix)],
        out_specs=_dn_intra_specs(tb, ix), out_shape=_dn_intra_shapes(bsz, s),
        compiler_params=_cparams(("parallel", "parallel")),
    )(qkv, gb)


def _dn_intra_bwd(qkv, gb, cts, rev, name):
    bsz, s, _ = qkv.shape
    tb = min(DN_INTRA_TOKENS, s)
    nc = tb // CHUNK

    def body(qkv_ref, gb_ref, du_ref, dw_ref, dqd_ref, dkd_ref, dat_ref, dcd_ref, dqkv_ref, dgb_ref):
        units = _dn_units(nc)
        q, k, v, g, beta = _dn_load_units(qkv_ref, gb_ref, nc, rev)
        _, vjp = jax.vjp(functools.partial(_dn_intra, rev=rev), q, k, v, g, beta)
        tok = lambda ref: jnp.stack([ref[0, ci * CHUNK:(ci + 1) * CHUNK, h * LANES:(h + 1) * LANES] for ci, h in units])
        per = lambda ref: jnp.stack([ref[0, ci, h] for ci, h in units])
        dq, dk, dv, dg, dbeta = vjp((tok(du_ref), tok(dw_ref), tok(dqd_ref), tok(dkd_ref), per(dat_ref), per(dcd_ref)))
        lane = lax.broadcasted_iota(jnp.int32, (CHUNK, LANES), 1)
        for ci in range(nc):
            rows = slice(ci * CHUNK, (ci + 1) * CHUNK)
            dgates = jnp.zeros((CHUNK, LANES), F32)
            for h in range(DN_HEADS):
                i = units.index((ci, h))
                lb, lg = _dn_gate_lanes(rev, h)
                dqkv_ref[0, rows, h * LANES:(h + 1) * LANES] = dq[i]
                dqkv_ref[0, rows, DN_WIDTH + h * LANES:DN_WIDTH + (h + 1) * LANES] = dk[i]
                dqkv_ref[0, rows, 2 * DN_WIDTH + h * LANES:2 * DN_WIDTH + (h + 1) * LANES] = dv[i]
                dgates = dgates + jnp.where(lane == lb, dbeta[i], 0.0) + jnp.where(lane == lg, dg[i], 0.0)
            dgb_ref[0, rows, :] = dgates

    ix = lambda b, j: (b, j, 0)
    ix5 = lambda b, j: (b, j, 0, 0, 0)
    row = pl.BlockSpec((1, tb, DN_WIDTH), ix)
    return pl.pallas_call(
        body, name=name, grid=(bsz, s // tb),
        in_specs=[pl.BlockSpec((1, tb, 3 * DN_WIDTH), ix), pl.BlockSpec((1, tb, LANES), ix), row, row, row, row,
                  pl.BlockSpec((1, nc, DN_HEADS, CHUNK, CHUNK), ix5), pl.BlockSpec((1, nc, DN_HEADS, 1, LANES), ix5)],
        out_specs=[pl.BlockSpec((1, tb, 3 * DN_WIDTH), ix), pl.BlockSpec((1, tb, LANES), ix)],
        out_shape=[jax.ShapeDtypeStruct((bsz, s, 3 * DN_WIDTH), F32), jax.ShapeDtypeStruct((bsz, s, LANES), F32)],
        compiler_params=_cparams(("parallel", "parallel")),
    )(qkv, gb, *cts)


def _dn_rec_fwd(intra, rev, name):
    u = intra[0]
    bsz, s, _ = u.shape
    tb = min(DN_REC_TOKENS, s)
    nt = s // tb
    nc = tb // CHUNK

    def body(u_ref, w_ref, qd_ref, kd_ref, at_ref, cd_ref, o_ref, st_ref, state):
        @pl.when(pl.program_id(1) == 0)
        def _():
            state[...] = jnp.zeros_like(state)

        def step(ci, carry):
            cidx = (nc - 1 - ci) if rev else ci
            rows = pl.ds(pl.multiple_of(cidx * CHUNK, CHUNK), CHUNK)
            heads = lambda ref: jnp.stack([ref[0, rows, h * LANES:(h + 1) * LANES] for h in range(DN_HEADS)])
            st = state[...]
            o, new_state = _dn_rec(heads(u_ref), heads(w_ref), heads(qd_ref), heads(kd_ref), at_ref[0, cidx], cd_ref[0, cidx], st)
            st_ref[0, cidx] = st
            state[...] = new_state
            o_ref[0, rows, :] = jnp.concatenate([o[h] for h in range(DN_HEADS)], axis=-1)
            return carry

        lax.fori_loop(0, nc, step, 0)

    ix = (lambda b, j: (b, nt - 1 - j, 0)) if rev else (lambda b, j: (b, j, 0))
    ix5 = lambda b, j: ix(b, j) + (0, 0)
    return pl.pallas_call(
        body, name=name, grid=(bsz, nt), in_specs=_dn_intra_specs(tb, ix),
        out_specs=[pl.BlockSpec((1, tb, DN_WIDTH), ix), pl.BlockSpec((1, nc, DN_HEADS, DN_HEAD_DIM, DN_HEAD_DIM), ix5)],
        out_shape=[jax.ShapeDtypeStruct((bsz, s, DN_WIDTH), F32),
                   jax.ShapeDtypeStruct((bsz, s // CHUNK, DN_HEADS, DN_HEAD_DIM, DN_HEAD_DIM), F32)],
        scratch_shapes=[pltpu.VMEM((DN_HEADS, DN_HEAD_DIM, DN_HEAD_DIM), F32)],
        compiler_params=_cparams(("parallel", "arbitrary")),
    )(*intra)


def _dn_rec_bwd(intra, states, do, rev, name):
    u = intra[0]
    bsz, s, _ = u.shape
    tb = min(DN_REC_TOKENS, s)
    nt = s // tb
    nc = tb // CHUNK

    def body(u_ref, w_ref, qd_ref, kd_ref, at_ref, cd_ref, st_ref, do_ref,
             du_ref, dw_ref, dqd_ref, dkd_ref, dat_ref, dcd_ref, dstate):
        @pl.when(pl.program_id(1) == 0)
        def _():
            dstate[...] = jnp.zeros_like(dstate)

        def step(ci, carry):
            cidx = ci if rev else (nc - 1 - ci)
            rows = pl.ds(pl.multiple_of(cidx * CHUNK, CHUNK), CHUNK)
            heads = lambda ref: jnp.stack([ref[0, rows, h * LANES:(h + 1) * LANES] for h in range(DN_HEADS)])
            args = (heads(u_ref), heads(w_ref).astype(F32), heads(qd_ref).astype(F32), heads(kd_ref).astype(F32),
                    at_ref[0, cidx].astype(F32), cd_ref[0, cidx], st_ref[0, cidx])
            _, vjp = jax.vjp(_dn_rec, *args)
            du, dw, dqd, dkd, dat, dcd, dst = vjp((heads(do_ref), dstate[...]))
            dat_ref[0, cidx] = dat
            dcd_ref[0, cidx] = dcd
            dstate[...] = dst
            for ref, val in ((du_ref, du), (dw_ref, dw), (dqd_ref, dqd), (dkd_ref, dkd)):
                ref[0, rows, :] = jnp.concatenate([val[h] for h in range(DN_HEADS)], axis=-1)
            return carry

        lax.fori_loop(0, nc, step, 0)

    ix = (lambda b, j: (b, j, 0)) if rev else (lambda b, j: (b, nt - 1 - j, 0))
    ix5 = lambda b, j: ix(b, j) + (0, 0)
    row = pl.BlockSpec((1, tb, DN_WIDTH), ix)
    f32 = lambda sd: jax.ShapeDtypeStruct(sd.shape, F32)
    return pl.pallas_call(
        body, name=name, grid=(bsz, nt),
        in_specs=_dn_intra_specs(tb, ix) + [pl.BlockSpec((1, nc, DN_HEADS, DN_HEAD_DIM, DN_HEAD_DIM), ix5), row],
        out_specs=_dn_intra_specs(tb, ix), out_shape=[f32(sd) for sd in _dn_intra_shapes(bsz, s)],
        scratch_shapes=[pltpu.VMEM((DN_HEADS, DN_HEAD_DIM, DN_HEAD_DIM), F32)],
        compiler_params=_cparams(("parallel", "arbitrary")),
    )(*intra, states, do)


def _lru_gate_fn(xc, wa, ba, wx, bx, lam):
    r = _sigmoid(_bdot(xc, wa) + ba)
    ig = _sigmoid(_bdot(xc, wx) + bx)
    log_a = -LRU_C * r * _softplus(-lam)
    a = jnp.exp(log_a)
    b = jnp.sqrt(-jnp.tanh(log_a) * (a * a + 1.0)) * (ig * xc)
    return a, b


def _lru_fwd(proj3, conv_w, conv_b, wa, ba, wx, bx, lam, name):
    bsz, s, _ = proj3.shape
    nblk = LRU_WIDTH // LANES

    def body(x_ref, cw_ref, cb_ref, wa_ref, ba_ref, wx_ref, bx_ref, lam_ref, hf_ref, hb_ref):
        xc = _conv_fwd(x_ref[0], cw_ref[...], 2) + cb_ref[...]
        for d, h_ref in ((0, hf_ref), (1, hb_ref)):
            a, b = _lru_gate_fn(xc, wa_ref[d, 0], ba_ref[d:d + 1, :], wx_ref[d, 0], bx_ref[d:d + 1, :], lam_ref[d:d + 1, :])
            h_ref[0] = _scan(a, b, reverse=(d == 1))

    col = pl.BlockSpec((1, s, LANES), lambda b, j: (b, 0, j))
    vec2 = pl.BlockSpec((2, LANES), lambda b, j: (0, j))
    wspec = pl.BlockSpec((2, 1, LANES, LANES), lambda b, j: (0, j, 0, 0))
    return pl.pallas_call(
        body, name=name, grid=(bsz, nblk),
        in_specs=[col, pl.BlockSpec((4, LANES), lambda b, j: (0, j)), pl.BlockSpec((1, LANES), lambda b, j: (0, j)),
                  wspec, vec2, wspec, vec2, vec2],
        out_specs=[col, col],
        out_shape=[jax.ShapeDtypeStruct((bsz, s, LRU_WIDTH), F32)] * 2,
        compiler_params=_cparams(("parallel", "parallel")),
    )(proj3, conv_w, conv_b, wa, ba, wx, bx, lam)


def _lru_bwd(proj3, conv_w, conv_b, wa, ba, wx, bx, lam, hf, hb, dh, name):
    bsz, s, _ = proj3.shape
    nblk = LRU_WIDTH // LANES

    def body(x_ref, cw_ref, cb_ref, wa_ref, ba_ref, wx_ref, bx_ref, lam_ref, hf_ref, hb_ref, dh_ref,
             dx_ref, dcw_ref, dcb_ref, dwa_ref, dba_ref, dwx_ref, dbx_ref, dlam_ref):
        @pl.when(pl.program_id(1) == 0)
        def _():
            for r in (dcw_ref, dcb_ref, dwa_ref, dba_ref, dwx_ref, dbx_ref, dlam_ref):
                r[...] = jnp.zeros_like(r)

        x, cw = x_ref[0], cw_ref[...]
        xc = _conv_fwd(x, cw, 2) + cb_ref[...]
        dhv = dh_ref[0]
        dxc = jnp.zeros_like(xc)
        for d, h_ref in ((0, hf_ref), (1, hb_ref)):
            rev = d == 1
            args = (xc, wa_ref[d, 0].astype(F32), ba_ref[d:d + 1, :], wx_ref[d, 0].astype(F32), bx_ref[d:d + 1, :],
                    lam_ref[d:d + 1, :])
            (a, _), vjp = jax.vjp(_lru_gate_fn, *args)
            h = h_ref[0]
            a_next = _shift_down(a, 1) if rev else _shift_up(a, 1)
            lam_adj = _scan(a_next, dhv, reverse=not rev)
            h_prev = _shift_up(h, 1) if rev else _shift_down(h, 1)
            dxc_d, dwa, dba, dwx, dbx, dlam = vjp((lam_adj * h_prev, lam_adj))
            dxc = dxc + dxc_d
            dwa_ref[d, 0] += dwa
            dwx_ref[d, 0] += dwx
            dba_ref[d:d + 1, :] += dba
            dbx_ref[d:d + 1, :] += dbx
            dlam_ref[d:d + 1, :] += dlam
        dx, dcw = _conv_bwd(x, cw, 2, dxc)
        dx_ref[0] = dx.astype(dx_ref.dtype)
        dcw_ref[...] += dcw
        dcb_ref[...] += jnp.sum(dxc, axis=0, keepdims=True)

    col = pl.BlockSpec((1, s, LANES), lambda j, b: (b, 0, j))
    vec1 = pl.BlockSpec((1, LANES), lambda j, b: (0, j))
    vec2 = pl.BlockSpec((2, LANES), lambda j, b: (0, j))
    vec4 = pl.BlockSpec((4, LANES), lambda j, b: (0, j))
    wspec = pl.BlockSpec((2, 1, LANES, LANES), lambda j, b: (0, j, 0, 0))
    wshape = jax.ShapeDtypeStruct((2, nblk, LANES, LANES), F32)
    v2shape = jax.ShapeDtypeStruct((2, LRU_WIDTH), F32)
    return pl.pallas_call(
        body, name=name, grid=(nblk, bsz),
        in_specs=[col, vec4, vec1, wspec, vec2, wspec, vec2, vec2, col, col, col],
        out_specs=[col, vec4, vec1, wspec, vec2, wspec, vec2, vec2],
        out_shape=[jax.ShapeDtypeStruct((bsz, s, LRU_WIDTH), BF16), jax.ShapeDtypeStruct((4, LRU_WIDTH), F32),
                   jax.ShapeDtypeStruct((1, LRU_WIDTH), F32), wshape, v2shape, wshape, v2shape, v2shape],
        compiler_params=_cparams(("parallel", "arbitrary")),
    )(proj3, conv_w, conv_b, wa, ba, wx, bx, lam, hf, hb, dh)


def _mix_fn(o_f, o_b, z, lg, hf, hb, dn_g, lru_g):
    osum = o_f + o_b
    heads = []
    for h in range(DN_HEADS):
        sl = slice(h * LANES, (h + 1) * LANES)
        heads.append(_rms(osum[:, sl], dn_g) * _silu(z[:, sl]))
    lru = _rms(_gelu(lg) * (hf + hb), lru_g)
    return jnp.concatenate(heads + [lru], axis=-1)


def _mix_specs(tm):
    w = DN_WIDTH
    row = pl.BlockSpec((tm, w), lambda i: (i, 0))
    z = pl.BlockSpec((tm, w), lambda i: (i, P_Z // w))
    lg = pl.BlockSpec((tm, w), lambda i: (i, P_LG // w))
    return [row, row, z, lg, row, row, pl.BlockSpec((1, LANES), lambda i: (0, 0)), pl.BlockSpec((1, w), lambda i: (0, 0))]


def _mix_fwd(o_f, o_b, proj, hf, hb, dn_g, lru_g, name):
    t = proj.shape[0]
    tm = _row_tile(t)

    def body(of_ref, ob_ref, z_ref, lg_ref, hf_ref, hb_ref, dg_ref, lgn_ref, o_ref):
        o_ref[...] = _mix_fn(of_ref[...], ob_ref[...], z_ref[...], lg_ref[...], hf_ref[...], hb_ref[...],
                             dg_ref[...], lgn_ref[...]).astype(o_ref.dtype)

    return pl.pallas_call(
        body, name=name, grid=(t // tm,), in_specs=_mix_specs(tm),
        out_specs=pl.BlockSpec((tm, D_MODEL), lambda i: (i, 0)),
        out_shape=jax.ShapeDtypeStruct((t, D_MODEL), BF16),
        compiler_params=_cparams(("parallel",)),
    )(o_f, o_b, proj, proj, hf, hb, dn_g, lru_g)


def _mix_bwd(o_f, o_b, proj, hf, hb, dn_g, lru_g, dmix, name):
    t = proj.shape[0]
    tm = _row_tile(t)

    def body(of_ref, ob_ref, z_ref, lg_ref, hf_ref, hb_ref, dg_ref, lgn_ref, dm_ref,
             do_ref, dz_ref, dlg_ref, dh_ref, ddg_ref, dlgn_ref):
        _, vjp = jax.vjp(_mix_fn, of_ref[...], ob_ref[...], z_ref[...], lg_ref[...], hf_ref[...], hb_ref[...],
                         dg_ref[...], lgn_ref[...])
        do, _, dz, dlg, dh, _, ddg, dlgn = vjp(dm_ref[...])
        do_ref[...] = do
        dz_ref[...] = dz.astype(dz_ref.dtype)
        dlg_ref[...] = dlg.astype(dlg_ref.dtype)
        dh_ref[...] = dh

        @pl.when(pl.program_id(0) == 0)
        def _():
            ddg_ref[...] = jnp.zeros_like(ddg_ref)
            dlgn_ref[...] = jnp.zeros_like(dlgn_ref)

        ddg_ref[...] += ddg
        dlgn_ref[...] += dlgn

    row = pl.BlockSpec((tm, DN_WIDTH), lambda i: (i, 0))
    return pl.pallas_call(
        body, name=name, grid=(t // tm,),
        in_specs=_mix_specs(tm) + [pl.BlockSpec((tm, D_MODEL), lambda i: (i, 0))],
        out_specs=[row, row, row, row, pl.BlockSpec((1, LANES), lambda i: (0, 0)), pl.BlockSpec((1, DN_WIDTH), lambda i: (0, 0))],
        out_shape=[jax.ShapeDtypeStruct((t, DN_WIDTH), dt) for dt in (F32, BF16, BF16, F32)]
        + [jax.ShapeDtypeStruct((1, LANES), F32), jax.ShapeDtypeStruct((1, DN_WIDTH), F32)],
        compiler_params=_cparams(("arbitrary",)),
    )(o_f, o_b, proj, proj, hf, hb, dn_g, lru_g, dmix)


def _ffn_act_fwd(g3, u3, conv_w, conv_b, name):
    bsz, s, f = g3.shape
    nblk = f // LANES

    def body(g_ref, u_ref, w_ref, b_ref, o_ref):
        gate = _conv_fwd(g_ref[0], w_ref[...], 1) + b_ref[...]
        o_ref[0] = (_gelu(gate) * u_ref[0]).astype(o_ref.dtype)

    col = pl.BlockSpec((1, s, LANES), lambda b, j: (b, 0, j))
    return pl.pallas_call(
        body, name=name, grid=(bsz, nblk),
        in_specs=[col, col, pl.BlockSpec((3, LANES), lambda b, j: (0, j)), pl.BlockSpec((1, LANES), lambda b, j: (0, j))],
        out_specs=col, out_shape=jax.ShapeDtypeStruct((bsz, s, f), BF16),
        compiler_params=_cparams(("parallel", "parallel")),
    )(g3, u3, conv_w, conv_b)


def _ffn_act_bwd(g3, u3, conv_w, conv_b, dact3, name):
    bsz, s, f = g3.shape
    nblk = f // LANES

    def body(g_ref, u_ref, w_ref, b_ref, d_ref, dg_ref, du_ref, dw_ref, db_ref):
        g, w, u = g_ref[0], w_ref[...], u_ref[0]
        gate = _conv_fwd(g, w, 1) + b_ref[...]
        _, vjp = jax.vjp(lambda gt, uu: _gelu(gt) * uu, gate, u)
        dgate, du = vjp(d_ref[0])
        dg, dw = _conv_bwd(g, w, 1, dgate)
        dg_ref[0] = dg.astype(dg_ref.dtype)
        du_ref[0] = du.astype(du_ref.dtype)

        @pl.when(pl.program_id(1) == 0)
        def _():
            dw_ref[...] = jnp.zeros_like(dw_ref)
            db_ref[...] = jnp.zeros_like(db_ref)

        dw_ref[...] += dw
        db_ref[...] += jnp.sum(dgate, axis=0, keepdims=True)

    col = pl.BlockSpec((1, s, LANES), lambda j, b: (b, 0, j))
    w3 = pl.BlockSpec((3, LANES), lambda j, b: (0, j))
    w1 = pl.BlockSpec((1, LANES), lambda j, b: (0, j))
    return pl.pallas_call(
        body, name=name, grid=(nblk, bsz),
        in_specs=[col, col, w3, w1, col], out_specs=[col, col, w3, w1],
        out_shape=[jax.ShapeDtypeStruct((bsz, s, f), BF16), jax.ShapeDtypeStruct((bsz, s, f), BF16),
                   jax.ShapeDtypeStruct((3, f), F32), jax.ShapeDtypeStruct((1, f), F32)],
        compiler_params=_cparams(("parallel", "arbitrary")),
    )(g3, u3, conv_w, conv_b, dact3)


def _ple_fn(r, pg, pp, bg):
    return r + _sigmoid(pg + bg) * pp


def _ple_fwd(r, pg, pp, bg, name):
    t, d = r.shape
    tm = _row_tile(t)

    def body(r_ref, pg_ref, pp_ref, bg_ref, o_ref):
        o_ref[...] = _ple_fn(r_ref[...], pg_ref[...], pp_ref[...], bg_ref[...])

    row = pl.BlockSpec((tm, d), lambda i: (i, 0))
    return pl.pallas_call(
        body, name=name, grid=(t // tm,), in_specs=[row, row, row, pl.BlockSpec((1, d), lambda i: (0, 0))],
        out_specs=row, out_shape=jax.ShapeDtypeStruct((t, d), F32),
        compiler_params=_cparams(("parallel",)),
    )(r, pg, pp, bg)


def _ple_bwd(pg, pp, bg, dr, name):
    t, d = pg.shape
    tm = _row_tile(t)

    def body(pg_ref, pp_ref, bg_ref, dr_ref, dpg_ref, dpp_ref, dbg_ref):
        _, vjp = jax.vjp(lambda a, b, c: _sigmoid(a + c) * b, pg_ref[...], pp_ref[...], bg_ref[...])
        dpg, dpp, dbg = vjp(dr_ref[...])
        dpg_ref[...] = dpg.astype(dpg_ref.dtype)
        dpp_ref[...] = dpp.astype(dpp_ref.dtype)

        @pl.when(pl.program_id(0) == 0)
        def _():
            dbg_ref[...] = jnp.zeros_like(dbg_ref)

        dbg_ref[...] += dbg

    row = pl.BlockSpec((tm, d), lambda i: (i, 0))
    vec = pl.BlockSpec((1, d), lambda i: (0, 0))
    return pl.pallas_call(
        body, name=name, grid=(t // tm,), in_specs=[row, row, vec, row], out_specs=[row, row, vec],
        out_shape=[jax.ShapeDtypeStruct((t, d), BF16), jax.ShapeDtypeStruct((t, d), BF16), jax.ShapeDtypeStruct((1, d), F32)],
        compiler_params=_cparams(("arbitrary",)),
    )(pg, pp, bg, dr)


def _loss_head(r, g, target, name):
    t, d = r.shape
    tm = _row_tile(t)

    def loss_fn(x, gg, tgt):
        err = _rms(x, gg) - tgt
        return 0.5 * jnp.sum(jnp.sum(err * err, axis=-1, keepdims=True) * (1.0 / d), axis=0, keepdims=True)

    def body(r_ref, g_ref, t_ref, l_ref, dr_ref, dg_ref):
        val, vjp = jax.vjp(lambda x, gg: loss_fn(x, gg, t_ref[...]), r_ref[...], g_ref[...])
        dx, dg = vjp(jnp.ones((1, 1), F32))
        dr_ref[...] = dx

        @pl.when(pl.program_id(0) == 0)
        def _():
            l_ref[...] = jnp.zeros_like(l_ref)
            dg_ref[...] = jnp.zeros_like(dg_ref)

        l_ref[...] += val
        dg_ref[...] += dg

    row = pl.BlockSpec((tm, d), lambda i: (i, 0))
    vec = pl.BlockSpec((1, d), lambda i: (0, 0))
    one = pl.BlockSpec((1, 1), lambda i: (0, 0))
    return pl.pallas_call(
        body, name=name, grid=(t // tm,), in_specs=[row, vec, row], out_specs=[one, row, vec],
        out_shape=[jax.ShapeDtypeStruct((1, 1), F32), jax.ShapeDtypeStruct((t, d), F32), jax.ShapeDtypeStruct((1, d), F32)],
        compiler_params=_cparams(("arbitrary",)),
    )(r, g, target)


def _adamw_math(w, gg, m, v, d_ref, nm_ref, nv_ref):
    nm = ADAM_B1 * m + (1.0 - ADAM_B1) * gg
    nv = ADAM_B2 * v + (1.0 - ADAM_B2) * (gg * gg)
    m_hat = nm / (1.0 - ADAM_B1 ** ADAM_STEP)
    v_hat = nv / (1.0 - ADAM_B2 ** ADAM_STEP)
    d_ref[...] = -ADAM_LR * (m_hat / (jnp.sqrt(v_hat) + ADAM_EPS) + ADAM_WD * w)
    nm_ref[...] = nm
    nv_ref[...] = nv


def _adamw(w, g, m, v, name):
    r, c = w.shape
    tr = r if r <= 512 else _row_tile(r, 512)

    def body(w_ref, g_ref, m_ref, v_ref, d_ref, nm_ref, nv_ref):
        _adamw_math(w_ref[...], g_ref[...], m_ref[...], v_ref[...], d_ref, nm_ref, nv_ref)

    blk = pl.BlockSpec((tr, c), lambda i: (i, 0))
    return pl.pallas_call(
        body, name=name, grid=(r // tr,), in_specs=[blk] * 4, out_specs=[blk] * 3,
        out_shape=[jax.ShapeDtypeStruct((r, c), F32)] * 3,
        compiler_params=_cparams(("parallel",)),
    )(w, g, m, v)


def _prepare_weights(w):
    nl = w["w_in"].shape[0]
    w_in = w["w_in"].astype(BF16)
    pad = jnp.zeros(w_in.shape[:2] + (P_COLS - w_in.shape[2],), BF16)
    split = 4 * DN_WIDTH + N_GATE
    w_in_p = jnp.concatenate([w_in[:, :, split:], w_in[:, :, :split], pad], axis=-1)
    gate_vec = lambda a: jnp.pad(a.reshape(nl, 1, N_GATE // 2), ((0, 0), (0, 0), (N_GATE // 2, LANES - N_GATE)))

    def pair_blocks(a):
        a = a.reshape(nl, 2, 4, 2, 64, 64)
        z = jnp.zeros_like(a[:, :, :, 0])
        top = jnp.concatenate([a[:, :, :, 0], z], axis=-1)
        bot = jnp.concatenate([z, a[:, :, :, 1]], axis=-1)
        return jnp.concatenate([top, bot], axis=-2).astype(BF16)

    bf = lambda a: a.astype(BF16)
    return dict(
        norm1_g=w["norm1_g"][:, None, :], w_in=w_in_p,
        dn_conv_w=w["dn_conv_w"], alog=gate_vec(w["dn_a_log"]), dtb=gate_vec(w["dn_dt_bias"]),
        dn_norm_g=w["dn_norm_g"][:, None, :], lru_conv_w=w["lru_conv_w"], lru_conv_b=w["lru_conv_b"][:, None, :],
        lru_wa=pair_blocks(w["lru_wa"]), lru_ba=w["lru_ba"], lru_wx=pair_blocks(w["lru_wx"]), lru_bx=w["lru_bx"],
        lru_lambda=w["lru_lambda"], lru_norm_g=w["lru_norm_g"][:, None, :],
        w_out=bf(w["w_out"]), norm2_g=w["norm2_g"][:, None, :], ffn_wg=bf(w["ffn_wg"]), ffn_wu=bf(w["ffn_wu"]),
        ffn_conv_w=w["ffn_conv_w"], ffn_conv_b=w["ffn_conv_b"][:, None, :], ffn_wd=bf(w["ffn_wd"]),
        ple_norm_g=w["ple_norm_g"][:, None, :], ple_wg=bf(w["ple_wg"]), ple_bg=w["ple_bg"][:, None, :], ple_wp=bf(w["ple_wp"]),
        final_g=w["final_g"][None, :],
    )


def _unpair_blocks(a):
    top = a[:, :, :64, :64]
    bot = a[:, :, 64:, 64:]
    return jnp.stack([top, bot], axis=2).reshape(2, 8, 64, 64)


def _local_step(x, p, target, w):
    bsz, s, d = x.shape
    t = bsz * s
    nl = w["w_in"].shape[0]
    kw = _prepare_weights(w)
    flat = lambda a: a.reshape(t, a.shape[-1])
    seq = lambda a: a.reshape(bsz, s, a.shape[-1])

    saved = []
    r = flat(x)
    for i in range(nl):
        n = f"l{i}_"
        sv = {"r0": r}
        h = _rmsnorm_fwd(r, kw["norm1_g"][i], n + "norm1")
        proj = _matmul([(h, kw["w_in"][i])], n + "in_proj")
        gb = _dn_gates_fwd(proj, kw["alog"][i], kw["dtb"][i], n + "dn_gates")
        qkv = _dn_prep_fwd(seq(proj), kw["dn_conv_w"][i], n + "dn_prep")
        in_f = _dn_intra_fwd(qkv, seq(gb), False, n + "dn_intra_fwd")
        in_b = _dn_intra_fwd(qkv, seq(gb), True, n + "dn_intra_rev")
        o_f, st_f = _dn_rec_fwd(in_f, False, n + "dn_rec_fwd")
        o_b, st_b = _dn_rec_fwd(in_b, True, n + "dn_rec_rev")
        lru_args = (seq(proj), kw["lru_conv_w"][i], kw["lru_conv_b"][i], kw["lru_wa"][i], kw["lru_ba"][i],
                    kw["lru_wx"][i], kw["lru_bx"][i], kw["lru_lambda"][i])
        hf, hb = _lru_fwd(*lru_args, n + "lru")
        mix_args = (flat(o_f), flat(o_b), proj, flat(hf), flat(hb), kw["dn_norm_g"][i], kw["lru_norm_g"][i])
        mix = _mix_fwd(*mix_args, n + "mix")
        r1 = _matmul([(mix, kw["w_out"][i])], n + "out_proj", res=r)
        h2 = _rmsnorm_fwd(r1, kw["norm2_g"][i], n + "norm2")
        fg = _matmul([(h2, kw["ffn_wg"][i])], n + "ffn_g")
        fu = _matmul([(h2, kw["ffn_wu"][i])], n + "ffn_u")
        act = _ffn_act_fwd(seq(fg), seq(fu), kw["ffn_conv_w"][i], kw["ffn_conv_b"][i], n + "ffn_act")
        r2 = _matmul([(flat(act), kw["ffn_wd"][i])], n + "ffn_d", res=r1)
        hp = _rmsnorm_fwd(r2, kw["ple_norm_g"][i], n + "ple_norm")
        pg = _matmul([(hp, kw["ple_wg"][i])], n + "ple_g")
        pi = flat(p[i])
        pp = _matmul([(pi, kw["ple_wp"][i])], n + "ple_p")
        r3 = _ple_fwd(r2, pg, pp, kw["ple_bg"][i], n + "ple")
        sv.update(h=h, proj=proj, gb=gb, qkv=qkv, st_f=st_f, st_b=st_b, in_f=in_f, in_b=in_b, lru_args=lru_args, hf=hf, hb=hb,
                  mix_args=mix_args, mix=mix, r1=r1, h2=h2, fg=fg, fu=fu, act=act, r2=r2, hp=hp, pg=pg, pp=pp, pi=pi)
        saved.append(sv)
        r = r3

    loss, dr, dfinal = _loss_head(r, kw["final_g"], flat(target), "loss_head")
    grads = {k: [None] * nl for k in (
        "norm1_g", "w_in", "dn_conv_w", "dn_a_log", "dn_dt_bias", "dn_norm_g", "lru_conv_w", "lru_conv_b", "lru_wa", "lru_ba",
        "lru_wx", "lru_bx", "lru_lambda", "lru_norm_g", "w_out", "norm2_g", "ffn_wg", "ffn_wu", "ffn_conv_w", "ffn_conv_b",
        "ffn_wd", "ple_norm_g", "ple_wg", "ple_bg", "ple_wp", "dn_a_log_lanes", "dn_dt_bias_lanes")}

    for i in reversed(range(nl)):
        n = f"b{i}_"
        sv = saved[i]
        dpg, dpp, dbg = _ple_bwd(sv["pg"], sv["pp"], kw["ple_bg"][i], dr, n + "ple")
        grads["ple_bg"][i] = dbg[0]
        grads["ple_wp"][i] = _matmul_tn(sv["pi"], dpp, n + "ple_wp")
        grads["ple_wg"][i] = _matmul_tn(sv["hp"], dpg, n + "ple_wg")
        dhp = _matmul([(dpg, kw["ple_wg"][i])], n + "ple_dh", nt=True)
        dr2, dr2b, dg = _rmsnorm_bwd(sv["r2"], kw["ple_norm_g"][i], dhp, dr, n + "ple_norm")
        grads["ple_norm_g"][i] = dg[0]
        grads["ffn_wd"][i] = _matmul_tn(flat(sv["act"]), dr2b, n + "ffn_wd")
        dact = _matmul([(dr2b, kw["ffn_wd"][i])], n + "ffn_dact", nt=True)
        dfg, dfu, dcw, dcb = _ffn_act_bwd(seq(sv["fg"]), seq(sv["fu"]), kw["ffn_conv_w"][i], kw["ffn_conv_b"][i], seq(dact),
                                          n + "ffn_act")
        grads["ffn_conv_w"][i] = dcw
        grads["ffn_conv_b"][i] = dcb[0]
        grads["ffn_wg"][i] = _matmul_tn(sv["h2"], flat(dfg), n + "ffn_wg")
        grads["ffn_wu"][i] = _matmul_tn(sv["h2"], flat(dfu), n + "ffn_wu")
        dh2 = _matmul([(flat(dfg), kw["ffn_wg"][i]), (flat(dfu), kw["ffn_wu"][i])], n + "ffn_dh", nt=True)
        dr1, dr1b, dg = _rmsnorm_bwd(sv["r1"], kw["norm2_g"][i], dh2, dr2, n + "norm2")
        grads["norm2_g"][i] = dg[0]
        grads["w_out"][i] = _matmul_tn(sv["mix"], dr1b, n + "w_out")
        dmix = _matmul([(dr1b, kw["w_out"][i])], n + "dmix", nt=True)
        do, dz, dlg, dh, ddn_g, dlru_g = _mix_bwd(*sv["mix_args"], dmix, n + "mix")
        grads["dn_norm_g"][i] = ddn_g[0]
        grads["lru_norm_g"][i] = dlru_g[0]
        dlx, dcw, dcb, dwa, dba, dwx, dbx, dlam = _lru_bwd(*sv["lru_args"], sv["hf"], sv["hb"], seq(dh), n + "lru")
        grads["lru_conv_w"][i] = dcw
        grads["lru_conv_b"][i] = dcb[0]
        grads["lru_wa"][i] = _unpair_blocks(dwa)
        grads["lru_wx"][i] = _unpair_blocks(dwx)
        grads["lru_ba"][i], grads["lru_bx"][i], grads["lru_lambda"][i] = dba, dbx, dlam
        ct_f = _dn_rec_bwd(sv["in_f"], sv["st_f"], seq(do), False, n + "dn_rec_fwd")
        ct_b = _dn_rec_bwd(sv["in_b"], sv["st_b"], seq(do), True, n + "dn_rec_rev")
        dqkv_f, dgb_f = _dn_intra_bwd(sv["qkv"], seq(sv["gb"]), ct_f, False, n + "dn_intra_fwd")
        dqkv_b, dgb_b = _dn_intra_bwd(sv["qkv"], seq(sv["gb"]), ct_b, True, n + "dn_intra_rev")
        dgate, dalog, ddtb = _dn_gates_bwd(sv["proj"], kw["alog"][i], kw["dtb"][i], flat(dgb_f), flat(dgb_b), n + "dn_gates")
        grads["dn_a_log"][i] = dalog[0, N_GATE // 2:N_GATE].reshape(2, DN_HEADS)
        grads["dn_dt_bias"][i] = ddtb[0, N_GATE // 2:N_GATE].reshape(2, DN_HEADS)
        grads["dn_a_log_lanes"][i] = dalog[0]
        grads["dn_dt_bias_lanes"][i] = ddtb[0]
        dpqkv, dcw = _dn_prep_bwd(seq(sv["proj"]), kw["dn_conv_w"][i], dqkv_f, dqkv_b, n + "dn_prep")
        grads["dn_conv_w"][i] = dcw
        segs = [flat(dlx), dlg, flat(dpqkv), dz, dgate]
        w_in_i = kw["w_in"][i]
        dq3 = flat(dpqkv)
        wseg = DN_WIDTH
        terms = [(flat(dlx), (w_in_i, wseg, P_LX // wseg)), (dlg, (w_in_i, wseg, P_LG // wseg))]
        terms += [((dq3, wseg, j), (w_in_i, wseg, P_Q // wseg + j)) for j in range(3)]
        terms += [(dz, (w_in_i, wseg, P_Z // wseg)), (dgate, (w_in_i, LANES, P_GATE // LANES))]
        dh1 = _matmul(terms, n + "in_dh", nt=True)
        dwp = jnp.concatenate([_matmul_tn(sv["h"], sg, n + f"w_in{j}") for j, sg in enumerate(segs)], axis=-1)
        split = 4 * DN_WIDTH + N_GATE
        grads["w_in"][i] = jnp.concatenate([dwp[:, P_Q:P_Q + split], dwp[:, :P_Q]], axis=-1)
        dr, _, dg = _rmsnorm_bwd(sv["r0"], kw["norm1_g"][i], dh1, dr1, n + "norm1")
        grads["norm1_g"][i] = dg[0]

    grads["final_g"] = dfinal[0]
    return loss[0, 0], dr.reshape(bsz, s, d), grads


MESH = pl.DeviceIdType.MESH
ANY = pl.BlockSpec(memory_space=pl.ANY)
N_CHIPS = 4
HALF = 2


def _place():
    x, y, c = lax.axis_index("x"), lax.axis_index("y"), lax.axis_index("c")
    chips = [(1 - x, y), (x, 1 - y), (1 - x, 1 - y)]
    return x, y, c, chips


def _dma_sems(n):
    return pltpu.SemaphoreType.DMA((n,))


def _gather_weights(shards):
    nt = len(shards)

    def body(*refs):
        w_refs, g_refs = refs[:nt], refs[nt:2 * nt]
        send_sems, recv_sems = refs[2 * nt:]
        x, y, c, chips = _place()
        me = 2 * x + y
        mine = pl.ds(c * HALF, HALF)
        theirs = pl.ds((1 - c) * HALF, HALF)

        def rc(src, dst, k, to):
            return pltpu.make_async_remote_copy(src_ref=src, dst_ref=dst, send_sem=send_sems.at[k], recv_sem=recv_sems.at[k],
                                                device_id=to, device_id_type=MESH)

        first = []
        for t in range(nt):
            for r, (px, py) in enumerate(chips):
                first.append(rc(w_refs[t].at[mine], g_refs[t].at[me, mine], 6 * t + r, (px, py, c)))
        for cp in first:
            cp.start()
        passed = []
        for t in range(nt):
            for r, (px, py) in enumerate(chips):
                peer = 2 * px + py
                rc(w_refs[t].at[mine], g_refs[t].at[peer, mine], 6 * t + r, (px, py, c)).wait_recv()
                fw = rc(g_refs[t].at[peer, mine], g_refs[t].at[peer, mine], 6 * t + 3 + r, (x, y, 1 - c))
                fw.start()
                passed.append(fw)
        for t in range(nt):
            for r, (px, py) in enumerate(chips):
                peer = 2 * px + py
                rc(g_refs[t].at[peer, theirs], g_refs[t].at[peer, theirs], 6 * t + 3 + r, (x, y, 1 - c)).wait_recv()
        for cp in first + passed:
            cp.wait_send()

    return pl.pallas_call(
        body, name="gather_weights", in_specs=[ANY] * nt, out_specs=[ANY] * nt,
        out_shape=[jax.ShapeDtypeStruct((N_CHIPS,) + a.shape, a.dtype) for a in shards],
        scratch_shapes=[_dma_sems(6 * nt), _dma_sems(6 * nt)],
    )(*shards)


def _swap_halves(blocks):
    nt = len(blocks)

    def body(*refs):
        p_refs, l_refs = refs[:nt], refs[nt:2 * nt]
        send_sems, recv_sems = refs[2 * nt:]
        x, y, c, _ = _place()
        theirs = pl.ds((1 - c) * HALF, HALF)
        cps = [pltpu.make_async_remote_copy(src_ref=p_refs[t].at[:, theirs], dst_ref=l_refs[t], send_sem=send_sems.at[t],
                                            recv_sem=recv_sems.at[t], device_id=(x, y, 1 - c), device_id_type=MESH)
               for t in range(nt)]
        for cp in cps:
            cp.start()
        for cp in cps:
            cp.wait_send()
            cp.wait_recv()

    return pl.pallas_call(
        body, name="swap_halves", in_specs=[ANY] * nt, out_specs=[ANY] * nt,
        out_shape=[jax.ShapeDtypeStruct((a.shape[0], HALF) + a.shape[2:], a.dtype) for a in blocks],
        scratch_shapes=[_dma_sems(nt), _dma_sems(nt)],
    )(*blocks)


def _tile_rows(a, row_bytes):
    best = None
    for d in range(16, a + 1, 16):
        if a % d == 0 and d * row_bytes <= 2 * 1024 * 1024:
            best = d
    return best if best is not None else a


def _add_halves(pg, l1, c_arr, name):
    n, _, a, b = pg.shape
    ta = _tile_rows(a, 4 * b)

    def body(c_ref, a_ref, b_ref, o_ref):
        o_ref[...] = (a_ref[...] + b_ref[...]).astype(o_ref.dtype)

    grid_spec = pltpu.PrefetchScalarGridSpec(
        num_scalar_prefetch=1, grid=(n, HALF, a // ta),
        in_specs=[pl.BlockSpec((1, 1, ta, b), lambda k, l, i, c_ref: (k, c_ref[0] * HALF + l, i, 0)),
                  pl.BlockSpec((1, 1, ta, b), lambda k, l, i, c_ref: (k, l, i, 0))],
        out_specs=pl.BlockSpec((1, 1, ta, b), lambda k, l, i, c_ref: (k, l, i, 0)))
    return pl.pallas_call(
        body, name=name, grid_spec=grid_spec, out_shape=jax.ShapeDtypeStruct((n, HALF, a, b), BF16),
        compiler_params=_cparams(("parallel", "parallel", "parallel")),
    )(c_arr, pg, l1)


def _scatter_chips(blocks):
    nt = len(blocks)

    def body(*refs):
        q_refs, l_refs = refs[:nt], refs[nt:2 * nt]
        send_sems, recv_sems = refs[2 * nt:]
        x, y, c, chips = _place()
        me = 2 * x + y

        def rc(t, r, src_chip, dst_chip, to):
            return pltpu.make_async_remote_copy(src_ref=q_refs[t].at[src_chip], dst_ref=l_refs[t].at[dst_chip],
                                                send_sem=send_sems.at[3 * t + r], recv_sem=recv_sems.at[3 * t + r],
                                                device_id=to, device_id_type=MESH)

        sends = [rc(t, r, 2 * px + py, me, (px, py, c)) for t in range(nt) for r, (px, py) in enumerate(chips)]
        for cp in sends:
            cp.start()
        for t in range(nt):
            for r, (px, py) in enumerate(chips):
                rc(t, r, 2 * px + py, 2 * px + py, (px, py, c)).wait_recv()
        for cp in sends:
            cp.wait_send()

    return pl.pallas_call(
        body, name="scatter_chips", in_specs=[ANY] * nt, out_specs=[ANY] * nt,
        out_shape=[jax.ShapeDtypeStruct(a.shape, a.dtype) for a in blocks],
        scratch_shapes=[_dma_sems(3 * nt), _dma_sems(3 * nt)],
    )(*blocks)


def _sum_chips(l2, own, me_arr, name):
    n, _, a, b = l2.shape
    ta = _tile_rows(a, 4 * b)

    def body(me_ref, a_ref, own_ref, o_ref):
        me = me_ref[0]
        acc = jnp.where(me == 0, own_ref[0, 0], a_ref[0, 0]).astype(F32)
        for k in range(1, n):
            acc = acc + jnp.where(me == k, own_ref[0, 0], a_ref[k, 0]).astype(F32)
        o_ref[0] = acc

    grid_spec = pltpu.PrefetchScalarGridSpec(
        num_scalar_prefetch=1, grid=(HALF, a // ta),
        in_specs=[pl.BlockSpec((n, 1, ta, b), lambda l, i, me_ref: (0, l, i, 0)),
                  pl.BlockSpec((1, 1, ta, b), lambda l, i, me_ref: (me_ref[0], l, i, 0))],
        out_specs=pl.BlockSpec((1, ta, b), lambda l, i, me_ref: (l, i, 0)))
    return pl.pallas_call(
        body, name=name, grid_spec=grid_spec, out_shape=jax.ShapeDtypeStruct((HALF, a, b), F32),
        compiler_params=_cparams(("parallel", "parallel")),
    )(me_arr, l2, own)


def _swap_reduced(parts):
    nt = len(parts)

    def body(*refs):
        r_refs, g_refs = refs[:nt], refs[nt:2 * nt]
        send_sems, recv_sems = refs[2 * nt:]
        x, y, c, _ = _place()
        cps = [pltpu.make_async_remote_copy(src_ref=r_refs[t], dst_ref=g_refs[t], send_sem=send_sems.at[t],
                                            recv_sem=recv_sems.at[t], device_id=(x, y, 1 - c), device_id_type=MESH)
               for t in range(nt)]
        for cp in cps:
            cp.start()
        for cp in cps:
            cp.wait_send()
            cp.wait_recv()

    return pl.pallas_call(
        body, name="swap_reduced", in_specs=[ANY] * nt, out_specs=[ANY] * nt,
        out_shape=[jax.ShapeDtypeStruct(a.shape, a.dtype) for a in parts],
        scratch_shapes=[_dma_sems(nt), _dma_sems(nt)],
    )(*parts)


def _adamw_halves(w, mine, theirs, m, v, c_arr, name):
    nl, a, b = w.shape
    ta = _tile_rows(a, 4 * b)

    def body(c_ref, w_ref, g1_ref, g2_ref, m_ref, v_ref, g_ref, d_ref, nm_ref, nv_ref):
        gg = jnp.where(pl.program_id(0) // HALF == c_ref[0], g1_ref[...], g2_ref[...])
        g_ref[...] = gg
        _adamw_math(w_ref[...], gg, m_ref[...], v_ref[...], d_ref, nm_ref, nv_ref)

    full = pl.BlockSpec((1, ta, b), lambda l, i, c_ref: (l, i, 0))
    half = pl.BlockSpec((1, ta, b), lambda l, i, c_ref: (l % HALF, i, 0))
    grid_spec = pltpu.PrefetchScalarGridSpec(num_scalar_prefetch=1, grid=(nl, a // ta),
                                             in_specs=[full, half, half, full, full], out_specs=[full] * 4)
    return pl.pallas_call(
        body, name=name, grid_spec=grid_spec, out_shape=[jax.ShapeDtypeStruct(w.shape, F32)] * 4,
        compiler_params=_cparams(("parallel", "parallel")),
    )(c_arr, w, mine, theirs, m, v)


def _gather_chips(v):
    def body(v_ref, g_ref, send_sems, recv_sems, local_sem):
        x, y, c, chips = _place()
        me = 2 * x + y
        loc = pltpu.make_async_copy(v_ref, g_ref.at[me], local_sem)
        loc.start()
        sends = [pltpu.make_async_remote_copy(src_ref=v_ref, dst_ref=g_ref.at[me], send_sem=send_sems.at[r], recv_sem=recv_sems.at[r],
                                              device_id=(px, py, c), device_id_type=MESH) for r, (px, py) in enumerate(chips)]
        for cp in sends:
            cp.start()
        for r, (px, py) in enumerate(chips):
            pltpu.make_async_remote_copy(src_ref=v_ref, dst_ref=g_ref.at[2 * px + py], send_sem=send_sems.at[r],
                                         recv_sem=recv_sems.at[r], device_id=(px, py, c), device_id_type=MESH).wait_recv()
        for cp in sends:
            cp.wait_send()
        loc.wait()

    return pl.pallas_call(
        body, name="gather_chips", in_specs=[ANY], out_specs=ANY, out_shape=jax.ShapeDtypeStruct((N_CHIPS,) + v.shape, v.dtype),
        scratch_shapes=[_dma_sems(3), _dma_sems(3), pltpu.SemaphoreType.DMA],
    )(v)


BIG = (("w_in", 2), ("w_out", 1), ("ffn_wg", 2), ("ffn_wu", 2), ("ffn_wd", 1), ("ple_wg", 1), ("ple_wp", 2))
SMALL = (("dn_conv_w", 2), ("lru_conv_w", 2), ("lru_ba", 2), ("lru_bx", 2), ("lru_lambda", 2), ("ffn_conv_w", 2))
REPL = ("norm1_g", "dn_a_log", "dn_dt_bias", "dn_norm_g", "lru_conv_b", "lru_wa", "lru_wx", "lru_norm_g", "norm2_g",
        "ffn_conv_b", "ple_norm_g", "ple_bg", "final_g")
WEIGHTS = ("norm1_g", "w_in", "dn_conv_w", "dn_a_log", "dn_dt_bias", "dn_norm_g", "lru_conv_w", "lru_conv_b", "lru_wa", "lru_ba",
           "lru_wx", "lru_bx", "lru_lambda", "lru_norm_g", "w_out", "norm2_g", "ffn_wg", "ffn_wu", "ffn_conv_w", "ffn_conv_b",
           "ffn_wd", "ple_norm_g", "ple_wg", "ple_bg", "ple_wp", "final_g")
GATE_PARAMS = ("dn_a_log", "dn_dt_bias")
ROW_ALIGN = 16


def _rows_for(n_elems):
    rows = -(-n_elems // LANES)
    return -(-rows // ROW_ALIGN) * ROW_ALIGN


def _join_chips(g, own, me, axis):
    return jnp.concatenate([jnp.where(me == k, own, g[k]) for k in range(N_CHIPS)], axis=axis)


def _chip_blocks(layers, axis):
    nl = len(layers)
    size = layers[0].shape[axis - 1] // N_CHIPS
    cut = (lambda g, k: g[:, k * size:(k + 1) * size]) if axis == 2 else (lambda g, k: g[k * size:(k + 1) * size])
    return jnp.stack([jnp.stack([cut(layers[l], k) for l in range(nl)]) for k in range(N_CHIPS)])


def _pack_small(w):
    nl = w[SMALL[0][0]].shape[0]
    flat = jnp.concatenate([w[n].reshape(nl, -1) for n, _ in SMALL], axis=1)
    rows = _rows_for(flat.shape[1])
    return jnp.pad(flat, ((0, 0), (0, rows * LANES - flat.shape[1]))).reshape(nl, rows, LANES)


def _unpack_small_gathered(g, own, me, w):
    nl = g.shape[1]
    flat = jnp.stack([jnp.where(me == k, own, g[k]) for k in range(N_CHIPS)]).reshape(N_CHIPS, nl, -1)
    out, off = {}, 0
    for n, _ in SMALL:
        _, a, b = w[n].shape
        piece = flat[:, :, off:off + a * b].reshape(N_CHIPS, nl, a, b)
        off += a * b
        out[n] = jnp.transpose(piece, (1, 2, 0, 3)).reshape(nl, a, N_CHIPS * b)
    return out


def _pack_small_grads(grads, w):
    nl = w["w_in"].shape[0]
    cols = []
    for n, _ in SMALL:
        _, a, b = w[n].shape
        gfull = jnp.stack(grads[n])
        cols.append(jnp.transpose(gfull.reshape(nl, a, N_CHIPS, b), (2, 0, 1, 3)).reshape(N_CHIPS, nl, a * b))
    small = jnp.concatenate(cols, axis=2)
    rs = _rows_for(small.shape[2])
    small = jnp.pad(small, ((0, 0), (0, 0), (0, rs * LANES - small.shape[2])))
    src = lambda n: n + "_lanes" if n in GATE_PARAMS else n
    rep = jnp.concatenate([(grads[n] if n == "final_g" else jnp.stack(grads[src(n)])).reshape(-1) for n in REPL])
    rr = _rows_for(-(-rep.shape[0] // (N_CHIPS * nl)))
    rep = jnp.pad(rep, (0, N_CHIPS * nl * rr * LANES - rep.shape[0])).reshape(N_CHIPS, nl, rr * LANES)
    return jnp.concatenate([small, rep], axis=2).reshape(N_CHIPS, nl, rs + rr, LANES), (rs, rr)


def _unpack_small_reduced(g, rep_all, rows, w):
    rs, _ = rows
    nl = g.shape[0]
    out = {}
    flat = g[:, :rs].reshape(nl, -1)
    off = 0
    for n, _ in SMALL:
        _, a, b = w[n].shape
        out[n] = flat[:, off:off + a * b].reshape(nl, a, b)
        off += a * b
    flat = rep_all.reshape(-1)
    off = 0
    for n in REPL:
        if n in GATE_PARAMS:
            size = nl * LANES
            out[n] = flat[off:off + size].reshape(nl, LANES)[:, N_GATE // 2:N_GATE].reshape(w[n].shape)
        else:
            size = math.prod(w[n].shape)
            out[n] = flat[off:off + size].reshape(w[n].shape)
        off += size
    return out


def _as2d(a):
    if a.ndim == 1:
        return a.reshape(1, -1)
    return a.reshape(-1, a.shape[-1])


def kernel(x, p, norm1_g, w_in, dn_conv_w, dn_a_log, dn_dt_bias, dn_norm_g, lru_conv_w, lru_conv_b, lru_wa, lru_ba, lru_wx, lru_bx, lru_lambda, lru_norm_g, w_out, norm2_g, ffn_wg, ffn_wu, ffn_conv_w, ffn_conv_b, ffn_wd, ple_norm_g, ple_wg, ple_bg, ple_wp, final_g, loss_target, m_norm1_g, m_w_in, m_dn_conv_w, m_dn_a_log, m_dn_dt_bias, m_dn_norm_g, m_lru_conv_w, m_lru_conv_b, m_lru_wa, m_lru_ba, m_lru_wx, m_lru_bx, m_lru_lambda, m_lru_norm_g, m_w_out, m_norm2_g, m_ffn_wg, m_ffn_wu, m_ffn_conv_w, m_ffn_conv_b, m_ffn_wd, m_ple_norm_g, m_ple_wg, m_ple_bg, m_ple_wp, m_final_g, v_norm1_g, v_w_in, v_dn_conv_w, v_dn_a_log, v_dn_dt_bias, v_dn_norm_g, v_lru_conv_w, v_lru_conv_b, v_lru_wa, v_lru_ba, v_lru_wx, v_lru_bx, v_lru_lambda, v_lru_norm_g, v_w_out, v_norm2_g, v_ffn_wg, v_ffn_wu, v_ffn_conv_w, v_ffn_conv_b, v_ffn_wd, v_ple_norm_g, v_ple_wg, v_ple_bg, v_ple_wp, v_final_g):
    w = dict(norm1_g=norm1_g, w_in=w_in, dn_conv_w=dn_conv_w, dn_a_log=dn_a_log, dn_dt_bias=dn_dt_bias, dn_norm_g=dn_norm_g,
             lru_conv_w=lru_conv_w, lru_conv_b=lru_conv_b, lru_wa=lru_wa, lru_ba=lru_ba, lru_wx=lru_wx, lru_bx=lru_bx,
             lru_lambda=lru_lambda, lru_norm_g=lru_norm_g, w_out=w_out, norm2_g=norm2_g, ffn_wg=ffn_wg, ffn_wu=ffn_wu,
             ffn_conv_w=ffn_conv_w, ffn_conv_b=ffn_conv_b, ffn_wd=ffn_wd, ple_norm_g=ple_norm_g, ple_wg=ple_wg, ple_bg=ple_bg,
             ple_wp=ple_wp, final_g=final_g)
    m = dict(norm1_g=m_norm1_g, w_in=m_w_in, dn_conv_w=m_dn_conv_w, dn_a_log=m_dn_a_log, dn_dt_bias=m_dn_dt_bias,
             dn_norm_g=m_dn_norm_g, lru_conv_w=m_lru_conv_w, lru_conv_b=m_lru_conv_b, lru_wa=m_lru_wa, lru_ba=m_lru_ba,
             lru_wx=m_lru_wx, lru_bx=m_lru_bx, lru_lambda=m_lru_lambda, lru_norm_g=m_lru_norm_g, w_out=m_w_out, norm2_g=m_norm2_g,
             ffn_wg=m_ffn_wg, ffn_wu=m_ffn_wu, ffn_conv_w=m_ffn_conv_w, ffn_conv_b=m_ffn_conv_b, ffn_wd=m_ffn_wd,
             ple_norm_g=m_ple_norm_g, ple_wg=m_ple_wg, ple_bg=m_ple_bg, ple_wp=m_ple_wp, final_g=m_final_g)
    v = dict(norm1_g=v_norm1_g, w_in=v_w_in, dn_conv_w=v_dn_conv_w, dn_a_log=v_dn_a_log, dn_dt_bias=v_dn_dt_bias,
             dn_norm_g=v_dn_norm_g, lru_conv_w=v_lru_conv_w, lru_conv_b=v_lru_conv_b, lru_wa=v_lru_wa, lru_ba=v_lru_ba,
             lru_wx=v_lru_wx, lru_bx=v_lru_bx, lru_lambda=v_lru_lambda, lru_norm_g=v_lru_norm_g, w_out=v_w_out, norm2_g=v_norm2_g,
             ffn_wg=v_ffn_wg, ffn_wu=v_ffn_wu, ffn_conv_w=v_ffn_conv_w, ffn_conv_b=v_ffn_conv_b, ffn_wd=v_ffn_wd,
             ple_norm_g=v_ple_norm_g, ple_wg=v_ple_wg, ple_bg=v_ple_bg, ple_wp=v_ple_wp, final_g=v_final_g)

    me = 2 * lax.axis_index("x") + lax.axis_index("y")
    own = [w[n].astype(BF16) for n, _ in BIG] + [_pack_small(w)]
    gathered = _gather_weights(own)
    full = {n: w[n] for n in REPL}
    for (n, axis), g, o in zip(BIG, gathered, own):
        full[n] = _join_chips(g, o, me, axis)
    full.update(_unpack_small_gathered(gathered[-1], own[-1], me, w))

    loss_local, grad_x, grads = _local_step(x, p, loss_target, full)
    loss = lax.psum(loss_local, ("x", "y", "c"))

    small_pack, rows = _pack_small_grads(grads, w)
    blocks = [_chip_blocks(grads[n], axis) for n, axis in BIG]
    blocks.append(small_pack)
    names = [n for n, _ in BIG] + ["small"]
    c_idx = lax.axis_index("c")
    c_arr = c_idx.astype(jnp.int32).reshape(1)
    me_arr = me.astype(jnp.int32).reshape(1)
    from_sibling = _swap_halves(blocks)
    halves = [_add_halves(b4, l1, c_arr, "add_halves_" + n) for n, b4, l1 in zip(names, blocks, from_sibling)]
    arrived = _scatter_chips(halves)
    mine = [_sum_chips(l2, q, me_arr, "sum_chips_" + n) for n, l2, q in zip(names, arrived, halves)]
    theirs = _swap_reduced(mine)

    g, deltas, new_m, new_v = {}, {}, {}, {}
    for i, (n, _) in enumerate(BIG):
        g[n], deltas[n], new_m[n], new_v[n] = _adamw_halves(w[n], mine[i], theirs[i], m[n], v[n], c_arr, "adamw_" + n)
    small = jnp.where(c_idx == 0, jnp.concatenate([mine[-1], theirs[-1]]), jnp.concatenate([theirs[-1], mine[-1]]))
    rep_all = _gather_chips(small[:, rows[0]:])
    gs = _unpack_small_reduced(small, rep_all, rows, w)
    for n in WEIGHTS:
        if n in g:
            continue
        g[n] = gs[n].reshape(w[n].shape)
        d2, m2, v2 = _adamw(_as2d(w[n]), _as2d(g[n]), _as2d(m[n]), _as2d(v[n]), "adamw_" + n)
        deltas[n], new_m[n], new_v[n] = d2.reshape(w[n].shape), m2.reshape(w[n].shape), v2.reshape(w[n].shape)
    return (loss, grad_x, *[g[n] for n in WEIGHTS], *[deltas[n] for n in WEIGHTS], *[new_m[n] for n in WEIGHTS],
            *[new_v[n] for n in WEIGHTS])
```

```python
import functools
import math

import jax
import jax.numpy as jnp
from jax import lax
from jax.experimental import pallas as pl
from jax.experimental.pallas import tpu as pltpu

F32 = jnp.float32
BF16 = jnp.bfloat16

D_MODEL = 1024
DN_HEADS = 4
DN_HEAD_DIM = 128
DN_WIDTH = 512
LRU_WIDTH = 512
LRU_C = 8.0
CHUNK = 64
EPS = 1e-6
P_LX, P_LG, P_Q, P_K, P_V, P_Z, P_GATE, P_COLS = 0, 512, 1024, 1536, 2048, 2560, 3072, 3200
N_GATE = 16
LANES = 128
VMEM_LIMIT = 56 * 1024 * 1024
MM_VMEM_BYTES = 40 * 1024 * 1024

ADAM_LR, ADAM_B1, ADAM_B2, ADAM_EPS, ADAM_WD, ADAM_STEP = 0.001, 0.9, 0.999, 1e-08, 0.01, 10


def _cparams(sem):
    return pltpu.CompilerParams(dimension_semantics=sem, vmem_limit_bytes=VMEM_LIMIT)


def _row_tile(m, want=512):
    for t in range(min(want, m) // 16 * 16, 0, -16):
        if m % t == 0:
            return t
    return m


def _lane_divisors(n):
    out = [d for d in range(n, 0, -LANES) if d % LANES == 0 and n % d == 0] if n % LANES == 0 else []
    return out or [n]


def _mm_tiles(m, n, a_row_bytes, k_total, with_res):
    best = None
    for tm in (1024, 512, 256, 128):
        if m % tm:
            continue
        for tn in _lane_divisors(n):
            need = 2 * (tm * a_row_bytes + k_total * tn * 2 + tm * tn * 4 * (2 if with_res else 1))
            if need <= MM_VMEM_BYTES and (best is None or tm * tn > best[0] * best[1]):
                best = (tm, tn)
    return best if best is not None else (_row_tile(m, 128), _lane_divisors(n)[-1])


def _mm_tn_tiles(m, k, n):
    for tn in _lane_divisors(n):
        for tm in (1024, 512, 256, 128):
            if m % tm == 0 and 2 * (k * tn * 4 + tm * k * 2 + tm * tn * 2) <= MM_VMEM_BYTES:
                return tm, tn
    return _row_tile(m, 128), _lane_divisors(n)[-1]


def _bdot(a, b):
    return jnp.dot(a.astype(BF16), b.astype(BF16), preferred_element_type=F32)


def _bdot_nt(a, b):
    return lax.dot_general(a.astype(BF16), b.astype(BF16), (((1,), (1,)), ((), ())), preferred_element_type=F32)


def _bdot_tn(a, b):
    return lax.dot_general(a.astype(BF16), b.astype(BF16), (((0,), (0,)), ((), ())), preferred_element_type=F32)


def _rms(x, g):
    return x * lax.rsqrt(jnp.mean(x * x, axis=-1, keepdims=True) + EPS) * g


def _gelu(x):
    return 0.5 * x * (1.0 + jnp.tanh(0.7978845608028654 * (x + 0.044715 * x * x * x)))


def _sigmoid(x):
    return 1.0 / (1.0 + jnp.exp(-x))


def _silu(x):
    return x * _sigmoid(x)


def _softplus(x):
    return jnp.maximum(x, 0.0) + jnp.log(1.0 + jnp.exp(-jnp.abs(x)))


def _rmsnorm_fwd(x, g, name):
    t, d = x.shape
    tm = _row_tile(t)

    def body(x_ref, g_ref, o_ref):
        o_ref[...] = _rms(x_ref[...], g_ref[...]).astype(o_ref.dtype)

    return pl.pallas_call(
        body, name=name, grid=(t // tm,),
        in_specs=[pl.BlockSpec((tm, d), lambda i: (i, 0)), pl.BlockSpec((1, d), lambda i: (0, 0))],
        out_specs=pl.BlockSpec((tm, d), lambda i: (i, 0)),
        out_shape=jax.ShapeDtypeStruct((t, d), BF16),
        compiler_params=_cparams(("parallel",)),
    )(x, g)


def _rmsnorm_bwd(x, g, dh, dres, name):
    t, d = x.shape
    tm = _row_tile(t)

    def body(x_ref, g_ref, dh_ref, dres_ref, dx_ref, dxb_ref, dg_ref):
        _, vjp = jax.vjp(_rms, x_ref[...], g_ref[...])
        dx, dg = vjp(dh_ref[...])
        dx = dx + dres_ref[...]
        dx_ref[...] = dx
        dxb_ref[...] = dx.astype(BF16)

        @pl.when(pl.program_id(0) == 0)
        def _():
            dg_ref[...] = jnp.zeros_like(dg_ref)

        dg_ref[...] += dg

    return pl.pallas_call(
        body, name=name, grid=(t // tm,),
        in_specs=[pl.BlockSpec((tm, d), lambda i: (i, 0)), pl.BlockSpec((1, d), lambda i: (0, 0)),
                  pl.BlockSpec((tm, d), lambda i: (i, 0)), pl.BlockSpec((tm, d), lambda i: (i, 0))],
        out_specs=[pl.BlockSpec((tm, d), lambda i: (i, 0)), pl.BlockSpec((tm, d), lambda i: (i, 0)),
                   pl.BlockSpec((1, d), lambda i: (0, 0))],
        out_shape=[jax.ShapeDtypeStruct((t, d), F32), jax.ShapeDtypeStruct((t, d), BF16), jax.ShapeDtypeStruct((1, d), F32)],
        compiler_params=_cparams(("arbitrary",)),
    )(x, g, dh, dres)


def _matmul(terms, name, res=None, nt=False):
    norm = lambda op, axis: op if isinstance(op, tuple) else (op, op.shape[axis], 0)
    a_ops = [norm(a, 1) for a, _ in terms]
    b_ops = [norm(b, 1 if nt else 0) for _, b in terms]
    m = a_ops[0][0].shape[0]
    n = b_ops[0][0].shape[0 if nt else 1]
    a_row_bytes = sum(kw * a.dtype.itemsize for a, kw, _ in a_ops)
    tm, tn = _mm_tiles(m, n, a_row_bytes, sum(kw for _, kw, _ in b_ops), res is not None)
    na = len(terms)
    dot = _bdot_nt if nt else _bdot

    def body(*refs):
        a_refs, b_refs = refs[:na], refs[na:2 * na]
        acc = dot(a_refs[0][...], b_refs[0][...])
        for a_ref, b_ref in zip(a_refs[1:], b_refs[1:]):
            acc = acc + dot(a_ref[...], b_ref[...])
        if res is not None:
            acc = acc + refs[2 * na][...]
        refs[-1][...] = acc

    in_specs = [pl.BlockSpec((tm, kw), functools.partial(lambda i, j, kb: (i, kb), kb=kb)) for _, kw, kb in a_ops]
    if nt:
        in_specs += [pl.BlockSpec((tn, kw), functools.partial(lambda i, j, kb: (j, kb), kb=kb)) for _, kw, kb in b_ops]
    else:
        in_specs += [pl.BlockSpec((kw, tn), functools.partial(lambda i, j, kb: (kb, j), kb=kb)) for _, kw, kb in b_ops]
    args = [a for a, _, _ in a_ops] + [b for b, _, _ in b_ops]
    if res is not None:
        in_specs.append(pl.BlockSpec((tm, tn), lambda i, j: (i, j)))
        args.append(res)
    return pl.pallas_call(
        body, name=name, grid=(m // tm, n // tn), in_specs=in_specs,
        out_specs=pl.BlockSpec((tm, tn), lambda i, j: (i, j)),
        out_shape=jax.ShapeDtypeStruct((m, n), F32),
        compiler_params=_cparams(("parallel", "parallel")),
    )(*args)


def _matmul_tn(a, b, name):
    m, k = a.shape
    n = b.shape[1]
    tm, tn = _mm_tn_tiles(m, k, n)

    def body(a_ref, b_ref, o_ref):
        @pl.when(pl.program_id(1) == 0)
        def _():
            o_ref[...] = jnp.zeros_like(o_ref)

        o_ref[...] += _bdot_tn(a_ref[...], b_ref[...])

    return pl.pallas_call(
        body, name=name, grid=(n // tn, m // tm),
        in_specs=[pl.BlockSpec((tm, k), lambda j, i: (i, 0)), pl.BlockSpec((tm, tn), lambda j, i: (i, j))],
        out_specs=pl.BlockSpec((k, tn), lambda j, i: (0, j)),
        out_shape=jax.ShapeDtypeStruct((k, n), F32),
        compiler_params=_cparams(("parallel", "arbitrary")),
    )(a, b)


def _shift_down(x, k):
    row = lax.broadcasted_iota(jnp.int32, x.shape, 0)
    return jnp.where(row >= k, pltpu.roll(x, k, 0), 0.0)


def _shift_up(x, k):
    s = x.shape[0]
    row = lax.broadcasted_iota(jnp.int32, x.shape, 0)
    return jnp.where(row < s - k, pltpu.roll(x, s - k, 0), 0.0)


def _conv_taps(x, ntaps, left):
    out = []
    for j in range(ntaps):
        off = j - left
        out.append(_shift_down(x, -off) if off < 0 else (_shift_up(x, off) if off > 0 else x))
    return out


def _conv_fwd(x, w, left):
    taps = _conv_taps(x, w.shape[0], left)
    acc = taps[0] * w[0:1, :]
    for j in range(1, w.shape[0]):
        acc = acc + taps[j] * w[j:j + 1, :]
    return acc


def _conv_bwd(x, w, left, dout):
    ntaps = w.shape[0]
    dx = None
    for j in range(ntaps):
        off = j - left
        sh = _shift_up(dout, -off) if off < 0 else (_shift_down(dout, off) if off > 0 else dout)
        term = sh * w[j:j + 1, :]
        dx = term if dx is None else dx + term
    taps = _conv_taps(x, ntaps, left)
    dw = jnp.concatenate([jnp.sum(dout * tp, axis=0, keepdims=True) for tp in taps], axis=0)
    return dx, dw


def _scan(a, b, reverse):
    s = a.shape[0]
    d = 1
    sh = _shift_up if reverse else _shift_down
    while d < s:
        b = a * sh(b, d) + b
        a = a * sh(a, d)
        d *= 2
    return b


def _dn_gates_fn(pre, alog, dtb):
    lane = lax.broadcasted_iota(jnp.int32, pre.shape, 1)
    beta = _sigmoid(pre)
    g = -jnp.exp(alog) * _softplus(pre + dtb)
    return jnp.where(lane < N_GATE // 2, beta, jnp.where(lane < N_GATE, g, 0.0))


def _dn_gates_fwd(proj, alog, dtb, name):
    t = proj.shape[0]
    tm = _row_tile(t)
    cb = P_GATE // LANES

    def body(p_ref, a_ref, d_ref, o_ref):
        o_ref[...] = _dn_gates_fn(p_ref[...], a_ref[...], d_ref[...])

    return pl.pallas_call(
        body, name=name, grid=(t // tm,),
        in_specs=[pl.BlockSpec((tm, LANES), lambda i: (i, cb)), pl.BlockSpec((1, LANES), lambda i: (0, 0)),
                  pl.BlockSpec((1, LANES), lambda i: (0, 0))],
        out_specs=pl.BlockSpec((tm, LANES), lambda i: (i, 0)),
        out_shape=jax.ShapeDtypeStruct((t, LANES), F32),
        compiler_params=_cparams(("parallel",)),
    )(proj, alog, dtb)


def _dn_gates_bwd(proj, alog, dtb, dgb_f, dgb_b, name):
    t = proj.shape[0]
    tm = _row_tile(t)
    cb = P_GATE // LANES

    def body(p_ref, a_ref, d_ref, g1_ref, g2_ref, dp_ref, da_ref, dd_ref):
        _, vjp = jax.vjp(_dn_gates_fn, p_ref[...], a_ref[...], d_ref[...])
        dp, da, dd = vjp(g1_ref[...] + g2_ref[...])
        dp_ref[...] = dp.astype(dp_ref.dtype)

        @pl.when(pl.program_id(0) == 0)
        def _():
            da_ref[...] = jnp.zeros_like(da_ref)
            dd_ref[...] = jnp.zeros_like(dd_ref)

        da_ref[...] += da
        dd_ref[...] += dd

    row = pl.BlockSpec((tm, LANES), lambda i: (i, 0))
    vec = pl.BlockSpec((1, LANES), lambda i: (0, 0))
    return pl.pallas_call(
        body, name=name, grid=(t // tm,),
        in_specs=[pl.BlockSpec((tm, LANES), lambda i: (i, cb)), vec, vec, row, row],
        out_specs=[row, vec, vec],
        out_shape=[jax.ShapeDtypeStruct((t, LANES), BF16), jax.ShapeDtypeStruct((1, LANES), F32),
                   jax.ShapeDtypeStruct((1, LANES), F32)],
        compiler_params=_cparams(("arbitrary",)),
    )(proj, alog, dtb, dgb_f, dgb_b)


def _dn_prep_fn(x, w, is_qk):
    act = _silu(_conv_fwd(x, w, 2))
    nrm = act * lax.rsqrt(jnp.sum(act * act, axis=-1, keepdims=True) + EPS)
    return jnp.where(is_qk, nrm, act)


def _dn_prep_fwd(proj3, conv_w, name):
    bsz, s, _ = proj3.shape
    nblk = 3 * DN_WIDTH // LANES
    cb = P_Q // LANES

    def body(x_ref, w_ref, o_ref):
        o_ref[0] = _dn_prep_fn(x_ref[0], w_ref[...], pl.program_id(1) < 2 * DN_HEADS)

    return pl.pallas_call(
        body, name=name, grid=(bsz, nblk),
        in_specs=[pl.BlockSpec((1, s, LANES), lambda b, j: (b, 0, cb + j)), pl.BlockSpec((4, LANES), lambda b, j: (0, j))],
        out_specs=pl.BlockSpec((1, s, LANES), lambda b, j: (b, 0, j)),
        out_shape=jax.ShapeDtypeStruct((bsz, s, 3 * DN_WIDTH), F32),
        compiler_params=_cparams(("parallel", "parallel")),
    )(proj3, conv_w)


def _dn_prep_bwd(proj3, conv_w, dqkv_f, dqkv_b, name):
    bsz, s, _ = proj3.shape
    nblk = 3 * DN_WIDTH // LANES
    cb = P_Q // LANES

    def body(x_ref, w_ref, d1_ref, d2_ref, dx_ref, dw_ref):
        x, w, d = x_ref[0], w_ref[...], d1_ref[0] + d2_ref[0]
        is_qk = pl.program_id(0) < 2 * DN_HEADS
        pre = _conv_fwd(x, w, 2)

        def post(pre):
            act = _silu(pre)
            nrm = act * lax.rsqrt(jnp.sum(act * act, axis=-1, keepdims=True) + EPS)
            return jnp.where(is_qk, nrm, act)

        _, vjp = jax.vjp(post, pre)
        (dpre,) = vjp(d)
        dx, dw = _conv_bwd(x, w, 2, dpre)
        dx_ref[0] = dx.astype(dx_ref.dtype)

        @pl.when(pl.program_id(1) == 0)
        def _():
            dw_ref[...] = jnp.zeros_like(dw_ref)

        dw_ref[...] += dw

    col = pl.BlockSpec((1, s, LANES), lambda j, b: (b, 0, j))
    return pl.pallas_call(
        body, name=name, grid=(nblk, bsz),
        in_specs=[pl.BlockSpec((1, s, LANES), lambda j, b: (b, 0, cb + j)), pl.BlockSpec((4, LANES), lambda j, b: (0, j)), col, col],
        out_specs=[col, pl.BlockSpec((4, LANES), lambda j, b: (0, j))],
        out_shape=[jax.ShapeDtypeStruct((bsz, s, 3 * DN_WIDTH), BF16), jax.ShapeDtypeStruct((4, 3 * DN_WIDTH), F32)],
        compiler_params=_cparams(("parallel", "arbitrary")),
    )(proj3, conv_w, dqkv_f, dqkv_b)


def _parts(x, n):
    out = []
    for _ in range(n):
        bits = lax.bitcast_convert_type(x, jnp.uint32) & jnp.uint32(0xFFFF0000)
        t = lax.bitcast_convert_type(bits, F32)
        out.append(t.astype(BF16))
        x = x - t
    return out


def _dg(x, y, cx, cy):
    return lax.dot_general(x, y, (((cx + 1,), (cy + 1,)), ((0,), (0,))), preferred_element_type=F32)


def _bmm(a, b):
    return _dg(a.astype(BF16), b.astype(BF16), 1, 0)


def _bmm_nt(a, b):
    return _dg(a.astype(BF16), b.astype(BF16), 1, 1)


def _bmm_tn(a, b):
    return _dg(a.astype(BF16), b.astype(BF16), 0, 0)


def _dot3_raw(a, b, ca, cb):
    a_hi, a_lo = _parts(a, 2)
    b_hi, b_lo = _parts(b, 2)
    return _dg(a_hi, b_hi, ca, cb) + (_dg(a_hi, b_lo, ca, cb) + _dg(a_lo, b_hi, ca, cb))


@jax.custom_vjp
def _sum_left(m, x):
    return sum(_dg(m, pt, 1, 0) for pt in _parts(x, 3))


def _sum_left_fwd(m, x):
    return _sum_left(m, x), m


def _sum_left_bwd(m, ct):
    return jnp.zeros_like(m), sum(_dg(m, pt, 0, 0) for pt in _parts(ct, 2))


_sum_left.defvjp(_sum_left_fwd, _sum_left_bwd)


@jax.custom_vjp
def _sum_right(x, m):
    return sum(_dg(pt, m, 0, 0) for pt in _parts(x, 3))


def _sum_right_fwd(x, m):
    return _sum_right(x, m), m


def _sum_right_bwd(m, ct):
    return sum(_dg(m, pt, 1, 1) for pt in _parts(ct, 2)), jnp.zeros_like(m)


_sum_right.defvjp(_sum_right_fwd, _sum_right_bwd)


def _unit_tri_inverse_raw(a):
    nu, c, _ = a.shape
    row = lax.broadcasted_iota(jnp.int32, (nu, c, c), 1)
    col = lax.broadcasted_iota(jnp.int32, (nu, c, c), 2)
    same = lambda n: (row // n) == (col // n)
    d = jnp.where(same(16), a, 0.0)
    tinv = jnp.where(row == col, 1.0, 0.0) - d
    pw = _dot3_raw(d, d, 1, 0)
    tinv = tinv + _dot3_raw(tinv, pw, 1, 0)
    pw = _dot3_raw(pw, pw, 1, 0)
    tinv = tinv + _dot3_raw(tinv, pw, 1, 0)
    pw = _dot3_raw(pw, pw, 1, 0)
    tinv = tinv + _dot3_raw(tinv, pw, 1, 0)
    n = 16
    while n < c:
        e = jnp.where(same(2 * n) & jnp.logical_not(same(n)), a, 0.0)
        tinv = tinv - _bmm(tinv, _bmm(e, tinv))
        n *= 2
    return tinv


@jax.custom_vjp
def _unit_tri_inverse(a):
    return _unit_tri_inverse_raw(a)


def _unit_tri_inverse_fwd(a):
    tinv = _unit_tri_inverse_raw(a)
    return tinv, tinv


def _unit_tri_inverse_bwd(tinv, ct):
    return (-_bmm_nt(_bmm_tn(tinv, ct), tinv),)


_unit_tri_inverse.defvjp(_unit_tri_inverse_fwd, _unit_tri_inverse_bwd)


def _dn_intra(q, k, v, g, beta, rev):
    nu, c, _ = q.shape
    row = lax.broadcasted_iota(jnp.int32, (nu, c, c), 1)
    col = lax.broadcasted_iota(jnp.int32, (nu, c, c), 2)
    incl = (row <= col) if rev else (row >= col)
    strict = (row < col) if rev else (row > col)
    ones_incl = jnp.where(incl, 1.0, 0.0).astype(BF16)
    ones_tr = jnp.where((row >= col) if rev else (row <= col), 1.0, 0.0).astype(BF16)
    gbc = jnp.broadcast_to(g, (nu, c, c))
    gc = _sum_left(ones_incl, gbc)
    gr = _sum_right(gbc, ones_tr)
    gcum = gc[:, :, 0:1]
    decay = jnp.where(incl, jnp.exp(jnp.where(incl, gc - gr, 0.0)), 0.0)
    qs = q * (DN_HEAD_DIM ** -0.5)
    kb = k * beta
    a = jnp.where(strict, _bmm_nt(kb, k) * decay, 0.0)
    tinv = _unit_tri_inverse(a)
    egc = jnp.exp(gcum)
    u = _bmm(tinv, v * beta)
    w = _bmm(tinv, kb * egc)
    attn = _bmm_nt(qs, k) * decay
    glast = gcum[:, 0:1, :] if rev else gcum[:, c - 1:c, :]
    q_dec = qs * egc
    k_dec = k * jnp.exp(glast - gcum)
    cdec = jnp.broadcast_to(jnp.exp(glast), (nu, 1, LANES))
    return u, w, q_dec, k_dec, attn, cdec


def _dn_rec(u, w, q_dec, k_dec, attn, cdec, state):
    v_new = u - _bmm(w, state)
    o = _bmm(q_dec, state) + _bmm(attn, v_new)
    return o, state * cdec + _bmm_tn(k_dec, v_new)


DN_INTRA_TOKENS = 512
DN_REC_TOKENS = 512
DN_REC_BWD_TOKENS = 256
DN_REC_EXAMPLES = 4


def _dn_gate_lanes(rev, h):
    lb = (DN_HEADS if rev else 0) + h
    return lb, N_GATE // 2 + lb


def _dn_intra_shapes(bsz, s):
    n = s // CHUNK
    return [jax.ShapeDtypeStruct((bsz, s, DN_WIDTH), F32), jax.ShapeDtypeStruct((bsz, s, DN_WIDTH), BF16),
            jax.ShapeDtypeStruct((bsz, s, DN_WIDTH), BF16), jax.ShapeDtypeStruct((bsz, s, DN_WIDTH), BF16),
            jax.ShapeDtypeStruct((bsz, n, DN_HEADS, CHUNK, CHUNK), BF16), jax.ShapeDtypeStruct((bsz, n, DN_HEADS, 1, LANES), F32)]


def _dn_intra_specs(tb, ix):
    nc = tb // CHUNK
    ix5 = lambda b, j: ix(b, j) + (0, 0)
    row = pl.BlockSpec((1, tb, DN_WIDTH), ix)
    return [row, row, row, row, pl.BlockSpec((1, nc, DN_HEADS, CHUNK, CHUNK), ix5), pl.BlockSpec((1, nc, DN_HEADS, 1, LANES), ix5)]


def _dn_units(nc):
    return [(ci, h) for ci in range(nc) for h in range(DN_HEADS)]


def _dn_load_units(qkv_ref, gb_ref, nc, rev):
    qs, ks, vs, gs, bs = [], [], [], [], []
    for ci, h in _dn_units(nc):
        rows = slice(ci * CHUNK, (ci + 1) * CHUNK)
        lb, lg = _dn_gate_lanes(rev, h)
        qs.append(qkv_ref[0, rows, h * LANES:(h + 1) * LANES])
        ks.append(qkv_ref[0, rows, DN_WIDTH + h * LANES:DN_WIDTH + (h + 1) * LANES])
        vs.append(qkv_ref[0, rows, 2 * DN_WIDTH + h * LANES:2 * DN_WIDTH + (h + 1) * LANES])
        gs.append(gb_ref[0, rows, lg:lg + 1])
        bs.append(gb_ref[0, rows, lb:lb + 1])
    return jnp.stack(qs), jnp.stack(ks), jnp.stack(vs), jnp.stack(gs), jnp.stack(bs)


def _dn_intra_fwd(qkv, gb, rev, name):
    bsz, s, _ = qkv.shape
    tb = min(DN_INTRA_TOKENS, s)
    nc = tb // CHUNK

    def body(qkv_ref, gb_ref, u_ref, w_ref, qd_ref, kd_ref, at_ref, cd_ref):
        q, k, v, g, beta = _dn_load_units(qkv_ref, gb_ref, nc, rev)
        u, w, qd, kd, at, cd = _dn_intra(q, k, v, g, beta, rev)
        for i, (ci, h) in enumerate(_dn_units(nc)):
            rows = slice(ci * CHUNK, (ci + 1) * CHUNK)
            cols = slice(h * LANES, (h + 1) * LANES)
            u_ref[0, rows, cols] = u[i]
            w_ref[0, rows, cols] = w[i].astype(BF16)
            qd_ref[0, rows, cols] = qd[i].astype(BF16)
            kd_ref[0, rows, cols] = kd[i].astype(BF16)
            at_ref[0, ci, h] = at[i].astype(BF16)
            cd_ref[0, ci, h] = cd[i]

    ix = lambda b, j: (b, j, 0)
    return pl.pallas_call(
        body, name=name, grid=(bsz, s // tb),
        in_specs=[pl.BlockSpec((1, tb, 3 * DN_WIDTH), ix), pl.BlockSpec((1, tb, LANES), ix)],
        out_specs=_dn_intra_specs(tb, ix), out_shape=_dn_intra_shapes(bsz, s),
        compiler_params=_cparams(("parallel", "parallel")),
    )(qkv, gb)


def _dn_intra_bwd(qkv, gb, cts, rev, name):
    bsz, s, _ = qkv.shape
    tb = min(DN_INTRA_TOKENS, s)
    nc = tb // CHUNK

    def body(qkv_ref, gb_ref, du_ref, dw_ref, dqd_ref, dkd_ref, dat_ref, dcd_ref, dqkv_ref, dgb_ref):
        units = _dn_units(nc)
        q, k, v, g, beta = _dn_load_units(qkv_ref, gb_ref, nc, rev)
        _, vjp = jax.vjp(functools.partial(_dn_intra, rev=rev), q, k, v, g, beta)
        tok = lambda ref: jnp.stack([ref[0, ci * CHUNK:(ci + 1) * CHUNK, h * LANES:(h + 1) * LANES] for ci, h in units])
        per = lambda ref: jnp.stack([ref[0, ci, h] for ci, h in units])
        dq, dk, dv, dg, dbeta = vjp((tok(du_ref), tok(dw_ref), tok(dqd_ref), tok(dkd_ref), per(dat_ref), per(dcd_ref)))
        lane = lax.broadcasted_iota(jnp.int32, (CHUNK, LANES), 1)
        for ci in range(nc):
            rows = slice(ci * CHUNK, (ci + 1) * CHUNK)
            dgates = jnp.zeros((CHUNK, LANES), F32)
            for h in range(DN_HEADS):
                i = units.index((ci, h))
                lb, lg = _dn_gate_lanes(rev, h)
                dqkv_ref[0, rows, h * LANES:(h + 1) * LANES] = dq[i]
                dqkv_ref[0, rows, DN_WIDTH + h * LANES:DN_WIDTH + (h + 1) * LANES] = dk[i]
                dqkv_ref[0, rows, 2 * DN_WIDTH + h * LANES:2 * DN_WIDTH + (h + 1) * LANES] = dv[i]
                dgates = dgates + jnp.where(lane == lb, dbeta[i], 0.0) + jnp.where(lane == lg, dg[i], 0.0)
            dgb_ref[0, rows, :] = dgates

    ix = lambda b, j: (b, j, 0)
    ix5 = lambda b, j: (b, j, 0, 0, 0)
    row = pl.BlockSpec((1, tb, DN_WIDTH), ix)
    return pl.pallas_call(
        body, name=name, grid=(bsz, s // tb),
        in_specs=[pl.BlockSpec((1, tb, 3 * DN_WIDTH), ix), pl.BlockSpec((1, tb, LANES), ix), row, row, row, row,
                  pl.BlockSpec((1, nc, DN_HEADS, CHUNK, CHUNK), ix5), pl.BlockSpec((1, nc, DN_HEADS, 1, LANES), ix5)],
        out_specs=[pl.BlockSpec((1, tb, 3 * DN_WIDTH), ix), pl.BlockSpec((1, tb, LANES), ix)],
        out_shape=[jax.ShapeDtypeStruct((bsz, s, 3 * DN_WIDTH), F32), jax.ShapeDtypeStruct((bsz, s, LANES), F32)],
        compiler_params=_cparams(("parallel", "parallel")),
    )(qkv, gb, *cts)


def _dn_rec_examples(bsz):
    return max(n for n in range(1, DN_REC_EXAMPLES + 1) if bsz % n == 0)


def _dn_rec_specs(nb, tb, ix):
    nc = tb // CHUNK
    ix5 = lambda b, j: ix(b, j) + (0, 0)
    row = pl.BlockSpec((nb, tb, DN_WIDTH), ix)
    return [row, row, row, row, pl.BlockSpec((nb, nc, DN_HEADS, CHUNK, CHUNK), ix5), pl.BlockSpec((nb, nc, DN_HEADS, 1, LANES), ix5)]


def _dn_rec_fwd(intra, rev, name):
    u = intra[0]
    bsz, s, _ = u.shape
    nb = _dn_rec_examples(bsz)
    tb = min(DN_REC_TOKENS, s)
    nt = s // tb
    nc = tb // CHUNK
    nh = nb * DN_HEADS

    def body(u_ref, w_ref, qd_ref, kd_ref, at_ref, cd_ref, o_ref, st_ref, state):
        @pl.when(pl.program_id(1) == 0)
        def _():
            state[...] = jnp.zeros_like(state)

        def step(ci, carry):
            cidx = (nc - 1 - ci) if rev else ci
            rows = pl.ds(pl.multiple_of(cidx * CHUNK, CHUNK), CHUNK)
            heads = lambda ref: jnp.stack([ref[e, rows, h * LANES:(h + 1) * LANES] for e in range(nb) for h in range(DN_HEADS)])
            per = lambda ref: jnp.concatenate([ref[e, cidx] for e in range(nb)])
            st = state[...]
            o, new_state = _dn_rec(heads(u_ref), heads(w_ref), heads(qd_ref), heads(kd_ref), per(at_ref), per(cd_ref), st)
            state[...] = new_state
            for e in range(nb):
                st_ref[e, cidx] = st[e * DN_HEADS:(e + 1) * DN_HEADS]
                o_ref[e, rows, :] = jnp.concatenate([o[e * DN_HEADS + h] for h in range(DN_HEADS)], axis=-1)
            return carry

        lax.fori_loop(0, nc, step, 0)

    ix = (lambda b, j: (b, nt - 1 - j, 0)) if rev else (lambda b, j: (b, j, 0))
    ix5 = lambda b, j: ix(b, j) + (0, 0)
    return pl.pallas_call(
        body, name=name, grid=(bsz // nb, nt), in_specs=_dn_rec_specs(nb, tb, ix),
        out_specs=[pl.BlockSpec((nb, tb, DN_WIDTH), ix), pl.BlockSpec((nb, nc, DN_HEADS, DN_HEAD_DIM, DN_HEAD_DIM), ix5)],
        out_shape=[jax.ShapeDtypeStruct((bsz, s, DN_WIDTH), F32),
                   jax.ShapeDtypeStruct((bsz, s // CHUNK, DN_HEADS, DN_HEAD_DIM, DN_HEAD_DIM), F32)],
        scratch_shapes=[pltpu.VMEM((nh, DN_HEAD_DIM, DN_HEAD_DIM), F32)],
        compiler_params=_cparams(("parallel", "arbitrary")),
    )(*intra)


def _dn_rec_bwd(intra, states, do, rev, name):
    u = intra[0]
    bsz, s, _ = u.shape
    nb = _dn_rec_examples(bsz)
    tb = min(DN_REC_BWD_TOKENS, s)
    nt = s // tb
    nc = tb // CHUNK
    nh = nb * DN_HEADS

    def body(u_ref, w_ref, qd_ref, kd_ref, at_ref, cd_ref, st_ref, do_ref,
             du_ref, dw_ref, dqd_ref, dkd_ref, dat_ref, dcd_ref, dstate):
        @pl.when(pl.program_id(1) == 0)
        def _():
            dstate[...] = jnp.zeros_like(dstate)

        def step(ci, carry):
            cidx = ci if rev else (nc - 1 - ci)
            rows = pl.ds(pl.multiple_of(cidx * CHUNK, CHUNK), CHUNK)
            heads = lambda ref: jnp.stack([ref[e, rows, h * LANES:(h + 1) * LANES] for e in range(nb) for h in range(DN_HEADS)])
            per = lambda ref: jnp.concatenate([ref[e, cidx] for e in range(nb)])
            args = (heads(u_ref), heads(w_ref).astype(F32), heads(qd_ref).astype(F32), heads(kd_ref).astype(F32),
                    per(at_ref).astype(F32), per(cd_ref), per(st_ref))
            _, vjp = jax.vjp(_dn_rec, *args)
            du, dw, dqd, dkd, dat, dcd, dst = vjp((heads(do_ref), dstate[...]))
            dstate[...] = dst
            for e in range(nb):
                hs = slice(e * DN_HEADS, (e + 1) * DN_HEADS)
                dat_ref[e, cidx] = dat[hs]
                dcd_ref[e, cidx] = dcd[hs]
                for ref, val in ((du_ref, du), (dw_ref, dw), (dqd_ref, dqd), (dkd_ref, dkd)):
                    ref[e, rows, :] = jnp.concatenate([val[e * DN_HEADS + h] for h in range(DN_HEADS)], axis=-1)
            return carry

        lax.fori_loop(0, nc, step, 0)

    ix = (lambda b, j: (b, j, 0)) if rev else (lambda b, j: (b, nt - 1 - j, 0))
    ix5 = lambda b, j: ix(b, j) + (0, 0)
    row = pl.BlockSpec((nb, tb, DN_WIDTH), ix)
    f32 = lambda sd: jax.ShapeDtypeStruct(sd.shape, F32)
    return pl.pallas_call(
        body, name=name, grid=(bsz // nb, nt),
        in_specs=_dn_rec_specs(nb, tb, ix) + [pl.BlockSpec((nb, nc, DN_HEADS, DN_HEAD_DIM, DN_HEAD_DIM), ix5), row],
        out_specs=_dn_rec_specs(nb, tb, ix), out_shape=[f32(sd) for sd in _dn_intra_shapes(bsz, s)],
        scratch_shapes=[pltpu.VMEM((nh, DN_HEAD_DIM, DN_HEAD_DIM), F32)],
        compiler_params=_cparams(("parallel", "arbitrary")),
    )(*intra, states, do)


def _lru_gate_fn(xc, wa, ba, wx, bx, lam):
    r = _sigmoid(_bdot(xc, wa) + ba)
    ig = _sigmoid(_bdot(xc, wx) + bx)
    log_a = -LRU_C * r * _softplus(-lam)
    a = jnp.exp(log_a)
    b = jnp.sqrt(-jnp.tanh(log_a) * (a * a + 1.0)) * (ig * xc)
    return a, b


def _lru_fwd(proj3, conv_w, conv_b, wa, ba, wx, bx, lam, name):
    bsz, s, _ = proj3.shape
    nblk = LRU_WIDTH // LANES

    def body(x_ref, cw_ref, cb_ref, wa_ref, ba_ref, wx_ref, bx_ref, lam_ref, hf_ref, hb_ref):
        xc = _conv_fwd(x_ref[0], cw_ref[...], 2) + cb_ref[...]
        for d, h_ref in ((0, hf_ref), (1, hb_ref)):
            a, b = _lru_gate_fn(xc, wa_ref[d, 0], ba_ref[d:d + 1, :], wx_ref[d, 0], bx_ref[d:d + 1, :], lam_ref[d:d + 1, :])
            h_ref[0] = _scan(a, b, reverse=(d == 1))

    col = pl.BlockSpec((1, s, LANES), lambda b, j: (b, 0, j))
    vec2 = pl.BlockSpec((2, LANES), lambda b, j: (0, j))
    wspec = pl.BlockSpec((2, 1, LANES, LANES), lambda b, j: (0, j, 0, 0))
    return pl.pallas_call(
        body, name=name, grid=(bsz, nblk),
        in_specs=[col, pl.BlockSpec((4, LANES), lambda b, j: (0, j)), pl.BlockSpec((1, LANES), lambda b, j: (0, j)),
                  wspec, vec2, wspec, vec2, vec2],
        out_specs=[col, col],
        out_shape=[jax.ShapeDtypeStruct((bsz, s, LRU_WIDTH), F32)] * 2,
        compiler_params=_cparams(("parallel", "parallel")),
    )(proj3, conv_w, conv_b, wa, ba, wx, bx, lam)


def _lru_bwd(proj3, conv_w, conv_b, wa, ba, wx, bx, lam, hf, hb, dh, name):
    bsz, s, _ = proj3.shape
    nblk = LRU_WIDTH // LANES

    def body(x_ref, cw_ref, cb_ref, wa_ref, ba_ref, wx_ref, bx_ref, lam_ref, hf_ref, hb_ref, dh_ref,
             dx_ref, dcw_ref, dcb_ref, dwa_ref, dba_ref, dwx_ref, dbx_ref, dlam_ref):
        @pl.when(pl.program_id(1) == 0)
        def _():
            for r in (dcw_ref, dcb_ref, dwa_ref, dba_ref, dwx_ref, dbx_ref, dlam_ref):
                r[...] = jnp.zeros_like(r)

        x, cw = x_ref[0], cw_ref[...]
        xc = _conv_fwd(x, cw, 2) + cb_ref[...]
        dhv = dh_ref[0]
        dxc = jnp.zeros_like(xc)
        for d, h_ref in ((0, hf_ref), (1, hb_ref)):
            rev = d == 1
            args = (xc, wa_ref[d, 0].astype(F32), ba_ref[d:d + 1, :], wx_ref[d, 0].astype(F32), bx_ref[d:d + 1, :],
                    lam_ref[d:d + 1, :])
            (a, _), vjp = jax.vjp(_lru_gate_fn, *args)
            h = h_ref[0]
            a_next = _shift_down(a, 1) if rev else _shift_up(a, 1)
            lam_adj = _scan(a_next, dhv, reverse=not rev)
            h_prev = _shift_up(h, 1) if rev else _shift_down(h, 1)
            dxc_d, dwa, dba, dwx, dbx, dlam = vjp((lam_adj * h_prev, lam_adj))
            dxc = dxc + dxc_d
            dwa_ref[d, 0] += dwa
            dwx_ref[d, 0] += dwx
            dba_ref[d:d + 1, :] += dba
            dbx_ref[d:d + 1, :] += dbx
            dlam_ref[d:d + 1, :] += dlam
        dx, dcw = _conv_bwd(x, cw, 2, dxc)
        dx_ref[0] = dx.astype(dx_ref.dtype)
        dcw_ref[...] += dcw
        dcb_ref[...] += jnp.sum(dxc, axis=0, keepdims=True)

    col = pl.BlockSpec((1, s, LANES), lambda j, b: (b, 0, j))
    vec1 = pl.BlockSpec((1, LANES), lambda j, b: (0, j))
    vec2 = pl.BlockSpec((2, LANES), lambda j, b: (0, j))
    vec4 = pl.BlockSpec((4, LANES), lambda j, b: (0, j))
    wspec = pl.BlockSpec((2, 1, LANES, LANES), lambda j, b: (0, j, 0, 0))
    wshape = jax.ShapeDtypeStruct((2, nblk, LANES, LANES), F32)
    v2shape = jax.ShapeDtypeStruct((2, LRU_WIDTH), F32)
    return pl.pallas_call(
        body, name=name, grid=(nblk, bsz),
        in_specs=[col, vec4, vec1, wspec, vec2, wspec, vec2, vec2, col, col, col],
        out_specs=[col, vec4, vec1, wspec, vec2, wspec, vec2, vec2],
        out_shape=[jax.ShapeDtypeStruct((bsz, s, LRU_WIDTH), BF16), jax.ShapeDtypeStruct((4, LRU_WIDTH), F32),
                   jax.ShapeDtypeStruct((1, LRU_WIDTH), F32), wshape, v2shape, wshape, v2shape, v2shape],
        compiler_params=_cparams(("parallel", "arbitrary")),
    )(proj3, conv_w, conv_b, wa, ba, wx, bx, lam, hf, hb, dh)


def _mix_fn(o_f, o_b, z, lg, hf, hb, dn_g, lru_g):
    osum = o_f + o_b
    heads = []
    for h in range(DN_HEADS):
        sl = slice(h * LANES, (h + 1) * LANES)
        heads.append(_rms(osum[:, sl], dn_g) * _silu(z[:, sl]))
    lru = _rms(_gelu(lg) * (hf + hb), lru_g)
    return jnp.concatenate(heads + [lru], axis=-1)


def _mix_specs(tm):
    w = DN_WIDTH
    row = pl.BlockSpec((tm, w), lambda i: (i, 0))
    z = pl.BlockSpec((tm, w), lambda i: (i, P_Z // w))
    lg = pl.BlockSpec((tm, w), lambda i: (i, P_LG // w))
    return [row, row, z, lg, row, row, pl.BlockSpec((1, LANES), lambda i: (0, 0)), pl.BlockSpec((1, w), lambda i: (0, 0))]


def _mix_fwd(o_f, o_b, proj, hf, hb, dn_g, lru_g, name):
    t = proj.shape[0]
    tm = _row_tile(t)

    def body(of_ref, ob_ref, z_ref, lg_ref, hf_ref, hb_ref, dg_ref, lgn_ref, o_ref):
        o_ref[...] = _mix_fn(of_ref[...], ob_ref[...], z_ref[...], lg_ref[...], hf_ref[...], hb_ref[...],
                             dg_ref[...], lgn_ref[...]).astype(o_ref.dtype)

    return pl.pallas_call(
        body, name=name, grid=(t // tm,), in_specs=_mix_specs(tm),
        out_specs=pl.BlockSpec((tm, D_MODEL), lambda i: (i, 0)),
        out_shape=jax.ShapeDtypeStruct((t, D_MODEL), BF16),
        compiler_params=_cparams(("parallel",)),
    )(o_f, o_b, proj, proj, hf, hb, dn_g, lru_g)


def _mix_bwd(o_f, o_b, proj, hf, hb, dn_g, lru_g, dmix, name):
    t = proj.shape[0]
    tm = _row_tile(t)

    def body(of_ref, ob_ref, z_ref, lg_ref, hf_ref, hb_ref, dg_ref, lgn_ref, dm_ref,
             do_ref, dz_ref, dlg_ref, dh_ref, ddg_ref, dlgn_ref):
        _, vjp = jax.vjp(_mix_fn, of_ref[...], ob_ref[...], z_ref[...], lg_ref[...], hf_ref[...], hb_ref[...],
                         dg_ref[...], lgn_ref[...])
        do, _, dz, dlg, dh, _, ddg, dlgn = vjp(dm_ref[...])
        do_ref[...] = do
        dz_ref[...] = dz.astype(dz_ref.dtype)
        dlg_ref[...] = dlg.astype(dlg_ref.dtype)
        dh_ref[...] = dh

        @pl.when(pl.program_id(0) == 0)
        def _():
            ddg_ref[...] = jnp.zeros_like(ddg_ref)
            dlgn_ref[...] = jnp.zeros_like(dlgn_ref)

        ddg_ref[...] += ddg
        dlgn_ref[...] += dlgn

    row = pl.BlockSpec((tm, DN_WIDTH), lambda i: (i, 0))
    return pl.pallas_call(
        body, name=name, grid=(t // tm,),
        in_specs=_mix_specs(tm) + [pl.BlockSpec((tm, D_MODEL), lambda i: (i, 0))],
        out_specs=[row, row, row, row, pl.BlockSpec((1, LANES), lambda i: (0, 0)), pl.BlockSpec((1, DN_WIDTH), lambda i: (0, 0))],
        out_shape=[jax.ShapeDtypeStruct((t, DN_WIDTH), dt) for dt in (F32, BF16, BF16, F32)]
        + [jax.ShapeDtypeStruct((1, LANES), F32), jax.ShapeDtypeStruct((1, DN_WIDTH), F32)],
        compiler_params=_cparams(("arbitrary",)),
    )(o_f, o_b, proj, proj, hf, hb, dn_g, lru_g, dmix)


def _ffn_act_fwd(g3, u3, conv_w, conv_b, name):
    bsz, s, f = g3.shape
    nblk = f // LANES

    def body(g_ref, u_ref, w_ref, b_ref, o_ref):
        gate = _conv_fwd(g_ref[0], w_ref[...], 1) + b_ref[...]
        o_ref[0] = (_gelu(gate) * u_ref[0]).astype(o_ref.dtype)

    col = pl.BlockSpec((1, s, LANES), lambda b, j: (b, 0, j))
    return pl.pallas_call(
        body, name=name, grid=(bsz, nblk),
        in_specs=[col, col, pl.BlockSpec((3, LANES), lambda b, j: (0, j)), pl.BlockSpec((1, LANES), lambda b, j: (0, j))],
        out_specs=col, out_shape=jax.ShapeDtypeStruct((bsz, s, f), BF16),
        compiler_params=_cparams(("parallel", "parallel")),
    )(g3, u3, conv_w, conv_b)


def _ffn_act_bwd(g3, u3, conv_w, conv_b, dact3, name):
    bsz, s, f = g3.shape
    nblk = f // LANES

    def body(g_ref, u_ref, w_ref, b_ref, d_ref, dg_ref, du_ref, dw_ref, db_ref):
        g, w, u = g_ref[0], w_ref[...], u_ref[0]
        gate = _conv_fwd(g, w, 1) + b_ref[...]
        _, vjp = jax.vjp(lambda gt, uu: _gelu(gt) * uu, gate, u)
        dgate, du = vjp(d_ref[0])
        dg, dw = _conv_bwd(g, w, 1, dgate)
        dg_ref[0] = dg.astype(dg_ref.dtype)
        du_ref[0] = du.astype(du_ref.dtype)

        @pl.when(pl.program_id(1) == 0)
        def _():
            dw_ref[...] = jnp.zeros_like(dw_ref)
            db_ref[...] = jnp.zeros_like(db_ref)

        dw_ref[...] += dw
        db_ref[...] += jnp.sum(dgate, axis=0, keepdims=True)

    col = pl.BlockSpec((1, s, LANES), lambda j, b: (b, 0, j))
    w3 = pl.BlockSpec((3, LANES), lambda j, b: (0, j))
    w1 = pl.BlockSpec((1, LANES), lambda j, b: (0, j))
    return pl.pallas_call(
        body, name=name, grid=(nblk, bsz),
        in_specs=[col, col, w3, w1, col], out_specs=[col, col, w3, w1],
        out_shape=[jax.ShapeDtypeStruct((bsz, s, f), BF16), jax.ShapeDtypeStruct((bsz, s, f), BF16),
                   jax.ShapeDtypeStruct((3, f), F32), jax.ShapeDtypeStruct((1, f), F32)],
        compiler_params=_cparams(("parallel", "arbitrary")),
    )(g3, u3, conv_w, conv_b, dact3)


def _ple_fn(r, pg, pp, bg):
    return r + _sigmoid(pg + bg) * pp


def _ple_fwd(r, pg, pp, bg, name):
    t, d = r.shape
    tm = _row_tile(t)

    def body(r_ref, pg_ref, pp_ref, bg_ref, o_ref):
        o_ref[...] = _ple_fn(r_ref[...], pg_ref[...], pp_ref[...], bg_ref[...])

    row = pl.BlockSpec((tm, d), lambda i: (i, 0))
    return pl.pallas_call(
        body, name=name, grid=(t // tm,), in_specs=[row, row, row, pl.BlockSpec((1, d), lambda i: (0, 0))],
        out_specs=row, out_shape=jax.ShapeDtypeStruct((t, d), F32),
        compiler_params=_cparams(("parallel",)),
    )(r, pg, pp, bg)


def _ple_bwd(pg, pp, bg, dr, name):
    t, d = pg.shape
    tm = _row_tile(t)

    def body(pg_ref, pp_ref, bg_ref, dr_ref, dpg_ref, dpp_ref, dbg_ref):
        _, vjp = jax.vjp(lambda a, b, c: _sigmoid(a + c) * b, pg_ref[...], pp_ref[...], bg_ref[...])
        dpg, dpp, dbg = vjp(dr_ref[...])
        dpg_ref[...] = dpg.astype(dpg_ref.dtype)
        dpp_ref[...] = dpp.astype(dpp_ref.dtype)

        @pl.when(pl.program_id(0) == 0)
        def _():
            dbg_ref[...] = jnp.zeros_like(dbg_ref)

        dbg_ref[...] += dbg

    row = pl.BlockSpec((tm, d), lambda i: (i, 0))
    vec = pl.BlockSpec((1, d), lambda i: (0, 0))
    return pl.pallas_call(
        body, name=name, grid=(t // tm,), in_specs=[row, row, vec, row], out_specs=[row, row, vec],
        out_shape=[jax.ShapeDtypeStruct((t, d), BF16), jax.ShapeDtypeStruct((t, d), BF16), jax.ShapeDtypeStruct((1, d), F32)],
        compiler_params=_cparams(("arbitrary",)),
    )(pg, pp, bg, dr)


def _loss_head(r, g, target, name):
    t, d = r.shape
    tm = _row_tile(t)

    def loss_fn(x, gg, tgt):
        err = _rms(x, gg) - tgt
        return 0.5 * jnp.sum(jnp.sum(err * err, axis=-1, keepdims=True) * (1.0 / d), axis=0, keepdims=True)

    def body(r_ref, g_ref, t_ref, l_ref, dr_ref, dg_ref):
        val, vjp = jax.vjp(lambda x, gg: loss_fn(x, gg, t_ref[...]), r_ref[...], g_ref[...])
        dx, dg = vjp(jnp.ones((1, 1), F32))
        dr_ref[...] = dx

        @pl.when(pl.program_id(0) == 0)
        def _():
            l_ref[...] = jnp.zeros_like(l_ref)
            dg_ref[...] = jnp.zeros_like(dg_ref)

        l_ref[...] += val
        dg_ref[...] += dg

    row = pl.BlockSpec((tm, d), lambda i: (i, 0))
    vec = pl.BlockSpec((1, d), lambda i: (0, 0))
    one = pl.BlockSpec((1, 1), lambda i: (0, 0))
    return pl.pallas_call(
        body, name=name, grid=(t // tm,), in_specs=[row, vec, row], out_specs=[one, row, vec],
        out_shape=[jax.ShapeDtypeStruct((1, 1), F32), jax.ShapeDtypeStruct((t, d), F32), jax.ShapeDtypeStruct((1, d), F32)],
        compiler_params=_cparams(("arbitrary",)),
    )(r, g, target)


def _adamw_math(w, gg, m, v, d_ref, nm_ref, nv_ref):
    nm = ADAM_B1 * m + (1.0 - ADAM_B1) * gg
    nv = ADAM_B2 * v + (1.0 - ADAM_B2) * (gg * gg)
    m_hat = nm / (1.0 - ADAM_B1 ** ADAM_STEP)
    v_hat = nv / (1.0 - ADAM_B2 ** ADAM_STEP)
    d_ref[...] = -ADAM_LR * (m_hat / (jnp.sqrt(v_hat) + ADAM_EPS) + ADAM_WD * w)
    nm_ref[...] = nm
    nv_ref[...] = nv


def _adamw(w, g, m, v, name):
    r, c = w.shape
    tr = r if r <= 512 else _row_tile(r, 512)

    def body(w_ref, g_ref, m_ref, v_ref, d_ref, nm_ref, nv_ref):
        _adamw_math(w_ref[...], g_ref[...], m_ref[...], v_ref[...], d_ref, nm_ref, nv_ref)

    blk = pl.BlockSpec((tr, c), lambda i: (i, 0))
    return pl.pallas_call(
        body, name=name, grid=(r // tr,), in_specs=[blk] * 4, out_specs=[blk] * 3,
        out_shape=[jax.ShapeDtypeStruct((r, c), F32)] * 3,
        compiler_params=_cparams(("parallel",)),
    )(w, g, m, v)


def _prepare_weights(w):
    nl = w["w_in"].shape[0]
    w_in = w["w_in"].astype(BF16)
    pad = jnp.zeros(w_in.shape[:2] + (P_COLS - w_in.shape[2],), BF16)
    split = 4 * DN_WIDTH + N_GATE
    w_in_p = jnp.concatenate([w_in[:, :, split:], w_in[:, :, :split], pad], axis=-1)
    gate_vec = lambda a: jnp.pad(a.reshape(nl, 1, N_GATE // 2), ((0, 0), (0, 0), (N_GATE // 2, LANES - N_GATE)))

    def pair_blocks(a):
        a = a.reshape(nl, 2, 4, 2, 64, 64)
        z = jnp.zeros_like(a[:, :, :, 0])
        top = jnp.concatenate([a[:, :, :, 0], z], axis=-1)
        bot = jnp.concatenate([z, a[:, :, :, 1]], axis=-1)
        return jnp.concatenate([top, bot], axis=-2).astype(BF16)

    bf = lambda a: a.astype(BF16)
    return dict(
        norm1_g=w["norm1_g"][:, None, :], w_in=w_in_p,
        dn_conv_w=w["dn_conv_w"], alog=gate_vec(w["dn_a_log"]), dtb=gate_vec(w["dn_dt_bias"]),
        dn_norm_g=w["dn_norm_g"][:, None, :], lru_conv_w=w["lru_conv_w"], lru_conv_b=w["lru_conv_b"][:, None, :],
        lru_wa=pair_blocks(w["lru_wa"]), lru_ba=w["lru_ba"], lru_wx=pair_blocks(w["lru_wx"]), lru_bx=w["lru_bx"],
        lru_lambda=w["lru_lambda"], lru_norm_g=w["lru_norm_g"][:, None, :],
        w_out=bf(w["w_out"]), norm2_g=w["norm2_g"][:, None, :], ffn_wg=bf(w["ffn_wg"]), ffn_wu=bf(w["ffn_wu"]),
        ffn_conv_w=w["ffn_conv_w"], ffn_conv_b=w["ffn_conv_b"][:, None, :], ffn_wd=bf(w["ffn_wd"]),
        ple_norm_g=w["ple_norm_g"][:, None, :], ple_wg=bf(w["ple_wg"]), ple_bg=w["ple_bg"][:, None, :], ple_wp=bf(w["ple_wp"]),
        final_g=w["final_g"][None, :],
    )


def _unpair_blocks(a):
    top = a[:, :, :64, :64]
    bot = a[:, :, 64:, 64:]
    return jnp.stack([top, bot], axis=2).reshape(2, 8, 64, 64)


def _local_step(x, p, target, w):
    bsz, s, d = x.shape
    t = bsz * s
    nl = w["w_in"].shape[0]
    kw = _prepare_weights(w)
    flat = lambda a: a.reshape(t, a.shape[-1])
    seq = lambda a: a.reshape(bsz, s, a.shape[-1])

    saved = []
    r = flat(x)
    for i in range(nl):
        n = f"l{i}_"
        sv = {"r0": r}
        h = _rmsnorm_fwd(r, kw["norm1_g"][i], n + "norm1")
        proj = _matmul([(h, kw["w_in"][i])], n + "in_proj")
        gb = _dn_gates_fwd(proj, kw["alog"][i], kw["dtb"][i], n + "dn_gates")
        qkv = _dn_prep_fwd(seq(proj), kw["dn_conv_w"][i], n + "dn_prep")
        in_f = _dn_intra_fwd(qkv, seq(gb), False, n + "dn_intra_fwd")
        in_b = _dn_intra_fwd(qkv, seq(gb), True, n + "dn_intra_rev")
        o_f, st_f = _dn_rec_fwd(in_f, False, n + "dn_rec_fwd")
        o_b, st_b = _dn_rec_fwd(in_b, True, n + "dn_rec_rev")
        lru_args = (seq(proj), kw["lru_conv_w"][i], kw["lru_conv_b"][i], kw["lru_wa"][i], kw["lru_ba"][i],
                    kw["lru_wx"][i], kw["lru_bx"][i], kw["lru_lambda"][i])
        hf, hb = _lru_fwd(*lru_args, n + "lru")
        mix_args = (flat(o_f), flat(o_b), proj, flat(hf), flat(hb), kw["dn_norm_g"][i], kw["lru_norm_g"][i])
        mix = _mix_fwd(*mix_args, n + "mix")
        r1 = _matmul([(mix, kw["w_out"][i])], n + "out_proj", res=r)
        h2 = _rmsnorm_fwd(r1, kw["norm2_g"][i], n + "norm2")
        fg = _matmul([(h2, kw["ffn_wg"][i])], n + "ffn_g")
        fu = _matmul([(h2, kw["ffn_wu"][i])], n + "ffn_u")
        act = _ffn_act_fwd(seq(fg), seq(fu), kw["ffn_conv_w"][i], kw["ffn_conv_b"][i], n + "ffn_act")
        r2 = _matmul([(flat(act), kw["ffn_wd"][i])], n + "ffn_d", res=r1)
        hp = _rmsnorm_fwd(r2, kw["ple_norm_g"][i], n + "ple_norm")
        pg = _matmul([(hp, kw["ple_wg"][i])], n + "ple_g")
        pi = flat(p[i])
        pp = _matmul([(pi, kw["ple_wp"][i])], n + "ple_p")
        r3 = _ple_fwd(r2, pg, pp, kw["ple_bg"][i], n + "ple")
        sv.update(h=h, proj=proj, gb=gb, qkv=qkv, st_f=st_f, st_b=st_b, in_f=in_f, in_b=in_b, lru_args=lru_args, hf=hf, hb=hb,
                  mix_args=mix_args, mix=mix, r1=r1, h2=h2, fg=fg, fu=fu, act=act, r2=r2, hp=hp, pg=pg, pp=pp, pi=pi)
        saved.append(sv)
        r = r3

    loss, dr, dfinal = _loss_head(r, kw["final_g"], flat(target), "loss_head")
    grads = {k: [None] * nl for k in (
        "norm1_g", "w_in", "dn_conv_w", "dn_a_log", "dn_dt_bias", "dn_norm_g", "lru_conv_w", "lru_conv_b", "lru_wa", "lru_ba",
        "lru_wx", "lru_bx", "lru_lambda", "lru_norm_g", "w_out", "norm2_g", "ffn_wg", "ffn_wu", "ffn_conv_w", "ffn_conv_b",
        "ffn_wd", "ple_norm_g", "ple_wg", "ple_bg", "ple_wp", "dn_a_log_lanes", "dn_dt_bias_lanes")}

    for i in reversed(range(nl)):
        n = f"b{i}_"
        sv = saved[i]
        dpg, dpp, dbg = _ple_bwd(sv["pg"], sv["pp"], kw["ple_bg"][i], dr, n + "ple")
        grads["ple_bg"][i] = dbg[0]
        grads["ple_wp"][i] = _matmul_tn(sv["pi"], dpp, n + "ple_wp")
        grads["ple_wg"][i] = _matmul_tn(sv["hp"], dpg, n + "ple_wg")
        dhp = _matmul([(dpg, kw["ple_wg"][i])], n + "ple_dh", nt=True)
        dr2, dr2b, dg = _rmsnorm_bwd(sv["r2"], kw["ple_norm_g"][i], dhp, dr, n + "ple_norm")
        grads["ple_norm_g"][i] = dg[0]
        grads["ffn_wd"][i] = _matmul_tn(flat(sv["act"]), dr2b, n + "ffn_wd")
        dact = _matmul([(dr2b, kw["ffn_wd"][i])], n + "ffn_dact", nt=True)
        dfg, dfu, dcw, dcb = _ffn_act_bwd(seq(sv["fg"]), seq(sv["fu"]), kw["ffn_conv_w"][i], kw["ffn_conv_b"][i], seq(dact),
                                          n + "ffn_act")
        grads["ffn_conv_w"][i] = dcw
        grads["ffn_conv_b"][i] = dcb[0]
        grads["ffn_wg"][i] = _matmul_tn(sv["h2"], flat(dfg), n + "ffn_wg")
        grads["ffn_wu"][i] = _matmul_tn(sv["h2"], flat(dfu), n + "ffn_wu")
        dh2 = _matmul([(flat(dfg), kw["ffn_wg"][i]), (flat(dfu), kw["ffn_wu"][i])], n + "ffn_dh", nt=True)
        dr1, dr1b, dg = _rmsnorm_bwd(sv["r1"], kw["norm2_g"][i], dh2, dr2, n + "norm2")
        grads["norm2_g"][i] = dg[0]
        grads["w_out"][i] = _matmul_tn(sv["mix"], dr1b, n + "w_out")
        dmix = _matmul([(dr1b, kw["w_out"][i])], n + "dmix", nt=True)
        do, dz, dlg, dh, ddn_g, dlru_g = _mix_bwd(*sv["mix_args"], dmix, n + "mix")
        grads["dn_norm_g"][i] = ddn_g[0]
        grads["lru_norm_g"][i] = dlru_g[0]
        dlx, dcw, dcb, dwa, dba, dwx, dbx, dlam = _lru_bwd(*sv["lru_args"], sv["hf"], sv["hb"], seq(dh), n + "lru")
        grads["lru_conv_w"][i] = dcw
        grads["lru_conv_b"][i] = dcb[0]
        grads["lru_wa"][i] = _unpair_blocks(dwa)
        grads["lru_wx"][i] = _unpair_blocks(dwx)
        grads["lru_ba"][i], grads["lru_bx"][i], grads["lru_lambda"][i] = dba, dbx, dlam
        ct_f = _dn_rec_bwd(sv["in_f"], sv["st_f"], seq(do), False, n + "dn_rec_fwd")
        ct_b = _dn_rec_bwd(sv["in_b"], sv["st_b"], seq(do), True, n + "dn_rec_rev")
        dqkv_f, dgb_f = _dn_intra_bwd(sv["qkv"], seq(sv["gb"]), ct_f, False, n + "dn_intra_fwd")
        dqkv_b, dgb_b = _dn_intra_bwd(sv["qkv"], seq(sv["gb"]), ct_b, True, n + "dn_intra_rev")
        dgate, dalog, ddtb = _dn_gates_bwd(sv["proj"], kw["alog"][i], kw["dtb"][i], flat(dgb_f), flat(dgb_b), n + "dn_gates")
        grads["dn_a_log"][i] = dalog[0, N_GATE // 2:N_GATE].reshape(2, DN_HEADS)
        grads["dn_dt_bias"][i] = ddtb[0, N_GATE // 2:N_GATE].reshape(2, DN_HEADS)
        grads["dn_a_log_lanes"][i] = dalog[0]
        grads["dn_dt_bias_lanes"][i] = ddtb[0]
        dpqkv, dcw = _dn_prep_bwd(seq(sv["proj"]), kw["dn_conv_w"][i], dqkv_f, dqkv_b, n + "dn_prep")
        grads["dn_conv_w"][i] = dcw
        segs = [flat(dlx), dlg, flat(dpqkv), dz, dgate]
        w_in_i = kw["w_in"][i]
        dq3 = flat(dpqkv)
        wseg = DN_WIDTH
        terms = [(flat(dlx), (w_in_i, wseg, P_LX // wseg)), (dlg, (w_in_i, wseg, P_LG // wseg))]
        terms += [((dq3, wseg, j), (w_in_i, wseg, P_Q // wseg + j)) for j in range(3)]
        terms += [(dz, (w_in_i, wseg, P_Z // wseg)), (dgate, (w_in_i, LANES, P_GATE // LANES))]
        dh1 = _matmul(terms, n + "in_dh", nt=True)
        dwp = jnp.concatenate([_matmul_tn(sv["h"], sg, n + f"w_in{j}") for j, sg in enumerate(segs)], axis=-1)
        split = 4 * DN_WIDTH + N_GATE
        grads["w_in"][i] = jnp.concatenate([dwp[:, P_Q:P_Q + split], dwp[:, :P_Q]], axis=-1)
        dr, _, dg = _rmsnorm_bwd(sv["r0"], kw["norm1_g"][i], dh1, dr1, n + "norm1")
        grads["norm1_g"][i] = dg[0]

    grads["final_g"] = dfinal[0]
    return loss[0, 0], dr.reshape(bsz, s, d), grads


MESH = pl.DeviceIdType.MESH
ANY = pl.BlockSpec(memory_space=pl.ANY)
N_CHIPS = 4
HALF = 2


def _place():
    x, y, c = lax.axis_index("x"), lax.axis_index("y"), lax.axis_index("c")
    chips = [(1 - x, y), (x, 1 - y), (1 - x, 1 - y)]
    return x, y, c, chips


def _dma_sems(n):
    return pltpu.SemaphoreType.DMA((n,))


def _gather_weights(shards):
    nt = len(shards)

    def body(*refs):
        w_refs, g_refs = refs[:nt], refs[nt:2 * nt]
        send_sems, recv_sems = refs[2 * nt:]
        x, y, c, chips = _place()
        me = 2 * x + y
        mine = pl.ds(c * HALF, HALF)
        theirs = pl.ds((1 - c) * HALF, HALF)

        def rc(src, dst, k, to):
            return pltpu.make_async_remote_copy(src_ref=src, dst_ref=dst, send_sem=send_sems.at[k], recv_sem=recv_sems.at[k],
                                                device_id=to, device_id_type=MESH)

        first = []
        for t in range(nt):
            for r, (px, py) in enumerate(chips):
                first.append(rc(w_refs[t].at[mine], g_refs[t].at[me, mine], 6 * t + r, (px, py, c)))
        for cp in first:
            cp.start()
        passed = []
        for t in range(nt):
            for r, (px, py) in enumerate(chips):
                peer = 2 * px + py
                rc(w_refs[t].at[mine], g_refs[t].at[peer, mine], 6 * t + r, (px, py, c)).wait_recv()
                fw = rc(g_refs[t].at[peer, mine], g_refs[t].at[peer, mine], 6 * t + 3 + r, (x, y, 1 - c))
                fw.start()
                passed.append(fw)
        for t in range(nt):
            for r, (px, py) in enumerate(chips):
                peer = 2 * px + py
                rc(g_refs[t].at[peer, theirs], g_refs[t].at[peer, theirs], 6 * t + 3 + r, (x, y, 1 - c)).wait_recv()
        for cp in first + passed:
            cp.wait_send()

    return pl.pallas_call(
        body, name="gather_weights", in_specs=[ANY] * nt, out_specs=[ANY] * nt,
        out_shape=[jax.ShapeDtypeStruct((N_CHIPS,) + a.shape, a.dtype) for a in shards],
        scratch_shapes=[_dma_sems(6 * nt), _dma_sems(6 * nt)],
    )(*shards)


def _swap_halves(blocks):
    nt = len(blocks)

    def body(*refs):
        p_refs, l_refs = refs[:nt], refs[nt:2 * nt]
        send_sems, recv_sems = refs[2 * nt:]
        x, y, c, _ = _place()
        theirs = pl.ds((1 - c) * HALF, HALF)
        cps = [pltpu.make_async_remote_copy(src_ref=p_refs[t].at[:, theirs], dst_ref=l_refs[t], send_sem=send_sems.at[t],
                                            recv_sem=recv_sems.at[t], device_id=(x, y, 1 - c), device_id_type=MESH)
               for t in range(nt)]
        for cp in cps:
            cp.start()
        for cp in cps:
            cp.wait_send()
            cp.wait_recv()

    return pl.pallas_call(
        body, name="swap_halves", in_specs=[ANY] * nt, out_specs=[ANY] * nt,
        out_shape=[jax.ShapeDtypeStruct((a.shape[0], HALF) + a.shape[2:], a.dtype) for a in blocks],
        scratch_shapes=[_dma_sems(nt), _dma_sems(nt)],
    )(*blocks)


def _tile_rows(a, row_bytes):
    best = None
    for d in range(16, a + 1, 16):
        if a % d == 0 and d * row_bytes <= 2 * 1024 * 1024:
            best = d
    return best if best is not None else a


def _add_halves(pg, l1, c_arr, name):
    n, _, a, b = pg.shape
    ta = _tile_rows(a, 4 * b)

    def body(c_ref, a_ref, b_ref, o_ref):
        o_ref[...] = (a_ref[...] + b_ref[...]).astype(o_ref.dtype)

    grid_spec = pltpu.PrefetchScalarGridSpec(
        num_scalar_prefetch=1, grid=(n, HALF, a // ta),
        in_specs=[pl.BlockSpec((1, 1, ta, b), lambda k, l, i, c_ref: (k, c_ref[0] * HALF + l, i, 0)),
                  pl.BlockSpec((1, 1, ta, b), lambda k, l, i, c_ref: (k, l, i, 0))],
        out_specs=pl.BlockSpec((1, 1, ta, b), lambda k, l, i, c_ref: (k, l, i, 0)))
    return pl.pallas_call(
        body, name=name, grid_spec=grid_spec, out_shape=jax.ShapeDtypeStruct((n, HALF, a, b), BF16),
        compiler_params=_cparams(("parallel", "parallel", "parallel")),
    )(c_arr, pg, l1)


def _scatter_chips(blocks):
    nt = len(blocks)

    def body(*refs):
        q_refs, l_refs = refs[:nt], refs[nt:2 * nt]
        send_sems, recv_sems = refs[2 * nt:]
        x, y, c, chips = _place()
        me = 2 * x + y

        def rc(t, r, src_chip, dst_chip, to):
            return pltpu.make_async_remote_copy(src_ref=q_refs[t].at[src_chip], dst_ref=l_refs[t].at[dst_chip],
                                                send_sem=send_sems.at[3 * t + r], recv_sem=recv_sems.at[3 * t + r],
                                                device_id=to, device_id_type=MESH)

        sends = [rc(t, r, 2 * px + py, me, (px, py, c)) for t in range(nt) for r, (px, py) in enumerate(chips)]
        for cp in sends:
            cp.start()
        for t in range(nt):
            for r, (px, py) in enumerate(chips):
                rc(t, r, 2 * px + py, 2 * px + py, (px, py, c)).wait_recv()
        for cp in sends:
            cp.wait_send()

    return pl.pallas_call(
        body, name="scatter_chips", in_specs=[ANY] * nt, out_specs=[ANY] * nt,
        out_shape=[jax.ShapeDtypeStruct(a.shape, a.dtype) for a in blocks],
        scratch_shapes=[_dma_sems(3 * nt), _dma_sems(3 * nt)],
    )(*blocks)


def _sum_chips(l2, own, me_arr, name):
    n, _, a, b = l2.shape
    ta = _tile_rows(a, 4 * b)

    def body(me_ref, a_ref, own_ref, o_ref):
        me = me_ref[0]
        acc = jnp.where(me == 0, own_ref[0, 0], a_ref[0, 0]).astype(F32)
        for k in range(1, n):
            acc = acc + jnp.where(me == k, own_ref[0, 0], a_ref[k, 0]).astype(F32)
        o_ref[0] = acc

    grid_spec = pltpu.PrefetchScalarGridSpec(
        num_scalar_prefetch=1, grid=(HALF, a // ta),
        in_specs=[pl.BlockSpec((n, 1, ta, b), lambda l, i, me_ref: (0, l, i, 0)),
                  pl.BlockSpec((1, 1, ta, b), lambda l, i, me_ref: (me_ref[0], l, i, 0))],
        out_specs=pl.BlockSpec((1, ta, b), lambda l, i, me_ref: (l, i, 0)))
    return pl.pallas_call(
        body, name=name, grid_spec=grid_spec, out_shape=jax.ShapeDtypeStruct((HALF, a, b), F32),
        compiler_params=_cparams(("parallel", "parallel")),
    )(me_arr, l2, own)


def _swap_reduced(parts):
    nt = len(parts)

    def body(*refs):
        r_refs, g_refs = refs[:nt], refs[nt:2 * nt]
        send_sems, recv_sems = refs[2 * nt:]
        x, y, c, _ = _place()
        cps = [pltpu.make_async_remote_copy(src_ref=r_refs[t], dst_ref=g_refs[t], send_sem=send_sems.at[t],
                                            recv_sem=recv_sems.at[t], device_id=(x, y, 1 - c), device_id_type=MESH)
               for t in range(nt)]
        for cp in cps:
            cp.start()
        for cp in cps:
            cp.wait_send()
            cp.wait_recv()

    return pl.pallas_call(
        body, name="swap_reduced", in_specs=[ANY] * nt, out_specs=[ANY] * nt,
        out_shape=[jax.ShapeDtypeStruct(a.shape, a.dtype) for a in parts],
        scratch_shapes=[_dma_sems(nt), _dma_sems(nt)],
    )(*parts)


def _adamw_halves(w, mine, theirs, m, v, c_arr, name):
    nl, a, b = w.shape
    ta = _tile_rows(a, 4 * b)

    def body(c_ref, w_ref, g1_ref, g2_ref, m_ref, v_ref, g_ref, d_ref, nm_ref, nv_ref):
        gg = jnp.where(pl.program_id(0) // HALF == c_ref[0], g1_ref[...], g2_ref[...])
        g_ref[...] = gg
        _adamw_math(w_ref[...], gg, m_ref[...], v_ref[...], d_ref, nm_ref, nv_ref)

    full = pl.BlockSpec((1, ta, b), lambda l, i, c_ref: (l, i, 0))
    half = pl.BlockSpec((1, ta, b), lambda l, i, c_ref: (l % HALF, i, 0))
    grid_spec = pltpu.PrefetchScalarGridSpec(num_scalar_prefetch=1, grid=(nl, a // ta),
                                             in_specs=[full, half, half, full, full], out_specs=[full] * 4)
    return pl.pallas_call(
        body, name=name, grid_spec=grid_spec, out_shape=[jax.ShapeDtypeStruct(w.shape, F32)] * 4,
        compiler_params=_cparams(("parallel", "parallel")),
    )(c_arr, w, mine, theirs, m, v)


def _gather_chips(v):
    def body(v_ref, g_ref, send_sems, recv_sems, local_sem):
        x, y, c, chips = _place()
        me = 2 * x + y
        loc = pltpu.make_async_copy(v_ref, g_ref.at[me], local_sem)
        loc.start()
        sends = [pltpu.make_async_remote_copy(src_ref=v_ref, dst_ref=g_ref.at[me], send_sem=send_sems.at[r], recv_sem=recv_sems.at[r],
                                              device_id=(px, py, c), device_id_type=MESH) for r, (px, py) in enumerate(chips)]
        for cp in sends:
            cp.start()
        for r, (px, py) in enumerate(chips):
            pltpu.make_async_remote_copy(src_ref=v_ref, dst_ref=g_ref.at[2 * px + py], send_sem=send_sems.at[r],
                                         recv_sem=recv_sems.at[r], device_id=(px, py, c), device_id_type=MESH).wait_recv()
        for cp in sends:
            cp.wait_send()
        loc.wait()

    return pl.pallas_call(
        body, name="gather_chips", in_specs=[ANY], out_specs=ANY, out_shape=jax.ShapeDtypeStruct((N_CHIPS,) + v.shape, v.dtype),
        scratch_shapes=[_dma_sems(3), _dma_sems(3), pltpu.SemaphoreType.DMA],
    )(v)


BIG = (("w_in", 2), ("w_out", 1), ("ffn_wg", 2), ("ffn_wu", 2), ("ffn_wd", 1), ("ple_wg", 1), ("ple_wp", 2))
SMALL = (("dn_conv_w", 2), ("lru_conv_w", 2), ("lru_ba", 2), ("lru_bx", 2), ("lru_lambda", 2), ("ffn_conv_w", 2))
REPL = ("norm1_g", "dn_a_log", "dn_dt_bias", "dn_norm_g", "lru_conv_b", "lru_wa", "lru_wx", "lru_norm_g", "norm2_g",
        "ffn_conv_b", "ple_norm_g", "ple_bg", "final_g")
WEIGHTS = ("norm1_g", "w_in", "dn_conv_w", "dn_a_log", "dn_dt_bias", "dn_norm_g", "lru_conv_w", "lru_conv_b", "lru_wa", "lru_ba",
           "lru_wx", "lru_bx", "lru_lambda", "lru_norm_g", "w_out", "norm2_g", "ffn_wg", "ffn_wu", "ffn_conv_w", "ffn_conv_b",
           "ffn_wd", "ple_norm_g", "ple_wg", "ple_bg", "ple_wp", "final_g")
GATE_PARAMS = ("dn_a_log", "dn_dt_bias")
ROW_ALIGN = 16


def _rows_for(n_elems):
    rows = -(-n_elems // LANES)
    return -(-rows // ROW_ALIGN) * ROW_ALIGN


def _join_chips(g, own, me, axis):
    return jnp.concatenate([jnp.where(me == k, own, g[k]) for k in range(N_CHIPS)], axis=axis)


def _chip_blocks(layers, axis):
    nl = len(layers)
    size = layers[0].shape[axis - 1] // N_CHIPS
    cut = (lambda g, k: g[:, k * size:(k + 1) * size]) if axis == 2 else (lambda g, k: g[k * size:(k + 1) * size])
    return jnp.stack([jnp.stack([cut(layers[l], k) for l in range(nl)]) for k in range(N_CHIPS)])


def _pack_small(w):
    nl = w[SMALL[0][0]].shape[0]
    flat = jnp.concatenate([w[n].reshape(nl, -1) for n, _ in SMALL], axis=1)
    rows = _rows_for(flat.shape[1])
    return jnp.pad(flat, ((0, 0), (0, rows * LANES - flat.shape[1]))).reshape(nl, rows, LANES)


def _unpack_small_gathered(g, own, me, w):
    nl = g.shape[1]
    flat = jnp.stack([jnp.where(me == k, own, g[k]) for k in range(N_CHIPS)]).reshape(N_CHIPS, nl, -1)
    out, off = {}, 0
    for n, _ in SMALL:
        _, a, b = w[n].shape
        piece = flat[:, :, off:off + a * b].reshape(N_CHIPS, nl, a, b)
        off += a * b
        out[n] = jnp.transpose(piece, (1, 2, 0, 3)).reshape(nl, a, N_CHIPS * b)
    return out


def _pack_small_grads(grads, w):
    nl = w["w_in"].shape[0]
    cols = []
    for n, _ in SMALL:
        _, a, b = w[n].shape
        gfull = jnp.stack(grads[n])
        cols.append(jnp.transpose(gfull.reshape(nl, a, N_CHIPS, b), (2, 0, 1, 3)).reshape(N_CHIPS, nl, a * b))
    small = jnp.concatenate(cols, axis=2)
    rs = _rows_for(small.shape[2])
    small = jnp.pad(small, ((0, 0), (0, 0), (0, rs * LANES - small.shape[2])))
    src = lambda n: n + "_lanes" if n in GATE_PARAMS else n
    rep = jnp.concatenate([(grads[n] if n == "final_g" else jnp.stack(grads[src(n)])).reshape(-1) for n in REPL])
    rr = _rows_for(-(-rep.shape[0] // (N_CHIPS * nl)))
    rep = jnp.pad(rep, (0, N_CHIPS * nl * rr * LANES - rep.shape[0])).reshape(N_CHIPS, nl, rr * LANES)
    return jnp.concatenate([small, rep], axis=2).reshape(N_CHIPS, nl, rs + rr, LANES), (rs, rr)


def _unpack_small_reduced(g, rep_all, rows, w):
    rs, _ = rows
    nl = g.shape[0]
    out = {}
    flat = g[:, :rs].reshape(nl, -1)
    off = 0
    for n, _ in SMALL:
        _, a, b = w[n].shape
        out[n] = flat[:, off:off + a * b].reshape(nl, a, b)
        off += a * b
    flat = rep_all.reshape(-1)
    off = 0
    for n in REPL:
        if n in GATE_PARAMS:
            size = nl * LANES
            out[n] = flat[off:off + size].reshape(nl, LANES)[:, N_GATE // 2:N_GATE].reshape(w[n].shape)
        else:
            size = math.prod(w[n].shape)
            out[n] = flat[off:off + size].reshape(w[n].shape)
        off += size
    return out


def _as2d(a):
    if a.ndim == 1:
        return a.reshape(1, -1)
    return a.reshape(-1, a.shape[-1])


def kernel(x, p, norm1_g, w_in, dn_conv_w, dn_a_log, dn_dt_bias, dn_norm_g, lru_conv_w, lru_conv_b, lru_wa, lru_ba, lru_wx, lru_bx, lru_lambda, lru_norm_g, w_out, norm2_g, ffn_wg, ffn_wu, ffn_conv_w, ffn_conv_b, ffn_wd, ple_norm_g, ple_wg, ple_bg, ple_wp, final_g, loss_target, m_norm1_g, m_w_in, m_dn_conv_w, m_dn_a_log, m_dn_dt_bias, m_dn_norm_g, m_lru_conv_w, m_lru_conv_b, m_lru_wa, m_lru_ba, m_lru_wx, m_lru_bx, m_lru_lambda, m_lru_norm_g, m_w_out, m_norm2_g, m_ffn_wg, m_ffn_wu, m_ffn_conv_w, m_ffn_conv_b, m_ffn_wd, m_ple_norm_g, m_ple_wg, m_ple_bg, m_ple_wp, m_final_g, v_norm1_g, v_w_in, v_dn_conv_w, v_dn_a_log, v_dn_dt_bias, v_dn_norm_g, v_lru_conv_w, v_lru_conv_b, v_lru_wa, v_lru_ba, v_lru_wx, v_lru_bx, v_lru_lambda, v_lru_norm_g, v_w_out, v_norm2_g, v_ffn_wg, v_ffn_wu, v_ffn_conv_w, v_ffn_conv_b, v_ffn_wd, v_ple_norm_g, v_ple_wg, v_ple_bg, v_ple_wp, v_final_g):
    w = dict(norm1_g=norm1_g, w_in=w_in, dn_conv_w=dn_conv_w, dn_a_log=dn_a_log, dn_dt_bias=dn_dt_bias, dn_norm_g=dn_norm_g,
             lru_conv_w=lru_conv_w, lru_conv_b=lru_conv_b, lru_wa=lru_wa, lru_ba=lru_ba, lru_wx=lru_wx, lru_bx=lru_bx,
             lru_lambda=lru_lambda, lru_norm_g=lru_norm_g, w_out=w_out, norm2_g=norm2_g, ffn_wg=ffn_wg, ffn_wu=ffn_wu,
             ffn_conv_w=ffn_conv_w, ffn_conv_b=ffn_conv_b, ffn_wd=ffn_wd, ple_norm_g=ple_norm_g, ple_wg=ple_wg, ple_bg=ple_bg,
             ple_wp=ple_wp, final_g=final_g)
    m = dict(norm1_g=m_norm1_g, w_in=m_w_in, dn_conv_w=m_dn_conv_w, dn_a_log=m_dn_a_log, dn_dt_bias=m_dn_dt_bias,
             dn_norm_g=m_dn_norm_g, lru_conv_w=m_lru_conv_w, lru_conv_b=m_lru_conv_b, lru_wa=m_lru_wa, lru_ba=m_lru_ba,
             lru_wx=m_lru_wx, lru_bx=m_lru_bx, lru_lambda=m_lru_lambda, lru_norm_g=m_lru_norm_g, w_out=m_w_out, norm2_g=m_norm2_g,
             ffn_wg=m_ffn_wg, ffn_wu=m_ffn_wu, ffn_conv_w=m_ffn_conv_w, ffn_conv_b=m_ffn_conv_b, ffn_wd=m_ffn_wd,
             ple_norm_g=m_ple_norm_g, ple_wg=m_ple_wg, ple_bg=m_ple_bg, ple_wp=m_ple_wp, final_g=m_final_g)
    v = dict(norm1_g=v_norm1_g, w_in=v_w_in, dn_conv_w=v_dn_conv_w, dn_a_log=v_dn_a_log, dn_dt_bias=v_dn_dt_bias,
             dn_norm_g=v_dn_norm_g, lru_conv_w=v_lru_conv_w, lru_conv_b=v_lru_conv_b, lru_wa=v_lru_wa, lru_ba=v_lru_ba,
             lru_wx=v_lru_wx, lru_bx=v_lru_bx, lru_lambda=v_lru_lambda, lru_norm_g=v_lru_norm_g, w_out=v_w_out, norm2_g=v_norm2_g,
             ffn_wg=v_ffn_wg, ffn_wu=v_ffn_wu, ffn_conv_w=v_ffn_conv_w, ffn_conv_b=v_ffn_conv_b, ffn_wd=v_ffn_wd,
             ple_norm_g=v_ple_norm_g, ple_wg=v_ple_wg, ple_bg=v_ple_bg, ple_wp=v_ple_wp, final_g=v_final_g)

    me = 2 * lax.axis_index("x") + lax.axis_index("y")
    own = [w[n].astype(BF16) for n, _ in BIG] + [_pack_small(w)]
    gathered = _gather_weights(own)
    full = {n: w[n] for n in REPL}
    for (n, axis), g, o in zip(BIG, gathered, own):
        full[n] = _join_chips(g, o, me, axis)
    full.update(_unpack_small_gathered(gathered[-1], own[-1], me, w))

    loss_local, grad_x, grads = _local_step(x, p, loss_target, full)
    loss = lax.psum(loss_local, ("x", "y", "c"))

    small_pack, rows = _pack_small_grads(grads, w)
    blocks = [_chip_blocks(grads[n], axis) for n, axis in BIG]
    blocks.append(small_pack)
    names = [n for n, _ in BIG] + ["small"]
    c_idx = lax.axis_index("c")
    c_arr = c_idx.astype(jnp.int32).reshape(1)
    me_arr = me.astype(jnp.int32).reshape(1)
    from_sibling = _swap_halves(blocks)
    halves = [_add_halves(b4, l1, c_arr, "add_halves_" + n) for n, b4, l1 in zip(names, blocks, from_sibling)]
    arrived = _scatter_chips(halves)
    mine = [_sum_chips(l2, q, me_arr, "sum_chips_" + n) for n, l2, q in zip(names, arrived, halves)]
    theirs = _swap_reduced(mine)

    g, deltas, new_m, new_v = {}, {}, {}, {}
    for i, (n, _) in enumerate(BIG):
        g[n], deltas[n], new_m[n], new_v[n] = _adamw_halves(w[n], mine[i], theirs[i], m[n], v[n], c_arr, "adamw_" + n)
    small = jnp.where(c_idx == 0, jnp.concatenate([mine[-1], theirs[-1]]), jnp.concatenate([theirs[-1], mine[-1]]))
    rep_all = _gather_chips(small[:, rows[0]:])
    gs = _unpack_small_reduced(small, rep_all, rows, w)
    for n in WEIGHTS:
        if n in g:
            continue
        g[n] = gs[n].reshape(w[n].shape)
        d2, m2, v2 = _adamw(_as2d(w[n]), _as2d(g[n]), _as2d(m[n]), _as2d(v[n]), "adamw_" + n)
        deltas[n], new_m[n], new_v[n] = d2.reshape(w[n].shape), m2.reshape(w[n].shape), v2.reshape(w[n].shape)
    return (loss, grad_x, *[g[n] for n in WEIGHTS], *[deltas[n] for n in WEIGHTS], *[new_m[n] for n in WEIGHTS],
            *[new_v[n] for n in WEIGHTS])
```

```python
import functools
import math

import jax
import jax.numpy as jnp
from jax import lax
from jax.experimental import pallas as pl
from jax.experimental.pallas import tpu as pltpu

F32 = jnp.float32
BF16 = jnp.bfloat16

D_MODEL = 1024
DN_HEADS = 4
DN_HEAD_DIM = 128
DN_WIDTH = 512
LRU_WIDTH = 512
LRU_C = 8.0
CHUNK = 64
EPS = 1e-6
P_LX, P_LG, P_Q, P_K, P_V, P_Z, P_GATE, P_COLS = 0, 512, 1024, 1536, 2048, 2560, 3072, 3200
N_GATE = 16
LANES = 128
VMEM_LIMIT = 56 * 1024 * 1024
MM_VMEM_BYTES = 40 * 1024 * 1024

ADAM_LR, ADAM_B1, ADAM_B2, ADAM_EPS, ADAM_WD, ADAM_STEP = 0.001, 0.9, 0.999, 1e-08, 0.01, 10


def _cparams(sem):
    return pltpu.CompilerParams(dimension_semantics=sem, vmem_limit_bytes=VMEM_LIMIT)


def _row_tile(m, want=512):
    for t in range(min(want, m) // 16 * 16, 0, -16):
        if m % t == 0:
            return t
    return m


def _lane_divisors(n):
    out = [d for d in range(n, 0, -LANES) if d % LANES == 0 and n % d == 0] if n % LANES == 0 else []
    return out or [n]


def _mm_tiles(m, n, a_row_bytes, k_total, with_res):
    best = None
    for tm in (1024, 512, 256, 128):
        if m % tm:
            continue
        for tn in _lane_divisors(n):
            need = 2 * (tm * a_row_bytes + k_total * tn * 2 + tm * tn * 4 * (2 if with_res else 1))
            if need <= MM_VMEM_BYTES and (best is None or tm * tn > best[0] * best[1]):
                best = (tm, tn)
    return best if best is not None else (_row_tile(m, 128), _lane_divisors(n)[-1])


def _mm_tn_tiles(m, k, n):
    for tn in _lane_divisors(n):
        for tm in (1024, 512, 256, 128):
            if m % tm == 0 and 2 * (k * tn * 4 + tm * k * 2 + tm * tn * 2) <= MM_VMEM_BYTES:
                return tm, tn
    return _row_tile(m, 128), _lane_divisors(n)[-1]


def _bdot(a, b):
    return jnp.dot(a.astype(BF16), b.astype(BF16), preferred_element_type=F32)


def _bdot_nt(a, b):
    return lax.dot_general(a.astype(BF16), b.astype(BF16), (((1,), (1,)), ((), ())), preferred_element_type=F32)


def _bdot_tn(a, b):
    return lax.dot_general(a.astype(BF16), b.astype(BF16), (((0,), (0,)), ((), ())), preferred_element_type=F32)


def _rms(x, g):
    return x * lax.rsqrt(jnp.mean(x * x, axis=-1, keepdims=True) + EPS) * g


def _gelu(x):
    return 0.5 * x * (1.0 + jnp.tanh(0.7978845608028654 * (x + 0.044715 * x * x * x)))


def _sigmoid(x):
    return 1.0 / (1.0 + jnp.exp(-x))


def _silu(x):
    return x * _sigmoid(x)


def _softplus(x):
    return jnp.maximum(x, 0.0) + jnp.log(1.0 + jnp.exp(-jnp.abs(x)))


def _matmul(terms, name, res=None, nt=False):
    norm = lambda op, axis: op if isinstance(op, tuple) else (op, op.shape[axis], 0)
    a_ops = [norm(a, 1) for a, _ in terms]
    b_ops = [norm(b, 1 if nt else 0) for _, b in terms]
    m = a_ops[0][0].shape[0]
    n = b_ops[0][0].shape[0 if nt else 1]
    a_row_bytes = sum(kw * a.dtype.itemsize for a, kw, _ in a_ops)
    tm, tn = _mm_tiles(m, n, a_row_bytes, sum(kw for _, kw, _ in b_ops), res is not None)
    na = len(terms)
    dot = _bdot_nt if nt else _bdot

    def body(*refs):
        a_refs, b_refs = refs[:na], refs[na:2 * na]
        acc = dot(a_refs[0][...], b_refs[0][...])
        for a_ref, b_ref in zip(a_refs[1:], b_refs[1:]):
            acc = acc + dot(a_ref[...], b_ref[...])
        if res is not None:
            acc = acc + refs[2 * na][...]
        refs[-1][...] = acc

    in_specs = [pl.BlockSpec((tm, kw), functools.partial(lambda i, j, kb: (i, kb), kb=kb)) for _, kw, kb in a_ops]
    if nt:
        in_specs += [pl.BlockSpec((tn, kw), functools.partial(lambda i, j, kb: (j, kb), kb=kb)) for _, kw, kb in b_ops]
    else:
        in_specs += [pl.BlockSpec((kw, tn), functools.partial(lambda i, j, kb: (kb, j), kb=kb)) for _, kw, kb in b_ops]
    args = [a for a, _, _ in a_ops] + [b for b, _, _ in b_ops]
    if res is not None:
        in_specs.append(pl.BlockSpec((tm, tn), lambda i, j: (i, j)))
        args.append(res)
    return pl.pallas_call(
        body, name=name, grid=(m // tm, n // tn), in_specs=in_specs,
        out_specs=pl.BlockSpec((tm, tn), lambda i, j: (i, j)),
        out_shape=jax.ShapeDtypeStruct((m, n), F32),
        compiler_params=_cparams(("parallel", "parallel")),
    )(*args)


def _norm_matmul(x, g, b_list, name):
    m, k = x.shape
    b_bytes = sum(2 * b.shape[0] * b.shape[1] * 2 for b in b_list)
    row_bytes = 2 * (k * 4 + k * 2 + sum(b.shape[1] * 4 for b in b_list))
    tm = next((t for t in (1024, 512, 256, 128) if m % t == 0 and b_bytes + t * row_bytes <= MM_VMEM_BYTES), _row_tile(m, 128))
    nb = len(b_list)

    def body(x_ref, g_ref, *refs):
        b_refs, h_ref, o_refs = refs[:nb], refs[nb], refs[nb + 1:]
        h = _rms(x_ref[...], g_ref[...]).astype(BF16)
        h_ref[...] = h
        for b_ref, o_ref in zip(b_refs, o_refs):
            o_ref[...] = jnp.dot(h, b_ref[...], preferred_element_type=F32)

    row = lambda width: pl.BlockSpec((tm, width), lambda i: (i, 0))
    return pl.pallas_call(
        body, name=name, grid=(m // tm,),
        in_specs=[row(k), pl.BlockSpec((1, k), lambda i: (0, 0))] + [pl.BlockSpec(b.shape, lambda i: (0, 0)) for b in b_list],
        out_specs=[row(k)] + [row(b.shape[1]) for b in b_list],
        out_shape=[jax.ShapeDtypeStruct((m, k), BF16)] + [jax.ShapeDtypeStruct((m, b.shape[1]), F32) for b in b_list],
        compiler_params=_cparams(("parallel",)),
    )(x, g, *b_list)


def _matmul_norm_bwd(terms, x, g, dres, name):
    norm = lambda op: op if isinstance(op, tuple) else (op, op.shape[1], 0)
    a_ops = [norm(a) for a, _ in terms]
    b_ops = [norm(b) for _, b in terms]
    m, n = x.shape
    b_bytes = sum(2 * n * kw * 2 for _, kw, _ in b_ops)
    row_bytes = 2 * (sum(kw * 2 for _, kw, _ in a_ops) + n * (4 + 4 + 4 + 2)) + n * 8
    tm = next((t for t in (1024, 512, 256, 128) if m % t == 0 and b_bytes + t * row_bytes <= MM_VMEM_BYTES), _row_tile(m, 128))
    na = len(terms)

    def body(*refs):
        a_refs, b_refs = refs[:na], refs[na:2 * na]
        x_ref, g_ref, dres_ref, dx_ref, dxb_ref, dg_ref = refs[2 * na:]
        dh = _bdot_nt(a_refs[0][...], b_refs[0][...])
        for a_ref, b_ref in zip(a_refs[1:], b_refs[1:]):
            dh = dh + _bdot_nt(a_ref[...], b_ref[...])
        _, vjp = jax.vjp(_rms, x_ref[...], g_ref[...])
        dx, dg = vjp(dh)
        dx = dx + dres_ref[...]
        dx_ref[...] = dx
        dxb_ref[...] = dx.astype(BF16)

        @pl.when(pl.program_id(0) == 0)
        def _():
            dg_ref[...] = jnp.zeros_like(dg_ref)

        dg_ref[...] += dg

    row = pl.BlockSpec((tm, n), lambda i: (i, 0))
    vec = pl.BlockSpec((1, n), lambda i: (0, 0))
    in_specs = [pl.BlockSpec((tm, kw), functools.partial(lambda i, kb: (i, kb), kb=kb)) for _, kw, kb in a_ops]
    in_specs += [pl.BlockSpec((n, kw), functools.partial(lambda i, kb: (0, kb), kb=kb)) for _, kw, kb in b_ops]
    return pl.pallas_call(
        body, name=name, grid=(m // tm,), in_specs=in_specs + [row, vec, row], out_specs=[row, row, vec],
        out_shape=[jax.ShapeDtypeStruct((m, n), F32), jax.ShapeDtypeStruct((m, n), BF16), jax.ShapeDtypeStruct((1, n), F32)],
        compiler_params=_cparams(("arbitrary",)),
    )(*[a for a, _, _ in a_ops], *[b for b, _, _ in b_ops], x, g, dres)


def _matmul_tn_cols(a, b_list, name):
    m, k = a.shape
    n = sum(b.shape[1] for b in b_list)
    tm = next((t for t in (1024, 512, 256, 128) if m % t == 0 and 2 * (k * n * 4 + t * k * 2 + t * n * 2) <= MM_VMEM_BYTES),
              _row_tile(m, 128))

    def body(a_ref, *refs):
        b_refs, o_ref = refs[:-1], refs[-1]

        @pl.when(pl.program_id(0) == 0)
        def _():
            o_ref[...] = jnp.zeros_like(o_ref)

        off = 0
        for b_ref in b_refs:
            w = b_ref.shape[1]
            o_ref[:, off:off + w] += _bdot_tn(a_ref[...], b_ref[...])
            off += w

    return pl.pallas_call(
        body, name=name, grid=(m // tm,),
        in_specs=[pl.BlockSpec((tm, k), lambda i: (i, 0))] + [pl.BlockSpec((tm, b.shape[1]), lambda i: (i, 0)) for b in b_list],
        out_specs=pl.BlockSpec((k, n), lambda i: (0, 0)), out_shape=jax.ShapeDtypeStruct((k, n), F32),
        compiler_params=_cparams(("arbitrary",)),
    )(a, *b_list)


def _matmul_tn(a, b, name):
    m, k = a.shape
    n = b.shape[1]
    tm, tn = _mm_tn_tiles(m, k, n)

    def body(a_ref, b_ref, o_ref):
        @pl.when(pl.program_id(1) == 0)
        def _():
            o_ref[...] = jnp.zeros_like(o_ref)

        o_ref[...] += _bdot_tn(a_ref[...], b_ref[...])

    return pl.pallas_call(
        body, name=name, grid=(n // tn, m // tm),
        in_specs=[pl.BlockSpec((tm, k), lambda j, i: (i, 0)), pl.BlockSpec((tm, tn), lambda j, i: (i, j))],
        out_specs=pl.BlockSpec((k, tn), lambda j, i: (0, j)),
        out_shape=jax.ShapeDtypeStruct((k, n), F32),
        compiler_params=_cparams(("parallel", "arbitrary")),
    )(a, b)


def _shift_down(x, k):
    row = lax.broadcasted_iota(jnp.int32, x.shape, 0)
    return jnp.where(row >= k, pltpu.roll(x, k, 0), 0.0)


def _shift_up(x, k):
    s = x.shape[0]
    row = lax.broadcasted_iota(jnp.int32, x.shape, 0)
    return jnp.where(row < s - k, pltpu.roll(x, s - k, 0), 0.0)


def _conv_taps(x, ntaps, left):
    out = []
    for j in range(ntaps):
        off = j - left
        out.append(_shift_down(x, -off) if off < 0 else (_shift_up(x, off) if off > 0 else x))
    return out


def _conv_fwd(x, w, left):
    taps = _conv_taps(x, w.shape[0], left)
    acc = taps[0] * w[0:1, :]
    for j in range(1, w.shape[0]):
        acc = acc + taps[j] * w[j:j + 1, :]
    return acc


def _conv_bwd(x, w, left, dout):
    ntaps = w.shape[0]
    dx = None
    for j in range(ntaps):
        off = j - left
        sh = _shift_up(dout, -off) if off < 0 else (_shift_down(dout, off) if off > 0 else dout)
        term = sh * w[j:j + 1, :]
        dx = term if dx is None else dx + term
    taps = _conv_taps(x, ntaps, left)
    dw = jnp.concatenate([jnp.sum(dout * tp, axis=0, keepdims=True) for tp in taps], axis=0)
    return dx, dw


def _scan(a, b, reverse):
    s = a.shape[0]
    d = 1
    sh = _shift_up if reverse else _shift_down
    while d < s:
        b = a * sh(b, d) + b
        a = a * sh(a, d)
        d *= 2
    return b


def _dn_gates_fn(pre, alog, dtb):
    lane = lax.broadcasted_iota(jnp.int32, pre.shape, 1)
    beta = _sigmoid(pre)
    g = -jnp.exp(alog) * _softplus(pre + dtb)
    return jnp.where(lane < N_GATE // 2, beta, jnp.where(lane < N_GATE, g, 0.0))


def _dn_gates_fwd(proj, alog, dtb, name):
    t = proj.shape[0]
    tm = _row_tile(t)
    cb = P_GATE // LANES

    def body(p_ref, a_ref, d_ref, o_ref):
        o_ref[...] = _dn_gates_fn(p_ref[...], a_ref[...], d_ref[...])

    return pl.pallas_call(
        body, name=name, grid=(t // tm,),
        in_specs=[pl.BlockSpec((tm, LANES), lambda i: (i, cb)), pl.BlockSpec((1, LANES), lambda i: (0, 0)),
                  pl.BlockSpec((1, LANES), lambda i: (0, 0))],
        out_specs=pl.BlockSpec((tm, LANES), lambda i: (i, 0)),
        out_shape=jax.ShapeDtypeStruct((t, LANES), F32),
        compiler_params=_cparams(("parallel",)),
    )(proj, alog, dtb)


def _dn_gates_bwd(proj, alog, dtb, dgb_f, dgb_b, name):
    t = proj.shape[0]
    tm = _row_tile(t)
    cb = P_GATE // LANES

    def body(p_ref, a_ref, d_ref, g1_ref, g2_ref, dp_ref, da_ref, dd_ref):
        _, vjp = jax.vjp(_dn_gates_fn, p_ref[...], a_ref[...], d_ref[...])
        dp, da, dd = vjp(g1_ref[...] + g2_ref[...])
        dp_ref[...] = dp.astype(dp_ref.dtype)

        @pl.when(pl.program_id(0) == 0)
        def _():
            da_ref[...] = jnp.zeros_like(da_ref)
            dd_ref[...] = jnp.zeros_like(dd_ref)

        da_ref[...] += da
        dd_ref[...] += dd

    row = pl.BlockSpec((tm, LANES), lambda i: (i, 0))
    vec = pl.BlockSpec((1, LANES), lambda i: (0, 0))
    return pl.pallas_call(
        body, name=name, grid=(t // tm,),
        in_specs=[pl.BlockSpec((tm, LANES), lambda i: (i, cb)), vec, vec, row, row],
        out_specs=[row, vec, vec],
        out_shape=[jax.ShapeDtypeStruct((t, LANES), BF16), jax.ShapeDtypeStruct((1, LANES), F32),
                   jax.ShapeDtypeStruct((1, LANES), F32)],
        compiler_params=_cparams(("arbitrary",)),
    )(proj, alog, dtb, dgb_f, dgb_b)


def _dn_prep_fn(x, w, is_qk):
    act = _silu(_conv_fwd(x, w, 2))
    nrm = act * lax.rsqrt(jnp.sum(act * act, axis=-1, keepdims=True) + EPS)
    return jnp.where(is_qk, nrm, act)


def _dn_prep_fwd(proj3, conv_w, name):
    bsz, s, _ = proj3.shape
    nblk = 3 * DN_WIDTH // LANES
    cb = P_Q // LANES

    def body(x_ref, w_ref, o_ref):
        o_ref[0] = _dn_prep_fn(x_ref[0], w_ref[...], pl.program_id(1) < 2 * DN_HEADS)

    return pl.pallas_call(
        body, name=name, grid=(bsz, nblk),
        in_specs=[pl.BlockSpec((1, s, LANES), lambda b, j: (b, 0, cb + j)), pl.BlockSpec((4, LANES), lambda b, j: (0, j))],
        out_specs=pl.BlockSpec((1, s, LANES), lambda b, j: (b, 0, j)),
        out_shape=jax.ShapeDtypeStruct((bsz, s, 3 * DN_WIDTH), F32),
        compiler_params=_cparams(("parallel", "parallel")),
    )(proj3, conv_w)


def _dn_prep_bwd(proj3, conv_w, dqkv_f, dqkv_b, name):
    bsz, s, _ = proj3.shape
    nblk = 3 * DN_WIDTH // LANES
    cb = P_Q // LANES

    def body(x_ref, w_ref, d1_ref, d2_ref, dx_ref, dw_ref):
        x, w, d = x_ref[0], w_ref[...], d1_ref[0] + d2_ref[0]
        is_qk = pl.program_id(0) < 2 * DN_HEADS
        pre = _conv_fwd(x, w, 2)

        def post(pre):
            act = _silu(pre)
            nrm = act * lax.rsqrt(jnp.sum(act * act, axis=-1, keepdims=True) + EPS)
            return jnp.where(is_qk, nrm, act)

        _, vjp = jax.vjp(post, pre)
        (dpre,) = vjp(d)
        dx, dw = _conv_bwd(x, w, 2, dpre)
        dx_ref[0] = dx.astype(dx_ref.dtype)

        @pl.when(pl.program_id(1) == 0)
        def _():
            dw_ref[...] = jnp.zeros_like(dw_ref)

        dw_ref[...] += dw

    col = pl.BlockSpec((1, s, LANES), lambda j, b: (b, 0, j))
    return pl.pallas_call(
        body, name=name, grid=(nblk, bsz),
        in_specs=[pl.BlockSpec((1, s, LANES), lambda j, b: (b, 0, cb + j)), pl.BlockSpec((4, LANES), lambda j, b: (0, j)), col, col],
        out_specs=[col, pl.BlockSpec((4, LANES), lambda j, b: (0, j))],
        out_shape=[jax.ShapeDtypeStruct((bsz, s, 3 * DN_WIDTH), BF16), jax.ShapeDtypeStruct((4, 3 * DN_WIDTH), F32)],
        compiler_params=_cparams(("parallel", "arbitrary")),
    )(proj3, conv_w, dqkv_f, dqkv_b)


def _parts(x, n):
    out = []
    for _ in range(n):
        bits = lax.bitcast_convert_type(x, jnp.uint32) & jnp.uint32(0xFFFF0000)
        t = lax.bitcast_convert_type(bits, F32)
        out.append(t.astype(BF16))
        x = x - t
    return out


def _dg(x, y, cx, cy):
    return lax.dot_general(x, y, (((cx + 1,), (cy + 1,)), ((0,), (0,))), preferred_element_type=F32)


def _bmm(a, b):
    return _dg(a.astype(BF16), b.astype(BF16), 1, 0)


def _bmm_nt(a, b):
    return _dg(a.astype(BF16), b.astype(BF16), 1, 1)


def _bmm_tn(a, b):
    return _dg(a.astype(BF16), b.astype(BF16), 0, 0)


def _dot3_raw(a, b, ca, cb):
    a_hi, a_lo = _parts(a, 2)
    b_hi, b_lo = _parts(b, 2)
    return _dg(a_hi, b_hi, ca, cb) + (_dg(a_hi, b_lo, ca, cb) + _dg(a_lo, b_hi, ca, cb))


@jax.custom_vjp
def _sum_left(m, x):
    return sum(_dg(m, pt, 1, 0) for pt in _parts(x, 3))


def _sum_left_fwd(m, x):
    return _sum_left(m, x), m


def _sum_left_bwd(m, ct):
    return jnp.zeros_like(m), sum(_dg(m, pt, 0, 0) for pt in _parts(ct, 2))


_sum_left.defvjp(_sum_left_fwd, _sum_left_bwd)


@jax.custom_vjp
def _sum_right(x, m):
    return sum(_dg(pt, m, 0, 0) for pt in _parts(x, 3))


def _sum_right_fwd(x, m):
    return _sum_right(x, m), m


def _sum_right_bwd(m, ct):
    return sum(_dg(m, pt, 1, 1) for pt in _parts(ct, 2)), jnp.zeros_like(m)


_sum_right.defvjp(_sum_right_fwd, _sum_right_bwd)


def _unit_tri_inverse_raw(a):
    nu, c, _ = a.shape
    row = lax.broadcasted_iota(jnp.int32, (nu, c, c), 1)
    col = lax.broadcasted_iota(jnp.int32, (nu, c, c), 2)
    same = lambda n: (row // n) == (col // n)
    d = jnp.where(same(8), a, 0.0)
    tinv = jnp.where(row == col, 1.0, 0.0) - d
    pw = _dot3_raw(d, d, 1, 0)
    tinv = tinv + _dot3_raw(tinv, pw, 1, 0)
    pw = _dot3_raw(pw, pw, 1, 0)
    tinv = tinv + _dot3_raw(tinv, pw, 1, 0)
    n = 8
    while n < c:
        e = jnp.where(same(2 * n) & jnp.logical_not(same(n)), a, 0.0)
        tinv = tinv - _bmm(tinv, _bmm(e, tinv))
        n *= 2
    return tinv


@jax.custom_vjp
def _unit_tri_inverse(a):
    return _unit_tri_inverse_raw(a)


def _unit_tri_inverse_fwd(a):
    tinv = _unit_tri_inverse_raw(a)
    return tinv, tinv


def _unit_tri_inverse_bwd(tinv, ct):
    return (-_bmm_nt(_bmm_tn(tinv, ct), tinv),)


_unit_tri_inverse.defvjp(_unit_tri_inverse_fwd, _unit_tri_inverse_bwd)


def _dn_intra(q, k, v, g, beta, rev):
    nu, c, _ = q.shape
    row = lax.broadcasted_iota(jnp.int32, (nu, c, c), 1)
    col = lax.broadcasted_iota(jnp.int32, (nu, c, c), 2)
    incl = (row <= col) if rev else (row >= col)
    strict = (row < col) if rev else (row > col)
    ones_incl = jnp.where(incl, 1.0, 0.0).astype(BF16)
    ones_tr = jnp.where((row >= col) if rev else (row <= col), 1.0, 0.0).astype(BF16)
    gbc = jnp.broadcast_to(g, (nu, c, c))
    gc = _sum_left(ones_incl, gbc)
    gr = _sum_right(gbc, ones_tr)
    gcum = gc[:, :, 0:1]
    decay = jnp.where(incl, jnp.exp(jnp.where(incl, gc - gr, 0.0)), 0.0)
    qs = q * (DN_HEAD_DIM ** -0.5)
    kb = k * beta
    a = jnp.where(strict, _bmm_nt(kb, k) * decay, 0.0)
    tinv = _unit_tri_inverse(a)
    egc = jnp.exp(gcum)
    u = _bmm(tinv, v * beta)
    w = _bmm(tinv, kb * egc)
    attn = _bmm_nt(qs, k) * decay
    glast = gcum[:, 0:1, :] if rev else gcum[:, c - 1:c, :]
    q_dec = qs * egc
    k_dec = k * jnp.exp(glast - gcum)
    cdec = jnp.broadcast_to(jnp.exp(glast), (nu, 1, LANES))
    return u, w, q_dec, k_dec, attn, cdec


def _dn_rec(u, w, q_dec, k_dec, attn, cdec, state):
    v_new = u - _bmm(w, state)
    o = _bmm(q_dec, state) + _bmm(attn, v_new)
    return o, state * cdec + _bmm_tn(k_dec, v_new)


DN_INTRA_TOKENS = 512
DN_REC_TOKENS = 512
DN_REC_BWD_TOKENS = 256
DN_REC_EXAMPLES = 4


def _dn_gate_lanes(rev, h):
    lb = (DN_HEADS if rev else 0) + h
    return lb, N_GATE // 2 + lb


def _dn_intra_shapes(bsz, s):
    n = s // CHUNK
    return [jax.ShapeDtypeStruct((bsz, s, DN_WIDTH), F32), jax.ShapeDtypeStruct((bsz, s, DN_WIDTH), BF16),
            jax.ShapeDtypeStruct((bsz, s, DN_WIDTH), BF16), jax.ShapeDtypeStruct((bsz, s, DN_WIDTH), BF16),
            jax.ShapeDtypeStruct((bsz, n, DN_HEADS, CHUNK, CHUNK), BF16), jax.ShapeDtypeStruct((bsz, n, DN_HEADS, 1, LANES), F32)]


def _dn_intra_specs(tb, ix):
    nc = tb // CHUNK
    ix5 = lambda b, j: ix(b, j) + (0, 0)
    row = pl.BlockSpec((1, tb, DN_WIDTH), ix)
    return [row, row, row, row, pl.BlockSpec((1, nc, DN_HEADS, CHUNK, CHUNK), ix5), pl.BlockSpec((1, nc, DN_HEADS, 1, LANES), ix5)]


def _dn_units(nc):
    return [(ci, h) for ci in range(nc) for h in range(DN_HEADS)]


def _dn_load_units(qkv_ref, gb_ref, nc, rev):
    qs, ks, vs, gs, bs = [], [], [], [], []
    for ci, h in _dn_units(nc):
        rows = slice(ci * CHUNK, (ci + 1) * CHUNK)
        lb, lg = _dn_gate_lanes(rev, h)
        qs.append(qkv_ref[0, rows, h * LANES:(h + 1) * LANES])
        ks.append(qkv_ref[0, rows, DN_WIDTH + h * LANES:DN_WIDTH + (h + 1) * LANES])
        vs.append(qkv_ref[0, rows, 2 * DN_WIDTH + h * LANES:2 * DN_WIDTH + (h + 1) * LANES])
        gs.append(gb_ref[0, rows, lg:lg + 1])
        bs.append(gb_ref[0, rows, lb:lb + 1])
    return jnp.stack(qs), jnp.stack(ks), jnp.stack(vs), jnp.stack(gs), jnp.stack(bs)


def _dn_intra_fwd(qkv, gb, rev, name):
    bsz, s, _ = qkv.shape
    tb = min(DN_INTRA_TOKENS, s)
    nc = tb // CHUNK

    def body(qkv_ref, gb_ref, u_ref, w_ref, qd_ref, kd_ref, at_ref, cd_ref):
        q, k, v, g, beta = _dn_load_units(qkv_ref, gb_ref, nc, rev)
        u, w, qd, kd, at, cd = _dn_intra(q, k, v, g, beta, rev)
        for i, (ci, h) in enumerate(_dn_units(nc)):
            rows = slice(ci * CHUNK, (ci + 1) * CHUNK)
            cols = slice(h * LANES, (h + 1) * LANES)
            u_ref[0, rows, cols] = u[i]
            w_ref[0, rows, cols] = w[i].astype(BF16)
            qd_ref[0, rows, cols] = qd[i].astype(BF16)
            kd_ref[0, rows, cols] = kd[i].astype(BF16)
            at_ref[0, ci, h] = at[i].astype(BF16)
            cd_ref[0, ci, h] = cd[i]

    ix = lambda b, j: (b, j, 0)
    return pl.pallas_call(
        body, name=name, grid=(bsz, s // tb),
        in_specs=[pl.BlockSpec((1, tb, 3 * DN_WIDTH), ix), pl.BlockSpec((1, tb, LANES), ix)],
        out_specs=_dn_intra_specs(tb, ix), out_shape=_dn_intra_shapes(bsz, s),
        compiler_params=_cparams(("parallel", "parallel")),
    )(qkv, gb)


def _dn_intra_bwd(qkv, gb, cts, rev, name):
    bsz, s, _ = qkv.shape
    tb = min(DN_INTRA_TOKENS, s)
    nc = tb // CHUNK

    def body(qkv_ref, gb_ref, du_ref, dw_ref, dqd_ref, dkd_ref, dat_ref, dcd_ref, dqkv_ref, dgb_ref):
        units = _dn_units(nc)
        q, k, v, g, beta = _dn_load_units(qkv_ref, gb_ref, nc, rev)
        _, vjp = jax.vjp(functools.partial(_dn_intra, rev=rev), q, k, v, g, beta)
        tok = lambda ref: jnp.stack([ref[0, ci * CHUNK:(ci + 1) * CHUNK, h * LANES:(h + 1) * LANES] for ci, h in units])
        per = lambda ref: jnp.stack([ref[0, ci, h] for ci, h in units])
        dq, dk, dv, dg, dbeta = vjp((tok(du_ref), tok(dw_ref), tok(dqd_ref), tok(dkd_ref), per(dat_ref), per(dcd_ref)))
        lane = lax.broadcasted_iota(jnp.int32, (CHUNK, LANES), 1)
        for ci in range(nc):
            rows = slice(ci * CHUNK, (ci + 1) * CHUNK)
            dgates = jnp.zeros((CHUNK, LANES), F32)
            for h in range(DN_HEADS):
                i = units.index((ci, h))
                lb, lg = _dn_gate_lanes(rev, h)
                dqkv_ref[0, rows, h * LANES:(h + 1) * LANES] = dq[i]
                dqkv_ref[0, rows, DN_WIDTH + h * LANES:DN_WIDTH + (h + 1) * LANES] = dk[i]
                dqkv_ref[0, rows, 2 * DN_WIDTH + h * LANES:2 * DN_WIDTH + (h + 1) * LANES] = dv[i]
                dgates = dgates + jnp.where(lane == lb, dbeta[i], 0.0) + jnp.where(lane == lg, dg[i], 0.0)
            dgb_ref[0, rows, :] = dgates

    ix = lambda b, j: (b, j, 0)
    ix5 = lambda b, j: (b, j, 0, 0, 0)
    row = pl.BlockSpec((1, tb, DN_WIDTH), ix)
    return pl.pallas_call(
        body, name=name, grid=(bsz, s // tb),
        in_specs=[pl.BlockSpec((1, tb, 3 * DN_WIDTH), ix), pl.BlockSpec((1, tb, LANES), ix), row, row, row, row,
                  pl.BlockSpec((1, nc, DN_HEADS, CHUNK, CHUNK), ix5), pl.BlockSpec((1, nc, DN_HEADS, 1, LANES), ix5)],
        out_specs=[pl.BlockSpec((1, tb, 3 * DN_WIDTH), ix), pl.BlockSpec((1, tb, LANES), ix)],
        out_shape=[jax.ShapeDtypeStruct((bsz, s, 3 * DN_WIDTH), F32), jax.ShapeDtypeStruct((bsz, s, LANES), F32)],
        compiler_params=_cparams(("parallel", "parallel")),
    )(qkv, gb, *cts)


def _dn_rec_examples(bsz):
    return max(n for n in range(1, DN_REC_EXAMPLES + 1) if bsz % n == 0)


def _dn_rec_specs(nb, tb, ix):
    nc = tb // CHUNK
    ix5 = lambda b, j: ix(b, j) + (0, 0)
    row = pl.BlockSpec((nb, tb, DN_WIDTH), ix)
    return [row, row, row, row, pl.BlockSpec((nb, nc, DN_HEADS, CHUNK, CHUNK), ix5), pl.BlockSpec((nb, nc, DN_HEADS, 1, LANES), ix5)]


def _dn_rec_fwd(intra, rev, name):
    u = intra[0]
    bsz, s, _ = u.shape
    nb = _dn_rec_examples(bsz)
    tb = min(DN_REC_TOKENS, s)
    nt = s // tb
    nc = tb // CHUNK
    nh = nb * DN_HEADS

    def body(u_ref, w_ref, qd_ref, kd_ref, at_ref, cd_ref, o_ref, st_ref, state):
        @pl.when(pl.program_id(1) == 0)
        def _():
            state[...] = jnp.zeros_like(state)

        def step(ci, carry):
            cidx = (nc - 1 - ci) if rev else ci
            rows = pl.ds(pl.multiple_of(cidx * CHUNK, CHUNK), CHUNK)
            heads = lambda ref: jnp.stack([ref[e, rows, h * LANES:(h + 1) * LANES] for e in range(nb) for h in range(DN_HEADS)])
            per = lambda ref: jnp.concatenate([ref[e, cidx] for e in range(nb)])
            st = state[...]
            o, new_state = _dn_rec(heads(u_ref), heads(w_ref), heads(qd_ref), heads(kd_ref), per(at_ref), per(cd_ref), st)
            state[...] = new_state
            for e in range(nb):
                st_ref[e, cidx] = st[e * DN_HEADS:(e + 1) * DN_HEADS]
                o_ref[e, rows, :] = jnp.concatenate([o[e * DN_HEADS + h] for h in range(DN_HEADS)], axis=-1)
            return carry

        lax.fori_loop(0, nc, step, 0)

    ix = (lambda b, j: (b, nt - 1 - j, 0)) if rev else (lambda b, j: (b, j, 0))
    ix5 = lambda b, j: ix(b, j) + (0, 0)
    return pl.pallas_call(
        body, name=name, grid=(bsz // nb, nt), in_specs=_dn_rec_specs(nb, tb, ix),
        out_specs=[pl.BlockSpec((nb, tb, DN_WIDTH), ix), pl.BlockSpec((nb, nc, DN_HEADS, DN_HEAD_DIM, DN_HEAD_DIM), ix5)],
        out_shape=[jax.ShapeDtypeStruct((bsz, s, DN_WIDTH), F32),
                   jax.ShapeDtypeStruct((bsz, s // CHUNK, DN_HEADS, DN_HEAD_DIM, DN_HEAD_DIM), F32)],
        scratch_shapes=[pltpu.VMEM((nh, DN_HEAD_DIM, DN_HEAD_DIM), F32)],
        compiler_params=_cparams(("parallel", "arbitrary")),
    )(*intra)


def _dn_rec_bwd(intra, states, do, rev, name):
    u = intra[0]
    bsz, s, _ = u.shape
    nb = _dn_rec_examples(bsz)
    tb = min(DN_REC_BWD_TOKENS, s)
    nt = s // tb
    nc = tb // CHUNK
    nh = nb * DN_HEADS

    def body(u_ref, w_ref, qd_ref, kd_ref, at_ref, cd_ref, st_ref, do_ref,
             du_ref, dw_ref, dqd_ref, dkd_ref, dat_ref, dcd_ref, dstate):
        @pl.when(pl.program_id(1) == 0)
        def _():
            dstate[...] = jnp.zeros_like(dstate)

        def step(ci, carry):
            cidx = ci if rev else (nc - 1 - ci)
            rows = pl.ds(pl.multiple_of(cidx * CHUNK, CHUNK), CHUNK)
            heads = lambda ref: jnp.stack([ref[e, rows, h * LANES:(h + 1) * LANES] for e in range(nb) for h in range(DN_HEADS)])
            per = lambda ref: jnp.concatenate([ref[e, cidx] for e in range(nb)])
            args = (heads(u_ref), heads(w_ref).astype(F32), heads(qd_ref).astype(F32), heads(kd_ref).astype(F32),
                    per(at_ref).astype(F32), per(cd_ref), per(st_ref))
            _, vjp = jax.vjp(_dn_rec, *args)
            du, dw, dqd, dkd, dat, dcd, dst = vjp((heads(do_ref), dstate[...]))
            dstate[...] = dst
            for e in range(nb):
                hs = slice(e * DN_HEADS, (e + 1) * DN_HEADS)
                dat_ref[e, cidx] = dat[hs]
                dcd_ref[e, cidx] = dcd[hs]
                for ref, val in ((du_ref, du), (dw_ref, dw), (dqd_ref, dqd), (dkd_ref, dkd)):
                    ref[e, rows, :] = jnp.concatenate([val[e * DN_HEADS + h] for h in range(DN_HEADS)], axis=-1)
            return carry

        lax.fori_loop(0, nc, step, 0)

    ix = (lambda b, j: (b, j, 0)) if rev else (lambda b, j: (b, nt - 1 - j, 0))
    ix5 = lambda b, j: ix(b, j) + (0, 0)
    row = pl.BlockSpec((nb, tb, DN_WIDTH), ix)
    f32 = lambda sd: jax.ShapeDtypeStruct(sd.shape, F32)
    return pl.pallas_call(
        body, name=name, grid=(bsz // nb, nt),
        in_specs=_dn_rec_specs(nb, tb, ix) + [pl.BlockSpec((nb, nc, DN_HEADS, DN_HEAD_DIM, DN_HEAD_DIM), ix5), row],
        out_specs=_dn_rec_specs(nb, tb, ix), out_shape=[f32(sd) for sd in _dn_intra_shapes(bsz, s)],
        scratch_shapes=[pltpu.VMEM((nh, DN_HEAD_DIM, DN_HEAD_DIM), F32)],
        compiler_params=_cparams(("parallel", "arbitrary")),
    )(*intra, states, do)


def _lru_gate_fn(xc, wa, ba, wx, bx, lam):
    r = _sigmoid(_bdot(xc, wa) + ba)
    ig = _sigmoid(_bdot(xc, wx) + bx)
    log_a = -LRU_C * r * _softplus(-lam)
    a = jnp.exp(log_a)
    b = jnp.sqrt(-jnp.tanh(log_a) * (a * a + 1.0)) * (ig * xc)
    return a, b


def _lru_fwd(proj3, conv_w, conv_b, wa, ba, wx, bx, lam, name):
    bsz, s, _ = proj3.shape
    nblk = LRU_WIDTH // LANES

    def body(x_ref, cw_ref, cb_ref, wa_ref, ba_ref, wx_ref, bx_ref, lam_ref, hf_ref, hb_ref):
        xc = _conv_fwd(x_ref[0], cw_ref[...], 2) + cb_ref[...]
        for d, h_ref in ((0, hf_ref), (1, hb_ref)):
            a, b = _lru_gate_fn(xc, wa_ref[d, 0], ba_ref[d:d + 1, :], wx_ref[d, 0], bx_ref[d:d + 1, :], lam_ref[d:d + 1, :])
            h_ref[0] = _scan(a, b, reverse=(d == 1))

    col = pl.BlockSpec((1, s, LANES), lambda b, j: (b, 0, j))
    vec2 = pl.BlockSpec((2, LANES), lambda b, j: (0, j))
    wspec = pl.BlockSpec((2, 1, LANES, LANES), lambda b, j: (0, j, 0, 0))
    return pl.pallas_call(
        body, name=name, grid=(bsz, nblk),
        in_specs=[col, pl.BlockSpec((4, LANES), lambda b, j: (0, j)), pl.BlockSpec((1, LANES), lambda b, j: (0, j)),
                  wspec, vec2, wspec, vec2, vec2],
        out_specs=[col, col],
        out_shape=[jax.ShapeDtypeStruct((bsz, s, LRU_WIDTH), F32)] * 2,
        compiler_params=_cparams(("parallel", "parallel")),
    )(proj3, conv_w, conv_b, wa, ba, wx, bx, lam)


def _lru_bwd(proj3, conv_w, conv_b, wa, ba, wx, bx, lam, hf, hb, dh, name):
    bsz, s, _ = proj3.shape
    nblk = LRU_WIDTH // LANES

    def body(x_ref, cw_ref, cb_ref, wa_ref, ba_ref, wx_ref, bx_ref, lam_ref, hf_ref, hb_ref, dh_ref,
             dx_ref, dcw_ref, dcb_ref, dwa_ref, dba_ref, dwx_ref, dbx_ref, dlam_ref):
        @pl.when(pl.program_id(1) == 0)
        def _():
            for r in (dcw_ref, dcb_ref, dwa_ref, dba_ref, dwx_ref, dbx_ref, dlam_ref):
                r[...] = jnp.zeros_like(r)

        x, cw = x_ref[0], cw_ref[...]
        xc = _conv_fwd(x, cw, 2) + cb_ref[...]
        dhv = dh_ref[0]
        dxc = jnp.zeros_like(xc)
        for d, h_ref in ((0, hf_ref), (1, hb_ref)):
            rev = d == 1
            args = (xc, wa_ref[d, 0].astype(F32), ba_ref[d:d + 1, :], wx_ref[d, 0].astype(F32), bx_ref[d:d + 1, :],
                    lam_ref[d:d + 1, :])
            (a, _), vjp = jax.vjp(_lru_gate_fn, *args)
            h = h_ref[0]
            a_next = _shift_down(a, 1) if rev else _shift_up(a, 1)
            lam_adj = _scan(a_next, dhv, reverse=not rev)
            h_prev = _shift_up(h, 1) if rev else _shift_down(h, 1)
            dxc_d, dwa, dba, dwx, dbx, dlam = vjp((lam_adj * h_prev, lam_adj))
            dxc = dxc + dxc_d
            dwa_ref[d, 0] += dwa
            dwx_ref[d, 0] += dwx
            dba_ref[d:d + 1, :] += dba
            dbx_ref[d:d + 1, :] += dbx
            dlam_ref[d:d + 1, :] += dlam
        dx, dcw = _conv_bwd(x, cw, 2, dxc)
        dx_ref[0] = dx.astype(dx_ref.dtype)
        dcw_ref[...] += dcw
        dcb_ref[...] += jnp.sum(dxc, axis=0, keepdims=True)

    col = pl.BlockSpec((1, s, LANES), lambda j, b: (b, 0, j))
    vec1 = pl.BlockSpec((1, LANES), lambda j, b: (0, j))
    vec2 = pl.BlockSpec((2, LANES), lambda j, b: (0, j))
    vec4 = pl.BlockSpec((4, LANES), lambda j, b: (0, j))
    wspec = pl.BlockSpec((2, 1, LANES, LANES), lambda j, b: (0, j, 0, 0))
    wshape = jax.ShapeDtypeStruct((2, nblk, LANES, LANES), F32)
    v2shape = jax.ShapeDtypeStruct((2, LRU_WIDTH), F32)
    return pl.pallas_call(
        body, name=name, grid=(nblk, bsz),
        in_specs=[col, vec4, vec1, wspec, vec2, wspec, vec2, vec2, col, col, col],
        out_specs=[col, vec4, vec1, wspec, vec2, wspec, vec2, vec2],
        out_shape=[jax.ShapeDtypeStruct((bsz, s, LRU_WIDTH), BF16), jax.ShapeDtypeStruct((4, LRU_WIDTH), F32),
                   jax.ShapeDtypeStruct((1, LRU_WIDTH), F32), wshape, v2shape, wshape, v2shape, v2shape],
        compiler_params=_cparams(("parallel", "arbitrary")),
    )(proj3, conv_w, conv_b, wa, ba, wx, bx, lam, hf, hb, dh)


def _mix_fn(o_f, o_b, z, lg, hf, hb, dn_g, lru_g):
    osum = o_f + o_b
    heads = []
    for h in range(DN_HEADS):
        sl = slice(h * LANES, (h + 1) * LANES)
        heads.append(_rms(osum[:, sl], dn_g) * _silu(z[:, sl]))
    lru = _rms(_gelu(lg) * (hf + hb), lru_g)
    return jnp.concatenate(heads + [lru], axis=-1)


def _mix_specs(tm):
    w = DN_WIDTH
    row = pl.BlockSpec((tm, w), lambda i: (i, 0))
    z = pl.BlockSpec((tm, w), lambda i: (i, P_Z // w))
    lg = pl.BlockSpec((tm, w), lambda i: (i, P_LG // w))
    return [row, row, z, lg, row, row, pl.BlockSpec((1, LANES), lambda i: (0, 0)), pl.BlockSpec((1, w), lambda i: (0, 0))]


def _mix_fwd(o_f, o_b, proj, hf, hb, dn_g, lru_g, name):
    t = proj.shape[0]
    tm = _row_tile(t)

    def body(of_ref, ob_ref, z_ref, lg_ref, hf_ref, hb_ref, dg_ref, lgn_ref, o_ref):
        o_ref[...] = _mix_fn(of_ref[...], ob_ref[...], z_ref[...], lg_ref[...], hf_ref[...], hb_ref[...],
                             dg_ref[...], lgn_ref[...]).astype(o_ref.dtype)

    return pl.pallas_call(
        body, name=name, grid=(t // tm,), in_specs=_mix_specs(tm),
        out_specs=pl.BlockSpec((tm, D_MODEL), lambda i: (i, 0)),
        out_shape=jax.ShapeDtypeStruct((t, D_MODEL), BF16),
        compiler_params=_cparams(("parallel",)),
    )(o_f, o_b, proj, proj, hf, hb, dn_g, lru_g)


def _mix_bwd(o_f, o_b, proj, hf, hb, dn_g, lru_g, dmix, name):
    t = proj.shape[0]
    tm = _row_tile(t)

    def body(of_ref, ob_ref, z_ref, lg_ref, hf_ref, hb_ref, dg_ref, lgn_ref, dm_ref,
             do_ref, dz_ref, dlg_ref, dh_ref, ddg_ref, dlgn_ref):
        _, vjp = jax.vjp(_mix_fn, of_ref[...], ob_ref[...], z_ref[...], lg_ref[...], hf_ref[...], hb_ref[...],
                         dg_ref[...], lgn_ref[...])
        do, _, dz, dlg, dh, _, ddg, dlgn = vjp(dm_ref[...])
        do_ref[...] = do
        dz_ref[...] = dz.astype(dz_ref.dtype)
        dlg_ref[...] = dlg.astype(dlg_ref.dtype)
        dh_ref[...] = dh

        @pl.when(pl.program_id(0) == 0)
        def _():
            ddg_ref[...] = jnp.zeros_like(ddg_ref)
            dlgn_ref[...] = jnp.zeros_like(dlgn_ref)

        ddg_ref[...] += ddg
        dlgn_ref[...] += dlgn

    row = pl.BlockSpec((tm, DN_WIDTH), lambda i: (i, 0))
    return pl.pallas_call(
        body, name=name, grid=(t // tm,),
        in_specs=_mix_specs(tm) + [pl.BlockSpec((tm, D_MODEL), lambda i: (i, 0))],
        out_specs=[row, row, row, row, pl.BlockSpec((1, LANES), lambda i: (0, 0)), pl.BlockSpec((1, DN_WIDTH), lambda i: (0, 0))],
        out_shape=[jax.ShapeDtypeStruct((t, DN_WIDTH), dt) for dt in (F32, BF16, BF16, F32)]
        + [jax.ShapeDtypeStruct((1, LANES), F32), jax.ShapeDtypeStruct((1, DN_WIDTH), F32)],
        compiler_params=_cparams(("arbitrary",)),
    )(o_f, o_b, proj, proj, hf, hb, dn_g, lru_g, dmix)


def _ffn_act_fwd(g3, u3, conv_w, conv_b, name):
    bsz, s, f = g3.shape
    nblk = f // LANES

    def body(g_ref, u_ref, w_ref, b_ref, o_ref):
        gate = _conv_fwd(g_ref[0], w_ref[...], 1) + b_ref[...]
        o_ref[0] = (_gelu(gate) * u_ref[0]).astype(o_ref.dtype)

    col = pl.BlockSpec((1, s, LANES), lambda b, j: (b, 0, j))
    return pl.pallas_call(
        body, name=name, grid=(bsz, nblk),
        in_specs=[col, col, pl.BlockSpec((3, LANES), lambda b, j: (0, j)), pl.BlockSpec((1, LANES), lambda b, j: (0, j))],
        out_specs=col, out_shape=jax.ShapeDtypeStruct((bsz, s, f), BF16),
        compiler_params=_cparams(("parallel", "parallel")),
    )(g3, u3, conv_w, conv_b)


def _ffn_act_bwd(g3, u3, conv_w, conv_b, dact3, name):
    bsz, s, f = g3.shape
    nblk = f // LANES

    def body(g_ref, u_ref, w_ref, b_ref, d_ref, dg_ref, du_ref, dw_ref, db_ref):
        g, w, u = g_ref[0], w_ref[...], u_ref[0]
        gate = _conv_fwd(g, w, 1) + b_ref[...]
        _, vjp = jax.vjp(lambda gt, uu: _gelu(gt) * uu, gate, u)
        dgate, du = vjp(d_ref[0])
        dg, dw = _conv_bwd(g, w, 1, dgate)
        dg_ref[0] = dg.astype(dg_ref.dtype)
        du_ref[0] = du.astype(du_ref.dtype)

        @pl.when(pl.program_id(1) == 0)
        def _():
            dw_ref[...] = jnp.zeros_like(dw_ref)
            db_ref[...] = jnp.zeros_like(db_ref)

        dw_ref[...] += dw
        db_ref[...] += jnp.sum(dgate, axis=0, keepdims=True)

    col = pl.BlockSpec((1, s, LANES), lambda j, b: (b, 0, j))
    w3 = pl.BlockSpec((3, LANES), lambda j, b: (0, j))
    w1 = pl.BlockSpec((1, LANES), lambda j, b: (0, j))
    return pl.pallas_call(
        body, name=name, grid=(nblk, bsz),
        in_specs=[col, col, w3, w1, col], out_specs=[col, col, w3, w1],
        out_shape=[jax.ShapeDtypeStruct((bsz, s, f), BF16), jax.ShapeDtypeStruct((bsz, s, f), BF16),
                   jax.ShapeDtypeStruct((3, f), F32), jax.ShapeDtypeStruct((1, f), F32)],
        compiler_params=_cparams(("parallel", "arbitrary")),
    )(g3, u3, conv_w, conv_b, dact3)


def _ple_fn(r, pg, pp, bg):
    return r + _sigmoid(pg + bg) * pp


def _ple_fwd(r, pg, pp, bg, name):
    t, d = r.shape
    tm = _row_tile(t)

    def body(r_ref, pg_ref, pp_ref, bg_ref, o_ref):
        o_ref[...] = _ple_fn(r_ref[...], pg_ref[...], pp_ref[...], bg_ref[...])

    row = pl.BlockSpec((tm, d), lambda i: (i, 0))
    return pl.pallas_call(
        body, name=name, grid=(t // tm,), in_specs=[row, row, row, pl.BlockSpec((1, d), lambda i: (0, 0))],
        out_specs=row, out_shape=jax.ShapeDtypeStruct((t, d), F32),
        compiler_params=_cparams(("parallel",)),
    )(r, pg, pp, bg)


def _ple_bwd(pg, pp, bg, dr, name):
    t, d = pg.shape
    tm = _row_tile(t)

    def body(pg_ref, pp_ref, bg_ref, dr_ref, dpg_ref, dpp_ref, dbg_ref):
        _, vjp = jax.vjp(lambda a, b, c: _sigmoid(a + c) * b, pg_ref[...], pp_ref[...], bg_ref[...])
        dpg, dpp, dbg = vjp(dr_ref[...])
        dpg_ref[...] = dpg.astype(dpg_ref.dtype)
        dpp_ref[...] = dpp.astype(dpp_ref.dtype)

        @pl.when(pl.program_id(0) == 0)
        def _():
            dbg_ref[...] = jnp.zeros_like(dbg_ref)

        dbg_ref[...] += dbg

    row = pl.BlockSpec((tm, d), lambda i: (i, 0))
    vec = pl.BlockSpec((1, d), lambda i: (0, 0))
    return pl.pallas_call(
        body, name=name, grid=(t // tm,), in_specs=[row, row, vec, row], out_specs=[row, row, vec],
        out_shape=[jax.ShapeDtypeStruct((t, d), BF16), jax.ShapeDtypeStruct((t, d), BF16), jax.ShapeDtypeStruct((1, d), F32)],
        compiler_params=_cparams(("arbitrary",)),
    )(pg, pp, bg, dr)


def _loss_head(r, g, target, name):
    t, d = r.shape
    tm = _row_tile(t)

    def loss_fn(x, gg, tgt):
        err = _rms(x, gg) - tgt
        return 0.5 * jnp.sum(jnp.sum(err * err, axis=-1, keepdims=True) * (1.0 / d), axis=0, keepdims=True)

    def body(r_ref, g_ref, t_ref, l_ref, dr_ref, dg_ref):
        val, vjp = jax.vjp(lambda x, gg: loss_fn(x, gg, t_ref[...]), r_ref[...], g_ref[...])
        dx, dg = vjp(jnp.ones((1, 1), F32))
        dr_ref[...] = dx

        @pl.when(pl.program_id(0) == 0)
        def _():
            l_ref[...] = jnp.zeros_like(l_ref)
            dg_ref[...] = jnp.zeros_like(dg_ref)

        l_ref[...] += val
        dg_ref[...] += dg

    row = pl.BlockSpec((tm, d), lambda i: (i, 0))
    vec = pl.BlockSpec((1, d), lambda i: (0, 0))
    one = pl.BlockSpec((1, 1), lambda i: (0, 0))
    return pl.pallas_call(
        body, name=name, grid=(t // tm,), in_specs=[row, vec, row], out_specs=[one, row, vec],
        out_shape=[jax.ShapeDtypeStruct((1, 1), F32), jax.ShapeDtypeStruct((t, d), F32), jax.ShapeDtypeStruct((1, d), F32)],
        compiler_params=_cparams(("arbitrary",)),
    )(r, g, target)


def _adamw_math(w, gg, m, v, d_ref, nm_ref, nv_ref):
    nm = ADAM_B1 * m + (1.0 - ADAM_B1) * gg
    nv = ADAM_B2 * v + (1.0 - ADAM_B2) * (gg * gg)
    m_hat = nm / (1.0 - ADAM_B1 ** ADAM_STEP)
    v_hat = nv / (1.0 - ADAM_B2 ** ADAM_STEP)
    d_ref[...] = -ADAM_LR * (m_hat / (jnp.sqrt(v_hat) + ADAM_EPS) + ADAM_WD * w)
    nm_ref[...] = nm
    nv_ref[...] = nv


def _adamw(w, g, m, v, name):
    r, c = w.shape
    tr = r if r <= 512 else _row_tile(r, 512)

    def body(w_ref, g_ref, m_ref, v_ref, d_ref, nm_ref, nv_ref):
        _adamw_math(w_ref[...], g_ref[...], m_ref[...], v_ref[...], d_ref, nm_ref, nv_ref)

    blk = pl.BlockSpec((tr, c), lambda i: (i, 0))
    return pl.pallas_call(
        body, name=name, grid=(r // tr,), in_specs=[blk] * 4, out_specs=[blk] * 3,
        out_shape=[jax.ShapeDtypeStruct((r, c), F32)] * 3,
        compiler_params=_cparams(("parallel",)),
    )(w, g, m, v)


def _prepare_weights(w):
    nl = w["w_in"].shape[0]
    w_in = w["w_in"].astype(BF16)
    pad = jnp.zeros(w_in.shape[:2] + (P_COLS - w_in.shape[2],), BF16)
    split = 4 * DN_WIDTH + N_GATE
    w_in_p = jnp.concatenate([w_in[:, :, split:], w_in[:, :, :split], pad], axis=-1)
    gate_vec = lambda a: jnp.pad(a.reshape(nl, 1, N_GATE // 2), ((0, 0), (0, 0), (N_GATE // 2, LANES - N_GATE)))

    def pair_blocks(a):
        a = a.reshape(nl, 2, 4, 2, 64, 64)
        z = jnp.zeros_like(a[:, :, :, 0])
        top = jnp.concatenate([a[:, :, :, 0], z], axis=-1)
        bot = jnp.concatenate([z, a[:, :, :, 1]], axis=-1)
        return jnp.concatenate([top, bot], axis=-2).astype(BF16)

    bf = lambda a: a.astype(BF16)
    return dict(
        norm1_g=w["norm1_g"][:, None, :], w_in=w_in_p,
        dn_conv_w=w["dn_conv_w"], alog=gate_vec(w["dn_a_log"]), dtb=gate_vec(w["dn_dt_bias"]),
        dn_norm_g=w["dn_norm_g"][:, None, :], lru_conv_w=w["lru_conv_w"], lru_conv_b=w["lru_conv_b"][:, None, :],
        lru_wa=pair_blocks(w["lru_wa"]), lru_ba=w["lru_ba"], lru_wx=pair_blocks(w["lru_wx"]), lru_bx=w["lru_bx"],
        lru_lambda=w["lru_lambda"], lru_norm_g=w["lru_norm_g"][:, None, :],
        w_out=bf(w["w_out"]), norm2_g=w["norm2_g"][:, None, :], ffn_wg=bf(w["ffn_wg"]), ffn_wu=bf(w["ffn_wu"]),
        ffn_conv_w=w["ffn_conv_w"], ffn_conv_b=w["ffn_conv_b"][:, None, :], ffn_wd=bf(w["ffn_wd"]),
        ple_norm_g=w["ple_norm_g"][:, None, :], ple_wg=bf(w["ple_wg"]), ple_bg=w["ple_bg"][:, None, :], ple_wp=bf(w["ple_wp"]),
        final_g=w["final_g"][None, :],
    )


def _unpair_blocks(a):
    top = a[:, :, :64, :64]
    bot = a[:, :, 64:, 64:]
    return jnp.stack([top, bot], axis=2).reshape(2, 8, 64, 64)


def _local_step(x, p, target, w):
    bsz, s, d = x.shape
    t = bsz * s
    nl = w["w_in"].shape[0]
    kw = _prepare_weights(w)
    flat = lambda a: a.reshape(t, a.shape[-1])
    seq = lambda a: a.reshape(bsz, s, a.shape[-1])

    saved = []
    r = flat(x)
    for i in range(nl):
        n = f"l{i}_"
        sv = {"r0": r}
        h, proj = _norm_matmul(r, kw["norm1_g"][i], [kw["w_in"][i]], n + "in_proj")
        gb = _dn_gates_fwd(proj, kw["alog"][i], kw["dtb"][i], n + "dn_gates")
        qkv = _dn_prep_fwd(seq(proj), kw["dn_conv_w"][i], n + "dn_prep")
        in_f = _dn_intra_fwd(qkv, seq(gb), False, n + "dn_intra_fwd")
        in_b = _dn_intra_fwd(qkv, seq(gb), True, n + "dn_intra_rev")
        o_f, st_f = _dn_rec_fwd(in_f, False, n + "dn_rec_fwd")
        o_b, st_b = _dn_rec_fwd(in_b, True, n + "dn_rec_rev")
        lru_args = (seq(proj), kw["lru_conv_w"][i], kw["lru_conv_b"][i], kw["lru_wa"][i], kw["lru_ba"][i],
                    kw["lru_wx"][i], kw["lru_bx"][i], kw["lru_lambda"][i])
        hf, hb = _lru_fwd(*lru_args, n + "lru")
        mix_args = (flat(o_f), flat(o_b), proj, flat(hf), flat(hb), kw["dn_norm_g"][i], kw["lru_norm_g"][i])
        mix = _mix_fwd(*mix_args, n + "mix")
        r1 = _matmul([(mix, kw["w_out"][i])], n + "out_proj", res=r)
        h2, fg, fu = _norm_matmul(r1, kw["norm2_g"][i], [kw["ffn_wg"][i], kw["ffn_wu"][i]], n + "ffn_gu")
        act = _ffn_act_fwd(seq(fg), seq(fu), kw["ffn_conv_w"][i], kw["ffn_conv_b"][i], n + "ffn_act")
        r2 = _matmul([(flat(act), kw["ffn_wd"][i])], n + "ffn_d", res=r1)
        hp, pg = _norm_matmul(r2, kw["ple_norm_g"][i], [kw["ple_wg"][i]], n + "ple_g")
        pi = flat(p[i])
        pp = _matmul([(pi, kw["ple_wp"][i])], n + "ple_p")
        r3 = _ple_fwd(r2, pg, pp, kw["ple_bg"][i], n + "ple")
        sv.update(h=h, proj=proj, gb=gb, qkv=qkv, st_f=st_f, st_b=st_b, in_f=in_f, in_b=in_b, lru_args=lru_args, hf=hf, hb=hb,
                  mix_args=mix_args, mix=mix, r1=r1, h2=h2, fg=fg, fu=fu, act=act, r2=r2, hp=hp, pg=pg, pp=pp, pi=pi)
        saved.append(sv)
        r = r3

    loss, dr, dfinal = _loss_head(r, kw["final_g"], flat(target), "loss_head")
    grads = {k: [None] * nl for k in (
        "norm1_g", "w_in", "dn_conv_w", "dn_a_log", "dn_dt_bias", "dn_norm_g", "lru_conv_w", "lru_conv_b", "lru_wa", "lru_ba",
        "lru_wx", "lru_bx", "lru_lambda", "lru_norm_g", "w_out", "norm2_g", "ffn_wg", "ffn_wu", "ffn_conv_w", "ffn_conv_b",
        "ffn_wd", "ple_norm_g", "ple_wg", "ple_bg", "ple_wp", "dn_a_log_lanes", "dn_dt_bias_lanes")}

    for i in reversed(range(nl)):
        n = f"b{i}_"
        sv = saved[i]
        dpg, dpp, dbg = _ple_bwd(sv["pg"], sv["pp"], kw["ple_bg"][i], dr, n + "ple")
        grads["ple_bg"][i] = dbg[0]
        grads["ple_wp"][i] = _matmul_tn(sv["pi"], dpp, n + "ple_wp")
        grads["ple_wg"][i] = _matmul_tn(sv["hp"], dpg, n + "ple_wg")
        dr2, dr2b, dg = _matmul_norm_bwd([(dpg, kw["ple_wg"][i])], sv["r2"], kw["ple_norm_g"][i], dr, n + "ple_dh")
        grads["ple_norm_g"][i] = dg[0]
        grads["ffn_wd"][i] = _matmul_tn(flat(sv["act"]), dr2b, n + "ffn_wd")
        dact = _matmul([(dr2b, kw["ffn_wd"][i])], n + "ffn_dact", nt=True)
        dfg, dfu, dcw, dcb = _ffn_act_bwd(seq(sv["fg"]), seq(sv["fu"]), kw["ffn_conv_w"][i], kw["ffn_conv_b"][i], seq(dact),
                                          n + "ffn_act")
        grads["ffn_conv_w"][i] = dcw
        grads["ffn_conv_b"][i] = dcb[0]
        grads["ffn_wg"][i] = _matmul_tn(sv["h2"], flat(dfg), n + "ffn_wg")
        grads["ffn_wu"][i] = _matmul_tn(sv["h2"], flat(dfu), n + "ffn_wu")
        dr1, dr1b, dg = _matmul_norm_bwd([(flat(dfg), kw["ffn_wg"][i]), (flat(dfu), kw["ffn_wu"][i])], sv["r1"], kw["norm2_g"][i],
                                         dr2, n + "ffn_dh")
        grads["norm2_g"][i] = dg[0]
        grads["w_out"][i] = _matmul_tn(sv["mix"], dr1b, n + "w_out")
        dmix = _matmul([(dr1b, kw["w_out"][i])], n + "dmix", nt=True)
        do, dz, dlg, dh, ddn_g, dlru_g = _mix_bwd(*sv["mix_args"], dmix, n + "mix")
        grads["dn_norm_g"][i] = ddn_g[0]
        grads["lru_norm_g"][i] = dlru_g[0]
        dlx, dcw, dcb, dwa, dba, dwx, dbx, dlam = _lru_bwd(*sv["lru_args"], sv["hf"], sv["hb"], seq(dh), n + "lru")
        grads["lru_conv_w"][i] = dcw
        grads["lru_conv_b"][i] = dcb[0]
        grads["lru_wa"][i] = _unpair_blocks(dwa)
        grads["lru_wx"][i] = _unpair_blocks(dwx)
        grads["lru_ba"][i], grads["lru_bx"][i], grads["lru_lambda"][i] = dba, dbx, dlam
        ct_f = _dn_rec_bwd(sv["in_f"], sv["st_f"], seq(do), False, n + "dn_rec_fwd")
        ct_b = _dn_rec_bwd(sv["in_b"], sv["st_b"], seq(do), True, n + "dn_rec_rev")
        dqkv_f, dgb_f = _dn_intra_bwd(sv["qkv"], seq(sv["gb"]), ct_f, False, n + "dn_intra_fwd")
        dqkv_b, dgb_b = _dn_intra_bwd(sv["qkv"], seq(sv["gb"]), ct_b, True, n + "dn_intra_rev")
        dgate, dalog, ddtb = _dn_gates_bwd(sv["proj"], kw["alog"][i], kw["dtb"][i], flat(dgb_f), flat(dgb_b), n + "dn_gates")
        grads["dn_a_log"][i] = dalog[0, N_GATE // 2:N_GATE].reshape(2, DN_HEADS)
        grads["dn_dt_bias"][i] = ddtb[0, N_GATE // 2:N_GATE].reshape(2, DN_HEADS)
        grads["dn_a_log_lanes"][i] = dalog[0]
        grads["dn_dt_bias_lanes"][i] = ddtb[0]
        dpqkv, dcw = _dn_prep_bwd(seq(sv["proj"]), kw["dn_conv_w"][i], dqkv_f, dqkv_b, n + "dn_prep")
        grads["dn_conv_w"][i] = dcw
        segs = [flat(dlx), dlg, flat(dpqkv), dz, dgate]
        w_in_i = kw["w_in"][i]
        dq3 = flat(dpqkv)
        wseg = DN_WIDTH
        terms = [(flat(dlx), (w_in_i, wseg, P_LX // wseg)), (dlg, (w_in_i, wseg, P_LG // wseg))]
        terms += [((dq3, wseg, j), (w_in_i, wseg, P_Q // wseg + j)) for j in range(3)]
        terms += [(dz, (w_in_i, wseg, P_Z // wseg)), (dgate, (w_in_i, LANES, P_GATE // LANES))]
        dwp = _matmul_tn_cols(sv["h"], segs, n + "w_in")
        split = 4 * DN_WIDTH + N_GATE
        grads["w_in"][i] = jnp.concatenate([dwp[:, P_Q:P_Q + split], dwp[:, :P_Q]], axis=-1)
        dr, _, dg = _matmul_norm_bwd(terms, sv["r0"], kw["norm1_g"][i], dr1, n + "in_dh")
        grads["norm1_g"][i] = dg[0]

    grads["final_g"] = dfinal[0]
    return loss[0, 0], dr.reshape(bsz, s, d), grads


MESH = pl.DeviceIdType.MESH
ANY = pl.BlockSpec(memory_space=pl.ANY)
N_CHIPS = 4
HALF = 2


def _place():
    x, y, c = lax.axis_index("x"), lax.axis_index("y"), lax.axis_index("c")
    chips = [(1 - x, y), (x, 1 - y), (1 - x, 1 - y)]
    return x, y, c, chips


def _dma_sems(n):
    return pltpu.SemaphoreType.DMA((n,))


def _gather_weights(shards):
    nt = len(shards)

    def body(*refs):
        w_refs, g_refs = refs[:nt], refs[nt:2 * nt]
        send_sems, recv_sems = refs[2 * nt:]
        x, y, c, chips = _place()
        me = 2 * x + y
        mine = pl.ds(c * HALF, HALF)
        theirs = pl.ds((1 - c) * HALF, HALF)

        def rc(src, dst, k, to):
            return pltpu.make_async_remote_copy(src_ref=src, dst_ref=dst, send_sem=send_sems.at[k], recv_sem=recv_sems.at[k],
                                                device_id=to, device_id_type=MESH)

        first = []
        for t in range(nt):
            for r, (px, py) in enumerate(chips):
                first.append(rc(w_refs[t].at[mine], g_refs[t].at[me, mine], 6 * t + r, (px, py, c)))
        for cp in first:
            cp.start()
        passed = []
        for t in range(nt):
            for r, (px, py) in enumerate(chips):
                peer = 2 * px + py
                rc(w_refs[t].at[mine], g_refs[t].at[peer, mine], 6 * t + r, (px, py, c)).wait_recv()
                fw = rc(g_refs[t].at[peer, mine], g_refs[t].at[peer, mine], 6 * t + 3 + r, (x, y, 1 - c))
                fw.start()
                passed.append(fw)
        for t in range(nt):
            for r, (px, py) in enumerate(chips):
                peer = 2 * px + py
                rc(g_refs[t].at[peer, theirs], g_refs[t].at[peer, theirs], 6 * t + 3 + r, (x, y, 1 - c)).wait_recv()
        for cp in first + passed:
            cp.wait_send()

    return pl.pallas_call(
        body, name="gather_weights", in_specs=[ANY] * nt, out_specs=[ANY] * nt,
        out_shape=[jax.ShapeDtypeStruct((N_CHIPS,) + a.shape, a.dtype) for a in shards],
        scratch_shapes=[_dma_sems(6 * nt), _dma_sems(6 * nt)],
    )(*shards)


def _swap_halves(blocks):
    nt = len(blocks)

    def body(*refs):
        p_refs, l_refs = refs[:nt], refs[nt:2 * nt]
        send_sems, recv_sems = refs[2 * nt:]
        x, y, c, _ = _place()
        theirs = pl.ds((1 - c) * HALF, HALF)
        cps = [pltpu.make_async_remote_copy(src_ref=p_refs[t].at[:, theirs], dst_ref=l_refs[t], send_sem=send_sems.at[t],
                                            recv_sem=recv_sems.at[t], device_id=(x, y, 1 - c), device_id_type=MESH)
               for t in range(nt)]
        for cp in cps:
            cp.start()
        for cp in cps:
            cp.wait_send()
            cp.wait_recv()

    return pl.pallas_call(
        body, name="swap_halves", in_specs=[ANY] * nt, out_specs=[ANY] * nt,
        out_shape=[jax.ShapeDtypeStruct((a.shape[0], HALF) + a.shape[2:], a.dtype) for a in blocks],
        scratch_shapes=[_dma_sems(nt), _dma_sems(nt)],
    )(*blocks)


def _tile_rows(a, row_bytes):
    best = None
    for d in range(16, a + 1, 16):
        if a % d == 0 and d * row_bytes <= 2 * 1024 * 1024:
            best = d
    return best if best is not None else a


def _add_halves(pg, l1, c_arr, name):
    n, _, a, b = pg.shape
    ta = _tile_rows(a, 4 * b)

    def body(c_ref, a_ref, b_ref, o_ref):
        o_ref[...] = (a_ref[...] + b_ref[...]).astype(o_ref.dtype)

    grid_spec = pltpu.PrefetchScalarGridSpec(
        num_scalar_prefetch=1, grid=(n, HALF, a // ta),
        in_specs=[pl.BlockSpec((1, 1, ta, b), lambda k, l, i, c_ref: (k, c_ref[0] * HALF + l, i, 0)),
                  pl.BlockSpec((1, 1, ta, b), lambda k, l, i, c_ref: (k, l, i, 0))],
        out_specs=pl.BlockSpec((1, 1, ta, b), lambda k, l, i, c_ref: (k, l, i, 0)))
    return pl.pallas_call(
        body, name=name, grid_spec=grid_spec, out_shape=jax.ShapeDtypeStruct((n, HALF, a, b), BF16),
        compiler_params=_cparams(("parallel", "parallel", "parallel")),
    )(c_arr, pg, l1)


def _scatter_chips(blocks):
    nt = len(blocks)

    def body(*refs):
        q_refs, l_refs = refs[:nt], refs[nt:2 * nt]
        send_sems, recv_sems = refs[2 * nt:]
        x, y, c, chips = _place()
        me = 2 * x + y

        def rc(t, r, src_chip, dst_chip, to):
            return pltpu.make_async_remote_copy(src_ref=q_refs[t].at[src_chip], dst_ref=l_refs[t].at[dst_chip],
                                                send_sem=send_sems.at[3 * t + r], recv_sem=recv_sems.at[3 * t + r],
                                                device_id=to, device_id_type=MESH)

        sends = [rc(t, r, 2 * px + py, me, (px, py, c)) for t in range(nt) for r, (px, py) in enumerate(chips)]
        for cp in sends:
            cp.start()
        for t in range(nt):
            for r, (px, py) in enumerate(chips):
                rc(t, r, 2 * px + py, 2 * px + py, (px, py, c)).wait_recv()
        for cp in sends:
            cp.wait_send()

    return pl.pallas_call(
        body, name="scatter_chips", in_specs=[ANY] * nt, out_specs=[ANY] * nt,
        out_shape=[jax.ShapeDtypeStruct(a.shape, a.dtype) for a in blocks],
        scratch_shapes=[_dma_sems(3 * nt), _dma_sems(3 * nt)],
    )(*blocks)


def _sum_chips(l2, own, me_arr, name):
    n, _, a, b = l2.shape
    ta = _tile_rows(a, 4 * b)

    def body(me_ref, a_ref, own_ref, o_ref):
        me = me_ref[0]
        acc = jnp.where(me == 0, own_ref[0, 0], a_ref[0, 0]).astype(F32)
        for k in range(1, n):
            acc = acc + jnp.where(me == k, own_ref[0, 0], a_ref[k, 0]).astype(F32)
        o_ref[0] = acc

    grid_spec = pltpu.PrefetchScalarGridSpec(
        num_scalar_prefetch=1, grid=(HALF, a // ta),
        in_specs=[pl.BlockSpec((n, 1, ta, b), lambda l, i, me_ref: (0, l, i, 0)),
                  pl.BlockSpec((1, 1, ta, b), lambda l, i, me_ref: (me_ref[0], l, i, 0))],
        out_specs=pl.BlockSpec((1, ta, b), lambda l, i, me_ref: (l, i, 0)))
    return pl.pallas_call(
        body, name=name, grid_spec=grid_spec, out_shape=jax.ShapeDtypeStruct((HALF, a, b), F32),
        compiler_params=_cparams(("parallel", "parallel")),
    )(me_arr, l2, own)


def _swap_reduced(parts):
    nt = len(parts)

    def body(*refs):
        r_refs, g_refs = refs[:nt], refs[nt:2 * nt]
        send_sems, recv_sems = refs[2 * nt:]
        x, y, c, _ = _place()
        cps = [pltpu.make_async_remote_copy(src_ref=r_refs[t], dst_ref=g_refs[t], send_sem=send_sems.at[t],
                                            recv_sem=recv_sems.at[t], device_id=(x, y, 1 - c), device_id_type=MESH)
               for t in range(nt)]
        for cp in cps:
            cp.start()
        for cp in cps:
            cp.wait_send()
            cp.wait_recv()

    return pl.pallas_call(
        body, name="swap_reduced", in_specs=[ANY] * nt, out_specs=[ANY] * nt,
        out_shape=[jax.ShapeDtypeStruct(a.shape, a.dtype) for a in parts],
        scratch_shapes=[_dma_sems(nt), _dma_sems(nt)],
    )(*parts)


def _adamw_halves(w, mine, theirs, m, v, c_arr, name):
    nl, a, b = w.shape
    ta = _tile_rows(a, 4 * b)

    def body(c_ref, w_ref, g1_ref, g2_ref, m_ref, v_ref, g_ref, d_ref, nm_ref, nv_ref):
        gg = jnp.where(pl.program_id(0) // HALF == c_ref[0], g1_ref[...], g2_ref[...])
        g_ref[...] = gg
        _adamw_math(w_ref[...], gg, m_ref[...], v_ref[...], d_ref, nm_ref, nv_ref)

    full = pl.BlockSpec((1, ta, b), lambda l, i, c_ref: (l, i, 0))
    half = pl.BlockSpec((1, ta, b), lambda l, i, c_ref: (l % HALF, i, 0))
    grid_spec = pltpu.PrefetchScalarGridSpec(num_scalar_prefetch=1, grid=(nl, a // ta),
                                             in_specs=[full, half, half, full, full], out_specs=[full] * 4)
    return pl.pallas_call(
        body, name=name, grid_spec=grid_spec, out_shape=[jax.ShapeDtypeStruct(w.shape, F32)] * 4,
        compiler_params=_cparams(("parallel", "parallel")),
    )(c_arr, w, mine, theirs, m, v)


def _gather_chips(v):
    def body(v_ref, g_ref, send_sems, recv_sems, local_sem):
        x, y, c, chips = _place()
        me = 2 * x + y
        loc = pltpu.make_async_copy(v_ref, g_ref.at[me], local_sem)
        loc.start()
        sends = [pltpu.make_async_remote_copy(src_ref=v_ref, dst_ref=g_ref.at[me], send_sem=send_sems.at[r], recv_sem=recv_sems.at[r],
                                              device_id=(px, py, c), device_id_type=MESH) for r, (px, py) in enumerate(chips)]
        for cp in sends:
            cp.start()
        for r, (px, py) in enumerate(chips):
            pltpu.make_async_remote_copy(src_ref=v_ref, dst_ref=g_ref.at[2 * px + py], send_sem=send_sems.at[r],
                                         recv_sem=recv_sems.at[r], device_id=(px, py, c), device_id_type=MESH).wait_recv()
        for cp in sends:
            cp.wait_send()
        loc.wait()

    return pl.pallas_call(
        body, name="gather_chips", in_specs=[ANY], out_specs=ANY, out_shape=jax.ShapeDtypeStruct((N_CHIPS,) + v.shape, v.dtype),
        scratch_shapes=[_dma_sems(3), _dma_sems(3), pltpu.SemaphoreType.DMA],
    )(v)


BIG = (("w_in", 2), ("w_out", 1), ("ffn_wg", 2), ("ffn_wu", 2), ("ffn_wd", 1), ("ple_wg", 1), ("ple_wp", 2))
SMALL = (("dn_conv_w", 2), ("lru_conv_w", 2), ("lru_ba", 2), ("lru_bx", 2), ("lru_lambda", 2), ("ffn_conv_w", 2))
REPL = ("norm1_g", "dn_a_log", "dn_dt_bias", "dn_norm_g", "lru_conv_b", "lru_wa", "lru_wx", "lru_norm_g", "norm2_g",
        "ffn_conv_b", "ple_norm_g", "ple_bg", "final_g")
WEIGHTS = ("norm1_g", "w_in", "dn_conv_w", "dn_a_log", "dn_dt_bias", "dn_norm_g", "lru_conv_w", "lru_conv_b", "lru_wa", "lru_ba",
           "lru_wx", "lru_bx", "lru_lambda", "lru_norm_g", "w_out", "norm2_g", "ffn_wg", "ffn_wu", "ffn_conv_w", "ffn_conv_b",
           "ffn_wd", "ple_norm_g", "ple_wg", "ple_bg", "ple_wp", "final_g")
GATE_PARAMS = ("dn_a_log", "dn_dt_bias")
ROW_ALIGN = 16


def _rows_for(n_elems):
    rows = -(-n_elems // LANES)
    return -(-rows // ROW_ALIGN) * ROW_ALIGN


def _join_chips(g, own, me, axis):
    return jnp.concatenate([jnp.where(me == k, own, g[k]) for k in range(N_CHIPS)], axis=axis)


def _chip_blocks(layers, axis):
    nl = len(layers)
    size = layers[0].shape[axis - 1] // N_CHIPS
    cut = (lambda g, k: g[:, k * size:(k + 1) * size]) if axis == 2 else (lambda g, k: g[k * size:(k + 1) * size])
    return jnp.stack([jnp.stack([cut(layers[l], k) for l in range(nl)]) for k in range(N_CHIPS)])


def _pack_small(w):
    nl = w[SMALL[0][0]].shape[0]
    flat = jnp.concatenate([w[n].reshape(nl, -1) for n, _ in SMALL], axis=1)
    rows = _rows_for(flat.shape[1])
    return jnp.pad(flat, ((0, 0), (0, rows * LANES - flat.shape[1]))).reshape(nl, rows, LANES)


def _unpack_small_gathered(g, own, me, w):
    nl = g.shape[1]
    flat = jnp.stack([jnp.where(me == k, own, g[k]) for k in range(N_CHIPS)]).reshape(N_CHIPS, nl, -1)
    out, off = {}, 0
    for n, _ in SMALL:
        _, a, b = w[n].shape
        piece = flat[:, :, off:off + a * b].reshape(N_CHIPS, nl, a, b)
        off += a * b
        out[n] = jnp.transpose(piece, (1, 2, 0, 3)).reshape(nl, a, N_CHIPS * b)
    return out


def _pack_small_grads(grads, w):
    nl = w["w_in"].shape[0]
    cols = []
    for n, _ in SMALL:
        _, a, b = w[n].shape
        gfull = jnp.stack(grads[n])
        cols.append(jnp.transpose(gfull.reshape(nl, a, N_CHIPS, b), (2, 0, 1, 3)).reshape(N_CHIPS, nl, a * b))
    small = jnp.concatenate(cols, axis=2)
    rs = _rows_for(small.shape[2])
    small = jnp.pad(small, ((0, 0), (0, 0), (0, rs * LANES - small.shape[2])))
    src = lambda n: n + "_lanes" if n in GATE_PARAMS else n
    rep = jnp.concatenate([(grads[n] if n == "final_g" else jnp.stack(grads[src(n)])).reshape(-1) for n in REPL])
    rr = _rows_for(-(-rep.shape[0] // (N_CHIPS * nl)))
    rep = jnp.pad(rep, (0, N_CHIPS * nl * rr * LANES - rep.shape[0])).reshape(N_CHIPS, nl, rr * LANES)
    return jnp.concatenate([small, rep], axis=2).reshape(N_CHIPS, nl, rs + rr, LANES), (rs, rr)


def _unpack_small_reduced(g, rep_all, rows, w):
    rs, _ = rows
    nl = g.shape[0]
    out = {}
    flat = g[:, :rs].reshape(nl, -1)
    off = 0
    for n, _ in SMALL:
        _, a, b = w[n].shape
        out[n] = flat[:, off:off + a * b].reshape(nl, a, b)
        off += a * b
    flat = rep_all.reshape(-1)
    off = 0
    for n in REPL:
        if n in GATE_PARAMS:
            size = nl * LANES
            out[n] = flat[off:off + size].reshape(nl, LANES)[:, N_GATE // 2:N_GATE].reshape(w[n].shape)
        else:
            size = math.prod(w[n].shape)
            out[n] = flat[off:off + size].reshape(w[n].shape)
        off += size
    return out


def _as2d(a):
    if a.ndim == 1:
        return a.reshape(1, -1)
    return a.reshape(-1, a.shape[-1])


def kernel(x, p, norm1_g, w_in, dn_conv_w, dn_a_log, dn_dt_bias, dn_norm_g, lru_conv_w, lru_conv_b, lru_wa, lru_ba, lru_wx, lru_bx, lru_lambda, lru_norm_g, w_out, norm2_g, ffn_wg, ffn_wu, ffn_conv_w, ffn_conv_b, ffn_wd, ple_norm_g, ple_wg, ple_bg, ple_wp, final_g, loss_target, m_norm1_g, m_w_in, m_dn_conv_w, m_dn_a_log, m_dn_dt_bias, m_dn_norm_g, m_lru_conv_w, m_lru_conv_b, m_lru_wa, m_lru_ba, m_lru_wx, m_lru_bx, m_lru_lambda, m_lru_norm_g, m_w_out, m_norm2_g, m_ffn_wg, m_ffn_wu, m_ffn_conv_w, m_ffn_conv_b, m_ffn_wd, m_ple_norm_g, m_ple_wg, m_ple_bg, m_ple_wp, m_final_g, v_norm1_g, v_w_in, v_dn_conv_w, v_dn_a_log, v_dn_dt_bias, v_dn_norm_g, v_lru_conv_w, v_lru_conv_b, v_lru_wa, v_lru_ba, v_lru_wx, v_lru_bx, v_lru_lambda, v_lru_norm_g, v_w_out, v_norm2_g, v_ffn_wg, v_ffn_wu, v_ffn_conv_w, v_ffn_conv_b, v_ffn_wd, v_ple_norm_g, v_ple_wg, v_ple_bg, v_ple_wp, v_final_g):
    w = dict(norm1_g=norm1_g, w_in=w_in, dn_conv_w=dn_conv_w, dn_a_log=dn_a_log, dn_dt_bias=dn_dt_bias, dn_norm_g=dn_norm_g,
             lru_conv_w=lru_conv_w, lru_conv_b=lru_conv_b, lru_wa=lru_wa, lru_ba=lru_ba, lru_wx=lru_wx, lru_bx=lru_bx,
             lru_lambda=lru_lambda, lru_norm_g=lru_norm_g, w_out=w_out, norm2_g=norm2_g, ffn_wg=ffn_wg, ffn_wu=ffn_wu,
             ffn_conv_w=ffn_conv_w, ffn_conv_b=ffn_conv_b, ffn_wd=ffn_wd, ple_norm_g=ple_norm_g, ple_wg=ple_wg, ple_bg=ple_bg,
             ple_wp=ple_wp, final_g=final_g)
    m = dict(norm1_g=m_norm1_g, w_in=m_w_in, dn_conv_w=m_dn_conv_w, dn_a_log=m_dn_a_log, dn_dt_bias=m_dn_dt_bias,
             dn_norm_g=m_dn_norm_g, lru_conv_w=m_lru_conv_w, lru_conv_b=m_lru_conv_b, lru_wa=m_lru_wa, lru_ba=m_lru_ba,
             lru_wx=m_lru_wx, lru_bx=m_lru_bx, lru_lambda=m_lru_lambda, lru_norm_g=m_lru_norm_g, w_out=m_w_out, norm2_g=m_norm2_g,
             ffn_wg=m_ffn_wg, ffn_wu=m_ffn_wu, ffn_conv_w=m_ffn_conv_w, ffn_conv_b=m_ffn_conv_b, ffn_wd=m_ffn_wd,
             ple_norm_g=m_ple_norm_g, ple_wg=m_ple_wg, ple_bg=m_ple_bg, ple_wp=m_ple_wp, final_g=m_final_g)
    v = dict(norm1_g=v_norm1_g, w_in=v_w_in, dn_conv_w=v_dn_conv_w, dn_a_log=v_dn_a_log, dn_dt_bias=v_dn_dt_bias,
             dn_norm_g=v_dn_norm_g, lru_conv_w=v_lru_conv_w, lru_conv_b=v_lru_conv_b, lru_wa=v_lru_wa, lru_ba=v_lru_ba,
             lru_wx=v_lru_wx, lru_bx=v_lru_bx, lru_lambda=v_lru_lambda, lru_norm_g=v_lru_norm_g, w_out=v_w_out, norm2_g=v_norm2_g,
             ffn_wg=v_ffn_wg, ffn_wu=v_ffn_wu, ffn_conv_w=v_ffn_conv_w, ffn_conv_b=v_ffn_conv_b, ffn_wd=v_ffn_wd,
             ple_norm_g=v_ple_norm_g, ple_wg=v_ple_wg, ple_bg=v_ple_bg, ple_wp=v_ple_wp, final_g=v_final_g)

    me = 2 * lax.axis_index("x") + lax.axis_index("y")
    own = [w[n].astype(BF16) for n, _ in BIG] + [_pack_small(w)]
    gathered = _gather_weights(own)
    full = {n: w[n] for n in REPL}
    for (n, axis), g, o in zip(BIG, gathered, own):
        full[n] = _join_chips(g, o, me, axis)
    full.update(_unpack_small_gathered(gathered[-1], own[-1], me, w))

    loss_local, grad_x, grads = _local_step(x, p, loss_target, full)
    loss = lax.psum(loss_local, ("x", "y", "c"))

    small_pack, rows = _pack_small_grads(grads, w)
    blocks = [_chip_blocks(grads[n], axis) for n, axis in BIG]
    blocks.append(small_pack)
    names = [n for n, _ in BIG] + ["small"]
    c_idx = lax.axis_index("c")
    c_arr = c_idx.astype(jnp.int32).reshape(1)
    me_arr = me.astype(jnp.int32).reshape(1)
    from_sibling = _swap_halves(blocks)
    halves = [_add_halves(b4, l1, c_arr, "add_halves_" + n) for n, b4, l1 in zip(names, blocks, from_sibling)]
    arrived = _scatter_chips(halves)
    mine = [_sum_chips(l2, q, me_arr, "sum_chips_" + n) for n, l2, q in zip(names, arrived, halves)]
    theirs = _swap_reduced(mine)

    g, deltas, new_m, new_v = {}, {}, {}, {}
    for i, (n, _) in enumerate(BIG):
        g[n], deltas[n], new_m[n], new_v[n] = _adamw_halves(w[n], mine[i], theirs[i], m[n], v[n], c_arr, "adamw_" + n)
    small = jnp.where(c_idx == 0, jnp.concatenate([mine[-1], theirs[-1]]), jnp.concatenate([theirs[-1], mine[-1]]))
    rep_all = _gather_chips(small[:, rows[0]:])
    gs = _unpack_small_reduced(small, rep_all, rows, w)
    for n in WEIGHTS:
        if n in g:
            continue
        g[n] = gs[n].reshape(w[n].shape)
        d2, m2, v2 = _adamw(_as2d(w[n]), _as2d(g[n]), _as2d(m[n]), _as2d(v[n]), "adamw_" + n)
        deltas[n], new_m[n], new_v[n] = d2.reshape(w[n].shape), m2.reshape(w[n].shape), v2.reshape(w[n].shape)
    return (loss, grad_x, *[g[n] for n in WEIGHTS], *[deltas[n] for n in WEIGHTS], *[new_m[n] for n in WEIGHTS],
            *[new_v[n] for n in WEIGHTS])
```

```python
import functools
import math

import jax
import jax.numpy as jnp
from jax import lax
from jax.experimental import pallas as pl
from jax.experimental.pallas import tpu as pltpu

F32 = jnp.float32
BF16 = jnp.bfloat16

D_MODEL = 1024
DN_HEADS = 4
DN_HEAD_DIM = 128
DN_WIDTH = 512
LRU_WIDTH = 512
LRU_C = 8.0
CHUNK = 64
EPS = 1e-6
P_LX, P_LG, P_Q, P_K, P_V, P_Z, P_GATE, P_COLS = 0, 512, 1024, 1536, 2048, 2560, 3072, 3200
N_GATE = 16
LANES = 128
VMEM_LIMIT = 56 * 1024 * 1024
MM_VMEM_BYTES = 40 * 1024 * 1024

ADAM_LR, ADAM_B1, ADAM_B2, ADAM_EPS, ADAM_WD, ADAM_STEP = 0.001, 0.9, 0.999, 1e-08, 0.01, 10


def _cparams(sem):
    return pltpu.CompilerParams(dimension_semantics=sem, vmem_limit_bytes=VMEM_LIMIT)


def _row_tile(m, want=512):
    for t in range(min(want, m) // 16 * 16, 0, -16):
        if m % t == 0:
            return t
    return m


def _lane_divisors(n):
    out = [d for d in range(n, 0, -LANES) if d % LANES == 0 and n % d == 0] if n % LANES == 0 else []
    return out or [n]


def _mm_tiles(m, n, a_row_bytes, k_total, with_res):
    best = None
    for tm in (1024, 512, 256, 128):
        if m % tm:
            continue
        for tn in _lane_divisors(n):
            need = 2 * (tm * a_row_bytes + k_total * tn * 2 + tm * tn * 4 * (2 if with_res else 1))
            if need <= MM_VMEM_BYTES and (best is None or tm * tn > best[0] * best[1]):
                best = (tm, tn)
    return best if best is not None else (_row_tile(m, 128), _lane_divisors(n)[-1])


def _mm_tn_tiles(m, k, n):
    for tn in _lane_divisors(n):
        for tm in (1024, 512, 256, 128):
            if m % tm == 0 and 2 * (k * tn * 4 + tm * k * 2 + tm * tn * 2) <= MM_VMEM_BYTES:
                return tm, tn
    return _row_tile(m, 128), _lane_divisors(n)[-1]


def _bdot(a, b):
    return jnp.dot(a.astype(BF16), b.astype(BF16), preferred_element_type=F32)


def _bdot_nt(a, b):
    return lax.dot_general(a.astype(BF16), b.astype(BF16), (((1,), (1,)), ((), ())), preferred_element_type=F32)


def _bdot_tn(a, b):
    return lax.dot_general(a.astype(BF16), b.astype(BF16), (((0,), (0,)), ((), ())), preferred_element_type=F32)


def _rms(x, g):
    return x * lax.rsqrt(jnp.mean(x * x, axis=-1, keepdims=True) + EPS) * g


def _gelu(x):
    return 0.5 * x * (1.0 + jnp.tanh(0.7978845608028654 * (x + 0.044715 * x * x * x)))


def _sigmoid(x):
    return 1.0 / (1.0 + jnp.exp(-x))


def _silu(x):
    return x * _sigmoid(x)


def _softplus(x):
    return jnp.maximum(x, 0.0) + jnp.log(1.0 + jnp.exp(-jnp.abs(x)))


def _matmul(terms, name, res=None, nt=False):
    norm = lambda op, axis: op if isinstance(op, tuple) else (op, op.shape[axis], 0)
    a_ops = [norm(a, 1) for a, _ in terms]
    b_ops = [norm(b, 1 if nt else 0) for _, b in terms]
    m = a_ops[0][0].shape[0]
    n = b_ops[0][0].shape[0 if nt else 1]
    a_row_bytes = sum(kw * a.dtype.itemsize for a, kw, _ in a_ops)
    tm, tn = _mm_tiles(m, n, a_row_bytes, sum(kw for _, kw, _ in b_ops), res is not None)
    na = len(terms)
    dot = _bdot_nt if nt else _bdot

    def body(*refs):
        a_refs, b_refs = refs[:na], refs[na:2 * na]
        acc = dot(a_refs[0][...], b_refs[0][...])
        for a_ref, b_ref in zip(a_refs[1:], b_refs[1:]):
            acc = acc + dot(a_ref[...], b_ref[...])
        if res is not None:
            acc = acc + refs[2 * na][...]
        refs[-1][...] = acc

    in_specs = [pl.BlockSpec((tm, kw), functools.partial(lambda i, j, kb: (i, kb), kb=kb)) for _, kw, kb in a_ops]
    if nt:
        in_specs += [pl.BlockSpec((tn, kw), functools.partial(lambda i, j, kb: (j, kb), kb=kb)) for _, kw, kb in b_ops]
    else:
        in_specs += [pl.BlockSpec((kw, tn), functools.partial(lambda i, j, kb: (kb, j), kb=kb)) for _, kw, kb in b_ops]
    args = [a for a, _, _ in a_ops] + [b for b, _, _ in b_ops]
    if res is not None:
        in_specs.append(pl.BlockSpec((tm, tn), lambda i, j: (i, j)))
        args.append(res)
    return pl.pallas_call(
        body, name=name, grid=(m // tm, n // tn), in_specs=in_specs,
        out_specs=pl.BlockSpec((tm, tn), lambda i, j: (i, j)),
        out_shape=jax.ShapeDtypeStruct((m, n), F32),
        compiler_params=_cparams(("parallel", "parallel")),
    )(*args)


def _norm_matmul(x, g, b_list, name):
    m, k = x.shape
    b_bytes = sum(2 * b.shape[0] * b.shape[1] * 2 for b in b_list)
    row_bytes = 2 * (k * 4 + k * 2 + sum(b.shape[1] * 4 for b in b_list))
    tm = next((t for t in (1024, 512, 256, 128) if m % t == 0 and b_bytes + t * row_bytes <= MM_VMEM_BYTES), _row_tile(m, 128))
    nb = len(b_list)

    def body(x_ref, g_ref, *refs):
        b_refs, h_ref, o_refs = refs[:nb], refs[nb], refs[nb + 1:]
        h = _rms(x_ref[...], g_ref[...]).astype(BF16)
        h_ref[...] = h
        for b_ref, o_ref in zip(b_refs, o_refs):
            o_ref[...] = jnp.dot(h, b_ref[...], preferred_element_type=F32)

    row = lambda width: pl.BlockSpec((tm, width), lambda i: (i, 0))
    return pl.pallas_call(
        body, name=name, grid=(m // tm,),
        in_specs=[row(k), pl.BlockSpec((1, k), lambda i: (0, 0))] + [pl.BlockSpec(b.shape, lambda i: (0, 0)) for b in b_list],
        out_specs=[row(k)] + [row(b.shape[1]) for b in b_list],
        out_shape=[jax.ShapeDtypeStruct((m, k), BF16)] + [jax.ShapeDtypeStruct((m, b.shape[1]), F32) for b in b_list],
        compiler_params=_cparams(("parallel",)),
    )(x, g, *b_list)


def _matmul_norm_bwd(terms, x, g, dres, name):
    norm = lambda op: op if isinstance(op, tuple) else (op, op.shape[1], 0)
    a_ops = [norm(a) for a, _ in terms]
    b_ops = [norm(b) for _, b in terms]
    m, n = x.shape
    b_bytes = sum(2 * n * kw * 2 for _, kw, _ in b_ops)
    row_bytes = 2 * (sum(kw * 2 for _, kw, _ in a_ops) + n * (4 + 4 + 4 + 2)) + n * 8
    tm = next((t for t in (1024, 512, 256, 128) if m % t == 0 and b_bytes + t * row_bytes <= MM_VMEM_BYTES), _row_tile(m, 128))
    na = len(terms)

    def body(*refs):
        a_refs, b_refs = refs[:na], refs[na:2 * na]
        x_ref, g_ref, dres_ref, dx_ref, dxb_ref, dg_ref = refs[2 * na:]
        dh = _bdot_nt(a_refs[0][...], b_refs[0][...])
        for a_ref, b_ref in zip(a_refs[1:], b_refs[1:]):
            dh = dh + _bdot_nt(a_ref[...], b_ref[...])
        _, vjp = jax.vjp(_rms, x_ref[...], g_ref[...])
        dx, dg = vjp(dh)
        dx = dx + dres_ref[...]
        dx_ref[...] = dx
        dxb_ref[...] = dx.astype(BF16)

        @pl.when(pl.program_id(0) == 0)
        def _():
            dg_ref[...] = jnp.zeros_like(dg_ref)

        dg_ref[...] += dg

    row = pl.BlockSpec((tm, n), lambda i: (i, 0))
    vec = pl.BlockSpec((1, n), lambda i: (0, 0))
    in_specs = [pl.BlockSpec((tm, kw), functools.partial(lambda i, kb: (i, kb), kb=kb)) for _, kw, kb in a_ops]
    in_specs += [pl.BlockSpec((n, kw), functools.partial(lambda i, kb: (0, kb), kb=kb)) for _, kw, kb in b_ops]
    return pl.pallas_call(
        body, name=name, grid=(m // tm,), in_specs=in_specs + [row, vec, row], out_specs=[row, row, vec],
        out_shape=[jax.ShapeDtypeStruct((m, n), F32), jax.ShapeDtypeStruct((m, n), BF16), jax.ShapeDtypeStruct((1, n), F32)],
        compiler_params=_cparams(("arbitrary",)),
    )(*[a for a, _, _ in a_ops], *[b for b, _, _ in b_ops], x, g, dres)


def _matmul_tn_cols(a, b_list, name):
    m, k = a.shape
    n = sum(b.shape[1] for b in b_list)
    tm = next((t for t in (1024, 512, 256, 128) if m % t == 0 and 2 * (k * n * 4 + t * k * 2 + t * n * 2) <= MM_VMEM_BYTES),
              _row_tile(m, 128))

    def body(a_ref, *refs):
        b_refs, o_ref = refs[:-1], refs[-1]

        @pl.when(pl.program_id(0) == 0)
        def _():
            o_ref[...] = jnp.zeros_like(o_ref)

        off = 0
        for b_ref in b_refs:
            w = b_ref.shape[1]
            o_ref[:, off:off + w] += _bdot_tn(a_ref[...], b_ref[...])
            off += w

    return pl.pallas_call(
        body, name=name, grid=(m // tm,),
        in_specs=[pl.BlockSpec((tm, k), lambda i: (i, 0))] + [pl.BlockSpec((tm, b.shape[1]), lambda i: (i, 0)) for b in b_list],
        out_specs=pl.BlockSpec((k, n), lambda i: (0, 0)), out_shape=jax.ShapeDtypeStruct((k, n), F32),
        compiler_params=_cparams(("arbitrary",)),
    )(a, *b_list)


def _matmul_tn(a, b, name):
    m, k = a.shape
    n = b.shape[1]
    tm, tn = _mm_tn_tiles(m, k, n)

    def body(a_ref, b_ref, o_ref):
        @pl.when(pl.program_id(1) == 0)
        def _():
            o_ref[...] = jnp.zeros_like(o_ref)

        o_ref[...] += _bdot_tn(a_ref[...], b_ref[...])

    return pl.pallas_call(
        body, name=name, grid=(n // tn, m // tm),
        in_specs=[pl.BlockSpec((tm, k), lambda j, i: (i, 0)), pl.BlockSpec((tm, tn), lambda j, i: (i, j))],
        out_specs=pl.BlockSpec((k, tn), lambda j, i: (0, j)),
        out_shape=jax.ShapeDtypeStruct((k, n), F32),
        compiler_params=_cparams(("parallel", "arbitrary")),
    )(a, b)


def _shift_down(x, k):
    row = lax.broadcasted_iota(jnp.int32, x.shape, 0)
    return jnp.where(row >= k, pltpu.roll(x, k, 0), 0.0)


def _shift_up(x, k):
    s = x.shape[0]
    row = lax.broadcasted_iota(jnp.int32, x.shape, 0)
    return jnp.where(row < s - k, pltpu.roll(x, s - k, 0), 0.0)


def _conv_taps(x, ntaps, left):
    out = []
    for j in range(ntaps):
        off = j - left
        out.append(_shift_down(x, -off) if off < 0 else (_shift_up(x, off) if off > 0 else x))
    return out


def _conv_fwd(x, w, left):
    taps = _conv_taps(x, w.shape[0], left)
    acc = taps[0] * w[0:1, :]
    for j in range(1, w.shape[0]):
        acc = acc + taps[j] * w[j:j + 1, :]
    return acc


def _conv_bwd(x, w, left, dout):
    ntaps = w.shape[0]
    dx = None
    for j in range(ntaps):
        off = j - left
        sh = _shift_up(dout, -off) if off < 0 else (_shift_down(dout, off) if off > 0 else dout)
        term = sh * w[j:j + 1, :]
        dx = term if dx is None else dx + term
    taps = _conv_taps(x, ntaps, left)
    dw = jnp.concatenate([jnp.sum(dout * tp, axis=0, keepdims=True) for tp in taps], axis=0)
    return dx, dw


SCAN_BLOCK = 32


def _scan(a, b, reverse):
    s = a.shape[0]
    blk = SCAN_BLOCK if s % SCAN_BLOCK == 0 else s
    row = lax.broadcasted_iota(jnp.int32, a.shape, 0)
    pos = row & (blk - 1) if blk & (blk - 1) == 0 else row % blk
    d = 1
    while d < blk:
        if reverse:
            keep = pos < blk - d
            sb, sa = pltpu.roll(b, s - d, 0), pltpu.roll(a, s - d, 0)
        else:
            keep = pos >= d
            sb, sa = pltpu.roll(b, d, 0), pltpu.roll(a, d, 0)
        b = a * jnp.where(keep, sb, 0.0) + b
        a = a * jnp.where(keep, sa, 1.0)
        d *= 2
    nblk = s // blk
    if nblk == 1:
        return b
    carry = jnp.zeros((1, a.shape[1]), F32)
    carries = [None] * nblk
    order = range(nblk - 1, -1, -1) if reverse else range(nblk)
    for j in order:
        carries[j] = carry
        last = j * blk if reverse else (j + 1) * blk - 1
        carry = b[last:last + 1, :] + a[last:last + 1, :] * carry
    carry_in = jnp.concatenate([jnp.broadcast_to(cj, (blk, a.shape[1])) for cj in carries], axis=0)
    return b + a * carry_in


def _dn_gates_fn(pre, alog, dtb):
    lane = lax.broadcasted_iota(jnp.int32, pre.shape, 1)
    beta = _sigmoid(pre)
    g = -jnp.exp(alog) * _softplus(pre + dtb)
    return jnp.where(lane < N_GATE // 2, beta, jnp.where(lane < N_GATE, g, 0.0))


def _dn_gates_fwd(proj, alog, dtb, name):
    t = proj.shape[0]
    tm = _row_tile(t)
    cb = P_GATE // LANES

    def body(p_ref, a_ref, d_ref, o_ref):
        o_ref[...] = _dn_gates_fn(p_ref[...], a_ref[...], d_ref[...])

    return pl.pallas_call(
        body, name=name, grid=(t // tm,),
        in_specs=[pl.BlockSpec((tm, LANES), lambda i: (i, cb)), pl.BlockSpec((1, LANES), lambda i: (0, 0)),
                  pl.BlockSpec((1, LANES), lambda i: (0, 0))],
        out_specs=pl.BlockSpec((tm, LANES), lambda i: (i, 0)),
        out_shape=jax.ShapeDtypeStruct((t, LANES), F32),
        compiler_params=_cparams(("parallel",)),
    )(proj, alog, dtb)


def _dn_gates_bwd(proj, alog, dtb, dgb_f, dgb_b, name):
    t = proj.shape[0]
    tm = _row_tile(t)
    cb = P_GATE // LANES

    def body(p_ref, a_ref, d_ref, g1_ref, g2_ref, dp_ref, da_ref, dd_ref):
        _, vjp = jax.vjp(_dn_gates_fn, p_ref[...], a_ref[...], d_ref[...])
        dp, da, dd = vjp(g1_ref[...] + g2_ref[...])
        dp_ref[...] = dp.astype(dp_ref.dtype)

        @pl.when(pl.program_id(0) == 0)
        def _():
            da_ref[...] = jnp.zeros_like(da_ref)
            dd_ref[...] = jnp.zeros_like(dd_ref)

        da_ref[...] += da
        dd_ref[...] += dd

    row = pl.BlockSpec((tm, LANES), lambda i: (i, 0))
    vec = pl.BlockSpec((1, LANES), lambda i: (0, 0))
    return pl.pallas_call(
        body, name=name, grid=(t // tm,),
        in_specs=[pl.BlockSpec((tm, LANES), lambda i: (i, cb)), vec, vec, row, row],
        out_specs=[row, vec, vec],
        out_shape=[jax.ShapeDtypeStruct((t, LANES), BF16), jax.ShapeDtypeStruct((1, LANES), F32),
                   jax.ShapeDtypeStruct((1, LANES), F32)],
        compiler_params=_cparams(("arbitrary",)),
    )(proj, alog, dtb, dgb_f, dgb_b)


def _dn_prep_fn(x, w, is_qk):
    act = _silu(_conv_fwd(x, w, 2))
    nrm = act * lax.rsqrt(jnp.sum(act * act, axis=-1, keepdims=True) + EPS)
    return jnp.where(is_qk, nrm, act)


def _dn_prep_fwd(proj3, conv_w, name):
    bsz, s, _ = proj3.shape
    nblk = 3 * DN_WIDTH // LANES
    cb = P_Q // LANES

    def body(x_ref, w_ref, o_ref):
        o_ref[0] = _dn_prep_fn(x_ref[0], w_ref[...], pl.program_id(1) < 2 * DN_HEADS)

    return pl.pallas_call(
        body, name=name, grid=(bsz, nblk),
        in_specs=[pl.BlockSpec((1, s, LANES), lambda b, j: (b, 0, cb + j)), pl.BlockSpec((4, LANES), lambda b, j: (0, j))],
        out_specs=pl.BlockSpec((1, s, LANES), lambda b, j: (b, 0, j)),
        out_shape=jax.ShapeDtypeStruct((bsz, s, 3 * DN_WIDTH), F32),
        compiler_params=_cparams(("parallel", "parallel")),
    )(proj3, conv_w)


def _dn_prep_bwd(proj3, conv_w, dqkv_f, dqkv_b, name):
    bsz, s, _ = proj3.shape
    nblk = 3 * DN_WIDTH // LANES
    cb = P_Q // LANES

    def body(x_ref, w_ref, d1_ref, d2_ref, dx_ref, dw_ref):
        x, w, d = x_ref[0], w_ref[...], d1_ref[0] + d2_ref[0]
        is_qk = pl.program_id(0) < 2 * DN_HEADS
        pre = _conv_fwd(x, w, 2)

        def post(pre):
            act = _silu(pre)
            nrm = act * lax.rsqrt(jnp.sum(act * act, axis=-1, keepdims=True) + EPS)
            return jnp.where(is_qk, nrm, act)

        _, vjp = jax.vjp(post, pre)
        (dpre,) = vjp(d)
        dx, dw = _conv_bwd(x, w, 2, dpre)
        dx_ref[0] = dx.astype(dx_ref.dtype)

        @pl.when(pl.program_id(1) == 0)
        def _():
            dw_ref[...] = jnp.zeros_like(dw_ref)

        dw_ref[...] += dw

    col = pl.BlockSpec((1, s, LANES), lambda j, b: (b, 0, j))
    return pl.pallas_call(
        body, name=name, grid=(nblk, bsz),
        in_specs=[pl.BlockSpec((1, s, LANES), lambda j, b: (b, 0, cb + j)), pl.BlockSpec((4, LANES), lambda j, b: (0, j)), col, col],
        out_specs=[col, pl.BlockSpec((4, LANES), lambda j, b: (0, j))],
        out_shape=[jax.ShapeDtypeStruct((bsz, s, 3 * DN_WIDTH), BF16), jax.ShapeDtypeStruct((4, 3 * DN_WIDTH), F32)],
        compiler_params=_cparams(("parallel", "arbitrary")),
    )(proj3, conv_w, dqkv_f, dqkv_b)


def _parts(x, n):
    out = []
    for _ in range(n):
        bits = lax.bitcast_convert_type(x, jnp.uint32) & jnp.uint32(0xFFFF0000)
        t = lax.bitcast_convert_type(bits, F32)
        out.append(t.astype(BF16))
        x = x - t
    return out


def _dg(x, y, cx, cy):
    return lax.dot_general(x, y, (((cx + 1,), (cy + 1,)), ((0,), (0,))), preferred_element_type=F32)


def _bmm(a, b):
    return _dg(a.astype(BF16), b.astype(BF16), 1, 0)


def _bmm_nt(a, b):
    return _dg(a.astype(BF16), b.astype(BF16), 1, 1)


def _bmm_tn(a, b):
    return _dg(a.astype(BF16), b.astype(BF16), 0, 0)


def _dot3_raw(a, b, ca, cb):
    a_hi, a_lo = _parts(a, 2)
    b_hi, b_lo = _parts(b, 2)
    return _dg(a_hi, b_hi, ca, cb) + (_dg(a_hi, b_lo, ca, cb) + _dg(a_lo, b_hi, ca, cb))


@jax.custom_vjp
def _sum_left(m, x):
    return sum(_dg(m, pt, 1, 0) for pt in _parts(x, 3))


def _sum_left_fwd(m, x):
    return _sum_left(m, x), m


def _sum_left_bwd(m, ct):
    return jnp.zeros_like(m), sum(_dg(m, pt, 0, 0) for pt in _parts(ct, 2))


_sum_left.defvjp(_sum_left_fwd, _sum_left_bwd)


@jax.custom_vjp
def _sum_right(x, m):
    return sum(_dg(pt, m, 0, 0) for pt in _parts(x, 3))


def _sum_right_fwd(x, m):
    return _sum_right(x, m), m


def _sum_right_bwd(m, ct):
    return sum(_dg(m, pt, 1, 1) for pt in _parts(ct, 2)), jnp.zeros_like(m)


_sum_right.defvjp(_sum_right_fwd, _sum_right_bwd)


def _unit_tri_inverse_raw(a):
    nu, c, _ = a.shape
    row = lax.broadcasted_iota(jnp.int32, (nu, c, c), 1)
    col = lax.broadcasted_iota(jnp.int32, (nu, c, c), 2)
    same = lambda n: (row // n) == (col // n)
    d = jnp.where(same(8), a, 0.0)
    tinv = jnp.where(row == col, 1.0, 0.0) - d
    pw = _dot3_raw(d, d, 1, 0)
    tinv = tinv + _dot3_raw(tinv, pw, 1, 0)
    pw = _dot3_raw(pw, pw, 1, 0)
    tinv = tinv + _dot3_raw(tinv, pw, 1, 0)
    n = 8
    while n < c:
        e = jnp.where(same(2 * n) & jnp.logical_not(same(n)), a, 0.0)
        tinv = tinv - _bmm(tinv, _bmm(e, tinv))
        n *= 2
    return tinv


@jax.custom_vjp
def _unit_tri_inverse(a):
    return _unit_tri_inverse_raw(a)


def _unit_tri_inverse_fwd(a):
    tinv = _unit_tri_inverse_raw(a)
    return tinv, tinv


def _unit_tri_inverse_bwd(tinv, ct):
    return (-_bmm_nt(_bmm_tn(tinv, ct), tinv),)


_unit_tri_inverse.defvjp(_unit_tri_inverse_fwd, _unit_tri_inverse_bwd)


def _dn_intra(q, k, v, g, beta, rev):
    nu, c, _ = q.shape
    row = lax.broadcasted_iota(jnp.int32, (nu, c, c), 1)
    col = lax.broadcasted_iota(jnp.int32, (nu, c, c), 2)
    incl = (row <= col) if rev else (row >= col)
    strict = (row < col) if rev else (row > col)
    ones_incl = jnp.where(incl, 1.0, 0.0).astype(BF16)
    ones_tr = jnp.where((row >= col) if rev else (row <= col), 1.0, 0.0).astype(BF16)
    gbc = jnp.broadcast_to(g, (nu, c, c))
    gc = _sum_left(ones_incl, gbc)
    gr = _sum_right(gbc, ones_tr)
    gcum = gc[:, :, 0:1]
    decay = jnp.where(incl, jnp.exp(jnp.where(incl, gc - gr, 0.0)), 0.0)
    qs = q * (DN_HEAD_DIM ** -0.5)
    kb = k * beta
    a = jnp.where(strict, _bmm_nt(kb, k) * decay, 0.0)
    tinv = _unit_tri_inverse(a)
    egc = jnp.exp(gcum)
    u = _bmm(tinv, v * beta)
    w = _bmm(tinv, kb * egc)
    attn = _bmm_nt(qs, k) * decay
    glast = gcum[:, 0:1, :] if rev else gcum[:, c - 1:c, :]
    q_dec = qs * egc
    k_dec = k * jnp.exp(glast - gcum)
    cdec = jnp.broadcast_to(jnp.exp(glast), (nu, 1, LANES))
    return u, w, q_dec, k_dec, attn, cdec


def _dn_rec(u, w, q_dec, k_dec, attn, cdec, state):
    v_new = u - _bmm(w, state)
    o = _bmm(q_dec, state) + _bmm(attn, v_new)
    return o, state * cdec + _bmm_tn(k_dec, v_new)


DN_INTRA_TOKENS = 512
DN_REC_TOKENS = 512
DN_REC_BWD_TOKENS = 256
DN_REC_EXAMPLES = 4


def _dn_gate_lanes(rev, h):
    lb = (DN_HEADS if rev else 0) + h
    return lb, N_GATE // 2 + lb


def _dn_intra_shapes(bsz, s):
    n = s // CHUNK
    return [jax.ShapeDtypeStruct((bsz, s, DN_WIDTH), F32), jax.ShapeDtypeStruct((bsz, s, DN_WIDTH), BF16),
            jax.ShapeDtypeStruct((bsz, s, DN_WIDTH), BF16), jax.ShapeDtypeStruct((bsz, s, DN_WIDTH), BF16),
            jax.ShapeDtypeStruct((bsz, n, DN_HEADS, CHUNK, CHUNK), BF16), jax.ShapeDtypeStruct((bsz, n, DN_HEADS, 1, LANES), F32)]


def _dn_intra_specs(tb, ix):
    nc = tb // CHUNK
    ix5 = lambda b, j: ix(b, j) + (0, 0)
    row = pl.BlockSpec((1, tb, DN_WIDTH), ix)
    return [row, row, row, row, pl.BlockSpec((1, nc, DN_HEADS, CHUNK, CHUNK), ix5), pl.BlockSpec((1, nc, DN_HEADS, 1, LANES), ix5)]


def _dn_units(nc):
    return [(ci, h) for ci in range(nc) for h in range(DN_HEADS)]


def _dn_load_units(qkv_ref, gb_ref, nc, rev):
    qs, ks, vs, gs, bs = [], [], [], [], []
    for ci, h in _dn_units(nc):
        rows = slice(ci * CHUNK, (ci + 1) * CHUNK)
        lb, lg = _dn_gate_lanes(rev, h)
        qs.append(qkv_ref[0, rows, h * LANES:(h + 1) * LANES])
        ks.append(qkv_ref[0, rows, DN_WIDTH + h * LANES:DN_WIDTH + (h + 1) * LANES])
        vs.append(qkv_ref[0, rows, 2 * DN_WIDTH + h * LANES:2 * DN_WIDTH + (h + 1) * LANES])
        gs.append(gb_ref[0, rows, lg:lg + 1])
        bs.append(gb_ref[0, rows, lb:lb + 1])
    return jnp.stack(qs), jnp.stack(ks), jnp.stack(vs), jnp.stack(gs), jnp.stack(bs)


def _dn_intra_fwd(qkv, gb, rev, name):
    bsz, s, _ = qkv.shape
    tb = min(DN_INTRA_TOKENS, s)
    nc = tb // CHUNK

    def body(qkv_ref, gb_ref, u_ref, w_ref, qd_ref, kd_ref, at_ref, cd_ref):
        q, k, v, g, beta = _dn_load_units(qkv_ref, gb_ref, nc, rev)
        u, w, qd, kd, at, cd = _dn_intra(q, k, v, g, beta, rev)
        for i, (ci, h) in enumerate(_dn_units(nc)):
            rows = slice(ci * CHUNK, (ci + 1) * CHUNK)
            cols = slice(h * LANES, (h + 1) * LANES)
            u_ref[0, rows, cols] = u[i]
            w_ref[0, rows, cols] = w[i].astype(BF16)
            qd_ref[0, rows, cols] = qd[i].astype(BF16)
            kd_ref[0, rows, cols] = kd[i].astype(BF16)
            at_ref[0, ci, h] = at[i].astype(BF16)
            cd_ref[0, ci, h] = cd[i]

    ix = lambda b, j: (b, j, 0)
    return pl.pallas_call(
        body, name=name, grid=(bsz, s // tb),
        in_specs=[pl.BlockSpec((1, tb, 3 * DN_WIDTH), ix), pl.BlockSpec((1, tb, LANES), ix)],
        out_specs=_dn_intra_specs(tb, ix), out_shape=_dn_intra_shapes(bsz, s),
        compiler_params=_cparams(("parallel", "parallel")),
    )(qkv, gb)


def _dn_intra_bwd(qkv, gb, cts, rev, name):
    bsz, s, _ = qkv.shape
    tb = min(DN_INTRA_TOKENS, s)
    nc = tb // CHUNK

    def body(qkv_ref, gb_ref, du_ref, dw_ref, dqd_ref, dkd_ref, dat_ref, dcd_ref, dqkv_ref, dgb_ref):
        units = _dn_units(nc)
        q, k, v, g, beta = _dn_load_units(qkv_ref, gb_ref, nc, rev)
        _, vjp = jax.vjp(functools.partial(_dn_intra, rev=rev), q, k, v, g, beta)
        tok = lambda ref: jnp.stack([ref[0, ci * CHUNK:(ci + 1) * CHUNK, h * LANES:(h + 1) * LANES] for ci, h in units])
        per = lambda ref: jnp.stack([ref[0, ci, h] for ci, h in units])
        dq, dk, dv, dg, dbeta = vjp((tok(du_ref), tok(dw_ref), tok(dqd_ref), tok(dkd_ref), per(dat_ref), per(dcd_ref)))
        lane = lax.broadcasted_iota(jnp.int32, (CHUNK, LANES), 1)
        for ci in range(nc):
            rows = slice(ci * CHUNK, (ci + 1) * CHUNK)
            dgates = jnp.zeros((CHUNK, LANES), F32)
            for h in range(DN_HEADS):
                i = units.index((ci, h))
                lb, lg = _dn_gate_lanes(rev, h)
                dqkv_ref[0, rows, h * LANES:(h + 1) * LANES] = dq[i]
                dqkv_ref[0, rows, DN_WIDTH + h * LANES:DN_WIDTH + (h + 1) * LANES] = dk[i]
                dqkv_ref[0, rows, 2 * DN_WIDTH + h * LANES:2 * DN_WIDTH + (h + 1) * LANES] = dv[i]
                dgates = dgates + jnp.where(lane == lb, dbeta[i], 0.0) + jnp.where(lane == lg, dg[i], 0.0)
            dgb_ref[0, rows, :] = dgates

    ix = lambda b, j: (b, j, 0)
    ix5 = lambda b, j: (b, j, 0, 0, 0)
    row = pl.BlockSpec((1, tb, DN_WIDTH), ix)
    return pl.pallas_call(
        body, name=name, grid=(bsz, s // tb),
        in_specs=[pl.BlockSpec((1, tb, 3 * DN_WIDTH), ix), pl.BlockSpec((1, tb, LANES), ix), row, row, row, row,
                  pl.BlockSpec((1, nc, DN_HEADS, CHUNK, CHUNK), ix5), pl.BlockSpec((1, nc, DN_HEADS, 1, LANES), ix5)],
        out_specs=[pl.BlockSpec((1, tb, 3 * DN_WIDTH), ix), pl.BlockSpec((1, tb, LANES), ix)],
        out_shape=[jax.ShapeDtypeStruct((bsz, s, 3 * DN_WIDTH), F32), jax.ShapeDtypeStruct((bsz, s, LANES), F32)],
        compiler_params=_cparams(("parallel", "parallel")),
    )(qkv, gb, *cts)


def _dn_rec_examples(bsz):
    return max(n for n in range(1, DN_REC_EXAMPLES + 1) if bsz % n == 0)


def _dn_rec_specs(nb, tb, ix):
    nc = tb // CHUNK
    ix5 = lambda b, j: ix(b, j) + (0, 0)
    row = pl.BlockSpec((nb, tb, DN_WIDTH), ix)
    return [row, row, row, row, pl.BlockSpec((nb, nc, DN_HEADS, CHUNK, CHUNK), ix5), pl.BlockSpec((nb, nc, DN_HEADS, 1, LANES), ix5)]


def _dn_rec_fwd(intra, rev, name):
    u = intra[0]
    bsz, s, _ = u.shape
    nb = _dn_rec_examples(bsz)
    tb = min(DN_REC_TOKENS, s)
    nt = s // tb
    nc = tb // CHUNK
    nh = nb * DN_HEADS

    def body(u_ref, w_ref, qd_ref, kd_ref, at_ref, cd_ref, o_ref, st_ref, state):
        @pl.when(pl.program_id(1) == 0)
        def _():
            state[...] = jnp.zeros_like(state)

        def step(ci, carry):
            cidx = (nc - 1 - ci) if rev else ci
            rows = pl.ds(pl.multiple_of(cidx * CHUNK, CHUNK), CHUNK)
            heads = lambda ref: jnp.stack([ref[e, rows, h * LANES:(h + 1) * LANES] for e in range(nb) for h in range(DN_HEADS)])
            per = lambda ref: jnp.concatenate([ref[e, cidx] for e in range(nb)])
            st = state[...]
            o, new_state = _dn_rec(heads(u_ref), heads(w_ref), heads(qd_ref), heads(kd_ref), per(at_ref), per(cd_ref), st)
            state[...] = new_state
            for e in range(nb):
                st_ref[e, cidx] = st[e * DN_HEADS:(e + 1) * DN_HEADS]
                o_ref[e, rows, :] = jnp.concatenate([o[e * DN_HEADS + h] for h in range(DN_HEADS)], axis=-1)
            return carry

        lax.fori_loop(0, nc, step, 0)

    ix = (lambda b, j: (b, nt - 1 - j, 0)) if rev else (lambda b, j: (b, j, 0))
    ix5 = lambda b, j: ix(b, j) + (0, 0)
    return pl.pallas_call(
        body, name=name, grid=(bsz // nb, nt), in_specs=_dn_rec_specs(nb, tb, ix),
        out_specs=[pl.BlockSpec((nb, tb, DN_WIDTH), ix), pl.BlockSpec((nb, nc, DN_HEADS, DN_HEAD_DIM, DN_HEAD_DIM), ix5)],
        out_shape=[jax.ShapeDtypeStruct((bsz, s, DN_WIDTH), F32),
                   jax.ShapeDtypeStruct((bsz, s // CHUNK, DN_HEADS, DN_HEAD_DIM, DN_HEAD_DIM), F32)],
        scratch_shapes=[pltpu.VMEM((nh, DN_HEAD_DIM, DN_HEAD_DIM), F32)],
        compiler_params=_cparams(("parallel", "arbitrary")),
    )(*intra)


def _dn_rec_bwd(intra, states, do, rev, name):
    u = intra[0]
    bsz, s, _ = u.shape
    nb = _dn_rec_examples(bsz)
    tb = min(DN_REC_BWD_TOKENS, s)
    nt = s // tb
    nc = tb // CHUNK
    nh = nb * DN_HEADS

    def body(u_ref, w_ref, qd_ref, kd_ref, at_ref, cd_ref, st_ref, do_ref,
             du_ref, dw_ref, dqd_ref, dkd_ref, dat_ref, dcd_ref, dstate):
        @pl.when(pl.program_id(1) == 0)
        def _():
            dstate[...] = jnp.zeros_like(dstate)

        def step(ci, carry):
            cidx = ci if rev else (nc - 1 - ci)
            rows = pl.ds(pl.multiple_of(cidx * CHUNK, CHUNK), CHUNK)
            heads = lambda ref: jnp.stack([ref[e, rows, h * LANES:(h + 1) * LANES] for e in range(nb) for h in range(DN_HEADS)])
            per = lambda ref: jnp.concatenate([ref[e, cidx] for e in range(nb)])
            args = (heads(u_ref), heads(w_ref).astype(F32), heads(qd_ref).astype(F32), heads(kd_ref).astype(F32),
                    per(at_ref).astype(F32), per(cd_ref), per(st_ref))
            _, vjp = jax.vjp(_dn_rec, *args)
            du, dw, dqd, dkd, dat, dcd, dst = vjp((heads(do_ref), dstate[...]))
            dstate[...] = dst
            for e in range(nb):
                hs = slice(e * DN_HEADS, (e + 1) * DN_HEADS)
                dat_ref[e, cidx] = dat[hs]
                dcd_ref[e, cidx] = dcd[hs]
                for ref, val in ((du_ref, du), (dw_ref, dw), (dqd_ref, dqd), (dkd_ref, dkd)):
                    ref[e, rows, :] = jnp.concatenate([val[e * DN_HEADS + h] for h in range(DN_HEADS)], axis=-1)
            return carry

        lax.fori_loop(0, nc, step, 0)

    ix = (lambda b, j: (b, j, 0)) if rev else (lambda b, j: (b, nt - 1 - j, 0))
    ix5 = lambda b, j: ix(b, j) + (0, 0)
    row = pl.BlockSpec((nb, tb, DN_WIDTH), ix)
    f32 = lambda sd: jax.ShapeDtypeStruct(sd.shape, F32)
    return pl.pallas_call(
        body, name=name, grid=(bsz // nb, nt),
        in_specs=_dn_rec_specs(nb, tb, ix) + [pl.BlockSpec((nb, nc, DN_HEADS, DN_HEAD_DIM, DN_HEAD_DIM), ix5), row],
        out_specs=_dn_rec_specs(nb, tb, ix), out_shape=[f32(sd) for sd in _dn_intra_shapes(bsz, s)],
        scratch_shapes=[pltpu.VMEM((nh, DN_HEAD_DIM, DN_HEAD_DIM), F32)],
        compiler_params=_cparams(("parallel", "arbitrary")),
    )(*intra, states, do)


def _lru_gate_fn(xc, wa, ba, wx, bx, lam):
    r = _sigmoid(_bdot(xc, wa) + ba)
    ig = _sigmoid(_bdot(xc, wx) + bx)
    log_a = -LRU_C * r * _softplus(-lam)
    a = jnp.exp(log_a)
    b = jnp.sqrt(-jnp.tanh(log_a) * (a * a + 1.0)) * (ig * xc)
    return a, b


def _lru_fwd(proj3, conv_w, conv_b, wa, ba, wx, bx, lam, name):
    bsz, s, _ = proj3.shape
    nblk = LRU_WIDTH // LANES

    def body(x_ref, cw_ref, cb_ref, wa_ref, ba_ref, wx_ref, bx_ref, lam_ref, hf_ref, hb_ref):
        xc = _conv_fwd(x_ref[0], cw_ref[...], 2) + cb_ref[...]
        for d, h_ref in ((0, hf_ref), (1, hb_ref)):
            a, b = _lru_gate_fn(xc, wa_ref[d, 0], ba_ref[d:d + 1, :], wx_ref[d, 0], bx_ref[d:d + 1, :], lam_ref[d:d + 1, :])
            h_ref[0] = _scan(a, b, reverse=(d == 1))

    col = pl.BlockSpec((1, s, LANES), lambda b, j: (b, 0, j))
    vec2 = pl.BlockSpec((2, LANES), lambda b, j: (0, j))
    wspec = pl.BlockSpec((2, 1, LANES, LANES), lambda b, j: (0, j, 0, 0))
    return pl.pallas_call(
        body, name=name, grid=(bsz, nblk),
        in_specs=[col, pl.BlockSpec((4, LANES), lambda b, j: (0, j)), pl.BlockSpec((1, LANES), lambda b, j: (0, j)),
                  wspec, vec2, wspec, vec2, vec2],
        out_specs=[col, col],
        out_shape=[jax.ShapeDtypeStruct((bsz, s, LRU_WIDTH), F32)] * 2,
        compiler_params=_cparams(("parallel", "parallel")),
    )(proj3, conv_w, conv_b, wa, ba, wx, bx, lam)


def _lru_bwd(proj3, conv_w, conv_b, wa, ba, wx, bx, lam, hf, hb, dh, name):
    bsz, s, _ = proj3.shape
    nblk = LRU_WIDTH // LANES

    def body(x_ref, cw_ref, cb_ref, wa_ref, ba_ref, wx_ref, bx_ref, lam_ref, hf_ref, hb_ref, dh_ref,
             dx_ref, dcw_ref, dcb_ref, dwa_ref, dba_ref, dwx_ref, dbx_ref, dlam_ref):
        @pl.when(pl.program_id(1) == 0)
        def _():
            for r in (dcw_ref, dcb_ref, dwa_ref, dba_ref, dwx_ref, dbx_ref, dlam_ref):
                r[...] = jnp.zeros_like(r)

        x, cw = x_ref[0], cw_ref[...]
        xc = _conv_fwd(x, cw, 2) + cb_ref[...]
        dhv = dh_ref[0]
        dxc = jnp.zeros_like(xc)
        for d, h_ref in ((0, hf_ref), (1, hb_ref)):
            rev = d == 1
            args = (xc, wa_ref[d, 0].astype(F32), ba_ref[d:d + 1, :], wx_ref[d, 0].astype(F32), bx_ref[d:d + 1, :],
                    lam_ref[d:d + 1, :])
            (a, _), vjp = jax.vjp(_lru_gate_fn, *args)
            h = h_ref[0]
            a_next = _shift_down(a, 1) if rev else _shift_up(a, 1)
            lam_adj = _scan(a_next, dhv, reverse=not rev)
            h_prev = _shift_up(h, 1) if rev else _shift_down(h, 1)
            dxc_d, dwa, dba, dwx, dbx, dlam = vjp((lam_adj * h_prev, lam_adj))
            dxc = dxc + dxc_d
            dwa_ref[d, 0] += dwa
            dwx_ref[d, 0] += dwx
            dba_ref[d:d + 1, :] += dba
            dbx_ref[d:d + 1, :] += dbx
            dlam_ref[d:d + 1, :] += dlam
        dx, dcw = _conv_bwd(x, cw, 2, dxc)
        dx_ref[0] = dx.astype(dx_ref.dtype)
        dcw_ref[...] += dcw
        dcb_ref[...] += jnp.sum(dxc, axis=0, keepdims=True)

    col = pl.BlockSpec((1, s, LANES), lambda j, b: (b, 0, j))
    vec1 = pl.BlockSpec((1, LANES), lambda j, b: (0, j))
    vec2 = pl.BlockSpec((2, LANES), lambda j, b: (0, j))
    vec4 = pl.BlockSpec((4, LANES), lambda j, b: (0, j))
    wspec = pl.BlockSpec((2, 1, LANES, LANES), lambda j, b: (0, j, 0, 0))
    wshape = jax.ShapeDtypeStruct((2, nblk, LANES, LANES), F32)
    v2shape = jax.ShapeDtypeStruct((2, LRU_WIDTH), F32)
    return pl.pallas_call(
        body, name=name, grid=(nblk, bsz),
        in_specs=[col, vec4, vec1, wspec, vec2, wspec, vec2, vec2, col, col, col],
        out_specs=[col, vec4, vec1, wspec, vec2, wspec, vec2, vec2],
        out_shape=[jax.ShapeDtypeStruct((bsz, s, LRU_WIDTH), BF16), jax.ShapeDtypeStruct((4, LRU_WIDTH), F32),
                   jax.ShapeDtypeStruct((1, LRU_WIDTH), F32), wshape, v2shape, wshape, v2shape, v2shape],
        compiler_params=_cparams(("parallel", "arbitrary")),
    )(proj3, conv_w, conv_b, wa, ba, wx, bx, lam, hf, hb, dh)


def _mix_fn(o_f, o_b, z, lg, hf, hb, dn_g, lru_g):
    osum = o_f + o_b
    heads = []
    for h in range(DN_HEADS):
        sl = slice(h * LANES, (h + 1) * LANES)
        heads.append(_rms(osum[:, sl], dn_g) * _silu(z[:, sl]))
    lru = _rms(_gelu(lg) * (hf + hb), lru_g)
    return jnp.concatenate(heads + [lru], axis=-1)


def _mix_specs(tm):
    w = DN_WIDTH
    row = pl.BlockSpec((tm, w), lambda i: (i, 0))
    z = pl.BlockSpec((tm, w), lambda i: (i, P_Z // w))
    lg = pl.BlockSpec((tm, w), lambda i: (i, P_LG // w))
    return [row, row, z, lg, row, row, pl.BlockSpec((1, LANES), lambda i: (0, 0)), pl.BlockSpec((1, w), lambda i: (0, 0))]


def _mix_fwd(o_f, o_b, proj, hf, hb, dn_g, lru_g, name):
    t = proj.shape[0]
    tm = _row_tile(t)

    def body(of_ref, ob_ref, z_ref, lg_ref, hf_ref, hb_ref, dg_ref, lgn_ref, o_ref):
        o_ref[...] = _mix_fn(of_ref[...], ob_ref[...], z_ref[...], lg_ref[...], hf_ref[...], hb_ref[...],
                             dg_ref[...], lgn_ref[...]).astype(o_ref.dtype)

    return pl.pallas_call(
        body, name=name, grid=(t // tm,), in_specs=_mix_specs(tm),
        out_specs=pl.BlockSpec((tm, D_MODEL), lambda i: (i, 0)),
        out_shape=jax.ShapeDtypeStruct((t, D_MODEL), BF16),
        compiler_params=_cparams(("parallel",)),
    )(o_f, o_b, proj, proj, hf, hb, dn_g, lru_g)


def _mix_bwd(o_f, o_b, proj, hf, hb, dn_g, lru_g, dmix, name):
    t = proj.shape[0]
    tm = _row_tile(t)

    def body(of_ref, ob_ref, z_ref, lg_ref, hf_ref, hb_ref, dg_ref, lgn_ref, dm_ref,
             do_ref, dz_ref, dlg_ref, dh_ref, ddg_ref, dlgn_ref):
        _, vjp = jax.vjp(_mix_fn, of_ref[...], ob_ref[...], z_ref[...], lg_ref[...], hf_ref[...], hb_ref[...],
                         dg_ref[...], lgn_ref[...])
        do, _, dz, dlg, dh, _, ddg, dlgn = vjp(dm_ref[...])
        do_ref[...] = do
        dz_ref[...] = dz.astype(dz_ref.dtype)
        dlg_ref[...] = dlg.astype(dlg_ref.dtype)
        dh_ref[...] = dh

        @pl.when(pl.program_id(0) == 0)
        def _():
            ddg_ref[...] = jnp.zeros_like(ddg_ref)
            dlgn_ref[...] = jnp.zeros_like(dlgn_ref)

        ddg_ref[...] += ddg
        dlgn_ref[...] += dlgn

    row = pl.BlockSpec((tm, DN_WIDTH), lambda i: (i, 0))
    return pl.pallas_call(
        body, name=name, grid=(t // tm,),
        in_specs=_mix_specs(tm) + [pl.BlockSpec((tm, D_MODEL), lambda i: (i, 0))],
        out_specs=[row, row, row, row, pl.BlockSpec((1, LANES), lambda i: (0, 0)), pl.BlockSpec((1, DN_WIDTH), lambda i: (0, 0))],
        out_shape=[jax.ShapeDtypeStruct((t, DN_WIDTH), dt) for dt in (F32, BF16, BF16, F32)]
        + [jax.ShapeDtypeStruct((1, LANES), F32), jax.ShapeDtypeStruct((1, DN_WIDTH), F32)],
        compiler_params=_cparams(("arbitrary",)),
    )(o_f, o_b, proj, proj, hf, hb, dn_g, lru_g, dmix)


def _ffn_act_fwd(g3, u3, conv_w, conv_b, name):
    bsz, s, f = g3.shape
    nblk = f // LANES

    def body(g_ref, u_ref, w_ref, b_ref, o_ref):
        gate = _conv_fwd(g_ref[0], w_ref[...], 1) + b_ref[...]
        o_ref[0] = (_gelu(gate) * u_ref[0]).astype(o_ref.dtype)

    col = pl.BlockSpec((1, s, LANES), lambda b, j: (b, 0, j))
    return pl.pallas_call(
        body, name=name, grid=(bsz, nblk),
        in_specs=[col, col, pl.BlockSpec((3, LANES), lambda b, j: (0, j)), pl.BlockSpec((1, LANES), lambda b, j: (0, j))],
        out_specs=col, out_shape=jax.ShapeDtypeStruct((bsz, s, f), BF16),
        compiler_params=_cparams(("parallel", "parallel")),
    )(g3, u3, conv_w, conv_b)


def _ffn_act_bwd(g3, u3, conv_w, conv_b, dact3, name):
    bsz, s, f = g3.shape
    nblk = f // LANES

    def body(g_ref, u_ref, w_ref, b_ref, d_ref, dg_ref, du_ref, dw_ref, db_ref):
        g, w, u = g_ref[0], w_ref[...], u_ref[0]
        gate = _conv_fwd(g, w, 1) + b_ref[...]
        _, vjp = jax.vjp(lambda gt, uu: _gelu(gt) * uu, gate, u)
        dgate, du = vjp(d_ref[0])
        dg, dw = _conv_bwd(g, w, 1, dgate)
        dg_ref[0] = dg.astype(dg_ref.dtype)
        du_ref[0] = du.astype(du_ref.dtype)

        @pl.when(pl.program_id(1) == 0)
        def _():
            dw_ref[...] = jnp.zeros_like(dw_ref)
            db_ref[...] = jnp.zeros_like(db_ref)

        dw_ref[...] += dw
        db_ref[...] += jnp.sum(dgate, axis=0, keepdims=True)

    col = pl.BlockSpec((1, s, LANES), lambda j, b: (b, 0, j))
    w3 = pl.BlockSpec((3, LANES), lambda j, b: (0, j))
    w1 = pl.BlockSpec((1, LANES), lambda j, b: (0, j))
    return pl.pallas_call(
        body, name=name, grid=(nblk, bsz),
        in_specs=[col, col, w3, w1, col], out_specs=[col, col, w3, w1],
        out_shape=[jax.ShapeDtypeStruct((bsz, s, f), BF16), jax.ShapeDtypeStruct((bsz, s, f), BF16),
                   jax.ShapeDtypeStruct((3, f), F32), jax.ShapeDtypeStruct((1, f), F32)],
        compiler_params=_cparams(("parallel", "arbitrary")),
    )(g3, u3, conv_w, conv_b, dact3)


def _ple_fn(r, pg, pp, bg):
    return r + _sigmoid(pg + bg) * pp


def _ple_fwd(r, pg, pp, bg, name):
    t, d = r.shape
    tm = _row_tile(t)

    def body(r_ref, pg_ref, pp_ref, bg_ref, o_ref):
        o_ref[...] = _ple_fn(r_ref[...], pg_ref[...], pp_ref[...], bg_ref[...])

    row = pl.BlockSpec((tm, d), lambda i: (i, 0))
    return pl.pallas_call(
        body, name=name, grid=(t // tm,), in_specs=[row, row, row, pl.BlockSpec((1, d), lambda i: (0, 0))],
        out_specs=row, out_shape=jax.ShapeDtypeStruct((t, d), F32),
        compiler_params=_cparams(("parallel",)),
    )(r, pg, pp, bg)


def _ple_bwd(pg, pp, bg, dr, name):
    t, d = pg.shape
    tm = _row_tile(t)

    def body(pg_ref, pp_ref, bg_ref, dr_ref, dpg_ref, dpp_ref, dbg_ref):
        _, vjp = jax.vjp(lambda a, b, c: _sigmoid(a + c) * b, pg_ref[...], pp_ref[...], bg_ref[...])
        dpg, dpp, dbg = vjp(dr_ref[...])
        dpg_ref[...] = dpg.astype(dpg_ref.dtype)
        dpp_ref[...] = dpp.astype(dpp_ref.dtype)

        @pl.when(pl.program_id(0) == 0)
        def _():
            dbg_ref[...] = jnp.zeros_like(dbg_ref)

        dbg_ref[...] += dbg

    row = pl.BlockSpec((tm, d), lambda i: (i, 0))
    vec = pl.BlockSpec((1, d), lambda i: (0, 0))
    return pl.pallas_call(
        body, name=name, grid=(t // tm,), in_specs=[row, row, vec, row], out_specs=[row, row, vec],
        out_shape=[jax.ShapeDtypeStruct((t, d), BF16), jax.ShapeDtypeStruct((t, d), BF16), jax.ShapeDtypeStruct((1, d), F32)],
        compiler_params=_cparams(("arbitrary",)),
    )(pg, pp, bg, dr)


def _loss_head(r, g, target, name):
    t, d = r.shape
    tm = _row_tile(t)

    def loss_fn(x, gg, tgt):
        err = _rms(x, gg) - tgt
        return 0.5 * jnp.sum(jnp.sum(err * err, axis=-1, keepdims=True) * (1.0 / d), axis=0, keepdims=True)

    def body(r_ref, g_ref, t_ref, l_ref, dr_ref, dg_ref):
        val, vjp = jax.vjp(lambda x, gg: loss_fn(x, gg, t_ref[...]), r_ref[...], g_ref[...])
        dx, dg = vjp(jnp.ones((1, 1), F32))
        dr_ref[...] = dx

        @pl.when(pl.program_id(0) == 0)
        def _():
            l_ref[...] = jnp.zeros_like(l_ref)
            dg_ref[...] = jnp.zeros_like(dg_ref)

        l_ref[...] += val
        dg_ref[...] += dg

    row = pl.BlockSpec((tm, d), lambda i: (i, 0))
    vec = pl.BlockSpec((1, d), lambda i: (0, 0))
    one = pl.BlockSpec((1, 1), lambda i: (0, 0))
    return pl.pallas_call(
        body, name=name, grid=(t // tm,), in_specs=[row, vec, row], out_specs=[one, row, vec],
        out_shape=[jax.ShapeDtypeStruct((1, 1), F32), jax.ShapeDtypeStruct((t, d), F32), jax.ShapeDtypeStruct((1, d), F32)],
        compiler_params=_cparams(("arbitrary",)),
    )(r, g, target)


def _adamw_math(w, gg, m, v, d_ref, nm_ref, nv_ref):
    nm = ADAM_B1 * m + (1.0 - ADAM_B1) * gg
    nv = ADAM_B2 * v + (1.0 - ADAM_B2) * (gg * gg)
    m_hat = nm / (1.0 - ADAM_B1 ** ADAM_STEP)
    v_hat = nv / (1.0 - ADAM_B2 ** ADAM_STEP)
    d_ref[...] = -ADAM_LR * (m_hat / (jnp.sqrt(v_hat) + ADAM_EPS) + ADAM_WD * w)
    nm_ref[...] = nm
    nv_ref[...] = nv


def _adamw(w, g, m, v, name):
    def body(w_ref, g_ref, m_ref, v_ref, d_ref, nm_ref, nv_ref):
        _adamw_math(w_ref[...], g_ref[...], m_ref[...], v_ref[...], d_ref, nm_ref, nv_ref)

    whole = pl.BlockSpec(memory_space=pltpu.VMEM)
    return pl.pallas_call(
        body, name=name, in_specs=[whole] * 4, out_specs=[whole] * 3, out_shape=[jax.ShapeDtypeStruct(w.shape, F32)] * 3,
        compiler_params=pltpu.CompilerParams(vmem_limit_bytes=VMEM_LIMIT),
    )(w, g, m, v)


def _prepare_weights(w):
    nl = w["w_in"].shape[0]
    w_in = w["w_in"].astype(BF16)
    pad = jnp.zeros(w_in.shape[:2] + (P_COLS - w_in.shape[2],), BF16)
    split = 4 * DN_WIDTH + N_GATE
    w_in_p = jnp.concatenate([w_in[:, :, split:], w_in[:, :, :split], pad], axis=-1)
    gate_vec = lambda a: jnp.pad(a.reshape(nl, 1, N_GATE // 2), ((0, 0), (0, 0), (N_GATE // 2, LANES - N_GATE)))

    def pair_blocks(a):
        a = a.reshape(nl, 2, 4, 2, 64, 64)
        z = jnp.zeros_like(a[:, :, :, 0])
        top = jnp.concatenate([a[:, :, :, 0], z], axis=-1)
        bot = jnp.concatenate([z, a[:, :, :, 1]], axis=-1)
        return jnp.concatenate([top, bot], axis=-2).astype(BF16)

    bf = lambda a: a.astype(BF16)
    return dict(
        norm1_g=w["norm1_g"][:, None, :], w_in=w_in_p,
        dn_conv_w=w["dn_conv_w"], alog=gate_vec(w["dn_a_log"]), dtb=gate_vec(w["dn_dt_bias"]),
        dn_norm_g=w["dn_norm_g"][:, None, :], lru_conv_w=w["lru_conv_w"], lru_conv_b=w["lru_conv_b"][:, None, :],
        lru_wa=pair_blocks(w["lru_wa"]), lru_ba=w["lru_ba"], lru_wx=pair_blocks(w["lru_wx"]), lru_bx=w["lru_bx"],
        lru_lambda=w["lru_lambda"], lru_norm_g=w["lru_norm_g"][:, None, :],
        w_out=bf(w["w_out"]), norm2_g=w["norm2_g"][:, None, :], ffn_wg=bf(w["ffn_wg"]), ffn_wu=bf(w["ffn_wu"]),
        ffn_conv_w=w["ffn_conv_w"], ffn_conv_b=w["ffn_conv_b"][:, None, :], ffn_wd=bf(w["ffn_wd"]),
        ple_norm_g=w["ple_norm_g"][:, None, :], ple_wg=bf(w["ple_wg"]), ple_bg=w["ple_bg"][:, None, :], ple_wp=bf(w["ple_wp"]),
        final_g=w["final_g"][None, :],
    )


def _unpair_blocks(a):
    top = a[:, :, :64, :64]
    bot = a[:, :, 64:, 64:]
    return jnp.stack([top, bot], axis=2).reshape(2, 8, 64, 64)


def _local_step(x, p, target, w):
    bsz, s, d = x.shape
    t = bsz * s
    nl = w["w_in"].shape[0]
    kw = _prepare_weights(w)
    flat = lambda a: a.reshape(t, a.shape[-1])
    seq = lambda a: a.reshape(bsz, s, a.shape[-1])

    saved = []
    r = flat(x)
    for i in range(nl):
        n = f"l{i}_"
        sv = {"r0": r}
        h, proj = _norm_matmul(r, kw["norm1_g"][i], [kw["w_in"][i]], n + "in_proj")
        gb = _dn_gates_fwd(proj, kw["alog"][i], kw["dtb"][i], n + "dn_gates")
        qkv = _dn_prep_fwd(seq(proj), kw["dn_conv_w"][i], n + "dn_prep")
        in_f = _dn_intra_fwd(qkv, seq(gb), False, n + "dn_intra_fwd")
        in_b = _dn_intra_fwd(qkv, seq(gb), True, n + "dn_intra_rev")
        o_f, st_f = _dn_rec_fwd(in_f, False, n + "dn_rec_fwd")
        o_b, st_b = _dn_rec_fwd(in_b, True, n + "dn_rec_rev")
        lru_args = (seq(proj), kw["lru_conv_w"][i], kw["lru_conv_b"][i], kw["lru_wa"][i], kw["lru_ba"][i],
                    kw["lru_wx"][i], kw["lru_bx"][i], kw["lru_lambda"][i])
        hf, hb = _lru_fwd(*lru_args, n + "lru")
        mix_args = (flat(o_f), flat(o_b), proj, flat(hf), flat(hb), kw["dn_norm_g"][i], kw["lru_norm_g"][i])
        mix = _mix_fwd(*mix_args, n + "mix")
        r1 = _matmul([(mix, kw["w_out"][i])], n + "out_proj", res=r)
        h2, fg, fu = _norm_matmul(r1, kw["norm2_g"][i], [kw["ffn_wg"][i], kw["ffn_wu"][i]], n + "ffn_gu")
        act = _ffn_act_fwd(seq(fg), seq(fu), kw["ffn_conv_w"][i], kw["ffn_conv_b"][i], n + "ffn_act")
        r2 = _matmul([(flat(act), kw["ffn_wd"][i])], n + "ffn_d", res=r1)
        hp, pg = _norm_matmul(r2, kw["ple_norm_g"][i], [kw["ple_wg"][i]], n + "ple_g")
        pi = flat(p[i])
        pp = _matmul([(pi, kw["ple_wp"][i])], n + "ple_p")
        r3 = _ple_fwd(r2, pg, pp, kw["ple_bg"][i], n + "ple")
        sv.update(h=h, proj=proj, gb=gb, qkv=qkv, st_f=st_f, st_b=st_b, in_f=in_f, in_b=in_b, lru_args=lru_args, hf=hf, hb=hb,
                  mix_args=mix_args, mix=mix, r1=r1, h2=h2, fg=fg, fu=fu, act=act, r2=r2, hp=hp, pg=pg, pp=pp, pi=pi)
        saved.append(sv)
        r = r3

    loss, dr, dfinal = _loss_head(r, kw["final_g"], flat(target), "loss_head")
    grads = {k: [None] * nl for k in (
        "norm1_g", "w_in", "dn_conv_w", "dn_a_log", "dn_dt_bias", "dn_norm_g", "lru_conv_w", "lru_conv_b", "lru_wa", "lru_ba",
        "lru_wx", "lru_bx", "lru_lambda", "lru_norm_g", "w_out", "norm2_g", "ffn_wg", "ffn_wu", "ffn_conv_w", "ffn_conv_b",
        "ffn_wd", "ple_norm_g", "ple_wg", "ple_bg", "ple_wp", "dn_a_log_lanes", "dn_dt_bias_lanes")}

    for i in reversed(range(nl)):
        n = f"b{i}_"
        sv = saved[i]
        dpg, dpp, dbg = _ple_bwd(sv["pg"], sv["pp"], kw["ple_bg"][i], dr, n + "ple")
        grads["ple_bg"][i] = dbg[0]
        grads["ple_wp"][i] = _matmul_tn(sv["pi"], dpp, n + "ple_wp")
        grads["ple_wg"][i] = _matmul_tn(sv["hp"], dpg, n + "ple_wg")
        dr2, dr2b, dg = _matmul_norm_bwd([(dpg, kw["ple_wg"][i])], sv["r2"], kw["ple_norm_g"][i], dr, n + "ple_dh")
        grads["ple_norm_g"][i] = dg[0]
        grads["ffn_wd"][i] = _matmul_tn(flat(sv["act"]), dr2b, n + "ffn_wd")
        dact = _matmul([(dr2b, kw["ffn_wd"][i])], n + "ffn_dact", nt=True)
        dfg, dfu, dcw, dcb = _ffn_act_bwd(seq(sv["fg"]), seq(sv["fu"]), kw["ffn_conv_w"][i], kw["ffn_conv_b"][i], seq(dact),
                                          n + "ffn_act")
        grads["ffn_conv_w"][i] = dcw
        grads["ffn_conv_b"][i] = dcb[0]
        grads["ffn_wg"][i] = _matmul_tn(sv["h2"], flat(dfg), n + "ffn_wg")
        grads["ffn_wu"][i] = _matmul_tn(sv["h2"], flat(dfu), n + "ffn_wu")
        dr1, dr1b, dg = _matmul_norm_bwd([(flat(dfg), kw["ffn_wg"][i]), (flat(dfu), kw["ffn_wu"][i])], sv["r1"], kw["norm2_g"][i],
                                         dr2, n + "ffn_dh")
        grads["norm2_g"][i] = dg[0]
        grads["w_out"][i] = _matmul_tn(sv["mix"], dr1b, n + "w_out")
        dmix = _matmul([(dr1b, kw["w_out"][i])], n + "dmix", nt=True)
        do, dz, dlg, dh, ddn_g, dlru_g = _mix_bwd(*sv["mix_args"], dmix, n + "mix")
        grads["dn_norm_g"][i] = ddn_g[0]
        grads["lru_norm_g"][i] = dlru_g[0]
        dlx, dcw, dcb, dwa, dba, dwx, dbx, dlam = _lru_bwd(*sv["lru_args"], sv["hf"], sv["hb"], seq(dh), n + "lru")
        grads["lru_conv_w"][i] = dcw
        grads["lru_conv_b"][i] = dcb[0]
        grads["lru_wa"][i] = _unpair_blocks(dwa)
        grads["lru_wx"][i] = _unpair_blocks(dwx)
        grads["lru_ba"][i], grads["lru_bx"][i], grads["lru_lambda"][i] = dba, dbx, dlam
        ct_f = _dn_rec_bwd(sv["in_f"], sv["st_f"], seq(do), False, n + "dn_rec_fwd")
        ct_b = _dn_rec_bwd(sv["in_b"], sv["st_b"], seq(do), True, n + "dn_rec_rev")
        dqkv_f, dgb_f = _dn_intra_bwd(sv["qkv"], seq(sv["gb"]), ct_f, False, n + "dn_intra_fwd")
        dqkv_b, dgb_b = _dn_intra_bwd(sv["qkv"], seq(sv["gb"]), ct_b, True, n + "dn_intra_rev")
        dgate, dalog, ddtb = _dn_gates_bwd(sv["proj"], kw["alog"][i], kw["dtb"][i], flat(dgb_f), flat(dgb_b), n + "dn_gates")
        grads["dn_a_log"][i] = dalog[0, N_GATE // 2:N_GATE].reshape(2, DN_HEADS)
        grads["dn_dt_bias"][i] = ddtb[0, N_GATE // 2:N_GATE].reshape(2, DN_HEADS)
        grads["dn_a_log_lanes"][i] = dalog[0]
        grads["dn_dt_bias_lanes"][i] = ddtb[0]
        dpqkv, dcw = _dn_prep_bwd(seq(sv["proj"]), kw["dn_conv_w"][i], dqkv_f, dqkv_b, n + "dn_prep")
        grads["dn_conv_w"][i] = dcw
        segs = [flat(dlx), dlg, flat(dpqkv), dz, dgate]
        w_in_i = kw["w_in"][i]
        dq3 = flat(dpqkv)
        wseg = DN_WIDTH
        terms = [(flat(dlx), (w_in_i, wseg, P_LX // wseg)), (dlg, (w_in_i, wseg, P_LG // wseg))]
        terms += [((dq3, wseg, j), (w_in_i, wseg, P_Q // wseg + j)) for j in range(3)]
        terms += [(dz, (w_in_i, wseg, P_Z // wseg)), (dgate, (w_in_i, LANES, P_GATE // LANES))]
        dwp = _matmul_tn_cols(sv["h"], segs, n + "w_in")
        split = 4 * DN_WIDTH + N_GATE
        grads["w_in"][i] = jnp.concatenate([dwp[:, P_Q:P_Q + split], dwp[:, :P_Q]], axis=-1)
        dr, _, dg = _matmul_norm_bwd(terms, sv["r0"], kw["norm1_g"][i], dr1, n + "in_dh")
        grads["norm1_g"][i] = dg[0]

    grads["final_g"] = dfinal[0]
    return loss[0, 0], dr.reshape(bsz, s, d), grads


MESH = pl.DeviceIdType.MESH
ANY = pl.BlockSpec(memory_space=pl.ANY)
N_CHIPS = 4
HALF = 2


def _place():
    x, y, c = lax.axis_index("x"), lax.axis_index("y"), lax.axis_index("c")
    chips = [(1 - x, y), (x, 1 - y), (1 - x, 1 - y)]
    return x, y, c, chips


def _dma_sems(n):
    return pltpu.SemaphoreType.DMA((n,))


def _gather_weights(shards):
    nt = len(shards)

    def body(*refs):
        w_refs, g_refs = refs[:nt], refs[nt:2 * nt]
        send_sems, recv_sems = refs[2 * nt:]
        x, y, c, chips = _place()
        me = 2 * x + y
        mine = pl.ds(c * HALF, HALF)
        theirs = pl.ds((1 - c) * HALF, HALF)

        def rc(src, dst, k, to):
            return pltpu.make_async_remote_copy(src_ref=src, dst_ref=dst, send_sem=send_sems.at[k], recv_sem=recv_sems.at[k],
                                                device_id=to, device_id_type=MESH)

        first = []
        for t in range(nt):
            for r, (px, py) in enumerate(chips):
                first.append(rc(w_refs[t].at[mine], g_refs[t].at[me, mine], 6 * t + r, (px, py, c)))
        for cp in first:
            cp.start()
        passed = []
        for t in range(nt):
            for r, (px, py) in enumerate(chips):
                peer = 2 * px + py
                rc(w_refs[t].at[mine], g_refs[t].at[peer, mine], 6 * t + r, (px, py, c)).wait_recv()
                fw = rc(g_refs[t].at[peer, mine], g_refs[t].at[peer, mine], 6 * t + 3 + r, (x, y, 1 - c))
                fw.start()
                passed.append(fw)
        for t in range(nt):
            for r, (px, py) in enumerate(chips):
                peer = 2 * px + py
                rc(g_refs[t].at[peer, theirs], g_refs[t].at[peer, theirs], 6 * t + 3 + r, (x, y, 1 - c)).wait_recv()
        for cp in first + passed:
            cp.wait_send()

    return pl.pallas_call(
        body, name="gather_weights", in_specs=[ANY] * nt, out_specs=[ANY] * nt,
        out_shape=[jax.ShapeDtypeStruct((N_CHIPS,) + a.shape, a.dtype) for a in shards],
        scratch_shapes=[_dma_sems(6 * nt), _dma_sems(6 * nt)],
    )(*shards)


def _swap_halves(blocks):
    nt = len(blocks)

    def body(*refs):
        p_refs, l_refs = refs[:nt], refs[nt:2 * nt]
        send_sems, recv_sems = refs[2 * nt:]
        x, y, c, _ = _place()
        theirs = pl.ds((1 - c) * HALF, HALF)
        cps = [pltpu.make_async_remote_copy(src_ref=p_refs[t].at[:, theirs], dst_ref=l_refs[t], send_sem=send_sems.at[t],
                                            recv_sem=recv_sems.at[t], device_id=(x, y, 1 - c), device_id_type=MESH)
               for t in range(nt)]
        for cp in cps:
            cp.start()
        for cp in cps:
            cp.wait_send()
            cp.wait_recv()

    return pl.pallas_call(
        body, name="swap_halves", in_specs=[ANY] * nt, out_specs=[ANY] * nt,
        out_shape=[jax.ShapeDtypeStruct((a.shape[0], HALF) + a.shape[2:], a.dtype) for a in blocks],
        scratch_shapes=[_dma_sems(nt), _dma_sems(nt)],
    )(*blocks)


def _tile_rows(a, row_bytes):
    best = None
    for d in range(16, a + 1, 16):
        if a % d == 0 and d * row_bytes <= 2 * 1024 * 1024:
            best = d
    return best if best is not None else a


def _add_halves(pg, l1, c_arr, name):
    n, _, a, b = pg.shape
    ta = _tile_rows(a, 4 * b)

    def body(c_ref, a_ref, b_ref, o_ref):
        o_ref[...] = (a_ref[...] + b_ref[...]).astype(o_ref.dtype)

    grid_spec = pltpu.PrefetchScalarGridSpec(
        num_scalar_prefetch=1, grid=(n, HALF, a // ta),
        in_specs=[pl.BlockSpec((1, 1, ta, b), lambda k, l, i, c_ref: (k, c_ref[0] * HALF + l, i, 0)),
                  pl.BlockSpec((1, 1, ta, b), lambda k, l, i, c_ref: (k, l, i, 0))],
        out_specs=pl.BlockSpec((1, 1, ta, b), lambda k, l, i, c_ref: (k, l, i, 0)))
    return pl.pallas_call(
        body, name=name, grid_spec=grid_spec, out_shape=jax.ShapeDtypeStruct((n, HALF, a, b), BF16),
        compiler_params=_cparams(("parallel", "parallel", "parallel")),
    )(c_arr, pg, l1)


def _scatter_chips(blocks):
    nt = len(blocks)

    def body(*refs):
        q_refs, l_refs = refs[:nt], refs[nt:2 * nt]
        send_sems, recv_sems = refs[2 * nt:]
        x, y, c, chips = _place()
        me = 2 * x + y

        def rc(t, r, src_chip, dst_chip, to):
            return pltpu.make_async_remote_copy(src_ref=q_refs[t].at[src_chip], dst_ref=l_refs[t].at[dst_chip],
                                                send_sem=send_sems.at[3 * t + r], recv_sem=recv_sems.at[3 * t + r],
                                                device_id=to, device_id_type=MESH)

        sends = [rc(t, r, 2 * px + py, me, (px, py, c)) for t in range(nt) for r, (px, py) in enumerate(chips)]
        for cp in sends:
            cp.start()
        for t in range(nt):
            for r, (px, py) in enumerate(chips):
                rc(t, r, 2 * px + py, 2 * px + py, (px, py, c)).wait_recv()
        for cp in sends:
            cp.wait_send()

    return pl.pallas_call(
        body, name="scatter_chips", in_specs=[ANY] * nt, out_specs=[ANY] * nt,
        out_shape=[jax.ShapeDtypeStruct(a.shape, a.dtype) for a in blocks],
        scratch_shapes=[_dma_sems(3 * nt), _dma_sems(3 * nt)],
    )(*blocks)


def _sum_chips(l2, own, me_arr, name):
    n, _, a, b = l2.shape
    ta = _tile_rows(a, 4 * b)

    def body(me_ref, a_ref, own_ref, o_ref):
        me = me_ref[0]
        acc = jnp.where(me == 0, own_ref[0, 0], a_ref[0, 0]).astype(F32)
        for k in range(1, n):
            acc = acc + jnp.where(me == k, own_ref[0, 0], a_ref[k, 0]).astype(F32)
        o_ref[0] = acc

    grid_spec = pltpu.PrefetchScalarGridSpec(
        num_scalar_prefetch=1, grid=(HALF, a // ta),
        in_specs=[pl.BlockSpec((n, 1, ta, b), lambda l, i, me_ref: (0, l, i, 0)),
                  pl.BlockSpec((1, 1, ta, b), lambda l, i, me_ref: (me_ref[0], l, i, 0))],
        out_specs=pl.BlockSpec((1, ta, b), lambda l, i, me_ref: (l, i, 0)))
    return pl.pallas_call(
        body, name=name, grid_spec=grid_spec, out_shape=jax.ShapeDtypeStruct((HALF, a, b), F32),
        compiler_params=_cparams(("parallel", "parallel")),
    )(me_arr, l2, own)


def _swap_reduced(parts):
    nt = len(parts)

    def body(*refs):
        r_refs, g_refs = refs[:nt], refs[nt:2 * nt]
        send_sems, recv_sems = refs[2 * nt:]
        x, y, c, _ = _place()
        cps = [pltpu.make_async_remote_copy(src_ref=r_refs[t], dst_ref=g_refs[t], send_sem=send_sems.at[t],
                                            recv_sem=recv_sems.at[t], device_id=(x, y, 1 - c), device_id_type=MESH)
               for t in range(nt)]
        for cp in cps:
            cp.start()
        for cp in cps:
            cp.wait_send()
            cp.wait_recv()

    return pl.pallas_call(
        body, name="swap_reduced", in_specs=[ANY] * nt, out_specs=[ANY] * nt,
        out_shape=[jax.ShapeDtypeStruct(a.shape, a.dtype) for a in parts],
        scratch_shapes=[_dma_sems(nt), _dma_sems(nt)],
    )(*parts)


def _adamw_halves(w, mine, theirs, m, v, c_arr, name):
    nl, a, b = w.shape
    ta = _tile_rows(a, 4 * b)

    def body(c_ref, w_ref, g1_ref, g2_ref, m_ref, v_ref, g_ref, d_ref, nm_ref, nv_ref):
        gg = jnp.where(pl.program_id(0) // HALF == c_ref[0], g1_ref[...], g2_ref[...])
        g_ref[...] = gg
        _adamw_math(w_ref[...], gg, m_ref[...], v_ref[...], d_ref, nm_ref, nv_ref)

    full = pl.BlockSpec((1, ta, b), lambda l, i, c_ref: (l, i, 0))
    half = pl.BlockSpec((1, ta, b), lambda l, i, c_ref: (l % HALF, i, 0))
    grid_spec = pltpu.PrefetchScalarGridSpec(num_scalar_prefetch=1, grid=(nl, a // ta),
                                             in_specs=[full, half, half, full, full], out_specs=[full] * 4)
    return pl.pallas_call(
        body, name=name, grid_spec=grid_spec, out_shape=[jax.ShapeDtypeStruct(w.shape, F32)] * 4,
        compiler_params=_cparams(("parallel", "parallel")),
    )(c_arr, w, mine, theirs, m, v)


def _gather_chips(v):
    def body(v_ref, g_ref, send_sems, recv_sems, local_sem):
        x, y, c, chips = _place()
        me = 2 * x + y
        loc = pltpu.make_async_copy(v_ref, g_ref.at[me], local_sem)
        loc.start()
        sends = [pltpu.make_async_remote_copy(src_ref=v_ref, dst_ref=g_ref.at[me], send_sem=send_sems.at[r], recv_sem=recv_sems.at[r],
                                              device_id=(px, py, c), device_id_type=MESH) for r, (px, py) in enumerate(chips)]
        for cp in sends:
            cp.start()
        for r, (px, py) in enumerate(chips):
            pltpu.make_async_remote_copy(src_ref=v_ref, dst_ref=g_ref.at[2 * px + py], send_sem=send_sems.at[r],
                                         recv_sem=recv_sems.at[r], device_id=(px, py, c), device_id_type=MESH).wait_recv()
        for cp in sends:
            cp.wait_send()
        loc.wait()

    return pl.pallas_call(
        body, name="gather_chips", in_specs=[ANY], out_specs=ANY, out_shape=jax.ShapeDtypeStruct((N_CHIPS,) + v.shape, v.dtype),
        scratch_shapes=[_dma_sems(3), _dma_sems(3), pltpu.SemaphoreType.DMA],
    )(v)


BIG = (("w_in", 2), ("w_out", 1), ("ffn_wg", 2), ("ffn_wu", 2), ("ffn_wd", 1), ("ple_wg", 1), ("ple_wp", 2))
SMALL = (("dn_conv_w", 2), ("lru_conv_w", 2), ("lru_ba", 2), ("lru_bx", 2), ("lru_lambda", 2), ("ffn_conv_w", 2))
REPL = ("norm1_g", "dn_a_log", "dn_dt_bias", "dn_norm_g", "lru_conv_b", "lru_wa", "lru_wx", "lru_norm_g", "norm2_g",
        "ffn_conv_b", "ple_norm_g", "ple_bg", "final_g")
WEIGHTS = ("norm1_g", "w_in", "dn_conv_w", "dn_a_log", "dn_dt_bias", "dn_norm_g", "lru_conv_w", "lru_conv_b", "lru_wa", "lru_ba",
           "lru_wx", "lru_bx", "lru_lambda", "lru_norm_g", "w_out", "norm2_g", "ffn_wg", "ffn_wu", "ffn_conv_w", "ffn_conv_b",
           "ffn_wd", "ple_norm_g", "ple_wg", "ple_bg", "ple_wp", "final_g")
GATE_PARAMS = ("dn_a_log", "dn_dt_bias")
ROW_ALIGN = 16


def _rows_for(n_elems):
    rows = -(-n_elems // LANES)
    return -(-rows // ROW_ALIGN) * ROW_ALIGN


def _join_chips(g, own, me, axis):
    return jnp.concatenate([jnp.where(me == k, own, g[k]) for k in range(N_CHIPS)], axis=axis)


def _chip_blocks(layers, axis):
    nl = len(layers)
    size = layers[0].shape[axis - 1] // N_CHIPS
    cut = (lambda g, k: g[:, k * size:(k + 1) * size]) if axis == 2 else (lambda g, k: g[k * size:(k + 1) * size])
    return jnp.stack([jnp.stack([cut(layers[l], k) for l in range(nl)]) for k in range(N_CHIPS)])


def _pack_small(w):
    nl = w[SMALL[0][0]].shape[0]
    flat = jnp.concatenate([w[n].reshape(nl, -1) for n, _ in SMALL], axis=1)
    rows = _rows_for(flat.shape[1])
    return jnp.pad(flat, ((0, 0), (0, rows * LANES - flat.shape[1]))).reshape(nl, rows, LANES)


def _unpack_small_gathered(g, own, me, w):
    nl = g.shape[1]
    flat = jnp.stack([jnp.where(me == k, own, g[k]) for k in range(N_CHIPS)]).reshape(N_CHIPS, nl, -1)
    out, off = {}, 0
    for n, _ in SMALL:
        _, a, b = w[n].shape
        piece = flat[:, :, off:off + a * b].reshape(N_CHIPS, nl, a, b)
        off += a * b
        out[n] = jnp.transpose(piece, (1, 2, 0, 3)).reshape(nl, a, N_CHIPS * b)
    return out


def _pack_small_grads(grads, w):
    nl = w["w_in"].shape[0]
    cols = []
    for n, _ in SMALL:
        _, a, b = w[n].shape
        gfull = jnp.stack(grads[n])
        cols.append(jnp.transpose(gfull.reshape(nl, a, N_CHIPS, b), (2, 0, 1, 3)).reshape(N_CHIPS, nl, a * b))
    small = jnp.concatenate(cols, axis=2)
    rs = _rows_for(small.shape[2])
    small = jnp.pad(small, ((0, 0), (0, 0), (0, rs * LANES - small.shape[2])))
    src = lambda n: n + "_lanes" if n in GATE_PARAMS else n
    rep = jnp.concatenate([(grads[n] if n == "final_g" else jnp.stack(grads[src(n)])).reshape(-1) for n in REPL])
    rr = _rows_for(-(-rep.shape[0] // (N_CHIPS * nl)))
    rep = jnp.pad(rep, (0, N_CHIPS * nl * rr * LANES - rep.shape[0])).reshape(N_CHIPS, nl, rr * LANES)
    return jnp.concatenate([small, rep], axis=2).reshape(N_CHIPS, nl, rs + rr, LANES), (rs, rr)


def _unpack_small_reduced(g, rep_all, rows, w):
    rs, _ = rows
    nl = g.shape[0]
    out = {}
    flat = g[:, :rs].reshape(nl, -1)
    off = 0
    for n, _ in SMALL:
        _, a, b = w[n].shape
        out[n] = flat[:, off:off + a * b].reshape(nl, a, b)
        off += a * b
    flat = rep_all.reshape(-1)
    off = 0
    for n in REPL:
        if n in GATE_PARAMS:
            size = nl * LANES
            out[n] = flat[off:off + size].reshape(nl, LANES)[:, N_GATE // 2:N_GATE].reshape(w[n].shape)
        else:
            size = math.prod(w[n].shape)
            out[n] = flat[off:off + size].reshape(w[n].shape)
        off += size
    return out


def kernel(x, p, norm1_g, w_in, dn_conv_w, dn_a_log, dn_dt_bias, dn_norm_g, lru_conv_w, lru_conv_b, lru_wa, lru_ba, lru_wx, lru_bx, lru_lambda, lru_norm_g, w_out, norm2_g, ffn_wg, ffn_wu, ffn_conv_w, ffn_conv_b, ffn_wd, ple_norm_g, ple_wg, ple_bg, ple_wp, final_g, loss_target, m_norm1_g, m_w_in, m_dn_conv_w, m_dn_a_log, m_dn_dt_bias, m_dn_norm_g, m_lru_conv_w, m_lru_conv_b, m_lru_wa, m_lru_ba, m_lru_wx, m_lru_bx, m_lru_lambda, m_lru_norm_g, m_w_out, m_norm2_g, m_ffn_wg, m_ffn_wu, m_ffn_conv_w, m_ffn_conv_b, m_ffn_wd, m_ple_norm_g, m_ple_wg, m_ple_bg, m_ple_wp, m_final_g, v_norm1_g, v_w_in, v_dn_conv_w, v_dn_a_log, v_dn_dt_bias, v_dn_norm_g, v_lru_conv_w, v_lru_conv_b, v_lru_wa, v_lru_ba, v_lru_wx, v_lru_bx, v_lru_lambda, v_lru_norm_g, v_w_out, v_norm2_g, v_ffn_wg, v_ffn_wu, v_ffn_conv_w, v_ffn_conv_b, v_ffn_wd, v_ple_norm_g, v_ple_wg, v_ple_bg, v_ple_wp, v_final_g):
    w = dict(norm1_g=norm1_g, w_in=w_in, dn_conv_w=dn_conv_w, dn_a_log=dn_a_log, dn_dt_bias=dn_dt_bias, dn_norm_g=dn_norm_g,
             lru_conv_w=lru_conv_w, lru_conv_b=lru_conv_b, lru_wa=lru_wa, lru_ba=lru_ba, lru_wx=lru_wx, lru_bx=lru_bx,
             lru_lambda=lru_lambda, lru_norm_g=lru_norm_g, w_out=w_out, norm2_g=norm2_g, ffn_wg=ffn_wg, ffn_wu=ffn_wu,
             ffn_conv_w=ffn_conv_w, ffn_conv_b=ffn_conv_b, ffn_wd=ffn_wd, ple_norm_g=ple_norm_g, ple_wg=ple_wg, ple_bg=ple_bg,
             ple_wp=ple_wp, final_g=final_g)
    m = dict(norm1_g=m_norm1_g, w_in=m_w_in, dn_conv_w=m_dn_conv_w, dn_a_log=m_dn_a_log, dn_dt_bias=m_dn_dt_bias,
             dn_norm_g=m_dn_norm_g, lru_conv_w=m_lru_conv_w, lru_conv_b=m_lru_conv_b, lru_wa=m_lru_wa, lru_ba=m_lru_ba,
             lru_wx=m_lru_wx, lru_bx=m_lru_bx, lru_lambda=m_lru_lambda, lru_norm_g=m_lru_norm_g, w_out=m_w_out, norm2_g=m_norm2_g,
             ffn_wg=m_ffn_wg, ffn_wu=m_ffn_wu, ffn_conv_w=m_ffn_conv_w, ffn_conv_b=m_ffn_conv_b, ffn_wd=m_ffn_wd,
             ple_norm_g=m_ple_norm_g, ple_wg=m_ple_wg, ple_bg=m_ple_bg, ple_wp=m_ple_wp, final_g=m_final_g)
    v = dict(norm1_g=v_norm1_g, w_in=v_w_in, dn_conv_w=v_dn_conv_w, dn_a_log=v_dn_a_log, dn_dt_bias=v_dn_dt_bias,
             dn_norm_g=v_dn_norm_g, lru_conv_w=v_lru_conv_w, lru_conv_b=v_lru_conv_b, lru_wa=v_lru_wa, lru_ba=v_lru_ba,
             lru_wx=v_lru_wx, lru_bx=v_lru_bx, lru_lambda=v_lru_lambda, lru_norm_g=v_lru_norm_g, w_out=v_w_out, norm2_g=v_norm2_g,
             ffn_wg=v_ffn_wg, ffn_wu=v_ffn_wu, ffn_conv_w=v_ffn_conv_w, ffn_conv_b=v_ffn_conv_b, ffn_wd=v_ffn_wd,
             ple_norm_g=v_ple_norm_g, ple_wg=v_ple_wg, ple_bg=v_ple_bg, ple_wp=v_ple_wp, final_g=v_final_g)

    me = 2 * lax.axis_index("x") + lax.axis_index("y")
    own = [w[n].astype(BF16) for n, _ in BIG] + [_pack_small(w)]
    gathered = _gather_weights(own)
    full = {n: w[n] for n in REPL}
    for (n, axis), g, o in zip(BIG, gathered, own):
        full[n] = _join_chips(g, o, me, axis)
    full.update(_unpack_small_gathered(gathered[-1], own[-1], me, w))

    loss_local, grad_x, grads = _local_step(x, p, loss_target, full)
    loss = lax.psum(loss_local, ("x", "y", "c"))

    small_pack, rows = _pack_small_grads(grads, w)
    blocks = [_chip_blocks(grads[n], axis) for n, axis in BIG]
    blocks.append(small_pack)
    names = [n for n, _ in BIG] + ["small"]
    c_idx = lax.axis_index("c")
    c_arr = c_idx.astype(jnp.int32).reshape(1)
    me_arr = me.astype(jnp.int32).reshape(1)
    from_sibling = _swap_halves(blocks)
    halves = [_add_halves(b4, l1, c_arr, "add_halves_" + n) for n, b4, l1 in zip(names, blocks, from_sibling)]
    arrived = _scatter_chips(halves)
    mine = [_sum_chips(l2, q, me_arr, "sum_chips_" + n) for n, l2, q in zip(names, arrived, halves)]
    theirs = _swap_reduced(mine)

    g, deltas, new_m, new_v = {}, {}, {}, {}
    for i, (n, _) in enumerate(BIG):
        g[n], deltas[n], new_m[n], new_v[n] = _adamw_halves(w[n], mine[i], theirs[i], m[n], v[n], c_arr, "adamw_" + n)
    small = jnp.where(c_idx == 0, jnp.concatenate([mine[-1], theirs[-1]]), jnp.concatenate([theirs[-1], mine[-1]]))
    rep_all = _gather_chips(small[:, rows[0]:])
    gs = _unpack_small_reduced(small, rep_all, rows, w)
    for n in WEIGHTS:
        if n in g:
            continue
        g[n] = gs[n].reshape(w[n].shape)
        shape = (1,) + w[n].shape if w[n].ndim == 1 else w[n].shape
        d2, m2, v2 = _adamw(w[n].reshape(shape), g[n].reshape(shape), m[n].reshape(shape), v[n].reshape(shape), "adamw_" + n)
        deltas[n], new_m[n], new_v[n] = d2.reshape(w[n].shape), m2.reshape(w[n].shape), v2.reshape(w[n].shape)
    return (loss, grad_x, *[g[n] for n in WEIGHTS], *[deltas[n] for n in WEIGHTS], *[new_m[n] for n in WEIGHTS],
            *[new_v[n] for n in WEIGHTS])
```

```python
import functools
import math

import jax
import jax.numpy as jnp
from jax import lax
from jax.experimental import pallas as pl
from jax.experimental.pallas import tpu as pltpu

F32 = jnp.float32
BF16 = jnp.bfloat16

D_MODEL = 1024
DN_HEADS = 4
DN_HEAD_DIM = 128
DN_WIDTH = 512
LRU_WIDTH = 512
LRU_C = 8.0
CHUNK = 64
EPS = 1e-6
P_LX, P_LG, P_Q, P_K, P_V, P_Z, P_GATE, P_COLS = 0, 512, 1024, 1536, 2048, 2560, 3072, 3200
N_GATE = 16
LANES = 128
VMEM_LIMIT = 56 * 1024 * 1024
MM_VMEM_BYTES = 40 * 1024 * 1024

ADAM_LR, ADAM_B1, ADAM_B2, ADAM_EPS, ADAM_WD, ADAM_STEP = 0.001, 0.9, 0.999, 1e-08, 0.01, 10


def _cparams(sem):
    return pltpu.CompilerParams(dimension_semantics=sem, vmem_limit_bytes=VMEM_LIMIT)


def _row_tile(m, want=512):
    for t in range(min(want, m) // 16 * 16, 0, -16):
        if m % t == 0:
            return t
    return m


def _lane_divisors(n):
    out = [d for d in range(n, 0, -LANES) if d % LANES == 0 and n % d == 0] if n % LANES == 0 else []
    return out or [n]


def _mm_tiles(m, n, a_row_bytes, k_total, with_res):
    best = None
    for tm in (1024, 512, 256, 128):
        if m % tm:
            continue
        for tn in _lane_divisors(n):
            need = 2 * (tm * a_row_bytes + k_total * tn * 2 + tm * tn * 4 * (2 if with_res else 1))
            if need <= MM_VMEM_BYTES and (best is None or tm * tn > best[0] * best[1]):
                best = (tm, tn)
    return best if best is not None else (_row_tile(m, 128), _lane_divisors(n)[-1])


def _mm_tn_tiles(m, k, n):
    for tn in _lane_divisors(n):
        for tm in (1024, 512, 256, 128):
            if m % tm == 0 and 2 * (k * tn * 4 + tm * k * 2 + tm * tn * 2) <= MM_VMEM_BYTES:
                return tm, tn
    return _row_tile(m, 128), _lane_divisors(n)[-1]


def _bdot(a, b):
    return jnp.dot(a.astype(BF16), b.astype(BF16), preferred_element_type=F32)


def _bdot_nt(a, b):
    return lax.dot_general(a.astype(BF16), b.astype(BF16), (((1,), (1,)), ((), ())), preferred_element_type=F32)


def _bdot_tn(a, b):
    return lax.dot_general(a.astype(BF16), b.astype(BF16), (((0,), (0,)), ((), ())), preferred_element_type=F32)


def _rms(x, g):
    return x * lax.rsqrt(jnp.mean(x * x, axis=-1, keepdims=True) + EPS) * g


def _gelu(x):
    return 0.5 * x * (1.0 + jnp.tanh(0.7978845608028654 * (x + 0.044715 * x * x * x)))


def _sigmoid(x):
    return 1.0 / (1.0 + jnp.exp(-x))


def _silu(x):
    return x * _sigmoid(x)


def _softplus(x):
    return jnp.maximum(x, 0.0) + jnp.log(1.0 + jnp.exp(-jnp.abs(x)))


def _matmul(terms, name, res=None, nt=False):
    norm = lambda op, axis: op if isinstance(op, tuple) else (op, op.shape[axis], 0)
    a_ops = [norm(a, 1) for a, _ in terms]
    b_ops = [norm(b, 1 if nt else 0) for _, b in terms]
    m = a_ops[0][0].shape[0]
    n = b_ops[0][0].shape[0 if nt else 1]
    a_row_bytes = sum(kw * a.dtype.itemsize for a, kw, _ in a_ops)
    tm, tn = _mm_tiles(m, n, a_row_bytes, sum(kw for _, kw, _ in b_ops), res is not None)
    na = len(terms)
    dot = _bdot_nt if nt else _bdot

    def body(*refs):
        a_refs, b_refs = refs[:na], refs[na:2 * na]
        acc = dot(a_refs[0][...], b_refs[0][...])
        for a_ref, b_ref in zip(a_refs[1:], b_refs[1:]):
            acc = acc + dot(a_ref[...], b_ref[...])
        if res is not None:
            acc = acc + refs[2 * na][...]
        refs[-1][...] = acc

    in_specs = [pl.BlockSpec((tm, kw), functools.partial(lambda i, j, kb: (i, kb), kb=kb)) for _, kw, kb in a_ops]
    if nt:
        in_specs += [pl.BlockSpec((tn, kw), functools.partial(lambda i, j, kb: (j, kb), kb=kb)) for _, kw, kb in b_ops]
    else:
        in_specs += [pl.BlockSpec((kw, tn), functools.partial(lambda i, j, kb: (kb, j), kb=kb)) for _, kw, kb in b_ops]
    args = [a for a, _, _ in a_ops] + [b for b, _, _ in b_ops]
    if res is not None:
        in_specs.append(pl.BlockSpec((tm, tn), lambda i, j: (i, j)))
        args.append(res)
    return pl.pallas_call(
        body, name=name, grid=(m // tm, n // tn), in_specs=in_specs,
        out_specs=pl.BlockSpec((tm, tn), lambda i, j: (i, j)),
        out_shape=jax.ShapeDtypeStruct((m, n), F32),
        compiler_params=_cparams(("parallel", "parallel")),
    )(*args)


def _norm_matmul(x, g, b_list, name):
    m, k = x.shape
    b_bytes = sum(2 * b.shape[0] * b.shape[1] * 2 for b in b_list)
    row_bytes = 2 * (k * 4 + k * 2 + sum(b.shape[1] * 4 for b in b_list))
    tm = next((t for t in (1024, 512, 256, 128) if m % t == 0 and b_bytes + t * row_bytes <= MM_VMEM_BYTES), _row_tile(m, 128))
    nb = len(b_list)

    def body(x_ref, g_ref, *refs):
        b_refs, h_ref, o_refs = refs[:nb], refs[nb], refs[nb + 1:]
        h = _rms(x_ref[...], g_ref[...]).astype(BF16)
        h_ref[...] = h
        for b_ref, o_ref in zip(b_refs, o_refs):
            o_ref[...] = jnp.dot(h, b_ref[...], preferred_element_type=F32)

    row = lambda width: pl.BlockSpec((tm, width), lambda i: (i, 0))
    return pl.pallas_call(
        body, name=name, grid=(m // tm,),
        in_specs=[row(k), pl.BlockSpec((1, k), lambda i: (0, 0))] + [pl.BlockSpec(b.shape, lambda i: (0, 0)) for b in b_list],
        out_specs=[row(k)] + [row(b.shape[1]) for b in b_list],
        out_shape=[jax.ShapeDtypeStruct((m, k), BF16)] + [jax.ShapeDtypeStruct((m, b.shape[1]), F32) for b in b_list],
        compiler_params=_cparams(("parallel",)),
    )(x, g, *b_list)


def _matmul_norm_bwd(terms, x, g, dres, name):
    norm = lambda op: op if isinstance(op, tuple) else (op, op.shape[1], 0)
    a_ops = [norm(a) for a, _ in terms]
    b_ops = [norm(b) for _, b in terms]
    m, n = x.shape
    b_bytes = sum(2 * n * kw * 2 for _, kw, _ in b_ops)
    row_bytes = 2 * (sum(kw * 2 for _, kw, _ in a_ops) + n * (4 + 4 + 4 + 2)) + n * 8
    tm = next((t for t in (1024, 512, 256, 128) if m % t == 0 and b_bytes + t * row_bytes <= MM_VMEM_BYTES), _row_tile(m, 128))
    na = len(terms)

    def body(*refs):
        a_refs, b_refs = refs[:na], refs[na:2 * na]
        x_ref, g_ref, dres_ref, dx_ref, dxb_ref, dg_ref = refs[2 * na:]
        dh = _bdot_nt(a_refs[0][...], b_refs[0][...])
        for a_ref, b_ref in zip(a_refs[1:], b_refs[1:]):
            dh = dh + _bdot_nt(a_ref[...], b_ref[...])
        _, vjp = jax.vjp(_rms, x_ref[...], g_ref[...])
        dx, dg = vjp(dh)
        dx = dx + dres_ref[...]
        dx_ref[...] = dx
        dxb_ref[...] = dx.astype(BF16)

        @pl.when(pl.program_id(0) == 0)
        def _():
            dg_ref[...] = jnp.zeros_like(dg_ref)

        dg_ref[...] += dg

    row = pl.BlockSpec((tm, n), lambda i: (i, 0))
    vec = pl.BlockSpec((1, n), lambda i: (0, 0))
    in_specs = [pl.BlockSpec((tm, kw), functools.partial(lambda i, kb: (i, kb), kb=kb)) for _, kw, kb in a_ops]
    in_specs += [pl.BlockSpec((n, kw), functools.partial(lambda i, kb: (0, kb), kb=kb)) for _, kw, kb in b_ops]
    return pl.pallas_call(
        body, name=name, grid=(m // tm,), in_specs=in_specs + [row, vec, row], out_specs=[row, row, vec],
        out_shape=[jax.ShapeDtypeStruct((m, n), F32), jax.ShapeDtypeStruct((m, n), BF16), jax.ShapeDtypeStruct((1, n), F32)],
        compiler_params=_cparams(("arbitrary",)),
    )(*[a for a, _, _ in a_ops], *[b for b, _, _ in b_ops], x, g, dres)


def _matmul_tn_cols(a, b_list, name):
    m, k = a.shape
    n = sum(b.shape[1] for b in b_list)
    tm = next((t for t in (1024, 512, 256, 128) if m % t == 0 and 2 * (k * n * 4 + t * k * 2 + t * n * 2) <= MM_VMEM_BYTES),
              _row_tile(m, 128))

    def body(a_ref, *refs):
        b_refs, o_ref = refs[:-1], refs[-1]

        @pl.when(pl.program_id(0) == 0)
        def _():
            o_ref[...] = jnp.zeros_like(o_ref)

        off = 0
        for b_ref in b_refs:
            w = b_ref.shape[1]
            o_ref[:, off:off + w] += _bdot_tn(a_ref[...], b_ref[...])
            off += w

    return pl.pallas_call(
        body, name=name, grid=(m // tm,),
        in_specs=[pl.BlockSpec((tm, k), lambda i: (i, 0))] + [pl.BlockSpec((tm, b.shape[1]), lambda i: (i, 0)) for b in b_list],
        out_specs=pl.BlockSpec((k, n), lambda i: (0, 0)), out_shape=jax.ShapeDtypeStruct((k, n), F32),
        compiler_params=_cparams(("arbitrary",)),
    )(a, *b_list)


def _matmul_tn(a, b, name):
    m, k = a.shape
    n = b.shape[1]
    tm, tn = _mm_tn_tiles(m, k, n)

    def body(a_ref, b_ref, o_ref):
        @pl.when(pl.program_id(1) == 0)
        def _():
            o_ref[...] = jnp.zeros_like(o_ref)

        o_ref[...] += _bdot_tn(a_ref[...], b_ref[...])

    return pl.pallas_call(
        body, name=name, grid=(n // tn, m // tm),
        in_specs=[pl.BlockSpec((tm, k), lambda j, i: (i, 0)), pl.BlockSpec((tm, tn), lambda j, i: (i, j))],
        out_specs=pl.BlockSpec((k, tn), lambda j, i: (0, j)),
        out_shape=jax.ShapeDtypeStruct((k, n), F32),
        compiler_params=_cparams(("parallel", "arbitrary")),
    )(a, b)


def _shift_down(x, k):
    row = lax.broadcasted_iota(jnp.int32, x.shape, 0)
    return jnp.where(row >= k, pltpu.roll(x, k, 0), 0.0)


def _shift_up(x, k):
    s = x.shape[0]
    row = lax.broadcasted_iota(jnp.int32, x.shape, 0)
    return jnp.where(row < s - k, pltpu.roll(x, s - k, 0), 0.0)


def _conv_taps(x, ntaps, left):
    out = []
    for j in range(ntaps):
        off = j - left
        out.append(_shift_down(x, -off) if off < 0 else (_shift_up(x, off) if off > 0 else x))
    return out


def _conv_fwd(x, w, left):
    taps = _conv_taps(x, w.shape[0], left)
    acc = taps[0] * w[0:1, :]
    for j in range(1, w.shape[0]):
        acc = acc + taps[j] * w[j:j + 1, :]
    return acc


def _conv_bwd(x, w, left, dout):
    ntaps = w.shape[0]
    dx = None
    for j in range(ntaps):
        off = j - left
        sh = _shift_up(dout, -off) if off < 0 else (_shift_down(dout, off) if off > 0 else dout)
        term = sh * w[j:j + 1, :]
        dx = term if dx is None else dx + term
    taps = _conv_taps(x, ntaps, left)
    dw = jnp.concatenate([jnp.sum(dout * tp, axis=0, keepdims=True) for tp in taps], axis=0)
    return dx, dw


SCAN_BLOCK = 32


def _scan(a, b, reverse):
    s = a.shape[0]
    blk = SCAN_BLOCK if s % SCAN_BLOCK == 0 else s
    row = lax.broadcasted_iota(jnp.int32, a.shape, 0)
    pos = row & (blk - 1) if blk & (blk - 1) == 0 else row % blk
    d = 1
    while d < blk:
        if reverse:
            keep = pos < blk - d
            sb, sa = pltpu.roll(b, s - d, 0), pltpu.roll(a, s - d, 0)
        else:
            keep = pos >= d
            sb, sa = pltpu.roll(b, d, 0), pltpu.roll(a, d, 0)
        b = a * jnp.where(keep, sb, 0.0) + b
        a = a * jnp.where(keep, sa, 1.0)
        d *= 2
    nblk = s // blk
    if nblk == 1:
        return b
    carry = jnp.zeros((1, a.shape[1]), F32)
    carries = [None] * nblk
    order = range(nblk - 1, -1, -1) if reverse else range(nblk)
    for j in order:
        carries[j] = carry
        last = j * blk if reverse else (j + 1) * blk - 1
        carry = b[last:last + 1, :] + a[last:last + 1, :] * carry
    carry_in = jnp.concatenate([jnp.broadcast_to(cj, (blk, a.shape[1])) for cj in carries], axis=0)
    return b + a * carry_in


def _dn_gates_fn(pre, alog, dtb):
    lane = lax.broadcasted_iota(jnp.int32, pre.shape, 1)
    beta = _sigmoid(pre)
    g = -jnp.exp(alog) * _softplus(pre + dtb)
    return jnp.where(lane < N_GATE // 2, beta, jnp.where(lane < N_GATE, g, 0.0))


def _dn_gates_fwd(proj, alog, dtb, name):
    t = proj.shape[0]
    tm = _row_tile(t)
    cb = P_GATE // LANES

    def body(p_ref, a_ref, d_ref, o_ref):
        o_ref[...] = _dn_gates_fn(p_ref[...], a_ref[...], d_ref[...])

    return pl.pallas_call(
        body, name=name, grid=(t // tm,),
        in_specs=[pl.BlockSpec((tm, LANES), lambda i: (i, cb)), pl.BlockSpec((1, LANES), lambda i: (0, 0)),
                  pl.BlockSpec((1, LANES), lambda i: (0, 0))],
        out_specs=pl.BlockSpec((tm, LANES), lambda i: (i, 0)),
        out_shape=jax.ShapeDtypeStruct((t, LANES), F32),
        compiler_params=_cparams(("parallel",)),
    )(proj, alog, dtb)


def _dn_gates_bwd(proj, alog, dtb, dgb_f, dgb_b, name):
    t = proj.shape[0]
    tm = _row_tile(t)
    cb = P_GATE // LANES

    def body(p_ref, a_ref, d_ref, g1_ref, g2_ref, dp_ref, da_ref, dd_ref):
        _, vjp = jax.vjp(_dn_gates_fn, p_ref[...], a_ref[...], d_ref[...])
        dp, da, dd = vjp(g1_ref[...] + g2_ref[...])
        dp_ref[...] = dp.astype(dp_ref.dtype)

        @pl.when(pl.program_id(0) == 0)
        def _():
            da_ref[...] = jnp.zeros_like(da_ref)
            dd_ref[...] = jnp.zeros_like(dd_ref)

        da_ref[...] += da
        dd_ref[...] += dd

    row = pl.BlockSpec((tm, LANES), lambda i: (i, 0))
    vec = pl.BlockSpec((1, LANES), lambda i: (0, 0))
    return pl.pallas_call(
        body, name=name, grid=(t // tm,),
        in_specs=[pl.BlockSpec((tm, LANES), lambda i: (i, cb)), vec, vec, row, row],
        out_specs=[row, vec, vec],
        out_shape=[jax.ShapeDtypeStruct((t, LANES), BF16), jax.ShapeDtypeStruct((1, LANES), F32),
                   jax.ShapeDtypeStruct((1, LANES), F32)],
        compiler_params=_cparams(("arbitrary",)),
    )(proj, alog, dtb, dgb_f, dgb_b)


def _dn_prep_fn(x, w, is_qk):
    act = _silu(_conv_fwd(x, w, 2))
    nrm = act * lax.rsqrt(jnp.sum(act * act, axis=-1, keepdims=True) + EPS)
    return jnp.where(is_qk, nrm, act)


def _dn_prep_fwd(proj3, conv_w, name):
    bsz, s, _ = proj3.shape
    nblk = 3 * DN_WIDTH // LANES
    cb = P_Q // LANES

    def body(x_ref, w_ref, o_ref):
        o_ref[0] = _dn_prep_fn(x_ref[0], w_ref[...], pl.program_id(1) < 2 * DN_HEADS)

    return pl.pallas_call(
        body, name=name, grid=(bsz, nblk),
        in_specs=[pl.BlockSpec((1, s, LANES), lambda b, j: (b, 0, cb + j)), pl.BlockSpec((4, LANES), lambda b, j: (0, j))],
        out_specs=pl.BlockSpec((1, s, LANES), lambda b, j: (b, 0, j)),
        out_shape=jax.ShapeDtypeStruct((bsz, s, 3 * DN_WIDTH), F32),
        compiler_params=_cparams(("parallel", "parallel")),
    )(proj3, conv_w)


def _dn_prep_bwd(proj3, conv_w, dqkv_f, dqkv_b, name):
    bsz, s, _ = proj3.shape
    nblk = 3 * DN_WIDTH // LANES
    cb = P_Q // LANES

    def body(x_ref, w_ref, d1_ref, d2_ref, dx_ref, dw_ref):
        x, w, d = x_ref[0], w_ref[...], d1_ref[0] + d2_ref[0]
        is_qk = pl.program_id(0) < 2 * DN_HEADS
        pre = _conv_fwd(x, w, 2)

        def post(pre):
            act = _silu(pre)
            nrm = act * lax.rsqrt(jnp.sum(act * act, axis=-1, keepdims=True) + EPS)
            return jnp.where(is_qk, nrm, act)

        _, vjp = jax.vjp(post, pre)
        (dpre,) = vjp(d)
        dx, dw = _conv_bwd(x, w, 2, dpre)
        dx_ref[0] = dx.astype(dx_ref.dtype)

        @pl.when(pl.program_id(1) == 0)
        def _():
            dw_ref[...] = jnp.zeros_like(dw_ref)

        dw_ref[...] += dw

    col = pl.BlockSpec((1, s, LANES), lambda j, b: (b, 0, j))
    return pl.pallas_call(
        body, name=name, grid=(nblk, bsz),
        in_specs=[pl.BlockSpec((1, s, LANES), lambda j, b: (b, 0, cb + j)), pl.BlockSpec((4, LANES), lambda j, b: (0, j)), col, col],
        out_specs=[col, pl.BlockSpec((4, LANES), lambda j, b: (0, j))],
        out_shape=[jax.ShapeDtypeStruct((bsz, s, 3 * DN_WIDTH), BF16), jax.ShapeDtypeStruct((4, 3 * DN_WIDTH), F32)],
        compiler_params=_cparams(("parallel", "arbitrary")),
    )(proj3, conv_w, dqkv_f, dqkv_b)


def _parts(x, n):
    out = []
    for _ in range(n):
        bits = lax.bitcast_convert_type(x, jnp.uint32) & jnp.uint32(0xFFFF0000)
        t = lax.bitcast_convert_type(bits, F32)
        out.append(t.astype(BF16))
        x = x - t
    return out


def _dg(x, y, cx, cy):
    return lax.dot_general(x, y, (((cx + 1,), (cy + 1,)), ((0,), (0,))), preferred_element_type=F32)


def _bmm(a, b):
    return _dg(a.astype(BF16), b.astype(BF16), 1, 0)


def _bmm_nt(a, b):
    return _dg(a.astype(BF16), b.astype(BF16), 1, 1)


def _bmm_tn(a, b):
    return _dg(a.astype(BF16), b.astype(BF16), 0, 0)


def _dot3_raw(a, b, ca, cb):
    a_hi, a_lo = _parts(a, 2)
    b_hi, b_lo = _parts(b, 2)
    return _dg(a_hi, b_hi, ca, cb) + (_dg(a_hi, b_lo, ca, cb) + _dg(a_lo, b_hi, ca, cb))


@jax.custom_vjp
def _sum_left(m, x):
    return sum(_dg(m, pt, 1, 0) for pt in _parts(x, 3))


def _sum_left_fwd(m, x):
    return _sum_left(m, x), m


def _sum_left_bwd(m, ct):
    return jnp.zeros_like(m), sum(_dg(m, pt, 0, 0) for pt in _parts(ct, 2))


_sum_left.defvjp(_sum_left_fwd, _sum_left_bwd)


@jax.custom_vjp
def _sum_right(x, m):
    return sum(_dg(pt, m, 0, 0) for pt in _parts(x, 3))


def _sum_right_fwd(x, m):
    return _sum_right(x, m), m


def _sum_right_bwd(m, ct):
    return sum(_dg(m, pt, 1, 1) for pt in _parts(ct, 2)), jnp.zeros_like(m)


_sum_right.defvjp(_sum_right_fwd, _sum_right_bwd)


def _unit_tri_inverse_raw(a):
    nu, c, _ = a.shape
    row = lax.broadcasted_iota(jnp.int32, (nu, c, c), 1)
    col = lax.broadcasted_iota(jnp.int32, (nu, c, c), 2)
    same = lambda n: (row // n) == (col // n)
    d = jnp.where(same(8), a, 0.0)
    tinv = jnp.where(row == col, 1.0, 0.0) - d
    pw = _dot3_raw(d, d, 1, 0)
    tinv = tinv + _dot3_raw(tinv, pw, 1, 0)
    pw = _dot3_raw(pw, pw, 1, 0)
    tinv = tinv + _dot3_raw(tinv, pw, 1, 0)
    n = 8
    while n < c:
        e = jnp.where(same(2 * n) & jnp.logical_not(same(n)), a, 0.0)
        tinv = tinv - _bmm(tinv, _bmm(e, tinv))
        n *= 2
    return tinv


@jax.custom_vjp
def _unit_tri_inverse(a):
    return _unit_tri_inverse_raw(a)


def _unit_tri_inverse_fwd(a):
    tinv = _unit_tri_inverse_raw(a)
    return tinv, tinv


def _unit_tri_inverse_bwd(tinv, ct):
    return (-_bmm_nt(_bmm_tn(tinv, ct), tinv),)


_unit_tri_inverse.defvjp(_unit_tri_inverse_fwd, _unit_tri_inverse_bwd)


def _dn_intra(q, k, v, g, beta, rev):
    nu, c, _ = q.shape
    row = lax.broadcasted_iota(jnp.int32, (nu, c, c), 1)
    col = lax.broadcasted_iota(jnp.int32, (nu, c, c), 2)
    incl = (row <= col) if rev else (row >= col)
    strict = (row < col) if rev else (row > col)
    ones_incl = jnp.where(incl, 1.0, 0.0).astype(BF16)
    ones_tr = jnp.where((row >= col) if rev else (row <= col), 1.0, 0.0).astype(BF16)
    gbc = jnp.broadcast_to(g, (nu, c, c))
    gc = _sum_left(ones_incl, gbc)
    gr = _sum_right(gbc, ones_tr)
    gcum = gc[:, :, 0:1]
    decay = jnp.where(incl, jnp.exp(jnp.where(incl, gc - gr, 0.0)), 0.0)
    qs = q * (DN_HEAD_DIM ** -0.5)
    kb = k * beta
    a = jnp.where(strict, _bmm_nt(kb, k) * decay, 0.0)
    tinv = _unit_tri_inverse(a)
    egc = jnp.exp(gcum)
    u = _bmm(tinv, v * beta)
    w = _bmm(tinv, kb * egc)
    attn = _bmm_nt(qs, k) * decay
    glast = gcum[:, 0:1, :] if rev else gcum[:, c - 1:c, :]
    q_dec = qs * egc
    k_dec = k * jnp.exp(glast - gcum)
    cdec = jnp.broadcast_to(jnp.exp(glast), (nu, 1, LANES))
    return u, w, q_dec, k_dec, attn, cdec


def _dn_rec(u, w, q_dec, k_dec, attn, cdec, state):
    v_new = u - _bmm(w, state)
    o = _bmm(q_dec, state) + _bmm(attn, v_new)
    return o, state * cdec + _bmm_tn(k_dec, v_new)


DN_INTRA_TOKENS = 512
DN_REC_TOKENS = 512
DN_REC_BWD_TOKENS = 256
DN_REC_EXAMPLES = 4


def _dn_gate_lanes(rev, h):
    lb = (DN_HEADS if rev else 0) + h
    return lb, N_GATE // 2 + lb


def _dn_intra_shapes(bsz, s):
    n = s // CHUNK
    return [jax.ShapeDtypeStruct((bsz, s, DN_WIDTH), F32), jax.ShapeDtypeStruct((bsz, s, DN_WIDTH), BF16),
            jax.ShapeDtypeStruct((bsz, s, DN_WIDTH), BF16), jax.ShapeDtypeStruct((bsz, s, DN_WIDTH), BF16),
            jax.ShapeDtypeStruct((bsz, n, DN_HEADS, CHUNK, CHUNK), BF16), jax.ShapeDtypeStruct((bsz, n, DN_HEADS, 1, LANES), F32)]


def _dn_intra_specs(tb, ix):
    nc = tb // CHUNK
    ix5 = lambda b, j: ix(b, j) + (0, 0)
    row = pl.BlockSpec((1, tb, DN_WIDTH), ix)
    return [row, row, row, row, pl.BlockSpec((1, nc, DN_HEADS, CHUNK, CHUNK), ix5), pl.BlockSpec((1, nc, DN_HEADS, 1, LANES), ix5)]


def _dn_units(nc):
    return [(ci, h) for ci in range(nc) for h in range(DN_HEADS)]


def _dn_load_units(qkv_ref, gb_ref, nc, rev):
    qs, ks, vs, gs, bs = [], [], [], [], []
    for ci, h in _dn_units(nc):
        rows = slice(ci * CHUNK, (ci + 1) * CHUNK)
        lb, lg = _dn_gate_lanes(rev, h)
        qs.append(qkv_ref[0, rows, h * LANES:(h + 1) * LANES])
        ks.append(qkv_ref[0, rows, DN_WIDTH + h * LANES:DN_WIDTH + (h + 1) * LANES])
        vs.append(qkv_ref[0, rows, 2 * DN_WIDTH + h * LANES:2 * DN_WIDTH + (h + 1) * LANES])
        gs.append(gb_ref[0, rows, lg:lg + 1])
        bs.append(gb_ref[0, rows, lb:lb + 1])
    return jnp.stack(qs), jnp.stack(ks), jnp.stack(vs), jnp.stack(gs), jnp.stack(bs)


def _dn_intra_fwd(qkv, gb, rev, name):
    bsz, s, _ = qkv.shape
    tb = min(DN_INTRA_TOKENS, s)
    nc = tb // CHUNK

    def body(qkv_ref, gb_ref, u_ref, w_ref, qd_ref, kd_ref, at_ref, cd_ref):
        q, k, v, g, beta = _dn_load_units(qkv_ref, gb_ref, nc, rev)
        u, w, qd, kd, at, cd = _dn_intra(q, k, v, g, beta, rev)
        for i, (ci, h) in enumerate(_dn_units(nc)):
            rows = slice(ci * CHUNK, (ci + 1) * CHUNK)
            cols = slice(h * LANES, (h + 1) * LANES)
            u_ref[0, rows, cols] = u[i]
            w_ref[0, rows, cols] = w[i].astype(BF16)
            qd_ref[0, rows, cols] = qd[i].astype(BF16)
            kd_ref[0, rows, cols] = kd[i].astype(BF16)
            at_ref[0, ci, h] = at[i].astype(BF16)
            cd_ref[0, ci, h] = cd[i]

    ix = lambda b, j: (b, j, 0)
    return pl.pallas_call(
        body, name=name, grid=(bsz, s // tb),
        in_specs=[pl.BlockSpec((1, tb, 3 * DN_WIDTH), ix), pl.BlockSpec((1, tb, LANES), ix)],
        out_specs=_dn_intra_specs(tb, ix), out_shape=_dn_intra_shapes(bsz, s),
        compiler_params=_cparams(("parallel", "parallel")),
    )(qkv, gb)


def _dn_intra_bwd(qkv, gb, cts, rev, name):
    bsz, s, _ = qkv.shape
    tb = min(DN_INTRA_TOKENS, s)
    nc = tb // CHUNK

    def body(qkv_ref, gb_ref, du_ref, dw_ref, dqd_ref, dkd_ref, dat_ref, dcd_ref, dqkv_ref, dgb_ref):
        units = _dn_units(nc)
        q, k, v, g, beta = _dn_load_units(qkv_ref, gb_ref, nc, rev)
        _, vjp = jax.vjp(functools.partial(_dn_intra, rev=rev), q, k, v, g, beta)
        tok = lambda ref: jnp.stack([ref[0, ci * CHUNK:(ci + 1) * CHUNK, h * LANES:(h + 1) * LANES] for ci, h in units])
        per = lambda ref: jnp.stack([ref[0, ci, h] for ci, h in units])
        dq, dk, dv, dg, dbeta = vjp((tok(du_ref), tok(dw_ref), tok(dqd_ref), tok(dkd_ref), per(dat_ref), per(dcd_ref)))
        lane = lax.broadcasted_iota(jnp.int32, (CHUNK, LANES), 1)
        for ci in range(nc):
            rows = slice(ci * CHUNK, (ci + 1) * CHUNK)
            dgates = jnp.zeros((CHUNK, LANES), F32)
            for h in range(DN_HEADS):
                i = units.index((ci, h))
                lb, lg = _dn_gate_lanes(rev, h)
                dqkv_ref[0, rows, h * LANES:(h + 1) * LANES] = dq[i]
                dqkv_ref[0, rows, DN_WIDTH + h * LANES:DN_WIDTH + (h + 1) * LANES] = dk[i]
                dqkv_ref[0, rows, 2 * DN_WIDTH + h * LANES:2 * DN_WIDTH + (h + 1) * LANES] = dv[i]
                dgates = dgates + jnp.where(lane == lb, dbeta[i], 0.0) + jnp.where(lane == lg, dg[i], 0.0)
            dgb_ref[0, rows, :] = dgates

    ix = lambda b, j: (b, j, 0)
    ix5 = lambda b, j: (b, j, 0, 0, 0)
    row = pl.BlockSpec((1, tb, DN_WIDTH), ix)
    return pl.pallas_call(
        body, name=name, grid=(bsz, s // tb),
        in_specs=[pl.BlockSpec((1, tb, 3 * DN_WIDTH), ix), pl.BlockSpec((1, tb, LANES), ix), row, row, row, row,
                  pl.BlockSpec((1, nc, DN_HEADS, CHUNK, CHUNK), ix5), pl.BlockSpec((1, nc, DN_HEADS, 1, LANES), ix5)],
        out_specs=[pl.BlockSpec((1, tb, 3 * DN_WIDTH), ix), pl.BlockSpec((1, tb, LANES), ix)],
        out_shape=[jax.ShapeDtypeStruct((bsz, s, 3 * DN_WIDTH), F32), jax.ShapeDtypeStruct((bsz, s, LANES), F32)],
        compiler_params=_cparams(("parallel", "parallel")),
    )(qkv, gb, *cts)


def _dn_rec_examples(bsz):
    return max(n for n in range(1, DN_REC_EXAMPLES + 1) if bsz % n == 0)


def _dn_rec_specs(nb, tb, ix):
    nc = tb // CHUNK
    ix5 = lambda b, j: ix(b, j) + (0, 0)
    row = pl.BlockSpec((nb, tb, DN_WIDTH), ix)
    return [row, row, row, row, pl.BlockSpec((nb, nc, DN_HEADS, CHUNK, CHUNK), ix5), pl.BlockSpec((nb, nc, DN_HEADS, 1, LANES), ix5)]


def _dn_rec_fwd(intra, rev, name):
    u = intra[0]
    bsz, s, _ = u.shape
    nb = _dn_rec_examples(bsz)
    tb = min(DN_REC_TOKENS, s)
    nt = s // tb
    nc = tb // CHUNK
    nh = nb * DN_HEADS

    def body(u_ref, w_ref, qd_ref, kd_ref, at_ref, cd_ref, o_ref, st_ref, state):
        @pl.when(pl.program_id(1) == 0)
        def _():
            state[...] = jnp.zeros_like(state)

        def step(ci, carry):
            cidx = (nc - 1 - ci) if rev else ci
            rows = pl.ds(pl.multiple_of(cidx * CHUNK, CHUNK), CHUNK)
            heads = lambda ref: jnp.stack([ref[e, rows, h * LANES:(h + 1) * LANES] for e in range(nb) for h in range(DN_HEADS)])
            per = lambda ref: jnp.concatenate([ref[e, cidx] for e in range(nb)])
            st = state[...]
            o, new_state = _dn_rec(heads(u_ref), heads(w_ref), heads(qd_ref), heads(kd_ref), per(at_ref), per(cd_ref), st)
            state[...] = new_state
            for e in range(nb):
                st_ref[e, cidx] = st[e * DN_HEADS:(e + 1) * DN_HEADS]
                o_ref[e, rows, :] = jnp.concatenate([o[e * DN_HEADS + h] for h in range(DN_HEADS)], axis=-1)
            return carry

        lax.fori_loop(0, nc, step, 0)

    ix = (lambda b, j: (b, nt - 1 - j, 0)) if rev else (lambda b, j: (b, j, 0))
    ix5 = lambda b, j: ix(b, j) + (0, 0)
    return pl.pallas_call(
        body, name=name, grid=(bsz // nb, nt), in_specs=_dn_rec_specs(nb, tb, ix),
        out_specs=[pl.BlockSpec((nb, tb, DN_WIDTH), ix), pl.BlockSpec((nb, nc, DN_HEADS, DN_HEAD_DIM, DN_HEAD_DIM), ix5)],
        out_shape=[jax.ShapeDtypeStruct((bsz, s, DN_WIDTH), F32),
                   jax.ShapeDtypeStruct((bsz, s // CHUNK, DN_HEADS, DN_HEAD_DIM, DN_HEAD_DIM), F32)],
        scratch_shapes=[pltpu.VMEM((nh, DN_HEAD_DIM, DN_HEAD_DIM), F32)],
        compiler_params=_cparams(("parallel", "arbitrary")),
    )(*intra)


def _dn_rec_bwd(intra, states, do, rev, name):
    u = intra[0]
    bsz, s, _ = u.shape
    nb = _dn_rec_examples(bsz)
    tb = min(DN_REC_BWD_TOKENS, s)
    nt = s // tb
    nc = tb // CHUNK
    nh = nb * DN_HEADS

    def body(u_ref, w_ref, qd_ref, kd_ref, at_ref, cd_ref, st_ref, do_ref,
             du_ref, dw_ref, dqd_ref, dkd_ref, dat_ref, dcd_ref, dstate):
        @pl.when(pl.program_id(1) == 0)
        def _():
            dstate[...] = jnp.zeros_like(dstate)

        def step(ci, carry):
            cidx = ci if rev else (nc - 1 - ci)
            rows = pl.ds(pl.multiple_of(cidx * CHUNK, CHUNK), CHUNK)
            heads = lambda ref: jnp.stack([ref[e, rows, h * LANES:(h + 1) * LANES] for e in range(nb) for h in range(DN_HEADS)])
            per = lambda ref: jnp.concatenate([ref[e, cidx] for e in range(nb)])
            args = (heads(u_ref), heads(w_ref).astype(F32), heads(qd_ref).astype(F32), heads(kd_ref).astype(F32),
                    per(at_ref).astype(F32), per(cd_ref), per(st_ref))
            _, vjp = jax.vjp(_dn_rec, *args)
            du, dw, dqd, dkd, dat, dcd, dst = vjp((heads(do_ref), dstate[...]))
            dstate[...] = dst
            for e in range(nb):
                hs = slice(e * DN_HEADS, (e + 1) * DN_HEADS)
                dat_ref[e, cidx] = dat[hs]
                dcd_ref[e, cidx] = dcd[hs]
                for ref, val in ((du_ref, du), (dw_ref, dw), (dqd_ref, dqd), (dkd_ref, dkd)):
                    ref[e, rows, :] = jnp.concatenate([val[e * DN_HEADS + h] for h in range(DN_HEADS)], axis=-1)
            return carry

        lax.fori_loop(0, nc, step, 0)

    ix = (lambda b, j: (b, j, 0)) if rev else (lambda b, j: (b, nt - 1 - j, 0))
    ix5 = lambda b, j: ix(b, j) + (0, 0)
    row = pl.BlockSpec((nb, tb, DN_WIDTH), ix)
    f32 = lambda sd: jax.ShapeDtypeStruct(sd.shape, F32)
    return pl.pallas_call(
        body, name=name, grid=(bsz // nb, nt),
        in_specs=_dn_rec_specs(nb, tb, ix) + [pl.BlockSpec((nb, nc, DN_HEADS, DN_HEAD_DIM, DN_HEAD_DIM), ix5), row],
        out_specs=_dn_rec_specs(nb, tb, ix), out_shape=[f32(sd) for sd in _dn_intra_shapes(bsz, s)],
        scratch_shapes=[pltpu.VMEM((nh, DN_HEAD_DIM, DN_HEAD_DIM), F32)],
        compiler_params=_cparams(("parallel", "arbitrary")),
    )(*intra, states, do)


def _lru_gate_fn(xc, wa, ba, wx, bx, lam):
    r = _sigmoid(_bdot(xc, wa) + ba)
    ig = _sigmoid(_bdot(xc, wx) + bx)
    log_a = -LRU_C * r * _softplus(-lam)
    a = jnp.exp(log_a)
    b = jnp.sqrt(-jnp.tanh(log_a) * (a * a + 1.0)) * (ig * xc)
    return a, b


def _lru_fwd(proj3, conv_w, conv_b, wa, ba, wx, bx, lam, name):
    bsz, s, _ = proj3.shape
    nblk = LRU_WIDTH // LANES

    def body(x_ref, cw_ref, cb_ref, wa_ref, ba_ref, wx_ref, bx_ref, lam_ref, hf_ref, hb_ref):
        xc = _conv_fwd(x_ref[0], cw_ref[...], 2) + cb_ref[...]
        for d, h_ref in ((0, hf_ref), (1, hb_ref)):
            a, b = _lru_gate_fn(xc, wa_ref[d, 0], ba_ref[d:d + 1, :], wx_ref[d, 0], bx_ref[d:d + 1, :], lam_ref[d:d + 1, :])
            h_ref[0] = _scan(a, b, reverse=(d == 1))

    col = pl.BlockSpec((1, s, LANES), lambda b, j: (b, 0, j))
    vec2 = pl.BlockSpec((2, LANES), lambda b, j: (0, j))
    wspec = pl.BlockSpec((2, 1, LANES, LANES), lambda b, j: (0, j, 0, 0))
    return pl.pallas_call(
        body, name=name, grid=(bsz, nblk),
        in_specs=[col, pl.BlockSpec((4, LANES), lambda b, j: (0, j)), pl.BlockSpec((1, LANES), lambda b, j: (0, j)),
                  wspec, vec2, wspec, vec2, vec2],
        out_specs=[col, col],
        out_shape=[jax.ShapeDtypeStruct((bsz, s, LRU_WIDTH), F32)] * 2,
        compiler_params=_cparams(("parallel", "parallel")),
    )(proj3, conv_w, conv_b, wa, ba, wx, bx, lam)


def _lru_bwd(proj3, conv_w, conv_b, wa, ba, wx, bx, lam, hf, hb, dh, name):
    bsz, s, _ = proj3.shape
    nblk = LRU_WIDTH // LANES

    def body(x_ref, cw_ref, cb_ref, wa_ref, ba_ref, wx_ref, bx_ref, lam_ref, hf_ref, hb_ref, dh_ref,
             dx_ref, dcw_ref, dcb_ref, dwa_ref, dba_ref, dwx_ref, dbx_ref, dlam_ref):
        @pl.when(pl.program_id(1) == 0)
        def _():
            for r in (dcw_ref, dcb_ref, dwa_ref, dba_ref, dwx_ref, dbx_ref, dlam_ref):
                r[...] = jnp.zeros_like(r)

        x, cw = x_ref[0], cw_ref[...]
        xc = _conv_fwd(x, cw, 2) + cb_ref[...]
        dhv = dh_ref[0]
        dxc = jnp.zeros_like(xc)
        for d, h_ref in ((0, hf_ref), (1, hb_ref)):
            rev = d == 1
            args = (xc, wa_ref[d, 0].astype(F32), ba_ref[d:d + 1, :], wx_ref[d, 0].astype(F32), bx_ref[d:d + 1, :],
                    lam_ref[d:d + 1, :])
            (a, _), vjp = jax.vjp(_lru_gate_fn, *args)
            h = h_ref[0]
            a_next = _shift_down(a, 1) if rev else _shift_up(a, 1)
            lam_adj = _scan(a_next, dhv, reverse=not rev)
            h_prev = _shift_up(h, 1) if rev else _shift_down(h, 1)
            dxc_d, dwa, dba, dwx, dbx, dlam = vjp((lam_adj * h_prev, lam_adj))
            dxc = dxc + dxc_d
            dwa_ref[d, 0] += dwa
            dwx_ref[d, 0] += dwx
            dba_ref[d:d + 1, :] += dba
            dbx_ref[d:d + 1, :] += dbx
            dlam_ref[d:d + 1, :] += dlam
        dx, dcw = _conv_bwd(x, cw, 2, dxc)
        dx_ref[0] = dx.astype(dx_ref.dtype)
        dcw_ref[...] += dcw
        dcb_ref[...] += jnp.sum(dxc, axis=0, keepdims=True)

    col = pl.BlockSpec((1, s, LANES), lambda j, b: (b, 0, j))
    vec1 = pl.BlockSpec((1, LANES), lambda j, b: (0, j))
    vec2 = pl.BlockSpec((2, LANES), lambda j, b: (0, j))
    vec4 = pl.BlockSpec((4, LANES), lambda j, b: (0, j))
    wspec = pl.BlockSpec((2, 1, LANES, LANES), lambda j, b: (0, j, 0, 0))
    wshape = jax.ShapeDtypeStruct((2, nblk, LANES, LANES), F32)
    v2shape = jax.ShapeDtypeStruct((2, LRU_WIDTH), F32)
    return pl.pallas_call(
        body, name=name, grid=(nblk, bsz),
        in_specs=[col, vec4, vec1, wspec, vec2, wspec, vec2, vec2, col, col, col],
        out_specs=[col, vec4, vec1, wspec, vec2, wspec, vec2, vec2],
        out_shape=[jax.ShapeDtypeStruct((bsz, s, LRU_WIDTH), BF16), jax.ShapeDtypeStruct((4, LRU_WIDTH), F32),
                   jax.ShapeDtypeStruct((1, LRU_WIDTH), F32), wshape, v2shape, wshape, v2shape, v2shape],
        compiler_params=_cparams(("parallel", "arbitrary")),
    )(proj3, conv_w, conv_b, wa, ba, wx, bx, lam, hf, hb, dh)


def _mix_fn(o_f, o_b, z, lg, hf, hb, dn_g, lru_g):
    osum = o_f + o_b
    heads = []
    for h in range(DN_HEADS):
        sl = slice(h * LANES, (h + 1) * LANES)
        heads.append(_rms(osum[:, sl], dn_g) * _silu(z[:, sl]))
    lru = _rms(_gelu(lg) * (hf + hb), lru_g)
    return jnp.concatenate(heads + [lru], axis=-1)


def _mix_specs(tm):
    w = DN_WIDTH
    row = pl.BlockSpec((tm, w), lambda i: (i, 0))
    z = pl.BlockSpec((tm, w), lambda i: (i, P_Z // w))
    lg = pl.BlockSpec((tm, w), lambda i: (i, P_LG // w))
    return [row, row, z, lg, row, row, pl.BlockSpec((1, LANES), lambda i: (0, 0)), pl.BlockSpec((1, w), lambda i: (0, 0))]


def _mix_fwd(o_f, o_b, proj, hf, hb, dn_g, lru_g, name):
    t = proj.shape[0]
    tm = _row_tile(t)

    def body(of_ref, ob_ref, z_ref, lg_ref, hf_ref, hb_ref, dg_ref, lgn_ref, o_ref):
        o_ref[...] = _mix_fn(of_ref[...], ob_ref[...], z_ref[...], lg_ref[...], hf_ref[...], hb_ref[...],
                             dg_ref[...], lgn_ref[...]).astype(o_ref.dtype)

    return pl.pallas_call(
        body, name=name, grid=(t // tm,), in_specs=_mix_specs(tm),
        out_specs=pl.BlockSpec((tm, D_MODEL), lambda i: (i, 0)),
        out_shape=jax.ShapeDtypeStruct((t, D_MODEL), BF16),
        compiler_params=_cparams(("parallel",)),
    )(o_f, o_b, proj, proj, hf, hb, dn_g, lru_g)


def _mix_bwd(o_f, o_b, proj, hf, hb, dn_g, lru_g, dmix, name):
    t = proj.shape[0]
    tm = _row_tile(t)

    def body(of_ref, ob_ref, z_ref, lg_ref, hf_ref, hb_ref, dg_ref, lgn_ref, dm_ref,
             do_ref, dz_ref, dlg_ref, dh_ref, ddg_ref, dlgn_ref):
        _, vjp = jax.vjp(_mix_fn, of_ref[...], ob_ref[...], z_ref[...], lg_ref[...], hf_ref[...], hb_ref[...],
                         dg_ref[...], lgn_ref[...])
        do, _, dz, dlg, dh, _, ddg, dlgn = vjp(dm_ref[...])
        do_ref[...] = do
        dz_ref[...] = dz.astype(dz_ref.dtype)
        dlg_ref[...] = dlg.astype(dlg_ref.dtype)
        dh_ref[...] = dh

        @pl.when(pl.program_id(0) == 0)
        def _():
            ddg_ref[...] = jnp.zeros_like(ddg_ref)
            dlgn_ref[...] = jnp.zeros_like(dlgn_ref)

        ddg_ref[...] += ddg
        dlgn_ref[...] += dlgn

    row = pl.BlockSpec((tm, DN_WIDTH), lambda i: (i, 0))
    return pl.pallas_call(
        body, name=name, grid=(t // tm,),
        in_specs=_mix_specs(tm) + [pl.BlockSpec((tm, D_MODEL), lambda i: (i, 0))],
        out_specs=[row, row, row, row, pl.BlockSpec((1, LANES), lambda i: (0, 0)), pl.BlockSpec((1, DN_WIDTH), lambda i: (0, 0))],
        out_shape=[jax.ShapeDtypeStruct((t, DN_WIDTH), dt) for dt in (F32, BF16, BF16, F32)]
        + [jax.ShapeDtypeStruct((1, LANES), F32), jax.ShapeDtypeStruct((1, DN_WIDTH), F32)],
        compiler_params=_cparams(("arbitrary",)),
    )(o_f, o_b, proj, proj, hf, hb, dn_g, lru_g, dmix)


def _ffn_act_fwd(g3, u3, conv_w, conv_b, name):
    bsz, s, f = g3.shape
    nblk = f // LANES

    def body(g_ref, u_ref, w_ref, b_ref, o_ref):
        gate = _conv_fwd(g_ref[0], w_ref[...], 1) + b_ref[...]
        o_ref[0] = (_gelu(gate) * u_ref[0]).astype(o_ref.dtype)

    col = pl.BlockSpec((1, s, LANES), lambda b, j: (b, 0, j))
    return pl.pallas_call(
        body, name=name, grid=(bsz, nblk),
        in_specs=[col, col, pl.BlockSpec((3, LANES), lambda b, j: (0, j)), pl.BlockSpec((1, LANES), lambda b, j: (0, j))],
        out_specs=col, out_shape=jax.ShapeDtypeStruct((bsz, s, f), BF16),
        compiler_params=_cparams(("parallel", "parallel")),
    )(g3, u3, conv_w, conv_b)


def _ffn_act_bwd(g3, u3, conv_w, conv_b, dact3, name):
    bsz, s, f = g3.shape
    nblk = f // LANES

    def body(g_ref, u_ref, w_ref, b_ref, d_ref, dg_ref, du_ref, dw_ref, db_ref):
        g, w, u = g_ref[0], w_ref[...], u_ref[0]
        gate = _conv_fwd(g, w, 1) + b_ref[...]
        _, vjp = jax.vjp(lambda gt, uu: _gelu(gt) * uu, gate, u)
        dgate, du = vjp(d_ref[0])
        dg, dw = _conv_bwd(g, w, 1, dgate)
        dg_ref[0] = dg.astype(dg_ref.dtype)
        du_ref[0] = du.astype(du_ref.dtype)

        @pl.when(pl.program_id(1) == 0)
        def _():
            dw_ref[...] = jnp.zeros_like(dw_ref)
            db_ref[...] = jnp.zeros_like(db_ref)

        dw_ref[...] += dw
        db_ref[...] += jnp.sum(dgate, axis=0, keepdims=True)

    col = pl.BlockSpec((1, s, LANES), lambda j, b: (b, 0, j))
    w3 = pl.BlockSpec((3, LANES), lambda j, b: (0, j))
    w1 = pl.BlockSpec((1, LANES), lambda j, b: (0, j))
    return pl.pallas_call(
        body, name=name, grid=(nblk, bsz),
        in_specs=[col, col, w3, w1, col], out_specs=[col, col, w3, w1],
        out_shape=[jax.ShapeDtypeStruct((bsz, s, f), BF16), jax.ShapeDtypeStruct((bsz, s, f), BF16),
                   jax.ShapeDtypeStruct((3, f), F32), jax.ShapeDtypeStruct((1, f), F32)],
        compiler_params=_cparams(("parallel", "arbitrary")),
    )(g3, u3, conv_w, conv_b, dact3)


def _ple_fn(r, pg, pp, bg):
    return r + _sigmoid(pg + bg) * pp


def _ple_fwd(r, pg, pp, bg, name):
    t, d = r.shape
    tm = _row_tile(t)

    def body(r_ref, pg_ref, pp_ref, bg_ref, o_ref):
        o_ref[...] = _ple_fn(r_ref[...], pg_ref[...], pp_ref[...], bg_ref[...])

    row = pl.BlockSpec((tm, d), lambda i: (i, 0))
    return pl.pallas_call(
        body, name=name, grid=(t // tm,), in_specs=[row, row, row, pl.BlockSpec((1, d), lambda i: (0, 0))],
        out_specs=row, out_shape=jax.ShapeDtypeStruct((t, d), F32),
        compiler_params=_cparams(("parallel",)),
    )(r, pg, pp, bg)


def _ple_bwd(pg, pp, bg, dr, name):
    t, d = pg.shape
    tm = _row_tile(t)

    def body(pg_ref, pp_ref, bg_ref, dr_ref, dpg_ref, dpp_ref, dbg_ref):
        _, vjp = jax.vjp(lambda a, b, c: _sigmoid(a + c) * b, pg_ref[...], pp_ref[...], bg_ref[...])
        dpg, dpp, dbg = vjp(dr_ref[...])
        dpg_ref[...] = dpg.astype(dpg_ref.dtype)
        dpp_ref[...] = dpp.astype(dpp_ref.dtype)

        @pl.when(pl.program_id(0) == 0)
        def _():
            dbg_ref[...] = jnp.zeros_like(dbg_ref)

        dbg_ref[...] += dbg

    row = pl.BlockSpec((tm, d), lambda i: (i, 0))
    vec = pl.BlockSpec((1, d), lambda i: (0, 0))
    return pl.pallas_call(
        body, name=name, grid=(t // tm,), in_specs=[row, row, vec, row], out_specs=[row, row, vec],
        out_shape=[jax.ShapeDtypeStruct((t, d), BF16), jax.ShapeDtypeStruct((t, d), BF16), jax.ShapeDtypeStruct((1, d), F32)],
        compiler_params=_cparams(("arbitrary",)),
    )(pg, pp, bg, dr)


def _loss_head(r, g, target, name):
    t, d = r.shape
    tm = _row_tile(t)

    def loss_fn(x, gg, tgt):
        err = _rms(x, gg) - tgt
        return 0.5 * jnp.sum(jnp.sum(err * err, axis=-1, keepdims=True) * (1.0 / d), axis=0, keepdims=True)

    def body(r_ref, g_ref, t_ref, l_ref, dr_ref, dg_ref):
        val, vjp = jax.vjp(lambda x, gg: loss_fn(x, gg, t_ref[...]), r_ref[...], g_ref[...])
        dx, dg = vjp(jnp.ones((1, 1), F32))
        dr_ref[...] = dx

        @pl.when(pl.program_id(0) == 0)
        def _():
            l_ref[...] = jnp.zeros_like(l_ref)
            dg_ref[...] = jnp.zeros_like(dg_ref)

        l_ref[...] += val
        dg_ref[...] += dg

    row = pl.BlockSpec((tm, d), lambda i: (i, 0))
    vec = pl.BlockSpec((1, d), lambda i: (0, 0))
    one = pl.BlockSpec((1, 1), lambda i: (0, 0))
    return pl.pallas_call(
        body, name=name, grid=(t // tm,), in_specs=[row, vec, row], out_specs=[one, row, vec],
        out_shape=[jax.ShapeDtypeStruct((1, 1), F32), jax.ShapeDtypeStruct((t, d), F32), jax.ShapeDtypeStruct((1, d), F32)],
        compiler_params=_cparams(("arbitrary",)),
    )(r, g, target)


def _adamw_math(w, gg, m, v, d_ref, nm_ref, nv_ref):
    nm = ADAM_B1 * m + (1.0 - ADAM_B1) * gg
    nv = ADAM_B2 * v + (1.0 - ADAM_B2) * (gg * gg)
    m_hat = nm / (1.0 - ADAM_B1 ** ADAM_STEP)
    v_hat = nv / (1.0 - ADAM_B2 ** ADAM_STEP)
    d_ref[...] = -ADAM_LR * (m_hat / (jnp.sqrt(v_hat) + ADAM_EPS) + ADAM_WD * w)
    nm_ref[...] = nm
    nv_ref[...] = nv


def _adamw(w, g, m, v, name):
    def body(w_ref, g_ref, m_ref, v_ref, d_ref, nm_ref, nv_ref):
        _adamw_math(w_ref[...], g_ref[...], m_ref[...], v_ref[...], d_ref, nm_ref, nv_ref)

    whole = pl.BlockSpec(memory_space=pltpu.VMEM)
    return pl.pallas_call(
        body, name=name, in_specs=[whole] * 4, out_specs=[whole] * 3, out_shape=[jax.ShapeDtypeStruct(w.shape, F32)] * 3,
        compiler_params=pltpu.CompilerParams(vmem_limit_bytes=VMEM_LIMIT),
    )(w, g, m, v)


def _prepare_weights(w):
    nl = w["w_in"].shape[0]
    w_in = w["w_in"].astype(BF16)
    pad = jnp.zeros(w_in.shape[:2] + (P_COLS - w_in.shape[2],), BF16)
    split = 4 * DN_WIDTH + N_GATE
    w_in_p = jnp.concatenate([w_in[:, :, split:], w_in[:, :, :split], pad], axis=-1)
    gate_vec = lambda a: jnp.pad(a.reshape(nl, 1, N_GATE // 2), ((0, 0), (0, 0), (N_GATE // 2, LANES - N_GATE)))

    def pair_blocks(a):
        a = a.reshape(nl, 2, 4, 2, 64, 64)
        z = jnp.zeros_like(a[:, :, :, 0])
        top = jnp.concatenate([a[:, :, :, 0], z], axis=-1)
        bot = jnp.concatenate([z, a[:, :, :, 1]], axis=-1)
        return jnp.concatenate([top, bot], axis=-2).astype(BF16)

    bf = lambda a: a.astype(BF16)
    return dict(
        norm1_g=w["norm1_g"][:, None, :], w_in=w_in_p,
        dn_conv_w=w["dn_conv_w"], alog=gate_vec(w["dn_a_log"]), dtb=gate_vec(w["dn_dt_bias"]),
        dn_norm_g=w["dn_norm_g"][:, None, :], lru_conv_w=w["lru_conv_w"], lru_conv_b=w["lru_conv_b"][:, None, :],
        lru_wa=pair_blocks(w["lru_wa"]), lru_ba=w["lru_ba"], lru_wx=pair_blocks(w["lru_wx"]), lru_bx=w["lru_bx"],
        lru_lambda=w["lru_lambda"], lru_norm_g=w["lru_norm_g"][:, None, :],
        w_out=bf(w["w_out"]), norm2_g=w["norm2_g"][:, None, :], ffn_wg=bf(w["ffn_wg"]), ffn_wu=bf(w["ffn_wu"]),
        ffn_conv_w=w["ffn_conv_w"], ffn_conv_b=w["ffn_conv_b"][:, None, :], ffn_wd=bf(w["ffn_wd"]),
        ple_norm_g=w["ple_norm_g"][:, None, :], ple_wg=bf(w["ple_wg"]), ple_bg=w["ple_bg"][:, None, :], ple_wp=bf(w["ple_wp"]),
        final_g=w["final_g"][None, :],
    )


def _unpair_blocks(a):
    top = a[:, :, :64, :64]
    bot = a[:, :, 64:, 64:]
    return jnp.stack([top, bot], axis=2).reshape(2, 8, 64, 64)


def _local_step(x, p, target, w):
    bsz, s, d = x.shape
    t = bsz * s
    nl = w["w_in"].shape[0]
    kw = _prepare_weights(w)
    flat = lambda a: a.reshape(t, a.shape[-1])
    seq = lambda a: a.reshape(bsz, s, a.shape[-1])

    saved = []
    r = flat(x)
    for i in range(nl):
        n = f"l{i}_"
        sv = {"r0": r}
        h, proj = _norm_matmul(r, kw["norm1_g"][i], [kw["w_in"][i]], n + "in_proj")
        gb = _dn_gates_fwd(proj, kw["alog"][i], kw["dtb"][i], n + "dn_gates")
        qkv = _dn_prep_fwd(seq(proj), kw["dn_conv_w"][i], n + "dn_prep")
        in_f = _dn_intra_fwd(qkv, seq(gb), False, n + "dn_intra_fwd")
        in_b = _dn_intra_fwd(qkv, seq(gb), True, n + "dn_intra_rev")
        o_f, st_f = _dn_rec_fwd(in_f, False, n + "dn_rec_fwd")
        o_b, st_b = _dn_rec_fwd(in_b, True, n + "dn_rec_rev")
        lru_args = (seq(proj), kw["lru_conv_w"][i], kw["lru_conv_b"][i], kw["lru_wa"][i], kw["lru_ba"][i],
                    kw["lru_wx"][i], kw["lru_bx"][i], kw["lru_lambda"][i])
        hf, hb = _lru_fwd(*lru_args, n + "lru")
        mix_args = (flat(o_f), flat(o_b), proj, flat(hf), flat(hb), kw["dn_norm_g"][i], kw["lru_norm_g"][i])
        mix = _mix_fwd(*mix_args, n + "mix")
        r1 = _matmul([(mix, kw["w_out"][i])], n + "out_proj", res=r)
        h2, fg, fu = _norm_matmul(r1, kw["norm2_g"][i], [kw["ffn_wg"][i], kw["ffn_wu"][i]], n + "ffn_gu")
        act = _ffn_act_fwd(seq(fg), seq(fu), kw["ffn_conv_w"][i], kw["ffn_conv_b"][i], n + "ffn_act")
        r2 = _matmul([(flat(act), kw["ffn_wd"][i])], n + "ffn_d", res=r1)
        hp, pg = _norm_matmul(r2, kw["ple_norm_g"][i], [kw["ple_wg"][i]], n + "ple_g")
        pi = flat(p[i])
        pp = _matmul([(pi, kw["ple_wp"][i])], n + "ple_p")
        r3 = _ple_fwd(r2, pg, pp, kw["ple_bg"][i], n + "ple")
        sv.update(h=h, proj=proj, gb=gb, qkv=qkv, st_f=st_f, st_b=st_b, in_f=in_f, in_b=in_b, lru_args=lru_args, hf=hf, hb=hb,
                  mix_args=mix_args, mix=mix, r1=r1, h2=h2, fg=fg, fu=fu, act=act, r2=r2, hp=hp, pg=pg, pp=pp, pi=pi)
        saved.append(sv)
        r = r3

    loss, dr, dfinal = _loss_head(r, kw["final_g"], flat(target), "loss_head")
    grads = {k: [None] * nl for k in (
        "norm1_g", "w_in", "dn_conv_w", "dn_a_log", "dn_dt_bias", "dn_norm_g", "lru_conv_w", "lru_conv_b", "lru_wa", "lru_ba",
        "lru_wx", "lru_bx", "lru_lambda", "lru_norm_g", "w_out", "norm2_g", "ffn_wg", "ffn_wu", "ffn_conv_w", "ffn_conv_b",
        "ffn_wd", "ple_norm_g", "ple_wg", "ple_bg", "ple_wp", "dn_a_log_lanes", "dn_dt_bias_lanes")}

    for i in reversed(range(nl)):
        n = f"b{i}_"
        sv = saved[i]
        dpg, dpp, dbg = _ple_bwd(sv["pg"], sv["pp"], kw["ple_bg"][i], dr, n + "ple")
        grads["ple_bg"][i] = dbg[0]
        grads["ple_wp"][i] = _matmul_tn(sv["pi"], dpp, n + "ple_wp")
        grads["ple_wg"][i] = _matmul_tn(sv["hp"], dpg, n + "ple_wg")
        dr2, dr2b, dg = _matmul_norm_bwd([(dpg, kw["ple_wg"][i])], sv["r2"], kw["ple_norm_g"][i], dr, n + "ple_dh")
        grads["ple_norm_g"][i] = dg[0]
        grads["ffn_wd"][i] = _matmul_tn(flat(sv["act"]), dr2b, n + "ffn_wd")
        dact = _matmul([(dr2b, kw["ffn_wd"][i])], n + "ffn_dact", nt=True)
        dfg, dfu, dcw, dcb = _ffn_act_bwd(seq(sv["fg"]), seq(sv["fu"]), kw["ffn_conv_w"][i], kw["ffn_conv_b"][i], seq(dact),
                                          n + "ffn_act")
        grads["ffn_conv_w"][i] = dcw
        grads["ffn_conv_b"][i] = dcb[0]
        grads["ffn_wg"][i] = _matmul_tn(sv["h2"], flat(dfg), n + "ffn_wg")
        grads["ffn_wu"][i] = _matmul_tn(sv["h2"], flat(dfu), n + "ffn_wu")
        dr1, dr1b, dg = _matmul_norm_bwd([(flat(dfg), kw["ffn_wg"][i]), (flat(dfu), kw["ffn_wu"][i])], sv["r1"], kw["norm2_g"][i],
                                         dr2, n + "ffn_dh")
        grads["norm2_g"][i] = dg[0]
        grads["w_out"][i] = _matmul_tn(sv["mix"], dr1b, n + "w_out")
        dmix = _matmul([(dr1b, kw["w_out"][i])], n + "dmix", nt=True)
        do, dz, dlg, dh, ddn_g, dlru_g = _mix_bwd(*sv["mix_args"], dmix, n + "mix")
        grads["dn_norm_g"][i] = ddn_g[0]
        grads["lru_norm_g"][i] = dlru_g[0]
        dlx, dcw, dcb, dwa, dba, dwx, dbx, dlam = _lru_bwd(*sv["lru_args"], sv["hf"], sv["hb"], seq(dh), n + "lru")
        grads["lru_conv_w"][i] = dcw
        grads["lru_conv_b"][i] = dcb[0]
        grads["lru_wa"][i] = _unpair_blocks(dwa)
        grads["lru_wx"][i] = _unpair_blocks(dwx)
        grads["lru_ba"][i], grads["lru_bx"][i], grads["lru_lambda"][i] = dba, dbx, dlam
        ct_f = _dn_rec_bwd(sv["in_f"], sv["st_f"], seq(do), False, n + "dn_rec_fwd")
        ct_b = _dn_rec_bwd(sv["in_b"], sv["st_b"], seq(do), True, n + "dn_rec_rev")
        dqkv_f, dgb_f = _dn_intra_bwd(sv["qkv"], seq(sv["gb"]), ct_f, False, n + "dn_intra_fwd")
        dqkv_b, dgb_b = _dn_intra_bwd(sv["qkv"], seq(sv["gb"]), ct_b, True, n + "dn_intra_rev")
        dgate, dalog, ddtb = _dn_gates_bwd(sv["proj"], kw["alog"][i], kw["dtb"][i], flat(dgb_f), flat(dgb_b), n + "dn_gates")
        grads["dn_a_log"][i] = dalog[0, N_GATE // 2:N_GATE].reshape(2, DN_HEADS)
        grads["dn_dt_bias"][i] = ddtb[0, N_GATE // 2:N_GATE].reshape(2, DN_HEADS)
        grads["dn_a_log_lanes"][i] = dalog[0]
        grads["dn_dt_bias_lanes"][i] = ddtb[0]
        dpqkv, dcw = _dn_prep_bwd(seq(sv["proj"]), kw["dn_conv_w"][i], dqkv_f, dqkv_b, n + "dn_prep")
        grads["dn_conv_w"][i] = dcw
        segs = [flat(dlx), dlg, flat(dpqkv), dz, dgate]
        w_in_i = kw["w_in"][i]
        dq3 = flat(dpqkv)
        wseg = DN_WIDTH
        terms = [(flat(dlx), (w_in_i, wseg, P_LX // wseg)), (dlg, (w_in_i, wseg, P_LG // wseg))]
        terms += [((dq3, wseg, j), (w_in_i, wseg, P_Q // wseg + j)) for j in range(3)]
        terms += [(dz, (w_in_i, wseg, P_Z // wseg)), (dgate, (w_in_i, LANES, P_GATE // LANES))]
        grads["w_in"][i] = _matmul_tn_cols(sv["h"], segs, n + "w_in")
        dr, _, dg = _matmul_norm_bwd(terms, sv["r0"], kw["norm1_g"][i], dr1, n + "in_dh")
        grads["norm1_g"][i] = dg[0]

    grads["final_g"] = dfinal[0]
    return loss[0, 0], dr.reshape(bsz, s, d), grads


MESH = pl.DeviceIdType.MESH
ANY = pl.BlockSpec(memory_space=pl.ANY)
N_CHIPS = 4
HALF = 2


def _place():
    x, y, c = lax.axis_index("x"), lax.axis_index("y"), lax.axis_index("c")
    chips = [(1 - x, y), (x, 1 - y), (1 - x, 1 - y)]
    return x, y, c, chips


def _dma_sems(n):
    return pltpu.SemaphoreType.DMA((n,))


def _gather_weights(shards):
    nt = len(shards)

    def body(*refs):
        w_refs, g_refs = refs[:nt], refs[nt:2 * nt]
        send_sems, recv_sems = refs[2 * nt:]
        x, y, c, chips = _place()
        me = 2 * x + y
        mine = pl.ds(c * HALF, HALF)
        theirs = pl.ds((1 - c) * HALF, HALF)

        def rc(src, dst, k, to):
            return pltpu.make_async_remote_copy(src_ref=src, dst_ref=dst, send_sem=send_sems.at[k], recv_sem=recv_sems.at[k],
                                                device_id=to, device_id_type=MESH)

        first = []
        for t in range(nt):
            for r, (px, py) in enumerate(chips):
                first.append(rc(w_refs[t].at[mine], g_refs[t].at[me, mine], 6 * t + r, (px, py, c)))
        for cp in first:
            cp.start()
        passed = []
        for t in range(nt):
            for r, (px, py) in enumerate(chips):
                peer = 2 * px + py
                rc(w_refs[t].at[mine], g_refs[t].at[peer, mine], 6 * t + r, (px, py, c)).wait_recv()
                fw = rc(g_refs[t].at[peer, mine], g_refs[t].at[peer, mine], 6 * t + 3 + r, (x, y, 1 - c))
                fw.start()
                passed.append(fw)
        for t in range(nt):
            for r, (px, py) in enumerate(chips):
                peer = 2 * px + py
                rc(g_refs[t].at[peer, theirs], g_refs[t].at[peer, theirs], 6 * t + 3 + r, (x, y, 1 - c)).wait_recv()
        for cp in first + passed:
            cp.wait_send()

    return pl.pallas_call(
        body, name="gather_weights", in_specs=[ANY] * nt, out_specs=[ANY] * nt,
        out_shape=[jax.ShapeDtypeStruct((N_CHIPS,) + a.shape, a.dtype) for a in shards],
        scratch_shapes=[_dma_sems(6 * nt), _dma_sems(6 * nt)],
    )(*shards)


def _swap_halves(blocks):
    nt = len(blocks)

    def body(*refs):
        p_refs, l_refs = refs[:nt], refs[nt:2 * nt]
        send_sems, recv_sems = refs[2 * nt:]
        x, y, c, _ = _place()
        theirs = pl.ds((1 - c) * HALF, HALF)
        cps = [pltpu.make_async_remote_copy(src_ref=p_refs[t].at[:, theirs], dst_ref=l_refs[t], send_sem=send_sems.at[t],
                                            recv_sem=recv_sems.at[t], device_id=(x, y, 1 - c), device_id_type=MESH)
               for t in range(nt)]
        for cp in cps:
            cp.start()
        for cp in cps:
            cp.wait_send()
            cp.wait_recv()

    return pl.pallas_call(
        body, name="swap_halves", in_specs=[ANY] * nt, out_specs=[ANY] * nt,
        out_shape=[jax.ShapeDtypeStruct((a.shape[0], HALF) + a.shape[2:], a.dtype) for a in blocks],
        scratch_shapes=[_dma_sems(nt), _dma_sems(nt)],
    )(*blocks)


def _tile_rows(a, row_bytes):
    best = None
    for d in range(16, a + 1, 16):
        if a % d == 0 and d * row_bytes <= 2 * 1024 * 1024:
            best = d
    return best if best is not None else a


def _add_halves(pg, l1, c_arr, name):
    n, _, a, b = pg.shape
    ta = _tile_rows(a, 4 * b)

    def body(c_ref, a_ref, b_ref, o_ref):
        o_ref[...] = (a_ref[...] + b_ref[...]).astype(o_ref.dtype)

    grid_spec = pltpu.PrefetchScalarGridSpec(
        num_scalar_prefetch=1, grid=(n, HALF, a // ta),
        in_specs=[pl.BlockSpec((1, 1, ta, b), lambda k, l, i, c_ref: (k, c_ref[0] * HALF + l, i, 0)),
                  pl.BlockSpec((1, 1, ta, b), lambda k, l, i, c_ref: (k, l, i, 0))],
        out_specs=pl.BlockSpec((1, 1, ta, b), lambda k, l, i, c_ref: (k, l, i, 0)))
    return pl.pallas_call(
        body, name=name, grid_spec=grid_spec, out_shape=jax.ShapeDtypeStruct((n, HALF, a, b), BF16),
        compiler_params=_cparams(("parallel", "parallel", "parallel")),
    )(c_arr, pg, l1)


def _scatter_chips(blocks):
    nt = len(blocks)

    def body(*refs):
        q_refs, l_refs = refs[:nt], refs[nt:2 * nt]
        send_sems, recv_sems = refs[2 * nt:]
        x, y, c, chips = _place()
        me = 2 * x + y

        def rc(t, r, src_chip, dst_chip, to):
            return pltpu.make_async_remote_copy(src_ref=q_refs[t].at[src_chip], dst_ref=l_refs[t].at[dst_chip],
                                                send_sem=send_sems.at[3 * t + r], recv_sem=recv_sems.at[3 * t + r],
                                                device_id=to, device_id_type=MESH)

        sends = [rc(t, r, 2 * px + py, me, (px, py, c)) for t in range(nt) for r, (px, py) in enumerate(chips)]
        for cp in sends:
            cp.start()
        for t in range(nt):
            for r, (px, py) in enumerate(chips):
                rc(t, r, 2 * px + py, 2 * px + py, (px, py, c)).wait_recv()
        for cp in sends:
            cp.wait_send()

    return pl.pallas_call(
        body, name="scatter_chips", in_specs=[ANY] * nt, out_specs=[ANY] * nt,
        out_shape=[jax.ShapeDtypeStruct(a.shape, a.dtype) for a in blocks],
        scratch_shapes=[_dma_sems(3 * nt), _dma_sems(3 * nt)],
    )(*blocks)


def _sum_chips(l2, own, me_arr, name):
    n, _, a, b = l2.shape
    ta = _tile_rows(a, 4 * b)

    def body(me_ref, a_ref, own_ref, o_ref):
        me = me_ref[0]
        acc = jnp.where(me == 0, own_ref[0, 0], a_ref[0, 0]).astype(F32)
        for k in range(1, n):
            acc = acc + jnp.where(me == k, own_ref[0, 0], a_ref[k, 0]).astype(F32)
        o_ref[0] = acc

    grid_spec = pltpu.PrefetchScalarGridSpec(
        num_scalar_prefetch=1, grid=(HALF, a // ta),
        in_specs=[pl.BlockSpec((n, 1, ta, b), lambda l, i, me_ref: (0, l, i, 0)),
                  pl.BlockSpec((1, 1, ta, b), lambda l, i, me_ref: (me_ref[0], l, i, 0))],
        out_specs=pl.BlockSpec((1, ta, b), lambda l, i, me_ref: (l, i, 0)))
    return pl.pallas_call(
        body, name=name, grid_spec=grid_spec, out_shape=jax.ShapeDtypeStruct((HALF, a, b), F32),
        compiler_params=_cparams(("parallel", "parallel")),
    )(me_arr, l2, own)


def _swap_reduced(parts):
    nt = len(parts)

    def body(*refs):
        r_refs, g_refs = refs[:nt], refs[nt:2 * nt]
        send_sems, recv_sems = refs[2 * nt:]
        x, y, c, _ = _place()
        cps = [pltpu.make_async_remote_copy(src_ref=r_refs[t], dst_ref=g_refs[t], send_sem=send_sems.at[t],
                                            recv_sem=recv_sems.at[t], device_id=(x, y, 1 - c), device_id_type=MESH)
               for t in range(nt)]
        for cp in cps:
            cp.start()
        for cp in cps:
            cp.wait_send()
            cp.wait_recv()

    return pl.pallas_call(
        body, name="swap_reduced", in_specs=[ANY] * nt, out_specs=[ANY] * nt,
        out_shape=[jax.ShapeDtypeStruct(a.shape, a.dtype) for a in parts],
        scratch_shapes=[_dma_sems(nt), _dma_sems(nt)],
    )(*parts)


def _adamw_halves(w, mine, theirs, m, v, c_arr, name):
    nl, a, b = w.shape
    ta = _tile_rows(a, 4 * b)

    def body(c_ref, w_ref, g1_ref, g2_ref, m_ref, v_ref, g_ref, d_ref, nm_ref, nv_ref):
        gg = jnp.where(pl.program_id(0) // HALF == c_ref[0], g1_ref[...], g2_ref[...])
        g_ref[...] = gg
        _adamw_math(w_ref[...], gg, m_ref[...], v_ref[...], d_ref, nm_ref, nv_ref)

    full = pl.BlockSpec((1, ta, b), lambda l, i, c_ref: (l, i, 0))
    half = pl.BlockSpec((1, ta, b), lambda l, i, c_ref: (l % HALF, i, 0))
    grid_spec = pltpu.PrefetchScalarGridSpec(num_scalar_prefetch=1, grid=(nl, a // ta),
                                             in_specs=[full, half, half, full, full], out_specs=[full] * 4)
    return pl.pallas_call(
        body, name=name, grid_spec=grid_spec, out_shape=[jax.ShapeDtypeStruct(w.shape, F32)] * 4,
        compiler_params=_cparams(("parallel", "parallel")),
    )(c_arr, w, mine, theirs, m, v)


def _gather_chips(v):
    def body(v_ref, g_ref, send_sems, recv_sems, local_sem):
        x, y, c, chips = _place()
        me = 2 * x + y
        loc = pltpu.make_async_copy(v_ref, g_ref.at[me], local_sem)
        loc.start()
        sends = [pltpu.make_async_remote_copy(src_ref=v_ref, dst_ref=g_ref.at[me], send_sem=send_sems.at[r], recv_sem=recv_sems.at[r],
                                              device_id=(px, py, c), device_id_type=MESH) for r, (px, py) in enumerate(chips)]
        for cp in sends:
            cp.start()
        for r, (px, py) in enumerate(chips):
            pltpu.make_async_remote_copy(src_ref=v_ref, dst_ref=g_ref.at[2 * px + py], send_sem=send_sems.at[r],
                                         recv_sem=recv_sems.at[r], device_id=(px, py, c), device_id_type=MESH).wait_recv()
        for cp in sends:
            cp.wait_send()
        loc.wait()

    return pl.pallas_call(
        body, name="gather_chips", in_specs=[ANY], out_specs=ANY, out_shape=jax.ShapeDtypeStruct((N_CHIPS,) + v.shape, v.dtype),
        scratch_shapes=[_dma_sems(3), _dma_sems(3), pltpu.SemaphoreType.DMA],
    )(v)


BIG = (("w_in", 2), ("w_out", 1), ("ffn_wg", 2), ("ffn_wu", 2), ("ffn_wd", 1), ("ple_wg", 1), ("ple_wp", 2))
SMALL = (("dn_conv_w", 2), ("lru_conv_w", 2), ("lru_ba", 2), ("lru_bx", 2), ("lru_lambda", 2), ("ffn_conv_w", 2))
REPL = ("norm1_g", "dn_a_log", "dn_dt_bias", "dn_norm_g", "lru_conv_b", "lru_wa", "lru_wx", "lru_norm_g", "norm2_g",
        "ffn_conv_b", "ple_norm_g", "ple_bg", "final_g")
WEIGHTS = ("norm1_g", "w_in", "dn_conv_w", "dn_a_log", "dn_dt_bias", "dn_norm_g", "lru_conv_w", "lru_conv_b", "lru_wa", "lru_ba",
           "lru_wx", "lru_bx", "lru_lambda", "lru_norm_g", "w_out", "norm2_g", "ffn_wg", "ffn_wu", "ffn_conv_w", "ffn_conv_b",
           "ffn_wd", "ple_norm_g", "ple_wg", "ple_bg", "ple_wp", "final_g")
GATE_PARAMS = ("dn_a_log", "dn_dt_bias")
ROW_ALIGN = 16


def _rows_for(n_elems):
    rows = -(-n_elems // LANES)
    return -(-rows // ROW_ALIGN) * ROW_ALIGN


def _join_chips(g, own, me, axis):
    return jnp.concatenate([jnp.where(me == k, own, g[k]) for k in range(N_CHIPS)], axis=axis)


def _chip_split(layers, ranges, name):
    nl = len(layers)
    k, n = layers[0].shape
    w = sum(hi - lo for lo, hi in ranges[0])
    tk = _row_tile(k, 128)

    def body(*refs):
        x_refs, o_ref = refs[:nl], refs[nl]
        for l in range(nl):
            for c in range(N_CHIPS):
                off = 0
                for lo, hi in ranges[c]:
                    o_ref[c, l, :, off:off + hi - lo] = x_refs[l][:, lo:hi]
                    off += hi - lo

    return pl.pallas_call(
        body, name=name, grid=(k // tk,), in_specs=[pl.BlockSpec((tk, n), lambda i: (i, 0))] * nl,
        out_specs=pl.BlockSpec((N_CHIPS, nl, tk, w), lambda i: (0, 0, i, 0)),
        out_shape=jax.ShapeDtypeStruct((N_CHIPS, nl, k, w), F32),
        compiler_params=_cparams(("parallel",)),
    )(*layers)


def _even_ranges(n):
    w = n // N_CHIPS
    return [[(c * w, (c + 1) * w)] for c in range(N_CHIPS)]


def _w_in_ranges():
    split = 4 * DN_WIDTH + N_GATE
    total = split + 2 * LRU_WIDTH
    w = total // N_CHIPS
    out = []
    for c in range(N_CHIPS):
        lo, hi = c * w, (c + 1) * w
        parts = []
        if lo < split:
            parts.append((P_Q + lo, P_Q + min(hi, split)))
        if hi > split:
            parts.append((max(lo, split) - split, hi - split))
        out.append(parts)
    return out


def _chip_blocks(layers, axis):
    nl = len(layers)
    size = layers[0].shape[axis - 1] // N_CHIPS
    cut = (lambda g, k: g[:, k * size:(k + 1) * size]) if axis == 2 else (lambda g, k: g[k * size:(k + 1) * size])
    return jnp.stack([jnp.stack([cut(layers[l], k) for l in range(nl)]) for k in range(N_CHIPS)])


def _pack_small(w):
    nl = w[SMALL[0][0]].shape[0]
    flat = jnp.concatenate([w[n].reshape(nl, -1) for n, _ in SMALL], axis=1)
    rows = _rows_for(flat.shape[1])
    return jnp.pad(flat, ((0, 0), (0, rows * LANES - flat.shape[1]))).reshape(nl, rows, LANES)


def _unpack_small_gathered(g, own, me, w):
    nl = g.shape[1]
    flat = jnp.stack([jnp.where(me == k, own, g[k]) for k in range(N_CHIPS)]).reshape(N_CHIPS, nl, -1)
    out, off = {}, 0
    for n, _ in SMALL:
        _, a, b = w[n].shape
        piece = flat[:, :, off:off + a * b].reshape(N_CHIPS, nl, a, b)
        off += a * b
        out[n] = jnp.transpose(piece, (1, 2, 0, 3)).reshape(nl, a, N_CHIPS * b)
    return out


def _pack_small_grads(grads, w):
    nl = w["w_in"].shape[0]
    cols = []
    for n, _ in SMALL:
        _, a, b = w[n].shape
        gfull = jnp.stack(grads[n])
        cols.append(jnp.transpose(gfull.reshape(nl, a, N_CHIPS, b), (2, 0, 1, 3)).reshape(N_CHIPS, nl, a * b))
    small = jnp.concatenate(cols, axis=2)
    rs = _rows_for(small.shape[2])
    small = jnp.pad(small, ((0, 0), (0, 0), (0, rs * LANES - small.shape[2])))
    src = lambda n: n + "_lanes" if n in GATE_PARAMS else n
    rep = jnp.concatenate([(grads[n] if n == "final_g" else jnp.stack(grads[src(n)])).reshape(-1) for n in REPL])
    rr = _rows_for(-(-rep.shape[0] // (N_CHIPS * nl)))
    rep = jnp.pad(rep, (0, N_CHIPS * nl * rr * LANES - rep.shape[0])).reshape(N_CHIPS, nl, rr * LANES)
    return jnp.concatenate([small, rep], axis=2).reshape(N_CHIPS, nl, rs + rr, LANES), (rs, rr)


def _unpack_small_reduced(g, rep_all, rows, w):
    rs, _ = rows
    nl = g.shape[0]
    out = {}
    flat = g[:, :rs].reshape(nl, -1)
    off = 0
    for n, _ in SMALL:
        _, a, b = w[n].shape
        out[n] = flat[:, off:off + a * b].reshape(nl, a, b)
        off += a * b
    flat = rep_all.reshape(-1)
    off = 0
    for n in REPL:
        if n in GATE_PARAMS:
            size = nl * LANES
            out[n] = flat[off:off + size].reshape(nl, LANES)[:, N_GATE // 2:N_GATE].reshape(w[n].shape)
        else:
            size = math.prod(w[n].shape)
            out[n] = flat[off:off + size].reshape(w[n].shape)
        off += size
    return out


def kernel(x, p, norm1_g, w_in, dn_conv_w, dn_a_log, dn_dt_bias, dn_norm_g, lru_conv_w, lru_conv_b, lru_wa, lru_ba, lru_wx, lru_bx, lru_lambda, lru_norm_g, w_out, norm2_g, ffn_wg, ffn_wu, ffn_conv_w, ffn_conv_b, ffn_wd, ple_norm_g, ple_wg, ple_bg, ple_wp, final_g, loss_target, m_norm1_g, m_w_in, m_dn_conv_w, m_dn_a_log, m_dn_dt_bias, m_dn_norm_g, m_lru_conv_w, m_lru_conv_b, m_lru_wa, m_lru_ba, m_lru_wx, m_lru_bx, m_lru_lambda, m_lru_norm_g, m_w_out, m_norm2_g, m_ffn_wg, m_ffn_wu, m_ffn_conv_w, m_ffn_conv_b, m_ffn_wd, m_ple_norm_g, m_ple_wg, m_ple_bg, m_ple_wp, m_final_g, v_norm1_g, v_w_in, v_dn_conv_w, v_dn_a_log, v_dn_dt_bias, v_dn_norm_g, v_lru_conv_w, v_lru_conv_b, v_lru_wa, v_lru_ba, v_lru_wx, v_lru_bx, v_lru_lambda, v_lru_norm_g, v_w_out, v_norm2_g, v_ffn_wg, v_ffn_wu, v_ffn_conv_w, v_ffn_conv_b, v_ffn_wd, v_ple_norm_g, v_ple_wg, v_ple_bg, v_ple_wp, v_final_g):
    w = dict(norm1_g=norm1_g, w_in=w_in, dn_conv_w=dn_conv_w, dn_a_log=dn_a_log, dn_dt_bias=dn_dt_bias, dn_norm_g=dn_norm_g,
             lru_conv_w=lru_conv_w, lru_conv_b=lru_conv_b, lru_wa=lru_wa, lru_ba=lru_ba, lru_wx=lru_wx, lru_bx=lru_bx,
             lru_lambda=lru_lambda, lru_norm_g=lru_norm_g, w_out=w_out, norm2_g=norm2_g, ffn_wg=ffn_wg, ffn_wu=ffn_wu,
             ffn_conv_w=ffn_conv_w, ffn_conv_b=ffn_conv_b, ffn_wd=ffn_wd, ple_norm_g=ple_norm_g, ple_wg=ple_wg, ple_bg=ple_bg,
             ple_wp=ple_wp, final_g=final_g)
    m = dict(norm1_g=m_norm1_g, w_in=m_w_in, dn_conv_w=m_dn_conv_w, dn_a_log=m_dn_a_log, dn_dt_bias=m_dn_dt_bias,
             dn_norm_g=m_dn_norm_g, lru_conv_w=m_lru_conv_w, lru_conv_b=m_lru_conv_b, lru_wa=m_lru_wa, lru_ba=m_lru_ba,
             lru_wx=m_lru_wx, lru_bx=m_lru_bx, lru_lambda=m_lru_lambda, lru_norm_g=m_lru_norm_g, w_out=m_w_out, norm2_g=m_norm2_g,
             ffn_wg=m_ffn_wg, ffn_wu=m_ffn_wu, ffn_conv_w=m_ffn_conv_w, ffn_conv_b=m_ffn_conv_b, ffn_wd=m_ffn_wd,
             ple_norm_g=m_ple_norm_g, ple_wg=m_ple_wg, ple_bg=m_ple_bg, ple_wp=m_ple_wp, final_g=m_final_g)
    v = dict(norm1_g=v_norm1_g, w_in=v_w_in, dn_conv_w=v_dn_conv_w, dn_a_log=v_dn_a_log, dn_dt_bias=v_dn_dt_bias,
             dn_norm_g=v_dn_norm_g, lru_conv_w=v_lru_conv_w, lru_conv_b=v_lru_conv_b, lru_wa=v_lru_wa, lru_ba=v_lru_ba,
             lru_wx=v_lru_wx, lru_bx=v_lru_bx, lru_lambda=v_lru_lambda, lru_norm_g=v_lru_norm_g, w_out=v_w_out, norm2_g=v_norm2_g,
             ffn_wg=v_ffn_wg, ffn_wu=v_ffn_wu, ffn_conv_w=v_ffn_conv_w, ffn_conv_b=v_ffn_conv_b, ffn_wd=v_ffn_wd,
             ple_norm_g=v_ple_norm_g, ple_wg=v_ple_wg, ple_bg=v_ple_bg, ple_wp=v_ple_wp, final_g=v_final_g)

    me = 2 * lax.axis_index("x") + lax.axis_index("y")
    own = [w[n].astype(BF16) for n, _ in BIG] + [_pack_small(w)]
    gathered = _gather_weights(own)
    full = {n: w[n] for n in REPL}
    for (n, axis), g, o in zip(BIG, gathered, own):
        full[n] = _join_chips(g, o, me, axis)
    full.update(_unpack_small_gathered(gathered[-1], own[-1], me, w))

    loss_local, grad_x, grads = _local_step(x, p, loss_target, full)
    loss = lax.psum(loss_local, ("x", "y", "c"))

    small_pack, rows = _pack_small_grads(grads, w)
    blocks = []
    for n, axis in BIG:
        if axis == 2:
            ranges = _w_in_ranges() if n == "w_in" else _even_ranges(grads[n][0].shape[1])
            blocks.append(_chip_split(grads[n], ranges, "chip_split_" + n))
        else:
            blocks.append(_chip_blocks(grads[n], axis))
    blocks.append(small_pack)
    names = [n for n, _ in BIG] + ["small"]
    c_idx = lax.axis_index("c")
    c_arr = c_idx.astype(jnp.int32).reshape(1)
    me_arr = me.astype(jnp.int32).reshape(1)
    from_sibling = _swap_halves(blocks)
    halves = [_add_halves(b4, l1, c_arr, "add_halves_" + n) for n, b4, l1 in zip(names, blocks, from_sibling)]
    arrived = _scatter_chips(halves)
    mine = [_sum_chips(l2, q, me_arr, "sum_chips_" + n) for n, l2, q in zip(names, arrived, halves)]
    theirs = _swap_reduced(mine)

    g, deltas, new_m, new_v = {}, {}, {}, {}
    for i, (n, _) in enumerate(BIG):
        g[n], deltas[n], new_m[n], new_v[n] = _adamw_halves(w[n], mine[i], theirs[i], m[n], v[n], c_arr, "adamw_" + n)
    small = jnp.where(c_idx == 0, jnp.concatenate([mine[-1], theirs[-1]]), jnp.concatenate([theirs[-1], mine[-1]]))
    rep_all = _gather_chips(small[:, rows[0]:])
    gs = _unpack_small_reduced(small, rep_all, rows, w)
    for n in WEIGHTS:
        if n in g:
            continue
        g[n] = gs[n].reshape(w[n].shape)
        shape = (1,) + w[n].shape if w[n].ndim == 1 else w[n].shape
        d2, m2, v2 = _adamw(w[n].reshape(shape), g[n].reshape(shape), m[n].reshape(shape), v[n].reshape(shape), "adamw_" + n)
        deltas[n], new_m[n], new_v[n] = d2.reshape(w[n].shape), m2.reshape(w[n].shape), v2.reshape(w[n].shape)
    return (loss, grad_x, *[g[n] for n in WEIGHTS], *[deltas[n] for n in WEIGHTS], *[new_m[n] for n in WEIGHTS],
            *[new_v[n] for n in WEIGHTS])
```

```python
import functools
import math

import jax
import jax.numpy as jnp
from jax import lax
from jax.experimental import pallas as pl
from jax.experimental.pallas import tpu as pltpu

F32 = jnp.float32
BF16 = jnp.bfloat16

D_MODEL = 1024
DN_HEADS = 4
DN_HEAD_DIM = 128
DN_WIDTH = 512
LRU_WIDTH = 512
LRU_C = 8.0
CHUNK = 64
EPS = 1e-6
P_LX, P_LG, P_Q, P_K, P_V, P_Z, P_GATE, P_COLS = 0, 512, 1024, 1536, 2048, 2560, 3072, 3200
N_GATE = 16
LANES = 128
VMEM_LIMIT = 56 * 1024 * 1024
MM_VMEM_BYTES = 40 * 1024 * 1024

ADAM_LR, ADAM_B1, ADAM_B2, ADAM_EPS, ADAM_WD, ADAM_STEP = 0.001, 0.9, 0.999, 1e-08, 0.01, 10


def _cparams(sem):
    return pltpu.CompilerParams(dimension_semantics=sem, vmem_limit_bytes=VMEM_LIMIT)


def _row_tile(m, want=512):
    for t in range(min(want, m) // 16 * 16, 0, -16):
        if m % t == 0:
            return t
    return m


def _lane_divisors(n):
    out = [d for d in range(n, 0, -LANES) if d % LANES == 0 and n % d == 0] if n % LANES == 0 else []
    return out or [n]


def _mm_tiles(m, n, a_row_bytes, k_total, with_res):
    best = None
    for tm in (1024, 512, 256, 128):
        if m % tm:
            continue
        for tn in _lane_divisors(n):
            need = 2 * (tm * a_row_bytes + k_total * tn * 2 + tm * tn * 4 * (2 if with_res else 1))
            if need <= MM_VMEM_BYTES and (best is None or tm * tn > best[0] * best[1]):
                best = (tm, tn)
    return best if best is not None else (_row_tile(m, 128), _lane_divisors(n)[-1])


def _mm_tn_tiles(m, k, n):
    for tn in _lane_divisors(n):
        for tm in (1024, 512, 256, 128):
            if m % tm == 0 and 2 * (k * tn * 4 + tm * k * 2 + tm * tn * 2) <= MM_VMEM_BYTES:
                return tm, tn
    return _row_tile(m, 128), _lane_divisors(n)[-1]


def _bdot(a, b):
    return jnp.dot(a.astype(BF16), b.astype(BF16), preferred_element_type=F32)


def _bdot_nt(a, b):
    return lax.dot_general(a.astype(BF16), b.astype(BF16), (((1,), (1,)), ((), ())), preferred_element_type=F32)


def _bdot_tn(a, b):
    return lax.dot_general(a.astype(BF16), b.astype(BF16), (((0,), (0,)), ((), ())), preferred_element_type=F32)


def _rms(x, g):
    return x * lax.rsqrt(jnp.mean(x * x, axis=-1, keepdims=True) + EPS) * g


def _gelu(x):
    return 0.5 * x * (1.0 + jnp.tanh(0.7978845608028654 * (x + 0.044715 * x * x * x)))


def _sigmoid(x):
    return 0.5 * (jnp.tanh(0.5 * x) + 1.0)


def _silu(x):
    return x * _sigmoid(x)


def _softplus(x):
    return jnp.maximum(x, 0.0) + jnp.log(1.0 + jnp.exp(-jnp.abs(x)))


def _matmul(terms, name, res=None, nt=False):
    norm = lambda op, axis: op if isinstance(op, tuple) else (op, op.shape[axis], 0)
    a_ops = [norm(a, 1) for a, _ in terms]
    b_ops = [norm(b, 1 if nt else 0) for _, b in terms]
    m = a_ops[0][0].shape[0]
    n = b_ops[0][0].shape[0 if nt else 1]
    a_row_bytes = sum(kw * a.dtype.itemsize for a, kw, _ in a_ops)
    tm, tn = _mm_tiles(m, n, a_row_bytes, sum(kw for _, kw, _ in b_ops), res is not None)
    na = len(terms)
    dot = _bdot_nt if nt else _bdot

    def body(*refs):
        a_refs, b_refs = refs[:na], refs[na:2 * na]
        acc = dot(a_refs[0][...], b_refs[0][...])
        for a_ref, b_ref in zip(a_refs[1:], b_refs[1:]):
            acc = acc + dot(a_ref[...], b_ref[...])
        if res is not None:
            acc = acc + refs[2 * na][...]
        refs[-1][...] = acc

    in_specs = [pl.BlockSpec((tm, kw), functools.partial(lambda i, j, kb: (i, kb), kb=kb)) for _, kw, kb in a_ops]
    if nt:
        in_specs += [pl.BlockSpec((tn, kw), functools.partial(lambda i, j, kb: (j, kb), kb=kb)) for _, kw, kb in b_ops]
    else:
        in_specs += [pl.BlockSpec((kw, tn), functools.partial(lambda i, j, kb: (kb, j), kb=kb)) for _, kw, kb in b_ops]
    args = [a for a, _, _ in a_ops] + [b for b, _, _ in b_ops]
    if res is not None:
        in_specs.append(pl.BlockSpec((tm, tn), lambda i, j: (i, j)))
        args.append(res)
    return pl.pallas_call(
        body, name=name, grid=(m // tm, n // tn), in_specs=in_specs,
        out_specs=pl.BlockSpec((tm, tn), lambda i, j: (i, j)),
        out_shape=jax.ShapeDtypeStruct((m, n), F32),
        compiler_params=_cparams(("parallel", "parallel")),
    )(*args)


def _norm_matmul(x, g, b_list, name):
    m, k = x.shape
    b_bytes = sum(2 * b.shape[0] * b.shape[1] * 2 for b in b_list)
    row_bytes = 2 * (k * 4 + k * 2 + sum(b.shape[1] * 4 for b in b_list))
    tm = next((t for t in (1024, 512, 256, 128) if m % t == 0 and b_bytes + t * row_bytes <= MM_VMEM_BYTES), _row_tile(m, 128))
    nb = len(b_list)

    def body(x_ref, g_ref, *refs):
        b_refs, h_ref, o_refs = refs[:nb], refs[nb], refs[nb + 1:]
        h = _rms(x_ref[...], g_ref[...]).astype(BF16)
        h_ref[...] = h
        for b_ref, o_ref in zip(b_refs, o_refs):
            o_ref[...] = jnp.dot(h, b_ref[...], preferred_element_type=F32)

    row = lambda width: pl.BlockSpec((tm, width), lambda i: (i, 0))
    return pl.pallas_call(
        body, name=name, grid=(m // tm,),
        in_specs=[row(k), pl.BlockSpec((1, k), lambda i: (0, 0))] + [pl.BlockSpec(b.shape, lambda i: (0, 0)) for b in b_list],
        out_specs=[row(k)] + [row(b.shape[1]) for b in b_list],
        out_shape=[jax.ShapeDtypeStruct((m, k), BF16)] + [jax.ShapeDtypeStruct((m, b.shape[1]), F32) for b in b_list],
        compiler_params=_cparams(("parallel",)),
    )(x, g, *b_list)


def _matmul_norm_bwd(terms, x, g, dres, name):
    norm = lambda op: op if isinstance(op, tuple) else (op, op.shape[1], 0)
    a_ops = [norm(a) for a, _ in terms]
    b_ops = [norm(b) for _, b in terms]
    m, n = x.shape
    b_bytes = sum(2 * n * kw * 2 for _, kw, _ in b_ops)
    row_bytes = 2 * (sum(kw * 2 for _, kw, _ in a_ops) + n * (4 + 4 + 4 + 2)) + n * 8
    tm = next((t for t in (1024, 512, 256, 128) if m % t == 0 and b_bytes + t * row_bytes <= MM_VMEM_BYTES), _row_tile(m, 128))
    na = len(terms)

    def body(*refs):
        a_refs, b_refs = refs[:na], refs[na:2 * na]
        x_ref, g_ref, dres_ref, dx_ref, dxb_ref, dg_ref = refs[2 * na:]
        dh = _bdot_nt(a_refs[0][...], b_refs[0][...])
        for a_ref, b_ref in zip(a_refs[1:], b_refs[1:]):
            dh = dh + _bdot_nt(a_ref[...], b_ref[...])
        _, vjp = jax.vjp(_rms, x_ref[...], g_ref[...])
        dx, dg = vjp(dh)
        dx = dx + dres_ref[...]
        dx_ref[...] = dx
        dxb_ref[...] = dx.astype(BF16)

        @pl.when(pl.program_id(0) == 0)
        def _():
            dg_ref[...] = jnp.zeros_like(dg_ref)

        dg_ref[...] += dg

    row = pl.BlockSpec((tm, n), lambda i: (i, 0))
    vec = pl.BlockSpec((1, n), lambda i: (0, 0))
    in_specs = [pl.BlockSpec((tm, kw), functools.partial(lambda i, kb: (i, kb), kb=kb)) for _, kw, kb in a_ops]
    in_specs += [pl.BlockSpec((n, kw), functools.partial(lambda i, kb: (0, kb), kb=kb)) for _, kw, kb in b_ops]
    return pl.pallas_call(
        body, name=name, grid=(m // tm,), in_specs=in_specs + [row, vec, row], out_specs=[row, row, vec],
        out_shape=[jax.ShapeDtypeStruct((m, n), F32), jax.ShapeDtypeStruct((m, n), BF16), jax.ShapeDtypeStruct((1, n), F32)],
        compiler_params=_cparams(("arbitrary",)),
    )(*[a for a, _, _ in a_ops], *[b for b, _, _ in b_ops], x, g, dres)


def _matmul_tn_cols(a, b_list, name):
    m, k = a.shape
    n = sum(b.shape[1] for b in b_list)
    tm = next((t for t in (1024, 512, 256, 128) if m % t == 0 and 2 * (k * n * 4 + t * k * 2 + t * n * 2) <= MM_VMEM_BYTES),
              _row_tile(m, 128))

    def body(a_ref, *refs):
        b_refs, o_ref = refs[:-1], refs[-1]

        @pl.when(pl.program_id(0) == 0)
        def _():
            o_ref[...] = jnp.zeros_like(o_ref)

        off = 0
        for b_ref in b_refs:
            w = b_ref.shape[1]
            o_ref[:, off:off + w] += _bdot_tn(a_ref[...], b_ref[...])
            off += w

    return pl.pallas_call(
        body, name=name, grid=(m // tm,),
        in_specs=[pl.BlockSpec((tm, k), lambda i: (i, 0))] + [pl.BlockSpec((tm, b.shape[1]), lambda i: (i, 0)) for b in b_list],
        out_specs=pl.BlockSpec((k, n), lambda i: (0, 0)), out_shape=jax.ShapeDtypeStruct((k, n), F32),
        compiler_params=_cparams(("arbitrary",)),
    )(a, *b_list)


def _matmul_tn(a, b, name):
    m, k = a.shape
    n = b.shape[1]
    tm, tn = _mm_tn_tiles(m, k, n)

    def body(a_ref, b_ref, o_ref):
        @pl.when(pl.program_id(1) == 0)
        def _():
            o_ref[...] = jnp.zeros_like(o_ref)

        o_ref[...] += _bdot_tn(a_ref[...], b_ref[...])

    return pl.pallas_call(
        body, name=name, grid=(n // tn, m // tm),
        in_specs=[pl.BlockSpec((tm, k), lambda j, i: (i, 0)), pl.BlockSpec((tm, tn), lambda j, i: (i, j))],
        out_specs=pl.BlockSpec((k, tn), lambda j, i: (0, j)),
        out_shape=jax.ShapeDtypeStruct((k, n), F32),
        compiler_params=_cparams(("parallel", "arbitrary")),
    )(a, b)


def _shift_down(x, k):
    row = lax.broadcasted_iota(jnp.int32, x.shape, 0)
    return jnp.where(row >= k, pltpu.roll(x, k, 0), 0.0)


def _shift_up(x, k):
    s = x.shape[0]
    row = lax.broadcasted_iota(jnp.int32, x.shape, 0)
    return jnp.where(row < s - k, pltpu.roll(x, s - k, 0), 0.0)


def _conv_taps(x, ntaps, left):
    out = []
    for j in range(ntaps):
        off = j - left
        out.append(_shift_down(x, -off) if off < 0 else (_shift_up(x, off) if off > 0 else x))
    return out


def _conv_fwd(x, w, left):
    taps = _conv_taps(x, w.shape[0], left)
    acc = taps[0] * w[0:1, :]
    for j in range(1, w.shape[0]):
        acc = acc + taps[j] * w[j:j + 1, :]
    return acc


def _conv_bwd(x, w, left, dout):
    ntaps = w.shape[0]
    dx = None
    for j in range(ntaps):
        off = j - left
        sh = _shift_up(dout, -off) if off < 0 else (_shift_down(dout, off) if off > 0 else dout)
        term = sh * w[j:j + 1, :]
        dx = term if dx is None else dx + term
    taps = _conv_taps(x, ntaps, left)
    dw = jnp.concatenate([jnp.sum(dout * tp, axis=0, keepdims=True) for tp in taps], axis=0)
    return dx, dw


SCAN_BLOCK = 32


def _scan(a, b, reverse):
    s = a.shape[0]
    blk = SCAN_BLOCK if s % SCAN_BLOCK == 0 else s
    row = lax.broadcasted_iota(jnp.int32, a.shape, 0)
    pos = row & (blk - 1) if blk & (blk - 1) == 0 else row % blk
    d = 1
    while d < blk:
        if reverse:
            keep = pos < blk - d
            sb, sa = pltpu.roll(b, s - d, 0), pltpu.roll(a, s - d, 0)
        else:
            keep = pos >= d
            sb, sa = pltpu.roll(b, d, 0), pltpu.roll(a, d, 0)
        b = a * jnp.where(keep, sb, 0.0) + b
        a = a * jnp.where(keep, sa, 1.0)
        d *= 2
    nblk = s // blk
    if nblk == 1:
        return b
    carry = jnp.zeros((1, a.shape[1]), F32)
    carries = [None] * nblk
    order = range(nblk - 1, -1, -1) if reverse else range(nblk)
    for j in order:
        carries[j] = carry
        last = j * blk if reverse else (j + 1) * blk - 1
        carry = b[last:last + 1, :] + a[last:last + 1, :] * carry
    carry_in = jnp.concatenate([jnp.broadcast_to(cj, (blk, a.shape[1])) for cj in carries], axis=0)
    return b + a * carry_in


def _dn_gates_fn(pre, alog, dtb):
    lane = lax.broadcasted_iota(jnp.int32, pre.shape, 1)
    beta = _sigmoid(pre)
    g = -jnp.exp(alog) * _softplus(pre + dtb)
    return jnp.where(lane < N_GATE // 2, beta, jnp.where(lane < N_GATE, g, 0.0))


def _dn_gates_fwd(proj, alog, dtb, name):
    t = proj.shape[0]
    tm = _row_tile(t)
    cb = P_GATE // LANES

    def body(p_ref, a_ref, d_ref, o_ref):
        o_ref[...] = _dn_gates_fn(p_ref[...], a_ref[...], d_ref[...])

    return pl.pallas_call(
        body, name=name, grid=(t // tm,),
        in_specs=[pl.BlockSpec((tm, LANES), lambda i: (i, cb)), pl.BlockSpec((1, LANES), lambda i: (0, 0)),
                  pl.BlockSpec((1, LANES), lambda i: (0, 0))],
        out_specs=pl.BlockSpec((tm, LANES), lambda i: (i, 0)),
        out_shape=jax.ShapeDtypeStruct((t, LANES), F32),
        compiler_params=_cparams(("parallel",)),
    )(proj, alog, dtb)


def _dn_gates_bwd(proj, alog, dtb, dgb_f, dgb_b, name):
    t = proj.shape[0]
    tm = _row_tile(t)
    cb = P_GATE // LANES

    def body(p_ref, a_ref, d_ref, g1_ref, g2_ref, dp_ref, da_ref, dd_ref):
        _, vjp = jax.vjp(_dn_gates_fn, p_ref[...], a_ref[...], d_ref[...])
        dp, da, dd = vjp(g1_ref[...] + g2_ref[...])
        dp_ref[...] = dp.astype(dp_ref.dtype)

        @pl.when(pl.program_id(0) == 0)
        def _():
            da_ref[...] = jnp.zeros_like(da_ref)
            dd_ref[...] = jnp.zeros_like(dd_ref)

        da_ref[...] += da
        dd_ref[...] += dd

    row = pl.BlockSpec((tm, LANES), lambda i: (i, 0))
    vec = pl.BlockSpec((1, LANES), lambda i: (0, 0))
    return pl.pallas_call(
        body, name=name, grid=(t // tm,),
        in_specs=[pl.BlockSpec((tm, LANES), lambda i: (i, cb)), vec, vec, row, row],
        out_specs=[row, vec, vec],
        out_shape=[jax.ShapeDtypeStruct((t, LANES), BF16), jax.ShapeDtypeStruct((1, LANES), F32),
                   jax.ShapeDtypeStruct((1, LANES), F32)],
        compiler_params=_cparams(("arbitrary",)),
    )(proj, alog, dtb, dgb_f, dgb_b)


def _dn_prep_fn(x, w, is_qk):
    act = _silu(_conv_fwd(x, w, 2))
    nrm = act * lax.rsqrt(jnp.sum(act * act, axis=-1, keepdims=True) + EPS)
    return jnp.where(is_qk, nrm, act)


def _dn_prep_fwd(proj3, conv_w, name):
    bsz, s, _ = proj3.shape
    nblk = 3 * DN_WIDTH // LANES
    cb = P_Q // LANES

    def body(x_ref, w_ref, o_ref):
        o_ref[0] = _dn_prep_fn(x_ref[0], w_ref[...], pl.program_id(1) < 2 * DN_HEADS)

    return pl.pallas_call(
        body, name=name, grid=(bsz, nblk),
        in_specs=[pl.BlockSpec((1, s, LANES), lambda b, j: (b, 0, cb + j)), pl.BlockSpec((4, LANES), lambda b, j: (0, j))],
        out_specs=pl.BlockSpec((1, s, LANES), lambda b, j: (b, 0, j)),
        out_shape=jax.ShapeDtypeStruct((bsz, s, 3 * DN_WIDTH), F32),
        compiler_params=_cparams(("parallel", "parallel")),
    )(proj3, conv_w)


def _dn_prep_bwd(proj3, conv_w, dqkv_f, dqkv_b, name):
    bsz, s, _ = proj3.shape
    nblk = 3 * DN_WIDTH // LANES
    cb = P_Q // LANES

    def body(x_ref, w_ref, d1_ref, d2_ref, dx_ref, dw_ref):
        x, w, d = x_ref[0], w_ref[...], d1_ref[0] + d2_ref[0]
        is_qk = pl.program_id(0) < 2 * DN_HEADS
        pre = _conv_fwd(x, w, 2)

        def post(pre):
            act = _silu(pre)
            nrm = act * lax.rsqrt(jnp.sum(act * act, axis=-1, keepdims=True) + EPS)
            return jnp.where(is_qk, nrm, act)

        _, vjp = jax.vjp(post, pre)
        (dpre,) = vjp(d)
        dx, dw = _conv_bwd(x, w, 2, dpre)
        dx_ref[0] = dx.astype(dx_ref.dtype)

        @pl.when(pl.program_id(1) == 0)
        def _():
            dw_ref[...] = jnp.zeros_like(dw_ref)

        dw_ref[...] += dw

    col = pl.BlockSpec((1, s, LANES), lambda j, b: (b, 0, j))
    return pl.pallas_call(
        body, name=name, grid=(nblk, bsz),
        in_specs=[pl.BlockSpec((1, s, LANES), lambda j, b: (b, 0, cb + j)), pl.BlockSpec((4, LANES), lambda j, b: (0, j)), col, col],
        out_specs=[col, pl.BlockSpec((4, LANES), lambda j, b: (0, j))],
        out_shape=[jax.ShapeDtypeStruct((bsz, s, 3 * DN_WIDTH), BF16), jax.ShapeDtypeStruct((4, 3 * DN_WIDTH), F32)],
        compiler_params=_cparams(("parallel", "arbitrary")),
    )(proj3, conv_w, dqkv_f, dqkv_b)


def _parts(x, n):
    out = []
    for _ in range(n):
        bits = lax.bitcast_convert_type(x, jnp.uint32) & jnp.uint32(0xFFFF0000)
        t = lax.bitcast_convert_type(bits, F32)
        out.append(t.astype(BF16))
        x = x - t
    return out


def _dg(x, y, cx, cy):
    return lax.dot_general(x, y, (((cx + 1,), (cy + 1,)), ((0,), (0,))), preferred_element_type=F32)


def _bmm(a, b):
    return _dg(a.astype(BF16), b.astype(BF16), 1, 0)


def _bmm_nt(a, b):
    return _dg(a.astype(BF16), b.astype(BF16), 1, 1)


def _bmm_tn(a, b):
    return _dg(a.astype(BF16), b.astype(BF16), 0, 0)


def _dot3_raw(a, b, ca, cb):
    a_hi, a_lo = _parts(a, 2)
    b_hi, b_lo = _parts(b, 2)
    return _dg(a_hi, b_hi, ca, cb) + (_dg(a_hi, b_lo, ca, cb) + _dg(a_lo, b_hi, ca, cb))


@jax.custom_vjp
def _sum_left(m, x):
    return sum(_dg(m, pt, 1, 0) for pt in _parts(x, 3))


def _sum_left_fwd(m, x):
    return _sum_left(m, x), m


def _sum_left_bwd(m, ct):
    return jnp.zeros_like(m), sum(_dg(m, pt, 0, 0) for pt in _parts(ct, 2))


_sum_left.defvjp(_sum_left_fwd, _sum_left_bwd)


@jax.custom_vjp
def _sum_right(x, m):
    return sum(_dg(pt, m, 0, 0) for pt in _parts(x, 3))


def _sum_right_fwd(x, m):
    return _sum_right(x, m), m


def _sum_right_bwd(m, ct):
    return sum(_dg(m, pt, 1, 1) for pt in _parts(ct, 2)), jnp.zeros_like(m)


_sum_right.defvjp(_sum_right_fwd, _sum_right_bwd)


def _unit_tri_inverse_raw(a):
    nu, c, _ = a.shape
    row = lax.broadcasted_iota(jnp.int32, (nu, c, c), 1)
    col = lax.broadcasted_iota(jnp.int32, (nu, c, c), 2)
    same = lambda n: (row // n) == (col // n)
    d = jnp.where(same(8), a, 0.0)
    tinv = jnp.where(row == col, 1.0, 0.0) - d
    pw = _dot3_raw(d, d, 1, 0)
    tinv = tinv + _dot3_raw(tinv, pw, 1, 0)
    pw = _dot3_raw(pw, pw, 1, 0)
    tinv = tinv + _dot3_raw(tinv, pw, 1, 0)
    n = 8
    while n < c:
        e = jnp.where(same(2 * n) & jnp.logical_not(same(n)), a, 0.0)
        tinv = tinv - _bmm(tinv, _bmm(e, tinv))
        n *= 2
    return tinv


@jax.custom_vjp
def _unit_tri_inverse(a):
    return _unit_tri_inverse_raw(a)


def _unit_tri_inverse_fwd(a):
    tinv = _unit_tri_inverse_raw(a)
    return tinv, tinv


def _unit_tri_inverse_bwd(tinv, ct):
    return (-_bmm_nt(_bmm_tn(tinv, ct), tinv),)


_unit_tri_inverse.defvjp(_unit_tri_inverse_fwd, _unit_tri_inverse_bwd)


def _dn_intra(q, k, v, g, beta, rev):
    nu, c, _ = q.shape
    row = lax.broadcasted_iota(jnp.int32, (nu, c, c), 1)
    col = lax.broadcasted_iota(jnp.int32, (nu, c, c), 2)
    incl = (row <= col) if rev else (row >= col)
    strict = (row < col) if rev else (row > col)
    ones_incl = jnp.where(incl, 1.0, 0.0).astype(BF16)
    ones_tr = jnp.where((row >= col) if rev else (row <= col), 1.0, 0.0).astype(BF16)
    gbc = jnp.broadcast_to(g, (nu, c, c))
    gc = _sum_left(ones_incl, gbc)
    gr = _sum_right(gbc, ones_tr)
    gcum = gc[:, :, 0:1]
    decay = jnp.where(incl, jnp.exp(jnp.where(incl, gc - gr, 0.0)), 0.0)
    qs = q * (DN_HEAD_DIM ** -0.5)
    kb = k * beta
    a = jnp.where(strict, _bmm_nt(kb, k) * decay, 0.0)
    tinv = _unit_tri_inverse(a)
    egc = jnp.exp(gcum)
    u = _bmm(tinv, v * beta)
    w = _bmm(tinv, kb * egc)
    attn = _bmm_nt(qs, k) * decay
    glast = gcum[:, 0:1, :] if rev else gcum[:, c - 1:c, :]
    q_dec = qs * egc
    k_dec = k * jnp.exp(glast - gcum)
    cdec = jnp.broadcast_to(jnp.exp(glast), (nu, 1, LANES))
    return u, w, q_dec, k_dec, attn, cdec


def _dn_rec(u, w, q_dec, k_dec, attn, cdec, state):
    v_new = u - _bmm(w, state)
    o = _bmm(q_dec, state) + _bmm(attn, v_new)
    return o, state * cdec + _bmm_tn(k_dec, v_new)


DN_INTRA_TOKENS = 512
DN_REC_TOKENS = 512
DN_REC_BWD_TOKENS = 256
DN_REC_EXAMPLES = 4


def _dn_gate_lanes(rev, h):
    lb = (DN_HEADS if rev else 0) + h
    return lb, N_GATE // 2 + lb


def _dn_intra_shapes(bsz, s):
    n = s // CHUNK
    return [jax.ShapeDtypeStruct((bsz, s, DN_WIDTH), F32), jax.ShapeDtypeStruct((bsz, s, DN_WIDTH), BF16),
            jax.ShapeDtypeStruct((bsz, s, DN_WIDTH), BF16), jax.ShapeDtypeStruct((bsz, s, DN_WIDTH), BF16),
            jax.ShapeDtypeStruct((bsz, n, DN_HEADS, CHUNK, CHUNK), BF16), jax.ShapeDtypeStruct((bsz, n, DN_HEADS, 1, LANES), F32)]


def _dn_intra_specs(tb, ix):
    nc = tb // CHUNK
    ix5 = lambda b, j: ix(b, j) + (0, 0)
    row = pl.BlockSpec((1, tb, DN_WIDTH), ix)
    return [row, row, row, row, pl.BlockSpec((1, nc, DN_HEADS, CHUNK, CHUNK), ix5), pl.BlockSpec((1, nc, DN_HEADS, 1, LANES), ix5)]


def _dn_units(nc):
    return [(ci, h) for ci in range(nc) for h in range(DN_HEADS)]


def _dn_load_units(qkv_ref, gb_ref, nc, rev):
    qs, ks, vs, gs, bs = [], [], [], [], []
    for ci, h in _dn_units(nc):
        rows = slice(ci * CHUNK, (ci + 1) * CHUNK)
        lb, lg = _dn_gate_lanes(rev, h)
        qs.append(qkv_ref[0, rows, h * LANES:(h + 1) * LANES])
        ks.append(qkv_ref[0, rows, DN_WIDTH + h * LANES:DN_WIDTH + (h + 1) * LANES])
        vs.append(qkv_ref[0, rows, 2 * DN_WIDTH + h * LANES:2 * DN_WIDTH + (h + 1) * LANES])
        gs.append(gb_ref[0, rows, lg:lg + 1])
        bs.append(gb_ref[0, rows, lb:lb + 1])
    return jnp.stack(qs), jnp.stack(ks), jnp.stack(vs), jnp.stack(gs), jnp.stack(bs)


def _dn_intra_fwd(qkv, gb, rev, name):
    bsz, s, _ = qkv.shape
    tb = min(DN_INTRA_TOKENS, s)
    nc = tb // CHUNK

    def body(qkv_ref, gb_ref, u_ref, w_ref, qd_ref, kd_ref, at_ref, cd_ref):
        q, k, v, g, beta = _dn_load_units(qkv_ref, gb_ref, nc, rev)
        u, w, qd, kd, at, cd = _dn_intra(q, k, v, g, beta, rev)
        for i, (ci, h) in enumerate(_dn_units(nc)):
            rows = slice(ci * CHUNK, (ci + 1) * CHUNK)
            cols = slice(h * LANES, (h + 1) * LANES)
            u_ref[0, rows, cols] = u[i]
            w_ref[0, rows, cols] = w[i].astype(BF16)
            qd_ref[0, rows, cols] = qd[i].astype(BF16)
            kd_ref[0, rows, cols] = kd[i].astype(BF16)
            at_ref[0, ci, h] = at[i].astype(BF16)
            cd_ref[0, ci, h] = cd[i]

    ix = lambda b, j: (b, j, 0)
    return pl.pallas_call(
        body, name=name, grid=(bsz, s // tb),
        in_specs=[pl.BlockSpec((1, tb, 3 * DN_WIDTH), ix), pl.BlockSpec((1, tb, LANES), ix)],
        out_specs=_dn_intra_specs(tb, ix), out_shape=_dn_intra_shapes(bsz, s),
        compiler_params=_cparams(("parallel", "parallel")),
    )(qkv, gb)


def _dn_intra_bwd(qkv, gb, cts, rev, name):
    bsz, s, _ = qkv.shape
    tb = min(DN_INTRA_TOKENS, s)
    nc = tb // CHUNK

    def body(qkv_ref, gb_ref, du_ref, dw_ref, dqd_ref, dkd_ref, dat_ref, dcd_ref, dqkv_ref, dgb_ref):
        units = _dn_units(nc)
        q, k, v, g, beta = _dn_load_units(qkv_ref, gb_ref, nc, rev)
        _, vjp = jax.vjp(functools.partial(_dn_intra, rev=rev), q, k, v, g, beta)
        tok = lambda ref: jnp.stack([ref[0, ci * CHUNK:(ci + 1) * CHUNK, h * LANES:(h + 1) * LANES] for ci, h in units])
        per = lambda ref: jnp.stack([ref[0, ci, h] for ci, h in units])
        dq, dk, dv, dg, dbeta = vjp((tok(du_ref), tok(dw_ref), tok(dqd_ref), tok(dkd_ref), per(dat_ref), per(dcd_ref)))
        lane = lax.broadcasted_iota(jnp.int32, (CHUNK, LANES), 1)
        for ci in range(nc):
            rows = slice(ci * CHUNK, (ci + 1) * CHUNK)
            dgates = jnp.zeros((CHUNK, LANES), F32)
            for h in range(DN_HEADS):
                i = units.index((ci, h))
                lb, lg = _dn_gate_lanes(rev, h)
                dqkv_ref[0, rows, h * LANES:(h + 1) * LANES] = dq[i]
                dqkv_ref[0, rows, DN_WIDTH + h * LANES:DN_WIDTH + (h + 1) * LANES] = dk[i]
                dqkv_ref[0, rows, 2 * DN_WIDTH + h * LANES:2 * DN_WIDTH + (h + 1) * LANES] = dv[i]
                dgates = dgates + jnp.where(lane == lb, dbeta[i], 0.0) + jnp.where(lane == lg, dg[i], 0.0)
            dgb_ref[0, rows, :] = dgates

    ix = lambda b, j: (b, j, 0)
    ix5 = lambda b, j: (b, j, 0, 0, 0)
    row = pl.BlockSpec((1, tb, DN_WIDTH), ix)
    return pl.pallas_call(
        body, name=name, grid=(bsz, s // tb),
        in_specs=[pl.BlockSpec((1, tb, 3 * DN_WIDTH), ix), pl.BlockSpec((1, tb, LANES), ix), row, row, row, row,
                  pl.BlockSpec((1, nc, DN_HEADS, CHUNK, CHUNK), ix5), pl.BlockSpec((1, nc, DN_HEADS, 1, LANES), ix5)],
        out_specs=[pl.BlockSpec((1, tb, 3 * DN_WIDTH), ix), pl.BlockSpec((1, tb, LANES), ix)],
        out_shape=[jax.ShapeDtypeStruct((bsz, s, 3 * DN_WIDTH), F32), jax.ShapeDtypeStruct((bsz, s, LANES), F32)],
        compiler_params=_cparams(("parallel", "parallel")),
    )(qkv, gb, *cts)


def _dn_rec_examples(bsz):
    return max(n for n in range(1, DN_REC_EXAMPLES + 1) if bsz % n == 0)


def _dn_rec_specs(nb, tb, ix):
    nc = tb // CHUNK
    ix5 = lambda b, j: ix(b, j) + (0, 0)
    row = pl.BlockSpec((nb, tb, DN_WIDTH), ix)
    return [row, row, row, row, pl.BlockSpec((nb, nc, DN_HEADS, CHUNK, CHUNK), ix5), pl.BlockSpec((nb, nc, DN_HEADS, 1, LANES), ix5)]


def _dn_rec_fwd(intra, rev, name):
    u = intra[0]
    bsz, s, _ = u.shape
    nb = _dn_rec_examples(bsz)
    tb = min(DN_REC_TOKENS, s)
    nt = s // tb
    nc = tb // CHUNK
    nh = nb * DN_HEADS

    def body(u_ref, w_ref, qd_ref, kd_ref, at_ref, cd_ref, o_ref, st_ref, state):
        @pl.when(pl.program_id(1) == 0)
        def _():
            state[...] = jnp.zeros_like(state)

        def step(ci, carry):
            cidx = (nc - 1 - ci) if rev else ci
            rows = pl.ds(pl.multiple_of(cidx * CHUNK, CHUNK), CHUNK)
            heads = lambda ref: jnp.stack([ref[e, rows, h * LANES:(h + 1) * LANES] for e in range(nb) for h in range(DN_HEADS)])
            per = lambda ref: jnp.concatenate([ref[e, cidx] for e in range(nb)])
            st = state[...]
            o, new_state = _dn_rec(heads(u_ref), heads(w_ref), heads(qd_ref), heads(kd_ref), per(at_ref), per(cd_ref), st)
            state[...] = new_state
            for e in range(nb):
                st_ref[e, cidx] = st[e * DN_HEADS:(e + 1) * DN_HEADS]
                o_ref[e, rows, :] = jnp.concatenate([o[e * DN_HEADS + h] for h in range(DN_HEADS)], axis=-1)
            return carry

        lax.fori_loop(0, nc, step, 0)

    ix = (lambda b, j: (b, nt - 1 - j, 0)) if rev else (lambda b, j: (b, j, 0))
    ix5 = lambda b, j: ix(b, j) + (0, 0)
    return pl.pallas_call(
        body, name=name, grid=(bsz // nb, nt), in_specs=_dn_rec_specs(nb, tb, ix),
        out_specs=[pl.BlockSpec((nb, tb, DN_WIDTH), ix), pl.BlockSpec((nb, nc, DN_HEADS, DN_HEAD_DIM, DN_HEAD_DIM), ix5)],
        out_shape=[jax.ShapeDtypeStruct((bsz, s, DN_WIDTH), F32),
                   jax.ShapeDtypeStruct((bsz, s // CHUNK, DN_HEADS, DN_HEAD_DIM, DN_HEAD_DIM), F32)],
        scratch_shapes=[pltpu.VMEM((nh, DN_HEAD_DIM, DN_HEAD_DIM), F32)],
        compiler_params=_cparams(("parallel", "arbitrary")),
    )(*intra)


def _dn_rec_bwd(intra, states, do, rev, name):
    u = intra[0]
    bsz, s, _ = u.shape
    nb = _dn_rec_examples(bsz)
    tb = min(DN_REC_BWD_TOKENS, s)
    nt = s // tb
    nc = tb // CHUNK
    nh = nb * DN_HEADS

    def body(u_ref, w_ref, qd_ref, kd_ref, at_ref, cd_ref, st_ref, do_ref,
             du_ref, dw_ref, dqd_ref, dkd_ref, dat_ref, dcd_ref, dstate):
        @pl.when(pl.program_id(1) == 0)
        def _():
            dstate[...] = jnp.zeros_like(dstate)

        def step(ci, carry):
            cidx = ci if rev else (nc - 1 - ci)
            rows = pl.ds(pl.multiple_of(cidx * CHUNK, CHUNK), CHUNK)
            heads = lambda ref: jnp.stack([ref[e, rows, h * LANES:(h + 1) * LANES] for e in range(nb) for h in range(DN_HEADS)])
            per = lambda ref: jnp.concatenate([ref[e, cidx] for e in range(nb)])
            args = (heads(u_ref), heads(w_ref).astype(F32), heads(qd_ref).astype(F32), heads(kd_ref).astype(F32),
                    per(at_ref).astype(F32), per(cd_ref), per(st_ref))
            _, vjp = jax.vjp(_dn_rec, *args)
            du, dw, dqd, dkd, dat, dcd, dst = vjp((heads(do_ref), dstate[...]))
            dstate[...] = dst
            for e in range(nb):
                hs = slice(e * DN_HEADS, (e + 1) * DN_HEADS)
                dat_ref[e, cidx] = dat[hs]
                dcd_ref[e, cidx] = dcd[hs]
                for ref, val in ((du_ref, du), (dw_ref, dw), (dqd_ref, dqd), (dkd_ref, dkd)):
                    ref[e, rows, :] = jnp.concatenate([val[e * DN_HEADS + h] for h in range(DN_HEADS)], axis=-1)
            return carry

        lax.fori_loop(0, nc, step, 0)

    ix = (lambda b, j: (b, j, 0)) if rev else (lambda b, j: (b, nt - 1 - j, 0))
    ix5 = lambda b, j: ix(b, j) + (0, 0)
    row = pl.BlockSpec((nb, tb, DN_WIDTH), ix)
    f32 = lambda sd: jax.ShapeDtypeStruct(sd.shape, F32)
    return pl.pallas_call(
        body, name=name, grid=(bsz // nb, nt),
        in_specs=_dn_rec_specs(nb, tb, ix) + [pl.BlockSpec((nb, nc, DN_HEADS, DN_HEAD_DIM, DN_HEAD_DIM), ix5), row],
        out_specs=_dn_rec_specs(nb, tb, ix), out_shape=[f32(sd) for sd in _dn_intra_shapes(bsz, s)],
        scratch_shapes=[pltpu.VMEM((nh, DN_HEAD_DIM, DN_HEAD_DIM), F32)],
        compiler_params=_cparams(("parallel", "arbitrary")),
    )(*intra, states, do)


def _lru_gate_fn(xc, wa, ba, wx, bx, lam):
    r = _sigmoid(_bdot(xc, wa) + ba)
    ig = _sigmoid(_bdot(xc, wx) + bx)
    log_a = -LRU_C * r * _softplus(-lam)
    a = jnp.exp(log_a)
    b = jnp.sqrt(-jnp.tanh(log_a) * (a * a + 1.0)) * (ig * xc)
    return a, b


def _lru_fwd(proj3, conv_w, conv_b, wa, ba, wx, bx, lam, name):
    bsz, s, _ = proj3.shape
    nblk = LRU_WIDTH // LANES

    def body(x_ref, cw_ref, cb_ref, wa_ref, ba_ref, wx_ref, bx_ref, lam_ref, hf_ref, hb_ref):
        xc = _conv_fwd(x_ref[0], cw_ref[...], 2) + cb_ref[...]
        for d, h_ref in ((0, hf_ref), (1, hb_ref)):
            a, b = _lru_gate_fn(xc, wa_ref[d, 0], ba_ref[d:d + 1, :], wx_ref[d, 0], bx_ref[d:d + 1, :], lam_ref[d:d + 1, :])
            h_ref[0] = _scan(a, b, reverse=(d == 1))

    col = pl.BlockSpec((1, s, LANES), lambda b, j: (b, 0, j))
    vec2 = pl.BlockSpec((2, LANES), lambda b, j: (0, j))
    wspec = pl.BlockSpec((2, 1, LANES, LANES), lambda b, j: (0, j, 0, 0))
    return pl.pallas_call(
        body, name=name, grid=(bsz, nblk),
        in_specs=[col, pl.BlockSpec((4, LANES), lambda b, j: (0, j)), pl.BlockSpec((1, LANES), lambda b, j: (0, j)),
                  wspec, vec2, wspec, vec2, vec2],
        out_specs=[col, col],
        out_shape=[jax.ShapeDtypeStruct((bsz, s, LRU_WIDTH), F32)] * 2,
        compiler_params=_cparams(("parallel", "parallel")),
    )(proj3, conv_w, conv_b, wa, ba, wx, bx, lam)


def _lru_bwd(proj3, conv_w, conv_b, wa, ba, wx, bx, lam, hf, hb, dh, name):
    bsz, s, _ = proj3.shape
    nblk = LRU_WIDTH // LANES

    def body(x_ref, cw_ref, cb_ref, wa_ref, ba_ref, wx_ref, bx_ref, lam_ref, hf_ref, hb_ref, dh_ref,
             dx_ref, dcw_ref, dcb_ref, dwa_ref, dba_ref, dwx_ref, dbx_ref, dlam_ref):
        @pl.when(pl.program_id(1) == 0)
        def _():
            for r in (dcw_ref, dcb_ref, dwa_ref, dba_ref, dwx_ref, dbx_ref, dlam_ref):
                r[...] = jnp.zeros_like(r)

        x, cw = x_ref[0], cw_ref[...]
        xc = _conv_fwd(x, cw, 2) + cb_ref[...]
        dhv = dh_ref[0]
        dxc = jnp.zeros_like(xc)
        for d, h_ref in ((0, hf_ref), (1, hb_ref)):
            rev = d == 1
            args = (xc, wa_ref[d, 0].astype(F32), ba_ref[d:d + 1, :], wx_ref[d, 0].astype(F32), bx_ref[d:d + 1, :],
                    lam_ref[d:d + 1, :])
            (a, _), vjp = jax.vjp(_lru_gate_fn, *args)
            h = h_ref[0]
            a_next = _shift_down(a, 1) if rev else _shift_up(a, 1)
            lam_adj = _scan(a_next, dhv, reverse=not rev)
            h_prev = _shift_up(h, 1) if rev else _shift_down(h, 1)
            dxc_d, dwa, dba, dwx, dbx, dlam = vjp((lam_adj * h_prev, lam_adj))
            dxc = dxc + dxc_d
            dwa_ref[d, 0] += dwa
            dwx_ref[d, 0] += dwx
            dba_ref[d:d + 1, :] += dba
            dbx_ref[d:d + 1, :] += dbx
            dlam_ref[d:d + 1, :] += dlam
        dx, dcw = _conv_bwd(x, cw, 2, dxc)
        dx_ref[0] = dx.astype(dx_ref.dtype)
        dcw_ref[...] += dcw
        dcb_ref[...] += jnp.sum(dxc, axis=0, keepdims=True)

    col = pl.BlockSpec((1, s, LANES), lambda j, b: (b, 0, j))
    vec1 = pl.BlockSpec((1, LANES), lambda j, b: (0, j))
    vec2 = pl.BlockSpec((2, LANES), lambda j, b: (0, j))
    vec4 = pl.BlockSpec((4, LANES), lambda j, b: (0, j))
    wspec = pl.BlockSpec((2, 1, LANES, LANES), lambda j, b: (0, j, 0, 0))
    wshape = jax.ShapeDtypeStruct((2, nblk, LANES, LANES), F32)
    v2shape = jax.ShapeDtypeStruct((2, LRU_WIDTH), F32)
    return pl.pallas_call(
        body, name=name, grid=(nblk, bsz),
        in_specs=[col, vec4, vec1, wspec, vec2, wspec, vec2, vec2, col, col, col],
        out_specs=[col, vec4, vec1, wspec, vec2, wspec, vec2, vec2],
        out_shape=[jax.ShapeDtypeStruct((bsz, s, LRU_WIDTH), BF16), jax.ShapeDtypeStruct((4, LRU_WIDTH), F32),
                   jax.ShapeDtypeStruct((1, LRU_WIDTH), F32), wshape, v2shape, wshape, v2shape, v2shape],
        compiler_params=_cparams(("parallel", "arbitrary")),
    )(proj3, conv_w, conv_b, wa, ba, wx, bx, lam, hf, hb, dh)


def _mix_fn(o_f, o_b, z, lg, hf, hb, dn_g, lru_g):
    osum = o_f + o_b
    heads = []
    for h in range(DN_HEADS):
        sl = slice(h * LANES, (h + 1) * LANES)
        heads.append(_rms(osum[:, sl], dn_g) * _silu(z[:, sl]))
    lru = _rms(_gelu(lg) * (hf + hb), lru_g)
    return jnp.concatenate(heads + [lru], axis=-1)


def _mix_specs(tm):
    w = DN_WIDTH
    row = pl.BlockSpec((tm, w), lambda i: (i, 0))
    z = pl.BlockSpec((tm, w), lambda i: (i, P_Z // w))
    lg = pl.BlockSpec((tm, w), lambda i: (i, P_LG // w))
    return [row, row, z, lg, row, row, pl.BlockSpec((1, LANES), lambda i: (0, 0)), pl.BlockSpec((1, w), lambda i: (0, 0))]


def _mix_fwd(o_f, o_b, proj, hf, hb, dn_g, lru_g, name):
    t = proj.shape[0]
    tm = _row_tile(t)

    def body(of_ref, ob_ref, z_ref, lg_ref, hf_ref, hb_ref, dg_ref, lgn_ref, o_ref):
        o_ref[...] = _mix_fn(of_ref[...], ob_ref[...], z_ref[...], lg_ref[...], hf_ref[...], hb_ref[...],
                             dg_ref[...], lgn_ref[...]).astype(o_ref.dtype)

    return pl.pallas_call(
        body, name=name, grid=(t // tm,), in_specs=_mix_specs(tm),
        out_specs=pl.BlockSpec((tm, D_MODEL), lambda i: (i, 0)),
        out_shape=jax.ShapeDtypeStruct((t, D_MODEL), BF16),
        compiler_params=_cparams(("parallel",)),
    )(o_f, o_b, proj, proj, hf, hb, dn_g, lru_g)


def _mix_bwd(o_f, o_b, proj, hf, hb, dn_g, lru_g, dmix, name):
    t = proj.shape[0]
    tm = _row_tile(t)

    def body(of_ref, ob_ref, z_ref, lg_ref, hf_ref, hb_ref, dg_ref, lgn_ref, dm_ref,
             do_ref, dz_ref, dlg_ref, dh_ref, ddg_ref, dlgn_ref):
        _, vjp = jax.vjp(_mix_fn, of_ref[...], ob_ref[...], z_ref[...], lg_ref[...], hf_ref[...], hb_ref[...],
                         dg_ref[...], lgn_ref[...])
        do, _, dz, dlg, dh, _, ddg, dlgn = vjp(dm_ref[...])
        do_ref[...] = do
        dz_ref[...] = dz.astype(dz_ref.dtype)
        dlg_ref[...] = dlg.astype(dlg_ref.dtype)
        dh_ref[...] = dh

        @pl.when(pl.program_id(0) == 0)
        def _():
            ddg_ref[...] = jnp.zeros_like(ddg_ref)
            dlgn_ref[...] = jnp.zeros_like(dlgn_ref)

        ddg_ref[...] += ddg
        dlgn_ref[...] += dlgn

    row = pl.BlockSpec((tm, DN_WIDTH), lambda i: (i, 0))
    return pl.pallas_call(
        body, name=name, grid=(t // tm,),
        in_specs=_mix_specs(tm) + [pl.BlockSpec((tm, D_MODEL), lambda i: (i, 0))],
        out_specs=[row, row, row, row, pl.BlockSpec((1, LANES), lambda i: (0, 0)), pl.BlockSpec((1, DN_WIDTH), lambda i: (0, 0))],
        out_shape=[jax.ShapeDtypeStruct((t, DN_WIDTH), dt) for dt in (F32, BF16, BF16, F32)]
        + [jax.ShapeDtypeStruct((1, LANES), F32), jax.ShapeDtypeStruct((1, DN_WIDTH), F32)],
        compiler_params=_cparams(("arbitrary",)),
    )(o_f, o_b, proj, proj, hf, hb, dn_g, lru_g, dmix)


def _ffn_act_fwd(g3, u3, conv_w, conv_b, name):
    bsz, s, f = g3.shape
    nblk = f // LANES

    def body(g_ref, u_ref, w_ref, b_ref, o_ref):
        gate = _conv_fwd(g_ref[0], w_ref[...], 1) + b_ref[...]
        o_ref[0] = (_gelu(gate) * u_ref[0]).astype(o_ref.dtype)

    col = pl.BlockSpec((1, s, LANES), lambda b, j: (b, 0, j))
    return pl.pallas_call(
        body, name=name, grid=(bsz, nblk),
        in_specs=[col, col, pl.BlockSpec((3, LANES), lambda b, j: (0, j)), pl.BlockSpec((1, LANES), lambda b, j: (0, j))],
        out_specs=col, out_shape=jax.ShapeDtypeStruct((bsz, s, f), BF16),
        compiler_params=_cparams(("parallel", "parallel")),
    )(g3, u3, conv_w, conv_b)


def _ffn_act_bwd(g3, u3, conv_w, conv_b, dact3, name):
    bsz, s, f = g3.shape
    nblk = f // LANES

    def body(g_ref, u_ref, w_ref, b_ref, d_ref, dg_ref, du_ref, dw_ref, db_ref):
        g, w, u = g_ref[0], w_ref[...], u_ref[0]
        gate = _conv_fwd(g, w, 1) + b_ref[...]
        _, vjp = jax.vjp(lambda gt, uu: _gelu(gt) * uu, gate, u)
        dgate, du = vjp(d_ref[0])
        dg, dw = _conv_bwd(g, w, 1, dgate)
        dg_ref[0] = dg.astype(dg_ref.dtype)
        du_ref[0] = du.astype(du_ref.dtype)

        @pl.when(pl.program_id(1) == 0)
        def _():
            dw_ref[...] = jnp.zeros_like(dw_ref)
            db_ref[...] = jnp.zeros_like(db_ref)

        dw_ref[...] += dw
        db_ref[...] += jnp.sum(dgate, axis=0, keepdims=True)

    col = pl.BlockSpec((1, s, LANES), lambda j, b: (b, 0, j))
    w3 = pl.BlockSpec((3, LANES), lambda j, b: (0, j))
    w1 = pl.BlockSpec((1, LANES), lambda j, b: (0, j))
    return pl.pallas_call(
        body, name=name, grid=(nblk, bsz),
        in_specs=[col, col, w3, w1, col], out_specs=[col, col, w3, w1],
        out_shape=[jax.ShapeDtypeStruct((bsz, s, f), BF16), jax.ShapeDtypeStruct((bsz, s, f), BF16),
                   jax.ShapeDtypeStruct((3, f), F32), jax.ShapeDtypeStruct((1, f), F32)],
        compiler_params=_cparams(("parallel", "arbitrary")),
    )(g3, u3, conv_w, conv_b, dact3)


def _ple_fn(r, pg, pp, bg):
    return r + _sigmoid(pg + bg) * pp


def _ple_fwd(r, pg, pp, bg, name):
    t, d = r.shape
    tm = _row_tile(t)

    def body(r_ref, pg_ref, pp_ref, bg_ref, o_ref):
        o_ref[...] = _ple_fn(r_ref[...], pg_ref[...], pp_ref[...], bg_ref[...])

    row = pl.BlockSpec((tm, d), lambda i: (i, 0))
    return pl.pallas_call(
        body, name=name, grid=(t // tm,), in_specs=[row, row, row, pl.BlockSpec((1, d), lambda i: (0, 0))],
        out_specs=row, out_shape=jax.ShapeDtypeStruct((t, d), F32),
        compiler_params=_cparams(("parallel",)),
    )(r, pg, pp, bg)


def _ple_bwd(pg, pp, bg, dr, name):
    t, d = pg.shape
    tm = _row_tile(t)

    def body(pg_ref, pp_ref, bg_ref, dr_ref, dpg_ref, dpp_ref, dbg_ref):
        _, vjp = jax.vjp(lambda a, b, c: _sigmoid(a + c) * b, pg_ref[...], pp_ref[...], bg_ref[...])
        dpg, dpp, dbg = vjp(dr_ref[...])
        dpg_ref[...] = dpg.astype(dpg_ref.dtype)
        dpp_ref[...] = dpp.astype(dpp_ref.dtype)

        @pl.when(pl.program_id(0) == 0)
        def _():
            dbg_ref[...] = jnp.zeros_like(dbg_ref)

        dbg_ref[...] += dbg

    row = pl.BlockSpec((tm, d), lambda i: (i, 0))
    vec = pl.BlockSpec((1, d), lambda i: (0, 0))
    return pl.pallas_call(
        body, name=name, grid=(t // tm,), in_specs=[row, row, vec, row], out_specs=[row, row, vec],
        out_shape=[jax.ShapeDtypeStruct((t, d), BF16), jax.ShapeDtypeStruct((t, d), BF16), jax.ShapeDtypeStruct((1, d), F32)],
        compiler_params=_cparams(("arbitrary",)),
    )(pg, pp, bg, dr)


def _loss_head(r, g, target, name):
    t, d = r.shape
    tm = _row_tile(t)

    def loss_fn(x, gg, tgt):
        err = _rms(x, gg) - tgt
        return 0.5 * jnp.sum(jnp.sum(err * err, axis=-1, keepdims=True) * (1.0 / d), axis=0, keepdims=True)

    def body(r_ref, g_ref, t_ref, l_ref, dr_ref, dg_ref):
        val, vjp = jax.vjp(lambda x, gg: loss_fn(x, gg, t_ref[...]), r_ref[...], g_ref[...])
        dx, dg = vjp(jnp.ones((1, 1), F32))
        dr_ref[...] = dx

        @pl.when(pl.program_id(0) == 0)
        def _():
            l_ref[...] = jnp.zeros_like(l_ref)
            dg_ref[...] = jnp.zeros_like(dg_ref)

        l_ref[...] += val
        dg_ref[...] += dg

    row = pl.BlockSpec((tm, d), lambda i: (i, 0))
    vec = pl.BlockSpec((1, d), lambda i: (0, 0))
    one = pl.BlockSpec((1, 1), lambda i: (0, 0))
    return pl.pallas_call(
        body, name=name, grid=(t // tm,), in_specs=[row, vec, row], out_specs=[one, row, vec],
        out_shape=[jax.ShapeDtypeStruct((1, 1), F32), jax.ShapeDtypeStruct((t, d), F32), jax.ShapeDtypeStruct((1, d), F32)],
        compiler_params=_cparams(("arbitrary",)),
    )(r, g, target)


def _adamw_math(w, gg, m, v, d_ref, nm_ref, nv_ref):
    nm = ADAM_B1 * m + (1.0 - ADAM_B1) * gg
    nv = ADAM_B2 * v + (1.0 - ADAM_B2) * (gg * gg)
    m_hat = nm / (1.0 - ADAM_B1 ** ADAM_STEP)
    v_hat = nv / (1.0 - ADAM_B2 ** ADAM_STEP)
    d_ref[...] = -ADAM_LR * (m_hat / (jnp.sqrt(v_hat) + ADAM_EPS) + ADAM_WD * w)
    nm_ref[...] = nm
    nv_ref[...] = nv


def _adamw(w, g, m, v, name):
    def body(w_ref, g_ref, m_ref, v_ref, d_ref, nm_ref, nv_ref):
        _adamw_math(w_ref[...], g_ref[...], m_ref[...], v_ref[...], d_ref, nm_ref, nv_ref)

    whole = pl.BlockSpec(memory_space=pltpu.VMEM)
    return pl.pallas_call(
        body, name=name, in_specs=[whole] * 4, out_specs=[whole] * 3, out_shape=[jax.ShapeDtypeStruct(w.shape, F32)] * 3,
        compiler_params=pltpu.CompilerParams(vmem_limit_bytes=VMEM_LIMIT),
    )(w, g, m, v)


def _prepare_weights(w):
    nl = w["w_in"].shape[0]
    w_in = w["w_in"].astype(BF16)
    pad = jnp.zeros(w_in.shape[:2] + (P_COLS - w_in.shape[2],), BF16)
    split = 4 * DN_WIDTH + N_GATE
    w_in_p = jnp.concatenate([w_in[:, :, split:], w_in[:, :, :split], pad], axis=-1)
    gate_vec = lambda a: jnp.pad(a.reshape(nl, 1, N_GATE // 2), ((0, 0), (0, 0), (N_GATE // 2, LANES - N_GATE)))

    def pair_blocks(a):
        a = a.reshape(nl, 2, 4, 2, 64, 64)
        z = jnp.zeros_like(a[:, :, :, 0])
        top = jnp.concatenate([a[:, :, :, 0], z], axis=-1)
        bot = jnp.concatenate([z, a[:, :, :, 1]], axis=-1)
        return jnp.concatenate([top, bot], axis=-2).astype(BF16)

    bf = lambda a: a.astype(BF16)
    return dict(
        norm1_g=w["norm1_g"][:, None, :], w_in=w_in_p,
        dn_conv_w=w["dn_conv_w"], alog=gate_vec(w["dn_a_log"]), dtb=gate_vec(w["dn_dt_bias"]),
        dn_norm_g=w["dn_norm_g"][:, None, :], lru_conv_w=w["lru_conv_w"], lru_conv_b=w["lru_conv_b"][:, None, :],
        lru_wa=pair_blocks(w["lru_wa"]), lru_ba=w["lru_ba"], lru_wx=pair_blocks(w["lru_wx"]), lru_bx=w["lru_bx"],
        lru_lambda=w["lru_lambda"], lru_norm_g=w["lru_norm_g"][:, None, :],
        w_out=bf(w["w_out"]), norm2_g=w["norm2_g"][:, None, :], ffn_wg=bf(w["ffn_wg"]), ffn_wu=bf(w["ffn_wu"]),
        ffn_conv_w=w["ffn_conv_w"], ffn_conv_b=w["ffn_conv_b"][:, None, :], ffn_wd=bf(w["ffn_wd"]),
        ple_norm_g=w["ple_norm_g"][:, None, :], ple_wg=bf(w["ple_wg"]), ple_bg=w["ple_bg"][:, None, :], ple_wp=bf(w["ple_wp"]),
        final_g=w["final_g"][None, :],
    )


def _unpair_blocks(a):
    top = a[:, :, :64, :64]
    bot = a[:, :, 64:, 64:]
    return jnp.stack([top, bot], axis=2).reshape(2, 8, 64, 64)


def _local_step(x, p, target, w):
    bsz, s, d = x.shape
    t = bsz * s
    nl = w["w_in"].shape[0]
    kw = _prepare_weights(w)
    flat = lambda a: a.reshape(t, a.shape[-1])
    seq = lambda a: a.reshape(bsz, s, a.shape[-1])

    saved = []
    r = flat(x)
    for i in range(nl):
        n = f"l{i}_"
        sv = {"r0": r}
        h, proj = _norm_matmul(r, kw["norm1_g"][i], [kw["w_in"][i]], n + "in_proj")
        gb = _dn_gates_fwd(proj, kw["alog"][i], kw["dtb"][i], n + "dn_gates")
        qkv = _dn_prep_fwd(seq(proj), kw["dn_conv_w"][i], n + "dn_prep")
        in_f = _dn_intra_fwd(qkv, seq(gb), False, n + "dn_intra_fwd")
        in_b = _dn_intra_fwd(qkv, seq(gb), True, n + "dn_intra_rev")
        o_f, st_f = _dn_rec_fwd(in_f, False, n + "dn_rec_fwd")
        o_b, st_b = _dn_rec_fwd(in_b, True, n + "dn_rec_rev")
        lru_args = (seq(proj), kw["lru_conv_w"][i], kw["lru_conv_b"][i], kw["lru_wa"][i], kw["lru_ba"][i],
                    kw["lru_wx"][i], kw["lru_bx"][i], kw["lru_lambda"][i])
        hf, hb = _lru_fwd(*lru_args, n + "lru")
        mix_args = (flat(o_f), flat(o_b), proj, flat(hf), flat(hb), kw["dn_norm_g"][i], kw["lru_norm_g"][i])
        mix = _mix_fwd(*mix_args, n + "mix")
        r1 = _matmul([(mix, kw["w_out"][i])], n + "out_proj", res=r)
        h2, fg, fu = _norm_matmul(r1, kw["norm2_g"][i], [kw["ffn_wg"][i], kw["ffn_wu"][i]], n + "ffn_gu")
        act = _ffn_act_fwd(seq(fg), seq(fu), kw["ffn_conv_w"][i], kw["ffn_conv_b"][i], n + "ffn_act")
        r2 = _matmul([(flat(act), kw["ffn_wd"][i])], n + "ffn_d", res=r1)
        hp, pg = _norm_matmul(r2, kw["ple_norm_g"][i], [kw["ple_wg"][i]], n + "ple_g")
        pi = flat(p[i])
        pp = _matmul([(pi, kw["ple_wp"][i])], n + "ple_p")
        r3 = _ple_fwd(r2, pg, pp, kw["ple_bg"][i], n + "ple")
        sv.update(h=h, proj=proj, gb=gb, qkv=qkv, st_f=st_f, st_b=st_b, in_f=in_f, in_b=in_b, lru_args=lru_args, hf=hf, hb=hb,
                  mix_args=mix_args, mix=mix, r1=r1, h2=h2, fg=fg, fu=fu, act=act, r2=r2, hp=hp, pg=pg, pp=pp, pi=pi)
        saved.append(sv)
        r = r3

    loss, dr, dfinal = _loss_head(r, kw["final_g"], flat(target), "loss_head")
    grads = {k: [None] * nl for k in (
        "norm1_g", "w_in", "dn_conv_w", "dn_a_log", "dn_dt_bias", "dn_norm_g", "lru_conv_w", "lru_conv_b", "lru_wa", "lru_ba",
        "lru_wx", "lru_bx", "lru_lambda", "lru_norm_g", "w_out", "norm2_g", "ffn_wg", "ffn_wu", "ffn_conv_w", "ffn_conv_b",
        "ffn_wd", "ple_norm_g", "ple_wg", "ple_bg", "ple_wp", "dn_a_log_lanes", "dn_dt_bias_lanes")}

    for i in reversed(range(nl)):
        n = f"b{i}_"
        sv = saved[i]
        dpg, dpp, dbg = _ple_bwd(sv["pg"], sv["pp"], kw["ple_bg"][i], dr, n + "ple")
        grads["ple_bg"][i] = dbg[0]
        grads["ple_wp"][i] = _matmul_tn(sv["pi"], dpp, n + "ple_wp")
        grads["ple_wg"][i] = _matmul_tn(sv["hp"], dpg, n + "ple_wg")
        dr2, dr2b, dg = _matmul_norm_bwd([(dpg, kw["ple_wg"][i])], sv["r2"], kw["ple_norm_g"][i], dr, n + "ple_dh")
        grads["ple_norm_g"][i] = dg[0]
        grads["ffn_wd"][i] = _matmul_tn(flat(sv["act"]), dr2b, n + "ffn_wd")
        dact = _matmul([(dr2b, kw["ffn_wd"][i])], n + "ffn_dact", nt=True)
        dfg, dfu, dcw, dcb = _ffn_act_bwd(seq(sv["fg"]), seq(sv["fu"]), kw["ffn_conv_w"][i], kw["ffn_conv_b"][i], seq(dact),
                                          n + "ffn_act")
        grads["ffn_conv_w"][i] = dcw
        grads["ffn_conv_b"][i] = dcb[0]
        grads["ffn_wg"][i] = _matmul_tn(sv["h2"], flat(dfg), n + "ffn_wg")
        grads["ffn_wu"][i] = _matmul_tn(sv["h2"], flat(dfu), n + "ffn_wu")
        dr1, dr1b, dg = _matmul_norm_bwd([(flat(dfg), kw["ffn_wg"][i]), (flat(dfu), kw["ffn_wu"][i])], sv["r1"], kw["norm2_g"][i],
                                         dr2, n + "ffn_dh")
        grads["norm2_g"][i] = dg[0]
        grads["w_out"][i] = _matmul_tn(sv["mix"], dr1b, n + "w_out")
        dmix = _matmul([(dr1b, kw["w_out"][i])], n + "dmix", nt=True)
        do, dz, dlg, dh, ddn_g, dlru_g = _mix_bwd(*sv["mix_args"], dmix, n + "mix")
        grads["dn_norm_g"][i] = ddn_g[0]
        grads["lru_norm_g"][i] = dlru_g[0]
        dlx, dcw, dcb, dwa, dba, dwx, dbx, dlam = _lru_bwd(*sv["lru_args"], sv["hf"], sv["hb"], seq(dh), n + "lru")
        grads["lru_conv_w"][i] = dcw
        grads["lru_conv_b"][i] = dcb[0]
        grads["lru_wa"][i] = _unpair_blocks(dwa)
        grads["lru_wx"][i] = _unpair_blocks(dwx)
        grads["lru_ba"][i], grads["lru_bx"][i], grads["lru_lambda"][i] = dba, dbx, dlam
        ct_f = _dn_rec_bwd(sv["in_f"], sv["st_f"], seq(do), False, n + "dn_rec_fwd")
        ct_b = _dn_rec_bwd(sv["in_b"], sv["st_b"], seq(do), True, n + "dn_rec_rev")
        dqkv_f, dgb_f = _dn_intra_bwd(sv["qkv"], seq(sv["gb"]), ct_f, False, n + "dn_intra_fwd")
        dqkv_b, dgb_b = _dn_intra_bwd(sv["qkv"], seq(sv["gb"]), ct_b, True, n + "dn_intra_rev")
        dgate, dalog, ddtb = _dn_gates_bwd(sv["proj"], kw["alog"][i], kw["dtb"][i], flat(dgb_f), flat(dgb_b), n + "dn_gates")
        grads["dn_a_log"][i] = dalog[0, N_GATE // 2:N_GATE].reshape(2, DN_HEADS)
        grads["dn_dt_bias"][i] = ddtb[0, N_GATE // 2:N_GATE].reshape(2, DN_HEADS)
        grads["dn_a_log_lanes"][i] = dalog[0]
        grads["dn_dt_bias_lanes"][i] = ddtb[0]
        dpqkv, dcw = _dn_prep_bwd(seq(sv["proj"]), kw["dn_conv_w"][i], dqkv_f, dqkv_b, n + "dn_prep")
        grads["dn_conv_w"][i] = dcw
        segs = [flat(dlx), dlg, flat(dpqkv), dz, dgate]
        w_in_i = kw["w_in"][i]
        dq3 = flat(dpqkv)
        wseg = DN_WIDTH
        terms = [(flat(dlx), (w_in_i, wseg, P_LX // wseg)), (dlg, (w_in_i, wseg, P_LG // wseg))]
        terms += [((dq3, wseg, j), (w_in_i, wseg, P_Q // wseg + j)) for j in range(3)]
        terms += [(dz, (w_in_i, wseg, P_Z // wseg)), (dgate, (w_in_i, LANES, P_GATE // LANES))]
        grads["w_in"][i] = _matmul_tn_cols(sv["h"], segs, n + "w_in")
        dr, _, dg = _matmul_norm_bwd(terms, sv["r0"], kw["norm1_g"][i], dr1, n + "in_dh")
        grads["norm1_g"][i] = dg[0]

    grads["final_g"] = dfinal[0]
    return loss[0, 0], dr.reshape(bsz, s, d), grads


MESH = pl.DeviceIdType.MESH
ANY = pl.BlockSpec(memory_space=pl.ANY)
N_CHIPS = 4
HALF = 2


def _place():
    x, y, c = lax.axis_index("x"), lax.axis_index("y"), lax.axis_index("c")
    chips = [(1 - x, y), (x, 1 - y), (1 - x, 1 - y)]
    return x, y, c, chips


def _dma_sems(n):
    return pltpu.SemaphoreType.DMA((n,))


def _gather_weights(shards):
    nt = len(shards)

    def body(*refs):
        w_refs, g_refs = refs[:nt], refs[nt:2 * nt]
        send_sems, recv_sems = refs[2 * nt:]
        x, y, c, chips = _place()
        me = 2 * x + y
        mine = pl.ds(c * HALF, HALF)
        theirs = pl.ds((1 - c) * HALF, HALF)

        def rc(src, dst, k, to):
            return pltpu.make_async_remote_copy(src_ref=src, dst_ref=dst, send_sem=send_sems.at[k], recv_sem=recv_sems.at[k],
                                                device_id=to, device_id_type=MESH)

        first = []
        for t in range(nt):
            for r, (px, py) in enumerate(chips):
                first.append(rc(w_refs[t].at[mine], g_refs[t].at[me, mine], 6 * t + r, (px, py, c)))
        for cp in first:
            cp.start()
        passed = []
        for t in range(nt):
            for r, (px, py) in enumerate(chips):
                peer = 2 * px + py
                rc(w_refs[t].at[mine], g_refs[t].at[peer, mine], 6 * t + r, (px, py, c)).wait_recv()
                fw = rc(g_refs[t].at[peer, mine], g_refs[t].at[peer, mine], 6 * t + 3 + r, (x, y, 1 - c))
                fw.start()
                passed.append(fw)
        for t in range(nt):
            for r, (px, py) in enumerate(chips):
                peer = 2 * px + py
                rc(g_refs[t].at[peer, theirs], g_refs[t].at[peer, theirs], 6 * t + 3 + r, (x, y, 1 - c)).wait_recv()
        for cp in first + passed:
            cp.wait_send()

    return pl.pallas_call(
        body, name="gather_weights", in_specs=[ANY] * nt, out_specs=[ANY] * nt,
        out_shape=[jax.ShapeDtypeStruct((N_CHIPS,) + a.shape, a.dtype) for a in shards],
        scratch_shapes=[_dma_sems(6 * nt), _dma_sems(6 * nt)],
    )(*shards)


def _swap_halves(blocks):
    nt = len(blocks)

    def body(*refs):
        p_refs, l_refs = refs[:nt], refs[nt:2 * nt]
        send_sems, recv_sems = refs[2 * nt:]
        x, y, c, _ = _place()
        theirs = pl.ds((1 - c) * HALF, HALF)
        cps = [pltpu.make_async_remote_copy(src_ref=p_refs[t].at[:, theirs], dst_ref=l_refs[t], send_sem=send_sems.at[t],
                                            recv_sem=recv_sems.at[t], device_id=(x, y, 1 - c), device_id_type=MESH)
               for t in range(nt)]
        for cp in cps:
            cp.start()
        for cp in cps:
            cp.wait_send()
            cp.wait_recv()

    return pl.pallas_call(
        body, name="swap_halves", in_specs=[ANY] * nt, out_specs=[ANY] * nt,
        out_shape=[jax.ShapeDtypeStruct((a.shape[0], HALF) + a.shape[2:], a.dtype) for a in blocks],
        scratch_shapes=[_dma_sems(nt), _dma_sems(nt)],
    )(*blocks)


def _tile_rows(a, row_bytes):
    best = None
    for d in range(16, a + 1, 16):
        if a % d == 0 and d * row_bytes <= 2 * 1024 * 1024:
            best = d
    return best if best is not None else a


def _add_halves(pg, l1, c_arr, name):
    n, _, a, b = pg.shape
    ta = _tile_rows(a, 4 * b)

    def body(c_ref, a_ref, b_ref, o_ref):
        o_ref[...] = (a_ref[...] + b_ref[...]).astype(o_ref.dtype)

    grid_spec = pltpu.PrefetchScalarGridSpec(
        num_scalar_prefetch=1, grid=(n, HALF, a // ta),
        in_specs=[pl.BlockSpec((1, 1, ta, b), lambda k, l, i, c_ref: (k, c_ref[0] * HALF + l, i, 0)),
                  pl.BlockSpec((1, 1, ta, b), lambda k, l, i, c_ref: (k, l, i, 0))],
        out_specs=pl.BlockSpec((1, 1, ta, b), lambda k, l, i, c_ref: (k, l, i, 0)))
    return pl.pallas_call(
        body, name=name, grid_spec=grid_spec, out_shape=jax.ShapeDtypeStruct((n, HALF, a, b), BF16),
        compiler_params=_cparams(("parallel", "parallel", "parallel")),
    )(c_arr, pg, l1)


def _scatter_chips(blocks):
    nt = len(blocks)

    def body(*refs):
        q_refs, l_refs = refs[:nt], refs[nt:2 * nt]
        send_sems, recv_sems = refs[2 * nt:]
        x, y, c, chips = _place()
        me = 2 * x + y

        def rc(t, r, src_chip, dst_chip, to):
            return pltpu.make_async_remote_copy(src_ref=q_refs[t].at[src_chip], dst_ref=l_refs[t].at[dst_chip],
                                                send_sem=send_sems.at[3 * t + r], recv_sem=recv_sems.at[3 * t + r],
                                                device_id=to, device_id_type=MESH)

        sends = [rc(t, r, 2 * px + py, me, (px, py, c)) for t in range(nt) for r, (px, py) in enumerate(chips)]
        for cp in sends:
            cp.start()
        for t in range(nt):
            for r, (px, py) in enumerate(chips):
                rc(t, r, 2 * px + py, 2 * px + py, (px, py, c)).wait_recv()
        for cp in sends:
            cp.wait_send()

    return pl.pallas_call(
        body, name="scatter_chips", in_specs=[ANY] * nt, out_specs=[ANY] * nt,
        out_shape=[jax.ShapeDtypeStruct(a.shape, a.dtype) for a in blocks],
        scratch_shapes=[_dma_sems(3 * nt), _dma_sems(3 * nt)],
    )(*blocks)


def _sum_chips(l2, own, me_arr, name):
    n, _, a, b = l2.shape
    ta = _tile_rows(a, 4 * b)

    def body(me_ref, a_ref, own_ref, o_ref):
        me = me_ref[0]
        acc = jnp.where(me == 0, own_ref[0, 0], a_ref[0, 0]).astype(F32)
        for k in range(1, n):
            acc = acc + jnp.where(me == k, own_ref[0, 0], a_ref[k, 0]).astype(F32)
        o_ref[0] = acc

    grid_spec = pltpu.PrefetchScalarGridSpec(
        num_scalar_prefetch=1, grid=(HALF, a // ta),
        in_specs=[pl.BlockSpec((n, 1, ta, b), lambda l, i, me_ref: (0, l, i, 0)),
                  pl.BlockSpec((1, 1, ta, b), lambda l, i, me_ref: (me_ref[0], l, i, 0))],
        out_specs=pl.BlockSpec((1, ta, b), lambda l, i, me_ref: (l, i, 0)))
    return pl.pallas_call(
        body, name=name, grid_spec=grid_spec, out_shape=jax.ShapeDtypeStruct((HALF, a, b), F32),
        compiler_params=_cparams(("parallel", "parallel")),
    )(me_arr, l2, own)


def _swap_reduced(parts):
    nt = len(parts)

    def body(*refs):
        r_refs, g_refs = refs[:nt], refs[nt:2 * nt]
        send_sems, recv_sems = refs[2 * nt:]
        x, y, c, _ = _place()
        cps = [pltpu.make_async_remote_copy(src_ref=r_refs[t], dst_ref=g_refs[t], send_sem=send_sems.at[t],
                                            recv_sem=recv_sems.at[t], device_id=(x, y, 1 - c), device_id_type=MESH)
               for t in range(nt)]
        for cp in cps:
            cp.start()
        for cp in cps:
            cp.wait_send()
            cp.wait_recv()

    return pl.pallas_call(
        body, name="swap_reduced", in_specs=[ANY] * nt, out_specs=[ANY] * nt,
        out_shape=[jax.ShapeDtypeStruct(a.shape, a.dtype) for a in parts],
        scratch_shapes=[_dma_sems(nt), _dma_sems(nt)],
    )(*parts)


def _adamw_halves(w, mine, theirs, m, v, c_arr, name):
    nl, a, b = w.shape
    ta = _tile_rows(a, 4 * b)

    def body(c_ref, w_ref, g1_ref, g2_ref, m_ref, v_ref, g_ref, d_ref, nm_ref, nv_ref):
        gg = jnp.where(pl.program_id(0) // HALF == c_ref[0], g1_ref[...], g2_ref[...])
        g_ref[...] = gg
        _adamw_math(w_ref[...], gg, m_ref[...], v_ref[...], d_ref, nm_ref, nv_ref)

    full = pl.BlockSpec((1, ta, b), lambda l, i, c_ref: (l, i, 0))
    half = pl.BlockSpec((1, ta, b), lambda l, i, c_ref: (l % HALF, i, 0))
    grid_spec = pltpu.PrefetchScalarGridSpec(num_scalar_prefetch=1, grid=(nl, a // ta),
                                             in_specs=[full, half, half, full, full], out_specs=[full] * 4)
    return pl.pallas_call(
        body, name=name, grid_spec=grid_spec, out_shape=[jax.ShapeDtypeStruct(w.shape, F32)] * 4,
        compiler_params=_cparams(("parallel", "parallel")),
    )(c_arr, w, mine, theirs, m, v)


def _gather_chips(v):
    def body(v_ref, g_ref, send_sems, recv_sems, local_sem):
        x, y, c, chips = _place()
        me = 2 * x + y
        loc = pltpu.make_async_copy(v_ref, g_ref.at[me], local_sem)
        loc.start()
        sends = [pltpu.make_async_remote_copy(src_ref=v_ref, dst_ref=g_ref.at[me], send_sem=send_sems.at[r], recv_sem=recv_sems.at[r],
                                              device_id=(px, py, c), device_id_type=MESH) for r, (px, py) in enumerate(chips)]
        for cp in sends:
            cp.start()
        for r, (px, py) in enumerate(chips):
            pltpu.make_async_remote_copy(src_ref=v_ref, dst_ref=g_ref.at[2 * px + py], send_sem=send_sems.at[r],
                                         recv_sem=recv_sems.at[r], device_id=(px, py, c), device_id_type=MESH).wait_recv()
        for cp in sends:
            cp.wait_send()
        loc.wait()

    return pl.pallas_call(
        body, name="gather_chips", in_specs=[ANY], out_specs=ANY, out_shape=jax.ShapeDtypeStruct((N_CHIPS,) + v.shape, v.dtype),
        scratch_shapes=[_dma_sems(3), _dma_sems(3), pltpu.SemaphoreType.DMA],
    )(v)


BIG = (("w_in", 2), ("w_out", 1), ("ffn_wg", 2), ("ffn_wu", 2), ("ffn_wd", 1), ("ple_wg", 1), ("ple_wp", 2))
SMALL = (("dn_conv_w", 2), ("lru_conv_w", 2), ("lru_ba", 2), ("lru_bx", 2), ("lru_lambda", 2), ("ffn_conv_w", 2))
REPL = ("norm1_g", "dn_a_log", "dn_dt_bias", "dn_norm_g", "lru_conv_b", "lru_wa", "lru_wx", "lru_norm_g", "norm2_g",
        "ffn_conv_b", "ple_norm_g", "ple_bg", "final_g")
WEIGHTS = ("norm1_g", "w_in", "dn_conv_w", "dn_a_log", "dn_dt_bias", "dn_norm_g", "lru_conv_w", "lru_conv_b", "lru_wa", "lru_ba",
           "lru_wx", "lru_bx", "lru_lambda", "lru_norm_g", "w_out", "norm2_g", "ffn_wg", "ffn_wu", "ffn_conv_w", "ffn_conv_b",
           "ffn_wd", "ple_norm_g", "ple_wg", "ple_bg", "ple_wp", "final_g")
GATE_PARAMS = ("dn_a_log", "dn_dt_bias")
ROW_ALIGN = 16


def _rows_for(n_elems):
    rows = -(-n_elems // LANES)
    return -(-rows // ROW_ALIGN) * ROW_ALIGN


def _join_chips(g, own, me, axis):
    return jnp.concatenate([jnp.where(me == k, own, g[k]) for k in range(N_CHIPS)], axis=axis)


def _chip_split(layers, ranges, name):
    nl = len(layers)
    k, n = layers[0].shape
    w = sum(hi - lo for lo, hi in ranges[0])
    tk = _row_tile(k, 128)

    def body(*refs):
        x_refs, o_ref = refs[:nl], refs[nl]
        for l in range(nl):
            for c in range(N_CHIPS):
                off = 0
                for lo, hi in ranges[c]:
                    o_ref[c, l, :, off:off + hi - lo] = x_refs[l][:, lo:hi]
                    off += hi - lo

    return pl.pallas_call(
        body, name=name, grid=(k // tk,), in_specs=[pl.BlockSpec((tk, n), lambda i: (i, 0))] * nl,
        out_specs=pl.BlockSpec((N_CHIPS, nl, tk, w), lambda i: (0, 0, i, 0)),
        out_shape=jax.ShapeDtypeStruct((N_CHIPS, nl, k, w), F32),
        compiler_params=_cparams(("parallel",)),
    )(*layers)


def _even_ranges(n):
    w = n // N_CHIPS
    return [[(c * w, (c + 1) * w)] for c in range(N_CHIPS)]


def _w_in_ranges():
    split = 4 * DN_WIDTH + N_GATE
    total = split + 2 * LRU_WIDTH
    w = total // N_CHIPS
    out = []
    for c in range(N_CHIPS):
        lo, hi = c * w, (c + 1) * w
        parts = []
        if lo < split:
            parts.append((P_Q + lo, P_Q + min(hi, split)))
        if hi > split:
            parts.append((max(lo, split) - split, hi - split))
        out.append(parts)
    return out


def _chip_blocks(layers, axis):
    nl = len(layers)
    size = layers[0].shape[axis - 1] // N_CHIPS
    cut = (lambda g, k: g[:, k * size:(k + 1) * size]) if axis == 2 else (lambda g, k: g[k * size:(k + 1) * size])
    return jnp.stack([jnp.stack([cut(layers[l], k) for l in range(nl)]) for k in range(N_CHIPS)])


def _pack_small(w):
    nl = w[SMALL[0][0]].shape[0]
    flat = jnp.concatenate([w[n].reshape(nl, -1) for n, _ in SMALL], axis=1)
    rows = _rows_for(flat.shape[1])
    return jnp.pad(flat, ((0, 0), (0, rows * LANES - flat.shape[1]))).reshape(nl, rows, LANES)


def _unpack_small_gathered(g, own, me, w):
    nl = g.shape[1]
    flat = jnp.stack([jnp.where(me == k, own, g[k]) for k in range(N_CHIPS)]).reshape(N_CHIPS, nl, -1)
    out, off = {}, 0
    for n, _ in SMALL:
        _, a, b = w[n].shape
        piece = flat[:, :, off:off + a * b].reshape(N_CHIPS, nl, a, b)
        off += a * b
        out[n] = jnp.transpose(piece, (1, 2, 0, 3)).reshape(nl, a, N_CHIPS * b)
    return out


def _pack_small_grads(grads, w):
    nl = w["w_in"].shape[0]
    cols = []
    for n, _ in SMALL:
        _, a, b = w[n].shape
        gfull = jnp.stack(grads[n])
        cols.append(jnp.transpose(gfull.reshape(nl, a, N_CHIPS, b), (2, 0, 1, 3)).reshape(N_CHIPS, nl, a * b))
    small = jnp.concatenate(cols, axis=2)
    rs = _rows_for(small.shape[2])
    small = jnp.pad(small, ((0, 0), (0, 0), (0, rs * LANES - small.shape[2])))
    src = lambda n: n + "_lanes" if n in GATE_PARAMS else n
    rep = jnp.concatenate([(grads[n] if n == "final_g" else jnp.stack(grads[src(n)])).reshape(-1) for n in REPL])
    rr = _rows_for(-(-rep.shape[0] // (N_CHIPS * nl)))
    rep = jnp.pad(rep, (0, N_CHIPS * nl * rr * LANES - rep.shape[0])).reshape(N_CHIPS, nl, rr * LANES)
    return jnp.concatenate([small, rep], axis=2).reshape(N_CHIPS, nl, rs + rr, LANES), (rs, rr)


def _unpack_small_reduced(g, rep_all, rows, w):
    rs, _ = rows
    nl = g.shape[0]
    out = {}
    flat = g[:, :rs].reshape(nl, -1)
    off = 0
    for n, _ in SMALL:
        _, a, b = w[n].shape
        out[n] = flat[:, off:off + a * b].reshape(nl, a, b)
        off += a * b
    flat = rep_all.reshape(-1)
    off = 0
    for n in REPL:
        if n in GATE_PARAMS:
            size = nl * LANES
            out[n] = flat[off:off + size].reshape(nl, LANES)[:, N_GATE // 2:N_GATE].reshape(w[n].shape)
        else:
            size = math.prod(w[n].shape)
            out[n] = flat[off:off + size].reshape(w[n].shape)
        off += size
    return out


def kernel(x, p, norm1_g, w_in, dn_conv_w, dn_a_log, dn_dt_bias, dn_norm_g, lru_conv_w, lru_conv_b, lru_wa, lru_ba, lru_wx, lru_bx, lru_lambda, lru_norm_g, w_out, norm2_g, ffn_wg, ffn_wu, ffn_conv_w, ffn_conv_b, ffn_wd, ple_norm_g, ple_wg, ple_bg, ple_wp, final_g, loss_target, m_norm1_g, m_w_in, m_dn_conv_w, m_dn_a_log, m_dn_dt_bias, m_dn_norm_g, m_lru_conv_w, m_lru_conv_b, m_lru_wa, m_lru_ba, m_lru_wx, m_lru_bx, m_lru_lambda, m_lru_norm_g, m_w_out, m_norm2_g, m_ffn_wg, m_ffn_wu, m_ffn_conv_w, m_ffn_conv_b, m_ffn_wd, m_ple_norm_g, m_ple_wg, m_ple_bg, m_ple_wp, m_final_g, v_norm1_g, v_w_in, v_dn_conv_w, v_dn_a_log, v_dn_dt_bias, v_dn_norm_g, v_lru_conv_w, v_lru_conv_b, v_lru_wa, v_lru_ba, v_lru_wx, v_lru_bx, v_lru_lambda, v_lru_norm_g, v_w_out, v_norm2_g, v_ffn_wg, v_ffn_wu, v_ffn_conv_w, v_ffn_conv_b, v_ffn_wd, v_ple_norm_g, v_ple_wg, v_ple_bg, v_ple_wp, v_final_g):
    w = dict(norm1_g=norm1_g, w_in=w_in, dn_conv_w=dn_conv_w, dn_a_log=dn_a_log, dn_dt_bias=dn_dt_bias, dn_norm_g=dn_norm_g,
             lru_conv_w=lru_conv_w, lru_conv_b=lru_conv_b, lru_wa=lru_wa, lru_ba=lru_ba, lru_wx=lru_wx, lru_bx=lru_bx,
             lru_lambda=lru_lambda, lru_norm_g=lru_norm_g, w_out=w_out, norm2_g=norm2_g, ffn_wg=ffn_wg, ffn_wu=ffn_wu,
             ffn_conv_w=ffn_conv_w, ffn_conv_b=ffn_conv_b, ffn_wd=ffn_wd, ple_norm_g=ple_norm_g, ple_wg=ple_wg, ple_bg=ple_bg,
             ple_wp=ple_wp, final_g=final_g)
    m = dict(norm1_g=m_norm1_g, w_in=m_w_in, dn_conv_w=m_dn_conv_w, dn_a_log=m_dn_a_log, dn_dt_bias=m_dn_dt_bias,
             dn_norm_g=m_dn_norm_g, lru_conv_w=m_lru_conv_w, lru_conv_b=m_lru_conv_b, lru_wa=m_lru_wa, lru_ba=m_lru_ba,
             lru_wx=m_lru_wx, lru_bx=m_lru_bx, lru_lambda=m_lru_lambda, lru_norm_g=m_lru_norm_g, w_out=m_w_out, norm2_g=m_norm2_g,
             ffn_wg=m_ffn_wg, ffn_wu=m_ffn_wu, ffn_conv_w=m_ffn_conv_w, ffn_conv_b=m_ffn_conv_b, ffn_wd=m_ffn_wd,
             ple_norm_g=m_ple_norm_g, ple_wg=m_ple_wg, ple_bg=m_ple_bg, ple_wp=m_ple_wp, final_g=m_final_g)
    v = dict(norm1_g=v_norm1_g, w_in=v_w_in, dn_conv_w=v_dn_conv_w, dn_a_log=v_dn_a_log, dn_dt_bias=v_dn_dt_bias,
             dn_norm_g=v_dn_norm_g, lru_conv_w=v_lru_conv_w, lru_conv_b=v_lru_conv_b, lru_wa=v_lru_wa, lru_ba=v_lru_ba,
             lru_wx=v_lru_wx, lru_bx=v_lru_bx, lru_lambda=v_lru_lambda, lru_norm_g=v_lru_norm_g, w_out=v_w_out, norm2_g=v_norm2_g,
             ffn_wg=v_ffn_wg, ffn_wu=v_ffn_wu, ffn_conv_w=v_ffn_conv_w, ffn_conv_b=v_ffn_conv_b, ffn_wd=v_ffn_wd,
             ple_norm_g=v_ple_norm_g, ple_wg=v_ple_wg, ple_bg=v_ple_bg, ple_wp=v_ple_wp, final_g=v_final_g)

    me = 2 * lax.axis_index("x") + lax.axis_index("y")
    own = [w[n].astype(BF16) for n, _ in BIG] + [_pack_small(w)]
    gathered = _gather_weights(own)
    full = {n: w[n] for n in REPL}
    for (n, axis), g, o in zip(BIG, gathered, own):
        full[n] = _join_chips(g, o, me, axis)
    full.update(_unpack_small_gathered(gathered[-1], own[-1], me, w))

    loss_local, grad_x, grads = _local_step(x, p, loss_target, full)
    loss = lax.psum(loss_local, ("x", "y", "c"))

    small_pack, rows = _pack_small_grads(grads, w)
    blocks = []
    for n, axis in BIG:
        if axis == 2:
            ranges = _w_in_ranges() if n == "w_in" else _even_ranges(grads[n][0].shape[1])
            blocks.append(_chip_split(grads[n], ranges, "chip_split_" + n))
        else:
            blocks.append(_chip_blocks(grads[n], axis))
    blocks.append(small_pack)
    names = [n for n, _ in BIG] + ["small"]
    c_idx = lax.axis_index("c")
    c_arr = c_idx.astype(jnp.int32).reshape(1)
    me_arr = me.astype(jnp.int32).reshape(1)
    from_sibling = _swap_halves(blocks)
    halves = [_add_halves(b4, l1, c_arr, "add_halves_" + n) for n, b4, l1 in zip(names, blocks, from_sibling)]
    arrived = _scatter_chips(halves)
    mine = [_sum_chips(l2, q, me_arr, "sum_chips_" + n) for n, l2, q in zip(names, arrived, halves)]
    theirs = _swap_reduced(mine)

    g, deltas, new_m, new_v = {}, {}, {}, {}
    for i, (n, _) in enumerate(BIG):
        g[n], deltas[n], new_m[n], new_v[n] = _adamw_halves(w[n], mine[i], theirs[i], m[n], v[n], c_arr, "adamw_" + n)
    small = jnp.where(c_idx == 0, jnp.concatenate([mine[-1], theirs[-1]]), jnp.concatenate([theirs[-1], mine[-1]]))
    rep_all = _gather_chips(small[:, rows[0]:])
    gs = _unpack_small_reduced(small, rep_all, rows, w)
    for n in WEIGHTS:
        if n in g:
            continue
        g[n] = gs[n].reshape(w[n].shape)
        shape = (1,) + w[n].shape if w[n].ndim == 1 else w[n].shape
        d2, m2, v2 = _adamw(w[n].reshape(shape), g[n].reshape(shape), m[n].reshape(shape), v[n].reshape(shape), "adamw_" + n)
        deltas[n], new_m[n], new_v[n] = d2.reshape(w[n].shape), m2.reshape(w[n].shape), v2.reshape(w[n].shape)
    return (loss, grad_x, *[g[n] for n in WEIGHTS], *[deltas[n] for n in WEIGHTS], *[new_m[n] for n in WEIGHTS],
            *[new_v[n] for n in WEIGHTS])
```

```python
import functools
import math

import jax
import jax.numpy as jnp
from jax import lax
from jax.experimental import pallas as pl
from jax.experimental.pallas import tpu as pltpu

F32 = jnp.float32
BF16 = jnp.bfloat16

D_MODEL = 1024
DN_HEADS = 4
DN_HEAD_DIM = 128
DN_WIDTH = 512
LRU_WIDTH = 512
LRU_C = 8.0
CHUNK = 64
EPS = 1e-6
P_LX, P_LG, P_Q, P_K, P_V, P_Z, P_GATE, P_COLS = 0, 512, 1024, 1536, 2048, 2560, 3072, 3200
N_GATE = 16
LANES = 128
VMEM_LIMIT = 56 * 1024 * 1024
MM_VMEM_BYTES = 40 * 1024 * 1024

ADAM_LR, ADAM_B1, ADAM_B2, ADAM_EPS, ADAM_WD, ADAM_STEP = 0.001, 0.9, 0.999, 1e-08, 0.01, 10


def _cparams(sem):
    return pltpu.CompilerParams(dimension_semantics=sem, vmem_limit_bytes=VMEM_LIMIT)


def _row_tile(m, want=512):
    for t in range(min(want, m) // 16 * 16, 0, -16):
        if m % t == 0:
            return t
    return m


def _lane_divisors(n):
    out = [d for d in range(n, 0, -LANES) if d % LANES == 0 and n % d == 0] if n % LANES == 0 else []
    return out or [n]


def _mm_tiles(m, n, a_row_bytes, k_total, with_res):
    best = None
    for tm in (1024, 512, 256, 128):
        if m % tm:
            continue
        for tn in _lane_divisors(n):
            need = 2 * (tm * a_row_bytes + k_total * tn * 2 + tm * tn * 4 * (2 if with_res else 1))
            if need <= MM_VMEM_BYTES and (best is None or tm * tn > best[0] * best[1]):
                best = (tm, tn)
    return best if best is not None else (_row_tile(m, 128), _lane_divisors(n)[-1])


def _mm_tn_tiles(m, k, n):
    for tn in _lane_divisors(n):
        for tm in (1024, 512, 256, 128):
            if m % tm == 0 and 2 * (k * tn * 4 + tm * k * 2 + tm * tn * 2) <= MM_VMEM_BYTES:
                return tm, tn
    return _row_tile(m, 128), _lane_divisors(n)[-1]


def _bdot(a, b):
    return jnp.dot(a.astype(BF16), b.astype(BF16), preferred_element_type=F32)


def _bdot_nt(a, b):
    return lax.dot_general(a.astype(BF16), b.astype(BF16), (((1,), (1,)), ((), ())), preferred_element_type=F32)


def _bdot_tn(a, b):
    return lax.dot_general(a.astype(BF16), b.astype(BF16), (((0,), (0,)), ((), ())), preferred_element_type=F32)


def _rms(x, g):
    return x * lax.rsqrt(jnp.mean(x * x, axis=-1, keepdims=True) + EPS) * g


def _gelu(x):
    return 0.5 * x * (1.0 + jnp.tanh(0.7978845608028654 * (x + 0.044715 * x * x * x)))


def _sigmoid(x):
    return 0.5 * (jnp.tanh(0.5 * x) + 1.0)


def _silu(x):
    return x * _sigmoid(x)


def _softplus(x):
    return jnp.maximum(x, 0.0) + jnp.log(1.0 + jnp.exp(-jnp.abs(x)))


def _matmul(terms, name, res=None, nt=False):
    norm = lambda op, axis: op if isinstance(op, tuple) else (op, op.shape[axis], 0)
    a_ops = [norm(a, 1) for a, _ in terms]
    b_ops = [norm(b, 1 if nt else 0) for _, b in terms]
    m = a_ops[0][0].shape[0]
    n = b_ops[0][0].shape[0 if nt else 1]
    a_row_bytes = sum(kw * a.dtype.itemsize for a, kw, _ in a_ops)
    tm, tn = _mm_tiles(m, n, a_row_bytes, sum(kw for _, kw, _ in b_ops), res is not None)
    na = len(terms)
    dot = _bdot_nt if nt else _bdot

    def body(*refs):
        a_refs, b_refs = refs[:na], refs[na:2 * na]
        acc = dot(a_refs[0][...], b_refs[0][...])
        for a_ref, b_ref in zip(a_refs[1:], b_refs[1:]):
            acc = acc + dot(a_ref[...], b_ref[...])
        if res is not None:
            acc = acc + refs[2 * na][...]
        refs[-1][...] = acc

    in_specs = [pl.BlockSpec((tm, kw), functools.partial(lambda i, j, kb: (i, kb), kb=kb)) for _, kw, kb in a_ops]
    if nt:
        in_specs += [pl.BlockSpec((tn, kw), functools.partial(lambda i, j, kb: (j, kb), kb=kb)) for _, kw, kb in b_ops]
    else:
        in_specs += [pl.BlockSpec((kw, tn), functools.partial(lambda i, j, kb: (kb, j), kb=kb)) for _, kw, kb in b_ops]
    args = [a for a, _, _ in a_ops] + [b for b, _, _ in b_ops]
    if res is not None:
        in_specs.append(pl.BlockSpec((tm, tn), lambda i, j: (i, j)))
        args.append(res)
    return pl.pallas_call(
        body, name=name, grid=(m // tm, n // tn), in_specs=in_specs,
        out_specs=pl.BlockSpec((tm, tn), lambda i, j: (i, j)),
        out_shape=jax.ShapeDtypeStruct((m, n), F32),
        compiler_params=_cparams(("parallel", "parallel")),
    )(*args)


def _norm_matmul(x, g, b_list, name):
    m, k = x.shape
    b_bytes = sum(2 * b.shape[0] * b.shape[1] * 2 for b in b_list)
    row_bytes = 2 * (k * 4 + k * 2 + sum(b.shape[1] * 4 for b in b_list))
    tm = next((t for t in (1024, 512, 256, 128) if m % t == 0 and b_bytes + t * row_bytes <= MM_VMEM_BYTES), _row_tile(m, 128))
    nb = len(b_list)

    def body(x_ref, g_ref, *refs):
        b_refs, h_ref, o_refs = refs[:nb], refs[nb], refs[nb + 1:]
        h = _rms(x_ref[...], g_ref[...]).astype(BF16)
        h_ref[...] = h
        for b_ref, o_ref in zip(b_refs, o_refs):
            o_ref[...] = jnp.dot(h, b_ref[...], preferred_element_type=F32)

    row = lambda width: pl.BlockSpec((tm, width), lambda i: (i, 0))
    return pl.pallas_call(
        body, name=name, grid=(m // tm,),
        in_specs=[row(k), pl.BlockSpec((1, k), lambda i: (0, 0))] + [pl.BlockSpec(b.shape, lambda i: (0, 0)) for b in b_list],
        out_specs=[row(k)] + [row(b.shape[1]) for b in b_list],
        out_shape=[jax.ShapeDtypeStruct((m, k), BF16)] + [jax.ShapeDtypeStruct((m, b.shape[1]), F32) for b in b_list],
        compiler_params=_cparams(("parallel",)),
    )(x, g, *b_list)


def _matmul_norm_bwd(terms, x, g, dres, name):
    norm = lambda op: op if isinstance(op, tuple) else (op, op.shape[1], 0)
    a_ops = [norm(a) for a, _ in terms]
    b_ops = [norm(b) for _, b in terms]
    m, n = x.shape
    b_bytes = sum(2 * n * kw * 2 for _, kw, _ in b_ops)
    row_bytes = 2 * (sum(kw * 2 for _, kw, _ in a_ops) + n * (4 + 4 + 4 + 2)) + n * 8
    tm = next((t for t in (1024, 512, 256, 128) if m % t == 0 and b_bytes + t * row_bytes <= MM_VMEM_BYTES), _row_tile(m, 128))
    na = len(terms)

    def body(*refs):
        a_refs, b_refs = refs[:na], refs[na:2 * na]
        x_ref, g_ref, dres_ref, dx_ref, dxb_ref, dg_ref = refs[2 * na:]
        dh = _bdot_nt(a_refs[0][...], b_refs[0][...])
        for a_ref, b_ref in zip(a_refs[1:], b_refs[1:]):
            dh = dh + _bdot_nt(a_ref[...], b_ref[...])
        _, vjp = jax.vjp(_rms, x_ref[...], g_ref[...])
        dx, dg = vjp(dh)
        dx = dx + dres_ref[...]
        dx_ref[...] = dx
        dxb_ref[...] = dx.astype(BF16)

        @pl.when(pl.program_id(0) == 0)
        def _():
            dg_ref[...] = jnp.zeros_like(dg_ref)

        dg_ref[...] += dg

    row = pl.BlockSpec((tm, n), lambda i: (i, 0))
    vec = pl.BlockSpec((1, n), lambda i: (0, 0))
    in_specs = [pl.BlockSpec((tm, kw), functools.partial(lambda i, kb: (i, kb), kb=kb)) for _, kw, kb in a_ops]
    in_specs += [pl.BlockSpec((n, kw), functools.partial(lambda i, kb: (0, kb), kb=kb)) for _, kw, kb in b_ops]
    return pl.pallas_call(
        body, name=name, grid=(m // tm,), in_specs=in_specs + [row, vec, row], out_specs=[row, row, vec],
        out_shape=[jax.ShapeDtypeStruct((m, n), F32), jax.ShapeDtypeStruct((m, n), BF16), jax.ShapeDtypeStruct((1, n), F32)],
        compiler_params=_cparams(("arbitrary",)),
    )(*[a for a, _, _ in a_ops], *[b for b, _, _ in b_ops], x, g, dres)


def _matmul_tn_cols(a, b_list, name):
    m, k = a.shape
    n = sum(b.shape[1] for b in b_list)
    tm = next((t for t in (1024, 512, 256, 128) if m % t == 0 and 2 * (k * n * 4 + t * k * 2 + t * n * 2) <= MM_VMEM_BYTES),
              _row_tile(m, 128))

    def body(a_ref, *refs):
        b_refs, o_ref = refs[:-1], refs[-1]

        @pl.when(pl.program_id(0) == 0)
        def _():
            o_ref[...] = jnp.zeros_like(o_ref)

        off = 0
        for b_ref in b_refs:
            w = b_ref.shape[1]
            o_ref[:, off:off + w] += _bdot_tn(a_ref[...], b_ref[...])
            off += w

    return pl.pallas_call(
        body, name=name, grid=(m // tm,),
        in_specs=[pl.BlockSpec((tm, k), lambda i: (i, 0))] + [pl.BlockSpec((tm, b.shape[1]), lambda i: (i, 0)) for b in b_list],
        out_specs=pl.BlockSpec((k, n), lambda i: (0, 0)), out_shape=jax.ShapeDtypeStruct((k, n), F32),
        compiler_params=_cparams(("arbitrary",)),
    )(a, *b_list)


def _matmul_tn(a, b, name):
    m, k = a.shape
    n = b.shape[1]
    tm, tn = _mm_tn_tiles(m, k, n)

    def body(a_ref, b_ref, o_ref):
        @pl.when(pl.program_id(1) == 0)
        def _():
            o_ref[...] = jnp.zeros_like(o_ref)

        o_ref[...] += _bdot_tn(a_ref[...], b_ref[...])

    return pl.pallas_call(
        body, name=name, grid=(n // tn, m // tm),
        in_specs=[pl.BlockSpec((tm, k), lambda j, i: (i, 0)), pl.BlockSpec((tm, tn), lambda j, i: (i, j))],
        out_specs=pl.BlockSpec((k, tn), lambda j, i: (0, j)),
        out_shape=jax.ShapeDtypeStruct((k, n), F32),
        compiler_params=_cparams(("parallel", "arbitrary")),
    )(a, b)


def _shift_down(x, k):
    row = lax.broadcasted_iota(jnp.int32, x.shape, 0)
    return jnp.where(row >= k, pltpu.roll(x, k, 0), 0.0)


def _shift_up(x, k):
    s = x.shape[0]
    row = lax.broadcasted_iota(jnp.int32, x.shape, 0)
    return jnp.where(row < s - k, pltpu.roll(x, s - k, 0), 0.0)


def _conv_taps(x, ntaps, left):
    out = []
    for j in range(ntaps):
        off = j - left
        out.append(_shift_down(x, -off) if off < 0 else (_shift_up(x, off) if off > 0 else x))
    return out


def _conv_fwd(x, w, left):
    taps = _conv_taps(x, w.shape[0], left)
    acc = taps[0] * w[0:1, :]
    for j in range(1, w.shape[0]):
        acc = acc + taps[j] * w[j:j + 1, :]
    return acc


def _conv_bwd(x, w, left, dout):
    ntaps = w.shape[0]
    dx = None
    for j in range(ntaps):
        off = j - left
        sh = _shift_up(dout, -off) if off < 0 else (_shift_down(dout, off) if off > 0 else dout)
        term = sh * w[j:j + 1, :]
        dx = term if dx is None else dx + term
    taps = _conv_taps(x, ntaps, left)
    dw = jnp.concatenate([jnp.sum(dout * tp, axis=0, keepdims=True) for tp in taps], axis=0)
    return dx, dw


SCAN_BLOCK = 32


def _scan(a, b, reverse):
    s = a.shape[0]
    blk = SCAN_BLOCK if s % SCAN_BLOCK == 0 else s
    row = lax.broadcasted_iota(jnp.int32, a.shape, 0)
    pos = row & (blk - 1) if blk & (blk - 1) == 0 else row % blk
    d = 1
    while d < blk:
        if reverse:
            keep = pos < blk - d
            sb, sa = pltpu.roll(b, s - d, 0), pltpu.roll(a, s - d, 0)
        else:
            keep = pos >= d
            sb, sa = pltpu.roll(b, d, 0), pltpu.roll(a, d, 0)
        b = a * jnp.where(keep, sb, 0.0) + b
        a = a * jnp.where(keep, sa, 1.0)
        d *= 2
    nblk = s // blk
    if nblk == 1:
        return b
    carry = jnp.zeros((1, a.shape[1]), F32)
    carries = [None] * nblk
    order = range(nblk - 1, -1, -1) if reverse else range(nblk)
    for j in order:
        carries[j] = carry
        last = j * blk if reverse else (j + 1) * blk - 1
        carry = b[last:last + 1, :] + a[last:last + 1, :] * carry
    carry_in = jnp.concatenate([jnp.broadcast_to(cj, (blk, a.shape[1])) for cj in carries], axis=0)
    return b + a * carry_in


def _dn_gates_fn(pre, alog, dtb):
    lane = lax.broadcasted_iota(jnp.int32, pre.shape, 1)
    beta = _sigmoid(pre)
    g = -jnp.exp(alog) * _softplus(pre + dtb)
    return jnp.where(lane < N_GATE // 2, beta, jnp.where(lane < N_GATE, g, 0.0))


def _dn_gates_fwd(proj, alog, dtb, name):
    t = proj.shape[0]
    tm = _row_tile(t)
    cb = P_GATE // LANES

    def body(p_ref, a_ref, d_ref, o_ref):
        o_ref[...] = _dn_gates_fn(p_ref[...], a_ref[...], d_ref[...])

    return pl.pallas_call(
        body, name=name, grid=(t // tm,),
        in_specs=[pl.BlockSpec((tm, LANES), lambda i: (i, cb)), pl.BlockSpec((1, LANES), lambda i: (0, 0)),
                  pl.BlockSpec((1, LANES), lambda i: (0, 0))],
        out_specs=pl.BlockSpec((tm, LANES), lambda i: (i, 0)),
        out_shape=jax.ShapeDtypeStruct((t, LANES), F32),
        compiler_params=_cparams(("parallel",)),
    )(proj, alog, dtb)


def _dn_gates_bwd(proj, alog, dtb, dgb_f, dgb_b, name):
    t = proj.shape[0]
    tm = _row_tile(t)
    cb = P_GATE // LANES

    def body(p_ref, a_ref, d_ref, g1_ref, g2_ref, dp_ref, da_ref, dd_ref):
        _, vjp = jax.vjp(_dn_gates_fn, p_ref[...], a_ref[...], d_ref[...])
        dp, da, dd = vjp(g1_ref[...] + g2_ref[...])
        dp_ref[...] = dp.astype(dp_ref.dtype)

        @pl.when(pl.program_id(0) == 0)
        def _():
            da_ref[...] = jnp.zeros_like(da_ref)
            dd_ref[...] = jnp.zeros_like(dd_ref)

        da_ref[...] += da
        dd_ref[...] += dd

    row = pl.BlockSpec((tm, LANES), lambda i: (i, 0))
    vec = pl.BlockSpec((1, LANES), lambda i: (0, 0))
    return pl.pallas_call(
        body, name=name, grid=(t // tm,),
        in_specs=[pl.BlockSpec((tm, LANES), lambda i: (i, cb)), vec, vec, row, row],
        out_specs=[row, vec, vec],
        out_shape=[jax.ShapeDtypeStruct((t, LANES), BF16), jax.ShapeDtypeStruct((1, LANES), F32),
                   jax.ShapeDtypeStruct((1, LANES), F32)],
        compiler_params=_cparams(("arbitrary",)),
    )(proj, alog, dtb, dgb_f, dgb_b)


def _dn_prep_fn(x, w, is_qk):
    act = _silu(_conv_fwd(x, w, 2))
    nrm = act * lax.rsqrt(jnp.sum(act * act, axis=-1, keepdims=True) + EPS)
    return jnp.where(is_qk, nrm, act)


def _dn_prep_fwd(proj3, conv_w, name):
    bsz, s, _ = proj3.shape
    nblk = 3 * DN_WIDTH // LANES
    cb = P_Q // LANES

    def body(x_ref, w_ref, o_ref):
        o_ref[0] = _dn_prep_fn(x_ref[0], w_ref[...], pl.program_id(1) < 2 * DN_HEADS)

    return pl.pallas_call(
        body, name=name, grid=(bsz, nblk),
        in_specs=[pl.BlockSpec((1, s, LANES), lambda b, j: (b, 0, cb + j)), pl.BlockSpec((4, LANES), lambda b, j: (0, j))],
        out_specs=pl.BlockSpec((1, s, LANES), lambda b, j: (b, 0, j)),
        out_shape=jax.ShapeDtypeStruct((bsz, s, 3 * DN_WIDTH), F32),
        compiler_params=_cparams(("parallel", "parallel")),
    )(proj3, conv_w)


def _dn_prep_bwd(proj3, conv_w, dqkv_f, dqkv_b, name):
    bsz, s, _ = proj3.shape
    nblk = 3 * DN_WIDTH // LANES
    cb = P_Q // LANES

    def body(x_ref, w_ref, d1_ref, d2_ref, dx_ref, dw_ref):
        x, w, d = x_ref[0], w_ref[...], d1_ref[0] + d2_ref[0]
        is_qk = pl.program_id(0) < 2 * DN_HEADS
        pre = _conv_fwd(x, w, 2)

        def post(pre):
            act = _silu(pre)
            nrm = act * lax.rsqrt(jnp.sum(act * act, axis=-1, keepdims=True) + EPS)
            return jnp.where(is_qk, nrm, act)

        _, vjp = jax.vjp(post, pre)
        (dpre,) = vjp(d)
        dx, dw = _conv_bwd(x, w, 2, dpre)
        dx_ref[0] = dx.astype(dx_ref.dtype)

        @pl.when(pl.program_id(1) == 0)
        def _():
            dw_ref[...] = jnp.zeros_like(dw_ref)

        dw_ref[...] += dw

    col = pl.BlockSpec((1, s, LANES), lambda j, b: (b, 0, j))
    return pl.pallas_call(
        body, name=name, grid=(nblk, bsz),
        in_specs=[pl.BlockSpec((1, s, LANES), lambda j, b: (b, 0, cb + j)), pl.BlockSpec((4, LANES), lambda j, b: (0, j)), col, col],
        out_specs=[col, pl.BlockSpec((4, LANES), lambda j, b: (0, j))],
        out_shape=[jax.ShapeDtypeStruct((bsz, s, 3 * DN_WIDTH), BF16), jax.ShapeDtypeStruct((4, 3 * DN_WIDTH), F32)],
        compiler_params=_cparams(("parallel", "arbitrary")),
    )(proj3, conv_w, dqkv_f, dqkv_b)


def _parts(x, n):
    out = []
    for _ in range(n):
        bits = lax.bitcast_convert_type(x, jnp.uint32) & jnp.uint32(0xFFFF0000)
        t = lax.bitcast_convert_type(bits, F32)
        out.append(t.astype(BF16))
        x = x - t
    return out


def _dg(x, y, cx, cy):
    return lax.dot_general(x, y, (((cx + 1,), (cy + 1,)), ((0,), (0,))), preferred_element_type=F32)


def _bmm(a, b):
    return _dg(a.astype(BF16), b.astype(BF16), 1, 0)


def _bmm_nt(a, b):
    return _dg(a.astype(BF16), b.astype(BF16), 1, 1)


def _bmm_tn(a, b):
    return _dg(a.astype(BF16), b.astype(BF16), 0, 0)


def _dot3_raw(a, b, ca, cb):
    a_hi, a_lo = _parts(a, 2)
    b_hi, b_lo = _parts(b, 2)
    return _dg(a_hi, b_hi, ca, cb) + (_dg(a_hi, b_lo, ca, cb) + _dg(a_lo, b_hi, ca, cb))


@jax.custom_vjp
def _sum_left(m, x):
    return sum(_dg(m, pt, 1, 0) for pt in _parts(x, 3))


def _sum_left_fwd(m, x):
    return _sum_left(m, x), m


def _sum_left_bwd(m, ct):
    return jnp.zeros_like(m), sum(_dg(m, pt, 0, 0) for pt in _parts(ct, 2))


_sum_left.defvjp(_sum_left_fwd, _sum_left_bwd)


@jax.custom_vjp
def _sum_right(x, m):
    return sum(_dg(pt, m, 0, 0) for pt in _parts(x, 3))


def _sum_right_fwd(x, m):
    return _sum_right(x, m), m


def _sum_right_bwd(m, ct):
    return sum(_dg(m, pt, 1, 1) for pt in _parts(ct, 2)), jnp.zeros_like(m)


_sum_right.defvjp(_sum_right_fwd, _sum_right_bwd)


def _unit_tri_inverse_raw(a):
    nu, c, _ = a.shape
    row = lax.broadcasted_iota(jnp.int32, (nu, c, c), 1)
    col = lax.broadcasted_iota(jnp.int32, (nu, c, c), 2)
    same = lambda n: (row // n) == (col // n)
    d = jnp.where(same(8), a, 0.0)
    tinv = jnp.where(row == col, 1.0, 0.0) - d
    pw = _dot3_raw(d, d, 1, 0)
    tinv = tinv + _dot3_raw(tinv, pw, 1, 0)
    pw = _dot3_raw(pw, pw, 1, 0)
    tinv = tinv + _dot3_raw(tinv, pw, 1, 0)
    n = 8
    while n < c:
        e = jnp.where(same(2 * n) & jnp.logical_not(same(n)), a, 0.0)
        tinv = tinv - _bmm(tinv, _bmm(e, tinv))
        n *= 2
    return tinv


@jax.custom_vjp
def _unit_tri_inverse(a):
    return _unit_tri_inverse_raw(a)


def _unit_tri_inverse_fwd(a):
    tinv = _unit_tri_inverse_raw(a)
    return tinv, tinv


def _unit_tri_inverse_bwd(tinv, ct):
    return (-_bmm_nt(_bmm_tn(tinv, ct), tinv),)


_unit_tri_inverse.defvjp(_unit_tri_inverse_fwd, _unit_tri_inverse_bwd)


def _dn_intra(q, k, v, g, beta, rev):
    nu, c, _ = q.shape
    row = lax.broadcasted_iota(jnp.int32, (nu, c, c), 1)
    col = lax.broadcasted_iota(jnp.int32, (nu, c, c), 2)
    incl = (row <= col) if rev else (row >= col)
    strict = (row < col) if rev else (row > col)
    ones_incl = jnp.where(incl, 1.0, 0.0).astype(BF16)
    ones_tr = jnp.where((row >= col) if rev else (row <= col), 1.0, 0.0).astype(BF16)
    gbc = jnp.broadcast_to(g, (nu, c, c))
    gc = _sum_left(ones_incl, gbc)
    gr = _sum_right(gbc, ones_tr)
    gcum = gc[:, :, 0:1]
    decay = jnp.where(incl, jnp.exp(jnp.where(incl, gc - gr, 0.0)), 0.0)
    qs = q * (DN_HEAD_DIM ** -0.5)
    kb = k * beta
    a = jnp.where(strict, _bmm_nt(kb, k) * decay, 0.0)
    tinv = _unit_tri_inverse(a)
    egc = jnp.exp(gcum)
    u = _bmm(tinv, v * beta)
    w = _bmm(tinv, kb * egc)
    attn = _bmm_nt(qs, k) * decay
    glast = gcum[:, 0:1, :] if rev else gcum[:, c - 1:c, :]
    q_dec = qs * egc
    k_dec = k * jnp.exp(glast - gcum)
    cdec = jnp.broadcast_to(jnp.exp(glast), (nu, 1, LANES))
    return u, w, q_dec, k_dec, attn, cdec


def _dn_rec(u, w, q_dec, k_dec, attn, cdec, state):
    v_new = u - _bmm(w, state)
    o = _bmm(q_dec, state) + _bmm(attn, v_new)
    return o, state * cdec + _bmm_tn(k_dec, v_new)


DN_INTRA_TOKENS = 512
DN_REC_TOKENS = 512
DN_REC_BWD_TOKENS = 256
DN_REC_EXAMPLES = 4


def _dn_gate_lanes(rev, h):
    lb = (DN_HEADS if rev else 0) + h
    return lb, N_GATE // 2 + lb


def _dn_intra_shapes(bsz, s):
    n = s // CHUNK
    return [jax.ShapeDtypeStruct((bsz, s, DN_WIDTH), F32), jax.ShapeDtypeStruct((bsz, s, DN_WIDTH), BF16),
            jax.ShapeDtypeStruct((bsz, s, DN_WIDTH), BF16), jax.ShapeDtypeStruct((bsz, s, DN_WIDTH), BF16),
            jax.ShapeDtypeStruct((bsz, n, DN_HEADS, CHUNK, CHUNK), BF16), jax.ShapeDtypeStruct((bsz, n, DN_HEADS, 1, LANES), F32)]


def _dn_intra_specs(tb, ix):
    nc = tb // CHUNK
    ix5 = lambda b, j: ix(b, j) + (0, 0)
    row = pl.BlockSpec((1, tb, DN_WIDTH), ix)
    return [row, row, row, row, pl.BlockSpec((1, nc, DN_HEADS, CHUNK, CHUNK), ix5), pl.BlockSpec((1, nc, DN_HEADS, 1, LANES), ix5)]


def _dn_units(nc):
    return [(ci, h) for ci in range(nc) for h in range(DN_HEADS)]


def _dn_load_units(qkv_ref, gb_ref, nc, rev):
    qs, ks, vs, gs, bs = [], [], [], [], []
    for ci, h in _dn_units(nc):
        rows = slice(ci * CHUNK, (ci + 1) * CHUNK)
        lb, lg = _dn_gate_lanes(rev, h)
        qs.append(qkv_ref[0, rows, h * LANES:(h + 1) * LANES])
        ks.append(qkv_ref[0, rows, DN_WIDTH + h * LANES:DN_WIDTH + (h + 1) * LANES])
        vs.append(qkv_ref[0, rows, 2 * DN_WIDTH + h * LANES:2 * DN_WIDTH + (h + 1) * LANES])
        gs.append(gb_ref[0, rows, lg:lg + 1])
        bs.append(gb_ref[0, rows, lb:lb + 1])
    return jnp.stack(qs), jnp.stack(ks), jnp.stack(vs), jnp.stack(gs), jnp.stack(bs)


def _dn_intra_fwd(qkv, gb, rev, name):
    bsz, s, _ = qkv.shape
    tb = min(DN_INTRA_TOKENS, s)
    nc = tb // CHUNK

    def body(qkv_ref, gb_ref, u_ref, w_ref, qd_ref, kd_ref, at_ref, cd_ref):
        q, k, v, g, beta = _dn_load_units(qkv_ref, gb_ref, nc, rev)
        u, w, qd, kd, at, cd = _dn_intra(q, k, v, g, beta, rev)
        for i, (ci, h) in enumerate(_dn_units(nc)):
            rows = slice(ci * CHUNK, (ci + 1) * CHUNK)
            cols = slice(h * LANES, (h + 1) * LANES)
            u_ref[0, rows, cols] = u[i]
            w_ref[0, rows, cols] = w[i].astype(BF16)
            qd_ref[0, rows, cols] = qd[i].astype(BF16)
            kd_ref[0, rows, cols] = kd[i].astype(BF16)
            at_ref[0, ci, h] = at[i].astype(BF16)
            cd_ref[0, ci, h] = cd[i]

    ix = lambda b, j: (b, j, 0)
    return pl.pallas_call(
        body, name=name, grid=(bsz, s // tb),
        in_specs=[pl.BlockSpec((1, tb, 3 * DN_WIDTH), ix), pl.BlockSpec((1, tb, LANES), ix)],
        out_specs=_dn_intra_specs(tb, ix), out_shape=_dn_intra_shapes(bsz, s),
        compiler_params=_cparams(("parallel", "parallel")),
    )(qkv, gb)


def _dn_intra_bwd(qkv, gb, cts, rev, name):
    bsz, s, _ = qkv.shape
    tb = min(DN_INTRA_TOKENS, s)
    nc = tb // CHUNK

    def body(qkv_ref, gb_ref, du_ref, dw_ref, dqd_ref, dkd_ref, dat_ref, dcd_ref, dqkv_ref, dgb_ref):
        units = _dn_units(nc)
        q, k, v, g, beta = _dn_load_units(qkv_ref, gb_ref, nc, rev)
        _, vjp = jax.vjp(functools.partial(_dn_intra, rev=rev), q, k, v, g, beta)
        tok = lambda ref: jnp.stack([ref[0, ci * CHUNK:(ci + 1) * CHUNK, h * LANES:(h + 1) * LANES] for ci, h in units])
        per = lambda ref: jnp.stack([ref[0, ci, h] for ci, h in units])
        dq, dk, dv, dg, dbeta = vjp((tok(du_ref), tok(dw_ref), tok(dqd_ref), tok(dkd_ref), per(dat_ref), per(dcd_ref)))
        lane = lax.broadcasted_iota(jnp.int32, (CHUNK, LANES), 1)
        for ci in range(nc):
            rows = slice(ci * CHUNK, (ci + 1) * CHUNK)
            dgates = jnp.zeros((CHUNK, LANES), F32)
            for h in range(DN_HEADS):
                i = units.index((ci, h))
                lb, lg = _dn_gate_lanes(rev, h)
                dqkv_ref[0, rows, h * LANES:(h + 1) * LANES] = dq[i]
                dqkv_ref[0, rows, DN_WIDTH + h * LANES:DN_WIDTH + (h + 1) * LANES] = dk[i]
                dqkv_ref[0, rows, 2 * DN_WIDTH + h * LANES:2 * DN_WIDTH + (h + 1) * LANES] = dv[i]
                dgates = dgates + jnp.where(lane == lb, dbeta[i], 0.0) + jnp.where(lane == lg, dg[i], 0.0)
            dgb_ref[0, rows, :] = dgates

    ix = lambda b, j: (b, j, 0)
    ix5 = lambda b, j: (b, j, 0, 0, 0)
    row = pl.BlockSpec((1, tb, DN_WIDTH), ix)
    return pl.pallas_call(
        body, name=name, grid=(bsz, s // tb),
        in_specs=[pl.BlockSpec((1, tb, 3 * DN_WIDTH), ix), pl.BlockSpec((1, tb, LANES), ix), row, row, row, row,
                  pl.BlockSpec((1, nc, DN_HEADS, CHUNK, CHUNK), ix5), pl.BlockSpec((1, nc, DN_HEADS, 1, LANES), ix5)],
        out_specs=[pl.BlockSpec((1, tb, 3 * DN_WIDTH), ix), pl.BlockSpec((1, tb, LANES), ix)],
        out_shape=[jax.ShapeDtypeStruct((bsz, s, 3 * DN_WIDTH), F32), jax.ShapeDtypeStruct((bsz, s, LANES), F32)],
        compiler_params=_cparams(("parallel", "parallel")),
    )(qkv, gb, *cts)


def _dn_rec_examples(bsz):
    return max(n for n in range(1, DN_REC_EXAMPLES + 1) if bsz % n == 0)


def _dn_rec_specs(nb, tb, ix):
    nc = tb // CHUNK
    ix5 = lambda b, j: ix(b, j) + (0, 0)
    row = pl.BlockSpec((nb, tb, DN_WIDTH), ix)
    return [row, row, row, row, pl.BlockSpec((nb, nc, DN_HEADS, CHUNK, CHUNK), ix5), pl.BlockSpec((nb, nc, DN_HEADS, 1, LANES), ix5)]


def _dn_rec_fwd(intra, rev, name):
    u = intra[0]
    bsz, s, _ = u.shape
    nb = _dn_rec_examples(bsz)
    tb = min(DN_REC_TOKENS, s)
    nt = s // tb
    nc = tb // CHUNK
    nh = nb * DN_HEADS

    def body(u_ref, w_ref, qd_ref, kd_ref, at_ref, cd_ref, o_ref, st_ref, state):
        @pl.when(pl.program_id(1) == 0)
        def _():
            state[...] = jnp.zeros_like(state)

        def step(ci, carry):
            cidx = (nc - 1 - ci) if rev else ci
            rows = pl.ds(pl.multiple_of(cidx * CHUNK, CHUNK), CHUNK)
            heads = lambda ref: jnp.stack([ref[e, rows, h * LANES:(h + 1) * LANES] for e in range(nb) for h in range(DN_HEADS)])
            per = lambda ref: jnp.concatenate([ref[e, cidx] for e in range(nb)])
            st = state[...]
            o, new_state = _dn_rec(heads(u_ref), heads(w_ref), heads(qd_ref), heads(kd_ref), per(at_ref), per(cd_ref), st)
            state[...] = new_state
            for e in range(nb):
                st_ref[e, cidx] = st[e * DN_HEADS:(e + 1) * DN_HEADS]
                o_ref[e, rows, :] = jnp.concatenate([o[e * DN_HEADS + h] for h in range(DN_HEADS)], axis=-1)
            return carry

        lax.fori_loop(0, nc, step, 0)

    ix = (lambda b, j: (b, nt - 1 - j, 0)) if rev else (lambda b, j: (b, j, 0))
    ix5 = lambda b, j: ix(b, j) + (0, 0)
    return pl.pallas_call(
        body, name=name, grid=(bsz // nb, nt), in_specs=_dn_rec_specs(nb, tb, ix),
        out_specs=[pl.BlockSpec((nb, tb, DN_WIDTH), ix), pl.BlockSpec((nb, nc, DN_HEADS, DN_HEAD_DIM, DN_HEAD_DIM), ix5)],
        out_shape=[jax.ShapeDtypeStruct((bsz, s, DN_WIDTH), F32),
                   jax.ShapeDtypeStruct((bsz, s // CHUNK, DN_HEADS, DN_HEAD_DIM, DN_HEAD_DIM), F32)],
        scratch_shapes=[pltpu.VMEM((nh, DN_HEAD_DIM, DN_HEAD_DIM), F32)],
        compiler_params=_cparams(("parallel", "arbitrary")),
    )(*intra)


def _dn_rec_bwd(intra, states, do, rev, name):
    u = intra[0]
    bsz, s, _ = u.shape
    nb = _dn_rec_examples(bsz)
    tb = min(DN_REC_BWD_TOKENS, s)
    nt = s // tb
    nc = tb // CHUNK
    nh = nb * DN_HEADS

    def body(u_ref, w_ref, qd_ref, kd_ref, at_ref, cd_ref, st_ref, do_ref,
             du_ref, dw_ref, dqd_ref, dkd_ref, dat_ref, dcd_ref, dstate):
        @pl.when(pl.program_id(1) == 0)
        def _():
            dstate[...] = jnp.zeros_like(dstate)

        def step(ci, carry):
            cidx = ci if rev else (nc - 1 - ci)
            rows = pl.ds(pl.multiple_of(cidx * CHUNK, CHUNK), CHUNK)
            heads = lambda ref: jnp.stack([ref[e, rows, h * LANES:(h + 1) * LANES] for e in range(nb) for h in range(DN_HEADS)])
            per = lambda ref: jnp.concatenate([ref[e, cidx] for e in range(nb)])
            args = (heads(u_ref), heads(w_ref).astype(F32), heads(qd_ref).astype(F32), heads(kd_ref).astype(F32),
                    per(at_ref).astype(F32), per(cd_ref), per(st_ref))
            _, vjp = jax.vjp(_dn_rec, *args)
            du, dw, dqd, dkd, dat, dcd, dst = vjp((heads(do_ref), dstate[...]))
            dstate[...] = dst
            for e in range(nb):
                hs = slice(e * DN_HEADS, (e + 1) * DN_HEADS)
                dat_ref[e, cidx] = dat[hs]
                dcd_ref[e, cidx] = dcd[hs]
                for ref, val in ((du_ref, du), (dw_ref, dw), (dqd_ref, dqd), (dkd_ref, dkd)):
                    ref[e, rows, :] = jnp.concatenate([val[e * DN_HEADS + h] for h in range(DN_HEADS)], axis=-1)
            return carry

        lax.fori_loop(0, nc, step, 0)

    ix = (lambda b, j: (b, j, 0)) if rev else (lambda b, j: (b, nt - 1 - j, 0))
    ix5 = lambda b, j: ix(b, j) + (0, 0)
    row = pl.BlockSpec((nb, tb, DN_WIDTH), ix)
    f32 = lambda sd: jax.ShapeDtypeStruct(sd.shape, F32)
    return pl.pallas_call(
        body, name=name, grid=(bsz // nb, nt),
        in_specs=_dn_rec_specs(nb, tb, ix) + [pl.BlockSpec((nb, nc, DN_HEADS, DN_HEAD_DIM, DN_HEAD_DIM), ix5), row],
        out_specs=_dn_rec_specs(nb, tb, ix), out_shape=[f32(sd) for sd in _dn_intra_shapes(bsz, s)],
        scratch_shapes=[pltpu.VMEM((nh, DN_HEAD_DIM, DN_HEAD_DIM), F32)],
        compiler_params=_cparams(("parallel", "arbitrary")),
    )(*intra, states, do)


def _lru_gate_fn(xc, wa, ba, wx, bx, lam):
    r = _sigmoid(_bdot(xc, wa) + ba)
    ig = _sigmoid(_bdot(xc, wx) + bx)
    log_a = -LRU_C * r * _softplus(-lam)
    a = jnp.exp(log_a)
    b = jnp.sqrt(-jnp.tanh(log_a) * (a * a + 1.0)) * (ig * xc)
    return a, b


def _lru_fwd(proj3, conv_w, conv_b, wa, ba, wx, bx, lam, name):
    bsz, s, _ = proj3.shape
    nblk = LRU_WIDTH // LANES

    def body(x_ref, cw_ref, cb_ref, wa_ref, ba_ref, wx_ref, bx_ref, lam_ref, hf_ref, hb_ref):
        xc = _conv_fwd(x_ref[0], cw_ref[...], 2) + cb_ref[...]
        for d, h_ref in ((0, hf_ref), (1, hb_ref)):
            a, b = _lru_gate_fn(xc, wa_ref[d, 0], ba_ref[d:d + 1, :], wx_ref[d, 0], bx_ref[d:d + 1, :], lam_ref[d:d + 1, :])
            h_ref[0] = _scan(a, b, reverse=(d == 1))

    col = pl.BlockSpec((1, s, LANES), lambda b, j: (b, 0, j))
    vec2 = pl.BlockSpec((2, LANES), lambda b, j: (0, j))
    wspec = pl.BlockSpec((2, 1, LANES, LANES), lambda b, j: (0, j, 0, 0))
    return pl.pallas_call(
        body, name=name, grid=(bsz, nblk),
        in_specs=[col, pl.BlockSpec((4, LANES), lambda b, j: (0, j)), pl.BlockSpec((1, LANES), lambda b, j: (0, j)),
                  wspec, vec2, wspec, vec2, vec2],
        out_specs=[col, col],
        out_shape=[jax.ShapeDtypeStruct((bsz, s, LRU_WIDTH), F32)] * 2,
        compiler_params=_cparams(("parallel", "parallel")),
    )(proj3, conv_w, conv_b, wa, ba, wx, bx, lam)


def _lru_bwd(proj3, conv_w, conv_b, wa, ba, wx, bx, lam, hf, hb, dh, name):
    bsz, s, _ = proj3.shape
    nblk = LRU_WIDTH // LANES

    def body(x_ref, cw_ref, cb_ref, wa_ref, ba_ref, wx_ref, bx_ref, lam_ref, hf_ref, hb_ref, dh_ref,
             dx_ref, dcw_ref, dcb_ref, dwa_ref, dba_ref, dwx_ref, dbx_ref, dlam_ref):
        @pl.when(pl.program_id(1) == 0)
        def _():
            for r in (dcw_ref, dcb_ref, dwa_ref, dba_ref, dwx_ref, dbx_ref, dlam_ref):
                r[...] = jnp.zeros_like(r)

        x, cw = x_ref[0], cw_ref[...]
        xc = _conv_fwd(x, cw, 2) + cb_ref[...]
        dhv = dh_ref[0]
        dxc = jnp.zeros_like(xc)
        for d, h_ref in ((0, hf_ref), (1, hb_ref)):
            rev = d == 1
            args = (xc, wa_ref[d, 0].astype(F32), ba_ref[d:d + 1, :], wx_ref[d, 0].astype(F32), bx_ref[d:d + 1, :],
                    lam_ref[d:d + 1, :])
            (a, _), vjp = jax.vjp(_lru_gate_fn, *args)
            h = h_ref[0]
            a_next = _shift_down(a, 1) if rev else _shift_up(a, 1)
            lam_adj = _scan(a_next, dhv, reverse=not rev)
            h_prev = _shift_up(h, 1) if rev else _shift_down(h, 1)
            dxc_d, dwa, dba, dwx, dbx, dlam = vjp((lam_adj * h_prev, lam_adj))
            dxc = dxc + dxc_d
            dwa_ref[d, 0] += dwa
            dwx_ref[d, 0] += dwx
            dba_ref[d:d + 1, :] += dba
            dbx_ref[d:d + 1, :] += dbx
            dlam_ref[d:d + 1, :] += dlam
        dx, dcw = _conv_bwd(x, cw, 2, dxc)
        dx_ref[0] = dx.astype(dx_ref.dtype)
        dcw_ref[...] += dcw
        dcb_ref[...] += jnp.sum(dxc, axis=0, keepdims=True)

    col = pl.BlockSpec((1, s, LANES), lambda j, b: (b, 0, j))
    vec1 = pl.BlockSpec((1, LANES), lambda j, b: (0, j))
    vec2 = pl.BlockSpec((2, LANES), lambda j, b: (0, j))
    vec4 = pl.BlockSpec((4, LANES), lambda j, b: (0, j))
    wspec = pl.BlockSpec((2, 1, LANES, LANES), lambda j, b: (0, j, 0, 0))
    wshape = jax.ShapeDtypeStruct((2, nblk, LANES, LANES), F32)
    v2shape = jax.ShapeDtypeStruct((2, LRU_WIDTH), F32)
    return pl.pallas_call(
        body, name=name, grid=(nblk, bsz),
        in_specs=[col, vec4, vec1, wspec, vec2, wspec, vec2, vec2, col, col, col],
        out_specs=[col, vec4, vec1, wspec, vec2, wspec, vec2, vec2],
        out_shape=[jax.ShapeDtypeStruct((bsz, s, LRU_WIDTH), BF16), jax.ShapeDtypeStruct((4, LRU_WIDTH), F32),
                   jax.ShapeDtypeStruct((1, LRU_WIDTH), F32), wshape, v2shape, wshape, v2shape, v2shape],
        compiler_params=_cparams(("parallel", "arbitrary")),
    )(proj3, conv_w, conv_b, wa, ba, wx, bx, lam, hf, hb, dh)


def _mix_fn(o_f, o_b, z, lg, hf, hb, dn_g, lru_g):
    osum = o_f + o_b
    heads = []
    for h in range(DN_HEADS):
        sl = slice(h * LANES, (h + 1) * LANES)
        heads.append(_rms(osum[:, sl], dn_g) * _silu(z[:, sl]))
    lru = _rms(_gelu(lg) * (hf + hb), lru_g)
    return jnp.concatenate(heads + [lru], axis=-1)


def _mix_specs(tm):
    w = DN_WIDTH
    row = pl.BlockSpec((tm, w), lambda i: (i, 0))
    z = pl.BlockSpec((tm, w), lambda i: (i, P_Z // w))
    lg = pl.BlockSpec((tm, w), lambda i: (i, P_LG // w))
    return [row, row, z, lg, row, row, pl.BlockSpec((1, LANES), lambda i: (0, 0)), pl.BlockSpec((1, w), lambda i: (0, 0))]


def _mix_fwd(o_f, o_b, proj, hf, hb, dn_g, lru_g, name):
    t = proj.shape[0]
    tm = _row_tile(t)

    def body(of_ref, ob_ref, z_ref, lg_ref, hf_ref, hb_ref, dg_ref, lgn_ref, o_ref):
        o_ref[...] = _mix_fn(of_ref[...], ob_ref[...], z_ref[...], lg_ref[...], hf_ref[...], hb_ref[...],
                             dg_ref[...], lgn_ref[...]).astype(o_ref.dtype)

    return pl.pallas_call(
        body, name=name, grid=(t // tm,), in_specs=_mix_specs(tm),
        out_specs=pl.BlockSpec((tm, D_MODEL), lambda i: (i, 0)),
        out_shape=jax.ShapeDtypeStruct((t, D_MODEL), BF16),
        compiler_params=_cparams(("parallel",)),
    )(o_f, o_b, proj, proj, hf, hb, dn_g, lru_g)


def _mix_bwd(o_f, o_b, proj, hf, hb, dn_g, lru_g, dmix, name):
    t = proj.shape[0]
    tm = _row_tile(t)

    def body(of_ref, ob_ref, z_ref, lg_ref, hf_ref, hb_ref, dg_ref, lgn_ref, dm_ref,
             do_ref, dz_ref, dlg_ref, dh_ref, ddg_ref, dlgn_ref):
        _, vjp = jax.vjp(_mix_fn, of_ref[...], ob_ref[...], z_ref[...], lg_ref[...], hf_ref[...], hb_ref[...],
                         dg_ref[...], lgn_ref[...])
        do, _, dz, dlg, dh, _, ddg, dlgn = vjp(dm_ref[...])
        do_ref[...] = do
        dz_ref[...] = dz.astype(dz_ref.dtype)
        dlg_ref[...] = dlg.astype(dlg_ref.dtype)
        dh_ref[...] = dh

        @pl.when(pl.program_id(0) == 0)
        def _():
            ddg_ref[...] = jnp.zeros_like(ddg_ref)
            dlgn_ref[...] = jnp.zeros_like(dlgn_ref)

        ddg_ref[...] += ddg
        dlgn_ref[...] += dlgn

    row = pl.BlockSpec((tm, DN_WIDTH), lambda i: (i, 0))
    return pl.pallas_call(
        body, name=name, grid=(t // tm,),
        in_specs=_mix_specs(tm) + [pl.BlockSpec((tm, D_MODEL), lambda i: (i, 0))],
        out_specs=[row, row, row, row, pl.BlockSpec((1, LANES), lambda i: (0, 0)), pl.BlockSpec((1, DN_WIDTH), lambda i: (0, 0))],
        out_shape=[jax.ShapeDtypeStruct((t, DN_WIDTH), dt) for dt in (F32, BF16, BF16, F32)]
        + [jax.ShapeDtypeStruct((1, LANES), F32), jax.ShapeDtypeStruct((1, DN_WIDTH), F32)],
        compiler_params=_cparams(("arbitrary",)),
    )(o_f, o_b, proj, proj, hf, hb, dn_g, lru_g, dmix)


FFN_PIECES = 8


def _ffn_act_fwd(g3, u3, conv_w, conv_b, name):
    bsz, s, f = g3.shape
    nblk = f // LANES

    def body(g_ref, u_ref, w_ref, b_ref, o_ref):
        gate = _conv_fwd(g_ref[0], w_ref[...], 1) + b_ref[...]
        o_ref[0] = (_gelu(gate) * u_ref[0]).astype(o_ref.dtype)

    col = pl.BlockSpec((1, s, LANES), lambda b, j: (b, 0, j))
    return pl.pallas_call(
        body, name=name, grid=(bsz, nblk),
        in_specs=[col, col, pl.BlockSpec((3, LANES), lambda b, j: (0, j)), pl.BlockSpec((1, LANES), lambda b, j: (0, j))],
        out_specs=col, out_shape=jax.ShapeDtypeStruct((bsz, s, f), BF16),
        compiler_params=_cparams(("parallel", "parallel")),
    )(g3, u3, conv_w, conv_b)


def _ffn_act_bwd(g3, u3, conv_w, conv_b, dact3, name):
    bsz, s, f = g3.shape
    nblk = f // LANES

    def body(g_ref, u_ref, w_ref, b_ref, d_ref, dg_ref, du_ref, dw_ref, db_ref):
        g, w = g_ref[0], w_ref[...]
        gate = _conv_fwd(g, w, 1) + b_ref[...]
        rows = s // FFN_PIECES if s % (FFN_PIECES * 8) == 0 else s
        pieces = []
        for r0 in range(0, s, rows):
            _, vjp = jax.vjp(lambda gt, uu: _gelu(gt) * uu, gate[r0:r0 + rows], u_ref[0, r0:r0 + rows, :])
            dgate_p, du_p = vjp(d_ref[0, r0:r0 + rows, :])
            du_ref[0, r0:r0 + rows, :] = du_p.astype(du_ref.dtype)
            pieces.append(dgate_p)
        dgate = pieces[0] if len(pieces) == 1 else jnp.concatenate(pieces, axis=0)
        dg, dw = _conv_bwd(g, w, 1, dgate)
        dg_ref[0] = dg.astype(dg_ref.dtype)

        @pl.when(pl.program_id(1) == 0)
        def _():
            dw_ref[...] = jnp.zeros_like(dw_ref)
            db_ref[...] = jnp.zeros_like(db_ref)

        dw_ref[...] += dw
        db_ref[...] += jnp.sum(dgate, axis=0, keepdims=True)

    col = pl.BlockSpec((1, s, LANES), lambda j, b: (b, 0, j))
    w3 = pl.BlockSpec((3, LANES), lambda j, b: (0, j))
    w1 = pl.BlockSpec((1, LANES), lambda j, b: (0, j))
    return pl.pallas_call(
        body, name=name, grid=(nblk, bsz),
        in_specs=[col, col, w3, w1, col], out_specs=[col, col, w3, w1],
        out_shape=[jax.ShapeDtypeStruct((bsz, s, f), BF16), jax.ShapeDtypeStruct((bsz, s, f), BF16),
                   jax.ShapeDtypeStruct((3, f), F32), jax.ShapeDtypeStruct((1, f), F32)],
        compiler_params=_cparams(("parallel", "arbitrary")),
    )(g3, u3, conv_w, conv_b, dact3)


def _ple_fn(r, pg, pp, bg):
    return r + _sigmoid(pg + bg) * pp


def _ple_fwd(r, pg, pp, bg, name):
    t, d = r.shape
    tm = _row_tile(t)

    def body(r_ref, pg_ref, pp_ref, bg_ref, o_ref):
        o_ref[...] = _ple_fn(r_ref[...], pg_ref[...], pp_ref[...], bg_ref[...])

    row = pl.BlockSpec((tm, d), lambda i: (i, 0))
    return pl.pallas_call(
        body, name=name, grid=(t // tm,), in_specs=[row, row, row, pl.BlockSpec((1, d), lambda i: (0, 0))],
        out_specs=row, out_shape=jax.ShapeDtypeStruct((t, d), F32),
        compiler_params=_cparams(("parallel",)),
    )(r, pg, pp, bg)


def _ple_bwd(pg, pp, bg, dr, name):
    t, d = pg.shape
    tm = _row_tile(t)

    def body(pg_ref, pp_ref, bg_ref, dr_ref, dpg_ref, dpp_ref, dbg_ref):
        _, vjp = jax.vjp(lambda a, b, c: _sigmoid(a + c) * b, pg_ref[...], pp_ref[...], bg_ref[...])
        dpg, dpp, dbg = vjp(dr_ref[...])
        dpg_ref[...] = dpg.astype(dpg_ref.dtype)
        dpp_ref[...] = dpp.astype(dpp_ref.dtype)

        @pl.when(pl.program_id(0) == 0)
        def _():
            dbg_ref[...] = jnp.zeros_like(dbg_ref)

        dbg_ref[...] += dbg

    row = pl.BlockSpec((tm, d), lambda i: (i, 0))
    vec = pl.BlockSpec((1, d), lambda i: (0, 0))
    return pl.pallas_call(
        body, name=name, grid=(t // tm,), in_specs=[row, row, vec, row], out_specs=[row, row, vec],
        out_shape=[jax.ShapeDtypeStruct((t, d), BF16), jax.ShapeDtypeStruct((t, d), BF16), jax.ShapeDtypeStruct((1, d), F32)],
        compiler_params=_cparams(("arbitrary",)),
    )(pg, pp, bg, dr)


def _loss_head(r, g, target, name):
    t, d = r.shape
    tm = _row_tile(t)

    def loss_fn(x, gg, tgt):
        err = _rms(x, gg) - tgt
        return 0.5 * jnp.sum(jnp.sum(err * err, axis=-1, keepdims=True) * (1.0 / d), axis=0, keepdims=True)

    def body(r_ref, g_ref, t_ref, l_ref, dr_ref, dg_ref):
        val, vjp = jax.vjp(lambda x, gg: loss_fn(x, gg, t_ref[...]), r_ref[...], g_ref[...])
        dx, dg = vjp(jnp.ones((1, 1), F32))
        dr_ref[...] = dx

        @pl.when(pl.program_id(0) == 0)
        def _():
            l_ref[...] = jnp.zeros_like(l_ref)
            dg_ref[...] = jnp.zeros_like(dg_ref)

        l_ref[...] += val
        dg_ref[...] += dg

    row = pl.BlockSpec((tm, d), lambda i: (i, 0))
    vec = pl.BlockSpec((1, d), lambda i: (0, 0))
    one = pl.BlockSpec((1, 1), lambda i: (0, 0))
    return pl.pallas_call(
        body, name=name, grid=(t // tm,), in_specs=[row, vec, row], out_specs=[one, row, vec],
        out_shape=[jax.ShapeDtypeStruct((1, 1), F32), jax.ShapeDtypeStruct((t, d), F32), jax.ShapeDtypeStruct((1, d), F32)],
        compiler_params=_cparams(("arbitrary",)),
    )(r, g, target)


def _adamw_math(w, gg, m, v, d_ref, nm_ref, nv_ref):
    nm = ADAM_B1 * m + (1.0 - ADAM_B1) * gg
    nv = ADAM_B2 * v + (1.0 - ADAM_B2) * (gg * gg)
    m_hat = nm / (1.0 - ADAM_B1 ** ADAM_STEP)
    v_hat = nv / (1.0 - ADAM_B2 ** ADAM_STEP)
    d_ref[...] = -ADAM_LR * (m_hat / (jnp.sqrt(v_hat) + ADAM_EPS) + ADAM_WD * w)
    nm_ref[...] = nm
    nv_ref[...] = nv


def _adamw(w, g, m, v, name):
    def body(w_ref, g_ref, m_ref, v_ref, d_ref, nm_ref, nv_ref):
        _adamw_math(w_ref[...], g_ref[...], m_ref[...], v_ref[...], d_ref, nm_ref, nv_ref)

    whole = pl.BlockSpec(memory_space=pltpu.VMEM)
    return pl.pallas_call(
        body, name=name, in_specs=[whole] * 4, out_specs=[whole] * 3, out_shape=[jax.ShapeDtypeStruct(w.shape, F32)] * 3,
        compiler_params=pltpu.CompilerParams(vmem_limit_bytes=VMEM_LIMIT),
    )(w, g, m, v)


def _prepare_weights(w):
    nl = w["w_in"].shape[0]
    w_in = w["w_in"].astype(BF16)
    pad = jnp.zeros(w_in.shape[:2] + (P_COLS - w_in.shape[2],), BF16)
    split = 4 * DN_WIDTH + N_GATE
    w_in_p = jnp.concatenate([w_in[:, :, split:], w_in[:, :, :split], pad], axis=-1)
    gate_vec = lambda a: jnp.pad(a.reshape(nl, 1, N_GATE // 2), ((0, 0), (0, 0), (N_GATE // 2, LANES - N_GATE)))

    def pair_blocks(a):
        a = a.reshape(nl, 2, 4, 2, 64, 64)
        z = jnp.zeros_like(a[:, :, :, 0])
        top = jnp.concatenate([a[:, :, :, 0], z], axis=-1)
        bot = jnp.concatenate([z, a[:, :, :, 1]], axis=-1)
        return jnp.concatenate([top, bot], axis=-2).astype(BF16)

    bf = lambda a: a.astype(BF16)
    return dict(
        norm1_g=w["norm1_g"][:, None, :], w_in=w_in_p,
        dn_conv_w=w["dn_conv_w"], alog=gate_vec(w["dn_a_log"]), dtb=gate_vec(w["dn_dt_bias"]),
        dn_norm_g=w["dn_norm_g"][:, None, :], lru_conv_w=w["lru_conv_w"], lru_conv_b=w["lru_conv_b"][:, None, :],
        lru_wa=pair_blocks(w["lru_wa"]), lru_ba=w["lru_ba"], lru_wx=pair_blocks(w["lru_wx"]), lru_bx=w["lru_bx"],
        lru_lambda=w["lru_lambda"], lru_norm_g=w["lru_norm_g"][:, None, :],
        w_out=bf(w["w_out"]), norm2_g=w["norm2_g"][:, None, :], ffn_wg=bf(w["ffn_wg"]), ffn_wu=bf(w["ffn_wu"]),
        ffn_conv_w=w["ffn_conv_w"], ffn_conv_b=w["ffn_conv_b"][:, None, :], ffn_wd=bf(w["ffn_wd"]),
        ple_norm_g=w["ple_norm_g"][:, None, :], ple_wg=bf(w["ple_wg"]), ple_bg=w["ple_bg"][:, None, :], ple_wp=bf(w["ple_wp"]),
        final_g=w["final_g"][None, :],
    )


def _unpair_blocks(a):
    top = a[:, :, :64, :64]
    bot = a[:, :, 64:, 64:]
    return jnp.stack([top, bot], axis=2).reshape(2, 8, 64, 64)


def _local_step(x, p, target, w):
    bsz, s, d = x.shape
    t = bsz * s
    nl = w["w_in"].shape[0]
    kw = _prepare_weights(w)
    flat = lambda a: a.reshape(t, a.shape[-1])
    seq = lambda a: a.reshape(bsz, s, a.shape[-1])

    saved = []
    r = flat(x)
    for i in range(nl):
        n = f"l{i}_"
        sv = {"r0": r}
        h, proj = _norm_matmul(r, kw["norm1_g"][i], [kw["w_in"][i]], n + "in_proj")
        gb = _dn_gates_fwd(proj, kw["alog"][i], kw["dtb"][i], n + "dn_gates")
        qkv = _dn_prep_fwd(seq(proj), kw["dn_conv_w"][i], n + "dn_prep")
        in_f = _dn_intra_fwd(qkv, seq(gb), False, n + "dn_intra_fwd")
        in_b = _dn_intra_fwd(qkv, seq(gb), True, n + "dn_intra_rev")
        o_f, st_f = _dn_rec_fwd(in_f, False, n + "dn_rec_fwd")
        o_b, st_b = _dn_rec_fwd(in_b, True, n + "dn_rec_rev")
        lru_args = (seq(proj), kw["lru_conv_w"][i], kw["lru_conv_b"][i], kw["lru_wa"][i], kw["lru_ba"][i],
                    kw["lru_wx"][i], kw["lru_bx"][i], kw["lru_lambda"][i])
        hf, hb = _lru_fwd(*lru_args, n + "lru")
        mix_args = (flat(o_f), flat(o_b), proj, flat(hf), flat(hb), kw["dn_norm_g"][i], kw["lru_norm_g"][i])
        mix = _mix_fwd(*mix_args, n + "mix")
        r1 = _matmul([(mix, kw["w_out"][i])], n + "out_proj", res=r)
        h2, fg, fu = _norm_matmul(r1, kw["norm2_g"][i], [kw["ffn_wg"][i], kw["ffn_wu"][i]], n + "ffn_gu")
        act = _ffn_act_fwd(seq(fg), seq(fu), kw["ffn_conv_w"][i], kw["ffn_conv_b"][i], n + "ffn_act")
        r2 = _matmul([(flat(act), kw["ffn_wd"][i])], n + "ffn_d", res=r1)
        hp, pg = _norm_matmul(r2, kw["ple_norm_g"][i], [kw["ple_wg"][i]], n + "ple_g")
        pi = flat(p[i])
        pp = _matmul([(pi, kw["ple_wp"][i])], n + "ple_p")
        r3 = _ple_fwd(r2, pg, pp, kw["ple_bg"][i], n + "ple")
        sv.update(h=h, proj=proj, gb=gb, qkv=qkv, st_f=st_f, st_b=st_b, in_f=in_f, in_b=in_b, lru_args=lru_args, hf=hf, hb=hb,
                  mix_args=mix_args, mix=mix, r1=r1, h2=h2, fg=fg, fu=fu, act=act, r2=r2, hp=hp, pg=pg, pp=pp, pi=pi)
        saved.append(sv)
        r = r3

    loss, dr, dfinal = _loss_head(r, kw["final_g"], flat(target), "loss_head")
    grads = {k: [None] * nl for k in (
        "norm1_g", "w_in", "dn_conv_w", "dn_a_log", "dn_dt_bias", "dn_norm_g", "lru_conv_w", "lru_conv_b", "lru_wa", "lru_ba",
        "lru_wx", "lru_bx", "lru_lambda", "lru_norm_g", "w_out", "norm2_g", "ffn_wg", "ffn_wu", "ffn_conv_w", "ffn_conv_b",
        "ffn_wd", "ple_norm_g", "ple_wg", "ple_bg", "ple_wp", "dn_a_log_lanes", "dn_dt_bias_lanes")}

    for i in reversed(range(nl)):
        n = f"b{i}_"
        sv = saved[i]
        dpg, dpp, dbg = _ple_bwd(sv["pg"], sv["pp"], kw["ple_bg"][i], dr, n + "ple")
        grads["ple_bg"][i] = dbg[0]
        grads["ple_wp"][i] = _matmul_tn(sv["pi"], dpp, n + "ple_wp")
        grads["ple_wg"][i] = _matmul_tn(sv["hp"], dpg, n + "ple_wg")
        dr2, dr2b, dg = _matmul_norm_bwd([(dpg, kw["ple_wg"][i])], sv["r2"], kw["ple_norm_g"][i], dr, n + "ple_dh")
        grads["ple_norm_g"][i] = dg[0]
        grads["ffn_wd"][i] = _matmul_tn(flat(sv["act"]), dr2b, n + "ffn_wd")
        dact = _matmul([(dr2b, kw["ffn_wd"][i])], n + "ffn_dact", nt=True)
        dfg, dfu, dcw, dcb = _ffn_act_bwd(seq(sv["fg"]), seq(sv["fu"]), kw["ffn_conv_w"][i], kw["ffn_conv_b"][i], seq(dact),
                                          n + "ffn_act")
        grads["ffn_conv_w"][i] = dcw
        grads["ffn_conv_b"][i] = dcb[0]
        grads["ffn_wg"][i] = _matmul_tn(sv["h2"], flat(dfg), n + "ffn_wg")
        grads["ffn_wu"][i] = _matmul_tn(sv["h2"], flat(dfu), n + "ffn_wu")
        dr1, dr1b, dg = _matmul_norm_bwd([(flat(dfg), kw["ffn_wg"][i]), (flat(dfu), kw["ffn_wu"][i])], sv["r1"], kw["norm2_g"][i],
                                         dr2, n + "ffn_dh")
        grads["norm2_g"][i] = dg[0]
        grads["w_out"][i] = _matmul_tn(sv["mix"], dr1b, n + "w_out")
        dmix = _matmul([(dr1b, kw["w_out"][i])], n + "dmix", nt=True)
        do, dz, dlg, dh, ddn_g, dlru_g = _mix_bwd(*sv["mix_args"], dmix, n + "mix")
        grads["dn_norm_g"][i] = ddn_g[0]
        grads["lru_norm_g"][i] = dlru_g[0]
        dlx, dcw, dcb, dwa, dba, dwx, dbx, dlam = _lru_bwd(*sv["lru_args"], sv["hf"], sv["hb"], seq(dh), n + "lru")
        grads["lru_conv_w"][i] = dcw
        grads["lru_conv_b"][i] = dcb[0]
        grads["lru_wa"][i] = _unpair_blocks(dwa)
        grads["lru_wx"][i] = _unpair_blocks(dwx)
        grads["lru_ba"][i], grads["lru_bx"][i], grads["lru_lambda"][i] = dba, dbx, dlam
        ct_f = _dn_rec_bwd(sv["in_f"], sv["st_f"], seq(do), False, n + "dn_rec_fwd")
        ct_b = _dn_rec_bwd(sv["in_b"], sv["st_b"], seq(do), True, n + "dn_rec_rev")
        dqkv_f, dgb_f = _dn_intra_bwd(sv["qkv"], seq(sv["gb"]), ct_f, False, n + "dn_intra_fwd")
        dqkv_b, dgb_b = _dn_intra_bwd(sv["qkv"], seq(sv["gb"]), ct_b, True, n + "dn_intra_rev")
        dgate, dalog, ddtb = _dn_gates_bwd(sv["proj"], kw["alog"][i], kw["dtb"][i], flat(dgb_f), flat(dgb_b), n + "dn_gates")
        grads["dn_a_log"][i] = dalog[0, N_GATE // 2:N_GATE].reshape(2, DN_HEADS)
        grads["dn_dt_bias"][i] = ddtb[0, N_GATE // 2:N_GATE].reshape(2, DN_HEADS)
        grads["dn_a_log_lanes"][i] = dalog[0]
        grads["dn_dt_bias_lanes"][i] = ddtb[0]
        dpqkv, dcw = _dn_prep_bwd(seq(sv["proj"]), kw["dn_conv_w"][i], dqkv_f, dqkv_b, n + "dn_prep")
        grads["dn_conv_w"][i] = dcw
        segs = [flat(dlx), dlg, flat(dpqkv), dz, dgate]
        w_in_i = kw["w_in"][i]
        dq3 = flat(dpqkv)
        wseg = DN_WIDTH
        terms = [(flat(dlx), (w_in_i, wseg, P_LX // wseg)), (dlg, (w_in_i, wseg, P_LG // wseg))]
        terms += [((dq3, wseg, j), (w_in_i, wseg, P_Q // wseg + j)) for j in range(3)]
        terms += [(dz, (w_in_i, wseg, P_Z // wseg)), (dgate, (w_in_i, LANES, P_GATE // LANES))]
        grads["w_in"][i] = _matmul_tn_cols(sv["h"], segs, n + "w_in")
        dr, _, dg = _matmul_norm_bwd(terms, sv["r0"], kw["norm1_g"][i], dr1, n + "in_dh")
        grads["norm1_g"][i] = dg[0]

    grads["final_g"] = dfinal[0]
    return loss[0, 0], dr.reshape(bsz, s, d), grads


MESH = pl.DeviceIdType.MESH
ANY = pl.BlockSpec(memory_space=pl.ANY)
N_CHIPS = 4
HALF = 2


def _place():
    x, y, c = lax.axis_index("x"), lax.axis_index("y"), lax.axis_index("c")
    chips = [(1 - x, y), (x, 1 - y), (1 - x, 1 - y)]
    return x, y, c, chips


def _dma_sems(n):
    return pltpu.SemaphoreType.DMA((n,))


def _gather_weights(shards):
    nt = len(shards)

    def body(*refs):
        w_refs, g_refs = refs[:nt], refs[nt:2 * nt]
        send_sems, recv_sems = refs[2 * nt:]
        x, y, c, chips = _place()
        me = 2 * x + y
        mine = pl.ds(c * HALF, HALF)
        theirs = pl.ds((1 - c) * HALF, HALF)

        def rc(src, dst, k, to):
            return pltpu.make_async_remote_copy(src_ref=src, dst_ref=dst, send_sem=send_sems.at[k], recv_sem=recv_sems.at[k],
                                                device_id=to, device_id_type=MESH)

        first = []
        for t in range(nt):
            for r, (px, py) in enumerate(chips):
                first.append(rc(w_refs[t].at[mine], g_refs[t].at[me, mine], 6 * t + r, (px, py, c)))
        for cp in first:
            cp.start()
        passed = []
        for t in range(nt):
            for r, (px, py) in enumerate(chips):
                peer = 2 * px + py
                rc(w_refs[t].at[mine], g_refs[t].at[peer, mine], 6 * t + r, (px, py, c)).wait_recv()
                fw = rc(g_refs[t].at[peer, mine], g_refs[t].at[peer, mine], 6 * t + 3 + r, (x, y, 1 - c))
                fw.start()
                passed.append(fw)
        for t in range(nt):
            for r, (px, py) in enumerate(chips):
                peer = 2 * px + py
                rc(g_refs[t].at[peer, theirs], g_refs[t].at[peer, theirs], 6 * t + 3 + r, (x, y, 1 - c)).wait_recv()
        for cp in first + passed:
            cp.wait_send()

    return pl.pallas_call(
        body, name="gather_weights", in_specs=[ANY] * nt, out_specs=[ANY] * nt,
        out_shape=[jax.ShapeDtypeStruct((N_CHIPS,) + a.shape, a.dtype) for a in shards],
        scratch_shapes=[_dma_sems(6 * nt), _dma_sems(6 * nt)],
    )(*shards)


def _swap_halves(blocks):
    nt = len(blocks)

    def body(*refs):
        p_refs, l_refs = refs[:nt], refs[nt:2 * nt]
        send_sems, recv_sems = refs[2 * nt:]
        x, y, c, _ = _place()
        theirs = pl.ds((1 - c) * HALF, HALF)
        cps = [pltpu.make_async_remote_copy(src_ref=p_refs[t].at[:, theirs], dst_ref=l_refs[t], send_sem=send_sems.at[t],
                                            recv_sem=recv_sems.at[t], device_id=(x, y, 1 - c), device_id_type=MESH)
               for t in range(nt)]
        for cp in cps:
            cp.start()
        for cp in cps:
            cp.wait_send()
            cp.wait_recv()

    return pl.pallas_call(
        body, name="swap_halves", in_specs=[ANY] * nt, out_specs=[ANY] * nt,
        out_shape=[jax.ShapeDtypeStruct((a.shape[0], HALF) + a.shape[2:], a.dtype) for a in blocks],
        scratch_shapes=[_dma_sems(nt), _dma_sems(nt)],
    )(*blocks)


def _tile_rows(a, row_bytes):
    best = None
    for d in range(16, a + 1, 16):
        if a % d == 0 and d * row_bytes <= 2 * 1024 * 1024:
            best = d
    return best if best is not None else a


def _add_halves(pg, l1, c_arr, name):
    n, _, a, b = pg.shape
    ta = _tile_rows(a, 4 * b)

    def body(c_ref, a_ref, b_ref, o_ref):
        o_ref[...] = (a_ref[...] + b_ref[...]).astype(o_ref.dtype)

    grid_spec = pltpu.PrefetchScalarGridSpec(
        num_scalar_prefetch=1, grid=(n, HALF, a // ta),
        in_specs=[pl.BlockSpec((1, 1, ta, b), lambda k, l, i, c_ref: (k, c_ref[0] * HALF + l, i, 0)),
                  pl.BlockSpec((1, 1, ta, b), lambda k, l, i, c_ref: (k, l, i, 0))],
        out_specs=pl.BlockSpec((1, 1, ta, b), lambda k, l, i, c_ref: (k, l, i, 0)))
    return pl.pallas_call(
        body, name=name, grid_spec=grid_spec, out_shape=jax.ShapeDtypeStruct((n, HALF, a, b), BF16),
        compiler_params=_cparams(("parallel", "parallel", "parallel")),
    )(c_arr, pg, l1)


def _scatter_chips(blocks):
    nt = len(blocks)

    def body(*refs):
        q_refs, l_refs = refs[:nt], refs[nt:2 * nt]
        send_sems, recv_sems = refs[2 * nt:]
        x, y, c, chips = _place()
        me = 2 * x + y

        def rc(t, r, src_chip, dst_chip, to):
            return pltpu.make_async_remote_copy(src_ref=q_refs[t].at[src_chip], dst_ref=l_refs[t].at[dst_chip],
                                                send_sem=send_sems.at[3 * t + r], recv_sem=recv_sems.at[3 * t + r],
                                                device_id=to, device_id_type=MESH)

        sends = [rc(t, r, 2 * px + py, me, (px, py, c)) for t in range(nt) for r, (px, py) in enumerate(chips)]
        for cp in sends:
            cp.start()
        for t in range(nt):
            for r, (px, py) in enumerate(chips):
                rc(t, r, 2 * px + py, 2 * px + py, (px, py, c)).wait_recv()
        for cp in sends:
            cp.wait_send()

    return pl.pallas_call(
        body, name="scatter_chips", in_specs=[ANY] * nt, out_specs=[ANY] * nt,
        out_shape=[jax.ShapeDtypeStruct(a.shape, a.dtype) for a in blocks],
        scratch_shapes=[_dma_sems(3 * nt), _dma_sems(3 * nt)],
    )(*blocks)


def _sum_chips(l2, own, me_arr, name):
    n, _, a, b = l2.shape
    ta = _tile_rows(a, 4 * b)

    def body(me_ref, a_ref, own_ref, o_ref):
        me = me_ref[0]
        acc = jnp.where(me == 0, own_ref[0, 0], a_ref[0, 0]).astype(F32)
        for k in range(1, n):
            acc = acc + jnp.where(me == k, own_ref[0, 0], a_ref[k, 0]).astype(F32)
        o_ref[0] = acc

    grid_spec = pltpu.PrefetchScalarGridSpec(
        num_scalar_prefetch=1, grid=(HALF, a // ta),
        in_specs=[pl.BlockSpec((n, 1, ta, b), lambda l, i, me_ref: (0, l, i, 0)),
                  pl.BlockSpec((1, 1, ta, b), lambda l, i, me_ref: (me_ref[0], l, i, 0))],
        out_specs=pl.BlockSpec((1, ta, b), lambda l, i, me_ref: (l, i, 0)))
    return pl.pallas_call(
        body, name=name, grid_spec=grid_spec, out_shape=jax.ShapeDtypeStruct((HALF, a, b), F32),
        compiler_params=_cparams(("parallel", "parallel")),
    )(me_arr, l2, own)


def _swap_reduced(parts):
    nt = len(parts)

    def body(*refs):
        r_refs, g_refs = refs[:nt], refs[nt:2 * nt]
        send_sems, recv_sems = refs[2 * nt:]
        x, y, c, _ = _place()
        cps = [pltpu.make_async_remote_copy(src_ref=r_refs[t], dst_ref=g_refs[t], send_sem=send_sems.at[t],
                                            recv_sem=recv_sems.at[t], device_id=(x, y, 1 - c), device_id_type=MESH)
               for t in range(nt)]
        for cp in cps:
            cp.start()
        for cp in cps:
            cp.wait_send()
            cp.wait_recv()

    return pl.pallas_call(
        body, name="swap_reduced", in_specs=[ANY] * nt, out_specs=[ANY] * nt,
        out_shape=[jax.ShapeDtypeStruct(a.shape, a.dtype) for a in parts],
        scratch_shapes=[_dma_sems(nt), _dma_sems(nt)],
    )(*parts)


def _adamw_halves(w, mine, theirs, m, v, c_arr, name):
    nl, a, b = w.shape
    ta = _tile_rows(a, 4 * b)

    def body(c_ref, w_ref, g1_ref, g2_ref, m_ref, v_ref, g_ref, d_ref, nm_ref, nv_ref):
        gg = jnp.where(pl.program_id(0) // HALF == c_ref[0], g1_ref[...], g2_ref[...])
        g_ref[...] = gg
        _adamw_math(w_ref[...], gg, m_ref[...], v_ref[...], d_ref, nm_ref, nv_ref)

    full = pl.BlockSpec((1, ta, b), lambda l, i, c_ref: (l, i, 0))
    half = pl.BlockSpec((1, ta, b), lambda l, i, c_ref: (l % HALF, i, 0))
    grid_spec = pltpu.PrefetchScalarGridSpec(num_scalar_prefetch=1, grid=(nl, a // ta),
                                             in_specs=[full, half, half, full, full], out_specs=[full] * 4)
    return pl.pallas_call(
        body, name=name, grid_spec=grid_spec, out_shape=[jax.ShapeDtypeStruct(w.shape, F32)] * 4,
        compiler_params=_cparams(("parallel", "parallel")),
    )(c_arr, w, mine, theirs, m, v)


def _gather_chips(v):
    def body(v_ref, g_ref, send_sems, recv_sems, local_sem):
        x, y, c, chips = _place()
        me = 2 * x + y
        loc = pltpu.make_async_copy(v_ref, g_ref.at[me], local_sem)
        loc.start()
        sends = [pltpu.make_async_remote_copy(src_ref=v_ref, dst_ref=g_ref.at[me], send_sem=send_sems.at[r], recv_sem=recv_sems.at[r],
                                              device_id=(px, py, c), device_id_type=MESH) for r, (px, py) in enumerate(chips)]
        for cp in sends:
            cp.start()
        for r, (px, py) in enumerate(chips):
            pltpu.make_async_remote_copy(src_ref=v_ref, dst_ref=g_ref.at[2 * px + py], send_sem=send_sems.at[r],
                                         recv_sem=recv_sems.at[r], device_id=(px, py, c), device_id_type=MESH).wait_recv()
        for cp in sends:
            cp.wait_send()
        loc.wait()

    return pl.pallas_call(
        body, name="gather_chips", in_specs=[ANY], out_specs=ANY, out_shape=jax.ShapeDtypeStruct((N_CHIPS,) + v.shape, v.dtype),
        scratch_shapes=[_dma_sems(3), _dma_sems(3), pltpu.SemaphoreType.DMA],
    )(v)


BIG = (("w_in", 2), ("w_out", 1), ("ffn_wg", 2), ("ffn_wu", 2), ("ffn_wd", 1), ("ple_wg", 1), ("ple_wp", 2))
SMALL = (("dn_conv_w", 2), ("lru_conv_w", 2), ("lru_ba", 2), ("lru_bx", 2), ("lru_lambda", 2), ("ffn_conv_w", 2))
REPL = ("norm1_g", "dn_a_log", "dn_dt_bias", "dn_norm_g", "lru_conv_b", "lru_wa", "lru_wx", "lru_norm_g", "norm2_g",
        "ffn_conv_b", "ple_norm_g", "ple_bg", "final_g")
WEIGHTS = ("norm1_g", "w_in", "dn_conv_w", "dn_a_log", "dn_dt_bias", "dn_norm_g", "lru_conv_w", "lru_conv_b", "lru_wa", "lru_ba",
           "lru_wx", "lru_bx", "lru_lambda", "lru_norm_g", "w_out", "norm2_g", "ffn_wg", "ffn_wu", "ffn_conv_w", "ffn_conv_b",
           "ffn_wd", "ple_norm_g", "ple_wg", "ple_bg", "ple_wp", "final_g")
GATE_PARAMS = ("dn_a_log", "dn_dt_bias")
ROW_ALIGN = 16


def _rows_for(n_elems):
    rows = -(-n_elems // LANES)
    return -(-rows // ROW_ALIGN) * ROW_ALIGN


def _join_chips(g, own, me, axis):
    return jnp.concatenate([jnp.where(me == k, own, g[k]) for k in range(N_CHIPS)], axis=axis)


def _chip_split(layers, ranges, name):
    nl = len(layers)
    k, n = layers[0].shape
    w = sum(hi - lo for lo, hi in ranges[0])
    tk = _row_tile(k, 128)

    def body(*refs):
        x_refs, o_ref = refs[:nl], refs[nl]
        for l in range(nl):
            for c in range(N_CHIPS):
                off = 0
                for lo, hi in ranges[c]:
                    o_ref[c, l, :, off:off + hi - lo] = x_refs[l][:, lo:hi]
                    off += hi - lo

    return pl.pallas_call(
        body, name=name, grid=(k // tk,), in_specs=[pl.BlockSpec((tk, n), lambda i: (i, 0))] * nl,
        out_specs=pl.BlockSpec((N_CHIPS, nl, tk, w), lambda i: (0, 0, i, 0)),
        out_shape=jax.ShapeDtypeStruct((N_CHIPS, nl, k, w), F32),
        compiler_params=_cparams(("parallel",)),
    )(*layers)


def _even_ranges(n):
    w = n // N_CHIPS
    return [[(c * w, (c + 1) * w)] for c in range(N_CHIPS)]


def _w_in_ranges():
    split = 4 * DN_WIDTH + N_GATE
    total = split + 2 * LRU_WIDTH
    w = total // N_CHIPS
    out = []
    for c in range(N_CHIPS):
        lo, hi = c * w, (c + 1) * w
        parts = []
        if lo < split:
            parts.append((P_Q + lo, P_Q + min(hi, split)))
        if hi > split:
            parts.append((max(lo, split) - split, hi - split))
        out.append(parts)
    return out


def _chip_blocks(layers, axis):
    nl = len(layers)
    size = layers[0].shape[axis - 1] // N_CHIPS
    cut = (lambda g, k: g[:, k * size:(k + 1) * size]) if axis == 2 else (lambda g, k: g[k * size:(k + 1) * size])
    return jnp.stack([jnp.stack([cut(layers[l], k) for l in range(nl)]) for k in range(N_CHIPS)])


def _pack_small(w):
    nl = w[SMALL[0][0]].shape[0]
    flat = jnp.concatenate([w[n].reshape(nl, -1) for n, _ in SMALL], axis=1)
    rows = _rows_for(flat.shape[1])
    return jnp.pad(flat, ((0, 0), (0, rows * LANES - flat.shape[1]))).reshape(nl, rows, LANES)


def _unpack_small_gathered(g, own, me, w):
    nl = g.shape[1]
    flat = jnp.stack([jnp.where(me == k, own, g[k]) for k in range(N_CHIPS)]).reshape(N_CHIPS, nl, -1)
    out, off = {}, 0
    for n, _ in SMALL:
        _, a, b = w[n].shape
        piece = flat[:, :, off:off + a * b].reshape(N_CHIPS, nl, a, b)
        off += a * b
        out[n] = jnp.transpose(piece, (1, 2, 0, 3)).reshape(nl, a, N_CHIPS * b)
    return out


def _pack_small_grads(grads, w):
    nl = w["w_in"].shape[0]
    cols = []
    for n, _ in SMALL:
        _, a, b = w[n].shape
        gfull = jnp.stack(grads[n])
        cols.append(jnp.transpose(gfull.reshape(nl, a, N_CHIPS, b), (2, 0, 1, 3)).reshape(N_CHIPS, nl, a * b))
    small = jnp.concatenate(cols, axis=2)
    rs = _rows_for(small.shape[2])
    small = jnp.pad(small, ((0, 0), (0, 0), (0, rs * LANES - small.shape[2])))
    src = lambda n: n + "_lanes" if n in GATE_PARAMS else n
    rep = jnp.concatenate([(grads[n] if n == "final_g" else jnp.stack(grads[src(n)])).reshape(-1) for n in REPL])
    rr = _rows_for(-(-rep.shape[0] // (N_CHIPS * nl)))
    rep = jnp.pad(rep, (0, N_CHIPS * nl * rr * LANES - rep.shape[0])).reshape(N_CHIPS, nl, rr * LANES)
    return jnp.concatenate([small, rep], axis=2).reshape(N_CHIPS, nl, rs + rr, LANES), (rs, rr)


def _unpack_small_reduced(g, rep_all, rows, w):
    rs, _ = rows
    nl = g.shape[0]
    out = {}
    flat = g[:, :rs].reshape(nl, -1)
    off = 0
    for n, _ in SMALL:
        _, a, b = w[n].shape
        out[n] = flat[:, off:off + a * b].reshape(nl, a, b)
        off += a * b
    flat = rep_all.reshape(-1)
    off = 0
    for n in REPL:
        if n in GATE_PARAMS:
            size = nl * LANES
            out[n] = flat[off:off + size].reshape(nl, LANES)[:, N_GATE // 2:N_GATE].reshape(w[n].shape)
        else:
            size = math.prod(w[n].shape)
            out[n] = flat[off:off + size].reshape(w[n].shape)
        off += size
    return out


def kernel(x, p, norm1_g, w_in, dn_conv_w, dn_a_log, dn_dt_bias, dn_norm_g, lru_conv_w, lru_conv_b, lru_wa, lru_ba, lru_wx, lru_bx, lru_lambda, lru_norm_g, w_out, norm2_g, ffn_wg, ffn_wu, ffn_conv_w, ffn_conv_b, ffn_wd, ple_norm_g, ple_wg, ple_bg, ple_wp, final_g, loss_target, m_norm1_g, m_w_in, m_dn_conv_w, m_dn_a_log, m_dn_dt_bias, m_dn_norm_g, m_lru_conv_w, m_lru_conv_b, m_lru_wa, m_lru_ba, m_lru_wx, m_lru_bx, m_lru_lambda, m_lru_norm_g, m_w_out, m_norm2_g, m_ffn_wg, m_ffn_wu, m_ffn_conv_w, m_ffn_conv_b, m_ffn_wd, m_ple_norm_g, m_ple_wg, m_ple_bg, m_ple_wp, m_final_g, v_norm1_g, v_w_in, v_dn_conv_w, v_dn_a_log, v_dn_dt_bias, v_dn_norm_g, v_lru_conv_w, v_lru_conv_b, v_lru_wa, v_lru_ba, v_lru_wx, v_lru_bx, v_lru_lambda, v_lru_norm_g, v_w_out, v_norm2_g, v_ffn_wg, v_ffn_wu, v_ffn_conv_w, v_ffn_conv_b, v_ffn_wd, v_ple_norm_g, v_ple_wg, v_ple_bg, v_ple_wp, v_final_g):
    w = dict(norm1_g=norm1_g, w_in=w_in, dn_conv_w=dn_conv_w, dn_a_log=dn_a_log, dn_dt_bias=dn_dt_bias, dn_norm_g=dn_norm_g,
             lru_conv_w=lru_conv_w, lru_conv_b=lru_conv_b, lru_wa=lru_wa, lru_ba=lru_ba, lru_wx=lru_wx, lru_bx=lru_bx,
             lru_lambda=lru_lambda, lru_norm_g=lru_norm_g, w_out=w_out, norm2_g=norm2_g, ffn_wg=ffn_wg, ffn_wu=ffn_wu,
             ffn_conv_w=ffn_conv_w, ffn_conv_b=ffn_conv_b, ffn_wd=ffn_wd, ple_norm_g=ple_norm_g, ple_wg=ple_wg, ple_bg=ple_bg,
             ple_wp=ple_wp, final_g=final_g)
    m = dict(norm1_g=m_norm1_g, w_in=m_w_in, dn_conv_w=m_dn_conv_w, dn_a_log=m_dn_a_log, dn_dt_bias=m_dn_dt_bias,
             dn_norm_g=m_dn_norm_g, lru_conv_w=m_lru_conv_w, lru_conv_b=m_lru_conv_b, lru_wa=m_lru_wa, lru_ba=m_lru_ba,
             lru_wx=m_lru_wx, lru_bx=m_lru_bx, lru_lambda=m_lru_lambda, lru_norm_g=m_lru_norm_g, w_out=m_w_out, norm2_g=m_norm2_g,
             ffn_wg=m_ffn_wg, ffn_wu=m_ffn_wu, ffn_conv_w=m_ffn_conv_w, ffn_conv_b=m_ffn_conv_b, ffn_wd=m_ffn_wd,
             ple_norm_g=m_ple_norm_g, ple_wg=m_ple_wg, ple_bg=m_ple_bg, ple_wp=m_ple_wp, final_g=m_final_g)
    v = dict(norm1_g=v_norm1_g, w_in=v_w_in, dn_conv_w=v_dn_conv_w, dn_a_log=v_dn_a_log, dn_dt_bias=v_dn_dt_bias,
             dn_norm_g=v_dn_norm_g, lru_conv_w=v_lru_conv_w, lru_conv_b=v_lru_conv_b, lru_wa=v_lru_wa, lru_ba=v_lru_ba,
             lru_wx=v_lru_wx, lru_bx=v_lru_bx, lru_lambda=v_lru_lambda, lru_norm_g=v_lru_norm_g, w_out=v_w_out, norm2_g=v_norm2_g,
             ffn_wg=v_ffn_wg, ffn_wu=v_ffn_wu, ffn_conv_w=v_ffn_conv_w, ffn_conv_b=v_ffn_conv_b, ffn_wd=v_ffn_wd,
             ple_norm_g=v_ple_norm_g, ple_wg=v_ple_wg, ple_bg=v_ple_bg, ple_wp=v_ple_wp, final_g=v_final_g)

    me = 2 * lax.axis_index("x") + lax.axis_index("y")
    own = [w[n].astype(BF16) for n, _ in BIG] + [_pack_small(w)]
    gathered = _gather_weights(own)
    full = {n: w[n] for n in REPL}
    for (n, axis), g, o in zip(BIG, gathered, own):
        full[n] = _join_chips(g, o, me, axis)
    full.update(_unpack_small_gathered(gathered[-1], own[-1], me, w))

    loss_local, grad_x, grads = _local_step(x, p, loss_target, full)
    loss = lax.psum(loss_local, ("x", "y", "c"))

    small_pack, rows = _pack_small_grads(grads, w)
    blocks = []
    for n, axis in BIG:
        if axis == 2:
            ranges = _w_in_ranges() if n == "w_in" else _even_ranges(grads[n][0].shape[1])
            blocks.append(_chip_split(grads[n], ranges, "chip_split_" + n))
        else:
            blocks.append(_chip_blocks(grads[n], axis))
    blocks.append(small_pack)
    names = [n for n, _ in BIG] + ["small"]
    c_idx = lax.axis_index("c")
    c_arr = c_idx.astype(jnp.int32).reshape(1)
    me_arr = me.astype(jnp.int32).reshape(1)
    from_sibling = _swap_halves(blocks)
    halves = [_add_halves(b4, l1, c_arr, "add_halves_" + n) for n, b4, l1 in zip(names, blocks, from_sibling)]
    arrived = _scatter_chips(halves)
    mine = [_sum_chips(l2, q, me_arr, "sum_chips_" + n) for n, l2, q in zip(names, arrived, halves)]
    theirs = _swap_reduced(mine)

    g, deltas, new_m, new_v = {}, {}, {}, {}
    for i, (n, _) in enumerate(BIG):
        g[n], deltas[n], new_m[n], new_v[n] = _adamw_halves(w[n], mine[i], theirs[i], m[n], v[n], c_arr, "adamw_" + n)
    small = jnp.where(c_idx == 0, jnp.concatenate([mine[-1], theirs[-1]]), jnp.concatenate([theirs[-1], mine[-1]]))
    rep_all = _gather_chips(small[:, rows[0]:])
    gs = _unpack_small_reduced(small, rep_all, rows, w)
    for n in WEIGHTS:
        if n in g:
            continue
        g[n] = gs[n].reshape(w[n].shape)
        shape = (1,) + w[n].shape if w[n].ndim == 1 else w[n].shape
        d2, m2, v2 = _adamw(w[n].reshape(shape), g[n].reshape(shape), m[n].reshape(shape), v[n].reshape(shape), "adamw_" + n)
        deltas[n], new_m[n], new_v[n] = d2.reshape(w[n].shape), m2.reshape(w[n].shape), v2.reshape(w[n].shape)
    return (loss, grad_x, *[g[n] for n in WEIGHTS], *[deltas[n] for n in WEIGHTS], *[new_m[n] for n in WEIGHTS],
            *[new_v[n] for n in WEIGHTS])
```
